```python
import jax, jax.numpy as jnp
from jax import lax
import numpy as np

D_MODEL = 1024
BATCH = 8
SEQ = 4096
DEPTH = 4

N_META = 16
BLOCK = 128
N_PAD = (-N_META) % BLOCK

SB_HEADS = 8
SB_HEAD_DIM = 64
MLA_HEADS = 8
MLA_NOPE = 64
MLA_ROPE = 32
MLA_V = 64
MLA_Q_LORA = 384
MLA_KV_LORA = 256
RET_HEADS = 4
RET_QK = 64
RET_V = 128

D_SB = SB_HEADS * SB_HEAD_DIM
D_MLA = MLA_HEADS * MLA_V
D_RET = RET_HEADS * RET_V
D_MIX = D_SB + D_MLA + D_RET
D_FF = 4 * D_MODEL

IN_SIZES = (D_SB, D_SB, D_SB,
            MLA_Q_LORA, MLA_KV_LORA, MLA_ROPE,
            RET_HEADS * RET_QK, RET_HEADS * RET_QK, D_RET, D_RET)
IN_SPLITS = tuple(int(s) for s in np.cumsum(IN_SIZES)[:-1])
N_IN = int(sum(IN_SIZES))

ROPE_THETA = 10000.0
LN_EPS = 1e-5
DN_ALPHA = (2 * DEPTH) ** 0.25
DN_BETA = (8 * DEPTH) ** -0.25
RET_GAMMA = tuple(1.0 - 2.0 ** (-5 - h) for h in range(RET_HEADS))

kernel_name = "hybrid_sb_mla_retention_deepnorm"


def layer_norm(x, g, b):
    x32 = x.astype(jnp.float32)
    mu = jnp.mean(x32, -1, keepdims=True)
    var = jnp.mean(jnp.square(x32 - mu), -1, keepdims=True)
    y = (x32 - mu) * lax.rsqrt(var + LN_EPS)
    return (y * g.astype(jnp.float32) + b.astype(jnp.float32)).astype(x.dtype)


def rms_norm(x, g):
    x32 = x.astype(jnp.float32)
    y = x32 * lax.rsqrt(jnp.mean(jnp.square(x32), -1, keepdims=True) + LN_EPS)
    return (y * g.astype(jnp.float32)).astype(x.dtype)


def head_norm(y):
    y32 = y.astype(jnp.float32)
    mu = jnp.mean(y32, -1, keepdims=True)
    var = jnp.mean(jnp.square(y32 - mu), -1, keepdims=True)
    return (y32 - mu) * lax.rsqrt(var + LN_EPS)


def apply_rope(x, pos):
    half = x.shape[-1] // 2
    inv = ROPE_THETA ** (-jnp.arange(half, dtype=jnp.float32) / half)
    ang = pos[:, None] * inv[None, :]
    cos = jnp.cos(ang)[:, None, :]
    sin = jnp.sin(ang)[:, None, :]
    x1, x2 = x[..., :half], x[..., half:]
    return jnp.concatenate([x1 * cos - x2 * sin, x1 * sin + x2 * cos], -1).astype(x.dtype)


def stick_breaking_attention(q, k, v, valid):
    L, d = q.shape[1], q.shape[-1]
    scale = d ** -0.5
    outs = []
    for i in range(L // BLOCK):
        q0, q1 = i * BLOCK, (i + 1) * BLOCK
        z = jnp.einsum("bqhd,bkhd->bhqk", q[:, q0:q1], k[:, :q1]).astype(jnp.float32) * scale
        t_idx = jnp.arange(q0, q1)[:, None]
        s_idx = jnp.arange(q1)[None, :]
        mask = (s_idx < t_idx) & valid[None, :q1]
        log_beta = jnp.where(mask, jax.nn.log_sigmoid(z), -jnp.inf)
        log_keep = jnp.where(mask, jax.nn.log_sigmoid(-z), 0.0)
        incl = lax.cumsum(log_keep, axis=3, reverse=True)
        excl = jnp.concatenate([incl[..., 1:], jnp.zeros_like(incl[..., :1])], axis=-1)
        w = jnp.exp(log_beta + excl)
        outs.append(jnp.einsum("bhqk,bkhd->bqhd", w.astype(v.dtype), v[:, :q1]))
    return jnp.concatenate(outs, axis=1)


def mla_attention(q_nope, q_rope, k_nope, k_rope, v, valid):
    L = q_nope.shape[1]
    scale = (MLA_NOPE + MLA_ROPE) ** -0.5
    outs = []
    for i in range(L // BLOCK):
        q0, q1 = i * BLOCK, (i + 1) * BLOCK
        s = (jnp.einsum("bqhd,bkhd->bhqk", q_nope[:, q0:q1], k_nope[:, :q1])
             + jnp.einsum("bqhd,bkd->bhqk", q_rope[:, q0:q1], k_rope[:, :q1])).astype(jnp.float32) * scale
        t_idx = jnp.arange(q0, q1)[:, None]
        s_idx = jnp.arange(q1)[None, :]
        mask = (s_idx <= t_idx) & (valid[None, :q1] | (s_idx == t_idx))
        p = jax.nn.softmax(jnp.where(mask, s, -jnp.inf), axis=-1)
        outs.append(jnp.einsum("bhqk,bkhd->bqhd", p.astype(v.dtype), v[:, :q1]))
    return jnp.concatenate(outs, axis=1)


def multiscale_retention(q, k, v):
    B, L, H, dk = q.shape
    dv = v.shape[-1]
    n = L // BLOCK
    log_g = jnp.log(jnp.array(RET_GAMMA, jnp.float32))
    idx = jnp.arange(BLOCK, dtype=jnp.float32)
    diff = idx[:, None] - idx[None, :]
    d_in = jnp.where(diff[None] >= 0, jnp.exp(jnp.maximum(diff, 0.0)[None] * log_g[:, None, None]), 0.0)
    q_decay = jnp.exp((idx[:, None] + 1.0) * log_g[None, :])
    k_decay = jnp.exp((BLOCK - 1.0 - idx[:, None]) * log_g[None, :])
    c_decay = jnp.exp(BLOCK * log_g)

    def to_chunks(a):
        return jnp.moveaxis(a.astype(jnp.float32).reshape(B, n, BLOCK, H, a.shape[-1]), 1, 0)

    def step(state, inp):
        qc, kc, vc = inp
        inner = jnp.einsum("bqhd,bkhd->bhqk", qc, kc) * d_in[None]
        y = (jnp.einsum("bhqk,bkhe->bqhe", inner, vc)
             + jnp.einsum("bqhd,bhde->bqhe", qc, state) * q_decay[None, :, :, None])
        state = (state * c_decay[None, :, None, None]
                 + jnp.einsum("bkhd,bkhe->bhde", kc * k_decay[None, :, :, None], vc))
        return state, y

    state0 = jnp.zeros((B, H, dk, dv), jnp.float32)
    _, ys = lax.scan(step, state0, (to_chunks(q), to_chunks(k), to_chunks(v)))
    return jnp.moveaxis(ys, 0, 1).reshape(B, L, H, dv)


def hybrid_mixer(h, w_in, q_norm_g, kv_norm_g, w_uq, w_ukv, w_out, pos, valid):
    B, L, _ = h.shape
    proj = h @ w_in
    sb_q, sb_k, sb_v, c_q, c_kv, k_r, r_q, r_k, r_v, r_g = jnp.split(proj, IN_SPLITS, axis=-1)

    hs = lambda a, nh: a.reshape(B, L, nh, -1)
    out_a = stick_breaking_attention(hs(sb_q, SB_HEADS), hs(sb_k, SB_HEADS), hs(sb_v, SB_HEADS), valid)

    q = (rms_norm(c_q, q_norm_g) @ w_uq).reshape(B, L, MLA_HEADS, MLA_NOPE + MLA_ROPE)
    q_nope, q_rope = q[..., :MLA_NOPE], apply_rope(q[..., MLA_NOPE:], pos)
    kv = (rms_norm(c_kv, kv_norm_g) @ w_ukv).reshape(B, L, MLA_HEADS, MLA_NOPE + MLA_V)
    k_nope, v_b = kv[..., :MLA_NOPE], kv[..., MLA_NOPE:]
    k_rope = apply_rope(k_r[:, :, None, :], pos)[:, :, 0]
    out_b = mla_attention(q_nope, q_rope, k_nope, k_rope, v_b, valid)

    rq = apply_rope(hs(r_q, RET_HEADS), pos)
    rk = apply_rope(hs(r_k, RET_HEADS), pos) * (RET_QK ** -0.5)
    rk = jnp.where(valid[None, :, None, None], rk, jnp.zeros_like(rk))
    y_c = head_norm(multiscale_retention(rq, rk, hs(r_v, RET_HEADS))).reshape(B, L, D_RET)
    out_c = (jax.nn.silu(r_g.astype(jnp.float32)) * y_c).astype(h.dtype)

    mixed = jnp.concatenate([out_a.reshape(B, L, D_SB), out_b.reshape(B, L, D_MLA), out_c], axis=-1)
    return mixed @ w_out


def squared_relu_mlp(h, w1, w2):
    return jnp.square(jax.nn.relu(h @ w1)) @ w2


def _fwd_setup_inputs(seed: int = 0) -> dict:
    key = jax.random.key(seed)
    ks = jax.random.split(key, 16)
    f32 = jnp.float32

    def nrm(k, shape, std):
        return jax.random.normal(k, shape, f32) * std

    return {
        "x": nrm(ks[0], (BATCH, SEQ, D_MODEL), 1.0),
        "meta_tokens": nrm(ks[1], (N_META, D_MODEL), 1.0),
        "ln_emb_g": 1.0 + nrm(ks[2], (D_MODEL,), 0.02),
        "ln_emb_b": nrm(ks[3], (D_MODEL,), 0.02),
        "w_in": nrm(ks[4], (DEPTH, D_MODEL, N_IN), D_MODEL ** -0.5),
        "mla_q_norm": 1.0 + nrm(ks[5], (DEPTH, MLA_Q_LORA), 0.02),
        "mla_kv_norm": 1.0 + nrm(ks[6], (DEPTH, MLA_KV_LORA), 0.02),
        "w_uq": nrm(ks[7], (DEPTH, MLA_Q_LORA, MLA_HEADS * (MLA_NOPE + MLA_ROPE)), MLA_Q_LORA ** -0.5),
        "w_ukv": nrm(ks[8], (DEPTH, MLA_KV_LORA, MLA_HEADS * (MLA_NOPE + MLA_V)), MLA_KV_LORA ** -0.5),
        "w_out": nrm(ks[9], (DEPTH, D_MIX, D_MODEL), DN_BETA * D_MIX ** -0.5),
        "ln1_g": 1.0 + nrm(ks[10], (DEPTH, D_MODEL), 0.02),
        "ln1_b": nrm(ks[11], (DEPTH, D_MODEL), 0.02),
        "w_ff1": nrm(ks[12], (DEPTH, D_MODEL, D_FF), D_MODEL ** -0.5),
        "w_ff2": nrm(ks[13], (DEPTH, D_FF, D_MODEL), DN_BETA * D_FF ** -0.5),
        "ln2_g": 1.0 + nrm(ks[14], (DEPTH, D_MODEL), 0.02),
        "ln2_b": nrm(ks[15], (DEPTH, D_MODEL), 0.02),
    }


def _fwd_reference(x, meta_tokens, ln_emb_g, ln_emb_b, w_in, mla_q_norm, mla_kv_norm, w_uq, w_ukv, w_out,
              ln1_g, ln1_b, w_ff1, w_ff2, ln2_g, ln2_b):
    B, S, _ = x.shape
    meta = jnp.broadcast_to(meta_tokens[None].astype(x.dtype), (B, N_META, D_MODEL))
    pad = jnp.zeros((B, N_PAD, D_MODEL), x.dtype)
    h = jnp.concatenate([pad, meta, x], axis=1)
    L = h.shape[1]
    pos_i = jnp.arange(L) - N_PAD
    valid = pos_i >= 0
    pos = pos_i.astype(jnp.float32)
    h = layer_norm(h, ln_emb_g, ln_emb_b)
    for l in range(DEPTH):
        mix = hybrid_mixer(h, w_in[l], mla_q_norm[l], mla_kv_norm[l], w_uq[l], w_ukv[l], w_out[l], pos, valid)
        h = layer_norm(DN_ALPHA * h + mix, ln1_g[l], ln1_b[l])
        h = layer_norm(DN_ALPHA * h + squared_relu_mlp(h, w_ff1[l], w_ff2[l]), ln2_g[l], ln2_b[l])
    return h[:, N_PAD + N_META:]


import jax as _jax
import jax.numpy as _jnp

TWIN_FORMAT = 'train_step'
FWD_PARAMS = ['x', 'meta_tokens', 'ln_emb_g', 'ln_emb_b', 'w_in', 'mla_q_norm', 'mla_kv_norm', 'w_uq', 'w_ukv', 'w_out', 'ln1_g', 'ln1_b', 'w_ff1', 'w_ff2', 'ln2_g', 'ln2_b']
TWIN_WEIGHTS = ['meta_tokens', 'ln_emb_g', 'ln_emb_b', 'w_in', 'mla_q_norm', 'mla_kv_norm', 'w_uq', 'w_ukv', 'w_out', 'ln1_g', 'ln1_b', 'w_ff1', 'w_ff2', 'ln2_g', 'ln2_b']
TWIN_DIFF_INPUT = 'x'
TWIN_INPUTS = ['x', 'meta_tokens', 'ln_emb_g', 'ln_emb_b', 'w_in', 'mla_q_norm', 'mla_kv_norm', 'w_uq', 'w_ukv', 'w_out', 'ln1_g', 'ln1_b', 'w_ff1', 'w_ff2', 'ln2_g', 'ln2_b', 'loss_target', 'm_meta_tokens', 'm_ln_emb_g', 'm_ln_emb_b', 'm_w_in', 'm_mla_q_norm', 'm_mla_kv_norm', 'm_w_uq', 'm_w_ukv', 'm_w_out', 'm_ln1_g', 'm_ln1_b', 'm_w_ff1', 'm_w_ff2', 'm_ln2_g', 'm_ln2_b', 'v_meta_tokens', 'v_ln_emb_g', 'v_ln_emb_b', 'v_w_in', 'v_mla_q_norm', 'v_mla_kv_norm', 'v_w_uq', 'v_w_ukv', 'v_w_out', 'v_ln1_g', 'v_ln1_b', 'v_w_ff1', 'v_w_ff2', 'v_ln2_g', 'v_ln2_b']
TWIN_OUTPUTS = ['loss', 'grad_x', 'grad_meta_tokens', 'grad_ln_emb_g', 'grad_ln_emb_b', 'grad_w_in', 'grad_mla_q_norm', 'grad_mla_kv_norm', 'grad_w_uq', 'grad_w_ukv', 'grad_w_out', 'grad_ln1_g', 'grad_ln1_b', 'grad_w_ff1', 'grad_w_ff2', 'grad_ln2_g', 'grad_ln2_b', 'delta_meta_tokens', 'delta_ln_emb_g', 'delta_ln_emb_b', 'delta_w_in', 'delta_mla_q_norm', 'delta_mla_kv_norm', 'delta_w_uq', 'delta_w_ukv', 'delta_w_out', 'delta_ln1_g', 'delta_ln1_b', 'delta_w_ff1', 'delta_w_ff2', 'delta_ln2_g', 'delta_ln2_b', 'new_m_meta_tokens', 'new_m_ln_emb_g', 'new_m_ln_emb_b', 'new_m_w_in', 'new_m_mla_q_norm', 'new_m_mla_kv_norm', 'new_m_w_uq', 'new_m_w_ukv', 'new_m_w_out', 'new_m_ln1_g', 'new_m_ln1_b', 'new_m_w_ff1', 'new_m_w_ff2', 'new_m_ln2_g', 'new_m_ln2_b', 'new_v_meta_tokens', 'new_v_ln_emb_g', 'new_v_ln_emb_b', 'new_v_w_in', 'new_v_mla_q_norm', 'new_v_mla_kv_norm', 'new_v_w_uq', 'new_v_w_ukv', 'new_v_w_out', 'new_v_ln1_g', 'new_v_ln1_b', 'new_v_w_ff1', 'new_v_w_ff2', 'new_v_ln2_g', 'new_v_ln2_b']
TWIN_LEAF_KINDS = {'loss': 'loss', 'grad_x': 'grad_x', 'grad_meta_tokens': 'grad_w', 'grad_ln_emb_g': 'grad_w', 'grad_ln_emb_b': 'grad_w', 'grad_w_in': 'grad_w', 'grad_mla_q_norm': 'grad_w', 'grad_mla_kv_norm': 'grad_w', 'grad_w_uq': 'grad_w', 'grad_w_ukv': 'grad_w', 'grad_w_out': 'grad_w', 'grad_ln1_g': 'grad_w', 'grad_ln1_b': 'grad_w', 'grad_w_ff1': 'grad_w', 'grad_w_ff2': 'grad_w', 'grad_ln2_g': 'grad_w', 'grad_ln2_b': 'grad_w', 'delta_meta_tokens': 'delta_w', 'delta_ln_emb_g': 'delta_w', 'delta_ln_emb_b': 'delta_w', 'delta_w_in': 'delta_w', 'delta_mla_q_norm': 'delta_w', 'delta_mla_kv_norm': 'delta_w', 'delta_w_uq': 'delta_w', 'delta_w_ukv': 'delta_w', 'delta_w_out': 'delta_w', 'delta_ln1_g': 'delta_w', 'delta_ln1_b': 'delta_w', 'delta_w_ff1': 'delta_w', 'delta_w_ff2': 'delta_w', 'delta_ln2_g': 'delta_w', 'delta_ln2_b': 'delta_w', 'new_m_meta_tokens': 'new_m', 'new_m_ln_emb_g': 'new_m', 'new_m_ln_emb_b': 'new_m', 'new_m_w_in': 'new_m', 'new_m_mla_q_norm': 'new_m', 'new_m_mla_kv_norm': 'new_m', 'new_m_w_uq': 'new_m', 'new_m_w_ukv': 'new_m', 'new_m_w_out': 'new_m', 'new_m_ln1_g': 'new_m', 'new_m_ln1_b': 'new_m', 'new_m_w_ff1': 'new_m', 'new_m_w_ff2': 'new_m', 'new_m_ln2_g': 'new_m', 'new_m_ln2_b': 'new_m', 'new_v_meta_tokens': 'new_v', 'new_v_ln_emb_g': 'new_v', 'new_v_ln_emb_b': 'new_v', 'new_v_w_in': 'new_v', 'new_v_mla_q_norm': 'new_v', 'new_v_mla_kv_norm': 'new_v', 'new_v_w_uq': 'new_v', 'new_v_w_ukv': 'new_v', 'new_v_w_out': 'new_v', 'new_v_ln1_g': 'new_v', 'new_v_ln1_b': 'new_v', 'new_v_w_ff1': 'new_v', 'new_v_w_ff2': 'new_v', 'new_v_ln2_g': 'new_v', 'new_v_ln2_b': 'new_v'}


def _forward(args):
    return _fwd_reference(*[args[k] for k in FWD_PARAMS])


def _output_shape():
    out = _jax.eval_shape(lambda: _forward(_fwd_setup_inputs(0)))
    return out.shape, out.dtype

N_MICROBATCH = 1
ADAM_LR = 0.001
ADAM_B1 = 0.9
ADAM_B2 = 0.999
ADAM_EPS = 1e-08
ADAM_WD = 0.01
ADAM_STEP = 10
PER_EXAMPLE_BATCH_AXIS = {'x': 0, 'loss_target': 0}
SHARED_INPUTS = []
_WEIGHT_DTYPES = {'meta_tokens': _jnp.float32, 'ln_emb_g': _jnp.float32, 'ln_emb_b': _jnp.float32, 'w_in': _jnp.float32, 'mla_q_norm': _jnp.float32, 'mla_kv_norm': _jnp.float32, 'w_uq': _jnp.float32, 'w_ukv': _jnp.float32, 'w_out': _jnp.float32, 'ln1_g': _jnp.float32, 'ln1_b': _jnp.float32, 'w_ff1': _jnp.float32, 'w_ff2': _jnp.float32, 'ln2_g': _jnp.float32, 'ln2_b': _jnp.float32}
MOMENT_SCALE = {'meta_tokens': 3.943832e-03, 'ln_emb_g': 5.965606e-01, 'ln_emb_b': 4.816215e-01, 'w_in': 2.075241e-02, 'mla_q_norm': 6.998134e-03, 'mla_kv_norm': 1.581542e-02, 'w_uq': 4.941564e-03, 'w_ukv': 7.989299e-03, 'w_out': 6.220676e-02, 'ln1_g': 7.842432e-01, 'ln1_b': 5.047638e-01, 'w_ff1': 3.163500e-02, 'w_ff2': 1.734944e-01, 'ln2_g': 1.612731e+01, 'ln2_b': 3.780086e+00}


def _to_microbatches(a, axis):
    t = _jnp.moveaxis(a, axis, 0)
    t = t.reshape((N_MICROBATCH, t.shape[0] // N_MICROBATCH) + t.shape[1:])
    return _jnp.moveaxis(t, 1, axis + 1)


def setup_inputs(seed: int = 0) -> dict:
    inp = _fwd_setup_inputs(seed)
    key = _jax.random.fold_in(_jax.random.key(seed), 7919)
    shape, _ = _output_shape()
    out = dict(inp)
    out["loss_target"] = _jax.random.normal(_jax.random.fold_in(key, 0), shape, _jnp.float32)
    for i, name in enumerate(TWIN_WEIGHTS):
        w = inp[name].astype(_jnp.float32)
        if MOMENT_SCALE is None:
            s = _jnp.sqrt(_jnp.mean(_jnp.square(w)) + 1e-30)
        else:
            s = MOMENT_SCALE[name]
        km, kv = _jax.random.split(_jax.random.fold_in(key, i + 1))
        out[name] = w
        out["m_" + name] = s * _jax.random.normal(km, w.shape, _jnp.float32)
        out["v_" + name] = (s * s) * _jax.random.uniform(kv, w.shape, _jnp.float32, 0.5, 1.5)
    if N_MICROBATCH > 1:
        for name, axis in PER_EXAMPLE_BATCH_AXIS.items():
            out[name] = _to_microbatches(out[name], axis)
    return {'x': out['x'], 'meta_tokens': out['meta_tokens'], 'ln_emb_g': out['ln_emb_g'], 'ln_emb_b': out['ln_emb_b'], 'w_in': out['w_in'], 'mla_q_norm': out['mla_q_norm'], 'mla_kv_norm': out['mla_kv_norm'], 'w_uq': out['w_uq'], 'w_ukv': out['w_ukv'], 'w_out': out['w_out'], 'ln1_g': out['ln1_g'], 'ln1_b': out['ln1_b'], 'w_ff1': out['w_ff1'], 'w_ff2': out['w_ff2'], 'ln2_g': out['ln2_g'], 'ln2_b': out['ln2_b'], 'loss_target': out['loss_target'], 'm_meta_tokens': out['m_meta_tokens'], 'm_ln_emb_g': out['m_ln_emb_g'], 'm_ln_emb_b': out['m_ln_emb_b'], 'm_w_in': out['m_w_in'], 'm_mla_q_norm': out['m_mla_q_norm'], 'm_mla_kv_norm': out['m_mla_kv_norm'], 'm_w_uq': out['m_w_uq'], 'm_w_ukv': out['m_w_ukv'], 'm_w_out': out['m_w_out'], 'm_ln1_g': out['m_ln1_g'], 'm_ln1_b': out['m_ln1_b'], 'm_w_ff1': out['m_w_ff1'], 'm_w_ff2': out['m_w_ff2'], 'm_ln2_g': out['m_ln2_g'], 'm_ln2_b': out['m_ln2_b'], 'v_meta_tokens': out['v_meta_tokens'], 'v_ln_emb_g': out['v_ln_emb_g'], 'v_ln_emb_b': out['v_ln_emb_b'], 'v_w_in': out['v_w_in'], 'v_mla_q_norm': out['v_mla_q_norm'], 'v_mla_kv_norm': out['v_mla_kv_norm'], 'v_w_uq': out['v_w_uq'], 'v_w_ukv': out['v_w_ukv'], 'v_w_out': out['v_w_out'], 'v_ln1_g': out['v_ln1_g'], 'v_ln1_b': out['v_ln1_b'], 'v_w_ff1': out['v_w_ff1'], 'v_w_ff2': out['v_w_ff2'], 'v_ln2_g': out['v_ln2_g'], 'v_ln2_b': out['v_ln2_b']}


def _loss(weights, diff, rest, loss_target):
    with _jax.named_scope("forward"):
        args = {**rest, TWIN_DIFF_INPUT: diff, **{k: w.astype(_WEIGHT_DTYPES[k]) for k, w in weights.items()}}
        y = _forward(args)
    with _jax.named_scope("loss_head"):
        err = _jnp.square(y.astype(_jnp.float32) - loss_target)
        return 0.5 * _jnp.sum(_jnp.mean(err, axis=-1)) if err.ndim else 0.5 * err


def _adamw(w, g, m, v):
    m = ADAM_B1 * m + (1.0 - ADAM_B1) * g
    v = ADAM_B2 * v + (1.0 - ADAM_B2) * _jnp.square(g)
    m_hat = m / (1.0 - ADAM_B1 ** ADAM_STEP)
    v_hat = v / (1.0 - ADAM_B2 ** ADAM_STEP)
    delta = -ADAM_LR * (m_hat / (_jnp.sqrt(v_hat) + ADAM_EPS) + ADAM_WD * w)
    return delta, m, v


def reference(x, meta_tokens, ln_emb_g, ln_emb_b, w_in, mla_q_norm, mla_kv_norm, w_uq, w_ukv, w_out, ln1_g, ln1_b, w_ff1, w_ff2, ln2_g, ln2_b, loss_target, m_meta_tokens, m_ln_emb_g, m_ln_emb_b, m_w_in, m_mla_q_norm, m_mla_kv_norm, m_w_uq, m_w_ukv, m_w_out, m_ln1_g, m_ln1_b, m_w_ff1, m_w_ff2, m_ln2_g, m_ln2_b, v_meta_tokens, v_ln_emb_g, v_ln_emb_b, v_w_in, v_mla_q_norm, v_mla_kv_norm, v_w_uq, v_w_ukv, v_w_out, v_ln1_g, v_ln1_b, v_w_ff1, v_w_ff2, v_ln2_g, v_ln2_b):
    given = dict(x=x, meta_tokens=meta_tokens, ln_emb_g=ln_emb_g, ln_emb_b=ln_emb_b, w_in=w_in, mla_q_norm=mla_q_norm, mla_kv_norm=mla_kv_norm, w_uq=w_uq, w_ukv=w_ukv, w_out=w_out, ln1_g=ln1_g, ln1_b=ln1_b, w_ff1=w_ff1, w_ff2=w_ff2, ln2_g=ln2_g, ln2_b=ln2_b, loss_target=loss_target, m_meta_tokens=m_meta_tokens, m_ln_emb_g=m_ln_emb_g, m_ln_emb_b=m_ln_emb_b, m_w_in=m_w_in, m_mla_q_norm=m_mla_q_norm, m_mla_kv_norm=m_mla_kv_norm, m_w_uq=m_w_uq, m_w_ukv=m_w_ukv, m_w_out=m_w_out, m_ln1_g=m_ln1_g, m_ln1_b=m_ln1_b, m_w_ff1=m_w_ff1, m_w_ff2=m_w_ff2, m_ln2_g=m_ln2_g, m_ln2_b=m_ln2_b, v_meta_tokens=v_meta_tokens, v_ln_emb_g=v_ln_emb_g, v_ln_emb_b=v_ln_emb_b, v_w_in=v_w_in, v_mla_q_norm=v_mla_q_norm, v_mla_kv_norm=v_mla_kv_norm, v_w_uq=v_w_uq, v_w_ukv=v_w_ukv, v_w_out=v_w_out, v_ln1_g=v_ln1_g, v_ln1_b=v_ln1_b, v_w_ff1=v_w_ff1, v_w_ff2=v_w_ff2, v_ln2_g=v_ln2_g, v_ln2_b=v_ln2_b)
    weights = {n: given[n] for n in TWIN_WEIGHTS}
    shared = {n: given[n] for n in SHARED_INPUTS}
    per_example = {n: given[n] for n in ['x']}
    grad_fn = _jax.value_and_grad(_loss, argnums=(0, 1))

    def one_microbatch(ex, loss_target):
        ex = dict(ex)
        diff = ex.pop(TWIN_DIFF_INPUT)
        return grad_fn(weights, diff, {**shared, **ex}, loss_target)

    if N_MICROBATCH == 1:
        loss, (grad_w, grad_x) = one_microbatch(per_example, given["loss_target"])
    else:
        def body(carry, xs):
            loss_sum, grad_sum = carry
            l_k, (gw_k, gx_k) = one_microbatch(xs[0], xs[1])
            with _jax.named_scope("update"):
                return (loss_sum + l_k, _jax.tree.map(_jnp.add, grad_sum, gw_k)), gx_k

        init = (_jnp.zeros((), _jnp.float32), _jax.tree.map(_jnp.zeros_like, weights))
        (loss, grad_w), grad_x = _jax.lax.scan(body, init, (per_example, given["loss_target"]))
    with _jax.named_scope("update"):
        delta_w, new_m, new_v = {}, {}, {}
        for n in TWIN_WEIGHTS:
            delta_w[n], new_m[n], new_v[n] = _adamw(weights[n], grad_w[n], given["m_" + n], given["v_" + n])
    return (loss, grad_x, *[grad_w[n] for n in TWIN_WEIGHTS], *[delta_w[n] for n in TWIN_WEIGHTS],
            *[new_m[n] for n in TWIN_WEIGHTS], *[new_v[n] for n in TWIN_WEIGHTS])
```

```python
import functools
import math

import numpy as np
import jax
import jax.numpy as jnp
from jax import lax
from jax.experimental import pallas as pl
from jax.experimental.pallas import tpu as pltpu

F32 = jnp.float32
BF16 = jnp.bfloat16

D_MODEL = 1024
DEPTH = 4
N_META = 16
BLK = 128
N_PAD = 112
SB_HEADS = 8
MLA_HEADS = 8
MLA_NOPE = 64
MLA_ROPE = 32
MLA_V = 64
MLA_Q_LORA = 384
MLA_KV_LORA = 256
RET_HEADS = 4
RET_QK = 64
RET_V = 128
D_FF = 4 * D_MODEL
ROPE_THETA = 10000.0
LN_EPS = 1e-5
DN_ALPHA = (2 * DEPTH) ** 0.25
RET_GAMMA = tuple(1.0 - 2.0 ** (-5 - h) for h in range(RET_HEADS))
RET_LOG_G = tuple(float(np.log(np.float32(g))) for g in RET_GAMMA)
MLA_SCALE = (MLA_NOPE + MLA_ROPE) ** -0.5

ADAM_LR = 0.001
ADAM_B1 = 0.9
ADAM_B2 = 0.999
ADAM_EPS = 1e-08
ADAM_WD = 0.01
ADAM_STEP = 10

C_SBQ, C_SBK, C_SBV = 0, 512, 1024
C_RQ, C_RK, C_RV, C_RG = 1536, 1792, 2048, 2560
C_CQ, C_CKV, C_KR = 3072, 3456, 3712
N_IN = 3744
N_INP = 3840

NEG = -1e30


def _pick(n, cands):
    for t in cands:
        if n % t == 0:
            return t
    raise ValueError(f"no tile for {n} in {cands}")


def _row_tile(n):
    return _pick(n, (1056, 1024, 528, 512, 384, 256, 128))


def _dot(a, b):
    return jnp.dot(a, b, preferred_element_type=F32)


def _dot_nt(a, b):
    return lax.dot_general(a, b, (((1,), (1,)), ((), ())), preferred_element_type=F32)


def _dot_tn(a, b):
    return lax.dot_general(a, b, (((0,), (0,)), ((), ())), preferred_element_type=F32)


def mm_nn(a, b, *, tn, name, tk=None, prologue=None, axpy=None, out_dtype=F32):
    M, K = a.shape
    S, _, Ns = b.shape
    tm = _row_tile(M)
    tk = K if tk is None else tk
    npt = Ns // tn
    nk = K // tk
    alpha = None if axpy is None else axpy[1]

    def body(*refs):
        if axpy is None:
            a_ref, b_ref, o_ref, acc = refs
        else:
            a_ref, b_ref, e_ref, o_ref, acc = refs
        k = pl.program_id(2)

        @pl.when(k == 0)
        def _():
            acc[...] = jnp.zeros_like(acc)

        x = a_ref[...]
        if prologue == "relu2":
            x = jnp.square(jnp.maximum(x, 0.0))
        acc[...] += _dot(x.astype(BF16), b_ref[0])

        @pl.when(k == nk - 1)
        def _():
            r = acc[...]
            if axpy is not None:
                r = r + alpha * e_ref[...]
            o_ref[...] = r.astype(out_dtype)

    in_specs = [pl.BlockSpec((tm, tk), lambda i, j, k: (i, k)),
                pl.BlockSpec((1, tk, tn), lambda i, j, k: (j // npt, k, j % npt))]
    args = [a, b]
    if axpy is not None:
        in_specs.append(pl.BlockSpec((tm, tn), lambda i, j, k: (i, j)))
        args.append(axpy[0])
    return pl.pallas_call(
        body, name=name, grid=(M // tm, (S * Ns) // tn, nk), in_specs=in_specs,
        out_specs=pl.BlockSpec((tm, tn), lambda i, j, k: (i, j)),
        out_shape=jax.ShapeDtypeStruct((M, S * Ns), out_dtype),
        scratch_shapes=[pltpu.VMEM((tm, tn), F32)],
        compiler_params=pltpu.CompilerParams(dimension_semantics=("parallel", "parallel", "arbitrary")),
    )(*args)


def mm_nt(a, b, *, tn, tko, name, axpy=None, relu2grad=None):
    M, N = a.shape
    S, K, Ns = b.shape
    tm = _row_tile(M)
    npt = Ns // tn
    nn = N // tn
    alpha = None if axpy is None else axpy[1]

    def body(*refs):
        if axpy is None and relu2grad is None:
            a_ref, b_ref, o_ref, acc = refs
        else:
            a_ref, b_ref, e_ref, o_ref, acc = refs
        n = pl.program_id(2)

        @pl.when(n == 0)
        def _():
            acc[...] = jnp.zeros_like(acc)

        acc[...] += _dot_nt(a_ref[...].astype(BF16), b_ref[0])

        @pl.when(n == nn - 1)
        def _():
            r = acc[...]
            if axpy is not None:
                r = r + alpha * e_ref[...]
            if relu2grad is not None:
                r = r * (2.0 * jnp.maximum(e_ref[...], 0.0))
            o_ref[...] = r

    in_specs = [pl.BlockSpec((tm, tn), lambda i, j, n: (i, n)),
                pl.BlockSpec((1, tko, tn), lambda i, j, n: (n // npt, j, n % npt))]
    args = [a, b]
    extra = axpy[0] if axpy is not None else relu2grad
    if extra is not None:
        in_specs.append(pl.BlockSpec((tm, tko), lambda i, j, n: (i, j)))
        args.append(extra)
    return pl.pallas_call(
        body, name=name, grid=(M // tm, K // tko, nn), in_specs=in_specs,
        out_specs=pl.BlockSpec((tm, tko), lambda i, j, n: (i, j)),
        out_shape=jax.ShapeDtypeStruct((M, K), F32),
        scratch_shapes=[pltpu.VMEM((tm, tko), F32)],
        compiler_params=pltpu.CompilerParams(dimension_semantics=("parallel", "parallel", "arbitrary")),
    )(*args)


def mm_tn(a, g, *, shards, tko, tn, name, prologue=None):
    M, K = a.shape
    _, N = g.shape
    Ns = N // shards
    tm = _row_tile(M)
    npt = Ns // tn
    nm = M // tm

    def body(a_ref, g_ref, o_ref, acc):
        m = pl.program_id(2)

        @pl.when(m == 0)
        def _():
            acc[...] = jnp.zeros_like(acc)

        x = a_ref[...]
        if prologue == "relu2":
            x = jnp.square(jnp.maximum(x, 0.0))
        acc[...] += _dot_tn(x.astype(BF16), g_ref[...].astype(BF16))

        @pl.when(m == nm - 1)
        def _():
            o_ref[0] = acc[...]

    return pl.pallas_call(
        body, name=name, grid=(K // tko, N // tn, nm),
        in_specs=[pl.BlockSpec((tm, tko), lambda i, j, m: (m, i)),
                  pl.BlockSpec((tm, tn), lambda i, j, m: (m, j))],
        out_specs=pl.BlockSpec((1, tko, tn), lambda i, j, m: (j // npt, i, j % npt)),
        out_shape=jax.ShapeDtypeStruct((shards, K, Ns), F32),
        scratch_shapes=[pltpu.VMEM((tko, tn), F32)],
        compiler_params=pltpu.CompilerParams(dimension_semantics=("parallel", "parallel", "arbitrary")),
    )(a, g)


def _ln_stats(z):
    mu = jnp.mean(z, axis=-1, keepdims=True)
    zc = z - mu
    var = jnp.mean(jnp.square(zc), axis=-1, keepdims=True)
    r = lax.rsqrt(var + LN_EPS)
    return zc * r, r


def ln_fwd(x, g, b, *, name, res=None):
    L, Dm = x.shape
    tr = _row_tile(L)
    g2, b2 = g.reshape(1, Dm), b.reshape(1, Dm)

    def body(*refs):
        if res is None:
            x_ref, g_ref, b_ref, y_ref = refs
            z = x_ref[...]
        else:
            x_ref, r_ref, g_ref, b_ref, y_ref, z_ref = refs
            z = DN_ALPHA * r_ref[...] + x_ref[...]
            z_ref[...] = z
        xh, _ = _ln_stats(z)
        y_ref[...] = xh * g_ref[...] + b_ref[...]

    row = pl.BlockSpec((tr, Dm), lambda i: (i, 0))
    vec = pl.BlockSpec((1, Dm), lambda i: (0, 0))
    sds = jax.ShapeDtypeStruct((L, Dm), F32)
    if res is None:
        y = pl.pallas_call(body, name=name, grid=(L // tr,), in_specs=[row, vec, vec], out_specs=row, out_shape=sds)(x, g2, b2)
        return y, x
    return pl.pallas_call(body, name=name, grid=(L // tr,), in_specs=[row, row, vec, vec], out_specs=(row, row),
                          out_shape=(sds, sds))(x, res, g2, b2)


def ln_bwd(dy, z, g, *, name):
    L, Dm = z.shape
    tr = _row_tile(L)

    def body(dy_ref, z_ref, g_ref, dz_ref, dg_ref, db_ref):
        @pl.when(pl.program_id(0) == 0)
        def _():
            dg_ref[...] = jnp.zeros_like(dg_ref)
            db_ref[...] = jnp.zeros_like(db_ref)

        dyv = dy_ref[...]
        xh, r = _ln_stats(z_ref[...])
        dxh = dyv * g_ref[...]
        m1 = jnp.mean(dxh, axis=-1, keepdims=True)
        m2 = jnp.mean(dxh * xh, axis=-1, keepdims=True)
        dz_ref[...] = r * (dxh - m1 - xh * m2)
        dg_ref[...] += jnp.sum(dyv * xh, axis=0, keepdims=True)
        db_ref[...] += jnp.sum(dyv, axis=0, keepdims=True)

    row = pl.BlockSpec((tr, Dm), lambda i: (i, 0))
    vec = pl.BlockSpec((1, Dm), lambda i: (0, 0))
    return pl.pallas_call(
        body, name=name, grid=(L // tr,), in_specs=[row, row, vec], out_specs=(row, vec, vec),
        out_shape=(jax.ShapeDtypeStruct((L, Dm), F32), jax.ShapeDtypeStruct((1, Dm), F32), jax.ShapeDtypeStruct((1, Dm), F32)),
        compiler_params=pltpu.CompilerParams(dimension_semantics=("arbitrary",)),
    )(dy, z, g.reshape(1, Dm))


def loss_fwd_bwd(h, target, *, name):
    L, Dm = h.shape
    nb = L // BLK

    def body(h_ref, t_ref, l_ref, dh_ref):
        i = pl.program_id(0)

        @pl.when(i == 0)
        def _():
            l_ref[...] = jnp.zeros_like(l_ref)
            dh_ref[...] = jnp.zeros_like(dh_ref)

        @pl.when(i > 0)
        def _():
            e = h_ref[...] - t_ref[...]
            dh_ref[...] = e * (1.0 / Dm)
            part = jnp.sum(jnp.sum(jnp.square(e), axis=-1, keepdims=True) * (1.0 / Dm), axis=0, keepdims=True)
            l_ref[...] += 0.5 * part

    return pl.pallas_call(
        body, name=name, grid=(nb,),
        in_specs=[pl.BlockSpec((BLK, Dm), lambda i: (i, 0)),
                  pl.BlockSpec((BLK, Dm), lambda i: (jnp.maximum(i - 1, 0), 0))],
        out_specs=(pl.BlockSpec((8, 128), lambda i: (0, 0)), pl.BlockSpec((BLK, Dm), lambda i: (i, 0))),
        out_shape=(jax.ShapeDtypeStruct((8, 128), F32), jax.ShapeDtypeStruct((L, Dm), F32)),
        compiler_params=pltpu.CompilerParams(dimension_semantics=("arbitrary",)),
    )(h, target)


def _swap_half(x, half):
    ax = x.ndim - 1
    n = x.shape[ax]
    lane = lax.broadcasted_iota(jnp.int32, x.shape, ax)
    up = pltpu.roll(x, n - half, ax)
    dn = pltpu.roll(x, half, ax)
    return jnp.where((lane % (2 * half)) < half, up, dn)


def _rope(x, cs, sn, half):
    return x * cs + _swap_half(x, half) * sn


def _rope_t(dy, cs, sn, half):
    return dy * cs + _swap_half(dy * sn, half)


def _rms(x):
    r = lax.rsqrt(jnp.mean(jnp.square(x), axis=-1, keepdims=True) + LN_EPS)
    return x * r, r


def mla_pre_fwd(P, gq, gkv, cs, sn, *, name):
    L = P.shape[0]
    tr = _row_tile(L)

    def body(p_ref, gq_ref, gkv_ref, cs_ref, sn_ref, nq_ref, nkv_ref, kr_ref):
        cq = p_ref[:, 0:MLA_Q_LORA]
        ckv = p_ref[:, MLA_Q_LORA:MLA_Q_LORA + MLA_KV_LORA]
        kr = p_ref[:, 640:768]
        nq_ref[...] = (_rms(cq)[0] * gq_ref[...]).astype(BF16)
        nkv_ref[...] = (_rms(ckv)[0] * gkv_ref[...]).astype(BF16)
        kr_ref[...] = _rope(kr, cs_ref[...], sn_ref[...], MLA_ROPE // 2).astype(BF16)

    return pl.pallas_call(
        body, name=name, grid=(L // tr,),
        in_specs=[pl.BlockSpec((tr, 768), lambda i: (i, C_CQ // 768)),
                  pl.BlockSpec((1, MLA_Q_LORA), lambda i: (0, 0)), pl.BlockSpec((1, MLA_KV_LORA), lambda i: (0, 0)),
                  pl.BlockSpec((tr, 128), lambda i: (i, 0)), pl.BlockSpec((tr, 128), lambda i: (i, 0))],
        out_specs=(pl.BlockSpec((tr, MLA_Q_LORA), lambda i: (i, 0)), pl.BlockSpec((tr, MLA_KV_LORA), lambda i: (i, 0)),
                   pl.BlockSpec((tr, 128), lambda i: (i, 0))),
        out_shape=(jax.ShapeDtypeStruct((L, MLA_Q_LORA), BF16), jax.ShapeDtypeStruct((L, MLA_KV_LORA), BF16),
                   jax.ShapeDtypeStruct((L, 128), BF16)),
    )(P, gq.reshape(1, -1), gkv.reshape(1, -1), cs, sn)


def mla_pre_bwd(P, dnq, dnkv, dkr, gq, gkv, cs, sn, *, name):
    L = P.shape[0]
    tr = _row_tile(L)

    def body(p_ref, dnq_ref, dnkv_ref, dkr_ref, gq_ref, gkv_ref, cs_ref, sn_ref, dp_ref, dgq_ref, dgkv_ref):
        @pl.when(pl.program_id(0) == 0)
        def _():
            dgq_ref[...] = jnp.zeros_like(dgq_ref)
            dgkv_ref[...] = jnp.zeros_like(dgkv_ref)

        def rms_bwd(x, dy, g_ref, dg_ref):
            xn, r = _rms(x)
            dxn = dy * g_ref[...]
            dg_ref[...] += jnp.sum(dy * xn, axis=0, keepdims=True)
            return r * (dxn - xn * jnp.mean(dxn * xn, axis=-1, keepdims=True))

        dp_ref[:, 0:MLA_Q_LORA] = rms_bwd(p_ref[:, 0:MLA_Q_LORA], dnq_ref[...], gq_ref, dgq_ref)
        dp_ref[:, MLA_Q_LORA:640] = rms_bwd(p_ref[:, MLA_Q_LORA:640], dnkv_ref[...], gkv_ref, dgkv_ref)
        dp_ref[:, 640:768] = _rope_t(dkr_ref[...], cs_ref[...], sn_ref[...], MLA_ROPE // 2)

    return pl.pallas_call(
        body, name=name, grid=(L // tr,),
        in_specs=[pl.BlockSpec((tr, 768), lambda i: (i, C_CQ // 768)),
                  pl.BlockSpec((tr, MLA_Q_LORA), lambda i: (i, 0)), pl.BlockSpec((tr, MLA_KV_LORA), lambda i: (i, 0)),
                  pl.BlockSpec((tr, 128), lambda i: (i, 0)),
                  pl.BlockSpec((1, MLA_Q_LORA), lambda i: (0, 0)), pl.BlockSpec((1, MLA_KV_LORA), lambda i: (0, 0)),
                  pl.BlockSpec((tr, 128), lambda i: (i, 0)), pl.BlockSpec((tr, 128), lambda i: (i, 0))],
        out_specs=(pl.BlockSpec((tr, 768), lambda i: (i, 0)), pl.BlockSpec((1, MLA_Q_LORA), lambda i: (0, 0)),
                   pl.BlockSpec((1, MLA_KV_LORA), lambda i: (0, 0))),
        out_shape=(jax.ShapeDtypeStruct((L, 768), F32), jax.ShapeDtypeStruct((1, MLA_Q_LORA), F32),
                   jax.ShapeDtypeStruct((1, MLA_KV_LORA), F32)),
        compiler_params=pltpu.CompilerParams(dimension_semantics=("arbitrary",)),
    )(P, dnq, dnkv, dkr, gq.reshape(1, -1), gkv.reshape(1, -1), cs, sn)


def _tri(kind):
    r = lax.broadcasted_iota(jnp.int32, (BLK, BLK), 0)
    c = lax.broadcasted_iota(jnp.int32, (BLK, BLK), 1)
    t = ((r > c) if kind == "right" else (r < c)).astype(BF16)
    return jnp.concatenate([t, t], axis=0)


def _tri_sum(x, tt):
    hi = x.astype(BF16)
    lo = (x - hi.astype(F32)).astype(BF16)
    return _dot(jnp.concatenate([hi, lo], axis=1), tt)


def _sb_tile(q, k, i, j, tt_right, R):
    row = lax.broadcasted_iota(jnp.int32, (BLK, BLK), 0)
    col = lax.broadcasted_iota(jnp.int32, (BLK, BLK), 1)
    s_idx = j * BLK + col
    mask = (s_idx < i * BLK + row) & (s_idx >= N_PAD)
    z = _dot_nt(q, k)
    sp = jnp.maximum(z, 0.0) + jnp.log1p(jnp.exp(-jnp.abs(z)))
    lk = jnp.where(mask, -sp, 0.0)
    E = _tri_sum(lk, tt_right) + R
    return mask, z, sp, lk, E


def sb_fwd(P, *, name):
    L = P.shape[0]
    nb = L // BLK

    def body(q_ref, k_ref, v_ref, o_ref, c_ref):
        i = pl.program_id(1)
        tt = _tri("right")
        lane = lax.broadcasted_iota(jnp.int32, (BLK, 128), 1)
        qs = [(q_ref[:, 64 * h:64 * h + 64] * 0.125).astype(BF16) for h in range(2)]

        def step(jj, carry):
            j = i - jj
            off = pl.multiple_of(j * BLK, BLK)
            kb = k_ref[pl.ds(off, BLK), :].astype(BF16)
            vb = v_ref[pl.ds(off, BLK), :].astype(BF16)
            out = []
            for h in range(2):
                o, R = carry[2 * h], carry[2 * h + 1]
                sl = slice(64 * h, 64 * h + 64)
                mask, z, sp, lk, E = _sb_tile(qs[h], kb[:, sl], i, j, tt, R)
                w = jnp.where(mask, jnp.exp(z - sp + E), 0.0)
                c_ref[h] = jnp.where(lane == j, R, c_ref[h])
                out += [o + _dot(w.astype(BF16), vb[:, sl]), R + jnp.sum(lk, axis=1, keepdims=True)]
            return tuple(out)

        c_ref[...] = jnp.zeros_like(c_ref)
        z0 = (jnp.zeros((BLK, 64), F32), jnp.zeros((BLK, 1), F32))
        res = lax.fori_loop(0, i + 1, step, z0 + z0)
        o_ref[...] = jnp.concatenate([res[0], res[2]], axis=1)

    return pl.pallas_call(
        body, name=name, grid=(SB_HEADS // 2, nb),
        in_specs=[pl.BlockSpec((BLK, 128), lambda hp, i: (i, C_SBQ // 128 + hp)),
                  pl.BlockSpec((L, 128), lambda hp, i: (0, C_SBK // 128 + hp)),
                  pl.BlockSpec((L, 128), lambda hp, i: (0, C_SBV // 128 + hp))],
        out_specs=(pl.BlockSpec((BLK, 128), lambda hp, i: (i, hp)), pl.BlockSpec((2, BLK, 128), lambda hp, i: (hp, i, 0))),
        out_shape=(jax.ShapeDtypeStruct((L, 512), F32), jax.ShapeDtypeStruct((SB_HEADS, L, 128), F32)),
        compiler_params=pltpu.CompilerParams(dimension_semantics=("parallel", "arbitrary")),
    )(P, P, P)


def sb_bwd(P, carries, dmixed, *, name):
    L = P.shape[0]
    nb = L // BLK

    def body(q_ref, k_ref, v_ref, c_ref, do_ref, dq_ref, dk_ref, dv_ref):
        i = pl.program_id(1)

        @pl.when(i == 0)
        def _():
            dk_ref[...] = jnp.zeros_like(dk_ref)
            dv_ref[...] = jnp.zeros_like(dv_ref)

        tr = _tri("right")
        tl = _tri("left")
        lane = lax.broadcasted_iota(jnp.int32, (BLK, 128), 1)
        qs = [(q_ref[:, 64 * h:64 * h + 64] * 0.125).astype(BF16) for h in range(2)]
        dos = [do_ref[:, 64 * h:64 * h + 64].astype(BF16) for h in range(2)]

        def step(j, carry):
            off = pl.multiple_of(j * BLK, BLK)
            kb = k_ref[pl.ds(off, BLK), :].astype(BF16)
            vb = v_ref[pl.ds(off, BLK), :].astype(BF16)
            out, dks, dvs = [], [], []
            for h in range(2):
                dq, PL = carry[2 * h], carry[2 * h + 1]
                sl = slice(64 * h, 64 * h + 64)
                R = jnp.sum(jnp.where(lane == j, c_ref[h], 0.0), axis=1, keepdims=True)
                mask, z, sp, lk, E = _sb_tile(qs[h], kb[:, sl], i, j, tr, R)
                sig = jnp.exp(z - sp)
                w = jnp.where(mask, sig * jnp.exp(E), 0.0)
                dA = _dot_nt(dos[h], vb[:, sl]) * w
                Pp = _tri_sum(dA, tl) + PL
                dz = jnp.where(mask, dA - sig * (dA + Pp), 0.0).astype(BF16)
                dks.append(_dot_tn(dz, qs[h]))
                dvs.append(_dot_tn(w.astype(BF16), dos[h]))
                out += [dq + _dot(dz, kb[:, sl]), PL + jnp.sum(dA, axis=1, keepdims=True)]
            dk_ref[pl.ds(off, BLK), :] += jnp.concatenate(dks, axis=1)
            dv_ref[pl.ds(off, BLK), :] += jnp.concatenate(dvs, axis=1)
            return tuple(out)

        z0 = (jnp.zeros((BLK, 64), F32), jnp.zeros((BLK, 1), F32))
        res = lax.fori_loop(0, i + 1, step, z0 + z0)
        dq_ref[...] = jnp.concatenate([res[0], res[2]], axis=1) * 0.125

    blk = lambda c0: pl.BlockSpec((BLK, 128), lambda hp, i: (i, c0 + hp))
    full = lambda c0: pl.BlockSpec((L, 128), lambda hp, i: (0, c0 + hp))
    sds = jax.ShapeDtypeStruct((L, 512), F32)
    return pl.pallas_call(
        body, name=name, grid=(SB_HEADS // 2, nb),
        in_specs=[blk(C_SBQ // 128), full(C_SBK // 128), full(C_SBV // 128),
                  pl.BlockSpec((2, BLK, 128), lambda hp, i: (hp, i, 0)), blk(0)],
        out_specs=(blk(0), full(0), full(0)), out_shape=(sds, sds, sds),
        compiler_params=pltpu.CompilerParams(dimension_semantics=("parallel", "arbitrary")),
    )(P, P, P, carries, dmixed)


def _mla_mask(i, j):
    row = lax.broadcasted_iota(jnp.int32, (BLK, BLK), 0)
    col = lax.broadcasted_iota(jnp.int32, (BLK, BLK), 1)
    t_idx = i * BLK + row
    s_idx = j * BLK + col
    return (s_idx <= t_idx) & ((s_idx >= N_PAD) | (s_idx == t_idx))


def _mla_q(q_ref, cs_ref, sn_ref):
    qr = _rope(q_ref[:, 128:256], cs_ref[...], sn_ref[...], MLA_ROPE // 2)
    qn = [q_ref[:, 64 * h:64 * h + 64].astype(BF16) for h in range(2)]
    qrs = [qr[:, 32 * h:32 * h + 32].astype(BF16) for h in range(2)]
    return qn, qrs


def mla_fwd(Q, KV, KR, cs, sn, *, name):
    L = Q.shape[0]
    nb = L // BLK

    def body(q_ref, kn_ref, v_ref, kr_ref, cs_ref, sn_ref, o_ref, lse_ref):
        i = pl.program_id(1)
        qn, qrs = _mla_q(q_ref, cs_ref, sn_ref)

        def step(j, carry):
            off = pl.multiple_of(j * BLK, BLK)
            knb = kn_ref[pl.ds(off, BLK), :]
            vb = v_ref[pl.ds(off, BLK), :]
            krb = kr_ref[pl.ds(off, BLK), 0:MLA_ROPE]
            mask = _mla_mask(i, j)
            out = []
            for h in range(2):
                m, l, acc = carry[3 * h], carry[3 * h + 1], carry[3 * h + 2]
                sl = slice(64 * h, 64 * h + 64)
                s = (_dot_nt(qn[h], knb[:, sl]) + _dot_nt(qrs[h], krb)) * MLA_SCALE
                s = jnp.where(mask, s, NEG)
                m_new = jnp.maximum(m, jnp.max(s, axis=1, keepdims=True))
                a = jnp.exp(m - m_new)
                p = jnp.exp(s - m_new)
                out += [m_new, a * l + jnp.sum(p, axis=1, keepdims=True), a * acc + _dot(p.astype(BF16), vb[:, sl])]
            return tuple(out)

        z0 = (jnp.full((BLK, 1), NEG, F32), jnp.zeros((BLK, 1), F32), jnp.zeros((BLK, 64), F32))
        res = lax.fori_loop(0, i + 1, step, z0 + z0)
        o_ref[...] = jnp.concatenate([res[2] / res[1], res[5] / res[4]], axis=1)
        lane = lax.broadcasted_iota(jnp.int32, (BLK, 128), 1)
        lse0 = res[0] + jnp.log(res[1])
        lse1 = res[3] + jnp.log(res[4])
        lse_ref[0] = jnp.where(lane == 0, lse0, jnp.where(lane == 1, lse1, 0.0))

    return pl.pallas_call(
        body, name=name, grid=(MLA_HEADS // 2, nb),
        in_specs=[pl.BlockSpec((BLK, 256), lambda hp, i: (i, hp)),
                  pl.BlockSpec((L, 128), lambda hp, i: (0, hp)),
                  pl.BlockSpec((L, 128), lambda hp, i: (0, 4 + hp)),
                  pl.BlockSpec((L, 128), lambda hp, i: (0, 0)),
                  pl.BlockSpec((BLK, 128), lambda hp, i: (i, 0)), pl.BlockSpec((BLK, 128), lambda hp, i: (i, 0))],
        out_specs=(pl.BlockSpec((BLK, 128), lambda hp, i: (i, hp)), pl.BlockSpec((1, BLK, 128), lambda hp, i: (hp, i, 0))),
        out_shape=(jax.ShapeDtypeStruct((L, 512), F32), jax.ShapeDtypeStruct((4, L, 128), F32)),
        compiler_params=pltpu.CompilerParams(dimension_semantics=("parallel", "arbitrary")),
    )(Q, KV, KV, KR, cs, sn)


def mla_bwd(Q, KV, KR, cs, sn, mixed, dmixed, lse, *, name):
    L = Q.shape[0]
    nb = L // BLK

    def body(q_ref, kn_ref, v_ref, kr_ref, cs_ref, sn_ref, o_ref, do_ref, lse_ref, dq_ref, dkn_ref, dv_ref, dkr_ref):
        hp = pl.program_id(0)
        i = pl.program_id(1)

        @pl.when(i == 0)
        def _():
            dkn_ref[...] = jnp.zeros_like(dkn_ref)
            dv_ref[...] = jnp.zeros_like(dv_ref)

        @pl.when((i == 0) & (hp == 0))
        def _():
            dkr_ref[...] = jnp.zeros_like(dkr_ref)

        qn, qrs = _mla_q(q_ref, cs_ref, sn_ref)
        dos, dd, lses = [], [], []
        for h in range(2):
            sl = slice(64 * h, 64 * h + 64)
            d = do_ref[:, sl]
            dos.append(d.astype(BF16))
            dd.append(jnp.sum(d * o_ref[:, sl], axis=1, keepdims=True))
            lses.append(lse_ref[0, :, h:h + 1])

        def step(j, carry):
            off = pl.multiple_of(j * BLK, BLK)
            knb = kn_ref[pl.ds(off, BLK), :]
            vb = v_ref[pl.ds(off, BLK), :]
            krb = kr_ref[pl.ds(off, BLK), 0:MLA_ROPE]
            mask = _mla_mask(i, j)
            out, dkns, dvs = [], [], []
            dkr = jnp.zeros((BLK, MLA_ROPE), F32)
            for h in range(2):
                dqn, dqr = carry[2 * h], carry[2 * h + 1]
                sl = slice(64 * h, 64 * h + 64)
                s = (_dot_nt(qn[h], knb[:, sl]) + _dot_nt(qrs[h], krb)) * MLA_SCALE
                p = jnp.where(mask, jnp.exp(s - lses[h]), 0.0)
                dp = _dot_nt(dos[h], vb[:, sl])
                ds = (p * (dp - dd[h]) * MLA_SCALE).astype(BF16)
                dkns.append(_dot_tn(ds, qn[h]))
                dvs.append(_dot_tn(p.astype(BF16), dos[h]))
                dkr = dkr + _dot_tn(ds, qrs[h])
                out += [dqn + _dot(ds, knb[:, sl]), dqr + _dot(ds, krb)]
            dkn_ref[pl.ds(off, BLK), :] += jnp.concatenate(dkns, axis=1)
            dv_ref[pl.ds(off, BLK), :] += jnp.concatenate(dvs, axis=1)
            dkr_ref[pl.ds(off, BLK), :] += jnp.concatenate([dkr, jnp.zeros((BLK, 128 - MLA_ROPE), F32)], axis=1)
            return tuple(out)

        z0 = (jnp.zeros((BLK, 64), F32), jnp.zeros((BLK, MLA_ROPE), F32))
        res = lax.fori_loop(0, i + 1, step, z0 + z0)
        dqr = jnp.concatenate([res[1], res[3], jnp.zeros((BLK, 64), F32)], axis=1)
        dq_ref[...] = jnp.concatenate([res[0], res[2], _rope_t(dqr, cs_ref[...], sn_ref[...], MLA_ROPE // 2)], axis=1)

    blk = lambda c0: pl.BlockSpec((BLK, 128), lambda hp, i: (i, c0 + hp))
    full = lambda c0: pl.BlockSpec((L, 128), lambda hp, i: (0, c0 + hp))
    tab = pl.BlockSpec((BLK, 128), lambda hp, i: (i, 0))
    return pl.pallas_call(
        body, name=name, grid=(MLA_HEADS // 2, nb),
        in_specs=[pl.BlockSpec((BLK, 256), lambda hp, i: (i, hp)), full(0), full(4),
                  pl.BlockSpec((L, 128), lambda hp, i: (0, 0)), tab, tab, blk(4), blk(4),
                  pl.BlockSpec((1, BLK, 128), lambda hp, i: (hp, i, 0))],
        out_specs=(pl.BlockSpec((BLK, 256), lambda hp, i: (i, hp)), full(0), full(0),
                   pl.BlockSpec((L, 128), lambda hp, i: (0, 0))),
        out_shape=(jax.ShapeDtypeStruct((L, 1024), F32), jax.ShapeDtypeStruct((L, 512), F32),
                   jax.ShapeDtypeStruct((L, 512), F32), jax.ShapeDtypeStruct((L, 128), F32)),
        compiler_params=pltpu.CompilerParams(dimension_semantics=("arbitrary", "arbitrary")),
    )(Q, KV, KV, KR, cs, sn, mixed, dmixed, lse)


def _ret_decay(h):
    lg = RET_LOG_G[h]
    r = lax.broadcasted_iota(jnp.int32, (BLK, BLK), 0)
    c = lax.broadcasted_iota(jnp.int32, (BLK, BLK), 1)
    diff = (r - c).astype(F32)
    d_in = jnp.where(diff >= 0, jnp.exp(jnp.maximum(diff, 0.0) * lg), 0.0)
    idx = lax.broadcasted_iota(jnp.int32, (BLK, 1), 0).astype(F32)
    q_decay = jnp.exp((idx + 1.0) * lg)
    k_decay = jnp.exp((BLK - 1.0 - idx) * lg)
    c_decay = math.exp(BLK * lg)
    return d_in, q_decay, k_decay, c_decay


def _ret_qk(qk_ref, cs_ref, sn_ref, n):
    cs = jnp.concatenate([cs_ref[...]] * 2, axis=1)
    sn = jnp.concatenate([sn_ref[...]] * 2, axis=1)
    rq = _rope(qk_ref[:, 0:256], cs, sn, RET_QK // 2)
    row = n * BLK + lax.broadcasted_iota(jnp.int32, (BLK, 256), 0)
    kmul = jnp.where(row >= N_PAD, 0.125, 0.0)
    rk = _rope(qk_ref[:, 256:512], cs, sn, RET_QK // 2) * kmul
    return rq, rk, cs, sn, kmul


def _head_norm(y):
    mu = jnp.mean(y, axis=-1, keepdims=True)
    yc = y - mu
    r = lax.rsqrt(jnp.mean(jnp.square(yc), axis=-1, keepdims=True) + LN_EPS)
    return yc * r, r


def ret_fwd(P, cs, sn, *, name):
    L = P.shape[0]
    nb = L // BLK

    def body(qk_ref, v_ref, g_ref, cs_ref, sn_ref, o_ref, y_ref, st_ref, state):
        n = pl.program_id(0)

        @pl.when(n == 0)
        def _():
            state[...] = jnp.zeros_like(state)

        st_ref[0] = state[...]
        rq, rk, _, _, _ = _ret_qk(qk_ref, cs_ref, sn_ref, n)
        outs, ys = [], []
        for h in range(RET_HEADS):
            d_in, q_decay, k_decay, c_decay = _ret_decay(h)
            q = rq[:, 64 * h:64 * h + 64].astype(BF16)
            kf = rk[:, 64 * h:64 * h + 64]
            v = v_ref[:, 128 * h:128 * h + 128].astype(BF16)
            S = state[h]
            inner = _dot_nt(q, kf.astype(BF16)) * d_in
            y = _dot(inner.astype(BF16), v) + _dot(q, S.astype(BF16)) * q_decay
            state[h] = S * c_decay + _dot_tn((kf * k_decay).astype(BF16), v)
            g = g_ref[:, 128 * h:128 * h + 128]
            ys.append(y)
            outs.append(g * jax.nn.sigmoid(g) * _head_norm(y)[0])
        o_ref[...] = jnp.concatenate(outs, axis=1)
        y_ref[...] = jnp.concatenate(ys, axis=1)

    blk512 = lambda c: pl.BlockSpec((BLK, 512), lambda n: (n, c))
    tab = pl.BlockSpec((BLK, 128), lambda n: (n, 0))
    return pl.pallas_call(
        body, name=name, grid=(nb,),
        in_specs=[blk512(C_RQ // 512), blk512(C_RV // 512), blk512(C_RG // 512), tab, tab],
        out_specs=(blk512(0), blk512(0), pl.BlockSpec((1, RET_HEADS, RET_QK, RET_V), lambda n: (n, 0, 0, 0))),
        out_shape=(jax.ShapeDtypeStruct((L, 512), F32), jax.ShapeDtypeStruct((L, 512), F32),
                   jax.ShapeDtypeStruct((nb, RET_HEADS, RET_QK, RET_V), F32)),
        scratch_shapes=[pltpu.VMEM((RET_HEADS, RET_QK, RET_V), F32)],
        compiler_params=pltpu.CompilerParams(dimension_semantics=("arbitrary",)),
    )(P, P, P, cs, sn)


def ret_bwd(P, y, states, dmixed, cs, sn, *, name):
    L = P.shape[0]
    nb = L // BLK

    def body(qk_ref, v_ref, g_ref, y_ref, st_ref, do_ref, cs_ref, sn_ref, dqk_ref, dv_ref, dg_ref, dstate):
        n = nb - 1 - pl.program_id(0)

        @pl.when(pl.program_id(0) == 0)
        def _():
            dstate[...] = jnp.zeros_like(dstate)

        rq, rk, cs, sn, kmul = _ret_qk(qk_ref, cs_ref, sn_ref, n)
        dqs, dks, dvs, dgs = [], [], [], []
        for h in range(RET_HEADS):
            d_in, q_decay, k_decay, c_decay = _ret_decay(h)
            sv = slice(128 * h, 128 * h + 128)
            q = rq[:, 64 * h:64 * h + 64].astype(BF16)
            kf = rk[:, 64 * h:64 * h + 64]
            k = kf.astype(BF16)
            kd = (kf * k_decay).astype(BF16)
            v = v_ref[:, sv].astype(BF16)
            g = g_ref[:, sv]
            do = do_ref[:, sv]
            yh = y_ref[:, sv]
            S = st_ref[0, h].astype(BF16)
            dS = dstate[h]
            sg = jax.nn.sigmoid(g)
            yn, r = _head_norm(yh)
            dgs.append(do * yn * (sg * (1.0 + g * (1.0 - sg))))
            dyn = do * (g * sg)
            dy = r * (dyn - jnp.mean(dyn, axis=-1, keepdims=True) - yn * jnp.mean(dyn * yn, axis=-1, keepdims=True))
            dyb = dy.astype(BF16)
            dyq = (dy * q_decay).astype(BF16)
            inner = (_dot_nt(q, k) * d_in).astype(BF16)
            A = (_dot_nt(dyb, v) * d_in).astype(BF16)
            dSb = dS.astype(BF16)
            dqs.append(_dot(A, k) + _dot_nt(dyq, S))
            dks.append(_dot_tn(A, q) + _dot_nt(v, dSb) * k_decay)
            dvs.append(_dot_tn(inner, dyb) + _dot(kd, dSb))
            dstate[h] = dS * c_decay + _dot_tn(q, dyq)
        drq = _rope_t(jnp.concatenate(dqs, axis=1), cs, sn, RET_QK // 2)
        drk = _rope_t(jnp.concatenate(dks, axis=1) * kmul, cs, sn, RET_QK // 2)
        dqk_ref[...] = jnp.concatenate([drq, drk], axis=1)
        dv_ref[...] = jnp.concatenate(dvs, axis=1)
        dg_ref[...] = jnp.concatenate(dgs, axis=1)

    blk512 = lambda c: pl.BlockSpec((BLK, 512), lambda t: (nb - 1 - t, c))
    tab = pl.BlockSpec((BLK, 128), lambda t: (nb - 1 - t, 0))
    sds = jax.ShapeDtypeStruct((L, 512), F32)
    return pl.pallas_call(
        body, name=name, grid=(nb,),
        in_specs=[blk512(C_RQ // 512), blk512(C_RV // 512), blk512(C_RG // 512), blk512(0),
                  pl.BlockSpec((1, RET_HEADS, RET_QK, RET_V), lambda t: (nb - 1 - t, 0, 0, 0)), blk512(2), tab, tab],
        out_specs=(blk512(0), blk512(0), blk512(0)), out_shape=(sds, sds, sds),
        scratch_shapes=[pltpu.VMEM((RET_HEADS, RET_QK, RET_V), F32)],
        compiler_params=pltpu.CompilerParams(dimension_semantics=("arbitrary",)),
    )(P, P, P, y, states, dmixed, cs, sn)


def _perm_w_in(w):
    pad = jnp.zeros(w.shape[:-1] + (N_INP - N_IN,), w.dtype)
    return jnp.concatenate([w[..., 0:1536], w[..., 2208:3744], w[..., 1536:2208], pad], axis=-1)


def _unperm_w_in(g):
    return jnp.concatenate([g[..., 0:1536], g[..., 3072:3744], g[..., 1536:3072]], axis=-1)


def _perm_w_uq(w):
    lead = w.shape[:-1]
    w5 = w.reshape(lead + (4, 2, 96))
    nope = w5[..., :64].reshape(lead + (4, 128))
    rope = w5[..., 64:].reshape(lead + (4, 64))
    return jnp.concatenate([nope, rope, jnp.zeros(lead + (4, 64), w.dtype)], axis=-1).reshape(lead + (1024,))


def _unperm_w_uq(g):
    lead = g.shape[:-1]
    g4 = g.reshape(lead + (4, 256))
    nope = g4[..., :128].reshape(lead + (4, 2, 64))
    rope = g4[..., 128:192].reshape(lead + (4, 2, 32))
    return jnp.concatenate([nope, rope], axis=-1).reshape(lead + (768,))


def _perm_w_ukv(w):
    lead = w.shape[:-1]
    w4 = w.reshape(lead + (8, 128))
    return jnp.concatenate([w4[..., :64].reshape(lead + (512,)), w4[..., 64:].reshape(lead + (512,))], axis=-1)


def _unperm_w_ukv(g):
    lead = g.shape[:-1]
    return jnp.concatenate([g[..., :512].reshape(lead + (8, 64)), g[..., 512:].reshape(lead + (8, 64))],
                           axis=-1).reshape(lead + (1024,))


def _rope_tables(L, half):
    pos = (jnp.arange(L) - N_PAD).astype(F32)
    inv = ROPE_THETA ** (-jnp.arange(half, dtype=F32) / half)
    ang = pos[:, None] * inv[None, :]
    cos, sin = jnp.cos(ang), jnp.sin(ang)
    reps = 128 // (2 * half)
    cs = jnp.tile(jnp.concatenate([cos, cos], axis=1), (1, reps))
    sn = jnp.tile(jnp.concatenate([-sin, sin], axis=1), (1, reps))
    return cs, sn


def _device_step(x, target, meta, ln_emb_g, ln_emb_b, w_in, q_norm, kv_norm, w_uq, w_ukv, w_out,
                 ln1_g, ln1_b, w_ff1, w_ff2, ln2_g, ln2_b):
    S = x.shape[0]
    L = S + BLK
    depth = w_in.shape[0]
    cs_m, sn_m = _rope_tables(L, MLA_ROPE // 2)
    cs_r, sn_r = _rope_tables(L, RET_QK // 2)
    hcat = jnp.concatenate([jnp.zeros((N_PAD, D_MODEL), F32), meta, x], axis=0)
    h, _ = ln_fwd(hcat, ln_emb_g, ln_emb_b, name="ln_emb_fwd")

    saved = []
    for l in range(depth):
        P = mm_nn(h, w_in[l], tn=768, name=f"in_proj_{l}")
        out_a, sbc = sb_fwd(P, name=f"sb_fwd_{l}")
        nq, nkv, KR = mla_pre_fwd(P, q_norm[l], kv_norm[l], cs_m, sn_m, name=f"mla_pre_fwd_{l}")
        Q = mm_nn(nq, w_uq[l], tn=512, name=f"uq_{l}")
        KV = mm_nn(nkv, w_ukv[l], tn=512, name=f"ukv_{l}", out_dtype=BF16)
        out_b, lse = mla_fwd(Q, KV, KR, cs_m, sn_m, name=f"mla_fwd_{l}")
        out_c, y, states = ret_fwd(P, cs_r, sn_r, name=f"ret_fwd_{l}")
        mixed = jnp.concatenate([out_a, out_b, out_c], axis=1)
        w_out_l = w_out[l].reshape(1, 1536, D_MODEL)
        mix = mm_nn(mixed, w_out_l, tn=512, tk=512, name=f"out_proj_{l}")
        h1, z1 = ln_fwd(mix, ln1_g[l], ln1_b[l], res=h, name=f"ln1_fwd_{l}")
        U = mm_nn(h1, w_ff1[l], tn=1024, name=f"ff1_{l}")
        w_ff2_l = w_ff2[l].reshape(1, D_FF, D_MODEL)
        mlp = mm_nn(U, w_ff2_l, tn=512, tk=1024, prologue="relu2", name=f"ff2_{l}")
        h2, z2 = ln_fwd(mlp, ln2_g[l], ln2_b[l], res=h1, name=f"ln2_fwd_{l}")
        saved.append((h, P, sbc, nq, nkv, KR, Q, KV, lse, y, states, mixed, z1, h1, U, z2))
        h = h2

    loss_t, dh = loss_fwd_bwd(h, target, name="loss")

    grads = {k: [None] * depth for k in ("w_in", "q_norm", "kv_norm", "w_uq", "w_ukv", "w_out", "ln1_g", "ln1_b",
                                           "w_ff1", "w_ff2", "ln2_g", "ln2_b")}
    for l in reversed(range(depth)):
        h_in, P, sbc, nq, nkv, KR, Q, KV, lse, y, states, mixed, z1, h1, U, z2 = saved[l]
        dz2, grads["ln2_g"][l], grads["ln2_b"][l] = ln_bwd(dh, z2, ln2_g[l], name=f"ln2_bwd_{l}")
        w_ff2_l = w_ff2[l].reshape(1, D_FF, D_MODEL)
        grads["w_ff2"][l] = mm_tn(U, dz2, shards=1, tko=512, tn=1024, prologue="relu2", name=f"ff2_dw_{l}").reshape(4, 1024, D_MODEL)
        dU = mm_nt(dz2, w_ff2_l, tn=1024, tko=1024, relu2grad=U, name=f"ff2_dx_{l}")
        grads["w_ff1"][l] = mm_tn(h1, dU, shards=4, tko=512, tn=1024, name=f"ff1_dw_{l}")
        dh1 = mm_nt(dU, w_ff1[l], tn=1024, tko=1024, axpy=(dz2, DN_ALPHA), name=f"ff1_dx_{l}")
        dz1, grads["ln1_g"][l], grads["ln1_b"][l] = ln_bwd(dh1, z1, ln1_g[l], name=f"ln1_bwd_{l}")
        w_out_l = w_out[l].reshape(1, 1536, D_MODEL)
        grads["w_out"][l] = mm_tn(mixed, dz1, shards=1, tko=512, tn=1024, name=f"out_dw_{l}").reshape(4, 384, D_MODEL)
        dmixed = mm_nt(dz1, w_out_l, tn=1024, tko=512, name=f"out_dx_{l}")
        d_rqk, d_rv, d_rg = ret_bwd(P, y, states, dmixed, cs_r, sn_r, name=f"ret_bwd_{l}")
        dQ, dKN, dV, dKR = mla_bwd(Q, KV, KR, cs_m, sn_m, mixed, dmixed, lse, name=f"mla_bwd_{l}")
        dKV = jnp.concatenate([dKN, dV], axis=1)
        grads["w_uq"][l] = mm_tn(nq, dQ, shards=1, tko=MLA_Q_LORA, tn=512, name=f"uq_dw_{l}")[0]
        grads["w_ukv"][l] = mm_tn(nkv, dKV, shards=1, tko=MLA_KV_LORA, tn=512, name=f"ukv_dw_{l}")[0]
        dnq = mm_nt(dQ, w_uq[l], tn=1024, tko=MLA_Q_LORA, name=f"uq_dx_{l}")
        dnkv = mm_nt(dKV, w_ukv[l], tn=1024, tko=MLA_KV_LORA, name=f"ukv_dx_{l}")
        d_lat, grads["q_norm"][l], grads["kv_norm"][l] = mla_pre_bwd(P, dnq, dnkv, dKR, q_norm[l], kv_norm[l], cs_m, sn_m,
                                                                     name=f"mla_pre_bwd_{l}")
        dq_sb, dk_sb, dv_sb = sb_bwd(P, sbc, dmixed, name=f"sb_bwd_{l}")
        dP = jnp.concatenate([dq_sb, dk_sb, dv_sb, d_rqk, d_rv, d_rg, d_lat], axis=1)
        grads["w_in"][l] = mm_tn(h_in, dP, shards=1, tko=512, tn=768, name=f"in_dw_{l}")[0]
        dh = mm_nt(dP, w_in[l], tn=768, tko=1024, axpy=(dz1, DN_ALPHA), name=f"in_dx_{l}")

    dhcat, dg_emb, db_emb = ln_bwd(dh, hcat, ln_emb_g, name="ln_emb_bwd")
    out = {k: jnp.stack(v) for k, v in grads.items()}
    out["ln_emb_g"], out["ln_emb_b"] = dg_emb, db_emb
    out["meta"] = dhcat[N_PAD:BLK]
    return loss_t[0, 0], dhcat[BLK:], out


MESH = pl.DeviceIdType.MESH
PEER_XOR = (2, 1, 3)
_HBM = pl.BlockSpec(memory_space=pltpu.HBM)


def _place():
    x, y, c = lax.axis_index("x"), lax.axis_index("y"), lax.axis_index("c")
    peers = [(1 - x, y, c), (x, 1 - y, c), (1 - x, 1 - y, c)]
    return x, y, c, 2 * x + y, peers, (x, y, 1 - c)


def gather_weight(w_shard, *, name):
    nl = w_shard.shape[0]
    hl = nl // 2

    def body(w_ref, out_ref, send_sems, recv_sems, local_sem):
        x, y, c, s0, peers, sibling = _place()
        mine = pltpu.make_async_copy(w_ref, out_ref.at[s0], local_sem)
        mine.start()

        def piece(s, half):
            return out_ref.at[s, pl.ds(half * hl, hl)]

        def copy(k, s, half, to, src=None):
            return pltpu.make_async_remote_copy(src_ref=piece(s, half) if src is None else src, dst_ref=piece(s, half),
                                                send_sem=send_sems.at[k], recv_sem=recv_sems.at[k],
                                                device_id=to, device_id_type=MESH)

        first = [copy(k, s0, c, peers[k], src=w_ref.at[pl.ds(c * hl, hl)]) for k in range(3)]
        for cp in first:
            cp.start()
        passed = [copy(3 + k, s0 ^ PEER_XOR[k], c, sibling) for k in range(3)]
        for k in range(3):
            copy(k, s0 ^ PEER_XOR[k], c, peers[k]).wait_recv()
            passed[k].start()
        for k in range(3):
            copy(3 + k, s0 ^ PEER_XOR[k], 1 - c, sibling).wait_recv()
        for cp in first + passed:
            cp.wait_send()
        mine.wait()

    return pl.pallas_call(
        body, name=name, in_specs=[_HBM], out_specs=_HBM,
        out_shape=jax.ShapeDtypeStruct((4,) + w_shard.shape, w_shard.dtype),
        scratch_shapes=[pltpu.SemaphoreType.DMA((6,)), pltpu.SemaphoreType.DMA((6,)), pltpu.SemaphoreType.DMA],
    )(w_shard)


def send_half_to_sibling(G, *, name):
    hl = G.shape[1] // 2

    def body(g_ref, out_ref, send_sem, recv_sem):
        x, y, c, s0, peers, sibling = _place()
        cp = pltpu.make_async_remote_copy(src_ref=g_ref.at[:, pl.ds((1 - c) * hl, hl)], dst_ref=out_ref,
                                          send_sem=send_sem, recv_sem=recv_sem, device_id=sibling, device_id_type=MESH)
        cp.start()
        cp.wait()

    return pl.pallas_call(
        body, name=name, in_specs=[_HBM], out_specs=_HBM,
        out_shape=jax.ShapeDtypeStruct((4, hl) + G.shape[2:], G.dtype),
        scratch_shapes=[pltpu.SemaphoreType.DMA, pltpu.SemaphoreType.DMA],
    )(G)


def scatter_to_chips(A, *, name):
    def body(a_ref, out_ref, send_sems, recv_sems):
        x, y, c, s0, peers, sibling = _place()
        copies = [pltpu.make_async_remote_copy(src_ref=a_ref.at[s0 ^ PEER_XOR[k]], dst_ref=out_ref.at[k],
                                               send_sem=send_sems.at[k], recv_sem=recv_sems.at[k],
                                               device_id=peers[k], device_id_type=MESH) for k in range(3)]
        for cp in copies:
            cp.start()
        for cp in copies:
            cp.wait()

    return pl.pallas_call(
        body, name=name, in_specs=[_HBM], out_specs=_HBM,
        out_shape=jax.ShapeDtypeStruct((3,) + A.shape[1:], A.dtype),
        scratch_shapes=[pltpu.SemaphoreType.DMA((3,)), pltpu.SemaphoreType.DMA((3,))],
    )(A)


def join_halves(Rh, *, name):
    hl = Rh.shape[0]

    def body(r_ref, out_ref, send_sem, recv_sem, local_sem):
        x, y, c, s0, peers, sibling = _place()
        mine = pltpu.make_async_copy(r_ref, out_ref.at[pl.ds(c * hl, hl)], local_sem)
        mine.start()
        cp = pltpu.make_async_remote_copy(src_ref=r_ref, dst_ref=out_ref.at[pl.ds(c * hl, hl)],
                                          send_sem=send_sem, recv_sem=recv_sem, device_id=sibling, device_id_type=MESH)
        cp.start()
        pltpu.make_async_remote_copy(src_ref=r_ref, dst_ref=out_ref.at[pl.ds((1 - c) * hl, hl)],
                                     send_sem=send_sem, recv_sem=recv_sem, device_id=sibling, device_id_type=MESH).wait_recv()
        cp.wait_send()
        mine.wait()

    return pl.pallas_call(
        body, name=name, in_specs=[_HBM], out_specs=_HBM,
        out_shape=jax.ShapeDtypeStruct((2 * hl,) + Rh.shape[1:], Rh.dtype),
        scratch_shapes=[pltpu.SemaphoreType.DMA, pltpu.SemaphoreType.DMA, pltpu.SemaphoreType.DMA],
    )(Rh)


def allgather8(xs, *, name, reduce):
    M, N = xs.shape

    def body(x_ref, out_ref, *rest):
        if reduce:
            all_ref, send_sems, recv_sems, local_sem = rest
        else:
            all_ref = out_ref
            send_sems, recv_sems, local_sem = rest
        x, y, c, s0, peers, sibling = _place()
        me = (x, y, c)
        chips = [(1 - x, y), (x, 1 - y), (1 - x, 1 - y)]

        def rows(px, py, pc):
            return all_ref.at[pl.ds((4 * px + 2 * py + pc) * M, M), :]

        def copy(k, block, to, src=None):
            return pltpu.make_async_remote_copy(src_ref=rows(*block) if src is None else src, dst_ref=rows(*block),
                                                send_sem=send_sems.at[k], recv_sem=recv_sems.at[k],
                                                device_id=to, device_id_type=MESH)

        mine = pltpu.make_async_copy(x_ref, rows(*me), local_sem)
        mine.start()
        first = [copy(0, me, sibling, src=x_ref)]
        first += [copy(1 + j, me, (*chip, c), src=x_ref) for j, chip in enumerate(chips)]
        for cp in first:
            cp.start()
        passed = [copy(4 + j, (*chip, c), sibling) for j, chip in enumerate(chips)]
        for j, chip in enumerate(chips):
            copy(1 + j, (*chip, c), me).wait_recv()
            passed[j].start()
        copy(0, sibling, me).wait_recv()
        for j, chip in enumerate(chips):
            copy(4 + j, (*chip, 1 - c), me).wait_recv()
        for cp in first + passed:
            cp.wait_send()
        mine.wait()
        if reduce:
            acc = all_ref[pl.ds(0, M), :]
            for d in range(1, 8):
                acc = acc + all_ref[pl.ds(d * M, M), :]
            out_ref[...] = acc

    vm = pl.BlockSpec(memory_space=pltpu.VMEM)
    scratch = [pltpu.SemaphoreType.DMA((7,)), pltpu.SemaphoreType.DMA((7,)), pltpu.SemaphoreType.DMA]
    if reduce:
        scratch = [pltpu.VMEM((8 * M, N), xs.dtype)] + scratch
    return pl.pallas_call(
        body, name=name, in_specs=[vm], out_specs=vm,
        out_shape=jax.ShapeDtypeStruct((M if reduce else 8 * M, N), xs.dtype), scratch_shapes=scratch,
    )(xs)


def add_halves(G, B, c, *, name):
    S, nl, R, C = G.shape
    hl = nl // 2
    tr = _pick(R, (512, 384, 256, 128))

    def body(c_ref, g_ref, b_ref, o_ref):
        o_ref[...] = g_ref[...] + b_ref[...]

    blk = (1, 1, tr, C)
    return pl.pallas_call(
        body, name=name,
        grid_spec=pltpu.PrefetchScalarGridSpec(
            num_scalar_prefetch=1, grid=(S, hl, R // tr),
            in_specs=[pl.BlockSpec(blk, lambda s, l, r, cr: (s, cr[0] * hl + l, r, 0)),
                      pl.BlockSpec(blk, lambda s, l, r, cr: (s, l, r, 0))],
            out_specs=pl.BlockSpec(blk, lambda s, l, r, cr: (s, l, r, 0))),
        out_shape=jax.ShapeDtypeStruct((S, hl, R, C), F32),
    )(jnp.reshape(c, (1,)).astype(jnp.int32), G, B)


def add_chips(A, B, s0, *, name):
    S, hl, R, C = A.shape
    tr = _pick(R, (512, 384, 256, 128))

    def body(s_ref, a_ref, b0_ref, b1_ref, b2_ref, o_ref):
        o_ref[...] = ((a_ref[0] + b0_ref[0]) + b1_ref[0]) + b2_ref[0]

    blk = (1, 1, tr, C)
    bspec = lambda k: pl.BlockSpec(blk, lambda l, r, sr: (k, l, r, 0))
    return pl.pallas_call(
        body, name=name,
        grid_spec=pltpu.PrefetchScalarGridSpec(
            num_scalar_prefetch=1, grid=(hl, R // tr),
            in_specs=[pl.BlockSpec(blk, lambda l, r, sr: (sr[0], l, r, 0)), bspec(0), bspec(1), bspec(2)],
            out_specs=pl.BlockSpec((1, tr, C), lambda l, r, sr: (l, r, 0))),
        out_shape=jax.ShapeDtypeStruct((hl, R, C), F32),
    )(jnp.reshape(s0, (1,)).astype(jnp.int32), A, B, B, B)


def reduce_scatter_weight(G, c, s0, *, tag):
    B = send_half_to_sibling(G, name=f"rs_sib_{tag}")
    A = add_halves(G, B, c, name=f"rs_add1_{tag}")
    Bc = scatter_to_chips(A, name=f"rs_chips_{tag}")
    Rh = add_chips(A, Bc, s0, name=f"rs_add2_{tag}")
    return join_halves(Rh, name=f"rs_join_{tag}")


def adamw(w, g, m, v, *, name):
    shp = w.shape
    C = shp[-1]
    R = int(np.prod(shp[:-1])) if len(shp) > 1 else 1
    tr = R
    for t in (512, 384, 256, 128):
        if R % t == 0:
            tr = t
            break

    def body(w_ref, g_ref, m_ref, v_ref, d_ref, nm_ref, nv_ref):
        gv = g_ref[...]
        mn = ADAM_B1 * m_ref[...] + (1.0 - ADAM_B1) * gv
        vn = ADAM_B2 * v_ref[...] + (1.0 - ADAM_B2) * jnp.square(gv)
        m_hat = mn / (1.0 - ADAM_B1 ** ADAM_STEP)
        v_hat = vn / (1.0 - ADAM_B2 ** ADAM_STEP)
        d_ref[...] = -ADAM_LR * (m_hat / (jnp.sqrt(v_hat) + ADAM_EPS) + ADAM_WD * w_ref[...])
        nm_ref[...] = mn
        nv_ref[...] = vn

    spec = pl.BlockSpec((tr, C), lambda i: (i, 0))
    sds = jax.ShapeDtypeStruct((R, C), F32)
    d, nm, nv = pl.pallas_call(body, name=name, grid=(R // tr,), in_specs=[spec] * 4, out_specs=(spec,) * 3,
                               out_shape=(sds,) * 3)(*(a.reshape(R, C) for a in (w, g, m, v)))
    return d.reshape(shp), nm.reshape(shp), nv.reshape(shp)


_SMALL = ("ln_emb_g", "ln_emb_b", "q_norm", "kv_norm", "ln1_g", "ln1_b", "ln2_g", "ln2_b", "meta")


def _pack_small(d):
    flat = jnp.concatenate([d[k].reshape(-1) for k in _SMALL])
    rows = -(-flat.shape[0] // 128)
    rows = -(-rows // 8) * 8
    flat = jnp.concatenate([flat, jnp.zeros((rows * 128 - flat.shape[0],), F32)])
    return flat.reshape(rows, 128)


def _unpack_small(p, shapes):
    flat = p.reshape(-1)
    out, o = {}, 0
    for k in _SMALL:
        n = int(np.prod(shapes[k]))
        out[k] = flat[o:o + n].reshape(shapes[k])
        o += n
    return out


def kernel(x, meta_tokens, ln_emb_g, ln_emb_b, w_in, mla_q_norm, mla_kv_norm, w_uq, w_ukv, w_out, ln1_g, ln1_b, w_ff1, w_ff2, ln2_g, ln2_b, loss_target, m_meta_tokens, m_ln_emb_g, m_ln_emb_b, m_w_in, m_mla_q_norm, m_mla_kv_norm, m_w_uq, m_w_ukv, m_w_out, m_ln1_g, m_ln1_b, m_w_ff1, m_w_ff2, m_ln2_g, m_ln2_b, v_meta_tokens, v_ln_emb_g, v_ln_emb_b, v_w_in, v_mla_q_norm, v_mla_kv_norm, v_w_uq, v_w_ukv, v_w_out, v_ln1_g, v_ln1_b, v_w_ff1, v_w_ff2, v_ln2_g, v_ln2_b):
    xi, yi, ci = lax.axis_index("x"), lax.axis_index("y"), lax.axis_index("c")
    s0 = 2 * xi + yi
    nl = w_in.shape[0]

    big = {"w_in": w_in, "w_uq": w_uq, "w_ukv": w_ukv, "w_out": w_out, "w_ff1": w_ff1, "w_ff2": w_ff2}
    full = {k: gather_weight(v.astype(BF16), name=f"ag_{k}") for k, v in big.items()}
    cols = lambda a: jnp.moveaxis(a, 0, 2).reshape(a.shape[1], a.shape[2], 4 * a.shape[3])
    k_w_in = _perm_w_in(cols(full["w_in"]))[:, None]
    k_w_uq = _perm_w_uq(cols(full["w_uq"]))[:, None]
    k_w_ukv = _perm_w_ukv(cols(full["w_ukv"]))[:, None]
    k_w_out = jnp.moveaxis(full["w_out"], 0, 1)
    k_w_ff1 = jnp.moveaxis(full["w_ff1"], 0, 1)
    k_w_ff2 = jnp.moveaxis(full["w_ff2"], 0, 1)
    meta_all = allgather8(meta_tokens, name="ag_meta", reduce=False)
    meta_full = jnp.concatenate([meta_all[32 * s:32 * s + N_META] for s in range(4)], axis=1)

    loss_part, grad_x, g = _device_step(x[0], loss_target[0], meta_full, ln_emb_g, ln_emb_b, k_w_in, mla_q_norm, mla_kv_norm,
                                        k_w_uq, k_w_ukv, k_w_out, ln1_g, ln1_b, k_w_ff1, k_w_ff2, ln2_g, ln2_b)
    loss = lax.psum(loss_part, ("x", "y", "c"))

    def col_shards(a):
        return jnp.moveaxis(a.reshape(a.shape[0], a.shape[1], 4, a.shape[2] // 4), 2, 0)

    G = {"w_in": col_shards(_unperm_w_in(g["w_in"])), "w_uq": col_shards(_unperm_w_uq(g["w_uq"])),
         "w_ukv": col_shards(_unperm_w_ukv(g["w_ukv"])), "w_out": jnp.moveaxis(g["w_out"], 1, 0),
         "w_ff1": jnp.moveaxis(g["w_ff1"], 1, 0), "w_ff2": jnp.moveaxis(g["w_ff2"], 1, 0)}
    gw = {k: reduce_scatter_weight(v, ci, s0, tag=k) for k, v in G.items()}

    small_shapes = {"ln_emb_g": (D_MODEL,), "ln_emb_b": (D_MODEL,), "q_norm": (nl, MLA_Q_LORA), "kv_norm": (nl, MLA_KV_LORA),
                    "ln1_g": (nl, D_MODEL), "ln1_b": (nl, D_MODEL), "ln2_g": (nl, D_MODEL), "ln2_b": (nl, D_MODEL),
                    "meta": (N_META, D_MODEL)}
    gs = _unpack_small(allgather8(_pack_small(g), name="ar_small", reduce=True), small_shapes)
    gw.update({"ln_emb_g": gs["ln_emb_g"], "ln_emb_b": gs["ln_emb_b"], "mla_q_norm": gs["q_norm"], "mla_kv_norm": gs["kv_norm"],
               "ln1_g": gs["ln1_g"], "ln1_b": gs["ln1_b"], "ln2_g": gs["ln2_g"], "ln2_b": gs["ln2_b"],
               "meta_tokens": lax.dynamic_slice_in_dim(gs["meta"], s0 * 256, 256, axis=1)})

    names = ["meta_tokens", "ln_emb_g", "ln_emb_b", "w_in", "mla_q_norm", "mla_kv_norm", "w_uq", "w_ukv", "w_out",
             "ln1_g", "ln1_b", "w_ff1", "w_ff2", "ln2_g", "ln2_b"]
    ws = [meta_tokens, ln_emb_g, ln_emb_b, w_in, mla_q_norm, mla_kv_norm, w_uq, w_ukv, w_out, ln1_g, ln1_b, w_ff1, w_ff2, ln2_g, ln2_b]
    ms = [m_meta_tokens, m_ln_emb_g, m_ln_emb_b, m_w_in, m_mla_q_norm, m_mla_kv_norm, m_w_uq, m_w_ukv, m_w_out, m_ln1_g, m_ln1_b, m_w_ff1, m_w_ff2, m_ln2_g, m_ln2_b]
    vs = [v_meta_tokens, v_ln_emb_g, v_ln_emb_b, v_w_in, v_mla_q_norm, v_mla_kv_norm, v_w_uq, v_w_ukv, v_w_out, v_ln1_g, v_ln1_b, v_w_ff1, v_w_ff2, v_ln2_g, v_ln2_b]
    deltas, new_m, new_v = [], [], []
    for n, w, m, v in zip(names, ws, ms, vs):
        w2 = w.reshape(1, -1) if w.ndim == 1 else w
        d, nm, nv = adamw(w2, gw[n].reshape(w2.shape), m.reshape(w2.shape), v.reshape(w2.shape), name=f"adamw_{n}")
        deltas.append(d.reshape(w.shape))
        new_m.append(nm.reshape(w.shape))
        new_v.append(nv.reshape(w.shape))
    grads_out = [gw[n].reshape(w.shape) for n, w in zip(names, ws)]
    return (loss, grad_x[None], *grads_out, *deltas, *new_m, *new_v)
```

```python
import functools
import math

import numpy as np
import jax
import jax.numpy as jnp
from jax import lax
from jax.experimental import pallas as pl
from jax.experimental.pallas import tpu as pltpu

F32 = jnp.float32
BF16 = jnp.bfloat16

D_MODEL = 1024
DEPTH = 4
N_META = 16
BLK = 128
N_PAD = 112
SB_HEADS = 8
MLA_HEADS = 8
MLA_NOPE = 64
MLA_ROPE = 32
MLA_V = 64
MLA_Q_LORA = 384
MLA_KV_LORA = 256
RET_HEADS = 4
RET_QK = 64
RET_V = 128
D_FF = 4 * D_MODEL
ROPE_THETA = 10000.0
LN_EPS = 1e-5
DN_ALPHA = (2 * DEPTH) ** 0.25
RET_GAMMA = tuple(1.0 - 2.0 ** (-5 - h) for h in range(RET_HEADS))
RET_LOG_G = tuple(float(np.log(np.float32(g))) for g in RET_GAMMA)
MLA_SCALE = (MLA_NOPE + MLA_ROPE) ** -0.5

ADAM_LR = 0.001
ADAM_B1 = 0.9
ADAM_B2 = 0.999
ADAM_EPS = 1e-08
ADAM_WD = 0.01
ADAM_STEP = 10

C_SBQ, C_SBK, C_SBV = 0, 512, 1024
C_RQ, C_RK, C_RV, C_RG = 1536, 1792, 2048, 2560
C_CQ, C_CKV, C_KR = 3072, 3456, 3712
N_IN = 3744
N_INP = 3840

NEG = -1e30


def _pick(n, cands):
    for t in cands:
        if n % t == 0:
            return t
    raise ValueError(f"no tile for {n} in {cands}")


def _row_tile(n):
    return _pick(n, (1056, 1024, 528, 512, 384, 256, 128))


def _dot(a, b):
    return jnp.dot(a, b, preferred_element_type=F32)


def _dot_nt(a, b):
    return lax.dot_general(a, b, (((1,), (1,)), ((), ())), preferred_element_type=F32)


def _dot_tn(a, b):
    return lax.dot_general(a, b, (((0,), (0,)), ((), ())), preferred_element_type=F32)


def mm_nn(a, b, *, tn, name, tk=None, prologue=None, axpy=None, out_dtype=F32):
    M, K = a.shape
    S, _, Ns = b.shape
    tm = _row_tile(M)
    tk = K if tk is None else tk
    npt = Ns // tn
    nk = K // tk
    alpha = None if axpy is None else axpy[1]

    def body(*refs):
        if axpy is None:
            a_ref, b_ref, o_ref, acc = refs
        else:
            a_ref, b_ref, e_ref, o_ref, acc = refs
        k = pl.program_id(2)

        @pl.when(k == 0)
        def _():
            acc[...] = jnp.zeros_like(acc)

        x = a_ref[...]
        if prologue == "relu2":
            x = jnp.square(jnp.maximum(x, 0.0))
        acc[...] += _dot(x.astype(BF16), b_ref[0])

        @pl.when(k == nk - 1)
        def _():
            r = acc[...]
            if axpy is not None:
                r = r + alpha * e_ref[...]
            o_ref[...] = r.astype(out_dtype)

    in_specs = [pl.BlockSpec((tm, tk), lambda i, j, k: (i, k)),
                pl.BlockSpec((1, tk, tn), lambda i, j, k: (j // npt, k, j % npt))]
    args = [a, b]
    if axpy is not None:
        in_specs.append(pl.BlockSpec((tm, tn), lambda i, j, k: (i, j)))
        args.append(axpy[0])
    return pl.pallas_call(
        body, name=name, grid=(M // tm, (S * Ns) // tn, nk), in_specs=in_specs,
        out_specs=pl.BlockSpec((tm, tn), lambda i, j, k: (i, j)),
        out_shape=jax.ShapeDtypeStruct((M, S * Ns), out_dtype),
        scratch_shapes=[pltpu.VMEM((tm, tn), F32)],
        compiler_params=pltpu.CompilerParams(dimension_semantics=("parallel", "parallel", "arbitrary")),
    )(*args)


def mm_nt(a, b, *, tn, tko, name, axpy=None, relu2grad=None):
    M, N = a.shape
    S, K, Ns = b.shape
    tm = _row_tile(M)
    npt = Ns // tn
    nn = N // tn
    alpha = None if axpy is None else axpy[1]

    def body(*refs):
        if axpy is None and relu2grad is None:
            a_ref, b_ref, o_ref, acc = refs
        else:
            a_ref, b_ref, e_ref, o_ref, acc = refs
        n = pl.program_id(2)

        @pl.when(n == 0)
        def _():
            acc[...] = jnp.zeros_like(acc)

        acc[...] += _dot_nt(a_ref[...].astype(BF16), b_ref[0])

        @pl.when(n == nn - 1)
        def _():
            r = acc[...]
            if axpy is not None:
                r = r + alpha * e_ref[...]
            if relu2grad is not None:
                r = r * (2.0 * jnp.maximum(e_ref[...], 0.0))
            o_ref[...] = r

    in_specs = [pl.BlockSpec((tm, tn), lambda i, j, n: (i, n)),
                pl.BlockSpec((1, tko, tn), lambda i, j, n: (n // npt, j, n % npt))]
    args = [a, b]
    extra = axpy[0] if axpy is not None else relu2grad
    if extra is not None:
        in_specs.append(pl.BlockSpec((tm, tko), lambda i, j, n: (i, j)))
        args.append(extra)
    return pl.pallas_call(
        body, name=name, grid=(M // tm, K // tko, nn), in_specs=in_specs,
        out_specs=pl.BlockSpec((tm, tko), lambda i, j, n: (i, j)),
        out_shape=jax.ShapeDtypeStruct((M, K), F32),
        scratch_shapes=[pltpu.VMEM((tm, tko), F32)],
        compiler_params=pltpu.CompilerParams(dimension_semantics=("parallel", "parallel", "arbitrary")),
    )(*args)


def mm_tn(a, g, *, shards, tko, tn, name, prologue=None):
    M, K = a.shape
    _, N = g.shape
    Ns = N // shards
    tm = _row_tile(M)
    npt = Ns // tn
    nm = M // tm

    def body(a_ref, g_ref, o_ref, acc):
        m = pl.program_id(2)

        @pl.when(m == 0)
        def _():
            acc[...] = jnp.zeros_like(acc)

        x = a_ref[...]
        if prologue == "relu2":
            x = jnp.square(jnp.maximum(x, 0.0))
        acc[...] += _dot_tn(x.astype(BF16), g_ref[...].astype(BF16))

        @pl.when(m == nm - 1)
        def _():
            o_ref[0] = acc[...]

    return pl.pallas_call(
        body, name=name, grid=(K // tko, N // tn, nm),
        in_specs=[pl.BlockSpec((tm, tko), lambda i, j, m: (m, i)),
                  pl.BlockSpec((tm, tn), lambda i, j, m: (m, j))],
        out_specs=pl.BlockSpec((1, tko, tn), lambda i, j, m: (j // npt, i, j % npt)),
        out_shape=jax.ShapeDtypeStruct((shards, K, Ns), F32),
        scratch_shapes=[pltpu.VMEM((tko, tn), F32)],
        compiler_params=pltpu.CompilerParams(dimension_semantics=("parallel", "parallel", "arbitrary")),
    )(a, g)


def _ln_stats(z):
    mu = jnp.mean(z, axis=-1, keepdims=True)
    zc = z - mu
    var = jnp.mean(jnp.square(zc), axis=-1, keepdims=True)
    r = lax.rsqrt(var + LN_EPS)
    return zc * r, r


def ln_fwd(x, g, b, *, name, res=None):
    L, Dm = x.shape
    tr = _row_tile(L)
    g2, b2 = g.reshape(1, Dm), b.reshape(1, Dm)

    def body(*refs):
        if res is None:
            x_ref, g_ref, b_ref, y_ref = refs
            z = x_ref[...]
        else:
            x_ref, r_ref, g_ref, b_ref, y_ref, z_ref = refs
            z = DN_ALPHA * r_ref[...] + x_ref[...]
            z_ref[...] = z
        xh, _ = _ln_stats(z)
        y_ref[...] = xh * g_ref[...] + b_ref[...]

    row = pl.BlockSpec((tr, Dm), lambda i: (i, 0))
    vec = pl.BlockSpec((1, Dm), lambda i: (0, 0))
    sds = jax.ShapeDtypeStruct((L, Dm), F32)
    if res is None:
        y = pl.pallas_call(body, name=name, grid=(L // tr,), in_specs=[row, vec, vec], out_specs=row, out_shape=sds)(x, g2, b2)
        return y, x
    return pl.pallas_call(body, name=name, grid=(L // tr,), in_specs=[row, row, vec, vec], out_specs=(row, row),
                          out_shape=(sds, sds))(x, res, g2, b2)


def ln_bwd(dy, z, g, *, name):
    L, Dm = z.shape
    tr = _row_tile(L)

    def body(dy_ref, z_ref, g_ref, dz_ref, dg_ref, db_ref):
        @pl.when(pl.program_id(0) == 0)
        def _():
            dg_ref[...] = jnp.zeros_like(dg_ref)
            db_ref[...] = jnp.zeros_like(db_ref)

        dyv = dy_ref[...]
        xh, r = _ln_stats(z_ref[...])
        dxh = dyv * g_ref[...]
        m1 = jnp.mean(dxh, axis=-1, keepdims=True)
        m2 = jnp.mean(dxh * xh, axis=-1, keepdims=True)
        dz_ref[...] = r * (dxh - m1 - xh * m2)
        dg_ref[...] += jnp.sum(dyv * xh, axis=0, keepdims=True)
        db_ref[...] += jnp.sum(dyv, axis=0, keepdims=True)

    row = pl.BlockSpec((tr, Dm), lambda i: (i, 0))
    vec = pl.BlockSpec((1, Dm), lambda i: (0, 0))
    return pl.pallas_call(
        body, name=name, grid=(L // tr,), in_specs=[row, row, vec], out_specs=(row, vec, vec),
        out_shape=(jax.ShapeDtypeStruct((L, Dm), F32), jax.ShapeDtypeStruct((1, Dm), F32), jax.ShapeDtypeStruct((1, Dm), F32)),
        compiler_params=pltpu.CompilerParams(dimension_semantics=("arbitrary",)),
    )(dy, z, g.reshape(1, Dm))


def loss_fwd_bwd(h, target, *, name):
    L, Dm = h.shape
    nb = L // BLK

    def body(h_ref, t_ref, l_ref, dh_ref):
        i = pl.program_id(0)

        @pl.when(i == 0)
        def _():
            l_ref[...] = jnp.zeros_like(l_ref)
            dh_ref[...] = jnp.zeros_like(dh_ref)

        @pl.when(i > 0)
        def _():
            e = h_ref[...] - t_ref[...]
            dh_ref[...] = e * (1.0 / Dm)
            part = jnp.sum(jnp.sum(jnp.square(e), axis=-1, keepdims=True) * (1.0 / Dm), axis=0, keepdims=True)
            l_ref[...] += 0.5 * part

    return pl.pallas_call(
        body, name=name, grid=(nb,),
        in_specs=[pl.BlockSpec((BLK, Dm), lambda i: (i, 0)),
                  pl.BlockSpec((BLK, Dm), lambda i: (jnp.maximum(i - 1, 0), 0))],
        out_specs=(pl.BlockSpec((8, 128), lambda i: (0, 0)), pl.BlockSpec((BLK, Dm), lambda i: (i, 0))),
        out_shape=(jax.ShapeDtypeStruct((8, 128), F32), jax.ShapeDtypeStruct((L, Dm), F32)),
        compiler_params=pltpu.CompilerParams(dimension_semantics=("arbitrary",)),
    )(h, target)


def _swap_half(x, half):
    ax = x.ndim - 1
    n = x.shape[ax]
    lane = lax.broadcasted_iota(jnp.int32, x.shape, ax)
    up = pltpu.roll(x, n - half, ax)
    dn = pltpu.roll(x, half, ax)
    return jnp.where((lane % (2 * half)) < half, up, dn)


def _rope(x, cs, sn, half):
    return x * cs + _swap_half(x, half) * sn


def _rope_t(dy, cs, sn, half):
    return dy * cs + _swap_half(dy * sn, half)


def _rms(x):
    r = lax.rsqrt(jnp.mean(jnp.square(x), axis=-1, keepdims=True) + LN_EPS)
    return x * r, r


def mla_pre_fwd(P, gq, gkv, cs, sn, *, name):
    L = P.shape[0]
    tr = _row_tile(L)

    def body(p_ref, gq_ref, gkv_ref, cs_ref, sn_ref, nq_ref, nkv_ref, kr_ref):
        cq = p_ref[:, 0:MLA_Q_LORA]
        ckv = p_ref[:, MLA_Q_LORA:MLA_Q_LORA + MLA_KV_LORA]
        kr = p_ref[:, 640:768]
        nq_ref[...] = (_rms(cq)[0] * gq_ref[...]).astype(BF16)
        nkv_ref[...] = (_rms(ckv)[0] * gkv_ref[...]).astype(BF16)
        krr = _rope(kr, cs_ref[...], sn_ref[...], MLA_ROPE // 2)
        kr_ref[...] = (krr + pltpu.roll(krr, MLA_ROPE, 1)).astype(BF16)

    return pl.pallas_call(
        body, name=name, grid=(L // tr,),
        in_specs=[pl.BlockSpec((tr, 768), lambda i: (i, C_CQ // 768)),
                  pl.BlockSpec((1, MLA_Q_LORA), lambda i: (0, 0)), pl.BlockSpec((1, MLA_KV_LORA), lambda i: (0, 0)),
                  pl.BlockSpec((tr, 128), lambda i: (i, 0)), pl.BlockSpec((tr, 128), lambda i: (i, 0))],
        out_specs=(pl.BlockSpec((tr, MLA_Q_LORA), lambda i: (i, 0)), pl.BlockSpec((tr, MLA_KV_LORA), lambda i: (i, 0)),
                   pl.BlockSpec((tr, 128), lambda i: (i, 0))),
        out_shape=(jax.ShapeDtypeStruct((L, MLA_Q_LORA), BF16), jax.ShapeDtypeStruct((L, MLA_KV_LORA), BF16),
                   jax.ShapeDtypeStruct((L, 128), BF16)),
    )(P, gq.reshape(1, -1), gkv.reshape(1, -1), cs, sn)


def mla_pre_bwd(P, dnq, dnkv, dkr, gq, gkv, cs, sn, *, name):
    L = P.shape[0]
    tr = _row_tile(L)

    def body(p_ref, dnq_ref, dnkv_ref, dkr_ref, gq_ref, gkv_ref, cs_ref, sn_ref, dp_ref, dgq_ref, dgkv_ref):
        @pl.when(pl.program_id(0) == 0)
        def _():
            dgq_ref[...] = jnp.zeros_like(dgq_ref)
            dgkv_ref[...] = jnp.zeros_like(dgkv_ref)

        def rms_bwd(x, dy, g_ref, dg_ref):
            xn, r = _rms(x)
            dxn = dy * g_ref[...]
            dg_ref[...] += jnp.sum(dy * xn, axis=0, keepdims=True)
            return r * (dxn - xn * jnp.mean(dxn * xn, axis=-1, keepdims=True))

        dp_ref[:, 0:MLA_Q_LORA] = rms_bwd(p_ref[:, 0:MLA_Q_LORA], dnq_ref[...], gq_ref, dgq_ref)
        dp_ref[:, MLA_Q_LORA:640] = rms_bwd(p_ref[:, MLA_Q_LORA:640], dnkv_ref[...], gkv_ref, dgkv_ref)
        d2 = dkr_ref[...]
        lane = lax.broadcasted_iota(jnp.int32, d2.shape, 1)
        dkr = jnp.where(lane < MLA_ROPE, d2 + pltpu.roll(d2, 128 - MLA_ROPE, 1), 0.0)
        dp_ref[:, 640:768] = _rope_t(dkr, cs_ref[...], sn_ref[...], MLA_ROPE // 2)

    return pl.pallas_call(
        body, name=name, grid=(L // tr,),
        in_specs=[pl.BlockSpec((tr, 768), lambda i: (i, C_CQ // 768)),
                  pl.BlockSpec((tr, MLA_Q_LORA), lambda i: (i, 0)), pl.BlockSpec((tr, MLA_KV_LORA), lambda i: (i, 0)),
                  pl.BlockSpec((tr, 128), lambda i: (i, 0)),
                  pl.BlockSpec((1, MLA_Q_LORA), lambda i: (0, 0)), pl.BlockSpec((1, MLA_KV_LORA), lambda i: (0, 0)),
                  pl.BlockSpec((tr, 128), lambda i: (i, 0)), pl.BlockSpec((tr, 128), lambda i: (i, 0))],
        out_specs=(pl.BlockSpec((tr, 768), lambda i: (i, 0)), pl.BlockSpec((1, MLA_Q_LORA), lambda i: (0, 0)),
                   pl.BlockSpec((1, MLA_KV_LORA), lambda i: (0, 0))),
        out_shape=(jax.ShapeDtypeStruct((L, 768), F32), jax.ShapeDtypeStruct((1, MLA_Q_LORA), F32),
                   jax.ShapeDtypeStruct((1, MLA_KV_LORA), F32)),
        compiler_params=pltpu.CompilerParams(dimension_semantics=("arbitrary",)),
    )(P, dnq, dnkv, dkr, gq.reshape(1, -1), gkv.reshape(1, -1), cs, sn)


def _tri(kind):
    r = lax.broadcasted_iota(jnp.int32, (BLK, BLK), 0)
    c = lax.broadcasted_iota(jnp.int32, (BLK, BLK), 1)
    t = ((r > c) if kind == "right" else (r < c)).astype(BF16)
    return jnp.concatenate([t, t], axis=0)


def _tri_sum(x, tt):
    hi = x.astype(BF16)
    lo = (x - hi.astype(F32)).astype(BF16)
    return _dot(jnp.concatenate([hi, lo], axis=1), tt)


def _sb_tile(q, k, i, j, tt_right, R):
    row = lax.broadcasted_iota(jnp.int32, (BLK, BLK), 0)
    col = lax.broadcasted_iota(jnp.int32, (BLK, BLK), 1)
    s_idx = j * BLK + col
    mask = (s_idx < i * BLK + row) & (s_idx >= N_PAD)
    z = _dot_nt(q, k)
    sp = jnp.maximum(z, 0.0) + jnp.log1p(jnp.exp(-jnp.abs(z)))
    lk = jnp.where(mask, -sp, 0.0)
    E = _tri_sum(lk, tt_right) + R
    return mask, z, sp, lk, E


def _old_sb_fwd(P, *, name):
    L = P.shape[0]
    nb = L // BLK

    def body(q_ref, k_ref, v_ref, o_ref, c_ref):
        i = pl.program_id(1)
        tt = _tri("right")
        lane = lax.broadcasted_iota(jnp.int32, (BLK, 128), 1)
        qs = [(q_ref[:, 64 * h:64 * h + 64] * 0.125).astype(BF16) for h in range(2)]

        def step(jj, carry):
            j = i - jj
            off = pl.multiple_of(j * BLK, BLK)
            kb = k_ref[pl.ds(off, BLK), :].astype(BF16)
            vb = v_ref[pl.ds(off, BLK), :].astype(BF16)
            out = []
            for h in range(2):
                o, R = carry[2 * h], carry[2 * h + 1]
                sl = slice(64 * h, 64 * h + 64)
                mask, z, sp, lk, E = _sb_tile(qs[h], kb[:, sl], i, j, tt, R)
                w = jnp.where(mask, jnp.exp(z - sp + E), 0.0)
                c_ref[h] = jnp.where(lane == j, R, c_ref[h])
                out += [o + _dot(w.astype(BF16), vb[:, sl]), R + jnp.sum(lk, axis=1, keepdims=True)]
            return tuple(out)

        c_ref[...] = jnp.zeros_like(c_ref)
        z0 = (jnp.zeros((BLK, 64), F32), jnp.zeros((BLK, 1), F32))
        res = lax.fori_loop(0, i + 1, step, z0 + z0)
        o_ref[...] = jnp.concatenate([res[0], res[2]], axis=1)

    return pl.pallas_call(
        body, name=name, grid=(SB_HEADS // 2, nb),
        in_specs=[pl.BlockSpec((BLK, 128), lambda hp, i: (i, C_SBQ // 128 + hp)),
                  pl.BlockSpec((L, 128), lambda hp, i: (0, C_SBK // 128 + hp)),
                  pl.BlockSpec((L, 128), lambda hp, i: (0, C_SBV // 128 + hp))],
        out_specs=(pl.BlockSpec((BLK, 128), lambda hp, i: (i, hp)), pl.BlockSpec((2, BLK, 128), lambda hp, i: (hp, i, 0))),
        out_shape=(jax.ShapeDtypeStruct((L, 512), F32), jax.ShapeDtypeStruct((SB_HEADS, L, 128), F32)),
        compiler_params=pltpu.CompilerParams(dimension_semantics=("parallel", "arbitrary")),
    )(P, P, P)


def _old_sb_bwd(P, carries, dmixed, *, name):
    L = P.shape[0]
    nb = L // BLK

    def body(q_ref, k_ref, v_ref, c_ref, do_ref, dq_ref, dk_ref, dv_ref):
        i = pl.program_id(1)

        @pl.when(i == 0)
        def _():
            dk_ref[...] = jnp.zeros_like(dk_ref)
            dv_ref[...] = jnp.zeros_like(dv_ref)

        tr = _tri("right")
        tl = _tri("left")
        lane = lax.broadcasted_iota(jnp.int32, (BLK, 128), 1)
        qs = [(q_ref[:, 64 * h:64 * h + 64] * 0.125).astype(BF16) for h in range(2)]
        dos = [do_ref[:, 64 * h:64 * h + 64].astype(BF16) for h in range(2)]

        def step(j, carry):
            off = pl.multiple_of(j * BLK, BLK)
            kb = k_ref[pl.ds(off, BLK), :].astype(BF16)
            vb = v_ref[pl.ds(off, BLK), :].astype(BF16)
            out, dks, dvs = [], [], []
            for h in range(2):
                dq, PL = carry[2 * h], carry[2 * h + 1]
                sl = slice(64 * h, 64 * h + 64)
                R = jnp.sum(jnp.where(lane == j, c_ref[h], 0.0), axis=1, keepdims=True)
                mask, z, sp, lk, E = _sb_tile(qs[h], kb[:, sl], i, j, tr, R)
                sig = jnp.exp(z - sp)
                w = jnp.where(mask, sig * jnp.exp(E), 0.0)
                dA = _dot_nt(dos[h], vb[:, sl]) * w
                Pp = _tri_sum(dA, tl) + PL
                dz = jnp.where(mask, dA - sig * (dA + Pp), 0.0).astype(BF16)
                dks.append(_dot_tn(dz, qs[h]))
                dvs.append(_dot_tn(w.astype(BF16), dos[h]))
                out += [dq + _dot(dz, kb[:, sl]), PL + jnp.sum(dA, axis=1, keepdims=True)]
            dk_ref[pl.ds(off, BLK), :] += jnp.concatenate(dks, axis=1)
            dv_ref[pl.ds(off, BLK), :] += jnp.concatenate(dvs, axis=1)
            return tuple(out)

        z0 = (jnp.zeros((BLK, 64), F32), jnp.zeros((BLK, 1), F32))
        res = lax.fori_loop(0, i + 1, step, z0 + z0)
        dq_ref[...] = jnp.concatenate([res[0], res[2]], axis=1) * 0.125

    blk = lambda c0: pl.BlockSpec((BLK, 128), lambda hp, i: (i, c0 + hp))
    full = lambda c0: pl.BlockSpec((L, 128), lambda hp, i: (0, c0 + hp))
    sds = jax.ShapeDtypeStruct((L, 512), F32)
    return pl.pallas_call(
        body, name=name, grid=(SB_HEADS // 2, nb),
        in_specs=[blk(C_SBQ // 128), full(C_SBK // 128), full(C_SBV // 128),
                  pl.BlockSpec((2, BLK, 128), lambda hp, i: (hp, i, 0)), blk(0)],
        out_specs=(blk(0), full(0), full(0)), out_shape=(sds, sds, sds),
        compiler_params=pltpu.CompilerParams(dimension_semantics=("parallel", "arbitrary")),
    )(P, P, P, carries, dmixed)


def _mla_mask(i, j):
    row = lax.broadcasted_iota(jnp.int32, (BLK, BLK), 0)
    col = lax.broadcasted_iota(jnp.int32, (BLK, BLK), 1)
    t_idx = i * BLK + row
    s_idx = j * BLK + col
    return (s_idx <= t_idx) & ((s_idx >= N_PAD) | (s_idx == t_idx))


def _mla_q(q_ref, cs_ref, sn_ref):
    qr = _rope(q_ref[:, 128:256], cs_ref[...], sn_ref[...], MLA_ROPE // 2)
    qn = [q_ref[:, 64 * h:64 * h + 64].astype(BF16) for h in range(2)]
    qrs = [qr[:, 32 * h:32 * h + 32].astype(BF16) for h in range(2)]
    return qn, qrs


def _old_mla_fwd(Q, KV, KR, cs, sn, *, name):
    L = Q.shape[0]
    nb = L // BLK

    def body(q_ref, kn_ref, v_ref, kr_ref, cs_ref, sn_ref, o_ref, lse_ref):
        i = pl.program_id(1)
        qn, qrs = _mla_q(q_ref, cs_ref, sn_ref)

        def step(j, carry):
            off = pl.multiple_of(j * BLK, BLK)
            knb = kn_ref[pl.ds(off, BLK), :]
            vb = v_ref[pl.ds(off, BLK), :]
            krb = kr_ref[pl.ds(off, BLK), 0:MLA_ROPE]
            mask = _mla_mask(i, j)
            out = []
            for h in range(2):
                m, l, acc = carry[3 * h], carry[3 * h + 1], carry[3 * h + 2]
                sl = slice(64 * h, 64 * h + 64)
                s = (_dot_nt(qn[h], knb[:, sl]) + _dot_nt(qrs[h], krb)) * MLA_SCALE
                s = jnp.where(mask, s, NEG)
                m_new = jnp.maximum(m, jnp.max(s, axis=1, keepdims=True))
                a = jnp.exp(m - m_new)
                p = jnp.exp(s - m_new)
                out += [m_new, a * l + jnp.sum(p, axis=1, keepdims=True), a * acc + _dot(p.astype(BF16), vb[:, sl])]
            return tuple(out)

        z0 = (jnp.full((BLK, 1), NEG, F32), jnp.zeros((BLK, 1), F32), jnp.zeros((BLK, 64), F32))
        res = lax.fori_loop(0, i + 1, step, z0 + z0)
        o_ref[...] = jnp.concatenate([res[2] / res[1], res[5] / res[4]], axis=1)
        lane = lax.broadcasted_iota(jnp.int32, (BLK, 128), 1)
        lse0 = res[0] + jnp.log(res[1])
        lse1 = res[3] + jnp.log(res[4])
        lse_ref[0] = jnp.where(lane == 0, lse0, jnp.where(lane == 1, lse1, 0.0))

    return pl.pallas_call(
        body, name=name, grid=(MLA_HEADS // 2, nb),
        in_specs=[pl.BlockSpec((BLK, 256), lambda hp, i: (i, hp)),
                  pl.BlockSpec((L, 128), lambda hp, i: (0, hp)),
                  pl.BlockSpec((L, 128), lambda hp, i: (0, 4 + hp)),
                  pl.BlockSpec((L, 128), lambda hp, i: (0, 0)),
                  pl.BlockSpec((BLK, 128), lambda hp, i: (i, 0)), pl.BlockSpec((BLK, 128), lambda hp, i: (i, 0))],
        out_specs=(pl.BlockSpec((BLK, 128), lambda hp, i: (i, hp)), pl.BlockSpec((1, BLK, 128), lambda hp, i: (hp, i, 0))),
        out_shape=(jax.ShapeDtypeStruct((L, 512), F32), jax.ShapeDtypeStruct((4, L, 128), F32)),
        compiler_params=pltpu.CompilerParams(dimension_semantics=("parallel", "arbitrary")),
    )(Q, KV, KV, KR, cs, sn)


def _old_mla_bwd(Q, KV, KR, cs, sn, mixed, dmixed, lse, *, name):
    L = Q.shape[0]
    nb = L // BLK

    def body(q_ref, kn_ref, v_ref, kr_ref, cs_ref, sn_ref, o_ref, do_ref, lse_ref, dq_ref, dkn_ref, dv_ref, dkr_ref):
        hp = pl.program_id(0)
        i = pl.program_id(1)

        @pl.when(i == 0)
        def _():
            dkn_ref[...] = jnp.zeros_like(dkn_ref)
            dv_ref[...] = jnp.zeros_like(dv_ref)

        @pl.when((i == 0) & (hp == 0))
        def _():
            dkr_ref[...] = jnp.zeros_like(dkr_ref)

        qn, qrs = _mla_q(q_ref, cs_ref, sn_ref)
        dos, dd, lses = [], [], []
        for h in range(2):
            sl = slice(64 * h, 64 * h + 64)
            d = do_ref[:, sl]
            dos.append(d.astype(BF16))
            dd.append(jnp.sum(d * o_ref[:, sl], axis=1, keepdims=True))
            lses.append(lse_ref[0, :, h:h + 1])

        def step(j, carry):
            off = pl.multiple_of(j * BLK, BLK)
            knb = kn_ref[pl.ds(off, BLK), :]
            vb = v_ref[pl.ds(off, BLK), :]
            krb = kr_ref[pl.ds(off, BLK), 0:MLA_ROPE]
            mask = _mla_mask(i, j)
            out, dkns, dvs = [], [], []
            dkr = jnp.zeros((BLK, MLA_ROPE), F32)
            for h in range(2):
                dqn, dqr = carry[2 * h], carry[2 * h + 1]
                sl = slice(64 * h, 64 * h + 64)
                s = (_dot_nt(qn[h], knb[:, sl]) + _dot_nt(qrs[h], krb)) * MLA_SCALE
                p = jnp.where(mask, jnp.exp(s - lses[h]), 0.0)
                dp = _dot_nt(dos[h], vb[:, sl])
                ds = (p * (dp - dd[h]) * MLA_SCALE).astype(BF16)
                dkns.append(_dot_tn(ds, qn[h]))
                dvs.append(_dot_tn(p.astype(BF16), dos[h]))
                dkr = dkr + _dot_tn(ds, qrs[h])
                out += [dqn + _dot(ds, knb[:, sl]), dqr + _dot(ds, krb)]
            dkn_ref[pl.ds(off, BLK), :] += jnp.concatenate(dkns, axis=1)
            dv_ref[pl.ds(off, BLK), :] += jnp.concatenate(dvs, axis=1)
            dkr_ref[pl.ds(off, BLK), :] += jnp.concatenate([dkr, jnp.zeros((BLK, 128 - MLA_ROPE), F32)], axis=1)
            return tuple(out)

        z0 = (jnp.zeros((BLK, 64), F32), jnp.zeros((BLK, MLA_ROPE), F32))
        res = lax.fori_loop(0, i + 1, step, z0 + z0)
        dqr = jnp.concatenate([res[1], res[3], jnp.zeros((BLK, 64), F32)], axis=1)
        dq_ref[...] = jnp.concatenate([res[0], res[2], _rope_t(dqr, cs_ref[...], sn_ref[...], MLA_ROPE // 2)], axis=1)

    blk = lambda c0: pl.BlockSpec((BLK, 128), lambda hp, i: (i, c0 + hp))
    full = lambda c0: pl.BlockSpec((L, 128), lambda hp, i: (0, c0 + hp))
    tab = pl.BlockSpec((BLK, 128), lambda hp, i: (i, 0))
    return pl.pallas_call(
        body, name=name, grid=(MLA_HEADS // 2, nb),
        in_specs=[pl.BlockSpec((BLK, 256), lambda hp, i: (i, hp)), full(0), full(4),
                  pl.BlockSpec((L, 128), lambda hp, i: (0, 0)), tab, tab, blk(4), blk(4),
                  pl.BlockSpec((1, BLK, 128), lambda hp, i: (hp, i, 0))],
        out_specs=(pl.BlockSpec((BLK, 256), lambda hp, i: (i, hp)), full(0), full(0),
                   pl.BlockSpec((L, 128), lambda hp, i: (0, 0))),
        out_shape=(jax.ShapeDtypeStruct((L, 1024), F32), jax.ShapeDtypeStruct((L, 512), F32),
                   jax.ShapeDtypeStruct((L, 512), F32), jax.ShapeDtypeStruct((L, 128), F32)),
        compiler_params=pltpu.CompilerParams(dimension_semantics=("arbitrary", "arbitrary")),
    )(Q, KV, KV, KR, cs, sn, mixed, dmixed, lse)


def _tq(L):
    return 384 if L % 384 == 0 else BLK


def _softplus(z):
    return jnp.maximum(z, 0.0) + jnp.log(1.0 + jnp.exp(-jnp.abs(z)))


def _head_split(x, first):
    zero = jnp.zeros_like(x)
    return jnp.where(first, x, zero), jnp.where(first, zero, x)


def _sb_mask(I, j, tq):
    row = lax.broadcasted_iota(jnp.int32, (tq, BLK), 0)
    col = lax.broadcasted_iota(jnp.int32, (tq, BLK), 1)
    s_idx = j * BLK + col
    return (s_idx < I * tq + row) & (s_idx >= N_PAD)


def sb_fwd(P, *, name):
    L = P.shape[0]
    tq = _tq(L)
    nd = tq // BLK

    def body(q_ref, k_ref, v_ref, o_ref, c_ref):
        I = pl.program_id(1)
        tt = _tri("right")
        lane_q = lax.broadcasted_iota(jnp.int32, (tq, 128), 1)
        first_k = lax.broadcasted_iota(jnp.int32, (BLK, 128), 1) < 64
        qm = [x.astype(BF16) for x in _head_split(q_ref[...] * 0.125, lane_q < 64)]
        c_ref[...] = jnp.zeros_like(c_ref)

        def tile(j, carry, masked):
            o, R = carry[0], carry[1:]
            off = pl.multiple_of(j * BLK, BLK)
            kb = k_ref[pl.ds(off, BLK), :].astype(BF16)
            vcat = jnp.concatenate(_head_split(v_ref[pl.ds(off, BLK), :].astype(BF16), first_k), axis=0)
            mask = _sb_mask(I, j, tq) if masked else None
            ws, Rn = [], []
            for h in range(2):
                z = _dot_nt(qm[h], kb)
                sp = _softplus(z)
                spm = jnp.where(mask, sp, 0.0) if masked else sp
                w = jnp.exp(z - sp - _tri_sum(spm, tt) + R[h])
                if masked:
                    w = jnp.where(mask, w, 0.0)
                c_ref[h] = jnp.where(lane_q == j, R[h], c_ref[h])
                ws.append(w.astype(BF16))
                Rn.append(R[h] - jnp.sum(spm, axis=1, keepdims=True))
            return (o + _dot(jnp.concatenate(ws, axis=1), vcat), Rn[0], Rn[1])

        carry = (jnp.zeros((tq, 128), F32), jnp.zeros((tq, 1), F32), jnp.zeros((tq, 1), F32))
        carry = lax.fori_loop(0, nd, lambda t, c: tile(I * nd + nd - 1 - t, c, True), carry)
        carry = lax.fori_loop(0, jnp.maximum(I * nd - 1, 0), lambda t, c: tile(I * nd - 1 - t, c, False), carry)
        carry = lax.fori_loop(0, jnp.minimum(I, 1), lambda t, c: tile(0, c, True), carry)
        o_ref[...] = carry[0]

    return pl.pallas_call(
        body, name=name, grid=(SB_HEADS // 2, L // tq),
        in_specs=[pl.BlockSpec((tq, 128), lambda hp, i: (i, C_SBQ // 128 + hp)),
                  pl.BlockSpec((L, 128), lambda hp, i: (0, C_SBK // 128 + hp)),
                  pl.BlockSpec((L, 128), lambda hp, i: (0, C_SBV // 128 + hp))],
        out_specs=(pl.BlockSpec((tq, 128), lambda hp, i: (i, hp)), pl.BlockSpec((2, tq, 128), lambda hp, i: (hp, i, 0))),
        out_shape=(jax.ShapeDtypeStruct((L, 512), F32), jax.ShapeDtypeStruct((SB_HEADS, L, 128), F32)),
        compiler_params=pltpu.CompilerParams(dimension_semantics=("parallel", "arbitrary")),
    )(P, P, P)


def sb_bwd(P, carries, dmixed, *, name):
    L = P.shape[0]
    tq = _tq(L)
    nd = tq // BLK

    def body(q_ref, k_ref, v_ref, c_ref, do_ref, dq_ref, dk_ref, dv_ref):
        I = pl.program_id(1)

        @pl.when(I == 0)
        def _():
            dk_ref[...] = jnp.zeros_like(dk_ref)
            dv_ref[...] = jnp.zeros_like(dv_ref)

        tr = _tri("right")
        tl = _tri("left")
        lane_q = lax.broadcasted_iota(jnp.int32, (tq, 128), 1)
        first_k = lax.broadcasted_iota(jnp.int32, (BLK, 128), 1) < 64
        qm = [x.astype(BF16) for x in _head_split(q_ref[...] * 0.125, lane_q < 64)]
        dom = [x.astype(BF16) for x in _head_split(do_ref[...], lane_q < 64)]
        qcat = jnp.concatenate(qm, axis=0)
        docat = jnp.concatenate(dom, axis=0)

        def tile(j, carry, masked):
            dq, PL = carry[0], carry[1:]
            off = pl.multiple_of(j * BLK, BLK)
            kb = k_ref[pl.ds(off, BLK), :].astype(BF16)
            vb = v_ref[pl.ds(off, BLK), :].astype(BF16)
            kcat = jnp.concatenate(_head_split(kb, first_k), axis=0)
            mask = _sb_mask(I, j, tq) if masked else None
            dzs, wsb, PLn = [], [], []
            for h in range(2):
                R = jnp.sum(jnp.where(lane_q == j, c_ref[h], 0.0), axis=1, keepdims=True)
                z = _dot_nt(qm[h], kb)
                sp = _softplus(z)
                spm = jnp.where(mask, sp, 0.0) if masked else sp
                sig = jnp.exp(z - sp)
                w = sig * jnp.exp(R - _tri_sum(spm, tr))
                if masked:
                    w = jnp.where(mask, w, 0.0)
                dA = _dot_nt(dom[h], vb) * w
                dz = dA - sig * (dA + _tri_sum(dA, tl) + PL[h])
                if masked:
                    dz = jnp.where(mask, dz, 0.0)
                dzs.append(dz.astype(BF16))
                wsb.append(w.astype(BF16))
                PLn.append(PL[h] + jnp.sum(dA, axis=1, keepdims=True))
            dk_ref[pl.ds(off, BLK), :] += _dot_tn(jnp.concatenate(dzs, axis=0), qcat)
            dv_ref[pl.ds(off, BLK), :] += _dot_tn(jnp.concatenate(wsb, axis=0), docat)
            return (dq + _dot(jnp.concatenate(dzs, axis=1), kcat), PLn[0], PLn[1])

        carry = (jnp.zeros((tq, 128), F32), jnp.zeros((tq, 1), F32), jnp.zeros((tq, 1), F32))
        carry = lax.fori_loop(0, jnp.minimum(I, 1), lambda t, c: tile(0, c, True), carry)
        carry = lax.fori_loop(1, jnp.maximum(I * nd, 1), lambda j, c: tile(j, c, False), carry)
        carry = lax.fori_loop(0, nd, lambda t, c: tile(I * nd + t, c, True), carry)
        dq_ref[...] = carry[0] * 0.125

    blk = lambda c0: pl.BlockSpec((tq, 128), lambda hp, i: (i, c0 + hp))
    full = lambda c0: pl.BlockSpec((L, 128), lambda hp, i: (0, c0 + hp))
    sds = jax.ShapeDtypeStruct((L, 512), F32)
    return pl.pallas_call(
        body, name=name, grid=(SB_HEADS // 2, L // tq),
        in_specs=[blk(C_SBQ // 128), full(C_SBK // 128), full(C_SBV // 128),
                  pl.BlockSpec((2, tq, 128), lambda hp, i: (hp, i, 0)), blk(0)],
        out_specs=(blk(0), full(0), full(0)), out_shape=(sds, sds, sds),
        compiler_params=pltpu.CompilerParams(dimension_semantics=("parallel", "arbitrary")),
    )(P, P, P, carries, dmixed)


def _mla_mask2(I, j, tq):
    row = lax.broadcasted_iota(jnp.int32, (tq, tq), 0)
    col = lax.broadcasted_iota(jnp.int32, (tq, tq), 1)
    t_idx = I * tq + row
    s_idx = j * tq + col
    return (s_idx <= t_idx) & ((s_idx >= N_PAD) | (s_idx == t_idx))


def _mla_qcat(q_ref, cs_ref, sn_ref, lane_q):
    qn = q_ref[:, 0:128]
    qr = _rope(q_ref[:, 128:256], cs_ref[...], sn_ref[...], MLA_ROPE // 2)
    zero = jnp.zeros_like(qn)
    r0 = lane_q < MLA_ROPE
    r1 = (lane_q >= MLA_ROPE) & (lane_q < 2 * MLA_ROPE)
    n0, n1 = _head_split(qn, lane_q < 64)
    return [jnp.concatenate([n0, jnp.where(r0, qr, zero)], axis=1).astype(BF16),
            jnp.concatenate([n1, jnp.where(r1, qr, zero)], axis=1).astype(BF16)]


def mla_fwd(Q, KV, KR, cs, sn, *, name):
    L = Q.shape[0]
    tq = _tq(L)

    def body(q_ref, kn_ref, v_ref, kr_ref, cs_ref, sn_ref, o_ref, lse_ref):
        I = pl.program_id(1)
        lane_q = lax.broadcasted_iota(jnp.int32, (tq, 128), 1)
        first_q = lane_q < 64
        first_k = lax.broadcasted_iota(jnp.int32, (tq, 128), 1) < 64
        qcat = _mla_qcat(q_ref, cs_ref, sn_ref, lane_q)

        def tile(j, carry, masked):
            acc, ml = carry[0], carry[1:]
            off = pl.multiple_of(j * tq, tq)
            kcat = jnp.concatenate([kn_ref[pl.ds(off, tq), :], kr_ref[pl.ds(off, tq), :]], axis=1)
            vcat = jnp.concatenate(_head_split(v_ref[pl.ds(off, tq), :], first_k), axis=0)
            mask = _mla_mask2(I, j, tq) if masked else None
            ps, al, out = [], [], []
            for h in range(2):
                m, l = ml[2 * h], ml[2 * h + 1]
                s = _dot_nt(qcat[h], kcat) * MLA_SCALE
                if masked:
                    s = jnp.where(mask, s, NEG)
                m_new = jnp.maximum(m, jnp.max(s, axis=1, keepdims=True))
                a = jnp.exp(m - m_new)
                p = jnp.exp(s - m_new)
                ps.append(p.astype(BF16))
                al.append(a)
                out += [m_new, a * l + jnp.sum(p, axis=1, keepdims=True)]
            acc = acc * jnp.where(first_q, al[0], al[1]) + _dot(jnp.concatenate(ps, axis=1), vcat)
            return (acc,) + tuple(out)

        ml0 = (jnp.full((tq, 1), NEG, F32), jnp.zeros((tq, 1), F32))
        carry = (jnp.zeros((tq, 128), F32),) + ml0 + ml0
        carry = lax.fori_loop(0, jnp.minimum(I, 1), lambda t, c: tile(0, c, True), carry)
        carry = lax.fori_loop(1, jnp.maximum(I, 1), lambda j, c: tile(j, c, False), carry)
        carry = tile(I, carry, True)
        acc, m0, l0, m1, l1 = carry
        o_ref[...] = acc / jnp.where(first_q, l0, l1)
        lse_ref[0] = jnp.where(lane_q == 0, m0 + jnp.log(l0), jnp.where(lane_q == 1, m1 + jnp.log(l1), 0.0))

    return pl.pallas_call(
        body, name=name, grid=(MLA_HEADS // 2, L // tq),
        in_specs=[pl.BlockSpec((tq, 256), lambda hp, i: (i, hp)),
                  pl.BlockSpec((L, 128), lambda hp, i: (0, hp)),
                  pl.BlockSpec((L, 128), lambda hp, i: (0, 4 + hp)),
                  pl.BlockSpec((L, 128), lambda hp, i: (0, 0)),
                  pl.BlockSpec((tq, 128), lambda hp, i: (i, 0)), pl.BlockSpec((tq, 128), lambda hp, i: (i, 0))],
        out_specs=(pl.BlockSpec((tq, 128), lambda hp, i: (i, hp)), pl.BlockSpec((1, tq, 128), lambda hp, i: (hp, i, 0))),
        out_shape=(jax.ShapeDtypeStruct((L, 512), F32), jax.ShapeDtypeStruct((4, L, 128), F32)),
        compiler_params=pltpu.CompilerParams(dimension_semantics=("parallel", "arbitrary")),
    )(Q, KV, KV, KR, cs, sn)


def mla_bwd(Q, KV, KR, cs, sn, mixed, dmixed, lse, *, name):
    L = Q.shape[0]
    tq = _tq(L)

    def body(q_ref, kn_ref, v_ref, kr_ref, cs_ref, sn_ref, o_ref, do_ref, lse_ref, dq_ref, dkn_ref, dv_ref, dkr_ref):
        hp = pl.program_id(0)
        I = pl.program_id(1)

        @pl.when(I == 0)
        def _():
            dkn_ref[...] = jnp.zeros_like(dkn_ref)
            dv_ref[...] = jnp.zeros_like(dv_ref)

        @pl.when((I == 0) & (hp == 0))
        def _():
            dkr_ref[...] = jnp.zeros_like(dkr_ref)

        lane_q = lax.broadcasted_iota(jnp.int32, (tq, 128), 1)
        first_q = lane_q < 64
        lane_k = lax.broadcasted_iota(jnp.int32, (tq, 256), 1)
        sel0 = (lane_k < 64) | ((lane_k >= 128) & (lane_k < 128 + MLA_ROPE))
        sel1 = ((lane_k >= 64) & (lane_k < 128)) | ((lane_k >= 128 + MLA_ROPE) & (lane_k < 128 + 2 * MLA_ROPE))
        qcat = _mla_qcat(q_ref, cs_ref, sn_ref, lane_q)
        qq = jnp.concatenate(qcat, axis=0)
        do = do_ref[...]
        prod = do * o_ref[...]
        dd = [jnp.sum(jnp.where(first_q, prod, 0.0), axis=1, keepdims=True),
              jnp.sum(jnp.where(first_q, 0.0, prod), axis=1, keepdims=True)]
        dom = [x.astype(BF16) for x in _head_split(do, first_q)]
        docat = jnp.concatenate(dom, axis=0)
        lses = [lse_ref[0, :, 0:1], lse_ref[0, :, 1:2]]

        def tile(j, dq, masked):
            off = pl.multiple_of(j * tq, tq)
            kcat = jnp.concatenate([kn_ref[pl.ds(off, tq), :], kr_ref[pl.ds(off, tq), :]], axis=1)
            vb = v_ref[pl.ds(off, tq), :]
            zero = jnp.zeros_like(kcat)
            kk = jnp.concatenate([jnp.where(sel0, kcat, zero), jnp.where(sel1, kcat, zero)], axis=0)
            mask = _mla_mask2(I, j, tq) if masked else None
            dss, pbs = [], []
            for h in range(2):
                s = _dot_nt(qcat[h], kcat) * MLA_SCALE
                p = jnp.exp(s - lses[h])
                if masked:
                    p = jnp.where(mask, p, 0.0)
                dp = _dot_nt(dom[h], vb)
                dss.append((p * (dp - dd[h]) * MLA_SCALE).astype(BF16))
                pbs.append(p.astype(BF16))
            dkc = _dot_tn(jnp.concatenate(dss, axis=0), qq)
            dkn_ref[pl.ds(off, tq), :] += dkc[:, 0:128]
            dkr_ref[pl.ds(off, tq), :] += dkc[:, 128:256]
            dv_ref[pl.ds(off, tq), :] += _dot_tn(jnp.concatenate(pbs, axis=0), docat)
            return dq + _dot(jnp.concatenate(dss, axis=1), kk)

        dq = jnp.zeros((tq, 256), F32)
        dq = lax.fori_loop(0, jnp.minimum(I, 1), lambda t, c: tile(0, c, True), dq)
        dq = lax.fori_loop(1, jnp.maximum(I, 1), lambda j, c: tile(j, c, False), dq)
        dq = tile(I, dq, True)
        dq_ref[:, 0:128] = dq[:, 0:128]
        dq_ref[:, 128:256] = _rope_t(dq[:, 128:256], cs_ref[...], sn_ref[...], MLA_ROPE // 2)

    blk = lambda c0: pl.BlockSpec((tq, 128), lambda hp, i: (i, c0 + hp))
    full = lambda c0: pl.BlockSpec((L, 128), lambda hp, i: (0, c0 + hp))
    tab = pl.BlockSpec((tq, 128), lambda hp, i: (i, 0))
    return pl.pallas_call(
        body, name=name, grid=(MLA_HEADS // 2, L // tq),
        in_specs=[pl.BlockSpec((tq, 256), lambda hp, i: (i, hp)), full(0), full(4),
                  pl.BlockSpec((L, 128), lambda hp, i: (0, 0)), tab, tab, blk(4), blk(4),
                  pl.BlockSpec((1, tq, 128), lambda hp, i: (hp, i, 0))],
        out_specs=(pl.BlockSpec((tq, 256), lambda hp, i: (i, hp)), full(0), full(0),
                   pl.BlockSpec((L, 128), lambda hp, i: (0, 0))),
        out_shape=(jax.ShapeDtypeStruct((L, 1024), F32), jax.ShapeDtypeStruct((L, 512), F32),
                   jax.ShapeDtypeStruct((L, 512), F32), jax.ShapeDtypeStruct((L, 128), F32)),
        compiler_params=pltpu.CompilerParams(dimension_semantics=("arbitrary", "arbitrary")),
    )(Q, KV, KV, KR, cs, sn, mixed, dmixed, lse)


def _ret_decay(h):
    lg = RET_LOG_G[h]
    r = lax.broadcasted_iota(jnp.int32, (BLK, BLK), 0)
    c = lax.broadcasted_iota(jnp.int32, (BLK, BLK), 1)
    diff = (r - c).astype(F32)
    d_in = jnp.where(diff >= 0, jnp.exp(jnp.maximum(diff, 0.0) * lg), 0.0)
    idx = lax.broadcasted_iota(jnp.int32, (BLK, 1), 0).astype(F32)
    q_decay = jnp.exp((idx + 1.0) * lg)
    k_decay = jnp.exp((BLK - 1.0 - idx) * lg)
    c_decay = math.exp(BLK * lg)
    return d_in, q_decay, k_decay, c_decay


def _ret_qk(qk_ref, cs_ref, sn_ref, n):
    cs = jnp.concatenate([cs_ref[...]] * 2, axis=1)
    sn = jnp.concatenate([sn_ref[...]] * 2, axis=1)
    rq = _rope(qk_ref[:, 0:256], cs, sn, RET_QK // 2)
    row = n * BLK + lax.broadcasted_iota(jnp.int32, (BLK, 256), 0)
    kmul = jnp.where(row >= N_PAD, 0.125, 0.0)
    rk = _rope(qk_ref[:, 256:512], cs, sn, RET_QK // 2) * kmul
    return rq, rk, cs, sn, kmul


def _head_norm(y):
    mu = jnp.mean(y, axis=-1, keepdims=True)
    yc = y - mu
    r = lax.rsqrt(jnp.mean(jnp.square(yc), axis=-1, keepdims=True) + LN_EPS)
    return yc * r, r


def ret_fwd(P, cs, sn, *, name):
    L = P.shape[0]
    nb = L // BLK

    def body(qk_ref, v_ref, g_ref, cs_ref, sn_ref, o_ref, y_ref, st_ref, state):
        n = pl.program_id(0)

        @pl.when(n == 0)
        def _():
            state[...] = jnp.zeros_like(state)

        st_ref[0] = state[...]
        rq, rk, _, _, _ = _ret_qk(qk_ref, cs_ref, sn_ref, n)
        outs, ys = [], []
        for h in range(RET_HEADS):
            d_in, q_decay, k_decay, c_decay = _ret_decay(h)
            q = rq[:, 64 * h:64 * h + 64].astype(BF16)
            kf = rk[:, 64 * h:64 * h + 64]
            v = v_ref[:, 128 * h:128 * h + 128].astype(BF16)
            S = state[h]
            inner = _dot_nt(q, kf.astype(BF16)) * d_in
            y = _dot(inner.astype(BF16), v) + _dot(q, S.astype(BF16)) * q_decay
            state[h] = S * c_decay + _dot_tn((kf * k_decay).astype(BF16), v)
            g = g_ref[:, 128 * h:128 * h + 128]
            ys.append(y)
            outs.append(g * jax.nn.sigmoid(g) * _head_norm(y)[0])
        o_ref[...] = jnp.concatenate(outs, axis=1)
        y_ref[...] = jnp.concatenate(ys, axis=1)

    blk512 = lambda c: pl.BlockSpec((BLK, 512), lambda n: (n, c))
    tab = pl.BlockSpec((BLK, 128), lambda n: (n, 0))
    return pl.pallas_call(
        body, name=name, grid=(nb,),
        in_specs=[blk512(C_RQ // 512), blk512(C_RV // 512), blk512(C_RG // 512), tab, tab],
        out_specs=(blk512(0), blk512(0), pl.BlockSpec((1, RET_HEADS, RET_QK, RET_V), lambda n: (n, 0, 0, 0))),
        out_shape=(jax.ShapeDtypeStruct((L, 512), F32), jax.ShapeDtypeStruct((L, 512), F32),
                   jax.ShapeDtypeStruct((nb, RET_HEADS, RET_QK, RET_V), F32)),
        scratch_shapes=[pltpu.VMEM((RET_HEADS, RET_QK, RET_V), F32)],
        compiler_params=pltpu.CompilerParams(dimension_semantics=("arbitrary",)),
    )(P, P, P, cs, sn)


def ret_bwd(P, y, states, dmixed, cs, sn, *, name):
    L = P.shape[0]
    nb = L // BLK

    def body(qk_ref, v_ref, g_ref, y_ref, st_ref, do_ref, cs_ref, sn_ref, dqk_ref, dv_ref, dg_ref, dstate):
        n = nb - 1 - pl.program_id(0)

        @pl.when(pl.program_id(0) == 0)
        def _():
            dstate[...] = jnp.zeros_like(dstate)

        rq, rk, cs, sn, kmul = _ret_qk(qk_ref, cs_ref, sn_ref, n)
        dqs, dks, dvs, dgs = [], [], [], []
        for h in range(RET_HEADS):
            d_in, q_decay, k_decay, c_decay = _ret_decay(h)
            sv = slice(128 * h, 128 * h + 128)
            q = rq[:, 64 * h:64 * h + 64].astype(BF16)
            kf = rk[:, 64 * h:64 * h + 64]
            k = kf.astype(BF16)
            kd = (kf * k_decay).astype(BF16)
            v = v_ref[:, sv].astype(BF16)
            g = g_ref[:, sv]
            do = do_ref[:, sv]
            yh = y_ref[:, sv]
            S = st_ref[0, h].astype(BF16)
            dS = dstate[h]
            sg = jax.nn.sigmoid(g)
            yn, r = _head_norm(yh)
            dgs.append(do * yn * (sg * (1.0 + g * (1.0 - sg))))
            dyn = do * (g * sg)
            dy = r * (dyn - jnp.mean(dyn, axis=-1, keepdims=True) - yn * jnp.mean(dyn * yn, axis=-1, keepdims=True))
            dyb = dy.astype(BF16)
            dyq = (dy * q_decay).astype(BF16)
            inner = (_dot_nt(q, k) * d_in).astype(BF16)
            A = (_dot_nt(dyb, v) * d_in).astype(BF16)
            dSb = dS.astype(BF16)
            dqs.append(_dot(A, k) + _dot_nt(dyq, S))
            dks.append(_dot_tn(A, q) + _dot_nt(v, dSb) * k_decay)
            dvs.append(_dot_tn(inner, dyb) + _dot(kd, dSb))
            dstate[h] = dS * c_decay + _dot_tn(q, dyq)
        drq = _rope_t(jnp.concatenate(dqs, axis=1), cs, sn, RET_QK // 2)
        drk = _rope_t(jnp.concatenate(dks, axis=1) * kmul, cs, sn, RET_QK // 2)
        dqk_ref[...] = jnp.concatenate([drq, drk], axis=1)
        dv_ref[...] = jnp.concatenate(dvs, axis=1)
        dg_ref[...] = jnp.concatenate(dgs, axis=1)

    blk512 = lambda c: pl.BlockSpec((BLK, 512), lambda t: (nb - 1 - t, c))
    tab = pl.BlockSpec((BLK, 128), lambda t: (nb - 1 - t, 0))
    sds = jax.ShapeDtypeStruct((L, 512), F32)
    return pl.pallas_call(
        body, name=name, grid=(nb,),
        in_specs=[blk512(C_RQ // 512), blk512(C_RV // 512), blk512(C_RG // 512), blk512(0),
                  pl.BlockSpec((1, RET_HEADS, RET_QK, RET_V), lambda t: (nb - 1 - t, 0, 0, 0)), blk512(2), tab, tab],
        out_specs=(blk512(0), blk512(0), blk512(0)), out_shape=(sds, sds, sds),
        scratch_shapes=[pltpu.VMEM((RET_HEADS, RET_QK, RET_V), F32)],
        compiler_params=pltpu.CompilerParams(dimension_semantics=("arbitrary",)),
    )(P, P, P, y, states, dmixed, cs, sn)


def _perm_w_in(w):
    pad = jnp.zeros(w.shape[:-1] + (N_INP - N_IN,), w.dtype)
    return jnp.concatenate([w[..., 0:1536], w[..., 2208:3744], w[..., 1536:2208], pad], axis=-1)


def _unperm_w_in(g):
    return jnp.concatenate([g[..., 0:1536], g[..., 3072:3744], g[..., 1536:3072]], axis=-1)


def _perm_w_uq(w):
    lead = w.shape[:-1]
    w5 = w.reshape(lead + (4, 2, 96))
    nope = w5[..., :64].reshape(lead + (4, 128))
    rope = w5[..., 64:].reshape(lead + (4, 64))
    return jnp.concatenate([nope, rope, jnp.zeros(lead + (4, 64), w.dtype)], axis=-1).reshape(lead + (1024,))


def _unperm_w_uq(g):
    lead = g.shape[:-1]
    g4 = g.reshape(lead + (4, 256))
    nope = g4[..., :128].reshape(lead + (4, 2, 64))
    rope = g4[..., 128:192].reshape(lead + (4, 2, 32))
    return jnp.concatenate([nope, rope], axis=-1).reshape(lead + (768,))


def _perm_w_ukv(w):
    lead = w.shape[:-1]
    w4 = w.reshape(lead + (8, 128))
    return jnp.concatenate([w4[..., :64].reshape(lead + (512,)), w4[..., 64:].reshape(lead + (512,))], axis=-1)


def _unperm_w_ukv(g):
    lead = g.shape[:-1]
    return jnp.concatenate([g[..., :512].reshape(lead + (8, 64)), g[..., 512:].reshape(lead + (8, 64))],
                           axis=-1).reshape(lead + (1024,))


def _rope_tables(L, half):
    pos = (jnp.arange(L) - N_PAD).astype(F32)
    inv = ROPE_THETA ** (-jnp.arange(half, dtype=F32) / half)
    ang = pos[:, None] * inv[None, :]
    cos, sin = jnp.cos(ang), jnp.sin(ang)
    reps = 128 // (2 * half)
    cs = jnp.tile(jnp.concatenate([cos, cos], axis=1), (1, reps))
    sn = jnp.tile(jnp.concatenate([-sin, sin], axis=1), (1, reps))
    return cs, sn


def _device_step(x, target, meta, ln_emb_g, ln_emb_b, w_in, q_norm, kv_norm, w_uq, w_ukv, w_out,
                 ln1_g, ln1_b, w_ff1, w_ff2, ln2_g, ln2_b):
    S = x.shape[0]
    L = S + BLK
    depth = w_in.shape[0]
    cs_m, sn_m = _rope_tables(L, MLA_ROPE // 2)
    cs_r, sn_r = _rope_tables(L, RET_QK // 2)
    hcat = jnp.concatenate([jnp.zeros((N_PAD, D_MODEL), F32), meta, x], axis=0)
    h, _ = ln_fwd(hcat, ln_emb_g, ln_emb_b, name="ln_emb_fwd")

    saved = []
    for l in range(depth):
        P = mm_nn(h, w_in[l], tn=768, name=f"in_proj_{l}")
        out_a, sbc = sb_fwd(P, name=f"sb_fwd_{l}")
        nq, nkv, KR = mla_pre_fwd(P, q_norm[l], kv_norm[l], cs_m, sn_m, name=f"mla_pre_fwd_{l}")
        Q = mm_nn(nq, w_uq[l], tn=512, name=f"uq_{l}")
        KV = mm_nn(nkv, w_ukv[l], tn=512, name=f"ukv_{l}", out_dtype=BF16)
        out_b, lse = mla_fwd(Q, KV, KR, cs_m, sn_m, name=f"mla_fwd_{l}")
        out_c, y, states = ret_fwd(P, cs_r, sn_r, name=f"ret_fwd_{l}")
        mixed = jnp.concatenate([out_a, out_b, out_c], axis=1)
        w_out_l = w_out[l].reshape(1, 1536, D_MODEL)
        mix = mm_nn(mixed, w_out_l, tn=512, tk=512, name=f"out_proj_{l}")
        h1, z1 = ln_fwd(mix, ln1_g[l], ln1_b[l], res=h, name=f"ln1_fwd_{l}")
        U = mm_nn(h1, w_ff1[l], tn=1024, name=f"ff1_{l}")
        w_ff2_l = w_ff2[l].reshape(1, D_FF, D_MODEL)
        mlp = mm_nn(U, w_ff2_l, tn=512, tk=1024, prologue="relu2", name=f"ff2_{l}")
        h2, z2 = ln_fwd(mlp, ln2_g[l], ln2_b[l], res=h1, name=f"ln2_fwd_{l}")
        saved.append((h, P, sbc, nq, nkv, KR, Q, KV, lse, y, states, mixed, z1, h1, U, z2))
        h = h2

    loss_t, dh = loss_fwd_bwd(h, target, name="loss")

    grads = {k: [None] * depth for k in ("w_in", "q_norm", "kv_norm", "w_uq", "w_ukv", "w_out", "ln1_g", "ln1_b",
                                           "w_ff1", "w_ff2", "ln2_g", "ln2_b")}
    for l in reversed(range(depth)):
        h_in, P, sbc, nq, nkv, KR, Q, KV, lse, y, states, mixed, z1, h1, U, z2 = saved[l]
        dz2, grads["ln2_g"][l], grads["ln2_b"][l] = ln_bwd(dh, z2, ln2_g[l], name=f"ln2_bwd_{l}")
        w_ff2_l = w_ff2[l].reshape(1, D_FF, D_MODEL)
        grads["w_ff2"][l] = mm_tn(U, dz2, shards=1, tko=512, tn=1024, prologue="relu2", name=f"ff2_dw_{l}").reshape(4, 1024, D_MODEL)
        dU = mm_nt(dz2, w_ff2_l, tn=1024, tko=1024, relu2grad=U, name=f"ff2_dx_{l}")
        grads["w_ff1"][l] = mm_tn(h1, dU, shards=4, tko=512, tn=1024, name=f"ff1_dw_{l}")
        dh1 = mm_nt(dU, w_ff1[l], tn=1024, tko=1024, axpy=(dz2, DN_ALPHA), name=f"ff1_dx_{l}")
        dz1, grads["ln1_g"][l], grads["ln1_b"][l] = ln_bwd(dh1, z1, ln1_g[l], name=f"ln1_bwd_{l}")
        w_out_l = w_out[l].reshape(1, 1536, D_MODEL)
        grads["w_out"][l] = mm_tn(mixed, dz1, shards=1, tko=512, tn=1024, name=f"out_dw_{l}").reshape(4, 384, D_MODEL)
        dmixed = mm_nt(dz1, w_out_l, tn=1024, tko=512, name=f"out_dx_{l}")
        d_rqk, d_rv, d_rg = ret_bwd(P, y, states, dmixed, cs_r, sn_r, name=f"ret_bwd_{l}")
        dQ, dKN, dV, dKR = mla_bwd(Q, KV, KR, cs_m, sn_m, mixed, dmixed, lse, name=f"mla_bwd_{l}")
        dKV = jnp.concatenate([dKN, dV], axis=1)
        grads["w_uq"][l] = mm_tn(nq, dQ, shards=1, tko=MLA_Q_LORA, tn=512, name=f"uq_dw_{l}")[0]
        grads["w_ukv"][l] = mm_tn(nkv, dKV, shards=1, tko=MLA_KV_LORA, tn=512, name=f"ukv_dw_{l}")[0]
        dnq = mm_nt(dQ, w_uq[l], tn=1024, tko=MLA_Q_LORA, name=f"uq_dx_{l}")
        dnkv = mm_nt(dKV, w_ukv[l], tn=1024, tko=MLA_KV_LORA, name=f"ukv_dx_{l}")
        d_lat, grads["q_norm"][l], grads["kv_norm"][l] = mla_pre_bwd(P, dnq, dnkv, dKR, q_norm[l], kv_norm[l], cs_m, sn_m,
                                                                     name=f"mla_pre_bwd_{l}")
        dq_sb, dk_sb, dv_sb = sb_bwd(P, sbc, dmixed, name=f"sb_bwd_{l}")
        dP = jnp.concatenate([dq_sb, dk_sb, dv_sb, d_rqk, d_rv, d_rg, d_lat], axis=1)
        grads["w_in"][l] = mm_tn(h_in, dP, shards=1, tko=512, tn=768, name=f"in_dw_{l}")[0]
        dh = mm_nt(dP, w_in[l], tn=768, tko=1024, axpy=(dz1, DN_ALPHA), name=f"in_dx_{l}")

    dhcat, dg_emb, db_emb = ln_bwd(dh, hcat, ln_emb_g, name="ln_emb_bwd")
    out = {k: jnp.stack(v) for k, v in grads.items()}
    out["ln_emb_g"], out["ln_emb_b"] = dg_emb, db_emb
    out["meta"] = dhcat[N_PAD:BLK]
    return loss_t[0, 0], dhcat[BLK:], out


MESH = pl.DeviceIdType.MESH
PEER_XOR = (2, 1, 3)
_HBM = pl.BlockSpec(memory_space=pltpu.HBM)


def _place():
    x, y, c = lax.axis_index("x"), lax.axis_index("y"), lax.axis_index("c")
    peers = [(1 - x, y, c), (x, 1 - y, c), (1 - x, 1 - y, c)]
    return x, y, c, 2 * x + y, peers, (x, y, 1 - c)


def gather_weight(w_shard, *, name):
    nl = w_shard.shape[0]
    hl = nl // 2

    def body(w_ref, out_ref, send_sems, recv_sems, local_sem):
        x, y, c, s0, peers, sibling = _place()
        mine = pltpu.make_async_copy(w_ref, out_ref.at[s0], local_sem)
        mine.start()

        def piece(s, half):
            return out_ref.at[s, pl.ds(half * hl, hl)]

        def copy(k, s, half, to, src=None):
            return pltpu.make_async_remote_copy(src_ref=piece(s, half) if src is None else src, dst_ref=piece(s, half),
                                                send_sem=send_sems.at[k], recv_sem=recv_sems.at[k],
                                                device_id=to, device_id_type=MESH)

        first = [copy(k, s0, c, peers[k], src=w_ref.at[pl.ds(c * hl, hl)]) for k in range(3)]
        for cp in first:
            cp.start()
        passed = [copy(3 + k, s0 ^ PEER_XOR[k], c, sibling) for k in range(3)]
        for k in range(3):
            copy(k, s0 ^ PEER_XOR[k], c, peers[k]).wait_recv()
            passed[k].start()
        for k in range(3):
            copy(3 + k, s0 ^ PEER_XOR[k], 1 - c, sibling).wait_recv()
        for cp in first + passed:
            cp.wait_send()
        mine.wait()

    return pl.pallas_call(
        body, name=name, in_specs=[_HBM], out_specs=_HBM,
        out_shape=jax.ShapeDtypeStruct((4,) + w_shard.shape, w_shard.dtype),
        scratch_shapes=[pltpu.SemaphoreType.DMA((6,)), pltpu.SemaphoreType.DMA((6,)), pltpu.SemaphoreType.DMA],
    )(w_shard)


def send_half_to_sibling(G, *, name):
    hl = G.shape[1] // 2

    def body(g_ref, out_ref, send_sem, recv_sem):
        x, y, c, s0, peers, sibling = _place()
        cp = pltpu.make_async_remote_copy(src_ref=g_ref.at[:, pl.ds((1 - c) * hl, hl)], dst_ref=out_ref,
                                          send_sem=send_sem, recv_sem=recv_sem, device_id=sibling, device_id_type=MESH)
        cp.start()
        cp.wait()

    return pl.pallas_call(
        body, name=name, in_specs=[_HBM], out_specs=_HBM,
        out_shape=jax.ShapeDtypeStruct((4, hl) + G.shape[2:], G.dtype),
        scratch_shapes=[pltpu.SemaphoreType.DMA, pltpu.SemaphoreType.DMA],
    )(G)


def scatter_to_chips(A, *, name):
    def body(a_ref, out_ref, send_sems, recv_sems):
        x, y, c, s0, peers, sibling = _place()
        copies = [pltpu.make_async_remote_copy(src_ref=a_ref.at[s0 ^ PEER_XOR[k]], dst_ref=out_ref.at[k],
                                               send_sem=send_sems.at[k], recv_sem=recv_sems.at[k],
                                               device_id=peers[k], device_id_type=MESH) for k in range(3)]
        for cp in copies:
            cp.start()
        for cp in copies:
            cp.wait()

    return pl.pallas_call(
        body, name=name, in_specs=[_HBM], out_specs=_HBM,
        out_shape=jax.ShapeDtypeStruct((3,) + A.shape[1:], A.dtype),
        scratch_shapes=[pltpu.SemaphoreType.DMA((3,)), pltpu.SemaphoreType.DMA((3,))],
    )(A)


def join_halves(Rh, *, name):
    hl = Rh.shape[0]

    def body(r_ref, out_ref, send_sem, recv_sem, local_sem):
        x, y, c, s0, peers, sibling = _place()
        mine = pltpu.make_async_copy(r_ref, out_ref.at[pl.ds(c * hl, hl)], local_sem)
        mine.start()
        cp = pltpu.make_async_remote_copy(src_ref=r_ref, dst_ref=out_ref.at[pl.ds(c * hl, hl)],
                                          send_sem=send_sem, recv_sem=recv_sem, device_id=sibling, device_id_type=MESH)
        cp.start()
        pltpu.make_async_remote_copy(src_ref=r_ref, dst_ref=out_ref.at[pl.ds((1 - c) * hl, hl)],
                                     send_sem=send_sem, recv_sem=recv_sem, device_id=sibling, device_id_type=MESH).wait_recv()
        cp.wait_send()
        mine.wait()

    return pl.pallas_call(
        body, name=name, in_specs=[_HBM], out_specs=_HBM,
        out_shape=jax.ShapeDtypeStruct((2 * hl,) + Rh.shape[1:], Rh.dtype),
        scratch_shapes=[pltpu.SemaphoreType.DMA, pltpu.SemaphoreType.DMA, pltpu.SemaphoreType.DMA],
    )(Rh)


def allgather8(xs, *, name, reduce):
    M, N = xs.shape

    def body(x_ref, out_ref, *rest):
        if reduce:
            all_ref, send_sems, recv_sems, local_sem = rest
        else:
            all_ref = out_ref
            send_sems, recv_sems, local_sem = rest
        x, y, c, s0, peers, sibling = _place()
        me = (x, y, c)
        chips = [(1 - x, y), (x, 1 - y), (1 - x, 1 - y)]

        def rows(px, py, pc):
            return all_ref.at[pl.ds((4 * px + 2 * py + pc) * M, M), :]

        def copy(k, block, to, src=None):
            return pltpu.make_async_remote_copy(src_ref=rows(*block) if src is None else src, dst_ref=rows(*block),
                                                send_sem=send_sems.at[k], recv_sem=recv_sems.at[k],
                                                device_id=to, device_id_type=MESH)

        mine = pltpu.make_async_copy(x_ref, rows(*me), local_sem)
        mine.start()
        first = [copy(0, me, sibling, src=x_ref)]
        first += [copy(1 + j, me, (*chip, c), src=x_ref) for j, chip in enumerate(chips)]
        for cp in first:
            cp.start()
        passed = [copy(4 + j, (*chip, c), sibling) for j, chip in enumerate(chips)]
        for j, chip in enumerate(chips):
            copy(1 + j, (*chip, c), me).wait_recv()
            passed[j].start()
        copy(0, sibling, me).wait_recv()
        for j, chip in enumerate(chips):
            copy(4 + j, (*chip, 1 - c), me).wait_recv()
        for cp in first + passed:
            cp.wait_send()
        mine.wait()
        if reduce:
            acc = all_ref[pl.ds(0, M), :]
            for d in range(1, 8):
                acc = acc + all_ref[pl.ds(d * M, M), :]
            out_ref[...] = acc

    vm = pl.BlockSpec(memory_space=pltpu.VMEM)
    scratch = [pltpu.SemaphoreType.DMA((7,)), pltpu.SemaphoreType.DMA((7,)), pltpu.SemaphoreType.DMA]
    if reduce:
        scratch = [pltpu.VMEM((8 * M, N), xs.dtype)] + scratch
    return pl.pallas_call(
        body, name=name, in_specs=[vm], out_specs=vm,
        out_shape=jax.ShapeDtypeStruct((M if reduce else 8 * M, N), xs.dtype), scratch_shapes=scratch,
    )(xs)


def add_halves(G, B, c, *, name):
    S, nl, R, C = G.shape
    hl = nl // 2
    tr = _pick(R, (512, 384, 256, 128))

    def body(c_ref, g_ref, b_ref, o_ref):
        o_ref[...] = g_ref[...] + b_ref[...]

    blk = (1, 1, tr, C)
    return pl.pallas_call(
        body, name=name,
        grid_spec=pltpu.PrefetchScalarGridSpec(
            num_scalar_prefetch=1, grid=(S, hl, R // tr),
            in_specs=[pl.BlockSpec(blk, lambda s, l, r, cr: (s, cr[0] * hl + l, r, 0)),
                      pl.BlockSpec(blk, lambda s, l, r, cr: (s, l, r, 0))],
            out_specs=pl.BlockSpec(blk, lambda s, l, r, cr: (s, l, r, 0))),
        out_shape=jax.ShapeDtypeStruct((S, hl, R, C), F32),
    )(jnp.reshape(c, (1,)).astype(jnp.int32), G, B)


def add_chips(A, B, s0, *, name):
    S, hl, R, C = A.shape
    tr = _pick(R, (512, 384, 256, 128))

    def body(s_ref, a_ref, b0_ref, b1_ref, b2_ref, o_ref):
        o_ref[...] = ((a_ref[0] + b0_ref[0]) + b1_ref[0]) + b2_ref[0]

    blk = (1, 1, tr, C)
    bspec = lambda k: pl.BlockSpec(blk, lambda l, r, sr: (k, l, r, 0))
    return pl.pallas_call(
        body, name=name,
        grid_spec=pltpu.PrefetchScalarGridSpec(
            num_scalar_prefetch=1, grid=(hl, R // tr),
            in_specs=[pl.BlockSpec(blk, lambda l, r, sr: (sr[0], l, r, 0)), bspec(0), bspec(1), bspec(2)],
            out_specs=pl.BlockSpec((1, tr, C), lambda l, r, sr: (l, r, 0))),
        out_shape=jax.ShapeDtypeStruct((hl, R, C), F32),
    )(jnp.reshape(s0, (1,)).astype(jnp.int32), A, B, B, B)


def reduce_scatter_weight(G, c, s0, *, tag):
    B = send_half_to_sibling(G, name=f"rs_sib_{tag}")
    A = add_halves(G, B, c, name=f"rs_add1_{tag}")
    Bc = scatter_to_chips(A, name=f"rs_chips_{tag}")
    Rh = add_chips(A, Bc, s0, name=f"rs_add2_{tag}")
    return join_halves(Rh, name=f"rs_join_{tag}")


def adamw(w, g, m, v, *, name):
    shp = w.shape
    C = shp[-1]
    R = int(np.prod(shp[:-1])) if len(shp) > 1 else 1
    tr = R
    for t in (512, 384, 256, 128):
        if R % t == 0:
            tr = t
            break

    def body(w_ref, g_ref, m_ref, v_ref, d_ref, nm_ref, nv_ref):
        gv = g_ref[...]
        mn = ADAM_B1 * m_ref[...] + (1.0 - ADAM_B1) * gv
        vn = ADAM_B2 * v_ref[...] + (1.0 - ADAM_B2) * jnp.square(gv)
        m_hat = mn / (1.0 - ADAM_B1 ** ADAM_STEP)
        v_hat = vn / (1.0 - ADAM_B2 ** ADAM_STEP)
        d_ref[...] = -ADAM_LR * (m_hat / (jnp.sqrt(v_hat) + ADAM_EPS) + ADAM_WD * w_ref[...])
        nm_ref[...] = mn
        nv_ref[...] = vn

    spec = pl.BlockSpec((tr, C), lambda i: (i, 0))
    sds = jax.ShapeDtypeStruct((R, C), F32)
    d, nm, nv = pl.pallas_call(body, name=name, grid=(R // tr,), in_specs=[spec] * 4, out_specs=(spec,) * 3,
                               out_shape=(sds,) * 3)(*(a.reshape(R, C) for a in (w, g, m, v)))
    return d.reshape(shp), nm.reshape(shp), nv.reshape(shp)


_SMALL = ("ln_emb_g", "ln_emb_b", "q_norm", "kv_norm", "ln1_g", "ln1_b", "ln2_g", "ln2_b", "meta")


def _pack_small(d):
    flat = jnp.concatenate([d[k].reshape(-1) for k in _SMALL])
    rows = -(-flat.shape[0] // 128)
    rows = -(-rows // 8) * 8
    flat = jnp.concatenate([flat, jnp.zeros((rows * 128 - flat.shape[0],), F32)])
    return flat.reshape(rows, 128)


def _unpack_small(p, shapes):
    flat = p.reshape(-1)
    out, o = {}, 0
    for k in _SMALL:
        n = int(np.prod(shapes[k]))
        out[k] = flat[o:o + n].reshape(shapes[k])
        o += n
    return out


def kernel(x, meta_tokens, ln_emb_g, ln_emb_b, w_in, mla_q_norm, mla_kv_norm, w_uq, w_ukv, w_out, ln1_g, ln1_b, w_ff1, w_ff2, ln2_g, ln2_b, loss_target, m_meta_tokens, m_ln_emb_g, m_ln_emb_b, m_w_in, m_mla_q_norm, m_mla_kv_norm, m_w_uq, m_w_ukv, m_w_out, m_ln1_g, m_ln1_b, m_w_ff1, m_w_ff2, m_ln2_g, m_ln2_b, v_meta_tokens, v_ln_emb_g, v_ln_emb_b, v_w_in, v_mla_q_norm, v_mla_kv_norm, v_w_uq, v_w_ukv, v_w_out, v_ln1_g, v_ln1_b, v_w_ff1, v_w_ff2, v_ln2_g, v_ln2_b):
    xi, yi, ci = lax.axis_index("x"), lax.axis_index("y"), lax.axis_index("c")
    s0 = 2 * xi + yi
    nl = w_in.shape[0]

    big = {"w_in": w_in, "w_uq": w_uq, "w_ukv": w_ukv, "w_out": w_out, "w_ff1": w_ff1, "w_ff2": w_ff2}
    full = {k: gather_weight(v.astype(BF16), name=f"ag_{k}") for k, v in big.items()}
    cols = lambda a: jnp.moveaxis(a, 0, 2).reshape(a.shape[1], a.shape[2], 4 * a.shape[3])
    k_w_in = _perm_w_in(cols(full["w_in"]))[:, None]
    k_w_uq = _perm_w_uq(cols(full["w_uq"]))[:, None]
    k_w_ukv = _perm_w_ukv(cols(full["w_ukv"]))[:, None]
    k_w_out = jnp.moveaxis(full["w_out"], 0, 1)
    k_w_ff1 = jnp.moveaxis(full["w_ff1"], 0, 1)
    k_w_ff2 = jnp.moveaxis(full["w_ff2"], 0, 1)
    meta_all = allgather8(meta_tokens, name="ag_meta", reduce=False)
    meta_full = jnp.concatenate([meta_all[32 * s:32 * s + N_META] for s in range(4)], axis=1)

    loss_part, grad_x, g = _device_step(x[0], loss_target[0], meta_full, ln_emb_g, ln_emb_b, k_w_in, mla_q_norm, mla_kv_norm,
                                        k_w_uq, k_w_ukv, k_w_out, ln1_g, ln1_b, k_w_ff1, k_w_ff2, ln2_g, ln2_b)
    loss = lax.psum(loss_part, ("x", "y", "c"))

    def col_shards(a):
        return jnp.moveaxis(a.reshape(a.shape[0], a.shape[1], 4, a.shape[2] // 4), 2, 0)

    G = {"w_in": col_shards(_unperm_w_in(g["w_in"])), "w_uq": col_shards(_unperm_w_uq(g["w_uq"])),
         "w_ukv": col_shards(_unperm_w_ukv(g["w_ukv"])), "w_out": jnp.moveaxis(g["w_out"], 1, 0),
         "w_ff1": jnp.moveaxis(g["w_ff1"], 1, 0), "w_ff2": jnp.moveaxis(g["w_ff2"], 1, 0)}
    gw = {k: reduce_scatter_weight(v, ci, s0, tag=k) for k, v in G.items()}

    small_shapes = {"ln_emb_g": (D_MODEL,), "ln_emb_b": (D_MODEL,), "q_norm": (nl, MLA_Q_LORA), "kv_norm": (nl, MLA_KV_LORA),
                    "ln1_g": (nl, D_MODEL), "ln1_b": (nl, D_MODEL), "ln2_g": (nl, D_MODEL), "ln2_b": (nl, D_MODEL),
                    "meta": (N_META, D_MODEL)}
    gs = _unpack_small(allgather8(_pack_small(g), name="ar_small", reduce=True), small_shapes)
    gw.update({"ln_emb_g": gs["ln_emb_g"], "ln_emb_b": gs["ln_emb_b"], "mla_q_norm": gs["q_norm"], "mla_kv_norm": gs["kv_norm"],
               "ln1_g": gs["ln1_g"], "ln1_b": gs["ln1_b"], "ln2_g": gs["ln2_g"], "ln2_b": gs["ln2_b"],
               "meta_tokens": lax.dynamic_slice_in_dim(gs["meta"], s0 * 256, 256, axis=1)})

    names = ["meta_tokens", "ln_emb_g", "ln_emb_b", "w_in", "mla_q_norm", "mla_kv_norm", "w_uq", "w_ukv", "w_out",
             "ln1_g", "ln1_b", "w_ff1", "w_ff2", "ln2_g", "ln2_b"]
    ws = [meta_tokens, ln_emb_g, ln_emb_b, w_in, mla_q_norm, mla_kv_norm, w_uq, w_ukv, w_out, ln1_g, ln1_b, w_ff1, w_ff2, ln2_g, ln2_b]
    ms = [m_meta_tokens, m_ln_emb_g, m_ln_emb_b, m_w_in, m_mla_q_norm, m_mla_kv_norm, m_w_uq, m_w_ukv, m_w_out, m_ln1_g, m_ln1_b, m_w_ff1, m_w_ff2, m_ln2_g, m_ln2_b]
    vs = [v_meta_tokens, v_ln_emb_g, v_ln_emb_b, v_w_in, v_mla_q_norm, v_mla_kv_norm, v_w_uq, v_w_ukv, v_w_out, v_ln1_g, v_ln1_b, v_w_ff1, v_w_ff2, v_ln2_g, v_ln2_b]
    deltas, new_m, new_v = [], [], []
    for n, w, m, v in zip(names, ws, ms, vs):
        w2 = w.reshape(1, -1) if w.ndim == 1 else w
        d, nm, nv = adamw(w2, gw[n].reshape(w2.shape), m.reshape(w2.shape), v.reshape(w2.shape), name=f"adamw_{n}")
        deltas.append(d.reshape(w.shape))
        new_m.append(nm.reshape(w.shape))
        new_v.append(nv.reshape(w.shape))
    grads_out = [gw[n].reshape(w.shape) for n, w in zip(names, ws)]
    return (loss, grad_x[None], *grads_out, *deltas, *new_m, *new_v)
```

```python
import functools
import math

import numpy as np
import jax
import jax.numpy as jnp
from jax import lax
from jax.experimental import pallas as pl
from jax.experimental.pallas import tpu as pltpu

F32 = jnp.float32
BF16 = jnp.bfloat16

D_MODEL = 1024
DEPTH = 4
N_META = 16
BLK = 128
N_PAD = 112
SB_HEADS = 8
MLA_HEADS = 8
MLA_NOPE = 64
MLA_ROPE = 32
MLA_V = 64
MLA_Q_LORA = 384
MLA_KV_LORA = 256
RET_HEADS = 4
RET_QK = 64
RET_V = 128
D_FF = 4 * D_MODEL
ROPE_THETA = 10000.0
LN_EPS = 1e-5
DN_ALPHA = (2 * DEPTH) ** 0.25
RET_GAMMA = tuple(1.0 - 2.0 ** (-5 - h) for h in range(RET_HEADS))
RET_LOG_G = tuple(float(np.log(np.float32(g))) for g in RET_GAMMA)
MLA_SCALE = (MLA_NOPE + MLA_ROPE) ** -0.5

ADAM_LR = 0.001
ADAM_B1 = 0.9
ADAM_B2 = 0.999
ADAM_EPS = 1e-08
ADAM_WD = 0.01
ADAM_STEP = 10

C_SBQ, C_SBK, C_SBV = 0, 512, 1024
C_RQ, C_RK, C_RV, C_RG = 1536, 1792, 2048, 2560
C_CQ, C_CKV, C_KR = 3072, 3456, 3712
N_IN = 3744
N_INP = 3840

NEG = -1e30


def _pick(n, cands):
    for t in cands:
        if n % t == 0:
            return t
    raise ValueError(f"no tile for {n} in {cands}")


def _row_tile(n):
    return _pick(n, (1056, 1024, 528, 512, 384, 256, 128))


def _dot(a, b):
    return jnp.dot(a, b, preferred_element_type=F32)


def _dot_nt(a, b):
    return lax.dot_general(a, b, (((1,), (1,)), ((), ())), preferred_element_type=F32)


def _dot_tn(a, b):
    return lax.dot_general(a, b, (((0,), (0,)), ((), ())), preferred_element_type=F32)


def mm_nn(a, b, *, tn, name, tk=None, prologue=None, axpy=None, out_dtype=F32):
    M, K = a.shape
    S, _, Ns = b.shape
    tm = _row_tile(M)
    tk = K if tk is None else tk
    npt = Ns // tn
    nk = K // tk
    alpha = None if axpy is None else axpy[1]

    def body(*refs):
        if axpy is None:
            a_ref, b_ref, o_ref, acc = refs
        else:
            a_ref, b_ref, e_ref, o_ref, acc = refs
        k = pl.program_id(2)

        @pl.when(k == 0)
        def _():
            acc[...] = jnp.zeros_like(acc)

        x = a_ref[...]
        if prologue == "relu2":
            x = jnp.square(jnp.maximum(x, 0.0))
        acc[...] += _dot(x.astype(BF16), b_ref[0])

        @pl.when(k == nk - 1)
        def _():
            r = acc[...]
            if axpy is not None:
                r = r + alpha * e_ref[...]
            o_ref[...] = r.astype(out_dtype)

    in_specs = [pl.BlockSpec((tm, tk), lambda i, j, k: (i, k)),
                pl.BlockSpec((1, tk, tn), lambda i, j, k: (j // npt, k, j % npt))]
    args = [a, b]
    if axpy is not None:
        in_specs.append(pl.BlockSpec((tm, tn), lambda i, j, k: (i, j)))
        args.append(axpy[0])
    return pl.pallas_call(
        body, name=name, grid=(M // tm, (S * Ns) // tn, nk), in_specs=in_specs,
        out_specs=pl.BlockSpec((tm, tn), lambda i, j, k: (i, j)),
        out_shape=jax.ShapeDtypeStruct((M, S * Ns), out_dtype),
        scratch_shapes=[pltpu.VMEM((tm, tn), F32)],
        compiler_params=pltpu.CompilerParams(dimension_semantics=("parallel", "parallel", "arbitrary")),
    )(*args)


def mm_nt(a, b, *, tn, tko, name, axpy=None, relu2grad=None):
    M, N = a.shape
    S, K, Ns = b.shape
    tm = _row_tile(M)
    npt = Ns // tn
    nn = N // tn
    alpha = None if axpy is None else axpy[1]

    def body(*refs):
        if axpy is None and relu2grad is None:
            a_ref, b_ref, o_ref, acc = refs
        else:
            a_ref, b_ref, e_ref, o_ref, acc = refs
        n = pl.program_id(2)

        @pl.when(n == 0)
        def _():
            acc[...] = jnp.zeros_like(acc)

        acc[...] += _dot_nt(a_ref[...].astype(BF16), b_ref[0])

        @pl.when(n == nn - 1)
        def _():
            r = acc[...]
            if axpy is not None:
                r = r + alpha * e_ref[...]
            if relu2grad is not None:
                r = r * (2.0 * jnp.maximum(e_ref[...], 0.0))
            o_ref[...] = r

    in_specs = [pl.BlockSpec((tm, tn), lambda i, j, n: (i, n)),
                pl.BlockSpec((1, tko, tn), lambda i, j, n: (n // npt, j, n % npt))]
    args = [a, b]
    extra = axpy[0] if axpy is not None else relu2grad
    if extra is not None:
        in_specs.append(pl.BlockSpec((tm, tko), lambda i, j, n: (i, j)))
        args.append(extra)
    return pl.pallas_call(
        body, name=name, grid=(M // tm, K // tko, nn), in_specs=in_specs,
        out_specs=pl.BlockSpec((tm, tko), lambda i, j, n: (i, j)),
        out_shape=jax.ShapeDtypeStruct((M, K), F32),
        scratch_shapes=[pltpu.VMEM((tm, tko), F32)],
        compiler_params=pltpu.CompilerParams(dimension_semantics=("parallel", "parallel", "arbitrary")),
    )(*args)


def mm_tn(a, g, *, shards, tko, tn, name, prologue=None):
    M, K = a.shape
    _, N = g.shape
    Ns = N // shards
    tm = _row_tile(M)
    npt = Ns // tn
    nm = M // tm

    def body(a_ref, g_ref, o_ref, acc):
        m = pl.program_id(2)

        @pl.when(m == 0)
        def _():
            acc[...] = jnp.zeros_like(acc)

        x = a_ref[...]
        if prologue == "relu2":
            x = jnp.square(jnp.maximum(x, 0.0))
        acc[...] += _dot_tn(x.astype(BF16), g_ref[...].astype(BF16))

        @pl.when(m == nm - 1)
        def _():
            o_ref[0] = acc[...]

    return pl.pallas_call(
        body, name=name, grid=(K // tko, N // tn, nm),
        in_specs=[pl.BlockSpec((tm, tko), lambda i, j, m: (m, i)),
                  pl.BlockSpec((tm, tn), lambda i, j, m: (m, j))],
        out_specs=pl.BlockSpec((1, tko, tn), lambda i, j, m: (j // npt, i, j % npt)),
        out_shape=jax.ShapeDtypeStruct((shards, K, Ns), F32),
        scratch_shapes=[pltpu.VMEM((tko, tn), F32)],
        compiler_params=pltpu.CompilerParams(dimension_semantics=("parallel", "parallel", "arbitrary")),
    )(a, g)


def _ln_stats(z):
    mu = jnp.mean(z, axis=-1, keepdims=True)
    zc = z - mu
    var = jnp.mean(jnp.square(zc), axis=-1, keepdims=True)
    r = lax.rsqrt(var + LN_EPS)
    return zc * r, r


def ln_fwd(x, g, b, *, name, res=None):
    L, Dm = x.shape
    tr = _row_tile(L)
    g2, b2 = g.reshape(1, Dm), b.reshape(1, Dm)

    def body(*refs):
        if res is None:
            x_ref, g_ref, b_ref, y_ref = refs
            z = x_ref[...]
        else:
            x_ref, r_ref, g_ref, b_ref, y_ref, z_ref = refs
            z = DN_ALPHA * r_ref[...] + x_ref[...]
            z_ref[...] = z
        xh, _ = _ln_stats(z)
        y_ref[...] = xh * g_ref[...] + b_ref[...]

    row = pl.BlockSpec((tr, Dm), lambda i: (i, 0))
    vec = pl.BlockSpec((1, Dm), lambda i: (0, 0))
    sds = jax.ShapeDtypeStruct((L, Dm), F32)
    if res is None:
        y = pl.pallas_call(body, name=name, grid=(L // tr,), in_specs=[row, vec, vec], out_specs=row, out_shape=sds)(x, g2, b2)
        return y, x
    return pl.pallas_call(body, name=name, grid=(L // tr,), in_specs=[row, row, vec, vec], out_specs=(row, row),
                          out_shape=(sds, sds))(x, res, g2, b2)


def ln_bwd(dy, z, g, *, name):
    L, Dm = z.shape
    tr = _row_tile(L)

    def body(dy_ref, z_ref, g_ref, dz_ref, dg_ref, db_ref):
        @pl.when(pl.program_id(0) == 0)
        def _():
            dg_ref[...] = jnp.zeros_like(dg_ref)
            db_ref[...] = jnp.zeros_like(db_ref)

        dyv = dy_ref[...]
        xh, r = _ln_stats(z_ref[...])
        dxh = dyv * g_ref[...]
        m1 = jnp.mean(dxh, axis=-1, keepdims=True)
        m2 = jnp.mean(dxh * xh, axis=-1, keepdims=True)
        dz_ref[...] = r * (dxh - m1 - xh * m2)
        dg_ref[...] += jnp.sum(dyv * xh, axis=0, keepdims=True)
        db_ref[...] += jnp.sum(dyv, axis=0, keepdims=True)

    row = pl.BlockSpec((tr, Dm), lambda i: (i, 0))
    vec = pl.BlockSpec((1, Dm), lambda i: (0, 0))
    return pl.pallas_call(
        body, name=name, grid=(L // tr,), in_specs=[row, row, vec], out_specs=(row, vec, vec),
        out_shape=(jax.ShapeDtypeStruct((L, Dm), F32), jax.ShapeDtypeStruct((1, Dm), F32), jax.ShapeDtypeStruct((1, Dm), F32)),
        compiler_params=pltpu.CompilerParams(dimension_semantics=("arbitrary",)),
    )(dy, z, g.reshape(1, Dm))


def loss_fwd_bwd(h, target, *, name):
    L, Dm = h.shape
    nb = L // BLK

    def body(h_ref, t_ref, l_ref, dh_ref):
        i = pl.program_id(0)

        @pl.when(i == 0)
        def _():
            l_ref[...] = jnp.zeros_like(l_ref)
            dh_ref[...] = jnp.zeros_like(dh_ref)

        @pl.when(i > 0)
        def _():
            e = h_ref[...] - t_ref[...]
            dh_ref[...] = e * (1.0 / Dm)
            part = jnp.sum(jnp.sum(jnp.square(e), axis=-1, keepdims=True) * (1.0 / Dm), axis=0, keepdims=True)
            l_ref[...] += 0.5 * part

    return pl.pallas_call(
        body, name=name, grid=(nb,),
        in_specs=[pl.BlockSpec((BLK, Dm), lambda i: (i, 0)),
                  pl.BlockSpec((BLK, Dm), lambda i: (jnp.maximum(i - 1, 0), 0))],
        out_specs=(pl.BlockSpec((8, 128), lambda i: (0, 0)), pl.BlockSpec((BLK, Dm), lambda i: (i, 0))),
        out_shape=(jax.ShapeDtypeStruct((8, 128), F32), jax.ShapeDtypeStruct((L, Dm), F32)),
        compiler_params=pltpu.CompilerParams(dimension_semantics=("arbitrary",)),
    )(h, target)


def _swap_half(x, half):
    ax = x.ndim - 1
    n = x.shape[ax]
    lane = lax.broadcasted_iota(jnp.int32, x.shape, ax)
    up = pltpu.roll(x, n - half, ax)
    dn = pltpu.roll(x, half, ax)
    return jnp.where((lane % (2 * half)) < half, up, dn)


def _rope(x, cs, sn, half):
    return x * cs + _swap_half(x, half) * sn


def _rope_t(dy, cs, sn, half):
    return dy * cs + _swap_half(dy * sn, half)


def _rms(x):
    r = lax.rsqrt(jnp.mean(jnp.square(x), axis=-1, keepdims=True) + LN_EPS)
    return x * r, r


def mla_pre_fwd(P, gq, gkv, cs, sn, *, name):
    L = P.shape[0]
    tr = _row_tile(L)

    def body(p_ref, gq_ref, gkv_ref, cs_ref, sn_ref, nq_ref, nkv_ref, kr_ref):
        cq = p_ref[:, 0:MLA_Q_LORA]
        ckv = p_ref[:, MLA_Q_LORA:MLA_Q_LORA + MLA_KV_LORA]
        kr = p_ref[:, 640:768]
        nq_ref[...] = (_rms(cq)[0] * gq_ref[...]).astype(BF16)
        nkv_ref[...] = (_rms(ckv)[0] * gkv_ref[...]).astype(BF16)
        krr = _rope(kr, cs_ref[...], sn_ref[...], MLA_ROPE // 2)
        kr_ref[...] = (krr + pltpu.roll(krr, MLA_ROPE, 1)).astype(BF16)

    return pl.pallas_call(
        body, name=name, grid=(L // tr,),
        in_specs=[pl.BlockSpec((tr, 768), lambda i: (i, C_CQ // 768)),
                  pl.BlockSpec((1, MLA_Q_LORA), lambda i: (0, 0)), pl.BlockSpec((1, MLA_KV_LORA), lambda i: (0, 0)),
                  pl.BlockSpec((tr, 128), lambda i: (i, 0)), pl.BlockSpec((tr, 128), lambda i: (i, 0))],
        out_specs=(pl.BlockSpec((tr, MLA_Q_LORA), lambda i: (i, 0)), pl.BlockSpec((tr, MLA_KV_LORA), lambda i: (i, 0)),
                   pl.BlockSpec((tr, 128), lambda i: (i, 0))),
        out_shape=(jax.ShapeDtypeStruct((L, MLA_Q_LORA), BF16), jax.ShapeDtypeStruct((L, MLA_KV_LORA), BF16),
                   jax.ShapeDtypeStruct((L, 128), BF16)),
    )(P, gq.reshape(1, -1), gkv.reshape(1, -1), cs, sn)


def mla_pre_bwd(P, dnq, dnkv, dkr, gq, gkv, cs, sn, *, name):
    L = P.shape[0]
    tr = _row_tile(L)

    def body(p_ref, dnq_ref, dnkv_ref, dkr_ref, gq_ref, gkv_ref, cs_ref, sn_ref, dp_ref, dgq_ref, dgkv_ref):
        @pl.when(pl.program_id(0) == 0)
        def _():
            dgq_ref[...] = jnp.zeros_like(dgq_ref)
            dgkv_ref[...] = jnp.zeros_like(dgkv_ref)

        def rms_bwd(x, dy, g_ref, dg_ref):
            xn, r = _rms(x)
            dxn = dy * g_ref[...]
            dg_ref[...] += jnp.sum(dy * xn, axis=0, keepdims=True)
            return r * (dxn - xn * jnp.mean(dxn * xn, axis=-1, keepdims=True))

        dp_ref[:, 0:MLA_Q_LORA] = rms_bwd(p_ref[:, 0:MLA_Q_LORA], dnq_ref[...], gq_ref, dgq_ref)
        dp_ref[:, MLA_Q_LORA:640] = rms_bwd(p_ref[:, MLA_Q_LORA:640], dnkv_ref[...], gkv_ref, dgkv_ref)
        d2 = dkr_ref[...]
        lane = lax.broadcasted_iota(jnp.int32, d2.shape, 1)
        dkr = jnp.where(lane < MLA_ROPE, d2 + pltpu.roll(d2, 128 - MLA_ROPE, 1), 0.0)
        dp_ref[:, 640:768] = _rope_t(dkr, cs_ref[...], sn_ref[...], MLA_ROPE // 2)

    return pl.pallas_call(
        body, name=name, grid=(L // tr,),
        in_specs=[pl.BlockSpec((tr, 768), lambda i: (i, C_CQ // 768)),
                  pl.BlockSpec((tr, MLA_Q_LORA), lambda i: (i, 0)), pl.BlockSpec((tr, MLA_KV_LORA), lambda i: (i, 0)),
                  pl.BlockSpec((tr, 128), lambda i: (i, 0)),
                  pl.BlockSpec((1, MLA_Q_LORA), lambda i: (0, 0)), pl.BlockSpec((1, MLA_KV_LORA), lambda i: (0, 0)),
                  pl.BlockSpec((tr, 128), lambda i: (i, 0)), pl.BlockSpec((tr, 128), lambda i: (i, 0))],
        out_specs=(pl.BlockSpec((tr, 768), lambda i: (i, 0)), pl.BlockSpec((1, MLA_Q_LORA), lambda i: (0, 0)),
                   pl.BlockSpec((1, MLA_KV_LORA), lambda i: (0, 0))),
        out_shape=(jax.ShapeDtypeStruct((L, 768), F32), jax.ShapeDtypeStruct((1, MLA_Q_LORA), F32),
                   jax.ShapeDtypeStruct((1, MLA_KV_LORA), F32)),
        compiler_params=pltpu.CompilerParams(dimension_semantics=("arbitrary",)),
    )(P, dnq, dnkv, dkr, gq.reshape(1, -1), gkv.reshape(1, -1), cs, sn)


def _tri(kind):
    r = lax.broadcasted_iota(jnp.int32, (BLK, BLK), 0)
    c = lax.broadcasted_iota(jnp.int32, (BLK, BLK), 1)
    t = ((r > c) if kind == "right" else (r < c)).astype(BF16)
    return jnp.concatenate([t, t], axis=0)


def _tri_sum(x, tt):
    hi = x.astype(BF16)
    lo = (x - hi.astype(F32)).astype(BF16)
    return _dot(jnp.concatenate([hi, lo], axis=1), tt)


def _sb_tile(q, k, i, j, tt_right, R):
    row = lax.broadcasted_iota(jnp.int32, (BLK, BLK), 0)
    col = lax.broadcasted_iota(jnp.int32, (BLK, BLK), 1)
    s_idx = j * BLK + col
    mask = (s_idx < i * BLK + row) & (s_idx >= N_PAD)
    z = _dot_nt(q, k)
    sp = jnp.maximum(z, 0.0) + jnp.log1p(jnp.exp(-jnp.abs(z)))
    lk = jnp.where(mask, -sp, 0.0)
    E = _tri_sum(lk, tt_right) + R
    return mask, z, sp, lk, E


def _old_sb_fwd(P, *, name):
    L = P.shape[0]
    nb = L // BLK

    def body(q_ref, k_ref, v_ref, o_ref, c_ref):
        i = pl.program_id(1)
        tt = _tri("right")
        lane = lax.broadcasted_iota(jnp.int32, (BLK, 128), 1)
        qs = [(q_ref[:, 64 * h:64 * h + 64] * 0.125).astype(BF16) for h in range(2)]

        def step(jj, carry):
            j = i - jj
            off = pl.multiple_of(j * BLK, BLK)
            kb = k_ref[pl.ds(off, BLK), :].astype(BF16)
            vb = v_ref[pl.ds(off, BLK), :].astype(BF16)
            out = []
            for h in range(2):
                o, R = carry[2 * h], carry[2 * h + 1]
                sl = slice(64 * h, 64 * h + 64)
                mask, z, sp, lk, E = _sb_tile(qs[h], kb[:, sl], i, j, tt, R)
                w = jnp.where(mask, jnp.exp(z - sp + E), 0.0)
                c_ref[h] = jnp.where(lane == j, R, c_ref[h])
                out += [o + _dot(w.astype(BF16), vb[:, sl]), R + jnp.sum(lk, axis=1, keepdims=True)]
            return tuple(out)

        c_ref[...] = jnp.zeros_like(c_ref)
        z0 = (jnp.zeros((BLK, 64), F32), jnp.zeros((BLK, 1), F32))
        res = lax.fori_loop(0, i + 1, step, z0 + z0)
        o_ref[...] = jnp.concatenate([res[0], res[2]], axis=1)

    return pl.pallas_call(
        body, name=name, grid=(SB_HEADS // 2, nb),
        in_specs=[pl.BlockSpec((BLK, 128), lambda hp, i: (i, C_SBQ // 128 + hp)),
                  pl.BlockSpec((L, 128), lambda hp, i: (0, C_SBK // 128 + hp)),
                  pl.BlockSpec((L, 128), lambda hp, i: (0, C_SBV // 128 + hp))],
        out_specs=(pl.BlockSpec((BLK, 128), lambda hp, i: (i, hp)), pl.BlockSpec((2, BLK, 128), lambda hp, i: (hp, i, 0))),
        out_shape=(jax.ShapeDtypeStruct((L, 512), F32), jax.ShapeDtypeStruct((SB_HEADS, L, 128), F32)),
        compiler_params=pltpu.CompilerParams(dimension_semantics=("parallel", "arbitrary")),
    )(P, P, P)


def _old_sb_bwd(P, carries, dmixed, *, name):
    L = P.shape[0]
    nb = L // BLK

    def body(q_ref, k_ref, v_ref, c_ref, do_ref, dq_ref, dk_ref, dv_ref):
        i = pl.program_id(1)

        @pl.when(i == 0)
        def _():
            dk_ref[...] = jnp.zeros_like(dk_ref)
            dv_ref[...] = jnp.zeros_like(dv_ref)

        tr = _tri("right")
        tl = _tri("left")
        lane = lax.broadcasted_iota(jnp.int32, (BLK, 128), 1)
        qs = [(q_ref[:, 64 * h:64 * h + 64] * 0.125).astype(BF16) for h in range(2)]
        dos = [do_ref[:, 64 * h:64 * h + 64].astype(BF16) for h in range(2)]

        def step(j, carry):
            off = pl.multiple_of(j * BLK, BLK)
            kb = k_ref[pl.ds(off, BLK), :].astype(BF16)
            vb = v_ref[pl.ds(off, BLK), :].astype(BF16)
            out, dks, dvs = [], [], []
            for h in range(2):
                dq, PL = carry[2 * h], carry[2 * h + 1]
                sl = slice(64 * h, 64 * h + 64)
                R = jnp.sum(jnp.where(lane == j, c_ref[h], 0.0), axis=1, keepdims=True)
                mask, z, sp, lk, E = _sb_tile(qs[h], kb[:, sl], i, j, tr, R)
                sig = jnp.exp(z - sp)
                w = jnp.where(mask, sig * jnp.exp(E), 0.0)
                dA = _dot_nt(dos[h], vb[:, sl]) * w
                Pp = _tri_sum(dA, tl) + PL
                dz = jnp.where(mask, dA - sig * (dA + Pp), 0.0).astype(BF16)
                dks.append(_dot_tn(dz, qs[h]))
                dvs.append(_dot_tn(w.astype(BF16), dos[h]))
                out += [dq + _dot(dz, kb[:, sl]), PL + jnp.sum(dA, axis=1, keepdims=True)]
            dk_ref[pl.ds(off, BLK), :] += jnp.concatenate(dks, axis=1)
            dv_ref[pl.ds(off, BLK), :] += jnp.concatenate(dvs, axis=1)
            return tuple(out)

        z0 = (jnp.zeros((BLK, 64), F32), jnp.zeros((BLK, 1), F32))
        res = lax.fori_loop(0, i + 1, step, z0 + z0)
        dq_ref[...] = jnp.concatenate([res[0], res[2]], axis=1) * 0.125

    blk = lambda c0: pl.BlockSpec((BLK, 128), lambda hp, i: (i, c0 + hp))
    full = lambda c0: pl.BlockSpec((L, 128), lambda hp, i: (0, c0 + hp))
    sds = jax.ShapeDtypeStruct((L, 512), F32)
    return pl.pallas_call(
        body, name=name, grid=(SB_HEADS // 2, nb),
        in_specs=[blk(C_SBQ // 128), full(C_SBK // 128), full(C_SBV // 128),
                  pl.BlockSpec((2, BLK, 128), lambda hp, i: (hp, i, 0)), blk(0)],
        out_specs=(blk(0), full(0), full(0)), out_shape=(sds, sds, sds),
        compiler_params=pltpu.CompilerParams(dimension_semantics=("parallel", "arbitrary")),
    )(P, P, P, carries, dmixed)


def _mla_mask(i, j):
    row = lax.broadcasted_iota(jnp.int32, (BLK, BLK), 0)
    col = lax.broadcasted_iota(jnp.int32, (BLK, BLK), 1)
    t_idx = i * BLK + row
    s_idx = j * BLK + col
    return (s_idx <= t_idx) & ((s_idx >= N_PAD) | (s_idx == t_idx))


def _mla_q(q_ref, cs_ref, sn_ref):
    qr = _rope(q_ref[:, 128:256], cs_ref[...], sn_ref[...], MLA_ROPE // 2)
    qn = [q_ref[:, 64 * h:64 * h + 64].astype(BF16) for h in range(2)]
    qrs = [qr[:, 32 * h:32 * h + 32].astype(BF16) for h in range(2)]
    return qn, qrs


def _old_mla_fwd(Q, KV, KR, cs, sn, *, name):
    L = Q.shape[0]
    nb = L // BLK

    def body(q_ref, kn_ref, v_ref, kr_ref, cs_ref, sn_ref, o_ref, lse_ref):
        i = pl.program_id(1)
        qn, qrs = _mla_q(q_ref, cs_ref, sn_ref)

        def step(j, carry):
            off = pl.multiple_of(j * BLK, BLK)
            knb = kn_ref[pl.ds(off, BLK), :]
            vb = v_ref[pl.ds(off, BLK), :]
            krb = kr_ref[pl.ds(off, BLK), 0:MLA_ROPE]
            mask = _mla_mask(i, j)
            out = []
            for h in range(2):
                m, l, acc = carry[3 * h], carry[3 * h + 1], carry[3 * h + 2]
                sl = slice(64 * h, 64 * h + 64)
                s = (_dot_nt(qn[h], knb[:, sl]) + _dot_nt(qrs[h], krb)) * MLA_SCALE
                s = jnp.where(mask, s, NEG)
                m_new = jnp.maximum(m, jnp.max(s, axis=1, keepdims=True))
                a = jnp.exp(m - m_new)
                p = jnp.exp(s - m_new)
                out += [m_new, a * l + jnp.sum(p, axis=1, keepdims=True), a * acc + _dot(p.astype(BF16), vb[:, sl])]
            return tuple(out)

        z0 = (jnp.full((BLK, 1), NEG, F32), jnp.zeros((BLK, 1), F32), jnp.zeros((BLK, 64), F32))
        res = lax.fori_loop(0, i + 1, step, z0 + z0)
        o_ref[...] = jnp.concatenate([res[2] / res[1], res[5] / res[4]], axis=1)
        lane = lax.broadcasted_iota(jnp.int32, (BLK, 128), 1)
        lse0 = res[0] + jnp.log(res[1])
        lse1 = res[3] + jnp.log(res[4])
        lse_ref[0] = jnp.where(lane == 0, lse0, jnp.where(lane == 1, lse1, 0.0))

    return pl.pallas_call(
        body, name=name, grid=(MLA_HEADS // 2, nb),
        in_specs=[pl.BlockSpec((BLK, 256), lambda hp, i: (i, hp)),
                  pl.BlockSpec((L, 128), lambda hp, i: (0, hp)),
                  pl.BlockSpec((L, 128), lambda hp, i: (0, 4 + hp)),
                  pl.BlockSpec((L, 128), lambda hp, i: (0, 0)),
                  pl.BlockSpec((BLK, 128), lambda hp, i: (i, 0)), pl.BlockSpec((BLK, 128), lambda hp, i: (i, 0))],
        out_specs=(pl.BlockSpec((BLK, 128), lambda hp, i: (i, hp)), pl.BlockSpec((1, BLK, 128), lambda hp, i: (hp, i, 0))),
        out_shape=(jax.ShapeDtypeStruct((L, 512), F32), jax.ShapeDtypeStruct((4, L, 128), F32)),
        compiler_params=pltpu.CompilerParams(dimension_semantics=("parallel", "arbitrary")),
    )(Q, KV, KV, KR, cs, sn)


def _old_mla_bwd(Q, KV, KR, cs, sn, mixed, dmixed, lse, *, name):
    L = Q.shape[0]
    nb = L // BLK

    def body(q_ref, kn_ref, v_ref, kr_ref, cs_ref, sn_ref, o_ref, do_ref, lse_ref, dq_ref, dkn_ref, dv_ref, dkr_ref):
        hp = pl.program_id(0)
        i = pl.program_id(1)

        @pl.when(i == 0)
        def _():
            dkn_ref[...] = jnp.zeros_like(dkn_ref)
            dv_ref[...] = jnp.zeros_like(dv_ref)

        @pl.when((i == 0) & (hp == 0))
        def _():
            dkr_ref[...] = jnp.zeros_like(dkr_ref)

        qn, qrs = _mla_q(q_ref, cs_ref, sn_ref)
        dos, dd, lses = [], [], []
        for h in range(2):
            sl = slice(64 * h, 64 * h + 64)
            d = do_ref[:, sl]
            dos.append(d.astype(BF16))
            dd.append(jnp.sum(d * o_ref[:, sl], axis=1, keepdims=True))
            lses.append(lse_ref[0, :, h:h + 1])

        def step(j, carry):
            off = pl.multiple_of(j * BLK, BLK)
            knb = kn_ref[pl.ds(off, BLK), :]
            vb = v_ref[pl.ds(off, BLK), :]
            krb = kr_ref[pl.ds(off, BLK), 0:MLA_ROPE]
            mask = _mla_mask(i, j)
            out, dkns, dvs = [], [], []
            dkr = jnp.zeros((BLK, MLA_ROPE), F32)
            for h in range(2):
                dqn, dqr = carry[2 * h], carry[2 * h + 1]
                sl = slice(64 * h, 64 * h + 64)
                s = (_dot_nt(qn[h], knb[:, sl]) + _dot_nt(qrs[h], krb)) * MLA_SCALE
                p = jnp.where(mask, jnp.exp(s - lses[h]), 0.0)
                dp = _dot_nt(dos[h], vb[:, sl])
                ds = (p * (dp - dd[h]) * MLA_SCALE).astype(BF16)
                dkns.append(_dot_tn(ds, qn[h]))
                dvs.append(_dot_tn(p.astype(BF16), dos[h]))
                dkr = dkr + _dot_tn(ds, qrs[h])
                out += [dqn + _dot(ds, knb[:, sl]), dqr + _dot(ds, krb)]
            dkn_ref[pl.ds(off, BLK), :] += jnp.concatenate(dkns, axis=1)
            dv_ref[pl.ds(off, BLK), :] += jnp.concatenate(dvs, axis=1)
            dkr_ref[pl.ds(off, BLK), :] += jnp.concatenate([dkr, jnp.zeros((BLK, 128 - MLA_ROPE), F32)], axis=1)
            return tuple(out)

        z0 = (jnp.zeros((BLK, 64), F32), jnp.zeros((BLK, MLA_ROPE), F32))
        res = lax.fori_loop(0, i + 1, step, z0 + z0)
        dqr = jnp.concatenate([res[1], res[3], jnp.zeros((BLK, 64), F32)], axis=1)
        dq_ref[...] = jnp.concatenate([res[0], res[2], _rope_t(dqr, cs_ref[...], sn_ref[...], MLA_ROPE // 2)], axis=1)

    blk = lambda c0: pl.BlockSpec((BLK, 128), lambda hp, i: (i, c0 + hp))
    full = lambda c0: pl.BlockSpec((L, 128), lambda hp, i: (0, c0 + hp))
    tab = pl.BlockSpec((BLK, 128), lambda hp, i: (i, 0))
    return pl.pallas_call(
        body, name=name, grid=(MLA_HEADS // 2, nb),
        in_specs=[pl.BlockSpec((BLK, 256), lambda hp, i: (i, hp)), full(0), full(4),
                  pl.BlockSpec((L, 128), lambda hp, i: (0, 0)), tab, tab, blk(4), blk(4),
                  pl.BlockSpec((1, BLK, 128), lambda hp, i: (hp, i, 0))],
        out_specs=(pl.BlockSpec((BLK, 256), lambda hp, i: (i, hp)), full(0), full(0),
                   pl.BlockSpec((L, 128), lambda hp, i: (0, 0))),
        out_shape=(jax.ShapeDtypeStruct((L, 1024), F32), jax.ShapeDtypeStruct((L, 512), F32),
                   jax.ShapeDtypeStruct((L, 512), F32), jax.ShapeDtypeStruct((L, 128), F32)),
        compiler_params=pltpu.CompilerParams(dimension_semantics=("arbitrary", "arbitrary")),
    )(Q, KV, KV, KR, cs, sn, mixed, dmixed, lse)


SB_UNROLL = 2


def _tq(L):
    return 384 if L % 384 == 0 else BLK


def _softplus(z):
    return jnp.maximum(z, 0.0) + jnp.log(1.0 + jnp.exp(-jnp.abs(z)))


def _head_split(x, first):
    zero = jnp.zeros_like(x)
    return jnp.where(first, x, zero), jnp.where(first, zero, x)


def _sb_mask(I, j, tq):
    row = lax.broadcasted_iota(jnp.int32, (tq, BLK), 0)
    col = lax.broadcasted_iota(jnp.int32, (tq, BLK), 1)
    s_idx = j * BLK + col
    return (s_idx < I * tq + row) & (s_idx >= N_PAD)


def _tri2(kind, splits):
    r = lax.broadcasted_iota(jnp.int32, (256, 256), 0)
    c = lax.broadcasted_iota(jnp.int32, (256, 256), 1)
    same = (r < BLK) == (c < BLK)
    t = (same & ((r > c) if kind == "right" else (r < c))).astype(BF16)
    return jnp.concatenate([t] * splits, axis=0)


def _split2(x):
    hi = x.astype(BF16)
    lo = (x - hi.astype(F32)).astype(BF16)
    return jnp.concatenate([hi, lo], axis=1)


def _sb_mask2(I, j, tq):
    row = lax.broadcasted_iota(jnp.int32, (tq, 256), 0)
    col = lax.broadcasted_iota(jnp.int32, (tq, 256), 1)
    s_idx = j * BLK + (col & (BLK - 1))
    return (s_idx < I * tq + row) & (s_idx >= N_PAD)


def _per_head(x, r0, r1):
    return jnp.concatenate([x[:, 0:BLK] + r0, x[:, BLK:2 * BLK] + r1], axis=1)


def sb_fwd(P, *, name):
    L = P.shape[0]
    tq = _tq(L)
    nd = tq // BLK

    def body(q_ref, k_ref, v_ref, o_ref, c_ref):
        I = pl.program_id(1)
        tt = _tri2("right", 2)
        lane_q = lax.broadcasted_iota(jnp.int32, (tq, 128), 1)
        first_k = lax.broadcasted_iota(jnp.int32, (BLK, 128), 1) < 64
        q = (q_ref[...] * 0.125).astype(BF16)
        c_ref[...] = jnp.zeros_like(c_ref)

        def tiles(T, carry, kind):
            o, R0, R1 = carry
            js = [T * nd + nd - 1 - u for u in range(nd)]
            st = []
            for j in js:
                off = pl.multiple_of(j * BLK, BLK)
                kcat = jnp.concatenate(_head_split(k_ref[pl.ds(off, BLK), :].astype(BF16), first_k), axis=0)
                st.append([_dot_nt(q, kcat), off])
            for u, (s, j) in enumerate(zip(st, js)):
                sp = _softplus(s[0])
                mask = _sb_mask2(I, j, tq) if kind == "diag" else (pad_ok if kind == "first" and u == nd - 1 else None)
                spm = sp if mask is None else jnp.where(mask, sp, 0.0)
                s += [sp, spm, mask, _dot(_split2(spm), tt)]
            for (z, off, sp, spm, mask, S), j in zip(st, js):
                vcat = jnp.concatenate(_head_split(v_ref[pl.ds(off, BLK), :].astype(BF16), first_k), axis=0)
                w = jnp.exp(_per_head(z - sp - S, R0, R1))
                if mask is not None:
                    w = jnp.where(mask, w, 0.0)
                c_ref[0] = jnp.where(lane_q == j, R0, c_ref[0])
                c_ref[1] = jnp.where(lane_q == j, R1, c_ref[1])
                tot = S + spm
                o, R0, R1 = o + _dot(w.astype(BF16), vcat), R0 - tot[:, 0:1], R1 - tot[:, BLK:BLK + 1]
            return (o, R0, R1)

        pad_ok = (lax.broadcasted_iota(jnp.int32, (tq, 256), 1) & (BLK - 1)) >= N_PAD
        carry = (jnp.zeros((tq, 128), F32), jnp.zeros((tq, 1), F32), jnp.zeros((tq, 1), F32))
        carry = tiles(I, carry, "diag")
        carry = lax.fori_loop(0, jnp.maximum(I - 1, 0), lambda t, c: tiles(I - 1 - t, c, None), carry)
        carry = lax.fori_loop(0, jnp.minimum(I, 1), lambda t, c: tiles(0, c, "first"), carry)
        o_ref[...] = carry[0]

    return pl.pallas_call(
        body, name=name, grid=(SB_HEADS // 2, L // tq),
        in_specs=[pl.BlockSpec((tq, 128), lambda hp, i: (i, C_SBQ // 128 + hp)),
                  pl.BlockSpec((L, 128), lambda hp, i: (0, C_SBK // 128 + hp)),
                  pl.BlockSpec((L, 128), lambda hp, i: (0, C_SBV // 128 + hp))],
        out_specs=(pl.BlockSpec((tq, 128), lambda hp, i: (i, hp)), pl.BlockSpec((2, tq, 128), lambda hp, i: (hp, i, 0))),
        out_shape=(jax.ShapeDtypeStruct((L, 512), F32), jax.ShapeDtypeStruct((SB_HEADS, L, 128), F32)),
        compiler_params=pltpu.CompilerParams(dimension_semantics=("parallel", "arbitrary")),
    )(P, P, P)


def sb_bwd(P, carries, dmixed, *, name):
    L = P.shape[0]
    tq = _tq(L)
    nd = tq // BLK

    def body(q_ref, k_ref, v_ref, c_ref, do_ref, dq_ref, dk_ref, dv_ref):
        I = pl.program_id(1)

        @pl.when(I == 0)
        def _():
            dk_ref[...] = jnp.zeros_like(dk_ref)
            dv_ref[...] = jnp.zeros_like(dv_ref)

        tr = _tri2("right", 2)
        tl = _tri2("left", 1)
        lane_q = lax.broadcasted_iota(jnp.int32, (tq, 128), 1)
        first_k = lax.broadcasted_iota(jnp.int32, (BLK, 128), 1) < 64
        q = (q_ref[...] * 0.125).astype(BF16)
        do = do_ref[...].astype(BF16)

        def tiles(T, carry, kind):
            dq, PL0, PL1 = carry
            js = [T * nd + u for u in range(nd)]
            st = []
            for j in js:
                off = pl.multiple_of(j * BLK, BLK)
                kcat = jnp.concatenate(_head_split(k_ref[pl.ds(off, BLK), :].astype(BF16), first_k), axis=0)
                vcat = jnp.concatenate(_head_split(v_ref[pl.ds(off, BLK), :].astype(BF16), first_k), axis=0)
                st.append([off, kcat, _dot_nt(q, kcat), _dot_nt(do, vcat)])
            for u, (s, j) in enumerate(zip(st, js)):
                z = s[2]
                sp = _softplus(z)
                mask = _sb_mask2(I, j, tq) if kind == "diag" else (pad_ok if kind == "first" and u == 0 else None)
                spm = sp if mask is None else jnp.where(mask, sp, 0.0)
                s += [mask, jnp.exp(z - sp), _dot(_split2(spm), tr)]
            for s, j in zip(st, js):
                off, kcat, z, dw, mask, sig, S = s
                R0 = jnp.sum(jnp.where(lane_q == j, c_ref[0], 0.0), axis=1, keepdims=True)
                R1 = jnp.sum(jnp.where(lane_q == j, c_ref[1], 0.0), axis=1, keepdims=True)
                w = sig * jnp.exp(_per_head(-S, R0, R1))
                if mask is not None:
                    w = jnp.where(mask, w, 0.0)
                dA = dw * w
                dvf = _dot_tn(w.astype(BF16), do)
                dv_ref[pl.ds(off, BLK), :] += jnp.where(first_k, dvf[0:BLK], dvf[BLK:2 * BLK])
                s += [dA, _dot(dA.astype(BF16), tl)]
            for off, kcat, z, dw, mask, sig, S, dA, pre in st:
                dz = dA - sig * (dA + _per_head(pre, PL0, PL1))
                if mask is not None:
                    dz = jnp.where(mask, dz, 0.0)
                dzb = dz.astype(BF16)
                dkf = _dot_tn(dzb, q)
                dk_ref[pl.ds(off, BLK), :] += jnp.where(first_k, dkf[0:BLK], dkf[BLK:2 * BLK])
                tot = pre + dA
                dq, PL0, PL1 = dq + _dot(dzb, kcat), PL0 + tot[:, BLK - 1:BLK], PL1 + tot[:, 2 * BLK - 1:2 * BLK]
            return (dq, PL0, PL1)

        pad_ok = (lax.broadcasted_iota(jnp.int32, (tq, 256), 1) & (BLK - 1)) >= N_PAD
        carry = (jnp.zeros((tq, 128), F32), jnp.zeros((tq, 1), F32), jnp.zeros((tq, 1), F32))
        carry = lax.fori_loop(0, jnp.minimum(I, 1), lambda t, c: tiles(0, c, "first"), carry)
        carry = lax.fori_loop(1, jnp.maximum(I, 1), lambda T, c: tiles(T, c, None), carry)
        carry = tiles(I, carry, "diag")
        dq_ref[...] = carry[0] * 0.125

    blk = lambda c0: pl.BlockSpec((tq, 128), lambda hp, i: (i, c0 + hp))
    full = lambda c0: pl.BlockSpec((L, 128), lambda hp, i: (0, c0 + hp))
    sds = jax.ShapeDtypeStruct((L, 512), F32)
    return pl.pallas_call(
        body, name=name, grid=(SB_HEADS // 2, L // tq),
        in_specs=[blk(C_SBQ // 128), full(C_SBK // 128), full(C_SBV // 128),
                  pl.BlockSpec((2, tq, 128), lambda hp, i: (hp, i, 0)), blk(0)],
        out_specs=(blk(0), full(0), full(0)), out_shape=(sds, sds, sds),
        compiler_params=pltpu.CompilerParams(dimension_semantics=("parallel", "arbitrary")),
    )(P, P, P, carries, dmixed)


def _v2_sb_fwd(P, *, name):
    L = P.shape[0]
    tq = _tq(L)
    nd = tq // BLK

    def body(q_ref, k_ref, v_ref, o_ref, c_ref):
        I = pl.program_id(1)
        tt = _tri("right")
        lane_q = lax.broadcasted_iota(jnp.int32, (tq, 128), 1)
        first_k = lax.broadcasted_iota(jnp.int32, (BLK, 128), 1) < 64
        qm = [x.astype(BF16) for x in _head_split(q_ref[...] * 0.125, lane_q < 64)]
        c_ref[...] = jnp.zeros_like(c_ref)

        def tile(j, carry, masked):
            o, R = carry[0], carry[1:]
            off = pl.multiple_of(j * BLK, BLK)
            kb = k_ref[pl.ds(off, BLK), :].astype(BF16)
            vcat = jnp.concatenate(_head_split(v_ref[pl.ds(off, BLK), :].astype(BF16), first_k), axis=0)
            mask = _sb_mask(I, j, tq) if masked else None
            ws, Rn = [], []
            for h in range(2):
                z = _dot_nt(qm[h], kb)
                sp = _softplus(z)
                spm = jnp.where(mask, sp, 0.0) if masked else sp
                w = jnp.exp(z - sp - _tri_sum(spm, tt) + R[h])
                if masked:
                    w = jnp.where(mask, w, 0.0)
                c_ref[h] = jnp.where(lane_q == j, R[h], c_ref[h])
                ws.append(w.astype(BF16))
                Rn.append(R[h] - jnp.sum(spm, axis=1, keepdims=True))
            return (o + _dot(jnp.concatenate(ws, axis=1), vcat), Rn[0], Rn[1])

        carry = (jnp.zeros((tq, 128), F32), jnp.zeros((tq, 1), F32), jnp.zeros((tq, 1), F32))
        carry = lax.fori_loop(0, nd, lambda t, c: tile(I * nd + nd - 1 - t, c, True), carry)
        carry = lax.fori_loop(0, jnp.maximum(I * nd - 1, 0), lambda t, c: tile(I * nd - 1 - t, c, False), carry)
        carry = lax.fori_loop(0, jnp.minimum(I, 1), lambda t, c: tile(0, c, True), carry)
        o_ref[...] = carry[0]

    return pl.pallas_call(
        body, name=name, grid=(SB_HEADS // 2, L // tq),
        in_specs=[pl.BlockSpec((tq, 128), lambda hp, i: (i, C_SBQ // 128 + hp)),
                  pl.BlockSpec((L, 128), lambda hp, i: (0, C_SBK // 128 + hp)),
                  pl.BlockSpec((L, 128), lambda hp, i: (0, C_SBV // 128 + hp))],
        out_specs=(pl.BlockSpec((tq, 128), lambda hp, i: (i, hp)), pl.BlockSpec((2, tq, 128), lambda hp, i: (hp, i, 0))),
        out_shape=(jax.ShapeDtypeStruct((L, 512), F32), jax.ShapeDtypeStruct((SB_HEADS, L, 128), F32)),
        compiler_params=pltpu.CompilerParams(dimension_semantics=("parallel", "arbitrary")),
    )(P, P, P)


def _v2_sb_bwd(P, carries, dmixed, *, name):
    L = P.shape[0]
    tq = _tq(L)
    nd = tq // BLK

    def body(q_ref, k_ref, v_ref, c_ref, do_ref, dq_ref, dk_ref, dv_ref):
        I = pl.program_id(1)

        @pl.when(I == 0)
        def _():
            dk_ref[...] = jnp.zeros_like(dk_ref)
            dv_ref[...] = jnp.zeros_like(dv_ref)

        tr = _tri("right")
        tl = _tri("left")
        lane_q = lax.broadcasted_iota(jnp.int32, (tq, 128), 1)
        first_k = lax.broadcasted_iota(jnp.int32, (BLK, 128), 1) < 64
        qm = [x.astype(BF16) for x in _head_split(q_ref[...] * 0.125, lane_q < 64)]
        dom = [x.astype(BF16) for x in _head_split(do_ref[...], lane_q < 64)]
        qcat = jnp.concatenate(qm, axis=0)
        docat = jnp.concatenate(dom, axis=0)

        def tile(j, carry, masked):
            dq, PL = carry[0], carry[1:]
            off = pl.multiple_of(j * BLK, BLK)
            kb = k_ref[pl.ds(off, BLK), :].astype(BF16)
            vb = v_ref[pl.ds(off, BLK), :].astype(BF16)
            kcat = jnp.concatenate(_head_split(kb, first_k), axis=0)
            mask = _sb_mask(I, j, tq) if masked else None
            dzs, wsb, PLn = [], [], []
            for h in range(2):
                R = jnp.sum(jnp.where(lane_q == j, c_ref[h], 0.0), axis=1, keepdims=True)
                z = _dot_nt(qm[h], kb)
                sp = _softplus(z)
                spm = jnp.where(mask, sp, 0.0) if masked else sp
                sig = jnp.exp(z - sp)
                w = sig * jnp.exp(R - _tri_sum(spm, tr))
                if masked:
                    w = jnp.where(mask, w, 0.0)
                dA = _dot_nt(dom[h], vb) * w
                dz = dA - sig * (dA + _tri_sum(dA, tl) + PL[h])
                if masked:
                    dz = jnp.where(mask, dz, 0.0)
                dzs.append(dz.astype(BF16))
                wsb.append(w.astype(BF16))
                PLn.append(PL[h] + jnp.sum(dA, axis=1, keepdims=True))
            dk_ref[pl.ds(off, BLK), :] += _dot_tn(jnp.concatenate(dzs, axis=0), qcat)
            dv_ref[pl.ds(off, BLK), :] += _dot_tn(jnp.concatenate(wsb, axis=0), docat)
            return (dq + _dot(jnp.concatenate(dzs, axis=1), kcat), PLn[0], PLn[1])

        carry = (jnp.zeros((tq, 128), F32), jnp.zeros((tq, 1), F32), jnp.zeros((tq, 1), F32))
        carry = lax.fori_loop(0, jnp.minimum(I, 1), lambda t, c: tile(0, c, True), carry)
        carry = lax.fori_loop(1, jnp.maximum(I * nd, 1), lambda j, c: tile(j, c, False), carry)
        carry = lax.fori_loop(0, nd, lambda t, c: tile(I * nd + t, c, True), carry)
        dq_ref[...] = carry[0] * 0.125

    blk = lambda c0: pl.BlockSpec((tq, 128), lambda hp, i: (i, c0 + hp))
    full = lambda c0: pl.BlockSpec((L, 128), lambda hp, i: (0, c0 + hp))
    sds = jax.ShapeDtypeStruct((L, 512), F32)
    return pl.pallas_call(
        body, name=name, grid=(SB_HEADS // 2, L // tq),
        in_specs=[blk(C_SBQ // 128), full(C_SBK // 128), full(C_SBV // 128),
                  pl.BlockSpec((2, tq, 128), lambda hp, i: (hp, i, 0)), blk(0)],
        out_specs=(blk(0), full(0), full(0)), out_shape=(sds, sds, sds),
        compiler_params=pltpu.CompilerParams(dimension_semantics=("parallel", "arbitrary")),
    )(P, P, P, carries, dmixed)


def _mla_mask2(I, j, tq):
    row = lax.broadcasted_iota(jnp.int32, (tq, tq), 0)
    col = lax.broadcasted_iota(jnp.int32, (tq, tq), 1)
    t_idx = I * tq + row
    s_idx = j * tq + col
    return (s_idx <= t_idx) & ((s_idx >= N_PAD) | (s_idx == t_idx))


def _mla_qcat(q_ref, cs_ref, sn_ref, lane_q):
    qn = q_ref[:, 0:128]
    qr = _rope(q_ref[:, 128:256], cs_ref[...], sn_ref[...], MLA_ROPE // 2)
    zero = jnp.zeros_like(qn)
    r0 = lane_q < MLA_ROPE
    r1 = (lane_q >= MLA_ROPE) & (lane_q < 2 * MLA_ROPE)
    n0, n1 = _head_split(qn, lane_q < 64)
    return [jnp.concatenate([n0, jnp.where(r0, qr, zero)], axis=1).astype(BF16),
            jnp.concatenate([n1, jnp.where(r1, qr, zero)], axis=1).astype(BF16)]


def mla_fwd(Q, KV, KR, cs, sn, *, name):
    L = Q.shape[0]
    tq = _tq(L)

    def body(q_ref, kn_ref, v_ref, kr_ref, cs_ref, sn_ref, o_ref, lse_ref):
        I = pl.program_id(1)
        lane_q = lax.broadcasted_iota(jnp.int32, (tq, 128), 1)
        first_q = lane_q < 64
        qcat = _mla_qcat(q_ref, cs_ref, sn_ref, lane_q)

        def tile(j, carry, masked, wide=1):
            acc, ml = carry[0], carry[1:]
            off = pl.multiple_of(j * tq, tq)
            tk = wide * tq
            first_k = lax.broadcasted_iota(jnp.int32, (tk, 128), 1) < 64
            kcat = jnp.concatenate([kn_ref[pl.ds(off, tk), :], kr_ref[pl.ds(off, tk), :]], axis=1)
            vcat = jnp.concatenate(_head_split(v_ref[pl.ds(off, tk), :], first_k), axis=0)
            mask = _mla_mask2(I, j, tq) if masked else None
            ps, al, out = [], [], []
            for h in range(2):
                m, l = ml[2 * h], ml[2 * h + 1]
                s = _dot_nt(qcat[h], kcat) * MLA_SCALE
                if masked:
                    s = jnp.where(mask, s, NEG)
                m_new = jnp.maximum(m, jnp.max(s, axis=1, keepdims=True))
                a = jnp.exp(m - m_new)
                p = jnp.exp(s - m_new)
                ps.append(p.astype(BF16))
                al.append(a)
                out += [m_new, a * l + jnp.sum(p, axis=1, keepdims=True)]
            acc = acc * jnp.where(first_q, al[0], al[1]) + _dot(jnp.concatenate(ps, axis=1), vcat)
            return (acc,) + tuple(out)

        ml0 = (jnp.full((tq, 1), NEG, F32), jnp.zeros((tq, 1), F32))
        carry = (jnp.zeros((tq, 128), F32),) + ml0 + ml0
        carry = lax.fori_loop(0, jnp.minimum(I, 1), lambda t, c: tile(0, c, True), carry)
        n_in = jnp.maximum(I - 1, 0)
        carry = lax.fori_loop(0, n_in // 2, lambda t, c: tile(1 + 2 * t, c, False, 2), carry)
        carry = lax.fori_loop(0, n_in % 2, lambda t, c: tile(I - 1, c, False), carry)
        carry = tile(I, carry, True)
        acc, m0, l0, m1, l1 = carry
        o_ref[...] = acc / jnp.where(first_q, l0, l1)
        lse_ref[0] = jnp.where(lane_q == 0, m0 + jnp.log(l0), jnp.where(lane_q == 1, m1 + jnp.log(l1), 0.0))

    return pl.pallas_call(
        body, name=name, grid=(MLA_HEADS // 2, L // tq),
        in_specs=[pl.BlockSpec((tq, 256), lambda hp, i: (i, hp)),
                  pl.BlockSpec((L, 128), lambda hp, i: (0, hp)),
                  pl.BlockSpec((L, 128), lambda hp, i: (0, 4 + hp)),
                  pl.BlockSpec((L, 128), lambda hp, i: (0, 0)),
                  pl.BlockSpec((tq, 128), lambda hp, i: (i, 0)), pl.BlockSpec((tq, 128), lambda hp, i: (i, 0))],
        out_specs=(pl.BlockSpec((tq, 128), lambda hp, i: (i, hp)), pl.BlockSpec((1, tq, 128), lambda hp, i: (hp, i, 0))),
        out_shape=(jax.ShapeDtypeStruct((L, 512), F32), jax.ShapeDtypeStruct((4, L, 128), F32)),
        compiler_params=pltpu.CompilerParams(dimension_semantics=("parallel", "arbitrary")),
    )(Q, KV, KV, KR, cs, sn)


def mla_bwd(Q, KV, KR, cs, sn, mixed, dmixed, lse, *, name):
    L = Q.shape[0]
    tq = _tq(L)

    def body(q_ref, kn_ref, v_ref, kr_ref, cs_ref, sn_ref, o_ref, do_ref, lse_ref, dq_ref, dkn_ref, dv_ref, dkr_ref):
        hp = pl.program_id(0)
        I = pl.program_id(1)

        @pl.when(I == 0)
        def _():
            dkn_ref[...] = jnp.zeros_like(dkn_ref)
            dv_ref[...] = jnp.zeros_like(dv_ref)

        @pl.when((I == 0) & (hp == 0))
        def _():
            dkr_ref[...] = jnp.zeros_like(dkr_ref)

        lane_q = lax.broadcasted_iota(jnp.int32, (tq, 128), 1)
        first_q = lane_q < 64
        qcat = _mla_qcat(q_ref, cs_ref, sn_ref, lane_q)
        qq = jnp.concatenate(qcat, axis=0)
        do = do_ref[...]
        prod = do * o_ref[...]
        dd = [jnp.sum(jnp.where(first_q, prod, 0.0), axis=1, keepdims=True),
              jnp.sum(jnp.where(first_q, 0.0, prod), axis=1, keepdims=True)]
        dom = [x.astype(BF16) for x in _head_split(do, first_q)]
        docat = jnp.concatenate(dom, axis=0)
        lses = [lse_ref[0, :, 0:1], lse_ref[0, :, 1:2]]

        def tile(j, dq, masked, wide=1):
            off = pl.multiple_of(j * tq, tq)
            tk = wide * tq
            lane_k = lax.broadcasted_iota(jnp.int32, (tk, 256), 1)
            sel0 = (lane_k < 64) | ((lane_k >= 128) & (lane_k < 128 + MLA_ROPE))
            sel1 = ((lane_k >= 64) & (lane_k < 128)) | ((lane_k >= 128 + MLA_ROPE) & (lane_k < 128 + 2 * MLA_ROPE))
            kcat = jnp.concatenate([kn_ref[pl.ds(off, tk), :], kr_ref[pl.ds(off, tk), :]], axis=1)
            vb = v_ref[pl.ds(off, tk), :]
            zero = jnp.zeros_like(kcat)
            kk = jnp.concatenate([jnp.where(sel0, kcat, zero), jnp.where(sel1, kcat, zero)], axis=0)
            mask = _mla_mask2(I, j, tq) if masked else None
            dss, pbs = [], []
            for h in range(2):
                s = _dot_nt(qcat[h], kcat) * MLA_SCALE
                p = jnp.exp(s - lses[h])
                if masked:
                    p = jnp.where(mask, p, 0.0)
                dp = _dot_nt(dom[h], vb)
                dss.append((p * (dp - dd[h]) * MLA_SCALE).astype(BF16))
                pbs.append(p.astype(BF16))
            dkc = _dot_tn(jnp.concatenate(dss, axis=0), qq)
            dkn_ref[pl.ds(off, tk), :] += dkc[:, 0:128]
            dkr_ref[pl.ds(off, tk), :] += dkc[:, 128:256]
            dv_ref[pl.ds(off, tk), :] += _dot_tn(jnp.concatenate(pbs, axis=0), docat)
            return dq + _dot(jnp.concatenate(dss, axis=1), kk)

        dq = jnp.zeros((tq, 256), F32)
        dq = lax.fori_loop(0, jnp.minimum(I, 1), lambda t, c: tile(0, c, True), dq)
        n_in = jnp.maximum(I - 1, 0)
        dq = lax.fori_loop(0, n_in // 2, lambda t, c: tile(1 + 2 * t, c, False, 2), dq)
        dq = lax.fori_loop(0, n_in % 2, lambda t, c: tile(I - 1, c, False), dq)
        dq = tile(I, dq, True)
        dq_ref[:, 0:128] = dq[:, 0:128]
        dq_ref[:, 128:256] = _rope_t(dq[:, 128:256], cs_ref[...], sn_ref[...], MLA_ROPE // 2)

    blk = lambda c0: pl.BlockSpec((tq, 128), lambda hp, i: (i, c0 + hp))
    full = lambda c0: pl.BlockSpec((L, 128), lambda hp, i: (0, c0 + hp))
    tab = pl.BlockSpec((tq, 128), lambda hp, i: (i, 0))
    return pl.pallas_call(
        body, name=name, grid=(MLA_HEADS // 2, L // tq),
        in_specs=[pl.BlockSpec((tq, 256), lambda hp, i: (i, hp)), full(0), full(4),
                  pl.BlockSpec((L, 128), lambda hp, i: (0, 0)), tab, tab, blk(4), blk(4),
                  pl.BlockSpec((1, tq, 128), lambda hp, i: (hp, i, 0))],
        out_specs=(pl.BlockSpec((tq, 256), lambda hp, i: (i, hp)), full(0), full(0),
                   pl.BlockSpec((L, 128), lambda hp, i: (0, 0))),
        out_shape=(jax.ShapeDtypeStruct((L, 1024), F32), jax.ShapeDtypeStruct((L, 512), F32),
                   jax.ShapeDtypeStruct((L, 512), F32), jax.ShapeDtypeStruct((L, 128), F32)),
        compiler_params=pltpu.CompilerParams(dimension_semantics=("arbitrary", "arbitrary")),
    )(Q, KV, KV, KR, cs, sn, mixed, dmixed, lse)


def _ret_decay(h):
    lg = RET_LOG_G[h]
    r = lax.broadcasted_iota(jnp.int32, (BLK, BLK), 0)
    c = lax.broadcasted_iota(jnp.int32, (BLK, BLK), 1)
    diff = (r - c).astype(F32)
    d_in = jnp.where(diff >= 0, jnp.exp(jnp.maximum(diff, 0.0) * lg), 0.0)
    idx = lax.broadcasted_iota(jnp.int32, (BLK, 1), 0).astype(F32)
    q_decay = jnp.exp((idx + 1.0) * lg)
    k_decay = jnp.exp((BLK - 1.0 - idx) * lg)
    c_decay = math.exp(BLK * lg)
    return d_in, q_decay, k_decay, c_decay


def _ret_qk(qk_ref, cs_ref, sn_ref, n):
    cs = jnp.concatenate([cs_ref[...]] * 2, axis=1)
    sn = jnp.concatenate([sn_ref[...]] * 2, axis=1)
    rq = _rope(qk_ref[:, 0:256], cs, sn, RET_QK // 2)
    row = n * BLK + lax.broadcasted_iota(jnp.int32, (BLK, 256), 0)
    kmul = jnp.where(row >= N_PAD, 0.125, 0.0)
    rk = _rope(qk_ref[:, 256:512], cs, sn, RET_QK // 2) * kmul
    return rq, rk, cs, sn, kmul


def _head_norm(y):
    mu = jnp.mean(y, axis=-1, keepdims=True)
    yc = y - mu
    r = lax.rsqrt(jnp.mean(jnp.square(yc), axis=-1, keepdims=True) + LN_EPS)
    return yc * r, r


def ret_fwd(P, cs, sn, *, name):
    L = P.shape[0]
    nb = L // BLK

    def body(qk_ref, v_ref, g_ref, cs_ref, sn_ref, o_ref, y_ref, st_ref, state):
        n = pl.program_id(0)

        @pl.when(n == 0)
        def _():
            state[...] = jnp.zeros_like(state)

        st_ref[0] = state[...]
        rq, rk, _, _, _ = _ret_qk(qk_ref, cs_ref, sn_ref, n)
        outs, ys = [], []
        for h in range(RET_HEADS):
            d_in, q_decay, k_decay, c_decay = _ret_decay(h)
            q = rq[:, 64 * h:64 * h + 64].astype(BF16)
            kf = rk[:, 64 * h:64 * h + 64]
            v = v_ref[:, 128 * h:128 * h + 128].astype(BF16)
            S = state[h]
            inner = _dot_nt(q, kf.astype(BF16)) * d_in
            y = _dot(inner.astype(BF16), v) + _dot(q, S.astype(BF16)) * q_decay
            state[h] = S * c_decay + _dot_tn((kf * k_decay).astype(BF16), v)
            g = g_ref[:, 128 * h:128 * h + 128]
            ys.append(y)
            outs.append(g * jax.nn.sigmoid(g) * _head_norm(y)[0])
        o_ref[...] = jnp.concatenate(outs, axis=1)
        y_ref[...] = jnp.concatenate(ys, axis=1)

    blk512 = lambda c: pl.BlockSpec((BLK, 512), lambda n: (n, c))
    tab = pl.BlockSpec((BLK, 128), lambda n: (n, 0))
    return pl.pallas_call(
        body, name=name, grid=(nb,),
        in_specs=[blk512(C_RQ // 512), blk512(C_RV // 512), blk512(C_RG // 512), tab, tab],
        out_specs=(blk512(0), blk512(0), pl.BlockSpec((1, RET_HEADS, RET_QK, RET_V), lambda n: (n, 0, 0, 0))),
        out_shape=(jax.ShapeDtypeStruct((L, 512), F32), jax.ShapeDtypeStruct((L, 512), F32),
                   jax.ShapeDtypeStruct((nb, RET_HEADS, RET_QK, RET_V), F32)),
        scratch_shapes=[pltpu.VMEM((RET_HEADS, RET_QK, RET_V), F32)],
        compiler_params=pltpu.CompilerParams(dimension_semantics=("arbitrary",)),
    )(P, P, P, cs, sn)


def ret_bwd(P, y, states, dmixed, cs, sn, *, name):
    L = P.shape[0]
    nb = L // BLK

    def body(qk_ref, v_ref, g_ref, y_ref, st_ref, do_ref, cs_ref, sn_ref, dqk_ref, dv_ref, dg_ref, dstate):
        n = nb - 1 - pl.program_id(0)

        @pl.when(pl.program_id(0) == 0)
        def _():
            dstate[...] = jnp.zeros_like(dstate)

        rq, rk, cs, sn, kmul = _ret_qk(qk_ref, cs_ref, sn_ref, n)
        dqs, dks, dvs, dgs = [], [], [], []
        for h in range(RET_HEADS):
            d_in, q_decay, k_decay, c_decay = _ret_decay(h)
            sv = slice(128 * h, 128 * h + 128)
            q = rq[:, 64 * h:64 * h + 64].astype(BF16)
            kf = rk[:, 64 * h:64 * h + 64]
            k = kf.astype(BF16)
            kd = (kf * k_decay).astype(BF16)
            v = v_ref[:, sv].astype(BF16)
            g = g_ref[:, sv]
            do = do_ref[:, sv]
            yh = y_ref[:, sv]
            S = st_ref[0, h].astype(BF16)
            dS = dstate[h]
            sg = jax.nn.sigmoid(g)
            yn, r = _head_norm(yh)
            dgs.append(do * yn * (sg * (1.0 + g * (1.0 - sg))))
            dyn = do * (g * sg)
            dy = r * (dyn - jnp.mean(dyn, axis=-1, keepdims=True) - yn * jnp.mean(dyn * yn, axis=-1, keepdims=True))
            dyb = dy.astype(BF16)
            dyq = (dy * q_decay).astype(BF16)
            inner = (_dot_nt(q, k) * d_in).astype(BF16)
            A = (_dot_nt(dyb, v) * d_in).astype(BF16)
            dSb = dS.astype(BF16)
            dqs.append(_dot(A, k) + _dot_nt(dyq, S))
            dks.append(_dot_tn(A, q) + _dot_nt(v, dSb) * k_decay)
            dvs.append(_dot_tn(inner, dyb) + _dot(kd, dSb))
            dstate[h] = dS * c_decay + _dot_tn(q, dyq)
        drq = _rope_t(jnp.concatenate(dqs, axis=1), cs, sn, RET_QK // 2)
        drk = _rope_t(jnp.concatenate(dks, axis=1) * kmul, cs, sn, RET_QK // 2)
        dqk_ref[...] = jnp.concatenate([drq, drk], axis=1)
        dv_ref[...] = jnp.concatenate(dvs, axis=1)
        dg_ref[...] = jnp.concatenate(dgs, axis=1)

    blk512 = lambda c: pl.BlockSpec((BLK, 512), lambda t: (nb - 1 - t, c))
    tab = pl.BlockSpec((BLK, 128), lambda t: (nb - 1 - t, 0))
    sds = jax.ShapeDtypeStruct((L, 512), F32)
    return pl.pallas_call(
        body, name=name, grid=(nb,),
        in_specs=[blk512(C_RQ // 512), blk512(C_RV // 512), blk512(C_RG // 512), blk512(0),
                  pl.BlockSpec((1, RET_HEADS, RET_QK, RET_V), lambda t: (nb - 1 - t, 0, 0, 0)), blk512(2), tab, tab],
        out_specs=(blk512(0), blk512(0), blk512(0)), out_shape=(sds, sds, sds),
        scratch_shapes=[pltpu.VMEM((RET_HEADS, RET_QK, RET_V), F32)],
        compiler_params=pltpu.CompilerParams(dimension_semantics=("arbitrary",)),
    )(P, P, P, y, states, dmixed, cs, sn)


def _perm_w_in(w):
    pad = jnp.zeros(w.shape[:-1] + (N_INP - N_IN,), w.dtype)
    return jnp.concatenate([w[..., 0:1536], w[..., 2208:3744], w[..., 1536:2208], pad], axis=-1)


def _unperm_w_in(g):
    return jnp.concatenate([g[..., 0:1536], g[..., 3072:3744], g[..., 1536:3072]], axis=-1)


def _perm_w_uq(w):
    lead = w.shape[:-1]
    w5 = w.reshape(lead + (4, 2, 96))
    nope = w5[..., :64].reshape(lead + (4, 128))
    rope = w5[..., 64:].reshape(lead + (4, 64))
    return jnp.concatenate([nope, rope, jnp.zeros(lead + (4, 64), w.dtype)], axis=-1).reshape(lead + (1024,))


def _unperm_w_uq(g):
    lead = g.shape[:-1]
    g4 = g.reshape(lead + (4, 256))
    nope = g4[..., :128].reshape(lead + (4, 2, 64))
    rope = g4[..., 128:192].reshape(lead + (4, 2, 32))
    return jnp.concatenate([nope, rope], axis=-1).reshape(lead + (768,))


def _perm_w_ukv(w):
    lead = w.shape[:-1]
    w4 = w.reshape(lead + (8, 128))
    return jnp.concatenate([w4[..., :64].reshape(lead + (512,)), w4[..., 64:].reshape(lead + (512,))], axis=-1)


def _unperm_w_ukv(g):
    lead = g.shape[:-1]
    return jnp.concatenate([g[..., :512].reshape(lead + (8, 64)), g[..., 512:].reshape(lead + (8, 64))],
                           axis=-1).reshape(lead + (1024,))


def _rope_tables(L, half):
    pos = (jnp.arange(L) - N_PAD).astype(F32)
    inv = ROPE_THETA ** (-jnp.arange(half, dtype=F32) / half)
    ang = pos[:, None] * inv[None, :]
    cos, sin = jnp.cos(ang), jnp.sin(ang)
    reps = 128 // (2 * half)
    cs = jnp.tile(jnp.concatenate([cos, cos], axis=1), (1, reps))
    sn = jnp.tile(jnp.concatenate([-sin, sin], axis=1), (1, reps))
    return cs, sn


def _device_step(x, target, meta, ln_emb_g, ln_emb_b, w_in, q_norm, kv_norm, w_uq, w_ukv, w_out,
                 ln1_g, ln1_b, w_ff1, w_ff2, ln2_g, ln2_b):
    S = x.shape[0]
    L = S + BLK
    depth = w_in.shape[0]
    cs_m, sn_m = _rope_tables(L, MLA_ROPE // 2)
    cs_r, sn_r = _rope_tables(L, RET_QK // 2)
    hcat = jnp.concatenate([jnp.zeros((N_PAD, D_MODEL), F32), meta, x], axis=0)
    h, _ = ln_fwd(hcat, ln_emb_g, ln_emb_b, name="ln_emb_fwd")

    saved = []
    for l in range(depth):
        P = mm_nn(h, w_in[l], tn=768, name=f"in_proj_{l}")
        out_a, sbc = sb_fwd(P, name=f"sb_fwd_{l}")
        nq, nkv, KR = mla_pre_fwd(P, q_norm[l], kv_norm[l], cs_m, sn_m, name=f"mla_pre_fwd_{l}")
        Q = mm_nn(nq, w_uq[l], tn=512, name=f"uq_{l}")
        KV = mm_nn(nkv, w_ukv[l], tn=512, name=f"ukv_{l}", out_dtype=BF16)
        out_b, lse = mla_fwd(Q, KV, KR, cs_m, sn_m, name=f"mla_fwd_{l}")
        out_c, y, states = ret_fwd(P, cs_r, sn_r, name=f"ret_fwd_{l}")
        mixed = jnp.concatenate([out_a, out_b, out_c], axis=1)
        w_out_l = w_out[l].reshape(1, 1536, D_MODEL)
        mix = mm_nn(mixed, w_out_l, tn=512, tk=512, name=f"out_proj_{l}")
        h1, z1 = ln_fwd(mix, ln1_g[l], ln1_b[l], res=h, name=f"ln1_fwd_{l}")
        U = mm_nn(h1, w_ff1[l], tn=1024, name=f"ff1_{l}")
        w_ff2_l = w_ff2[l].reshape(1, D_FF, D_MODEL)
        mlp = mm_nn(U, w_ff2_l, tn=512, tk=1024, prologue="relu2", name=f"ff2_{l}")
        h2, z2 = ln_fwd(mlp, ln2_g[l], ln2_b[l], res=h1, name=f"ln2_fwd_{l}")
        saved.append((h, P, sbc, nq, nkv, KR, Q, KV, lse, y, states, mixed, z1, h1, U, z2))
        h = h2

    loss_t, dh = loss_fwd_bwd(h, target, name="loss")

    grads = {k: [None] * depth for k in ("w_in", "q_norm", "kv_norm", "w_uq", "w_ukv", "w_out", "ln1_g", "ln1_b",
                                           "w_ff1", "w_ff2", "ln2_g", "ln2_b")}
    for l in reversed(range(depth)):
        h_in, P, sbc, nq, nkv, KR, Q, KV, lse, y, states, mixed, z1, h1, U, z2 = saved[l]
        dz2, grads["ln2_g"][l], grads["ln2_b"][l] = ln_bwd(dh, z2, ln2_g[l], name=f"ln2_bwd_{l}")
        w_ff2_l = w_ff2[l].reshape(1, D_FF, D_MODEL)
        grads["w_ff2"][l] = mm_tn(U, dz2, shards=1, tko=512, tn=1024, prologue="relu2", name=f"ff2_dw_{l}").reshape(4, 1024, D_MODEL)
        dU = mm_nt(dz2, w_ff2_l, tn=1024, tko=1024, relu2grad=U, name=f"ff2_dx_{l}")
        grads["w_ff1"][l] = mm_tn(h1, dU, shards=4, tko=512, tn=1024, name=f"ff1_dw_{l}")
        dh1 = mm_nt(dU, w_ff1[l], tn=1024, tko=1024, axpy=(dz2, DN_ALPHA), name=f"ff1_dx_{l}")
        dz1, grads["ln1_g"][l], grads["ln1_b"][l] = ln_bwd(dh1, z1, ln1_g[l], name=f"ln1_bwd_{l}")
        w_out_l = w_out[l].reshape(1, 1536, D_MODEL)
        grads["w_out"][l] = mm_tn(mixed, dz1, shards=1, tko=512, tn=1024, name=f"out_dw_{l}").reshape(4, 384, D_MODEL)
        dmixed = mm_nt(dz1, w_out_l, tn=1024, tko=512, name=f"out_dx_{l}")
        d_rqk, d_rv, d_rg = ret_bwd(P, y, states, dmixed, cs_r, sn_r, name=f"ret_bwd_{l}")
        dQ, dKN, dV, dKR = mla_bwd(Q, KV, KR, cs_m, sn_m, mixed, dmixed, lse, name=f"mla_bwd_{l}")
        dKV = jnp.concatenate([dKN, dV], axis=1)
        grads["w_uq"][l] = mm_tn(nq, dQ, shards=1, tko=MLA_Q_LORA, tn=512, name=f"uq_dw_{l}")[0]
        grads["w_ukv"][l] = mm_tn(nkv, dKV, shards=1, tko=MLA_KV_LORA, tn=512, name=f"ukv_dw_{l}")[0]
        dnq = mm_nt(dQ, w_uq[l], tn=1024, tko=MLA_Q_LORA, name=f"uq_dx_{l}")
        dnkv = mm_nt(dKV, w_ukv[l], tn=1024, tko=MLA_KV_LORA, name=f"ukv_dx_{l}")
        d_lat, grads["q_norm"][l], grads["kv_norm"][l] = mla_pre_bwd(P, dnq, dnkv, dKR, q_norm[l], kv_norm[l], cs_m, sn_m,
                                                                     name=f"mla_pre_bwd_{l}")
        dq_sb, dk_sb, dv_sb = sb_bwd(P, sbc, dmixed, name=f"sb_bwd_{l}")
        dP = jnp.concatenate([dq_sb, dk_sb, dv_sb, d_rqk, d_rv, d_rg, d_lat], axis=1)
        grads["w_in"][l] = mm_tn(h_in, dP, shards=1, tko=512, tn=768, name=f"in_dw_{l}")[0]
        dh = mm_nt(dP, w_in[l], tn=768, tko=1024, axpy=(dz1, DN_ALPHA), name=f"in_dx_{l}")

    dhcat, dg_emb, db_emb = ln_bwd(dh, hcat, ln_emb_g, name="ln_emb_bwd")
    out = {k: jnp.stack(v) for k, v in grads.items()}
    out["ln_emb_g"], out["ln_emb_b"] = dg_emb, db_emb
    out["meta"] = dhcat[N_PAD:BLK]
    return loss_t[0, 0], dhcat[BLK:], out


MESH = pl.DeviceIdType.MESH
PEER_XOR = (2, 1, 3)
_HBM = pl.BlockSpec(memory_space=pltpu.HBM)


def _place():
    x, y, c = lax.axis_index("x"), lax.axis_index("y"), lax.axis_index("c")
    peers = [(1 - x, y, c), (x, 1 - y, c), (1 - x, 1 - y, c)]
    return x, y, c, 2 * x + y, peers, (x, y, 1 - c)


def gather_weight(w_shard, *, name):
    nl = w_shard.shape[0]
    hl = nl // 2

    def body(w_ref, out_ref, send_sems, recv_sems):
        x, y, c, s0, peers, sibling = _place()

        def piece(s, half):
            return out_ref.at[s, pl.ds(half * hl, hl)]

        def copy(k, s, half, to, src=None):
            return pltpu.make_async_remote_copy(src_ref=piece(s, half) if src is None else src, dst_ref=piece(s, half),
                                                send_sem=send_sems.at[k], recv_sem=recv_sems.at[k],
                                                device_id=to, device_id_type=MESH)

        first = [copy(k, s0, c, peers[k], src=w_ref.at[pl.ds(c * hl, hl)]) for k in range(3)]
        for cp in first:
            cp.start()
        passed = [copy(3 + k, s0 ^ PEER_XOR[k], c, sibling) for k in range(3)]
        for k in range(3):
            copy(k, s0 ^ PEER_XOR[k], c, peers[k]).wait_recv()
            passed[k].start()
        for k in range(3):
            copy(3 + k, s0 ^ PEER_XOR[k], 1 - c, sibling).wait_recv()
        for cp in first + passed:
            cp.wait_send()

    return pl.pallas_call(
        body, name=name, in_specs=[_HBM], out_specs=_HBM,
        out_shape=jax.ShapeDtypeStruct((4,) + w_shard.shape, w_shard.dtype),
        scratch_shapes=[pltpu.SemaphoreType.DMA((6,)), pltpu.SemaphoreType.DMA((6,))],
    )(w_shard)


def send_half_to_sibling(G, *, name):
    hl = G.shape[1] // 2

    def body(g_ref, out_ref, send_sem, recv_sem):
        x, y, c, s0, peers, sibling = _place()
        cp = pltpu.make_async_remote_copy(src_ref=g_ref.at[:, pl.ds((1 - c) * hl, hl)], dst_ref=out_ref,
                                          send_sem=send_sem, recv_sem=recv_sem, device_id=sibling, device_id_type=MESH)
        cp.start()
        cp.wait()

    return pl.pallas_call(
        body, name=name, in_specs=[_HBM], out_specs=_HBM,
        out_shape=jax.ShapeDtypeStruct((4, hl) + G.shape[2:], G.dtype),
        scratch_shapes=[pltpu.SemaphoreType.DMA, pltpu.SemaphoreType.DMA],
    )(G)


def scatter_to_chips(A, *, name):
    def body(a_ref, out_ref, send_sems, recv_sems):
        x, y, c, s0, peers, sibling = _place()
        copies = [pltpu.make_async_remote_copy(src_ref=a_ref.at[s0 ^ PEER_XOR[k]], dst_ref=out_ref.at[k],
                                               send_sem=send_sems.at[k], recv_sem=recv_sems.at[k],
                                               device_id=peers[k], device_id_type=MESH) for k in range(3)]
        for cp in copies:
            cp.start()
        for cp in copies:
            cp.wait()

    return pl.pallas_call(
        body, name=name, in_specs=[_HBM], out_specs=_HBM,
        out_shape=jax.ShapeDtypeStruct((3,) + A.shape[1:], A.dtype),
        scratch_shapes=[pltpu.SemaphoreType.DMA((3,)), pltpu.SemaphoreType.DMA((3,))],
    )(A)


def join_halves(buf, *, name):
    hl = buf.shape[0] // 2

    def body(b_ref, out_ref, send_sem, recv_sem):
        x, y, c, s0, peers, sibling = _place()
        cp = pltpu.make_async_remote_copy(src_ref=b_ref.at[pl.ds(c * hl, hl)], dst_ref=out_ref.at[pl.ds(c * hl, hl)],
                                          send_sem=send_sem, recv_sem=recv_sem, device_id=sibling, device_id_type=MESH)
        cp.start()
        pltpu.make_async_remote_copy(src_ref=b_ref.at[pl.ds((1 - c) * hl, hl)], dst_ref=out_ref.at[pl.ds((1 - c) * hl, hl)],
                                     send_sem=send_sem, recv_sem=recv_sem, device_id=sibling, device_id_type=MESH).wait_recv()
        cp.wait_send()

    return pl.pallas_call(
        body, name=name, in_specs=[_HBM], out_specs=_HBM, input_output_aliases={0: 0},
        out_shape=jax.ShapeDtypeStruct(buf.shape, buf.dtype),
        scratch_shapes=[pltpu.SemaphoreType.DMA, pltpu.SemaphoreType.DMA],
    )(buf)


def allgather8(xs, *, name, reduce):
    M, N = xs.shape

    def body(x_ref, out_ref, *rest):
        if reduce:
            all_ref, send_sems, recv_sems, local_sem = rest
        else:
            all_ref = out_ref
            send_sems, recv_sems, local_sem = rest
        x, y, c, s0, peers, sibling = _place()
        me = (x, y, c)
        chips = [(1 - x, y), (x, 1 - y), (1 - x, 1 - y)]

        def rows(px, py, pc):
            return all_ref.at[pl.ds((4 * px + 2 * py + pc) * M, M), :]

        def copy(k, block, to, src=None):
            return pltpu.make_async_remote_copy(src_ref=rows(*block) if src is None else src, dst_ref=rows(*block),
                                                send_sem=send_sems.at[k], recv_sem=recv_sems.at[k],
                                                device_id=to, device_id_type=MESH)

        mine = pltpu.make_async_copy(x_ref, rows(*me), local_sem)
        mine.start()
        first = [copy(0, me, sibling, src=x_ref)]
        first += [copy(1 + j, me, (*chip, c), src=x_ref) for j, chip in enumerate(chips)]
        for cp in first:
            cp.start()
        passed = [copy(4 + j, (*chip, c), sibling) for j, chip in enumerate(chips)]
        for j, chip in enumerate(chips):
            copy(1 + j, (*chip, c), me).wait_recv()
            passed[j].start()
        copy(0, sibling, me).wait_recv()
        for j, chip in enumerate(chips):
            copy(4 + j, (*chip, 1 - c), me).wait_recv()
        for cp in first + passed:
            cp.wait_send()
        mine.wait()
        if reduce:
            acc = all_ref[pl.ds(0, M), :]
            for d in range(1, 8):
                acc = acc + all_ref[pl.ds(d * M, M), :]
            out_ref[...] = acc

    vm = pl.BlockSpec(memory_space=pltpu.VMEM)
    scratch = [pltpu.SemaphoreType.DMA((7,)), pltpu.SemaphoreType.DMA((7,)), pltpu.SemaphoreType.DMA]
    if reduce:
        scratch = [pltpu.VMEM((8 * M, N), xs.dtype)] + scratch
    return pl.pallas_call(
        body, name=name, in_specs=[vm], out_specs=vm,
        out_shape=jax.ShapeDtypeStruct((M if reduce else 8 * M, N), xs.dtype), scratch_shapes=scratch,
    )(xs)


def add_halves(G, B, c, *, name):
    S, nl, R, C = G.shape
    hl = nl // 2
    tr = _pick(R, (512, 384, 256, 128))

    def body(c_ref, g_ref, b_ref, o_ref):
        o_ref[...] = (g_ref[...] + b_ref[...]).astype(BF16)

    blk = (1, 1, tr, C)
    return pl.pallas_call(
        body, name=name,
        grid_spec=pltpu.PrefetchScalarGridSpec(
            num_scalar_prefetch=1, grid=(S, hl, R // tr),
            in_specs=[pl.BlockSpec(blk, lambda s, l, r, cr: (s, cr[0] * hl + l, r, 0)),
                      pl.BlockSpec(blk, lambda s, l, r, cr: (s, l, r, 0))],
            out_specs=pl.BlockSpec(blk, lambda s, l, r, cr: (s, l, r, 0))),
        out_shape=jax.ShapeDtypeStruct((S, hl, R, C), BF16),
    )(jnp.reshape(c, (1,)).astype(jnp.int32), G, B)


def add_chips(G, B, Bc, c, s0, *, name):
    S, nl, R, C = G.shape
    hl = nl // 2
    tr = _pick(R, (512, 384, 256, 128))

    def body(pc_ref, ps_ref, g_ref, b_ref, c0_ref, c1_ref, c2_ref, o_ref):
        o_ref[...] = ((((g_ref[0] + b_ref[0]) + c0_ref[0].astype(F32)) + c1_ref[0].astype(F32)) + c2_ref[0].astype(F32))

    blk = (1, 1, tr, C)
    cspec = lambda k: pl.BlockSpec(blk, lambda l, r, pc, ps: (k, l, r, 0))
    return pl.pallas_call(
        body, name=name,
        grid_spec=pltpu.PrefetchScalarGridSpec(
            num_scalar_prefetch=2, grid=(hl, R // tr),
            in_specs=[pl.BlockSpec(blk, lambda l, r, pc, ps: (ps[0], pc[0] * hl + l, r, 0)),
                      pl.BlockSpec(blk, lambda l, r, pc, ps: (ps[0], l, r, 0)), cspec(0), cspec(1), cspec(2)],
            out_specs=pl.BlockSpec((1, tr, C), lambda l, r, pc, ps: (pc[0] * hl + l, r, 0))),
        out_shape=jax.ShapeDtypeStruct((nl, R, C), F32),
    )(jnp.reshape(c, (1,)).astype(jnp.int32), jnp.reshape(s0, (1,)).astype(jnp.int32), G, B, Bc, Bc, Bc)


def reduce_scatter_weight(G, c, s0, *, tag):
    B = send_half_to_sibling(G, name=f"rs_sib_{tag}")
    A = add_halves(G, B, c, name=f"rs_add1_{tag}")
    Bc = scatter_to_chips(A, name=f"rs_chips_{tag}")
    half = add_chips(G, B, Bc, c, s0, name=f"rs_add2_{tag}")
    return join_halves(half, name=f"rs_join_{tag}")


def adamw(w, g, m, v, *, name):
    shp = w.shape
    C = shp[-1]
    R = int(np.prod(shp[:-1])) if len(shp) > 1 else 1
    tr = R
    for t in (512, 384, 256, 128):
        if R % t == 0:
            tr = t
            break

    def body(w_ref, g_ref, m_ref, v_ref, d_ref, nm_ref, nv_ref):
        gv = g_ref[...]
        mn = ADAM_B1 * m_ref[...] + (1.0 - ADAM_B1) * gv
        vn = ADAM_B2 * v_ref[...] + (1.0 - ADAM_B2) * jnp.square(gv)
        m_hat = mn / (1.0 - ADAM_B1 ** ADAM_STEP)
        v_hat = vn / (1.0 - ADAM_B2 ** ADAM_STEP)
        d_ref[...] = -ADAM_LR * (m_hat / (jnp.sqrt(v_hat) + ADAM_EPS) + ADAM_WD * w_ref[...])
        nm_ref[...] = mn
        nv_ref[...] = vn

    spec = pl.BlockSpec((tr, C), lambda i: (i, 0))
    sds = jax.ShapeDtypeStruct((R, C), F32)
    d, nm, nv = pl.pallas_call(body, name=name, grid=(R // tr,), in_specs=[spec] * 4, out_specs=(spec,) * 3,
                               out_shape=(sds,) * 3)(*(a.reshape(R, C) for a in (w, g, m, v)))
    return d.reshape(shp), nm.reshape(shp), nv.reshape(shp)


_SMALL = ("ln_emb_g", "ln_emb_b", "q_norm", "kv_norm", "ln1_g", "ln1_b", "ln2_g", "ln2_b", "meta")


def _pack_small(d):
    flat = jnp.concatenate([d[k].reshape(-1) for k in _SMALL])
    rows = -(-flat.shape[0] // 128)
    rows = -(-rows // 8) * 8
    flat = jnp.concatenate([flat, jnp.zeros((rows * 128 - flat.shape[0],), F32)])
    return flat.reshape(rows, 128)


def _unpack_small(p, shapes):
    flat = p.reshape(-1)
    out, o = {}, 0
    for k in _SMALL:
        n = int(np.prod(shapes[k]))
        out[k] = flat[o:o + n].reshape(shapes[k])
        o += n
    return out


def kernel(x, meta_tokens, ln_emb_g, ln_emb_b, w_in, mla_q_norm, mla_kv_norm, w_uq, w_ukv, w_out, ln1_g, ln1_b, w_ff1, w_ff2, ln2_g, ln2_b, loss_target, m_meta_tokens, m_ln_emb_g, m_ln_emb_b, m_w_in, m_mla_q_norm, m_mla_kv_norm, m_w_uq, m_w_ukv, m_w_out, m_ln1_g, m_ln1_b, m_w_ff1, m_w_ff2, m_ln2_g, m_ln2_b, v_meta_tokens, v_ln_emb_g, v_ln_emb_b, v_w_in, v_mla_q_norm, v_mla_kv_norm, v_w_uq, v_w_ukv, v_w_out, v_ln1_g, v_ln1_b, v_w_ff1, v_w_ff2, v_ln2_g, v_ln2_b):
    xi, yi, ci = lax.axis_index("x"), lax.axis_index("y"), lax.axis_index("c")
    s0 = 2 * xi + yi
    nl = w_in.shape[0]

    big = {"w_in": w_in, "w_uq": w_uq, "w_ukv": w_ukv, "w_out": w_out, "w_ff1": w_ff1, "w_ff2": w_ff2}
    full = {}
    for k, v in big.items():
        vb = v.astype(BF16)
        full[k] = lax.dynamic_update_slice(gather_weight(vb, name=f"ag_{k}"), vb[None], (s0, 0, 0, 0))
    cols = lambda a: jnp.moveaxis(a, 0, 2).reshape(a.shape[1], a.shape[2], 4 * a.shape[3])
    k_w_in = _perm_w_in(cols(full["w_in"]))[:, None]
    k_w_uq = _perm_w_uq(cols(full["w_uq"]))[:, None]
    k_w_ukv = _perm_w_ukv(cols(full["w_ukv"]))[:, None]
    k_w_out = jnp.moveaxis(full["w_out"], 0, 1)
    k_w_ff1 = jnp.moveaxis(full["w_ff1"], 0, 1)
    k_w_ff2 = jnp.moveaxis(full["w_ff2"], 0, 1)
    meta_all = allgather8(meta_tokens, name="ag_meta", reduce=False)
    meta_full = jnp.concatenate([meta_all[32 * s:32 * s + N_META] for s in range(4)], axis=1)

    loss_part, grad_x, g = _device_step(x[0], loss_target[0], meta_full, ln_emb_g, ln_emb_b, k_w_in, mla_q_norm, mla_kv_norm,
                                        k_w_uq, k_w_ukv, k_w_out, ln1_g, ln1_b, k_w_ff1, k_w_ff2, ln2_g, ln2_b)
    loss = lax.psum(loss_part, ("x", "y", "c"))

    def col_shards(a):
        return jnp.moveaxis(a.reshape(a.shape[0], a.shape[1], 4, a.shape[2] // 4), 2, 0)

    G = {"w_in": col_shards(_unperm_w_in(g["w_in"])), "w_uq": col_shards(_unperm_w_uq(g["w_uq"])),
         "w_ukv": col_shards(_unperm_w_ukv(g["w_ukv"])), "w_out": jnp.moveaxis(g["w_out"], 1, 0),
         "w_ff1": jnp.moveaxis(g["w_ff1"], 1, 0), "w_ff2": jnp.moveaxis(g["w_ff2"], 1, 0)}
    gw = {k: reduce_scatter_weight(v, ci, s0, tag=k) for k, v in G.items()}

    small_shapes = {"ln_emb_g": (D_MODEL,), "ln_emb_b": (D_MODEL,), "q_norm": (nl, MLA_Q_LORA), "kv_norm": (nl, MLA_KV_LORA),
                    "ln1_g": (nl, D_MODEL), "ln1_b": (nl, D_MODEL), "ln2_g": (nl, D_MODEL), "ln2_b": (nl, D_MODEL),
                    "meta": (N_META, D_MODEL)}
    gs = _unpack_small(allgather8(_pack_small(g), name="ar_small", reduce=True), small_shapes)
    gw.update({"ln_emb_g": gs["ln_emb_g"], "ln_emb_b": gs["ln_emb_b"], "mla_q_norm": gs["q_norm"], "mla_kv_norm": gs["kv_norm"],
               "ln1_g": gs["ln1_g"], "ln1_b": gs["ln1_b"], "ln2_g": gs["ln2_g"], "ln2_b": gs["ln2_b"],
               "meta_tokens": lax.dynamic_slice_in_dim(gs["meta"], s0 * 256, 256, axis=1)})

    names = ["meta_tokens", "ln_emb_g", "ln_emb_b", "w_in", "mla_q_norm", "mla_kv_norm", "w_uq", "w_ukv", "w_out",
             "ln1_g", "ln1_b", "w_ff1", "w_ff2", "ln2_g", "ln2_b"]
    ws = [meta_tokens, ln_emb_g, ln_emb_b, w_in, mla_q_norm, mla_kv_norm, w_uq, w_ukv, w_out, ln1_g, ln1_b, w_ff1, w_ff2, ln2_g, ln2_b]
    ms = [m_meta_tokens, m_ln_emb_g, m_ln_emb_b, m_w_in, m_mla_q_norm, m_mla_kv_norm, m_w_uq, m_w_ukv, m_w_out, m_ln1_g, m_ln1_b, m_w_ff1, m_w_ff2, m_ln2_g, m_ln2_b]
    vs = [v_meta_tokens, v_ln_emb_g, v_ln_emb_b, v_w_in, v_mla_q_norm, v_mla_kv_norm, v_w_uq, v_w_ukv, v_w_out, v_ln1_g, v_ln1_b, v_w_ff1, v_w_ff2, v_ln2_g, v_ln2_b]
    deltas, new_m, new_v = [], [], []
    for n, w, m, v in zip(names, ws, ms, vs):
        w2 = w.reshape(1, -1) if w.ndim == 1 else w
        d, nm, nv = adamw(w2, gw[n].reshape(w2.shape), m.reshape(w2.shape), v.reshape(w2.shape), name=f"adamw_{n}")
        deltas.append(d.reshape(w.shape))
        new_m.append(nm.reshape(w.shape))
        new_v.append(nv.reshape(w.shape))
    grads_out = [gw[n].reshape(w.shape) for n, w in zip(names, ws)]
    return (loss, grad_x[None], *grads_out, *deltas, *new_m, *new_v)
```

```python
import functools
import math

import numpy as np
import jax
import jax.numpy as jnp
from jax import lax
from jax.experimental import pallas as pl
from jax.experimental.pallas import tpu as pltpu

F32 = jnp.float32
BF16 = jnp.bfloat16

D_MODEL = 1024
DEPTH = 4
N_META = 16
BLK = 128
N_PAD = 112
SB_HEADS = 8
MLA_HEADS = 8
MLA_NOPE = 64
MLA_ROPE = 32
MLA_V = 64
MLA_Q_LORA = 384
MLA_KV_LORA = 256
RET_HEADS = 4
RET_QK = 64
RET_V = 128
D_FF = 4 * D_MODEL
ROPE_THETA = 10000.0
LN_EPS = 1e-5
DN_ALPHA = (2 * DEPTH) ** 0.25
RET_GAMMA = tuple(1.0 - 2.0 ** (-5 - h) for h in range(RET_HEADS))
RET_LOG_G = tuple(float(np.log(np.float32(g))) for g in RET_GAMMA)
MLA_SCALE = (MLA_NOPE + MLA_ROPE) ** -0.5

ADAM_LR = 0.001
ADAM_B1 = 0.9
ADAM_B2 = 0.999
ADAM_EPS = 1e-08
ADAM_WD = 0.01
ADAM_STEP = 10

N_SB = 1536
C_SBQ, C_SBK, C_SBV = 0, 512, 1024
C_RQ, C_RK, C_RV, C_RG = 0, 256, 512, 1024
C_CQ, C_CKV, C_KR = 1536, 1920, 2176
N_IN = 3744
N_INP = 3840

NEG = -1e30


def _pick(n, cands):
    for t in cands:
        if n % t == 0:
            return t
    raise ValueError(f"no tile for {n} in {cands}")


def _row_tile(n):
    return _pick(n, (1056, 1024, 528, 512, 384, 256, 128))


def _dot(a, b):
    return jnp.dot(a, b, preferred_element_type=F32)


def _dot_nt(a, b):
    return lax.dot_general(a, b, (((1,), (1,)), ((), ())), preferred_element_type=F32)


def _dot_tn(a, b):
    return lax.dot_general(a, b, (((0,), (0,)), ((), ())), preferred_element_type=F32)


def mm_nn(a, b, *, tn, name, tk=None, prologue=None, axpy=None, out_dtype=F32):
    M, K = a.shape
    S, _, Ns = b.shape
    tm = _row_tile(M)
    tk = K if tk is None else tk
    npt = Ns // tn
    nk = K // tk
    alpha = None if axpy is None else axpy[1]

    def body(*refs):
        if axpy is None:
            a_ref, b_ref, o_ref, acc = refs
        else:
            a_ref, b_ref, e_ref, o_ref, acc = refs
        k = pl.program_id(2)

        @pl.when(k == 0)
        def _():
            acc[...] = jnp.zeros_like(acc)

        x = a_ref[...]
        if prologue == "relu2":
            x = jnp.square(jnp.maximum(x, 0.0))
        acc[...] += _dot(x.astype(BF16), b_ref[0])

        @pl.when(k == nk - 1)
        def _():
            r = acc[...]
            if axpy is not None:
                r = r + alpha * e_ref[...]
            o_ref[...] = r.astype(out_dtype)

    in_specs = [pl.BlockSpec((tm, tk), lambda i, j, k: (i, k)),
                pl.BlockSpec((1, tk, tn), lambda i, j, k: (j // npt, k, j % npt))]
    args = [a, b]
    if axpy is not None:
        in_specs.append(pl.BlockSpec((tm, tn), lambda i, j, k: (i, j)))
        args.append(axpy[0])
    return pl.pallas_call(
        body, name=name, grid=(M // tm, (S * Ns) // tn, nk), in_specs=in_specs,
        out_specs=pl.BlockSpec((tm, tn), lambda i, j, k: (i, j)),
        out_shape=jax.ShapeDtypeStruct((M, S * Ns), out_dtype),
        scratch_shapes=[pltpu.VMEM((tm, tn), F32)],
        compiler_params=pltpu.CompilerParams(dimension_semantics=("parallel", "parallel", "arbitrary")),
    )(*args)


def mm_nt(a, b, *, tn, tko, name, axpy=None, relu2grad=None, out_dtype=F32):
    M, N = a.shape
    S, K, Ns = b.shape
    tm = _row_tile(M)
    npt = Ns // tn
    nn = N // tn
    alpha = None if axpy is None else axpy[1]

    def body(*refs):
        if axpy is None and relu2grad is None:
            a_ref, b_ref, o_ref, acc = refs
        else:
            a_ref, b_ref, e_ref, o_ref, acc = refs
        n = pl.program_id(2)

        @pl.when(n == 0)
        def _():
            acc[...] = jnp.zeros_like(acc)

        acc[...] += _dot_nt(a_ref[...].astype(BF16), b_ref[0])

        @pl.when(n == nn - 1)
        def _():
            r = acc[...]
            if axpy is not None:
                r = r + alpha * e_ref[...]
            if relu2grad is not None:
                r = r * (2.0 * jnp.maximum(e_ref[...], 0.0))
            o_ref[...] = r.astype(out_dtype)

    in_specs = [pl.BlockSpec((tm, tn), lambda i, j, n: (i, n)),
                pl.BlockSpec((1, tko, tn), lambda i, j, n: (n // npt, j, n % npt))]
    args = [a, b]
    extra = axpy[0] if axpy is not None else relu2grad
    if extra is not None:
        in_specs.append(pl.BlockSpec((tm, tko), lambda i, j, n: (i, j)))
        args.append(extra)
    return pl.pallas_call(
        body, name=name, grid=(M // tm, K // tko, nn), in_specs=in_specs,
        out_specs=pl.BlockSpec((tm, tko), lambda i, j, n: (i, j)),
        out_shape=jax.ShapeDtypeStruct((M, K), out_dtype),
        scratch_shapes=[pltpu.VMEM((tm, tko), F32)],
        compiler_params=pltpu.CompilerParams(dimension_semantics=("parallel", "parallel", "arbitrary")),
    )(*args)


def mm_tn(a, g, *, shards, tko, tn, name, prologue=None, into=None):
    M, K = a.shape
    _, N = g.shape
    Ns = N // shards
    tm = _row_tile(M)
    npt = Ns // tn
    nm = M // tm

    def body(*refs):
        if into is None:
            a_ref, g_ref, o_ref, acc = refs
        else:
            a_ref, g_ref, _, o_ref, acc = refs
        m = pl.program_id(2)

        @pl.when(m == 0)
        def _():
            acc[...] = jnp.zeros_like(acc)

        x = a_ref[...]
        if prologue == "relu2":
            x = jnp.square(jnp.maximum(x, 0.0))
        acc[...] += _dot_tn(x.astype(BF16), g_ref[...].astype(BF16))

        @pl.when(m == nm - 1)
        def _():
            if into is None:
                o_ref[0] = acc[...]
            else:
                o_ref[0, 0] = acc[...]

    in_specs = [pl.BlockSpec((tm, tko), lambda i, j, m: (m, i)),
                pl.BlockSpec((tm, tn), lambda i, j, m: (m, j))]
    scratch = [pltpu.VMEM((tko, tn), F32)]
    params = pltpu.CompilerParams(dimension_semantics=("parallel", "parallel", "arbitrary"))
    if into is None:
        return pl.pallas_call(
            body, name=name, grid=(K // tko, N // tn, nm), in_specs=in_specs,
            out_specs=pl.BlockSpec((1, tko, tn), lambda i, j, m: (j // npt, i, j % npt)),
            out_shape=jax.ShapeDtypeStruct((shards, K, Ns), F32), scratch_shapes=scratch, compiler_params=params,
        )(a, g)
    buf, layer, how = into
    if how == "cols":
        npt4 = (N // 4) // tn
        out_spec = pl.BlockSpec((1, 1, tko, tn), lambda i, j, m: (j // npt4, layer, i, j % npt4))
    else:
        kpt4 = (K // 4) // tko
        out_spec = pl.BlockSpec((1, 1, tko, tn), lambda i, j, m: (i // kpt4, layer, i % kpt4, j))
    return pl.pallas_call(
        body, name=name, grid=(K // tko, N // tn, nm), in_specs=in_specs + [pl.BlockSpec(memory_space=pl.ANY)],
        out_specs=out_spec, out_shape=jax.ShapeDtypeStruct(buf.shape, F32), input_output_aliases={2: 0},
        scratch_shapes=scratch, compiler_params=params,
    )(a, g, buf)


def _ln_stats(z):
    mu = jnp.mean(z, axis=-1, keepdims=True)
    zc = z - mu
    var = jnp.mean(jnp.square(zc), axis=-1, keepdims=True)
    r = lax.rsqrt(var + LN_EPS)
    return zc * r, r


def ln_fwd(x, g, b, *, name, res=None):
    L, Dm = x.shape
    tr = _row_tile(L)
    g2, b2 = g.reshape(1, Dm), b.reshape(1, Dm)

    def body(*refs):
        if res is None:
            x_ref, g_ref, b_ref, y_ref = refs
            z = x_ref[...]
        else:
            x_ref, r_ref, g_ref, b_ref, y_ref, z_ref = refs
            z = DN_ALPHA * r_ref[...] + x_ref[...]
            z_ref[...] = z
        xh, _ = _ln_stats(z)
        y_ref[...] = xh * g_ref[...] + b_ref[...]

    row = pl.BlockSpec((tr, Dm), lambda i: (i, 0))
    vec = pl.BlockSpec((1, Dm), lambda i: (0, 0))
    sds = jax.ShapeDtypeStruct((L, Dm), F32)
    if res is None:
        y = pl.pallas_call(body, name=name, grid=(L // tr,), in_specs=[row, vec, vec], out_specs=row, out_shape=sds)(x, g2, b2)
        return y, x
    return pl.pallas_call(body, name=name, grid=(L // tr,), in_specs=[row, row, vec, vec], out_specs=(row, row),
                          out_shape=(sds, sds))(x, res, g2, b2)


def ln_bwd(dy, z, g, *, name):
    L, Dm = z.shape
    tr = _row_tile(L)

    def body(dy_ref, z_ref, g_ref, dz_ref, dg_ref, db_ref):
        @pl.when(pl.program_id(0) == 0)
        def _():
            dg_ref[...] = jnp.zeros_like(dg_ref)
            db_ref[...] = jnp.zeros_like(db_ref)

        dyv = dy_ref[...]
        xh, r = _ln_stats(z_ref[...])
        dxh = dyv * g_ref[...]
        m1 = jnp.mean(dxh, axis=-1, keepdims=True)
        m2 = jnp.mean(dxh * xh, axis=-1, keepdims=True)
        dz_ref[...] = r * (dxh - m1 - xh * m2)
        dg_ref[...] += jnp.sum(dyv * xh, axis=0, keepdims=True)
        db_ref[...] += jnp.sum(dyv, axis=0, keepdims=True)

    row = pl.BlockSpec((tr, Dm), lambda i: (i, 0))
    vec = pl.BlockSpec((1, Dm), lambda i: (0, 0))
    return pl.pallas_call(
        body, name=name, grid=(L // tr,), in_specs=[row, row, vec], out_specs=(row, vec, vec),
        out_shape=(jax.ShapeDtypeStruct((L, Dm), F32), jax.ShapeDtypeStruct((1, Dm), F32), jax.ShapeDtypeStruct((1, Dm), F32)),
        compiler_params=pltpu.CompilerParams(dimension_semantics=("arbitrary",)),
    )(dy, z, g.reshape(1, Dm))


def loss_fwd_bwd(h, target, *, name):
    L, Dm = h.shape
    nb = L // BLK

    def body(h_ref, t_ref, l_ref, dh_ref):
        i = pl.program_id(0)

        @pl.when(i == 0)
        def _():
            l_ref[...] = jnp.zeros_like(l_ref)
            dh_ref[...] = jnp.zeros_like(dh_ref)

        @pl.when(i > 0)
        def _():
            e = h_ref[...] - t_ref[...]
            dh_ref[...] = e * (1.0 / Dm)
            part = jnp.sum(jnp.sum(jnp.square(e), axis=-1, keepdims=True) * (1.0 / Dm), axis=0, keepdims=True)
            l_ref[...] += 0.5 * part

    return pl.pallas_call(
        body, name=name, grid=(nb,),
        in_specs=[pl.BlockSpec((BLK, Dm), lambda i: (i, 0)),
                  pl.BlockSpec((BLK, Dm), lambda i: (jnp.maximum(i - 1, 0), 0))],
        out_specs=(pl.BlockSpec((8, 128), lambda i: (0, 0)), pl.BlockSpec((BLK, Dm), lambda i: (i, 0))),
        out_shape=(jax.ShapeDtypeStruct((8, 128), F32), jax.ShapeDtypeStruct((L, Dm), F32)),
        compiler_params=pltpu.CompilerParams(dimension_semantics=("arbitrary",)),
    )(h, target)


def _swap_half(x, half):
    ax = x.ndim - 1
    n = x.shape[ax]
    lane = lax.broadcasted_iota(jnp.int32, x.shape, ax)
    up = pltpu.roll(x, n - half, ax)
    dn = pltpu.roll(x, half, ax)
    return jnp.where((lane % (2 * half)) < half, up, dn)


def _rope(x, cs, sn, half):
    return x * cs + _swap_half(x, half) * sn


def _rope_t(dy, cs, sn, half):
    return dy * cs + _swap_half(dy * sn, half)


def _rms(x):
    r = lax.rsqrt(jnp.mean(jnp.square(x), axis=-1, keepdims=True) + LN_EPS)
    return x * r, r


def mla_pre_fwd(P, gq, gkv, cs, sn, *, name):
    L = P.shape[0]
    tr = _row_tile(L)

    def body(p_ref, gq_ref, gkv_ref, cs_ref, sn_ref, nq_ref, nkv_ref, kr_ref):
        cq = p_ref[:, 0:MLA_Q_LORA]
        ckv = p_ref[:, MLA_Q_LORA:MLA_Q_LORA + MLA_KV_LORA]
        kr = p_ref[:, 640:768]
        nq_ref[...] = (_rms(cq)[0] * gq_ref[...]).astype(BF16)
        nkv_ref[...] = (_rms(ckv)[0] * gkv_ref[...]).astype(BF16)
        krr = _rope(kr, cs_ref[...], sn_ref[...], MLA_ROPE // 2)
        kr_ref[...] = (krr + pltpu.roll(krr, MLA_ROPE, 1)).astype(BF16)

    return pl.pallas_call(
        body, name=name, grid=(L // tr,),
        in_specs=[pl.BlockSpec((tr, 768), lambda i: (i, C_CQ // 768)),
                  pl.BlockSpec((1, MLA_Q_LORA), lambda i: (0, 0)), pl.BlockSpec((1, MLA_KV_LORA), lambda i: (0, 0)),
                  pl.BlockSpec((tr, 128), lambda i: (i, 0)), pl.BlockSpec((tr, 128), lambda i: (i, 0))],
        out_specs=(pl.BlockSpec((tr, MLA_Q_LORA), lambda i: (i, 0)), pl.BlockSpec((tr, MLA_KV_LORA), lambda i: (i, 0)),
                   pl.BlockSpec((tr, 128), lambda i: (i, 0))),
        out_shape=(jax.ShapeDtypeStruct((L, MLA_Q_LORA), BF16), jax.ShapeDtypeStruct((L, MLA_KV_LORA), BF16),
                   jax.ShapeDtypeStruct((L, 128), BF16)),
    )(P, gq.reshape(1, -1), gkv.reshape(1, -1), cs, sn)


def mla_pre_bwd(P, dnq, dnkv, dkr, gq, gkv, cs, sn, *, name):
    L = P.shape[0]
    tr = _row_tile(L)

    def body(p_ref, dnq_ref, dnkv_ref, dkr_ref, gq_ref, gkv_ref, cs_ref, sn_ref, dp_ref, dgq_ref, dgkv_ref):
        @pl.when(pl.program_id(0) == 0)
        def _():
            dgq_ref[...] = jnp.zeros_like(dgq_ref)
            dgkv_ref[...] = jnp.zeros_like(dgkv_ref)

        def rms_bwd(x, dy, g_ref, dg_ref):
            xn, r = _rms(x)
            dxn = dy * g_ref[...]
            dg_ref[...] += jnp.sum(dy * xn, axis=0, keepdims=True)
            return r * (dxn - xn * jnp.mean(dxn * xn, axis=-1, keepdims=True))

        dp_ref[:, 0:MLA_Q_LORA] = rms_bwd(p_ref[:, 0:MLA_Q_LORA], dnq_ref[...], gq_ref, dgq_ref)
        dp_ref[:, MLA_Q_LORA:640] = rms_bwd(p_ref[:, MLA_Q_LORA:640], dnkv_ref[...], gkv_ref, dgkv_ref)
        d2 = dkr_ref[...]
        lane = lax.broadcasted_iota(jnp.int32, d2.shape, 1)
        dkr = jnp.where(lane < MLA_ROPE, d2 + pltpu.roll(d2, 128 - MLA_ROPE, 1), 0.0)
        dp_ref[:, 640:768] = _rope_t(dkr, cs_ref[...], sn_ref[...], MLA_ROPE // 2)

    return pl.pallas_call(
        body, name=name, grid=(L // tr,),
        in_specs=[pl.BlockSpec((tr, 768), lambda i: (i, C_CQ // 768)),
                  pl.BlockSpec((tr, MLA_Q_LORA), lambda i: (i, 0)), pl.BlockSpec((tr, MLA_KV_LORA), lambda i: (i, 0)),
                  pl.BlockSpec((tr, 128), lambda i: (i, 0)),
                  pl.BlockSpec((1, MLA_Q_LORA), lambda i: (0, 0)), pl.BlockSpec((1, MLA_KV_LORA), lambda i: (0, 0)),
                  pl.BlockSpec((tr, 128), lambda i: (i, 0)), pl.BlockSpec((tr, 128), lambda i: (i, 0))],
        out_specs=(pl.BlockSpec((tr, 768), lambda i: (i, 0)), pl.BlockSpec((1, MLA_Q_LORA), lambda i: (0, 0)),
                   pl.BlockSpec((1, MLA_KV_LORA), lambda i: (0, 0))),
        out_shape=(jax.ShapeDtypeStruct((L, 768), F32), jax.ShapeDtypeStruct((1, MLA_Q_LORA), F32),
                   jax.ShapeDtypeStruct((1, MLA_KV_LORA), F32)),
        compiler_params=pltpu.CompilerParams(dimension_semantics=("arbitrary",)),
    )(P, dnq, dnkv, dkr, gq.reshape(1, -1), gkv.reshape(1, -1), cs, sn)


def _tri(kind):
    r = lax.broadcasted_iota(jnp.int32, (BLK, BLK), 0)
    c = lax.broadcasted_iota(jnp.int32, (BLK, BLK), 1)
    t = ((r > c) if kind == "right" else (r < c)).astype(BF16)
    return jnp.concatenate([t, t], axis=0)


def _tri_sum(x, tt):
    hi = x.astype(BF16)
    lo = (x - hi.astype(F32)).astype(BF16)
    return _dot(jnp.concatenate([hi, lo], axis=1), tt)


def _sb_tile(q, k, i, j, tt_right, R):
    row = lax.broadcasted_iota(jnp.int32, (BLK, BLK), 0)
    col = lax.broadcasted_iota(jnp.int32, (BLK, BLK), 1)
    s_idx = j * BLK + col
    mask = (s_idx < i * BLK + row) & (s_idx >= N_PAD)
    z = _dot_nt(q, k)
    sp = jnp.maximum(z, 0.0) + jnp.log1p(jnp.exp(-jnp.abs(z)))
    lk = jnp.where(mask, -sp, 0.0)
    E = _tri_sum(lk, tt_right) + R
    return mask, z, sp, lk, E


def _old_sb_fwd(P, *, name):
    L = P.shape[0]
    nb = L // BLK

    def body(q_ref, k_ref, v_ref, o_ref, c_ref):
        i = pl.program_id(1)
        tt = _tri("right")
        lane = lax.broadcasted_iota(jnp.int32, (BLK, 128), 1)
        qs = [(q_ref[:, 64 * h:64 * h + 64] * 0.125).astype(BF16) for h in range(2)]

        def step(jj, carry):
            j = i - jj
            off = pl.multiple_of(j * BLK, BLK)
            kb = k_ref[pl.ds(off, BLK), :].astype(BF16)
            vb = v_ref[pl.ds(off, BLK), :].astype(BF16)
            out = []
            for h in range(2):
                o, R = carry[2 * h], carry[2 * h + 1]
                sl = slice(64 * h, 64 * h + 64)
                mask, z, sp, lk, E = _sb_tile(qs[h], kb[:, sl], i, j, tt, R)
                w = jnp.where(mask, jnp.exp(z - sp + E), 0.0)
                c_ref[h] = jnp.where(lane == j, R, c_ref[h])
                out += [o + _dot(w.astype(BF16), vb[:, sl]), R + jnp.sum(lk, axis=1, keepdims=True)]
            return tuple(out)

        c_ref[...] = jnp.zeros_like(c_ref)
        z0 = (jnp.zeros((BLK, 64), F32), jnp.zeros((BLK, 1), F32))
        res = lax.fori_loop(0, i + 1, step, z0 + z0)
        o_ref[...] = jnp.concatenate([res[0], res[2]], axis=1)

    return pl.pallas_call(
        body, name=name, grid=(SB_HEADS // 2, nb),
        in_specs=[pl.BlockSpec((BLK, 128), lambda hp, i: (i, C_SBQ // 128 + hp)),
                  pl.BlockSpec((L, 128), lambda hp, i: (0, C_SBK // 128 + hp)),
                  pl.BlockSpec((L, 128), lambda hp, i: (0, C_SBV // 128 + hp))],
        out_specs=(pl.BlockSpec((BLK, 128), lambda hp, i: (i, hp)), pl.BlockSpec((2, BLK, 128), lambda hp, i: (hp, i, 0))),
        out_shape=(jax.ShapeDtypeStruct((L, 512), F32), jax.ShapeDtypeStruct((SB_HEADS, L, 128), F32)),
        compiler_params=pltpu.CompilerParams(dimension_semantics=("parallel", "arbitrary")),
    )(P, P, P)


def _old_sb_bwd(P, carries, dmixed, *, name):
    L = P.shape[0]
    nb = L // BLK

    def body(q_ref, k_ref, v_ref, c_ref, do_ref, dq_ref, dk_ref, dv_ref):
        i = pl.program_id(1)

        @pl.when(i == 0)
        def _():
            dk_ref[...] = jnp.zeros_like(dk_ref)
            dv_ref[...] = jnp.zeros_like(dv_ref)

        tr = _tri("right")
        tl = _tri("left")
        lane = lax.broadcasted_iota(jnp.int32, (BLK, 128), 1)
        qs = [(q_ref[:, 64 * h:64 * h + 64] * 0.125).astype(BF16) for h in range(2)]
        dos = [do_ref[:, 64 * h:64 * h + 64].astype(BF16) for h in range(2)]

        def step(j, carry):
            off = pl.multiple_of(j * BLK, BLK)
            kb = k_ref[pl.ds(off, BLK), :].astype(BF16)
            vb = v_ref[pl.ds(off, BLK), :].astype(BF16)
            out, dks, dvs = [], [], []
            for h in range(2):
                dq, PL = carry[2 * h], carry[2 * h + 1]
                sl = slice(64 * h, 64 * h + 64)
                R = jnp.sum(jnp.where(lane == j, c_ref[h], 0.0), axis=1, keepdims=True)
                mask, z, sp, lk, E = _sb_tile(qs[h], kb[:, sl], i, j, tr, R)
                sig = jnp.exp(z - sp)
                w = jnp.where(mask, sig * jnp.exp(E), 0.0)
                dA = _dot_nt(dos[h], vb[:, sl]) * w
                Pp = _tri_sum(dA, tl) + PL
                dz = jnp.where(mask, dA - sig * (dA + Pp), 0.0).astype(BF16)
                dks.append(_dot_tn(dz, qs[h]))
                dvs.append(_dot_tn(w.astype(BF16), dos[h]))
                out += [dq + _dot(dz, kb[:, sl]), PL + jnp.sum(dA, axis=1, keepdims=True)]
            dk_ref[pl.ds(off, BLK), :] += jnp.concatenate(dks, axis=1)
            dv_ref[pl.ds(off, BLK), :] += jnp.concatenate(dvs, axis=1)
            return tuple(out)

        z0 = (jnp.zeros((BLK, 64), F32), jnp.zeros((BLK, 1), F32))
        res = lax.fori_loop(0, i + 1, step, z0 + z0)
        dq_ref[...] = jnp.concatenate([res[0], res[2]], axis=1) * 0.125

    blk = lambda c0: pl.BlockSpec((BLK, 128), lambda hp, i: (i, c0 + hp))
    full = lambda c0: pl.BlockSpec((L, 128), lambda hp, i: (0, c0 + hp))
    sds = jax.ShapeDtypeStruct((L, 512), F32)
    return pl.pallas_call(
        body, name=name, grid=(SB_HEADS // 2, nb),
        in_specs=[blk(C_SBQ // 128), full(C_SBK // 128), full(C_SBV // 128),
                  pl.BlockSpec((2, BLK, 128), lambda hp, i: (hp, i, 0)), blk(0)],
        out_specs=(blk(0), full(0), full(0)), out_shape=(sds, sds, sds),
        compiler_params=pltpu.CompilerParams(dimension_semantics=("parallel", "arbitrary")),
    )(P, P, P, carries, dmixed)


def _mla_mask(i, j):
    row = lax.broadcasted_iota(jnp.int32, (BLK, BLK), 0)
    col = lax.broadcasted_iota(jnp.int32, (BLK, BLK), 1)
    t_idx = i * BLK + row
    s_idx = j * BLK + col
    return (s_idx <= t_idx) & ((s_idx >= N_PAD) | (s_idx == t_idx))


def _mla_q(q_ref, cs_ref, sn_ref):
    qr = _rope(q_ref[:, 128:256], cs_ref[...], sn_ref[...], MLA_ROPE // 2)
    qn = [q_ref[:, 64 * h:64 * h + 64].astype(BF16) for h in range(2)]
    qrs = [qr[:, 32 * h:32 * h + 32].astype(BF16) for h in range(2)]
    return qn, qrs


def _old_mla_fwd(Q, KV, KR, cs, sn, *, name):
    L = Q.shape[0]
    nb = L // BLK

    def body(q_ref, kn_ref, v_ref, kr_ref, cs_ref, sn_ref, o_ref, lse_ref):
        i = pl.program_id(1)
        qn, qrs = _mla_q(q_ref, cs_ref, sn_ref)

        def step(j, carry):
            off = pl.multiple_of(j * BLK, BLK)
            knb = kn_ref[pl.ds(off, BLK), :]
            vb = v_ref[pl.ds(off, BLK), :]
            krb = kr_ref[pl.ds(off, BLK), 0:MLA_ROPE]
            mask = _mla_mask(i, j)
            out = []
            for h in range(2):
                m, l, acc = carry[3 * h], carry[3 * h + 1], carry[3 * h + 2]
                sl = slice(64 * h, 64 * h + 64)
                s = (_dot_nt(qn[h], knb[:, sl]) + _dot_nt(qrs[h], krb)) * MLA_SCALE
                s = jnp.where(mask, s, NEG)
                m_new = jnp.maximum(m, jnp.max(s, axis=1, keepdims=True))
                a = jnp.exp(m - m_new)
                p = jnp.exp(s - m_new)
                out += [m_new, a * l + jnp.sum(p, axis=1, keepdims=True), a * acc + _dot(p.astype(BF16), vb[:, sl])]
            return tuple(out)

        z0 = (jnp.full((BLK, 1), NEG, F32), jnp.zeros((BLK, 1), F32), jnp.zeros((BLK, 64), F32))
        res = lax.fori_loop(0, i + 1, step, z0 + z0)
        o_ref[...] = jnp.concatenate([res[2] / res[1], res[5] / res[4]], axis=1)
        lane = lax.broadcasted_iota(jnp.int32, (BLK, 128), 1)
        lse0 = res[0] + jnp.log(res[1])
        lse1 = res[3] + jnp.log(res[4])
        lse_ref[0] = jnp.where(lane == 0, lse0, jnp.where(lane == 1, lse1, 0.0))

    return pl.pallas_call(
        body, name=name, grid=(MLA_HEADS // 2, nb),
        in_specs=[pl.BlockSpec((BLK, 256), lambda hp, i: (i, hp)),
                  pl.BlockSpec((L, 128), lambda hp, i: (0, hp)),
                  pl.BlockSpec((L, 128), lambda hp, i: (0, 4 + hp)),
                  pl.BlockSpec((L, 128), lambda hp, i: (0, 0)),
                  pl.BlockSpec((BLK, 128), lambda hp, i: (i, 0)), pl.BlockSpec((BLK, 128), lambda hp, i: (i, 0))],
        out_specs=(pl.BlockSpec((BLK, 128), lambda hp, i: (i, hp)), pl.BlockSpec((1, BLK, 128), lambda hp, i: (hp, i, 0))),
        out_shape=(jax.ShapeDtypeStruct((L, 512), F32), jax.ShapeDtypeStruct((4, L, 128), F32)),
        compiler_params=pltpu.CompilerParams(dimension_semantics=("parallel", "arbitrary")),
    )(Q, KV, KV, KR, cs, sn)


def _old_mla_bwd(Q, KV, KR, cs, sn, mixed, dmixed, lse, *, name):
    L = Q.shape[0]
    nb = L // BLK

    def body(q_ref, kn_ref, v_ref, kr_ref, cs_ref, sn_ref, o_ref, do_ref, lse_ref, dq_ref, dkn_ref, dv_ref, dkr_ref):
        hp = pl.program_id(0)
        i = pl.program_id(1)

        @pl.when(i == 0)
        def _():
            dkn_ref[...] = jnp.zeros_like(dkn_ref)
            dv_ref[...] = jnp.zeros_like(dv_ref)

        @pl.when((i == 0) & (hp == 0))
        def _():
            dkr_ref[...] = jnp.zeros_like(dkr_ref)

        qn, qrs = _mla_q(q_ref, cs_ref, sn_ref)
        dos, dd, lses = [], [], []
        for h in range(2):
            sl = slice(64 * h, 64 * h + 64)
            d = do_ref[:, sl]
            dos.append(d.astype(BF16))
            dd.append(jnp.sum(d * o_ref[:, sl], axis=1, keepdims=True))
            lses.append(lse_ref[0, :, h:h + 1])

        def step(j, carry):
            off = pl.multiple_of(j * BLK, BLK)
            knb = kn_ref[pl.ds(off, BLK), :]
            vb = v_ref[pl.ds(off, BLK), :]
            krb = kr_ref[pl.ds(off, BLK), 0:MLA_ROPE]
            mask = _mla_mask(i, j)
            out, dkns, dvs = [], [], []
            dkr = jnp.zeros((BLK, MLA_ROPE), F32)
            for h in range(2):
                dqn, dqr = carry[2 * h], carry[2 * h + 1]
                sl = slice(64 * h, 64 * h + 64)
                s = (_dot_nt(qn[h], knb[:, sl]) + _dot_nt(qrs[h], krb)) * MLA_SCALE
                p = jnp.where(mask, jnp.exp(s - lses[h]), 0.0)
                dp = _dot_nt(dos[h], vb[:, sl])
                ds = (p * (dp - dd[h]) * MLA_SCALE).astype(BF16)
                dkns.append(_dot_tn(ds, qn[h]))
                dvs.append(_dot_tn(p.astype(BF16), dos[h]))
                dkr = dkr + _dot_tn(ds, qrs[h])
                out += [dqn + _dot(ds, knb[:, sl]), dqr + _dot(ds, krb)]
            dkn_ref[pl.ds(off, BLK), :] += jnp.concatenate(dkns, axis=1)
            dv_ref[pl.ds(off, BLK), :] += jnp.concatenate(dvs, axis=1)
            dkr_ref[pl.ds(off, BLK), :] += jnp.concatenate([dkr, jnp.zeros((BLK, 128 - MLA_ROPE), F32)], axis=1)
            return tuple(out)

        z0 = (jnp.zeros((BLK, 64), F32), jnp.zeros((BLK, MLA_ROPE), F32))
        res = lax.fori_loop(0, i + 1, step, z0 + z0)
        dqr = jnp.concatenate([res[1], res[3], jnp.zeros((BLK, 64), F32)], axis=1)
        dq_ref[...] = jnp.concatenate([res[0], res[2], _rope_t(dqr, cs_ref[...], sn_ref[...], MLA_ROPE // 2)], axis=1)

    blk = lambda c0: pl.BlockSpec((BLK, 128), lambda hp, i: (i, c0 + hp))
    full = lambda c0: pl.BlockSpec((L, 128), lambda hp, i: (0, c0 + hp))
    tab = pl.BlockSpec((BLK, 128), lambda hp, i: (i, 0))
    return pl.pallas_call(
        body, name=name, grid=(MLA_HEADS // 2, nb),
        in_specs=[pl.BlockSpec((BLK, 256), lambda hp, i: (i, hp)), full(0), full(4),
                  pl.BlockSpec((L, 128), lambda hp, i: (0, 0)), tab, tab, blk(4), blk(4),
                  pl.BlockSpec((1, BLK, 128), lambda hp, i: (hp, i, 0))],
        out_specs=(pl.BlockSpec((BLK, 256), lambda hp, i: (i, hp)), full(0), full(0),
                   pl.BlockSpec((L, 128), lambda hp, i: (0, 0))),
        out_shape=(jax.ShapeDtypeStruct((L, 1024), F32), jax.ShapeDtypeStruct((L, 512), F32),
                   jax.ShapeDtypeStruct((L, 512), F32), jax.ShapeDtypeStruct((L, 128), F32)),
        compiler_params=pltpu.CompilerParams(dimension_semantics=("arbitrary", "arbitrary")),
    )(Q, KV, KV, KR, cs, sn, mixed, dmixed, lse)


SB_UNROLL = 2


def _tq(L):
    return 384 if L % 384 == 0 else BLK


def _softplus(z):
    return jnp.maximum(z, 0.0) + jnp.log(1.0 + jnp.exp(-jnp.abs(z)))


def _head_split(x, first):
    zero = jnp.zeros_like(x)
    return jnp.where(first, x, zero), jnp.where(first, zero, x)


def _sb_mask(I, j, tq):
    row = lax.broadcasted_iota(jnp.int32, (tq, BLK), 0)
    col = lax.broadcasted_iota(jnp.int32, (tq, BLK), 1)
    s_idx = j * BLK + col
    return (s_idx < I * tq + row) & (s_idx >= N_PAD)


def _tri2(kind, splits):
    r = lax.broadcasted_iota(jnp.int32, (256, 256), 0)
    c = lax.broadcasted_iota(jnp.int32, (256, 256), 1)
    same = (r < BLK) == (c < BLK)
    t = (same & ((r > c) if kind == "right" else (r < c))).astype(BF16)
    return jnp.concatenate([t] * splits, axis=0)


def _split2(x):
    hi = x.astype(BF16)
    lo = (x - hi.astype(F32)).astype(BF16)
    return jnp.concatenate([hi, lo], axis=1)


def _sb_mask2(I, j, tq):
    row = lax.broadcasted_iota(jnp.int32, (tq, 256), 0)
    col = lax.broadcasted_iota(jnp.int32, (tq, 256), 1)
    s_idx = j * BLK + (col & (BLK - 1))
    return (s_idx < I * tq + row) & (s_idx >= N_PAD)


def _per_head(x, r0, r1):
    return jnp.concatenate([x[:, 0:BLK] + r0, x[:, BLK:2 * BLK] + r1], axis=1)


def sb_fwd(P, *, name):
    L = P.shape[0]
    tq = _tq(L)
    nd = tq // BLK

    def body(q_ref, k_ref, v_ref, o_ref, c_ref):
        I = pl.program_id(1)
        tt = _tri2("right", 2)
        lane_q = lax.broadcasted_iota(jnp.int32, (tq, 128), 1)
        first_k = lax.broadcasted_iota(jnp.int32, (BLK, 128), 1) < 64
        q = (q_ref[...] * 0.125).astype(BF16)
        c_ref[...] = jnp.zeros_like(c_ref)

        def tiles(T, carry, kind):
            o, R0, R1 = carry
            js = [T * nd + nd - 1 - u for u in range(nd)]
            st = []
            for j in js:
                off = pl.multiple_of(j * BLK, BLK)
                kcat = jnp.concatenate(_head_split(k_ref[pl.ds(off, BLK), :].astype(BF16), first_k), axis=0)
                st.append([_dot_nt(q, kcat), off])
            for u, (s, j) in enumerate(zip(st, js)):
                sp = _softplus(s[0])
                mask = _sb_mask2(I, j, tq) if kind == "diag" else (pad_ok if kind == "first" and u == nd - 1 else None)
                spm = sp if mask is None else jnp.where(mask, sp, 0.0)
                s += [sp, spm, mask, _dot(_split2(spm), tt)]
            for (z, off, sp, spm, mask, S), j in zip(st, js):
                vcat = jnp.concatenate(_head_split(v_ref[pl.ds(off, BLK), :].astype(BF16), first_k), axis=0)
                w = jnp.exp(_per_head(z - sp - S, R0, R1))
                if mask is not None:
                    w = jnp.where(mask, w, 0.0)
                c_ref[0] = jnp.where(lane_q == j, R0, c_ref[0])
                c_ref[1] = jnp.where(lane_q == j, R1, c_ref[1])
                tot = S + spm
                o, R0, R1 = o + _dot(w.astype(BF16), vcat), R0 - tot[:, 0:1], R1 - tot[:, BLK:BLK + 1]
            return (o, R0, R1)

        pad_ok = (lax.broadcasted_iota(jnp.int32, (tq, 256), 1) & (BLK - 1)) >= N_PAD
        carry = (jnp.zeros((tq, 128), F32), jnp.zeros((tq, 1), F32), jnp.zeros((tq, 1), F32))
        carry = tiles(I, carry, "diag")
        carry = lax.fori_loop(0, jnp.maximum(I - 1, 0), lambda t, c: tiles(I - 1 - t, c, None), carry)
        carry = lax.fori_loop(0, jnp.minimum(I, 1), lambda t, c: tiles(0, c, "first"), carry)
        o_ref[...] = carry[0]

    return pl.pallas_call(
        body, name=name, grid=(SB_HEADS // 2, L // tq),
        in_specs=[pl.BlockSpec((tq, 128), lambda hp, i: (i, C_SBQ // 128 + hp)),
                  pl.BlockSpec((L, 128), lambda hp, i: (0, C_SBK // 128 + hp)),
                  pl.BlockSpec((L, 128), lambda hp, i: (0, C_SBV // 128 + hp))],
        out_specs=(pl.BlockSpec((tq, 128), lambda hp, i: (i, hp)), pl.BlockSpec((2, tq, 128), lambda hp, i: (hp, i, 0))),
        out_shape=(jax.ShapeDtypeStruct((L, 512), F32), jax.ShapeDtypeStruct((SB_HEADS, L, 128), F32)),
        compiler_params=pltpu.CompilerParams(dimension_semantics=("parallel", "arbitrary")),
    )(P, P, P)


def sb_bwd(P, carries, dmixed, *, name):
    L = P.shape[0]
    tq = _tq(L)
    nd = tq // BLK

    def body(q_ref, k_ref, v_ref, c_ref, do_ref, dq_ref, dk_ref, dv_ref):
        I = pl.program_id(1)

        @pl.when(I == 0)
        def _():
            dk_ref[...] = jnp.zeros_like(dk_ref)
            dv_ref[...] = jnp.zeros_like(dv_ref)

        tr = _tri2("right", 2)
        tl = _tri2("left", 1)
        lane_q = lax.broadcasted_iota(jnp.int32, (tq, 128), 1)
        first_k = lax.broadcasted_iota(jnp.int32, (BLK, 128), 1) < 64
        q = (q_ref[...] * 0.125).astype(BF16)
        do = do_ref[...].astype(BF16)

        def tiles(T, carry, kind):
            dq, PL0, PL1 = carry
            js = [T * nd + u for u in range(nd)]
            st = []
            for j in js:
                off = pl.multiple_of(j * BLK, BLK)
                kcat = jnp.concatenate(_head_split(k_ref[pl.ds(off, BLK), :].astype(BF16), first_k), axis=0)
                vcat = jnp.concatenate(_head_split(v_ref[pl.ds(off, BLK), :].astype(BF16), first_k), axis=0)
                st.append([off, kcat, _dot_nt(q, kcat), _dot_nt(do, vcat)])
            for u, (s, j) in enumerate(zip(st, js)):
                z = s[2]
                sp = _softplus(z)
                mask = _sb_mask2(I, j, tq) if kind == "diag" else (pad_ok if kind == "first" and u == 0 else None)
                spm = sp if mask is None else jnp.where(mask, sp, 0.0)
                s += [mask, jnp.exp(z - sp), _dot(_split2(spm), tr)]
            for s, j in zip(st, js):
                off, kcat, z, dw, mask, sig, S = s
                R0 = jnp.sum(jnp.where(lane_q == j, c_ref[0], 0.0), axis=1, keepdims=True)
                R1 = jnp.sum(jnp.where(lane_q == j, c_ref[1], 0.0), axis=1, keepdims=True)
                w = sig * jnp.exp(_per_head(-S, R0, R1))
                if mask is not None:
                    w = jnp.where(mask, w, 0.0)
                dA = dw * w
                dvf = _dot_tn(w.astype(BF16), do)
                dv_ref[pl.ds(off, BLK), :] += jnp.where(first_k, dvf[0:BLK], dvf[BLK:2 * BLK])
                s += [dA, _dot(dA.astype(BF16), tl)]
            for off, kcat, z, dw, mask, sig, S, dA, pre in st:
                dz = dA - sig * (dA + _per_head(pre, PL0, PL1))
                if mask is not None:
                    dz = jnp.where(mask, dz, 0.0)
                dzb = dz.astype(BF16)
                dkf = _dot_tn(dzb, q)
                dk_ref[pl.ds(off, BLK), :] += jnp.where(first_k, dkf[0:BLK], dkf[BLK:2 * BLK])
                tot = pre + dA
                dq, PL0, PL1 = dq + _dot(dzb, kcat), PL0 + tot[:, BLK - 1:BLK], PL1 + tot[:, 2 * BLK - 1:2 * BLK]
            return (dq, PL0, PL1)

        pad_ok = (lax.broadcasted_iota(jnp.int32, (tq, 256), 1) & (BLK - 1)) >= N_PAD
        carry = (jnp.zeros((tq, 128), F32), jnp.zeros((tq, 1), F32), jnp.zeros((tq, 1), F32))
        carry = lax.fori_loop(0, jnp.minimum(I, 1), lambda t, c: tiles(0, c, "first"), carry)
        carry = lax.fori_loop(1, jnp.maximum(I, 1), lambda T, c: tiles(T, c, None), carry)
        carry = tiles(I, carry, "diag")
        dq_ref[...] = carry[0] * 0.125

    blk = lambda c0: pl.BlockSpec((tq, 128), lambda hp, i: (i, c0 + hp))
    full = lambda c0: pl.BlockSpec((L, 128), lambda hp, i: (0, c0 + hp))
    sds = jax.ShapeDtypeStruct((L, 512), F32)
    return pl.pallas_call(
        body, name=name, grid=(SB_HEADS // 2, L // tq),
        in_specs=[blk(C_SBQ // 128), full(C_SBK // 128), full(C_SBV // 128),
                  pl.BlockSpec((2, tq, 128), lambda hp, i: (hp, i, 0)), blk(0)],
        out_specs=(blk(0), full(0), full(0)), out_shape=(sds, sds, sds),
        compiler_params=pltpu.CompilerParams(dimension_semantics=("parallel", "arbitrary")),
    )(P, P, P, carries, dmixed)


def _v2_sb_fwd(P, *, name):
    L = P.shape[0]
    tq = _tq(L)
    nd = tq // BLK

    def body(q_ref, k_ref, v_ref, o_ref, c_ref):
        I = pl.program_id(1)
        tt = _tri("right")
        lane_q = lax.broadcasted_iota(jnp.int32, (tq, 128), 1)
        first_k = lax.broadcasted_iota(jnp.int32, (BLK, 128), 1) < 64
        qm = [x.astype(BF16) for x in _head_split(q_ref[...] * 0.125, lane_q < 64)]
        c_ref[...] = jnp.zeros_like(c_ref)

        def tile(j, carry, masked):
            o, R = carry[0], carry[1:]
            off = pl.multiple_of(j * BLK, BLK)
            kb = k_ref[pl.ds(off, BLK), :].astype(BF16)
            vcat = jnp.concatenate(_head_split(v_ref[pl.ds(off, BLK), :].astype(BF16), first_k), axis=0)
            mask = _sb_mask(I, j, tq) if masked else None
            ws, Rn = [], []
            for h in range(2):
                z = _dot_nt(qm[h], kb)
                sp = _softplus(z)
                spm = jnp.where(mask, sp, 0.0) if masked else sp
                w = jnp.exp(z - sp - _tri_sum(spm, tt) + R[h])
                if masked:
                    w = jnp.where(mask, w, 0.0)
                c_ref[h] = jnp.where(lane_q == j, R[h], c_ref[h])
                ws.append(w.astype(BF16))
                Rn.append(R[h] - jnp.sum(spm, axis=1, keepdims=True))
            return (o + _dot(jnp.concatenate(ws, axis=1), vcat), Rn[0], Rn[1])

        carry = (jnp.zeros((tq, 128), F32), jnp.zeros((tq, 1), F32), jnp.zeros((tq, 1), F32))
        carry = lax.fori_loop(0, nd, lambda t, c: tile(I * nd + nd - 1 - t, c, True), carry)
        carry = lax.fori_loop(0, jnp.maximum(I * nd - 1, 0), lambda t, c: tile(I * nd - 1 - t, c, False), carry)
        carry = lax.fori_loop(0, jnp.minimum(I, 1), lambda t, c: tile(0, c, True), carry)
        o_ref[...] = carry[0]

    return pl.pallas_call(
        body, name=name, grid=(SB_HEADS // 2, L // tq),
        in_specs=[pl.BlockSpec((tq, 128), lambda hp, i: (i, C_SBQ // 128 + hp)),
                  pl.BlockSpec((L, 128), lambda hp, i: (0, C_SBK // 128 + hp)),
                  pl.BlockSpec((L, 128), lambda hp, i: (0, C_SBV // 128 + hp))],
        out_specs=(pl.BlockSpec((tq, 128), lambda hp, i: (i, hp)), pl.BlockSpec((2, tq, 128), lambda hp, i: (hp, i, 0))),
        out_shape=(jax.ShapeDtypeStruct((L, 512), F32), jax.ShapeDtypeStruct((SB_HEADS, L, 128), F32)),
        compiler_params=pltpu.CompilerParams(dimension_semantics=("parallel", "arbitrary")),
    )(P, P, P)


def _v2_sb_bwd(P, carries, dmixed, *, name):
    L = P.shape[0]
    tq = _tq(L)
    nd = tq // BLK

    def body(q_ref, k_ref, v_ref, c_ref, do_ref, dq_ref, dk_ref, dv_ref):
        I = pl.program_id(1)

        @pl.when(I == 0)
        def _():
            dk_ref[...] = jnp.zeros_like(dk_ref)
            dv_ref[...] = jnp.zeros_like(dv_ref)

        tr = _tri("right")
        tl = _tri("left")
        lane_q = lax.broadcasted_iota(jnp.int32, (tq, 128), 1)
        first_k = lax.broadcasted_iota(jnp.int32, (BLK, 128), 1) < 64
        qm = [x.astype(BF16) for x in _head_split(q_ref[...] * 0.125, lane_q < 64)]
        dom = [x.astype(BF16) for x in _head_split(do_ref[...], lane_q < 64)]
        qcat = jnp.concatenate(qm, axis=0)
        docat = jnp.concatenate(dom, axis=0)

        def tile(j, carry, masked):
            dq, PL = carry[0], carry[1:]
            off = pl.multiple_of(j * BLK, BLK)
            kb = k_ref[pl.ds(off, BLK), :].astype(BF16)
            vb = v_ref[pl.ds(off, BLK), :].astype(BF16)
            kcat = jnp.concatenate(_head_split(kb, first_k), axis=0)
            mask = _sb_mask(I, j, tq) if masked else None
            dzs, wsb, PLn = [], [], []
            for h in range(2):
                R = jnp.sum(jnp.where(lane_q == j, c_ref[h], 0.0), axis=1, keepdims=True)
                z = _dot_nt(qm[h], kb)
                sp = _softplus(z)
                spm = jnp.where(mask, sp, 0.0) if masked else sp
                sig = jnp.exp(z - sp)
                w = sig * jnp.exp(R - _tri_sum(spm, tr))
                if masked:
                    w = jnp.where(mask, w, 0.0)
                dA = _dot_nt(dom[h], vb) * w
                dz = dA - sig * (dA + _tri_sum(dA, tl) + PL[h])
                if masked:
                    dz = jnp.where(mask, dz, 0.0)
                dzs.append(dz.astype(BF16))
                wsb.append(w.astype(BF16))
                PLn.append(PL[h] + jnp.sum(dA, axis=1, keepdims=True))
            dk_ref[pl.ds(off, BLK), :] += _dot_tn(jnp.concatenate(dzs, axis=0), qcat)
            dv_ref[pl.ds(off, BLK), :] += _dot_tn(jnp.concatenate(wsb, axis=0), docat)
            return (dq + _dot(jnp.concatenate(dzs, axis=1), kcat), PLn[0], PLn[1])

        carry = (jnp.zeros((tq, 128), F32), jnp.zeros((tq, 1), F32), jnp.zeros((tq, 1), F32))
        carry = lax.fori_loop(0, jnp.minimum(I, 1), lambda t, c: tile(0, c, True), carry)
        carry = lax.fori_loop(1, jnp.maximum(I * nd, 1), lambda j, c: tile(j, c, False), carry)
        carry = lax.fori_loop(0, nd, lambda t, c: tile(I * nd + t, c, True), carry)
        dq_ref[...] = carry[0] * 0.125

    blk = lambda c0: pl.BlockSpec((tq, 128), lambda hp, i: (i, c0 + hp))
    full = lambda c0: pl.BlockSpec((L, 128), lambda hp, i: (0, c0 + hp))
    sds = jax.ShapeDtypeStruct((L, 512), F32)
    return pl.pallas_call(
        body, name=name, grid=(SB_HEADS // 2, L // tq),
        in_specs=[blk(C_SBQ // 128), full(C_SBK // 128), full(C_SBV // 128),
                  pl.BlockSpec((2, tq, 128), lambda hp, i: (hp, i, 0)), blk(0)],
        out_specs=(blk(0), full(0), full(0)), out_shape=(sds, sds, sds),
        compiler_params=pltpu.CompilerParams(dimension_semantics=("parallel", "arbitrary")),
    )(P, P, P, carries, dmixed)


def _mla_mask2(I, j, tq):
    row = lax.broadcasted_iota(jnp.int32, (tq, tq), 0)
    col = lax.broadcasted_iota(jnp.int32, (tq, tq), 1)
    t_idx = I * tq + row
    s_idx = j * tq + col
    return (s_idx <= t_idx) & ((s_idx >= N_PAD) | (s_idx == t_idx))


def _mla_qcat(q_ref, cs_ref, sn_ref, lane_q):
    qn = q_ref[:, 0:128]
    qr = _rope(q_ref[:, 128:256], cs_ref[...], sn_ref[...], MLA_ROPE // 2)
    zero = jnp.zeros_like(qn)
    r0 = lane_q < MLA_ROPE
    r1 = (lane_q >= MLA_ROPE) & (lane_q < 2 * MLA_ROPE)
    n0, n1 = _head_split(qn, lane_q < 64)
    return [jnp.concatenate([n0, jnp.where(r0, qr, zero)], axis=1).astype(BF16),
            jnp.concatenate([n1, jnp.where(r1, qr, zero)], axis=1).astype(BF16)]


def mla_fwd(Q, KV, KR, cs, sn, *, name):
    L = Q.shape[0]
    tq = _tq(L)

    def body(q_ref, kn_ref, v_ref, kr_ref, cs_ref, sn_ref, o_ref, lse_ref):
        I = pl.program_id(1)
        lane_q = lax.broadcasted_iota(jnp.int32, (tq, 128), 1)
        first_q = lane_q < 64
        qcat = _mla_qcat(q_ref, cs_ref, sn_ref, lane_q)

        def tile(j, carry, masked, wide=1):
            acc, ml = carry[0], carry[1:]
            off = pl.multiple_of(j * tq, tq)
            tk = wide * tq
            first_k = lax.broadcasted_iota(jnp.int32, (tk, 128), 1) < 64
            kcat = jnp.concatenate([kn_ref[pl.ds(off, tk), :], kr_ref[pl.ds(off, tk), :]], axis=1)
            vcat = jnp.concatenate(_head_split(v_ref[pl.ds(off, tk), :], first_k), axis=0)
            mask = _mla_mask2(I, j, tq) if masked else None
            ps, al, out = [], [], []
            for h in range(2):
                m, l = ml[2 * h], ml[2 * h + 1]
                s = _dot_nt(qcat[h], kcat) * MLA_SCALE
                if masked:
                    s = jnp.where(mask, s, NEG)
                m_new = jnp.maximum(m, jnp.max(s, axis=1, keepdims=True))
                a = jnp.exp(m - m_new)
                p = jnp.exp(s - m_new)
                ps.append(p.astype(BF16))
                al.append(a)
                out += [m_new, a * l + jnp.sum(p, axis=1, keepdims=True)]
            acc = acc * jnp.where(first_q, al[0], al[1]) + _dot(jnp.concatenate(ps, axis=1), vcat)
            return (acc,) + tuple(out)

        ml0 = (jnp.full((tq, 1), NEG, F32), jnp.zeros((tq, 1), F32))
        carry = (jnp.zeros((tq, 128), F32),) + ml0 + ml0
        carry = lax.fori_loop(0, jnp.minimum(I, 1), lambda t, c: tile(0, c, True), carry)
        n_in = jnp.maximum(I - 1, 0)
        carry = lax.fori_loop(0, n_in // 2, lambda t, c: tile(1 + 2 * t, c, False, 2), carry)
        carry = lax.fori_loop(0, n_in % 2, lambda t, c: tile(I - 1, c, False), carry)
        carry = tile(I, carry, True)
        acc, m0, l0, m1, l1 = carry
        o_ref[...] = acc / jnp.where(first_q, l0, l1)
        lse_ref[0] = jnp.where(lane_q == 0, m0 + jnp.log(l0), jnp.where(lane_q == 1, m1 + jnp.log(l1), 0.0))

    return pl.pallas_call(
        body, name=name, grid=(MLA_HEADS // 2, L // tq),
        in_specs=[pl.BlockSpec((tq, 256), lambda hp, i: (i, hp)),
                  pl.BlockSpec((L, 128), lambda hp, i: (0, hp)),
                  pl.BlockSpec((L, 128), lambda hp, i: (0, 4 + hp)),
                  pl.BlockSpec((L, 128), lambda hp, i: (0, 0)),
                  pl.BlockSpec((tq, 128), lambda hp, i: (i, 0)), pl.BlockSpec((tq, 128), lambda hp, i: (i, 0))],
        out_specs=(pl.BlockSpec((tq, 128), lambda hp, i: (i, hp)), pl.BlockSpec((1, tq, 128), lambda hp, i: (hp, i, 0))),
        out_shape=(jax.ShapeDtypeStruct((L, 512), F32), jax.ShapeDtypeStruct((4, L, 128), F32)),
        compiler_params=pltpu.CompilerParams(dimension_semantics=("parallel", "arbitrary")),
    )(Q, KV, KV, KR, cs, sn)


def mla_bwd(Q, KV, KR, cs, sn, mixed, dmixed, lse, *, name):
    L = Q.shape[0]
    tq = _tq(L)

    def body(q_ref, kn_ref, v_ref, kr_ref, cs_ref, sn_ref, o_ref, do_ref, lse_ref, dq_ref, dkn_ref, dv_ref, dkr_ref):
        hp = pl.program_id(0)
        I = pl.program_id(1)

        @pl.when(I == 0)
        def _():
            dkn_ref[...] = jnp.zeros_like(dkn_ref)
            dv_ref[...] = jnp.zeros_like(dv_ref)

        @pl.when((I == 0) & (hp == 0))
        def _():
            dkr_ref[...] = jnp.zeros_like(dkr_ref)

        lane_q = lax.broadcasted_iota(jnp.int32, (tq, 128), 1)
        first_q = lane_q < 64
        qcat = _mla_qcat(q_ref, cs_ref, sn_ref, lane_q)
        qq = jnp.concatenate(qcat, axis=0)
        do = do_ref[...]
        prod = do * o_ref[...]
        dd = [jnp.sum(jnp.where(first_q, prod, 0.0), axis=1, keepdims=True),
              jnp.sum(jnp.where(first_q, 0.0, prod), axis=1, keepdims=True)]
        dom = [x.astype(BF16) for x in _head_split(do, first_q)]
        docat = jnp.concatenate(dom, axis=0)
        lses = [lse_ref[0, :, 0:1], lse_ref[0, :, 1:2]]

        def tile(j, dq, masked, wide=1):
            off = pl.multiple_of(j * tq, tq)
            tk = wide * tq
            lane_k = lax.broadcasted_iota(jnp.int32, (tk, 256), 1)
            sel0 = (lane_k < 64) | ((lane_k >= 128) & (lane_k < 128 + MLA_ROPE))
            sel1 = ((lane_k >= 64) & (lane_k < 128)) | ((lane_k >= 128 + MLA_ROPE) & (lane_k < 128 + 2 * MLA_ROPE))
            kcat = jnp.concatenate([kn_ref[pl.ds(off, tk), :], kr_ref[pl.ds(off, tk), :]], axis=1)
            vb = v_ref[pl.ds(off, tk), :]
            zero = jnp.zeros_like(kcat)
            kk = jnp.concatenate([jnp.where(sel0, kcat, zero), jnp.where(sel1, kcat, zero)], axis=0)
            mask = _mla_mask2(I, j, tq) if masked else None
            dss, pbs = [], []
            for h in range(2):
                s = _dot_nt(qcat[h], kcat) * MLA_SCALE
                p = jnp.exp(s - lses[h])
                if masked:
                    p = jnp.where(mask, p, 0.0)
                dp = _dot_nt(dom[h], vb)
                dss.append((p * (dp - dd[h]) * MLA_SCALE).astype(BF16))
                pbs.append(p.astype(BF16))
            dkc = _dot_tn(jnp.concatenate(dss, axis=0), qq)
            dkn_ref[pl.ds(off, tk), :] += dkc[:, 0:128]
            dkr_ref[pl.ds(off, tk), :] += dkc[:, 128:256]
            dv_ref[pl.ds(off, tk), :] += _dot_tn(jnp.concatenate(pbs, axis=0), docat)
            return dq + _dot(jnp.concatenate(dss, axis=1), kk)

        dq = jnp.zeros((tq, 256), F32)
        dq = lax.fori_loop(0, jnp.minimum(I, 1), lambda t, c: tile(0, c, True), dq)
        n_in = jnp.maximum(I - 1, 0)
        dq = lax.fori_loop(0, n_in // 2, lambda t, c: tile(1 + 2 * t, c, False, 2), dq)
        dq = lax.fori_loop(0, n_in % 2, lambda t, c: tile(I - 1, c, False), dq)
        dq = tile(I, dq, True)
        dq_ref[:, 0:128] = dq[:, 0:128]
        dq_ref[:, 128:256] = _rope_t(dq[:, 128:256], cs_ref[...], sn_ref[...], MLA_ROPE // 2)

    blk = lambda c0: pl.BlockSpec((tq, 128), lambda hp, i: (i, c0 + hp))
    full = lambda c0: pl.BlockSpec((L, 128), lambda hp, i: (0, c0 + hp))
    tab = pl.BlockSpec((tq, 128), lambda hp, i: (i, 0))
    return pl.pallas_call(
        body, name=name, grid=(MLA_HEADS // 2, L // tq),
        in_specs=[pl.BlockSpec((tq, 256), lambda hp, i: (i, hp)), full(0), full(4),
                  pl.BlockSpec((L, 128), lambda hp, i: (0, 0)), tab, tab, blk(4), blk(4),
                  pl.BlockSpec((1, tq, 128), lambda hp, i: (hp, i, 0))],
        out_specs=(pl.BlockSpec((tq, 256), lambda hp, i: (i, hp)), full(0), full(0),
                   pl.BlockSpec((L, 128), lambda hp, i: (0, 0))),
        out_shape=(jax.ShapeDtypeStruct((L, 1024), F32), jax.ShapeDtypeStruct((L, 512), F32),
                   jax.ShapeDtypeStruct((L, 512), F32), jax.ShapeDtypeStruct((L, 128), F32)),
        compiler_params=pltpu.CompilerParams(dimension_semantics=("arbitrary", "arbitrary")),
    )(Q, KV, KV, KR, cs, sn, mixed, dmixed, lse)


def _ret_decay(h):
    lg = RET_LOG_G[h]
    r = lax.broadcasted_iota(jnp.int32, (BLK, BLK), 0)
    c = lax.broadcasted_iota(jnp.int32, (BLK, BLK), 1)
    diff = (r - c).astype(F32)
    d_in = jnp.where(diff >= 0, jnp.exp(jnp.maximum(diff, 0.0) * lg), 0.0)
    idx = lax.broadcasted_iota(jnp.int32, (BLK, 1), 0).astype(F32)
    q_decay = jnp.exp((idx + 1.0) * lg)
    k_decay = jnp.exp((BLK - 1.0 - idx) * lg)
    c_decay = math.exp(BLK * lg)
    return d_in, q_decay, k_decay, c_decay


def _ret_qk(qk_ref, cs_ref, sn_ref, n):
    cs = jnp.concatenate([cs_ref[...]] * 2, axis=1)
    sn = jnp.concatenate([sn_ref[...]] * 2, axis=1)
    rq = _rope(qk_ref[:, 0:256], cs, sn, RET_QK // 2)
    row = n * BLK + lax.broadcasted_iota(jnp.int32, (BLK, 256), 0)
    kmul = jnp.where(row >= N_PAD, 0.125, 0.0)
    rk = _rope(qk_ref[:, 256:512], cs, sn, RET_QK // 2) * kmul
    return rq, rk, cs, sn, kmul


def _head_norm(y):
    mu = jnp.mean(y, axis=-1, keepdims=True)
    yc = y - mu
    r = lax.rsqrt(jnp.mean(jnp.square(yc), axis=-1, keepdims=True) + LN_EPS)
    return yc * r, r


def ret_fwd(P, cs, sn, *, name):
    L = P.shape[0]
    nb = L // BLK

    def body(qk_ref, v_ref, g_ref, cs_ref, sn_ref, o_ref, y_ref, st_ref, state):
        n = pl.program_id(0)

        @pl.when(n == 0)
        def _():
            state[...] = jnp.zeros_like(state)

        st_ref[0] = state[...]
        rq, rk, _, _, _ = _ret_qk(qk_ref, cs_ref, sn_ref, n)
        outs, ys = [], []
        for h in range(RET_HEADS):
            d_in, q_decay, k_decay, c_decay = _ret_decay(h)
            q = rq[:, 64 * h:64 * h + 64].astype(BF16)
            kf = rk[:, 64 * h:64 * h + 64]
            v = v_ref[:, 128 * h:128 * h + 128].astype(BF16)
            S = state[h]
            inner = _dot_nt(q, kf.astype(BF16)) * d_in
            y = _dot(inner.astype(BF16), v) + _dot(q, S.astype(BF16)) * q_decay
            state[h] = S * c_decay + _dot_tn((kf * k_decay).astype(BF16), v)
            g = g_ref[:, 128 * h:128 * h + 128]
            ys.append(y)
            outs.append(g * jax.nn.sigmoid(g) * _head_norm(y)[0])
        o_ref[...] = jnp.concatenate(outs, axis=1)
        y_ref[...] = jnp.concatenate(ys, axis=1)

    blk512 = lambda c: pl.BlockSpec((BLK, 512), lambda n: (n, c))
    tab = pl.BlockSpec((BLK, 128), lambda n: (n, 0))
    return pl.pallas_call(
        body, name=name, grid=(nb,),
        in_specs=[blk512(C_RQ // 512), blk512(C_RV // 512), blk512(C_RG // 512), tab, tab],
        out_specs=(blk512(0), blk512(0), pl.BlockSpec((1, RET_HEADS, RET_QK, RET_V), lambda n: (n, 0, 0, 0))),
        out_shape=(jax.ShapeDtypeStruct((L, 512), F32), jax.ShapeDtypeStruct((L, 512), F32),
                   jax.ShapeDtypeStruct((nb, RET_HEADS, RET_QK, RET_V), F32)),
        scratch_shapes=[pltpu.VMEM((RET_HEADS, RET_QK, RET_V), F32)],
        compiler_params=pltpu.CompilerParams(dimension_semantics=("arbitrary",)),
    )(P, P, P, cs, sn)


def ret_bwd(P, y, states, dmixed, cs, sn, *, name):
    L = P.shape[0]
    nb = L // BLK

    def body(qk_ref, v_ref, g_ref, y_ref, st_ref, do_ref, cs_ref, sn_ref, dqk_ref, dv_ref, dg_ref, dstate):
        n = nb - 1 - pl.program_id(0)

        @pl.when(pl.program_id(0) == 0)
        def _():
            dstate[...] = jnp.zeros_like(dstate)

        rq, rk, cs, sn, kmul = _ret_qk(qk_ref, cs_ref, sn_ref, n)
        dqs, dks, dvs, dgs = [], [], [], []
        for h in range(RET_HEADS):
            d_in, q_decay, k_decay, c_decay = _ret_decay(h)
            sv = slice(128 * h, 128 * h + 128)
            q = rq[:, 64 * h:64 * h + 64].astype(BF16)
            kf = rk[:, 64 * h:64 * h + 64]
            k = kf.astype(BF16)
            kd = (kf * k_decay).astype(BF16)
            v = v_ref[:, sv].astype(BF16)
            g = g_ref[:, sv]
            do = do_ref[:, sv]
            yh = y_ref[:, sv]
            S = st_ref[0, h].astype(BF16)
            dS = dstate[h]
            sg = jax.nn.sigmoid(g)
            yn, r = _head_norm(yh)
            dgs.append(do * yn * (sg * (1.0 + g * (1.0 - sg))))
            dyn = do * (g * sg)
            dy = r * (dyn - jnp.mean(dyn, axis=-1, keepdims=True) - yn * jnp.mean(dyn * yn, axis=-1, keepdims=True))
            dyb = dy.astype(BF16)
            dyq = (dy * q_decay).astype(BF16)
            inner = (_dot_nt(q, k) * d_in).astype(BF16)
            A = (_dot_nt(dyb, v) * d_in).astype(BF16)
            dSb = dS.astype(BF16)
            dqs.append(_dot(A, k) + _dot_nt(dyq, S))
            dks.append(_dot_tn(A, q) + _dot_nt(v, dSb) * k_decay)
            dvs.append(_dot_tn(inner, dyb) + _dot(kd, dSb))
            dstate[h] = dS * c_decay + _dot_tn(q, dyq)
        drq = _rope_t(jnp.concatenate(dqs, axis=1), cs, sn, RET_QK // 2)
        drk = _rope_t(jnp.concatenate(dks, axis=1) * kmul, cs, sn, RET_QK // 2)
        dqk_ref[...] = jnp.concatenate([drq, drk], axis=1)
        dv_ref[...] = jnp.concatenate(dvs, axis=1)
        dg_ref[...] = jnp.concatenate(dgs, axis=1)

    blk512 = lambda c: pl.BlockSpec((BLK, 512), lambda t: (nb - 1 - t, c))
    tab = pl.BlockSpec((BLK, 128), lambda t: (nb - 1 - t, 0))
    sds = jax.ShapeDtypeStruct((L, 512), F32)
    return pl.pallas_call(
        body, name=name, grid=(nb,),
        in_specs=[blk512(C_RQ // 512), blk512(C_RV // 512), blk512(C_RG // 512), blk512(0),
                  pl.BlockSpec((1, RET_HEADS, RET_QK, RET_V), lambda t: (nb - 1 - t, 0, 0, 0)), blk512(2), tab, tab],
        out_specs=(blk512(0), blk512(0), blk512(0)), out_shape=(sds, sds, sds),
        scratch_shapes=[pltpu.VMEM((RET_HEADS, RET_QK, RET_V), F32)],
        compiler_params=pltpu.CompilerParams(dimension_semantics=("arbitrary",)),
    )(P, P, P, y, states, dmixed, cs, sn)


def _perm_w_in(w):
    pad = jnp.zeros(w.shape[:-1] + (N_INP - N_IN,), w.dtype)
    return jnp.concatenate([w[..., 0:1536], w[..., 2208:3744], w[..., 1536:2208], pad], axis=-1)


def _unperm_w_in(g):
    return jnp.concatenate([g[..., 0:1536], g[..., 3072:3744], g[..., 1536:3072]], axis=-1)


def _perm_w_uq(w):
    lead = w.shape[:-1]
    w5 = w.reshape(lead + (4, 2, 96))
    nope = w5[..., :64].reshape(lead + (4, 128))
    rope = w5[..., 64:].reshape(lead + (4, 64))
    return jnp.concatenate([nope, rope, jnp.zeros(lead + (4, 64), w.dtype)], axis=-1).reshape(lead + (1024,))


def _unperm_w_uq(g):
    lead = g.shape[:-1]
    g4 = g.reshape(lead + (4, 256))
    nope = g4[..., :128].reshape(lead + (4, 2, 64))
    rope = g4[..., 128:192].reshape(lead + (4, 2, 32))
    return jnp.concatenate([nope, rope], axis=-1).reshape(lead + (768,))


def _perm_w_ukv(w):
    lead = w.shape[:-1]
    w4 = w.reshape(lead + (8, 128))
    return jnp.concatenate([w4[..., :64].reshape(lead + (512,)), w4[..., 64:].reshape(lead + (512,))], axis=-1)


def _unperm_w_ukv(g):
    lead = g.shape[:-1]
    return jnp.concatenate([g[..., :512].reshape(lead + (8, 64)), g[..., 512:].reshape(lead + (8, 64))],
                           axis=-1).reshape(lead + (1024,))


def _rope_tables(L, half):
    pos = (jnp.arange(L) - N_PAD).astype(F32)
    inv = ROPE_THETA ** (-jnp.arange(half, dtype=F32) / half)
    ang = pos[:, None] * inv[None, :]
    cos, sin = jnp.cos(ang), jnp.sin(ang)
    reps = 128 // (2 * half)
    cs = jnp.tile(jnp.concatenate([cos, cos], axis=1), (1, reps))
    sn = jnp.tile(jnp.concatenate([-sin, sin], axis=1), (1, reps))
    return cs, sn


def _device_step(x, target, meta, ln_emb_g, ln_emb_b, w_in, q_norm, kv_norm, w_uq, w_ukv, w_out,
                 ln1_g, ln1_b, w_ff1, w_ff2, ln2_g, ln2_b):
    S = x.shape[0]
    L = S + BLK
    depth = w_in.shape[0]
    cs_m, sn_m = _rope_tables(L, MLA_ROPE // 2)
    cs_r, sn_r = _rope_tables(L, RET_QK // 2)
    hcat = jnp.concatenate([jnp.zeros((N_PAD, D_MODEL), F32), meta, x], axis=0)
    h, _ = ln_fwd(hcat, ln_emb_g, ln_emb_b, name="ln_emb_fwd")

    w_in_sb, w_in_rest = w_in[..., :N_SB], w_in[..., N_SB:]
    saved = []
    for l in range(depth):
        Psb = mm_nn(h, w_in_sb[l], tn=768, name=f"in_proj_sb_{l}", out_dtype=BF16)
        P = mm_nn(h, w_in_rest[l], tn=768, name=f"in_proj_{l}")
        out_a, sbc = sb_fwd(Psb, name=f"sb_fwd_{l}")
        nq, nkv, KR = mla_pre_fwd(P, q_norm[l], kv_norm[l], cs_m, sn_m, name=f"mla_pre_fwd_{l}")
        Q = mm_nn(nq, w_uq[l], tn=512, name=f"uq_{l}")
        KV = mm_nn(nkv, w_ukv[l], tn=512, name=f"ukv_{l}", out_dtype=BF16)
        out_b, lse = mla_fwd(Q, KV, KR, cs_m, sn_m, name=f"mla_fwd_{l}")
        out_c, y, states = ret_fwd(P, cs_r, sn_r, name=f"ret_fwd_{l}")
        mixed = jnp.concatenate([out_a, out_b, out_c], axis=1)
        w_out_l = w_out[l].reshape(1, 1536, D_MODEL)
        mix = mm_nn(mixed, w_out_l, tn=512, tk=512, name=f"out_proj_{l}")
        h1, z1 = ln_fwd(mix, ln1_g[l], ln1_b[l], res=h, name=f"ln1_fwd_{l}")
        U = mm_nn(h1, w_ff1[l], tn=1024, name=f"ff1_{l}")
        w_ff2_l = w_ff2[l].reshape(1, D_FF, D_MODEL)
        mlp = mm_nn(U, w_ff2_l, tn=512, tk=1024, prologue="relu2", name=f"ff2_{l}")
        h2, z2 = ln_fwd(mlp, ln2_g[l], ln2_b[l], res=h1, name=f"ln2_fwd_{l}")
        saved.append((h, Psb, P, sbc, nq, nkv, KR, Q, KV, lse, y, states, mixed, z1, h1, U, z2))
        h = h2

    loss_t, dh = loss_fwd_bwd(h, target, name="loss")

    grads = {k: [None] * depth for k in ("w_in", "q_norm", "kv_norm", "w_uq", "w_ukv", "ln1_g", "ln1_b", "ln2_g", "ln2_b")}
    g_ff1 = lax.empty((4, depth, D_MODEL, D_FF // 4), F32)
    g_ff2 = lax.empty((4, depth, D_FF // 4, D_MODEL), F32)
    g_out = lax.empty((4, depth, 384, D_MODEL), F32)
    for l in reversed(range(depth)):
        h_in, Psb, P, sbc, nq, nkv, KR, Q, KV, lse, y, states, mixed, z1, h1, U, z2 = saved[l]
        dz2, grads["ln2_g"][l], grads["ln2_b"][l] = ln_bwd(dh, z2, ln2_g[l], name=f"ln2_bwd_{l}")
        w_ff2_l = w_ff2[l].reshape(1, D_FF, D_MODEL)
        g_ff2 = mm_tn(U, dz2, shards=1, tko=1024, tn=1024, prologue="relu2", name=f"ff2_dw_{l}", into=(g_ff2, l, "rows"))
        dU = mm_nt(dz2, w_ff2_l, tn=1024, tko=1024, relu2grad=U, name=f"ff2_dx_{l}", out_dtype=BF16)
        g_ff1 = mm_tn(h1, dU, shards=4, tko=1024, tn=1024, name=f"ff1_dw_{l}", into=(g_ff1, l, "cols"))
        dh1 = mm_nt(dU, w_ff1[l], tn=1024, tko=1024, axpy=(dz2, DN_ALPHA), name=f"ff1_dx_{l}")
        dz1, grads["ln1_g"][l], grads["ln1_b"][l] = ln_bwd(dh1, z1, ln1_g[l], name=f"ln1_bwd_{l}")
        w_out_l = w_out[l].reshape(1, 1536, D_MODEL)
        g_out = mm_tn(mixed, dz1, shards=1, tko=384, tn=1024, name=f"out_dw_{l}", into=(g_out, l, "rows"))
        dmixed = mm_nt(dz1, w_out_l, tn=1024, tko=512, name=f"out_dx_{l}")
        d_rqk, d_rv, d_rg = ret_bwd(P, y, states, dmixed, cs_r, sn_r, name=f"ret_bwd_{l}")
        dQ, dKN, dV, dKR = mla_bwd(Q, KV, KR, cs_m, sn_m, mixed, dmixed, lse, name=f"mla_bwd_{l}")
        dKV = jnp.concatenate([dKN, dV], axis=1)
        grads["w_uq"][l] = mm_tn(nq, dQ, shards=1, tko=MLA_Q_LORA, tn=512, name=f"uq_dw_{l}")[0]
        grads["w_ukv"][l] = mm_tn(nkv, dKV, shards=1, tko=MLA_KV_LORA, tn=512, name=f"ukv_dw_{l}")[0]
        dnq = mm_nt(dQ, w_uq[l], tn=1024, tko=MLA_Q_LORA, name=f"uq_dx_{l}")
        dnkv = mm_nt(dKV, w_ukv[l], tn=1024, tko=MLA_KV_LORA, name=f"ukv_dx_{l}")
        d_lat, grads["q_norm"][l], grads["kv_norm"][l] = mla_pre_bwd(P, dnq, dnkv, dKR, q_norm[l], kv_norm[l], cs_m, sn_m,
                                                                     name=f"mla_pre_bwd_{l}")
        dq_sb, dk_sb, dv_sb = sb_bwd(Psb, sbc, dmixed, name=f"sb_bwd_{l}")
        dP = jnp.concatenate([dq_sb, dk_sb, dv_sb, d_rqk, d_rv, d_rg, d_lat], axis=1).astype(BF16)
        grads["w_in"][l] = mm_tn(h_in, dP, shards=1, tko=1024, tn=1280, name=f"in_dw_{l}")[0]
        dh = mm_nt(dP, w_in[l], tn=768, tko=1024, axpy=(dz1, DN_ALPHA), name=f"in_dx_{l}")

    dhcat, dg_emb, db_emb = ln_bwd(dh, hcat, ln_emb_g, name="ln_emb_bwd")
    out = {k: jnp.stack(v) for k, v in grads.items()}
    out["w_ff1"], out["w_ff2"], out["w_out"] = g_ff1, g_ff2, g_out
    out["ln_emb_g"], out["ln_emb_b"] = dg_emb, db_emb
    out["meta"] = dhcat[N_PAD:BLK]
    return loss_t[0, 0], dhcat[BLK:], out


MESH = pl.DeviceIdType.MESH
PEER_XOR = (2, 1, 3)
_HBM = pl.BlockSpec(memory_space=pltpu.HBM)


def _place():
    x, y, c = lax.axis_index("x"), lax.axis_index("y"), lax.axis_index("c")
    peers = [(1 - x, y, c), (x, 1 - y, c), (1 - x, 1 - y, c)]
    return x, y, c, 2 * x + y, peers, (x, y, 1 - c)


def gather_weight(w_shard, *, name):
    nl = w_shard.shape[0]
    hl = nl // 2

    def body(w_ref, out_ref, send_sems, recv_sems):
        x, y, c, s0, peers, sibling = _place()

        def piece(s, half):
            return out_ref.at[s, pl.ds(half * hl, hl)]

        def copy(k, s, half, to, src=None):
            return pltpu.make_async_remote_copy(src_ref=piece(s, half) if src is None else src, dst_ref=piece(s, half),
                                                send_sem=send_sems.at[k], recv_sem=recv_sems.at[k],
                                                device_id=to, device_id_type=MESH)

        first = [copy(k, s0, c, peers[k], src=w_ref.at[pl.ds(c * hl, hl)]) for k in range(3)]
        for cp in first:
            cp.start()
        passed = [copy(3 + k, s0 ^ PEER_XOR[k], c, sibling) for k in range(3)]
        for k in range(3):
            copy(k, s0 ^ PEER_XOR[k], c, peers[k]).wait_recv()
            passed[k].start()
        for k in range(3):
            copy(3 + k, s0 ^ PEER_XOR[k], 1 - c, sibling).wait_recv()
        for cp in first + passed:
            cp.wait_send()

    return pl.pallas_call(
        body, name=name, in_specs=[_HBM], out_specs=_HBM,
        out_shape=jax.ShapeDtypeStruct((4,) + w_shard.shape, w_shard.dtype),
        scratch_shapes=[pltpu.SemaphoreType.DMA((6,)), pltpu.SemaphoreType.DMA((6,))],
    )(w_shard)


def send_half_to_sibling(G, *, name):
    hl = G.shape[1] // 2

    def body(g_ref, out_ref, send_sem, recv_sem):
        x, y, c, s0, peers, sibling = _place()
        cp = pltpu.make_async_remote_copy(src_ref=g_ref.at[:, pl.ds((1 - c) * hl, hl)], dst_ref=out_ref,
                                          send_sem=send_sem, recv_sem=recv_sem, device_id=sibling, device_id_type=MESH)
        cp.start()
        cp.wait()

    return pl.pallas_call(
        body, name=name, in_specs=[_HBM], out_specs=_HBM,
        out_shape=jax.ShapeDtypeStruct((4, hl) + G.shape[2:], G.dtype),
        scratch_shapes=[pltpu.SemaphoreType.DMA, pltpu.SemaphoreType.DMA],
    )(G)


def scatter_to_chips(A, *, name):
    def body(a_ref, out_ref, send_sems, recv_sems):
        x, y, c, s0, peers, sibling = _place()
        copies = [pltpu.make_async_remote_copy(src_ref=a_ref.at[s0 ^ PEER_XOR[k]], dst_ref=out_ref.at[k],
                                               send_sem=send_sems.at[k], recv_sem=recv_sems.at[k],
                                               device_id=peers[k], device_id_type=MESH) for k in range(3)]
        for cp in copies:
            cp.start()
        for cp in copies:
            cp.wait()

    return pl.pallas_call(
        body, name=name, in_specs=[_HBM], out_specs=_HBM,
        out_shape=jax.ShapeDtypeStruct((3,) + A.shape[1:], A.dtype),
        scratch_shapes=[pltpu.SemaphoreType.DMA((3,)), pltpu.SemaphoreType.DMA((3,))],
    )(A)


def join_halves(buf, *, name):
    hl = buf.shape[0] // 2

    def body(b_ref, out_ref, send_sem, recv_sem):
        x, y, c, s0, peers, sibling = _place()
        cp = pltpu.make_async_remote_copy(src_ref=b_ref.at[pl.ds(c * hl, hl)], dst_ref=out_ref.at[pl.ds(c * hl, hl)],
                                          send_sem=send_sem, recv_sem=recv_sem, device_id=sibling, device_id_type=MESH)
        cp.start()
        pltpu.make_async_remote_copy(src_ref=b_ref.at[pl.ds((1 - c) * hl, hl)], dst_ref=out_ref.at[pl.ds((1 - c) * hl, hl)],
                                     send_sem=send_sem, recv_sem=recv_sem, device_id=sibling, device_id_type=MESH).wait_recv()
        cp.wait_send()

    return pl.pallas_call(
        body, name=name, in_specs=[_HBM], out_specs=_HBM, input_output_aliases={0: 0},
        out_shape=jax.ShapeDtypeStruct(buf.shape, buf.dtype),
        scratch_shapes=[pltpu.SemaphoreType.DMA, pltpu.SemaphoreType.DMA],
    )(buf)


def allgather8(xs, *, name, reduce):
    M, N = xs.shape

    def body(x_ref, out_ref, *rest):
        if reduce:
            all_ref, send_sems, recv_sems, local_sem = rest
        else:
            all_ref = out_ref
            send_sems, recv_sems, local_sem = rest
        x, y, c, s0, peers, sibling = _place()
        me = (x, y, c)
        chips = [(1 - x, y), (x, 1 - y), (1 - x, 1 - y)]

        def rows(px, py, pc):
            return all_ref.at[pl.ds((4 * px + 2 * py + pc) * M, M), :]

        def copy(k, block, to, src=None):
            return pltpu.make_async_remote_copy(src_ref=rows(*block) if src is None else src, dst_ref=rows(*block),
                                                send_sem=send_sems.at[k], recv_sem=recv_sems.at[k],
                                                device_id=to, device_id_type=MESH)

        mine = pltpu.make_async_copy(x_ref, rows(*me), local_sem)
        mine.start()
        first = [copy(0, me, sibling, src=x_ref)]
        first += [copy(1 + j, me, (*chip, c), src=x_ref) for j, chip in enumerate(chips)]
        for cp in first:
            cp.start()
        passed = [copy(4 + j, (*chip, c), sibling) for j, chip in enumerate(chips)]
        for j, chip in enumerate(chips):
            copy(1 + j, (*chip, c), me).wait_recv()
            passed[j].start()
        copy(0, sibling, me).wait_recv()
        for j, chip in enumerate(chips):
            copy(4 + j, (*chip, 1 - c), me).wait_recv()
        for cp in first + passed:
            cp.wait_send()
        mine.wait()
        if reduce:
            acc = all_ref[pl.ds(0, M), :]
            for d in range(1, 8):
                acc = acc + all_ref[pl.ds(d * M, M), :]
            out_ref[...] = acc

    vm = pl.BlockSpec(memory_space=pltpu.VMEM)
    scratch = [pltpu.SemaphoreType.DMA((7,)), pltpu.SemaphoreType.DMA((7,)), pltpu.SemaphoreType.DMA]
    if reduce:
        scratch = [pltpu.VMEM((8 * M, N), xs.dtype)] + scratch
    return pl.pallas_call(
        body, name=name, in_specs=[vm], out_specs=vm,
        out_shape=jax.ShapeDtypeStruct((M if reduce else 8 * M, N), xs.dtype), scratch_shapes=scratch,
    )(xs)


def add_halves(G, B, c, *, name):
    S, nl, R, C = G.shape
    hl = nl // 2
    tr = _pick(R, (512, 384, 256, 128))

    def body(c_ref, g_ref, b_ref, o_ref):
        o_ref[...] = (g_ref[...] + b_ref[...]).astype(BF16)

    blk = (1, 1, tr, C)
    return pl.pallas_call(
        body, name=name,
        grid_spec=pltpu.PrefetchScalarGridSpec(
            num_scalar_prefetch=1, grid=(S, hl, R // tr),
            in_specs=[pl.BlockSpec(blk, lambda s, l, r, cr: (s, cr[0] * hl + l, r, 0)),
                      pl.BlockSpec(blk, lambda s, l, r, cr: (s, l, r, 0))],
            out_specs=pl.BlockSpec(blk, lambda s, l, r, cr: (s, l, r, 0))),
        out_shape=jax.ShapeDtypeStruct((S, hl, R, C), BF16),
    )(jnp.reshape(c, (1,)).astype(jnp.int32), G, B)


def add_chips(G, B, Bc, c, s0, *, name):
    S, nl, R, C = G.shape
    hl = nl // 2
    tr = _pick(R, (512, 384, 256, 128))

    def body(pc_ref, ps_ref, g_ref, b_ref, c0_ref, c1_ref, c2_ref, o_ref):
        o_ref[...] = ((((g_ref[0] + b_ref[0]) + c0_ref[0].astype(F32)) + c1_ref[0].astype(F32)) + c2_ref[0].astype(F32))

    blk = (1, 1, tr, C)
    cspec = lambda k: pl.BlockSpec(blk, lambda l, r, pc, ps: (k, l, r, 0))
    return pl.pallas_call(
        body, name=name,
        grid_spec=pltpu.PrefetchScalarGridSpec(
            num_scalar_prefetch=2, grid=(hl, R // tr),
            in_specs=[pl.BlockSpec(blk, lambda l, r, pc, ps: (ps[0], pc[0] * hl + l, r, 0)),
                      pl.BlockSpec(blk, lambda l, r, pc, ps: (ps[0], l, r, 0)), cspec(0), cspec(1), cspec(2)],
            out_specs=pl.BlockSpec((1, tr, C), lambda l, r, pc, ps: (pc[0] * hl + l, r, 0))),
        out_shape=jax.ShapeDtypeStruct((nl, R, C), F32),
    )(jnp.reshape(c, (1,)).astype(jnp.int32), jnp.reshape(s0, (1,)).astype(jnp.int32), G, B, Bc, Bc, Bc)


def reduce_scatter_weight(G, c, s0, *, tag):
    B = send_half_to_sibling(G, name=f"rs_sib_{tag}")
    A = add_halves(G, B, c, name=f"rs_add1_{tag}")
    Bc = scatter_to_chips(A, name=f"rs_chips_{tag}")
    half = add_chips(G, B, Bc, c, s0, name=f"rs_add2_{tag}")
    return join_halves(half, name=f"rs_join_{tag}")


def adamw(w, g, m, v, *, name):
    shp = w.shape
    if len(shp) == 2:
        w, g, m, v = (a[None] for a in (w, g, m, v))
    nl, R, C = w.shape
    tr = R
    for t in (512, 384, 256, 128):
        if R % t == 0:
            tr = t
            break

    def body(w_ref, g_ref, m_ref, v_ref, d_ref, nm_ref, nv_ref):
        gv = g_ref[...]
        mn = ADAM_B1 * m_ref[...] + (1.0 - ADAM_B1) * gv
        vn = ADAM_B2 * v_ref[...] + (1.0 - ADAM_B2) * jnp.square(gv)
        m_hat = mn / (1.0 - ADAM_B1 ** ADAM_STEP)
        v_hat = vn / (1.0 - ADAM_B2 ** ADAM_STEP)
        d_ref[...] = -ADAM_LR * (m_hat / (jnp.sqrt(v_hat) + ADAM_EPS) + ADAM_WD * w_ref[...])
        nm_ref[...] = mn
        nv_ref[...] = vn

    spec = pl.BlockSpec((1, tr, C), lambda l, i: (l, i, 0))
    sds = jax.ShapeDtypeStruct((nl, R, C), F32)
    d, nm, nv = pl.pallas_call(body, name=name, grid=(nl, R // tr), in_specs=[spec] * 4, out_specs=(spec,) * 3,
                               out_shape=(sds,) * 3)(w, g, m, v)
    return d.reshape(shp), nm.reshape(shp), nv.reshape(shp)


_SMALL = ("ln_emb_g", "ln_emb_b", "q_norm", "kv_norm", "ln1_g", "ln1_b", "ln2_g", "ln2_b", "meta")


def _pack_small(d):
    flat = jnp.concatenate([d[k].reshape(-1) for k in _SMALL])
    rows = -(-flat.shape[0] // 128)
    rows = -(-rows // 8) * 8
    flat = jnp.concatenate([flat, jnp.zeros((rows * 128 - flat.shape[0],), F32)])
    return flat.reshape(rows, 128)


def _unpack_small(p, shapes):
    flat = p.reshape(-1)
    out, o = {}, 0
    for k in _SMALL:
        n = int(np.prod(shapes[k]))
        out[k] = flat[o:o + n].reshape(shapes[k])
        o += n
    return out


def kernel(x, meta_tokens, ln_emb_g, ln_emb_b, w_in, mla_q_norm, mla_kv_norm, w_uq, w_ukv, w_out, ln1_g, ln1_b, w_ff1, w_ff2, ln2_g, ln2_b, loss_target, m_meta_tokens, m_ln_emb_g, m_ln_emb_b, m_w_in, m_mla_q_norm, m_mla_kv_norm, m_w_uq, m_w_ukv, m_w_out, m_ln1_g, m_ln1_b, m_w_ff1, m_w_ff2, m_ln2_g, m_ln2_b, v_meta_tokens, v_ln_emb_g, v_ln_emb_b, v_w_in, v_mla_q_norm, v_mla_kv_norm, v_w_uq, v_w_ukv, v_w_out, v_ln1_g, v_ln1_b, v_w_ff1, v_w_ff2, v_ln2_g, v_ln2_b):
    xi, yi, ci = lax.axis_index("x"), lax.axis_index("y"), lax.axis_index("c")
    s0 = 2 * xi + yi
    nl = w_in.shape[0]

    big = {"w_in": w_in, "w_uq": w_uq, "w_ukv": w_ukv, "w_out": w_out, "w_ff1": w_ff1, "w_ff2": w_ff2}
    full = {}
    for k, v in big.items():
        vb = v.astype(BF16)
        full[k] = lax.dynamic_update_slice(gather_weight(vb, name=f"ag_{k}"), vb[None], (s0, 0, 0, 0))
    cols = lambda a: jnp.moveaxis(a, 0, 2).reshape(a.shape[1], a.shape[2], 4 * a.shape[3])
    k_w_in = _perm_w_in(cols(full["w_in"]))[:, None]
    k_w_uq = _perm_w_uq(cols(full["w_uq"]))[:, None]
    k_w_ukv = _perm_w_ukv(cols(full["w_ukv"]))[:, None]
    k_w_out = jnp.moveaxis(full["w_out"], 0, 1)
    k_w_ff1 = jnp.moveaxis(full["w_ff1"], 0, 1)
    k_w_ff2 = jnp.moveaxis(full["w_ff2"], 0, 1)
    meta_all = allgather8(meta_tokens, name="ag_meta", reduce=False)
    meta_full = jnp.concatenate([meta_all[32 * s:32 * s + N_META] for s in range(4)], axis=1)

    loss_part, grad_x, g = _device_step(x[0], loss_target[0], meta_full, ln_emb_g, ln_emb_b, k_w_in, mla_q_norm, mla_kv_norm,
                                        k_w_uq, k_w_ukv, k_w_out, ln1_g, ln1_b, k_w_ff1, k_w_ff2, ln2_g, ln2_b)
    loss = lax.psum(loss_part, ("x", "y", "c"))

    def col_shards(a):
        return jnp.moveaxis(a.reshape(a.shape[0], a.shape[1], 4, a.shape[2] // 4), 2, 0)

    G = {"w_in": col_shards(_unperm_w_in(g["w_in"])), "w_uq": col_shards(_unperm_w_uq(g["w_uq"])),
         "w_ukv": col_shards(_unperm_w_ukv(g["w_ukv"])), "w_out": g["w_out"], "w_ff1": g["w_ff1"], "w_ff2": g["w_ff2"]}
    gw = {k: reduce_scatter_weight(v, ci, s0, tag=k) for k, v in G.items()}

    small_shapes = {"ln_emb_g": (D_MODEL,), "ln_emb_b": (D_MODEL,), "q_norm": (nl, MLA_Q_LORA), "kv_norm": (nl, MLA_KV_LORA),
                    "ln1_g": (nl, D_MODEL), "ln1_b": (nl, D_MODEL), "ln2_g": (nl, D_MODEL), "ln2_b": (nl, D_MODEL),
                    "meta": (N_META, D_MODEL)}
    gs = _unpack_small(allgather8(_pack_small(g), name="ar_small", reduce=True), small_shapes)
    gw.update({"ln_emb_g": gs["ln_emb_g"], "ln_emb_b": gs["ln_emb_b"], "mla_q_norm": gs["q_norm"], "mla_kv_norm": gs["kv_norm"],
               "ln1_g": gs["ln1_g"], "ln1_b": gs["ln1_b"], "ln2_g": gs["ln2_g"], "ln2_b": gs["ln2_b"],
               "meta_tokens": lax.dynamic_slice_in_dim(gs["meta"], s0 * 256, 256, axis=1)})

    names = ["meta_tokens", "ln_emb_g", "ln_emb_b", "w_in", "mla_q_norm", "mla_kv_norm", "w_uq", "w_ukv", "w_out",
             "ln1_g", "ln1_b", "w_ff1", "w_ff2", "ln2_g", "ln2_b"]
    ws = [meta_tokens, ln_emb_g, ln_emb_b, w_in, mla_q_norm, mla_kv_norm, w_uq, w_ukv, w_out, ln1_g, ln1_b, w_ff1, w_ff2, ln2_g, ln2_b]
    ms = [m_meta_tokens, m_ln_emb_g, m_ln_emb_b, m_w_in, m_mla_q_norm, m_mla_kv_norm, m_w_uq, m_w_ukv, m_w_out, m_ln1_g, m_ln1_b, m_w_ff1, m_w_ff2, m_ln2_g, m_ln2_b]
    vs = [v_meta_tokens, v_ln_emb_g, v_ln_emb_b, v_w_in, v_mla_q_norm, v_mla_kv_norm, v_w_uq, v_w_ukv, v_w_out, v_ln1_g, v_ln1_b, v_w_ff1, v_w_ff2, v_ln2_g, v_ln2_b]
    deltas, new_m, new_v = [], [], []
    for n, w, m, v in zip(names, ws, ms, vs):
        w2 = w.reshape(1, -1) if w.ndim == 1 else w
        d, nm, nv = adamw(w2, gw[n].reshape(w2.shape), m.reshape(w2.shape), v.reshape(w2.shape), name=f"adamw_{n}")
        deltas.append(d.reshape(w.shape))
        new_m.append(nm.reshape(w.shape))
        new_v.append(nv.reshape(w.shape))
    grads_out = [gw[n].reshape(w.shape) for n, w in zip(names, ws)]
    return (loss, grad_x[None], *grads_out, *deltas, *new_m, *new_v)
```

```python
import functools
import math

import numpy as np
import jax
import jax.numpy as jnp
from jax import lax
from jax.experimental import pallas as pl
from jax.experimental.pallas import tpu as pltpu

F32 = jnp.float32
BF16 = jnp.bfloat16

D_MODEL = 1024
DEPTH = 4
N_META = 16
BLK = 128
N_PAD = 112
SB_HEADS = 8
MLA_HEADS = 8
MLA_NOPE = 64
MLA_ROPE = 32
MLA_V = 64
MLA_Q_LORA = 384
MLA_KV_LORA = 256
RET_HEADS = 4
RET_QK = 64
RET_V = 128
D_FF = 4 * D_MODEL
ROPE_THETA = 10000.0
LN_EPS = 1e-5
DN_ALPHA = (2 * DEPTH) ** 0.25
RET_GAMMA = tuple(1.0 - 2.0 ** (-5 - h) for h in range(RET_HEADS))
RET_LOG_G = tuple(float(np.log(np.float32(g))) for g in RET_GAMMA)
MLA_SCALE = (MLA_NOPE + MLA_ROPE) ** -0.5

ADAM_LR = 0.001
ADAM_B1 = 0.9
ADAM_B2 = 0.999
ADAM_EPS = 1e-08
ADAM_WD = 0.01
ADAM_STEP = 10

N_SB = 1536
C_SBQ, C_SBK, C_SBV = 0, 512, 1024
C_RQ, C_RK, C_RV, C_RG = 0, 256, 512, 1024
C_CQ, C_CKV, C_KR = 1536, 1920, 2176
N_IN = 3744
N_INP = 3840

NEG = -1e30


def _pick(n, cands):
    for t in cands:
        if n % t == 0:
            return t
    raise ValueError(f"no tile for {n} in {cands}")


def _row_tile(n):
    return _pick(n, (1056, 1024, 528, 512, 384, 256, 128))


def _dot(a, b):
    return jnp.dot(a, b, preferred_element_type=F32)


def _dot_nt(a, b):
    return lax.dot_general(a, b, (((1,), (1,)), ((), ())), preferred_element_type=F32)


def _dot_tn(a, b):
    return lax.dot_general(a, b, (((0,), (0,)), ((), ())), preferred_element_type=F32)


def mm_nn(a, b, *, tn, name, tk=None, prologue=None, axpy=None, out_dtype=F32):
    M, K = a.shape
    S, _, Ns = b.shape
    tm = _row_tile(M)
    tk = K if tk is None else tk
    npt = Ns // tn
    nk = K // tk
    alpha = None if axpy is None else axpy[1]

    def body(*refs):
        if axpy is None:
            a_ref, b_ref, o_ref, acc = refs
        else:
            a_ref, b_ref, e_ref, o_ref, acc = refs
        k = pl.program_id(2)

        @pl.when(k == 0)
        def _():
            acc[...] = jnp.zeros_like(acc)

        x = a_ref[...]
        if prologue == "relu2":
            x = jnp.square(jnp.maximum(x, 0.0))
        acc[...] += _dot(x.astype(BF16), b_ref[0])

        @pl.when(k == nk - 1)
        def _():
            r = acc[...]
            if axpy is not None:
                r = r + alpha * e_ref[...]
            o_ref[...] = r.astype(out_dtype)

    in_specs = [pl.BlockSpec((tm, tk), lambda i, j, k: (i, k)),
                pl.BlockSpec((1, tk, tn), lambda i, j, k: (j // npt, k, j % npt))]
    args = [a, b]
    if axpy is not None:
        in_specs.append(pl.BlockSpec((tm, tn), lambda i, j, k: (i, j)))
        args.append(axpy[0])
    return pl.pallas_call(
        body, name=name, grid=(M // tm, (S * Ns) // tn, nk), in_specs=in_specs,
        out_specs=pl.BlockSpec((tm, tn), lambda i, j, k: (i, j)),
        out_shape=jax.ShapeDtypeStruct((M, S * Ns), out_dtype),
        scratch_shapes=[pltpu.VMEM((tm, tn), F32)],
        compiler_params=pltpu.CompilerParams(dimension_semantics=("parallel", "parallel", "arbitrary")),
    )(*args)


def mm_nt(a, b, *, tn, tko, name, axpy=None, relu2grad=None, out_dtype=F32):
    M, N = a.shape
    S, K, Ns = b.shape
    tm = _row_tile(M)
    npt = Ns // tn
    nn = N // tn
    alpha = None if axpy is None else axpy[1]

    def body(*refs):
        if axpy is None and relu2grad is None:
            a_ref, b_ref, o_ref, acc = refs
        else:
            a_ref, b_ref, e_ref, o_ref, acc = refs
        n = pl.program_id(2)

        @pl.when(n == 0)
        def _():
            acc[...] = jnp.zeros_like(acc)

        acc[...] += _dot_nt(a_ref[...].astype(BF16), b_ref[0])

        @pl.when(n == nn - 1)
        def _():
            r = acc[...]
            if axpy is not None:
                r = r + alpha * e_ref[...]
            if relu2grad is not None:
                r = r * (2.0 * jnp.maximum(e_ref[...], 0.0))
            o_ref[...] = r.astype(out_dtype)

    in_specs = [pl.BlockSpec((tm, tn), lambda i, j, n: (i, n)),
                pl.BlockSpec((1, tko, tn), lambda i, j, n: (n // npt, j, n % npt))]
    args = [a, b]
    extra = axpy[0] if axpy is not None else relu2grad
    if extra is not None:
        in_specs.append(pl.BlockSpec((tm, tko), lambda i, j, n: (i, j)))
        args.append(extra)
    return pl.pallas_call(
        body, name=name, grid=(M // tm, K // tko, nn), in_specs=in_specs,
        out_specs=pl.BlockSpec((tm, tko), lambda i, j, n: (i, j)),
        out_shape=jax.ShapeDtypeStruct((M, K), out_dtype),
        scratch_shapes=[pltpu.VMEM((tm, tko), F32)],
        compiler_params=pltpu.CompilerParams(dimension_semantics=("parallel", "parallel", "arbitrary")),
    )(*args)


def mm_tn(a, g, *, shards, tko, tn, name, prologue=None, into=None):
    M, K = a.shape
    _, N = g.shape
    Ns = N // shards
    tm = _row_tile(M)
    npt = Ns // tn
    nm = M // tm

    def body(*refs):
        if into is None:
            a_ref, g_ref, o_ref, acc = refs
        else:
            a_ref, g_ref, _, o_ref, acc = refs
        m = pl.program_id(2)

        @pl.when(m == 0)
        def _():
            acc[...] = jnp.zeros_like(acc)

        x = a_ref[...]
        if prologue == "relu2":
            x = jnp.square(jnp.maximum(x, 0.0))
        acc[...] += _dot_tn(x.astype(BF16), g_ref[...].astype(BF16))

        @pl.when(m == nm - 1)
        def _():
            if into is None or into[2] == "layer":
                o_ref[0] = acc[...]
            else:
                o_ref[0, 0] = acc[...]

    in_specs = [pl.BlockSpec((tm, tko), lambda i, j, m: (m, i)),
                pl.BlockSpec((tm, tn), lambda i, j, m: (m, j))]
    scratch = [pltpu.VMEM((tko, tn), F32)]
    params = pltpu.CompilerParams(dimension_semantics=("parallel", "parallel", "arbitrary"))
    if into is None:
        return pl.pallas_call(
            body, name=name, grid=(K // tko, N // tn, nm), in_specs=in_specs,
            out_specs=pl.BlockSpec((1, tko, tn), lambda i, j, m: (j // npt, i, j % npt)),
            out_shape=jax.ShapeDtypeStruct((shards, K, Ns), F32), scratch_shapes=scratch, compiler_params=params,
        )(a, g)
    buf, layer, how = into
    if how == "layer":
        out_spec = pl.BlockSpec((1, tko, tn), lambda i, j, m: (layer, i, j))
    elif how == "cols":
        npt4 = (N // 4) // tn
        out_spec = pl.BlockSpec((1, 1, tko, tn), lambda i, j, m: (j // npt4, layer, i, j % npt4))
    else:
        kpt4 = (K // 4) // tko
        out_spec = pl.BlockSpec((1, 1, tko, tn), lambda i, j, m: (i // kpt4, layer, i % kpt4, j))
    return pl.pallas_call(
        body, name=name, grid=(K // tko, N // tn, nm), in_specs=in_specs + [pl.BlockSpec(memory_space=pl.ANY)],
        out_specs=out_spec, out_shape=jax.ShapeDtypeStruct(buf.shape, F32), input_output_aliases={2: 0},
        scratch_shapes=scratch, compiler_params=params,
    )(a, g, buf)


def _ln_stats(z):
    mu = jnp.mean(z, axis=-1, keepdims=True)
    zc = z - mu
    var = jnp.mean(jnp.square(zc), axis=-1, keepdims=True)
    r = lax.rsqrt(var + LN_EPS)
    return zc * r, r


def ln_fwd(x, g, b, *, name, res=None):
    L, Dm = x.shape
    tr = _row_tile(L)
    g2, b2 = g.reshape(1, Dm), b.reshape(1, Dm)

    def body(*refs):
        if res is None:
            x_ref, g_ref, b_ref, y_ref, yb_ref = refs
            z = x_ref[...]
        else:
            x_ref, r_ref, g_ref, b_ref, y_ref, yb_ref, z_ref = refs
            z = DN_ALPHA * r_ref[...] + x_ref[...]
            z_ref[...] = z
        xh, _ = _ln_stats(z)
        y = xh * g_ref[...] + b_ref[...]
        y_ref[...] = y
        yb_ref[...] = y.astype(BF16)

    row = pl.BlockSpec((tr, Dm), lambda i: (i, 0))
    vec = pl.BlockSpec((1, Dm), lambda i: (0, 0))
    sds = jax.ShapeDtypeStruct((L, Dm), F32)
    sdb = jax.ShapeDtypeStruct((L, Dm), BF16)
    if res is None:
        y, yb = pl.pallas_call(body, name=name, grid=(L // tr,), in_specs=[row, vec, vec], out_specs=(row, row),
                               out_shape=(sds, sdb))(x, g2, b2)
        return y, yb, x
    return pl.pallas_call(body, name=name, grid=(L // tr,), in_specs=[row, row, vec, vec], out_specs=(row, row, row),
                          out_shape=(sds, sdb, sds))(x, res, g2, b2)


def ln_bwd(dy, z, g, *, name):
    L, Dm = z.shape
    tr = _row_tile(L)

    def body(dy_ref, z_ref, g_ref, dz_ref, dg_ref, db_ref):
        @pl.when(pl.program_id(0) == 0)
        def _():
            dg_ref[...] = jnp.zeros_like(dg_ref)
            db_ref[...] = jnp.zeros_like(db_ref)

        dyv = dy_ref[...]
        xh, r = _ln_stats(z_ref[...])
        dxh = dyv * g_ref[...]
        m1 = jnp.mean(dxh, axis=-1, keepdims=True)
        m2 = jnp.mean(dxh * xh, axis=-1, keepdims=True)
        dz_ref[...] = r * (dxh - m1 - xh * m2)
        dg_ref[...] += jnp.sum(dyv * xh, axis=0, keepdims=True)
        db_ref[...] += jnp.sum(dyv, axis=0, keepdims=True)

    row = pl.BlockSpec((tr, Dm), lambda i: (i, 0))
    vec = pl.BlockSpec((1, Dm), lambda i: (0, 0))
    return pl.pallas_call(
        body, name=name, grid=(L // tr,), in_specs=[row, row, vec], out_specs=(row, vec, vec),
        out_shape=(jax.ShapeDtypeStruct((L, Dm), F32), jax.ShapeDtypeStruct((1, Dm), F32), jax.ShapeDtypeStruct((1, Dm), F32)),
        compiler_params=pltpu.CompilerParams(dimension_semantics=("arbitrary",)),
    )(dy, z, g.reshape(1, Dm))


def loss_fwd_bwd(h, target, *, name):
    L, Dm = h.shape
    nb = L // BLK

    def body(h_ref, t_ref, l_ref, dh_ref):
        i = pl.program_id(0)

        @pl.when(i == 0)
        def _():
            l_ref[...] = jnp.zeros_like(l_ref)
            dh_ref[...] = jnp.zeros_like(dh_ref)

        @pl.when(i > 0)
        def _():
            e = h_ref[...] - t_ref[...]
            dh_ref[...] = e * (1.0 / Dm)
            part = jnp.sum(jnp.sum(jnp.square(e), axis=-1, keepdims=True) * (1.0 / Dm), axis=0, keepdims=True)
            l_ref[...] += 0.5 * part

    return pl.pallas_call(
        body, name=name, grid=(nb,),
        in_specs=[pl.BlockSpec((BLK, Dm), lambda i: (i, 0)),
                  pl.BlockSpec((BLK, Dm), lambda i: (jnp.maximum(i - 1, 0), 0))],
        out_specs=(pl.BlockSpec((8, 128), lambda i: (0, 0)), pl.BlockSpec((BLK, Dm), lambda i: (i, 0))),
        out_shape=(jax.ShapeDtypeStruct((8, 128), F32), jax.ShapeDtypeStruct((L, Dm), F32)),
        compiler_params=pltpu.CompilerParams(dimension_semantics=("arbitrary",)),
    )(h, target)


def _swap_half(x, half):
    ax = x.ndim - 1
    n = x.shape[ax]
    lane = lax.broadcasted_iota(jnp.int32, x.shape, ax)
    up = pltpu.roll(x, n - half, ax)
    dn = pltpu.roll(x, half, ax)
    return jnp.where((lane % (2 * half)) < half, up, dn)


def _rope(x, cs, sn, half):
    return x * cs + _swap_half(x, half) * sn


def _rope_t(dy, cs, sn, half):
    return dy * cs + _swap_half(dy * sn, half)


def _rms(x):
    r = lax.rsqrt(jnp.mean(jnp.square(x), axis=-1, keepdims=True) + LN_EPS)
    return x * r, r


def mla_pre_fwd(P, gq, gkv, cs, sn, *, name):
    L = P.shape[0]
    tr = _row_tile(L)

    def body(p_ref, gq_ref, gkv_ref, cs_ref, sn_ref, nq_ref, nkv_ref, kr_ref):
        cq = p_ref[:, 0:MLA_Q_LORA]
        ckv = p_ref[:, MLA_Q_LORA:MLA_Q_LORA + MLA_KV_LORA]
        kr = p_ref[:, 640:768]
        nq_ref[...] = (_rms(cq)[0] * gq_ref[...]).astype(BF16)
        nkv_ref[...] = (_rms(ckv)[0] * gkv_ref[...]).astype(BF16)
        krr = _rope(kr, cs_ref[...], sn_ref[...], MLA_ROPE // 2)
        kr_ref[...] = (krr + pltpu.roll(krr, MLA_ROPE, 1)).astype(BF16)

    return pl.pallas_call(
        body, name=name, grid=(L // tr,),
        in_specs=[pl.BlockSpec((tr, 768), lambda i: (i, C_CQ // 768)),
                  pl.BlockSpec((1, MLA_Q_LORA), lambda i: (0, 0)), pl.BlockSpec((1, MLA_KV_LORA), lambda i: (0, 0)),
                  pl.BlockSpec((tr, 128), lambda i: (i, 0)), pl.BlockSpec((tr, 128), lambda i: (i, 0))],
        out_specs=(pl.BlockSpec((tr, MLA_Q_LORA), lambda i: (i, 0)), pl.BlockSpec((tr, MLA_KV_LORA), lambda i: (i, 0)),
                   pl.BlockSpec((tr, 128), lambda i: (i, 0))),
        out_shape=(jax.ShapeDtypeStruct((L, MLA_Q_LORA), BF16), jax.ShapeDtypeStruct((L, MLA_KV_LORA), BF16),
                   jax.ShapeDtypeStruct((L, 128), BF16)),
    )(P, gq.reshape(1, -1), gkv.reshape(1, -1), cs, sn)


def mla_pre_bwd(P, dnq, dnkv, dkr, gq, gkv, cs, sn, *, name):
    L = P.shape[0]
    tr = _row_tile(L)

    def body(p_ref, dnq_ref, dnkv_ref, dkr_ref, gq_ref, gkv_ref, cs_ref, sn_ref, dp_ref, dgq_ref, dgkv_ref):
        @pl.when(pl.program_id(0) == 0)
        def _():
            dgq_ref[...] = jnp.zeros_like(dgq_ref)
            dgkv_ref[...] = jnp.zeros_like(dgkv_ref)

        def rms_bwd(x, dy, g_ref, dg_ref):
            xn, r = _rms(x)
            dxn = dy * g_ref[...]
            dg_ref[...] += jnp.sum(dy * xn, axis=0, keepdims=True)
            return r * (dxn - xn * jnp.mean(dxn * xn, axis=-1, keepdims=True))

        dp_ref[:, 0:MLA_Q_LORA] = rms_bwd(p_ref[:, 0:MLA_Q_LORA], dnq_ref[...], gq_ref, dgq_ref)
        dp_ref[:, MLA_Q_LORA:640] = rms_bwd(p_ref[:, MLA_Q_LORA:640], dnkv_ref[...], gkv_ref, dgkv_ref)
        d2 = dkr_ref[...]
        lane = lax.broadcasted_iota(jnp.int32, d2.shape, 1)
        dkr = jnp.where(lane < MLA_ROPE, d2 + pltpu.roll(d2, 128 - MLA_ROPE, 1), 0.0)
        dp_ref[:, 640:768] = _rope_t(dkr, cs_ref[...], sn_ref[...], MLA_ROPE // 2)

    return pl.pallas_call(
        body, name=name, grid=(L // tr,),
        in_specs=[pl.BlockSpec((tr, 768), lambda i: (i, C_CQ // 768)),
                  pl.BlockSpec((tr, MLA_Q_LORA), lambda i: (i, 0)), pl.BlockSpec((tr, MLA_KV_LORA), lambda i: (i, 0)),
                  pl.BlockSpec((tr, 128), lambda i: (i, 0)),
                  pl.BlockSpec((1, MLA_Q_LORA), lambda i: (0, 0)), pl.BlockSpec((1, MLA_KV_LORA), lambda i: (0, 0)),
                  pl.BlockSpec((tr, 128), lambda i: (i, 0)), pl.BlockSpec((tr, 128), lambda i: (i, 0))],
        out_specs=(pl.BlockSpec((tr, 768), lambda i: (i, 0)), pl.BlockSpec((1, MLA_Q_LORA), lambda i: (0, 0)),
                   pl.BlockSpec((1, MLA_KV_LORA), lambda i: (0, 0))),
        out_shape=(jax.ShapeDtypeStruct((L, 768), F32), jax.ShapeDtypeStruct((1, MLA_Q_LORA), F32),
                   jax.ShapeDtypeStruct((1, MLA_KV_LORA), F32)),
        compiler_params=pltpu.CompilerParams(dimension_semantics=("arbitrary",)),
    )(P, dnq, dnkv, dkr, gq.reshape(1, -1), gkv.reshape(1, -1), cs, sn)


def _tri(kind):
    r = lax.broadcasted_iota(jnp.int32, (BLK, BLK), 0)
    c = lax.broadcasted_iota(jnp.int32, (BLK, BLK), 1)
    t = ((r > c) if kind == "right" else (r < c)).astype(BF16)
    return jnp.concatenate([t, t], axis=0)


def _tri_sum(x, tt):
    hi = x.astype(BF16)
    lo = (x - hi.astype(F32)).astype(BF16)
    return _dot(jnp.concatenate([hi, lo], axis=1), tt)


def _sb_tile(q, k, i, j, tt_right, R):
    row = lax.broadcasted_iota(jnp.int32, (BLK, BLK), 0)
    col = lax.broadcasted_iota(jnp.int32, (BLK, BLK), 1)
    s_idx = j * BLK + col
    mask = (s_idx < i * BLK + row) & (s_idx >= N_PAD)
    z = _dot_nt(q, k)
    sp = jnp.maximum(z, 0.0) + jnp.log1p(jnp.exp(-jnp.abs(z)))
    lk = jnp.where(mask, -sp, 0.0)
    E = _tri_sum(lk, tt_right) + R
    return mask, z, sp, lk, E


def _old_sb_fwd(P, *, name):
    L = P.shape[0]
    nb = L // BLK

    def body(q_ref, k_ref, v_ref, o_ref, c_ref):
        i = pl.program_id(1)
        tt = _tri("right")
        lane = lax.broadcasted_iota(jnp.int32, (BLK, 128), 1)
        qs = [(q_ref[:, 64 * h:64 * h + 64] * 0.125).astype(BF16) for h in range(2)]

        def step(jj, carry):
            j = i - jj
            off = pl.multiple_of(j * BLK, BLK)
            kb = k_ref[pl.ds(off, BLK), :].astype(BF16)
            vb = v_ref[pl.ds(off, BLK), :].astype(BF16)
            out = []
            for h in range(2):
                o, R = carry[2 * h], carry[2 * h + 1]
                sl = slice(64 * h, 64 * h + 64)
                mask, z, sp, lk, E = _sb_tile(qs[h], kb[:, sl], i, j, tt, R)
                w = jnp.where(mask, jnp.exp(z - sp + E), 0.0)
                c_ref[h] = jnp.where(lane == j, R, c_ref[h])
                out += [o + _dot(w.astype(BF16), vb[:, sl]), R + jnp.sum(lk, axis=1, keepdims=True)]
            return tuple(out)

        c_ref[...] = jnp.zeros_like(c_ref)
        z0 = (jnp.zeros((BLK, 64), F32), jnp.zeros((BLK, 1), F32))
        res = lax.fori_loop(0, i + 1, step, z0 + z0)
        o_ref[...] = jnp.concatenate([res[0], res[2]], axis=1)

    return pl.pallas_call(
        body, name=name, grid=(SB_HEADS // 2, nb),
        in_specs=[pl.BlockSpec((BLK, 128), lambda hp, i: (i, C_SBQ // 128 + hp)),
                  pl.BlockSpec((L, 128), lambda hp, i: (0, C_SBK // 128 + hp)),
                  pl.BlockSpec((L, 128), lambda hp, i: (0, C_SBV // 128 + hp))],
        out_specs=(pl.BlockSpec((BLK, 128), lambda hp, i: (i, hp)), pl.BlockSpec((2, BLK, 128), lambda hp, i: (hp, i, 0))),
        out_shape=(jax.ShapeDtypeStruct((L, 512), F32), jax.ShapeDtypeStruct((SB_HEADS, L, 128), F32)),
        compiler_params=pltpu.CompilerParams(dimension_semantics=("parallel", "arbitrary")),
    )(P, P, P)


def _old_sb_bwd(P, carries, dmixed, *, name):
    L = P.shape[0]
    nb = L // BLK

    def body(q_ref, k_ref, v_ref, c_ref, do_ref, dq_ref, dk_ref, dv_ref):
        i = pl.program_id(1)

        @pl.when(i == 0)
        def _():
            dk_ref[...] = jnp.zeros_like(dk_ref)
            dv_ref[...] = jnp.zeros_like(dv_ref)

        tr = _tri("right")
        tl = _tri("left")
        lane = lax.broadcasted_iota(jnp.int32, (BLK, 128), 1)
        qs = [(q_ref[:, 64 * h:64 * h + 64] * 0.125).astype(BF16) for h in range(2)]
        dos = [do_ref[:, 64 * h:64 * h + 64].astype(BF16) for h in range(2)]

        def step(j, carry):
            off = pl.multiple_of(j * BLK, BLK)
            kb = k_ref[pl.ds(off, BLK), :].astype(BF16)
            vb = v_ref[pl.ds(off, BLK), :].astype(BF16)
            out, dks, dvs = [], [], []
            for h in range(2):
                dq, PL = carry[2 * h], carry[2 * h + 1]
                sl = slice(64 * h, 64 * h + 64)
                R = jnp.sum(jnp.where(lane == j, c_ref[h], 0.0), axis=1, keepdims=True)
                mask, z, sp, lk, E = _sb_tile(qs[h], kb[:, sl], i, j, tr, R)
                sig = jnp.exp(z - sp)
                w = jnp.where(mask, sig * jnp.exp(E), 0.0)
                dA = _dot_nt(dos[h], vb[:, sl]) * w
                Pp = _tri_sum(dA, tl) + PL
                dz = jnp.where(mask, dA - sig * (dA + Pp), 0.0).astype(BF16)
                dks.append(_dot_tn(dz, qs[h]))
                dvs.append(_dot_tn(w.astype(BF16), dos[h]))
                out += [dq + _dot(dz, kb[:, sl]), PL + jnp.sum(dA, axis=1, keepdims=True)]
            dk_ref[pl.ds(off, BLK), :] += jnp.concatenate(dks, axis=1)
            dv_ref[pl.ds(off, BLK), :] += jnp.concatenate(dvs, axis=1)
            return tuple(out)

        z0 = (jnp.zeros((BLK, 64), F32), jnp.zeros((BLK, 1), F32))
        res = lax.fori_loop(0, i + 1, step, z0 + z0)
        dq_ref[...] = jnp.concatenate([res[0], res[2]], axis=1) * 0.125

    blk = lambda c0: pl.BlockSpec((BLK, 128), lambda hp, i: (i, c0 + hp))
    full = lambda c0: pl.BlockSpec((L, 128), lambda hp, i: (0, c0 + hp))
    sds = jax.ShapeDtypeStruct((L, 512), F32)
    return pl.pallas_call(
        body, name=name, grid=(SB_HEADS // 2, nb),
        in_specs=[blk(C_SBQ // 128), full(C_SBK // 128), full(C_SBV // 128),
                  pl.BlockSpec((2, BLK, 128), lambda hp, i: (hp, i, 0)), blk(0)],
        out_specs=(blk(0), full(0), full(0)), out_shape=(sds, sds, sds),
        compiler_params=pltpu.CompilerParams(dimension_semantics=("parallel", "arbitrary")),
    )(P, P, P, carries, dmixed)


def _mla_mask(i, j):
    row = lax.broadcasted_iota(jnp.int32, (BLK, BLK), 0)
    col = lax.broadcasted_iota(jnp.int32, (BLK, BLK), 1)
    t_idx = i * BLK + row
    s_idx = j * BLK + col
    return (s_idx <= t_idx) & ((s_idx >= N_PAD) | (s_idx == t_idx))


def _mla_q(q_ref, cs_ref, sn_ref):
    qr = _rope(q_ref[:, 128:256], cs_ref[...], sn_ref[...], MLA_ROPE // 2)
    qn = [q_ref[:, 64 * h:64 * h + 64].astype(BF16) for h in range(2)]
    qrs = [qr[:, 32 * h:32 * h + 32].astype(BF16) for h in range(2)]
    return qn, qrs


def _old_mla_fwd(Q, KV, KR, cs, sn, *, name):
    L = Q.shape[0]
    nb = L // BLK

    def body(q_ref, kn_ref, v_ref, kr_ref, cs_ref, sn_ref, o_ref, lse_ref):
        i = pl.program_id(1)
        qn, qrs = _mla_q(q_ref, cs_ref, sn_ref)

        def step(j, carry):
            off = pl.multiple_of(j * BLK, BLK)
            knb = kn_ref[pl.ds(off, BLK), :]
            vb = v_ref[pl.ds(off, BLK), :]
            krb = kr_ref[pl.ds(off, BLK), 0:MLA_ROPE]
            mask = _mla_mask(i, j)
            out = []
            for h in range(2):
                m, l, acc = carry[3 * h], carry[3 * h + 1], carry[3 * h + 2]
                sl = slice(64 * h, 64 * h + 64)
                s = (_dot_nt(qn[h], knb[:, sl]) + _dot_nt(qrs[h], krb)) * MLA_SCALE
                s = jnp.where(mask, s, NEG)
                m_new = jnp.maximum(m, jnp.max(s, axis=1, keepdims=True))
                a = jnp.exp(m - m_new)
                p = jnp.exp(s - m_new)
                out += [m_new, a * l + jnp.sum(p, axis=1, keepdims=True), a * acc + _dot(p.astype(BF16), vb[:, sl])]
            return tuple(out)

        z0 = (jnp.full((BLK, 1), NEG, F32), jnp.zeros((BLK, 1), F32), jnp.zeros((BLK, 64), F32))
        res = lax.fori_loop(0, i + 1, step, z0 + z0)
        o_ref[...] = jnp.concatenate([res[2] / res[1], res[5] / res[4]], axis=1)
        lane = lax.broadcasted_iota(jnp.int32, (BLK, 128), 1)
        lse0 = res[0] + jnp.log(res[1])
        lse1 = res[3] + jnp.log(res[4])
        lse_ref[0] = jnp.where(lane == 0, lse0, jnp.where(lane == 1, lse1, 0.0))

    return pl.pallas_call(
        body, name=name, grid=(MLA_HEADS // 2, nb),
        in_specs=[pl.BlockSpec((BLK, 256), lambda hp, i: (i, hp)),
                  pl.BlockSpec((L, 128), lambda hp, i: (0, hp)),
                  pl.BlockSpec((L, 128), lambda hp, i: (0, 4 + hp)),
                  pl.BlockSpec((L, 128), lambda hp, i: (0, 0)),
                  pl.BlockSpec((BLK, 128), lambda hp, i: (i, 0)), pl.BlockSpec((BLK, 128), lambda hp, i: (i, 0))],
        out_specs=(pl.BlockSpec((BLK, 128), lambda hp, i: (i, hp)), pl.BlockSpec((1, BLK, 128), lambda hp, i: (hp, i, 0))),
        out_shape=(jax.ShapeDtypeStruct((L, 512), F32), jax.ShapeDtypeStruct((4, L, 128), F32)),
        compiler_params=pltpu.CompilerParams(dimension_semantics=("parallel", "arbitrary")),
    )(Q, KV, KV, KR, cs, sn)


def _old_mla_bwd(Q, KV, KR, cs, sn, mixed, dmixed, lse, *, name):
    L = Q.shape[0]
    nb = L // BLK

    def body(q_ref, kn_ref, v_ref, kr_ref, cs_ref, sn_ref, o_ref, do_ref, lse_ref, dq_ref, dkn_ref, dv_ref, dkr_ref):
        hp = pl.program_id(0)
        i = pl.program_id(1)

        @pl.when(i == 0)
        def _():
            dkn_ref[...] = jnp.zeros_like(dkn_ref)
            dv_ref[...] = jnp.zeros_like(dv_ref)

        @pl.when((i == 0) & (hp == 0))
        def _():
            dkr_ref[...] = jnp.zeros_like(dkr_ref)

        qn, qrs = _mla_q(q_ref, cs_ref, sn_ref)
        dos, dd, lses = [], [], []
        for h in range(2):
            sl = slice(64 * h, 64 * h + 64)
            d = do_ref[:, sl]
            dos.append(d.astype(BF16))
            dd.append(jnp.sum(d * o_ref[:, sl], axis=1, keepdims=True))
            lses.append(lse_ref[0, :, h:h + 1])

        def step(j, carry):
            off = pl.multiple_of(j * BLK, BLK)
            knb = kn_ref[pl.ds(off, BLK), :]
            vb = v_ref[pl.ds(off, BLK), :]
            krb = kr_ref[pl.ds(off, BLK), 0:MLA_ROPE]
            mask = _mla_mask(i, j)
            out, dkns, dvs = [], [], []
            dkr = jnp.zeros((BLK, MLA_ROPE), F32)
            for h in range(2):
                dqn, dqr = carry[2 * h], carry[2 * h + 1]
                sl = slice(64 * h, 64 * h + 64)
                s = (_dot_nt(qn[h], knb[:, sl]) + _dot_nt(qrs[h], krb)) * MLA_SCALE
                p = jnp.where(mask, jnp.exp(s - lses[h]), 0.0)
                dp = _dot_nt(dos[h], vb[:, sl])
                ds = (p * (dp - dd[h]) * MLA_SCALE).astype(BF16)
                dkns.append(_dot_tn(ds, qn[h]))
                dvs.append(_dot_tn(p.astype(BF16), dos[h]))
                dkr = dkr + _dot_tn(ds, qrs[h])
                out += [dqn + _dot(ds, knb[:, sl]), dqr + _dot(ds, krb)]
            dkn_ref[pl.ds(off, BLK), :] += jnp.concatenate(dkns, axis=1)
            dv_ref[pl.ds(off, BLK), :] += jnp.concatenate(dvs, axis=1)
            dkr_ref[pl.ds(off, BLK), :] += jnp.concatenate([dkr, jnp.zeros((BLK, 128 - MLA_ROPE), F32)], axis=1)
            return tuple(out)

        z0 = (jnp.zeros((BLK, 64), F32), jnp.zeros((BLK, MLA_ROPE), F32))
        res = lax.fori_loop(0, i + 1, step, z0 + z0)
        dqr = jnp.concatenate([res[1], res[3], jnp.zeros((BLK, 64), F32)], axis=1)
        dq_ref[...] = jnp.concatenate([res[0], res[2], _rope_t(dqr, cs_ref[...], sn_ref[...], MLA_ROPE // 2)], axis=1)

    blk = lambda c0: pl.BlockSpec((BLK, 128), lambda hp, i: (i, c0 + hp))
    full = lambda c0: pl.BlockSpec((L, 128), lambda hp, i: (0, c0 + hp))
    tab = pl.BlockSpec((BLK, 128), lambda hp, i: (i, 0))
    return pl.pallas_call(
        body, name=name, grid=(MLA_HEADS // 2, nb),
        in_specs=[pl.BlockSpec((BLK, 256), lambda hp, i: (i, hp)), full(0), full(4),
                  pl.BlockSpec((L, 128), lambda hp, i: (0, 0)), tab, tab, blk(4), blk(4),
                  pl.BlockSpec((1, BLK, 128), lambda hp, i: (hp, i, 0))],
        out_specs=(pl.BlockSpec((BLK, 256), lambda hp, i: (i, hp)), full(0), full(0),
                   pl.BlockSpec((L, 128), lambda hp, i: (0, 0))),
        out_shape=(jax.ShapeDtypeStruct((L, 1024), F32), jax.ShapeDtypeStruct((L, 512), F32),
                   jax.ShapeDtypeStruct((L, 512), F32), jax.ShapeDtypeStruct((L, 128), F32)),
        compiler_params=pltpu.CompilerParams(dimension_semantics=("arbitrary", "arbitrary")),
    )(Q, KV, KV, KR, cs, sn, mixed, dmixed, lse)


SB_UNROLL = 2


def _tq(L):
    return 384 if L % 384 == 0 else BLK


def _softplus(z):
    na = lax.bitcast_convert_type(lax.bitcast_convert_type(z, jnp.uint32) | jnp.uint32(0x80000000), F32)
    return jnp.maximum(z, 0.0) + jnp.log(1.0 + jnp.exp(na))


def _head_split(x, first):
    zero = jnp.zeros_like(x)
    return jnp.where(first, x, zero), jnp.where(first, zero, x)


def _sb_mask(I, j, tq):
    row = lax.broadcasted_iota(jnp.int32, (tq, BLK), 0)
    col = lax.broadcasted_iota(jnp.int32, (tq, BLK), 1)
    s_idx = j * BLK + col
    return (s_idx < I * tq + row) & (s_idx >= N_PAD)


def _tri2(kind, splits):
    r = lax.broadcasted_iota(jnp.int32, (256, 256), 0)
    c = lax.broadcasted_iota(jnp.int32, (256, 256), 1)
    same = (r < BLK) == (c < BLK)
    t = (same & ((r > c) if kind == "right" else (r < c))).astype(BF16)
    return jnp.concatenate([t] * splits, axis=0)


def _split2(x):
    hi = x.astype(BF16)
    lo = (x - hi.astype(F32)).astype(BF16)
    return jnp.concatenate([hi, lo], axis=1)


def _sb_mask2(I, j, tq):
    row = lax.broadcasted_iota(jnp.int32, (tq, 256), 0)
    col = lax.broadcasted_iota(jnp.int32, (tq, 256), 1)
    s_idx = j * BLK + (col & (BLK - 1))
    return (s_idx < I * tq + row) & (s_idx >= N_PAD)


def _per_head(x, r0, r1):
    return jnp.concatenate([x[:, 0:BLK] + r0, x[:, BLK:2 * BLK] + r1], axis=1)


def sb_fwd(P, *, name):
    L = P.shape[0]
    tq = _tq(L)
    nd = tq // BLK

    def body(q_ref, k_ref, v_ref, o_ref, c_ref):
        I = pl.program_id(1)
        tt = _tri2("right", 2)
        lane_q = lax.broadcasted_iota(jnp.int32, (tq, 128), 1)
        first_k = lax.broadcasted_iota(jnp.int32, (BLK, 128), 1) < 64
        q = (q_ref[...] * 0.125).astype(BF16)
        c_ref[...] = jnp.zeros_like(c_ref)

        def tiles(T, carry, kind):
            o, R0, R1 = carry
            js = [T * nd + nd - 1 - u for u in range(nd)]
            st = []
            for j in js:
                off = pl.multiple_of(j * BLK, BLK)
                kcat = jnp.concatenate(_head_split(k_ref[pl.ds(off, BLK), :].astype(BF16), first_k), axis=0)
                st.append([_dot_nt(q, kcat), off])
            for u, (s, j) in enumerate(zip(st, js)):
                sp = _softplus(s[0])
                mask = _sb_mask2(I, j, tq) if kind == "diag" else (pad_ok if kind == "first" and u == nd - 1 else None)
                spm = sp if mask is None else jnp.where(mask, sp, 0.0)
                s += [sp, spm, mask, _dot(_split2(spm), tt)]
            for (z, off, sp, spm, mask, S), j in zip(st, js):
                vcat = jnp.concatenate(_head_split(v_ref[pl.ds(off, BLK), :].astype(BF16), first_k), axis=0)
                w = jnp.exp(_per_head(z - sp - S, R0, R1))
                if mask is not None:
                    w = jnp.where(mask, w, 0.0)
                c_ref[0] = jnp.where(lane_q == j, R0, c_ref[0])
                c_ref[1] = jnp.where(lane_q == j, R1, c_ref[1])
                o = o + _dot(w.astype(BF16), vcat)
                R0 = R0 - (S[:, 0:1] + spm[:, 0:1])
                R1 = R1 - (S[:, BLK:BLK + 1] + spm[:, BLK:BLK + 1])
            return (o, R0, R1)

        pad_ok = (lax.broadcasted_iota(jnp.int32, (tq, 256), 1) & (BLK - 1)) >= N_PAD
        carry = (jnp.zeros((tq, 128), F32), jnp.zeros((tq, 1), F32), jnp.zeros((tq, 1), F32))
        carry = tiles(I, carry, "diag")
        carry = lax.fori_loop(0, jnp.maximum(I - 1, 0), lambda t, c: tiles(I - 1 - t, c, None), carry)
        carry = lax.fori_loop(0, jnp.minimum(I, 1), lambda t, c: tiles(0, c, "first"), carry)
        o_ref[...] = carry[0]

    return pl.pallas_call(
        body, name=name, grid=(SB_HEADS // 2, L // tq),
        in_specs=[pl.BlockSpec((tq, 128), lambda hp, i: (i, C_SBQ // 128 + hp)),
                  pl.BlockSpec((L, 128), lambda hp, i: (0, C_SBK // 128 + hp)),
                  pl.BlockSpec((L, 128), lambda hp, i: (0, C_SBV // 128 + hp))],
        out_specs=(pl.BlockSpec((tq, 128), lambda hp, i: (i, hp)), pl.BlockSpec((2, tq, 128), lambda hp, i: (hp, i, 0))),
        out_shape=(jax.ShapeDtypeStruct((L, 1536), F32), jax.ShapeDtypeStruct((SB_HEADS, L, 128), F32)),
        compiler_params=pltpu.CompilerParams(dimension_semantics=("parallel", "arbitrary")),
    )(P, P, P)


def sb_bwd(P, carries, dmixed, *, name):
    L = P.shape[0]
    tq = _tq(L)
    nd = tq // BLK

    def body(q_ref, k_ref, v_ref, c_ref, do_ref, dq_ref, dk_ref, dv_ref):
        I = pl.program_id(1)

        @pl.when(I == 0)
        def _():
            dk_ref[...] = jnp.zeros_like(dk_ref)
            dv_ref[...] = jnp.zeros_like(dv_ref)

        tr = _tri2("right", 2)
        tl = _tri2("left", 1)
        lane_q = lax.broadcasted_iota(jnp.int32, (tq, 128), 1)
        first_k = lax.broadcasted_iota(jnp.int32, (BLK, 128), 1) < 64
        q = (q_ref[...] * 0.125).astype(BF16)
        do = do_ref[...].astype(BF16)

        def tiles(T, carry, kind):
            dq, PL0, PL1 = carry
            js = [T * nd + u for u in range(nd)]
            st = []
            for j in js:
                off = pl.multiple_of(j * BLK, BLK)
                kcat = jnp.concatenate(_head_split(k_ref[pl.ds(off, BLK), :].astype(BF16), first_k), axis=0)
                vcat = jnp.concatenate(_head_split(v_ref[pl.ds(off, BLK), :].astype(BF16), first_k), axis=0)
                st.append([off, kcat, _dot_nt(q, kcat), _dot_nt(do, vcat)])
            for u, (s, j) in enumerate(zip(st, js)):
                z = s[2]
                sp = _softplus(z)
                mask = _sb_mask2(I, j, tq) if kind == "diag" else (pad_ok if kind == "first" and u == 0 else None)
                spm = sp if mask is None else jnp.where(mask, sp, 0.0)
                s += [mask, jnp.exp(z - sp), _dot(_split2(spm), tr)]
            for s, j in zip(st, js):
                off, kcat, z, dw, mask, sig, S = s
                R0 = jnp.sum(jnp.where(lane_q == j, c_ref[0], 0.0), axis=1, keepdims=True)
                R1 = jnp.sum(jnp.where(lane_q == j, c_ref[1], 0.0), axis=1, keepdims=True)
                w = sig * jnp.exp(_per_head(-S, R0, R1))
                if mask is not None:
                    w = jnp.where(mask, w, 0.0)
                dA = dw * w
                dvf = _dot_tn(w.astype(BF16), do)
                dv_ref[pl.ds(off, BLK), :] += jnp.where(first_k, dvf[0:BLK], dvf[BLK:2 * BLK])
                s += [dA, _dot(dA.astype(BF16), tl)]
            for off, kcat, z, dw, mask, sig, S, dA, pre in st:
                dz = dA - sig * (dA + _per_head(pre, PL0, PL1))
                if mask is not None:
                    dz = jnp.where(mask, dz, 0.0)
                dzb = dz.astype(BF16)
                dkf = _dot_tn(dzb, q)
                dk_ref[pl.ds(off, BLK), :] += jnp.where(first_k, dkf[0:BLK], dkf[BLK:2 * BLK])
                dq = dq + _dot(dzb, kcat)
                PL0 = PL0 + (pre[:, BLK - 1:BLK] + dA[:, BLK - 1:BLK])
                PL1 = PL1 + (pre[:, 2 * BLK - 1:2 * BLK] + dA[:, 2 * BLK - 1:2 * BLK])
            return (dq, PL0, PL1)

        pad_ok = (lax.broadcasted_iota(jnp.int32, (tq, 256), 1) & (BLK - 1)) >= N_PAD
        carry = (jnp.zeros((tq, 128), F32), jnp.zeros((tq, 1), F32), jnp.zeros((tq, 1), F32))
        carry = lax.fori_loop(0, jnp.minimum(I, 1), lambda t, c: tiles(0, c, "first"), carry)
        carry = lax.fori_loop(1, jnp.maximum(I, 1), lambda T, c: tiles(T, c, None), carry)
        carry = tiles(I, carry, "diag")
        dq_ref[...] = carry[0] * 0.125

    blk = lambda c0: pl.BlockSpec((tq, 128), lambda hp, i: (i, c0 + hp))
    full = lambda c0: pl.BlockSpec((L, 128), lambda hp, i: (0, c0 + hp))
    sds = jax.ShapeDtypeStruct((L, 512), F32)
    return pl.pallas_call(
        body, name=name, grid=(SB_HEADS // 2, L // tq),
        in_specs=[blk(C_SBQ // 128), full(C_SBK // 128), full(C_SBV // 128),
                  pl.BlockSpec((2, tq, 128), lambda hp, i: (hp, i, 0)), blk(0)],
        out_specs=(blk(0), full(0), full(0)), out_shape=(sds, sds, sds),
        compiler_params=pltpu.CompilerParams(dimension_semantics=("parallel", "arbitrary")),
    )(P, P, P, carries, dmixed)


def _v2_sb_fwd(P, *, name):
    L = P.shape[0]
    tq = _tq(L)
    nd = tq // BLK

    def body(q_ref, k_ref, v_ref, o_ref, c_ref):
        I = pl.program_id(1)
        tt = _tri("right")
        lane_q = lax.broadcasted_iota(jnp.int32, (tq, 128), 1)
        first_k = lax.broadcasted_iota(jnp.int32, (BLK, 128), 1) < 64
        qm = [x.astype(BF16) for x in _head_split(q_ref[...] * 0.125, lane_q < 64)]
        c_ref[...] = jnp.zeros_like(c_ref)

        def tile(j, carry, masked):
            o, R = carry[0], carry[1:]
            off = pl.multiple_of(j * BLK, BLK)
            kb = k_ref[pl.ds(off, BLK), :].astype(BF16)
            vcat = jnp.concatenate(_head_split(v_ref[pl.ds(off, BLK), :].astype(BF16), first_k), axis=0)
            mask = _sb_mask(I, j, tq) if masked else None
            ws, Rn = [], []
            for h in range(2):
                z = _dot_nt(qm[h], kb)
                sp = _softplus(z)
                spm = jnp.where(mask, sp, 0.0) if masked else sp
                w = jnp.exp(z - sp - _tri_sum(spm, tt) + R[h])
                if masked:
                    w = jnp.where(mask, w, 0.0)
                c_ref[h] = jnp.where(lane_q == j, R[h], c_ref[h])
                ws.append(w.astype(BF16))
                Rn.append(R[h] - jnp.sum(spm, axis=1, keepdims=True))
            return (o + _dot(jnp.concatenate(ws, axis=1), vcat), Rn[0], Rn[1])

        carry = (jnp.zeros((tq, 128), F32), jnp.zeros((tq, 1), F32), jnp.zeros((tq, 1), F32))
        carry = lax.fori_loop(0, nd, lambda t, c: tile(I * nd + nd - 1 - t, c, True), carry)
        carry = lax.fori_loop(0, jnp.maximum(I * nd - 1, 0), lambda t, c: tile(I * nd - 1 - t, c, False), carry)
        carry = lax.fori_loop(0, jnp.minimum(I, 1), lambda t, c: tile(0, c, True), carry)
        o_ref[...] = carry[0]

    return pl.pallas_call(
        body, name=name, grid=(SB_HEADS // 2, L // tq),
        in_specs=[pl.BlockSpec((tq, 128), lambda hp, i: (i, C_SBQ // 128 + hp)),
                  pl.BlockSpec((L, 128), lambda hp, i: (0, C_SBK // 128 + hp)),
                  pl.BlockSpec((L, 128), lambda hp, i: (0, C_SBV // 128 + hp))],
        out_specs=(pl.BlockSpec((tq, 128), lambda hp, i: (i, hp)), pl.BlockSpec((2, tq, 128), lambda hp, i: (hp, i, 0))),
        out_shape=(jax.ShapeDtypeStruct((L, 512), F32), jax.ShapeDtypeStruct((SB_HEADS, L, 128), F32)),
        compiler_params=pltpu.CompilerParams(dimension_semantics=("parallel", "arbitrary")),
    )(P, P, P)


def _v2_sb_bwd(P, carries, dmixed, *, name):
    L = P.shape[0]
    tq = _tq(L)
    nd = tq // BLK

    def body(q_ref, k_ref, v_ref, c_ref, do_ref, dq_ref, dk_ref, dv_ref):
        I = pl.program_id(1)

        @pl.when(I == 0)
        def _():
            dk_ref[...] = jnp.zeros_like(dk_ref)
            dv_ref[...] = jnp.zeros_like(dv_ref)

        tr = _tri("right")
        tl = _tri("left")
        lane_q = lax.broadcasted_iota(jnp.int32, (tq, 128), 1)
        first_k = lax.broadcasted_iota(jnp.int32, (BLK, 128), 1) < 64
        qm = [x.astype(BF16) for x in _head_split(q_ref[...] * 0.125, lane_q < 64)]
        dom = [x.astype(BF16) for x in _head_split(do_ref[...], lane_q < 64)]
        qcat = jnp.concatenate(qm, axis=0)
        docat = jnp.concatenate(dom, axis=0)

        def tile(j, carry, masked):
            dq, PL = carry[0], carry[1:]
            off = pl.multiple_of(j * BLK, BLK)
            kb = k_ref[pl.ds(off, BLK), :].astype(BF16)
            vb = v_ref[pl.ds(off, BLK), :].astype(BF16)
            kcat = jnp.concatenate(_head_split(kb, first_k), axis=0)
            mask = _sb_mask(I, j, tq) if masked else None
            dzs, wsb, PLn = [], [], []
            for h in range(2):
                R = jnp.sum(jnp.where(lane_q == j, c_ref[h], 0.0), axis=1, keepdims=True)
                z = _dot_nt(qm[h], kb)
                sp = _softplus(z)
                spm = jnp.where(mask, sp, 0.0) if masked else sp
                sig = jnp.exp(z - sp)
                w = sig * jnp.exp(R - _tri_sum(spm, tr))
                if masked:
                    w = jnp.where(mask, w, 0.0)
                dA = _dot_nt(dom[h], vb) * w
                dz = dA - sig * (dA + _tri_sum(dA, tl) + PL[h])
                if masked:
                    dz = jnp.where(mask, dz, 0.0)
                dzs.append(dz.astype(BF16))
                wsb.append(w.astype(BF16))
                PLn.append(PL[h] + jnp.sum(dA, axis=1, keepdims=True))
            dk_ref[pl.ds(off, BLK), :] += _dot_tn(jnp.concatenate(dzs, axis=0), qcat)
            dv_ref[pl.ds(off, BLK), :] += _dot_tn(jnp.concatenate(wsb, axis=0), docat)
            return (dq + _dot(jnp.concatenate(dzs, axis=1), kcat), PLn[0], PLn[1])

        carry = (jnp.zeros((tq, 128), F32), jnp.zeros((tq, 1), F32), jnp.zeros((tq, 1), F32))
        carry = lax.fori_loop(0, jnp.minimum(I, 1), lambda t, c: tile(0, c, True), carry)
        carry = lax.fori_loop(1, jnp.maximum(I * nd, 1), lambda j, c: tile(j, c, False), carry)
        carry = lax.fori_loop(0, nd, lambda t, c: tile(I * nd + t, c, True), carry)
        dq_ref[...] = carry[0] * 0.125

    blk = lambda c0: pl.BlockSpec((tq, 128), lambda hp, i: (i, c0 + hp))
    full = lambda c0: pl.BlockSpec((L, 128), lambda hp, i: (0, c0 + hp))
    sds = jax.ShapeDtypeStruct((L, 512), F32)
    return pl.pallas_call(
        body, name=name, grid=(SB_HEADS // 2, L // tq),
        in_specs=[blk(C_SBQ // 128), full(C_SBK // 128), full(C_SBV // 128),
                  pl.BlockSpec((2, tq, 128), lambda hp, i: (hp, i, 0)), blk(0)],
        out_specs=(blk(0), full(0), full(0)), out_shape=(sds, sds, sds),
        compiler_params=pltpu.CompilerParams(dimension_semantics=("parallel", "arbitrary")),
    )(P, P, P, carries, dmixed)


def _mla_mask2(I, j, tq):
    row = lax.broadcasted_iota(jnp.int32, (tq, tq), 0)
    col = lax.broadcasted_iota(jnp.int32, (tq, tq), 1)
    t_idx = I * tq + row
    s_idx = j * tq + col
    return (s_idx <= t_idx) & ((s_idx >= N_PAD) | (s_idx == t_idx))


def _mla_qcat(q_ref, cs_ref, sn_ref, lane_q):
    qn = q_ref[:, 0:128]
    qr = _rope(q_ref[:, 128:256], cs_ref[...], sn_ref[...], MLA_ROPE // 2)
    zero = jnp.zeros_like(qn)
    r0 = lane_q < MLA_ROPE
    r1 = (lane_q >= MLA_ROPE) & (lane_q < 2 * MLA_ROPE)
    n0, n1 = _head_split(qn, lane_q < 64)
    return [jnp.concatenate([n0, jnp.where(r0, qr, zero)], axis=1).astype(BF16),
            jnp.concatenate([n1, jnp.where(r1, qr, zero)], axis=1).astype(BF16)]


def mla_fwd(Q, KV, KR, cs, sn, mixed, *, name):
    L = Q.shape[0]
    tq = _tq(L)

    def body(q_ref, kn_ref, v_ref, kr_ref, cs_ref, sn_ref, o_ref, lse_ref):
        I = pl.program_id(1)
        lane_q = lax.broadcasted_iota(jnp.int32, (tq, 128), 1)
        first_q = lane_q < 64
        qcat = _mla_qcat(q_ref, cs_ref, sn_ref, lane_q)

        def tile(j, carry, masked, wide=1):
            acc, ml = carry[0], carry[1:]
            off = pl.multiple_of(j * tq, tq)
            tk = wide * tq
            first_k = lax.broadcasted_iota(jnp.int32, (tk, 128), 1) < 64
            kcat = jnp.concatenate([kn_ref[pl.ds(off, tk), :], kr_ref[pl.ds(off, tk), :]], axis=1)
            vcat = jnp.concatenate(_head_split(v_ref[pl.ds(off, tk), :], first_k), axis=0)
            mask = _mla_mask2(I, j, tq) if masked else None
            ps, al, out = [], [], []
            for h in range(2):
                m, l = ml[2 * h], ml[2 * h + 1]
                s = _dot_nt(qcat[h], kcat) * MLA_SCALE
                if masked:
                    s = jnp.where(mask, s, NEG)
                m_new = jnp.maximum(m, jnp.max(s, axis=1, keepdims=True))
                a = jnp.exp(m - m_new)
                p = jnp.exp(s - m_new)
                ps.append(p.astype(BF16))
                al.append(a)
                out += [m_new, a * l + jnp.sum(p, axis=1, keepdims=True)]
            acc = acc * jnp.where(first_q, al[0], al[1]) + _dot(jnp.concatenate(ps, axis=1), vcat)
            return (acc,) + tuple(out)

        ml0 = (jnp.full((tq, 1), NEG, F32), jnp.zeros((tq, 1), F32))
        carry = (jnp.zeros((tq, 128), F32),) + ml0 + ml0
        carry = lax.fori_loop(0, jnp.minimum(I, 1), lambda t, c: tile(0, c, True), carry)
        n_in = jnp.maximum(I - 1, 0)
        carry = lax.fori_loop(0, n_in // 2, lambda t, c: tile(1 + 2 * t, c, False, 2), carry)
        carry = lax.fori_loop(0, n_in % 2, lambda t, c: tile(I - 1, c, False), carry)
        carry = tile(I, carry, True)
        acc, m0, l0, m1, l1 = carry
        o_ref[...] = acc / jnp.where(first_q, l0, l1)
        lse_ref[0] = jnp.where(lane_q == 0, m0 + jnp.log(l0), jnp.where(lane_q == 1, m1 + jnp.log(l1), 0.0))

    return pl.pallas_call(
        lambda q, kn, v, kr, c, s, mixed_any, o, lse: body(q, kn, v, kr, c, s, o, lse),
        name=name, grid=(MLA_HEADS // 2, L // tq),
        in_specs=[pl.BlockSpec((tq, 256), lambda hp, i: (i, hp)),
                  pl.BlockSpec((L, 128), lambda hp, i: (0, hp)),
                  pl.BlockSpec((L, 128), lambda hp, i: (0, 4 + hp)),
                  pl.BlockSpec((L, 128), lambda hp, i: (0, 0)),
                  pl.BlockSpec((tq, 128), lambda hp, i: (i, 0)), pl.BlockSpec((tq, 128), lambda hp, i: (i, 0)),
                  pl.BlockSpec(memory_space=pl.ANY)],
        out_specs=(pl.BlockSpec((tq, 128), lambda hp, i: (i, 4 + hp)), pl.BlockSpec((1, tq, 128), lambda hp, i: (hp, i, 0))),
        out_shape=(jax.ShapeDtypeStruct(mixed.shape, F32), jax.ShapeDtypeStruct((4, L, 128), F32)),
        input_output_aliases={6: 0},
        compiler_params=pltpu.CompilerParams(dimension_semantics=("parallel", "arbitrary")),
    )(Q, KV, KV, KR, cs, sn, mixed)


def mla_bwd(Q, KV, KR, cs, sn, mixed, dmixed, lse, *, name):
    L = Q.shape[0]
    tq = _tq(L)

    def body(q_ref, kn_ref, v_ref, kr_ref, cs_ref, sn_ref, o_ref, do_ref, lse_ref, dq_ref, dkn_ref, dv_ref, dkr_ref):
        hp = pl.program_id(0)
        I = pl.program_id(1)

        @pl.when(I == 0)
        def _():
            dkn_ref[...] = jnp.zeros_like(dkn_ref)
            dv_ref[...] = jnp.zeros_like(dv_ref)

        @pl.when((I == 0) & (hp == 0))
        def _():
            dkr_ref[...] = jnp.zeros_like(dkr_ref)

        lane_q = lax.broadcasted_iota(jnp.int32, (tq, 128), 1)
        first_q = lane_q < 64
        qcat = _mla_qcat(q_ref, cs_ref, sn_ref, lane_q)
        qq = jnp.concatenate(qcat, axis=0)
        do = do_ref[...]
        prod = do * o_ref[...]
        dd = [jnp.sum(jnp.where(first_q, prod, 0.0), axis=1, keepdims=True),
              jnp.sum(jnp.where(first_q, 0.0, prod), axis=1, keepdims=True)]
        dom = [x.astype(BF16) for x in _head_split(do, first_q)]
        docat = jnp.concatenate(dom, axis=0)
        lses = [lse_ref[0, :, 0:1], lse_ref[0, :, 1:2]]

        def tile(j, dq, masked, wide=1):
            off = pl.multiple_of(j * tq, tq)
            tk = wide * tq
            lane_k = lax.broadcasted_iota(jnp.int32, (tk, 256), 1)
            sel0 = (lane_k < 64) | ((lane_k >= 128) & (lane_k < 128 + MLA_ROPE))
            sel1 = ((lane_k >= 64) & (lane_k < 128)) | ((lane_k >= 128 + MLA_ROPE) & (lane_k < 128 + 2 * MLA_ROPE))
            kcat = jnp.concatenate([kn_ref[pl.ds(off, tk), :], kr_ref[pl.ds(off, tk), :]], axis=1)
            vb = v_ref[pl.ds(off, tk), :]
            zero = jnp.zeros_like(kcat)
            kk = jnp.concatenate([jnp.where(sel0, kcat, zero), jnp.where(sel1, kcat, zero)], axis=0)
            mask = _mla_mask2(I, j, tq) if masked else None
            dss, pbs = [], []
            for h in range(2):
                s = _dot_nt(qcat[h], kcat) * MLA_SCALE
                p = jnp.exp(s - lses[h])
                if masked:
                    p = jnp.where(mask, p, 0.0)
                dp = _dot_nt(dom[h], vb)
                dss.append((p * (dp - dd[h]) * MLA_SCALE).astype(BF16))
                pbs.append(p.astype(BF16))
            dkc = _dot_tn(jnp.concatenate(dss, axis=0), qq)
            dkn_ref[pl.ds(off, tk), :] += dkc[:, 0:128]
            dkr_ref[pl.ds(off, tk), :] += dkc[:, 128:256]
            dv_ref[pl.ds(off, tk), :] += _dot_tn(jnp.concatenate(pbs, axis=0), docat)
            return dq + _dot(jnp.concatenate(dss, axis=1), kk)

        dq = jnp.zeros((tq, 256), F32)
        dq = lax.fori_loop(0, jnp.minimum(I, 1), lambda t, c: tile(0, c, True), dq)
        n_in = jnp.maximum(I - 1, 0)
        dq = lax.fori_loop(0, n_in // 2, lambda t, c: tile(1 + 2 * t, c, False, 2), dq)
        dq = lax.fori_loop(0, n_in % 2, lambda t, c: tile(I - 1, c, False), dq)
        dq = tile(I, dq, True)
        dq_ref[:, 0:128] = dq[:, 0:128]
        dq_ref[:, 128:256] = _rope_t(dq[:, 128:256], cs_ref[...], sn_ref[...], MLA_ROPE // 2)

    blk = lambda c0: pl.BlockSpec((tq, 128), lambda hp, i: (i, c0 + hp))
    full = lambda c0: pl.BlockSpec((L, 128), lambda hp, i: (0, c0 + hp))
    tab = pl.BlockSpec((tq, 128), lambda hp, i: (i, 0))
    return pl.pallas_call(
        body, name=name, grid=(MLA_HEADS // 2, L // tq),
        in_specs=[pl.BlockSpec((tq, 256), lambda hp, i: (i, hp)), full(0), full(4),
                  pl.BlockSpec((L, 128), lambda hp, i: (0, 0)), tab, tab, blk(4), blk(4),
                  pl.BlockSpec((1, tq, 128), lambda hp, i: (hp, i, 0))],
        out_specs=(pl.BlockSpec((tq, 256), lambda hp, i: (i, hp)), full(0), full(0),
                   pl.BlockSpec((L, 128), lambda hp, i: (0, 0))),
        out_shape=(jax.ShapeDtypeStruct((L, 1024), F32), jax.ShapeDtypeStruct((L, 512), F32),
                   jax.ShapeDtypeStruct((L, 512), F32), jax.ShapeDtypeStruct((L, 128), F32)),
        compiler_params=pltpu.CompilerParams(dimension_semantics=("arbitrary", "arbitrary")),
    )(Q, KV, KV, KR, cs, sn, mixed, dmixed, lse)


def _ret_decay(h):
    lg = RET_LOG_G[h]
    r = lax.broadcasted_iota(jnp.int32, (BLK, BLK), 0)
    c = lax.broadcasted_iota(jnp.int32, (BLK, BLK), 1)
    diff = (r - c).astype(F32)
    d_in = jnp.where(diff >= 0, jnp.exp(jnp.maximum(diff, 0.0) * lg), 0.0)
    idx = lax.broadcasted_iota(jnp.int32, (BLK, 1), 0).astype(F32)
    q_decay = jnp.exp((idx + 1.0) * lg)
    k_decay = jnp.exp((BLK - 1.0 - idx) * lg)
    c_decay = math.exp(BLK * lg)
    return d_in, q_decay, k_decay, c_decay


def _ret_qk(qk_ref, cs_ref, sn_ref, n):
    cs = jnp.concatenate([cs_ref[...]] * 2, axis=1)
    sn = jnp.concatenate([sn_ref[...]] * 2, axis=1)
    rq = _rope(qk_ref[:, 0:256], cs, sn, RET_QK // 2)
    row = n * BLK + lax.broadcasted_iota(jnp.int32, (BLK, 256), 0)
    kmul = jnp.where(row >= N_PAD, 0.125, 0.0)
    rk = _rope(qk_ref[:, 256:512], cs, sn, RET_QK // 2) * kmul
    return rq, rk, cs, sn, kmul


def _head_norm(y):
    mu = jnp.mean(y, axis=-1, keepdims=True)
    yc = y - mu
    r = lax.rsqrt(jnp.mean(jnp.square(yc), axis=-1, keepdims=True) + LN_EPS)
    return yc * r, r


def ret_fwd(P, cs, sn, mixed, *, name):
    L = P.shape[0]
    nb = L // BLK

    def body(qk_ref, v_ref, g_ref, cs_ref, sn_ref, o_ref, y_ref, st_ref, state):
        n = pl.program_id(0)

        @pl.when(n == 0)
        def _():
            state[...] = jnp.zeros_like(state)

        st_ref[0] = state[...]
        rq, rk, _, _, _ = _ret_qk(qk_ref, cs_ref, sn_ref, n)
        outs, ys = [], []
        for h in range(RET_HEADS):
            d_in, q_decay, k_decay, c_decay = _ret_decay(h)
            q = rq[:, 64 * h:64 * h + 64].astype(BF16)
            kf = rk[:, 64 * h:64 * h + 64]
            v = v_ref[:, 128 * h:128 * h + 128].astype(BF16)
            S = state[h]
            inner = _dot_nt(q, kf.astype(BF16)) * d_in
            y = _dot(inner.astype(BF16), v) + _dot(q, S.astype(BF16)) * q_decay
            state[h] = S * c_decay + _dot_tn((kf * k_decay).astype(BF16), v)
            g = g_ref[:, 128 * h:128 * h + 128]
            ys.append(y)
            outs.append(g * jax.nn.sigmoid(g) * _head_norm(y)[0])
        o_ref[...] = jnp.concatenate(outs, axis=1)
        y_ref[...] = jnp.concatenate(ys, axis=1)

    blk512 = lambda c: pl.BlockSpec((BLK, 512), lambda n: (n, c))
    tab = pl.BlockSpec((BLK, 128), lambda n: (n, 0))
    return pl.pallas_call(
        lambda qk, v, g, c, s, mixed_any, o, y, st, state: body(qk, v, g, c, s, o, y, st, state),
        name=name, grid=(nb,),
        in_specs=[blk512(C_RQ // 512), blk512(C_RV // 512), blk512(C_RG // 512), tab, tab, pl.BlockSpec(memory_space=pl.ANY)],
        out_specs=(blk512(2), blk512(0), pl.BlockSpec((1, RET_HEADS, RET_QK, RET_V), lambda n: (n, 0, 0, 0))),
        out_shape=(jax.ShapeDtypeStruct(mixed.shape, F32), jax.ShapeDtypeStruct((L, 512), F32),
                   jax.ShapeDtypeStruct((nb, RET_HEADS, RET_QK, RET_V), F32)),
        input_output_aliases={5: 0},
        scratch_shapes=[pltpu.VMEM((RET_HEADS, RET_QK, RET_V), F32)],
        compiler_params=pltpu.CompilerParams(dimension_semantics=("arbitrary",)),
    )(P, P, P, cs, sn, mixed)


def ret_bwd(P, y, states, dmixed, cs, sn, *, name):
    L = P.shape[0]
    nb = L // BLK

    def body(qk_ref, v_ref, g_ref, y_ref, st_ref, do_ref, cs_ref, sn_ref, dqk_ref, dv_ref, dg_ref, dstate):
        n = nb - 1 - pl.program_id(0)

        @pl.when(pl.program_id(0) == 0)
        def _():
            dstate[...] = jnp.zeros_like(dstate)

        rq, rk, cs, sn, kmul = _ret_qk(qk_ref, cs_ref, sn_ref, n)
        dqs, dks, dvs, dgs = [], [], [], []
        for h in range(RET_HEADS):
            d_in, q_decay, k_decay, c_decay = _ret_decay(h)
            sv = slice(128 * h, 128 * h + 128)
            q = rq[:, 64 * h:64 * h + 64].astype(BF16)
            kf = rk[:, 64 * h:64 * h + 64]
            k = kf.astype(BF16)
            kd = (kf * k_decay).astype(BF16)
            v = v_ref[:, sv].astype(BF16)
            g = g_ref[:, sv]
            do = do_ref[:, sv]
            yh = y_ref[:, sv]
            S = st_ref[0, h].astype(BF16)
            dS = dstate[h]
            sg = jax.nn.sigmoid(g)
            yn, r = _head_norm(yh)
            dgs.append(do * yn * (sg * (1.0 + g * (1.0 - sg))))
            dyn = do * (g * sg)
            dy = r * (dyn - jnp.mean(dyn, axis=-1, keepdims=True) - yn * jnp.mean(dyn * yn, axis=-1, keepdims=True))
            dyb = dy.astype(BF16)
            dyq = (dy * q_decay).astype(BF16)
            inner = (_dot_nt(q, k) * d_in).astype(BF16)
            A = (_dot_nt(dyb, v) * d_in).astype(BF16)
            dSb = dS.astype(BF16)
            dqs.append(_dot(A, k) + _dot_nt(dyq, S))
            dks.append(_dot_tn(A, q) + _dot_nt(v, dSb) * k_decay)
            dvs.append(_dot_tn(inner, dyb) + _dot(kd, dSb))
            dstate[h] = dS * c_decay + _dot_tn(q, dyq)
        drq = _rope_t(jnp.concatenate(dqs, axis=1), cs, sn, RET_QK // 2)
        drk = _rope_t(jnp.concatenate(dks, axis=1) * kmul, cs, sn, RET_QK // 2)
        dqk_ref[...] = jnp.concatenate([drq, drk], axis=1)
        dv_ref[...] = jnp.concatenate(dvs, axis=1)
        dg_ref[...] = jnp.concatenate(dgs, axis=1)

    blk512 = lambda c: pl.BlockSpec((BLK, 512), lambda t: (nb - 1 - t, c))
    tab = pl.BlockSpec((BLK, 128), lambda t: (nb - 1 - t, 0))
    sds = jax.ShapeDtypeStruct((L, 512), F32)
    return pl.pallas_call(
        body, name=name, grid=(nb,),
        in_specs=[blk512(C_RQ // 512), blk512(C_RV // 512), blk512(C_RG // 512), blk512(0),
                  pl.BlockSpec((1, RET_HEADS, RET_QK, RET_V), lambda t: (nb - 1 - t, 0, 0, 0)), blk512(2), tab, tab],
        out_specs=(blk512(0), blk512(0), blk512(0)), out_shape=(sds, sds, sds),
        scratch_shapes=[pltpu.VMEM((RET_HEADS, RET_QK, RET_V), F32)],
        compiler_params=pltpu.CompilerParams(dimension_semantics=("arbitrary",)),
    )(P, P, P, y, states, dmixed, cs, sn)


def _perm_w_in(w):
    pad = jnp.zeros(w.shape[:-1] + (N_INP - N_IN,), w.dtype)
    return jnp.concatenate([w[..., 0:1536], w[..., 2208:3744], w[..., 1536:2208], pad], axis=-1)


def _unperm_w_in(g):
    return jnp.concatenate([g[..., 0:1536], g[..., 3072:3744], g[..., 1536:3072]], axis=-1)


def _perm_w_uq(w):
    lead = w.shape[:-1]
    w5 = w.reshape(lead + (4, 2, 96))
    nope = w5[..., :64].reshape(lead + (4, 128))
    rope = w5[..., 64:].reshape(lead + (4, 64))
    return jnp.concatenate([nope, rope, jnp.zeros(lead + (4, 64), w.dtype)], axis=-1).reshape(lead + (1024,))


def _unperm_w_uq(g):
    lead = g.shape[:-1]
    g4 = g.reshape(lead + (4, 256))
    nope = g4[..., :128].reshape(lead + (4, 2, 64))
    rope = g4[..., 128:192].reshape(lead + (4, 2, 32))
    return jnp.concatenate([nope, rope], axis=-1).reshape(lead + (768,))


def _perm_w_ukv(w):
    lead = w.shape[:-1]
    w4 = w.reshape(lead + (8, 128))
    return jnp.concatenate([w4[..., :64].reshape(lead + (512,)), w4[..., 64:].reshape(lead + (512,))], axis=-1)


def _unperm_w_ukv(g):
    lead = g.shape[:-1]
    return jnp.concatenate([g[..., :512].reshape(lead + (8, 64)), g[..., 512:].reshape(lead + (8, 64))],
                           axis=-1).reshape(lead + (1024,))


def _rope_tables(L, half):
    pos = (jnp.arange(L) - N_PAD).astype(F32)
    inv = ROPE_THETA ** (-jnp.arange(half, dtype=F32) / half)
    ang = pos[:, None] * inv[None, :]
    cos, sin = jnp.cos(ang), jnp.sin(ang)
    reps = 128 // (2 * half)
    cs = jnp.tile(jnp.concatenate([cos, cos], axis=1), (1, reps))
    sn = jnp.tile(jnp.concatenate([-sin, sin], axis=1), (1, reps))
    return cs, sn


def _device_step(x, target, meta, ln_emb_g, ln_emb_b, w_in, q_norm, kv_norm, w_uq, w_ukv, w_out,
                 ln1_g, ln1_b, w_ff1, w_ff2, ln2_g, ln2_b):
    S = x.shape[0]
    L = S + BLK
    depth = w_in.shape[0]
    cs_m, sn_m = _rope_tables(L, MLA_ROPE // 2)
    cs_r, sn_r = _rope_tables(L, RET_QK // 2)
    hcat = jnp.concatenate([jnp.zeros((N_PAD, D_MODEL), F32), meta, x], axis=0)
    h, hb, _ = ln_fwd(hcat, ln_emb_g, ln_emb_b, name="ln_emb_fwd")

    w_in_sb, w_in_rest = w_in[..., :N_SB], w_in[..., N_SB:]
    saved = []
    for l in range(depth):
        Psb = mm_nn(hb, w_in_sb[l], tn=N_SB, name=f"in_proj_sb_{l}", out_dtype=BF16)
        P = mm_nn(hb, w_in_rest[l], tn=768, name=f"in_proj_{l}")
        mixed, sbc = sb_fwd(Psb, name=f"sb_fwd_{l}")
        nq, nkv, KR = mla_pre_fwd(P, q_norm[l], kv_norm[l], cs_m, sn_m, name=f"mla_pre_fwd_{l}")
        Q = mm_nn(nq, w_uq[l], tn=512, name=f"uq_{l}")
        KV = mm_nn(nkv, w_ukv[l], tn=512, name=f"ukv_{l}", out_dtype=BF16)
        mixed, lse = mla_fwd(Q, KV, KR, cs_m, sn_m, mixed, name=f"mla_fwd_{l}")
        mixed, y, states = ret_fwd(P, cs_r, sn_r, mixed, name=f"ret_fwd_{l}")
        w_out_l = w_out[l].reshape(1, 1536, D_MODEL)
        mix = mm_nn(mixed, w_out_l, tn=1024, name=f"out_proj_{l}")
        h1, h1b, z1 = ln_fwd(mix, ln1_g[l], ln1_b[l], res=h, name=f"ln1_fwd_{l}")
        U = mm_nn(h1b, w_ff1[l], tn=1024, name=f"ff1_{l}")
        w_ff2_l = w_ff2[l].reshape(1, D_FF, D_MODEL)
        mlp = mm_nn(U, w_ff2_l, tn=1024, tk=2048, prologue="relu2", name=f"ff2_{l}")
        h2, h2b, z2 = ln_fwd(mlp, ln2_g[l], ln2_b[l], res=h1, name=f"ln2_fwd_{l}")
        saved.append((hb, Psb, P, sbc, nq, nkv, KR, Q, KV, lse, y, states, mixed, z1, h1b, U, z2))
        h, hb = h2, h2b

    loss_t, dh = loss_fwd_bwd(h, target, name="loss")

    grads = {k: [None] * depth for k in ("q_norm", "kv_norm", "ln1_g", "ln1_b", "ln2_g", "ln2_b")}
    g_ff1 = lax.empty((4, depth, D_MODEL, D_FF // 4), F32)
    g_ff2 = lax.empty((4, depth, D_FF // 4, D_MODEL), F32)
    g_out = lax.empty((4, depth, 384, D_MODEL), F32)
    g_in = lax.empty((depth, D_MODEL, N_INP), F32)
    g_uq = lax.empty((depth, MLA_Q_LORA, 1024), F32)
    g_ukv = lax.empty((depth, MLA_KV_LORA, 1024), F32)
    for l in reversed(range(depth)):
        hb_in, Psb, P, sbc, nq, nkv, KR, Q, KV, lse, y, states, mixed, z1, h1b, U, z2 = saved[l]
        dz2, grads["ln2_g"][l], grads["ln2_b"][l] = ln_bwd(dh, z2, ln2_g[l], name=f"ln2_bwd_{l}")
        w_ff2_l = w_ff2[l].reshape(1, D_FF, D_MODEL)
        g_ff2 = mm_tn(U, dz2, shards=1, tko=1024, tn=1024, prologue="relu2", name=f"ff2_dw_{l}", into=(g_ff2, l, "rows"))
        dU = mm_nt(dz2, w_ff2_l, tn=1024, tko=1024, relu2grad=U, name=f"ff2_dx_{l}", out_dtype=BF16)
        g_ff1 = mm_tn(h1b, dU, shards=4, tko=1024, tn=1024, name=f"ff1_dw_{l}", into=(g_ff1, l, "cols"))
        dh1 = mm_nt(dU, w_ff1[l], tn=1024, tko=1024, axpy=(dz2, DN_ALPHA), name=f"ff1_dx_{l}")
        dz1, grads["ln1_g"][l], grads["ln1_b"][l] = ln_bwd(dh1, z1, ln1_g[l], name=f"ln1_bwd_{l}")
        w_out_l = w_out[l].reshape(1, 1536, D_MODEL)
        g_out = mm_tn(mixed, dz1, shards=1, tko=384, tn=1024, name=f"out_dw_{l}", into=(g_out, l, "rows"))
        dmixed = mm_nt(dz1, w_out_l, tn=1024, tko=1536, name=f"out_dx_{l}")
        d_rqk, d_rv, d_rg = ret_bwd(P, y, states, dmixed, cs_r, sn_r, name=f"ret_bwd_{l}")
        dQ, dKN, dV, dKR = mla_bwd(Q, KV, KR, cs_m, sn_m, mixed, dmixed, lse, name=f"mla_bwd_{l}")
        dKV = jnp.concatenate([dKN, dV], axis=1)
        g_uq = mm_tn(nq, dQ, shards=1, tko=MLA_Q_LORA, tn=512, name=f"uq_dw_{l}", into=(g_uq, l, "layer"))
        g_ukv = mm_tn(nkv, dKV, shards=1, tko=MLA_KV_LORA, tn=512, name=f"ukv_dw_{l}", into=(g_ukv, l, "layer"))
        dnq = mm_nt(dQ, w_uq[l], tn=1024, tko=MLA_Q_LORA, name=f"uq_dx_{l}")
        dnkv = mm_nt(dKV, w_ukv[l], tn=1024, tko=MLA_KV_LORA, name=f"ukv_dx_{l}")
        d_lat, grads["q_norm"][l], grads["kv_norm"][l] = mla_pre_bwd(P, dnq, dnkv, dKR, q_norm[l], kv_norm[l], cs_m, sn_m,
                                                                     name=f"mla_pre_bwd_{l}")
        dq_sb, dk_sb, dv_sb = sb_bwd(Psb, sbc, dmixed, name=f"sb_bwd_{l}")
        dP = jnp.concatenate([dq_sb, dk_sb, dv_sb, d_rqk, d_rv, d_rg, d_lat], axis=1).astype(BF16)
        g_in = mm_tn(hb_in, dP, shards=1, tko=1024, tn=1280, name=f"in_dw_{l}", into=(g_in, l, "layer"))
        dh = mm_nt(dP, w_in[l], tn=1920, tko=1024, axpy=(dz1, DN_ALPHA), name=f"in_dx_{l}")

    dhcat, dg_emb, db_emb = ln_bwd(dh, hcat, ln_emb_g, name="ln_emb_bwd")
    out = {k: jnp.stack(v) for k, v in grads.items()}
    out["w_ff1"], out["w_ff2"], out["w_out"] = g_ff1, g_ff2, g_out
    out["w_in"], out["w_uq"], out["w_ukv"] = g_in, g_uq, g_ukv
    out["ln_emb_g"], out["ln_emb_b"] = dg_emb, db_emb
    out["meta"] = dhcat[N_PAD:BLK]
    return loss_t[0, 0], dhcat[BLK:], out


MESH = pl.DeviceIdType.MESH
PEER_XOR = (2, 1, 3)
_HBM = pl.BlockSpec(memory_space=pltpu.HBM)


def _place():
    x, y, c = lax.axis_index("x"), lax.axis_index("y"), lax.axis_index("c")
    peers = [(1 - x, y, c), (x, 1 - y, c), (1 - x, 1 - y, c)]
    return x, y, c, 2 * x + y, peers, (x, y, 1 - c)


def gather_weight(w_shard, *, name):
    nl = w_shard.shape[0]
    hl = nl // 2

    def body(w_ref, out_ref, send_sems, recv_sems):
        x, y, c, s0, peers, sibling = _place()

        def piece(s, half):
            return out_ref.at[s, pl.ds(half * hl, hl)]

        def copy(k, s, half, to, src=None):
            return pltpu.make_async_remote_copy(src_ref=piece(s, half) if src is None else src, dst_ref=piece(s, half),
                                                send_sem=send_sems.at[k], recv_sem=recv_sems.at[k],
                                                device_id=to, device_id_type=MESH)

        first = [copy(k, s0, c, peers[k], src=w_ref.at[pl.ds(c * hl, hl)]) for k in range(3)]
        for cp in first:
            cp.start()
        passed = [copy(3 + k, s0 ^ PEER_XOR[k], c, sibling) for k in range(3)]
        for k in range(3):
            copy(k, s0 ^ PEER_XOR[k], c, peers[k]).wait_recv()
            passed[k].start()
        for k in range(3):
            copy(3 + k, s0 ^ PEER_XOR[k], 1 - c, sibling).wait_recv()
        for cp in first + passed:
            cp.wait_send()

    return pl.pallas_call(
        body, name=name, in_specs=[_HBM], out_specs=_HBM,
        out_shape=jax.ShapeDtypeStruct((4,) + w_shard.shape, w_shard.dtype),
        scratch_shapes=[pltpu.SemaphoreType.DMA((6,)), pltpu.SemaphoreType.DMA((6,))],
    )(w_shard)


def send_half_to_sibling(G, *, name):
    hl = G.shape[1] // 2

    def body(g_ref, out_ref, send_sem, recv_sem):
        x, y, c, s0, peers, sibling = _place()
        cp = pltpu.make_async_remote_copy(src_ref=g_ref.at[:, pl.ds((1 - c) * hl, hl)], dst_ref=out_ref,
                                          send_sem=send_sem, recv_sem=recv_sem, device_id=sibling, device_id_type=MESH)
        cp.start()
        cp.wait()

    return pl.pallas_call(
        body, name=name, in_specs=[_HBM], out_specs=_HBM,
        out_shape=jax.ShapeDtypeStruct((4, hl) + G.shape[2:], G.dtype),
        scratch_shapes=[pltpu.SemaphoreType.DMA, pltpu.SemaphoreType.DMA],
    )(G)


def scatter_to_chips(A, *, name):
    def body(a_ref, out_ref, send_sems, recv_sems):
        x, y, c, s0, peers, sibling = _place()
        copies = [pltpu.make_async_remote_copy(src_ref=a_ref.at[s0 ^ PEER_XOR[k]], dst_ref=out_ref.at[k],
                                               send_sem=send_sems.at[k], recv_sem=recv_sems.at[k],
                                               device_id=peers[k], device_id_type=MESH) for k in range(3)]
        for cp in copies:
            cp.start()
        for cp in copies:
            cp.wait()

    return pl.pallas_call(
        body, name=name, in_specs=[_HBM], out_specs=_HBM,
        out_shape=jax.ShapeDtypeStruct((3,) + A.shape[1:], A.dtype),
        scratch_shapes=[pltpu.SemaphoreType.DMA((3,)), pltpu.SemaphoreType.DMA((3,))],
    )(A)


def join_halves(buf, *, name):
    hl = buf.shape[0] // 2

    def body(b_ref, out_ref, send_sem, recv_sem):
        x, y, c, s0, peers, sibling = _place()
        cp = pltpu.make_async_remote_copy(src_ref=b_ref.at[pl.ds(c * hl, hl)], dst_ref=out_ref.at[pl.ds(c * hl, hl)],
                                          send_sem=send_sem, recv_sem=recv_sem, device_id=sibling, device_id_type=MESH)
        cp.start()
        pltpu.make_async_remote_copy(src_ref=b_ref.at[pl.ds((1 - c) * hl, hl)], dst_ref=out_ref.at[pl.ds((1 - c) * hl, hl)],
                                     send_sem=send_sem, recv_sem=recv_sem, device_id=sibling, device_id_type=MESH).wait_recv()
        cp.wait_send()

    return pl.pallas_call(
        body, name=name, in_specs=[_HBM], out_specs=_HBM, input_output_aliases={0: 0},
        out_shape=jax.ShapeDtypeStruct(buf.shape, buf.dtype),
        scratch_shapes=[pltpu.SemaphoreType.DMA, pltpu.SemaphoreType.DMA],
    )(buf)


def allgather8(xs, *, name, reduce):
    M, N = xs.shape

    def body(x_ref, out_ref, *rest):
        if reduce:
            all_ref, send_sems, recv_sems, local_sem = rest
        else:
            all_ref = out_ref
            send_sems, recv_sems, local_sem = rest
        x, y, c, s0, peers, sibling = _place()
        me = (x, y, c)
        chips = [(1 - x, y), (x, 1 - y), (1 - x, 1 - y)]

        def rows(px, py, pc):
            return all_ref.at[pl.ds((4 * px + 2 * py + pc) * M, M), :]

        def copy(k, block, to, src=None):
            return pltpu.make_async_remote_copy(src_ref=rows(*block) if src is None else src, dst_ref=rows(*block),
                                                send_sem=send_sems.at[k], recv_sem=recv_sems.at[k],
                                                device_id=to, device_id_type=MESH)

        mine = pltpu.make_async_copy(x_ref, rows(*me), local_sem)
        mine.start()
        first = [copy(0, me, sibling, src=x_ref)]
        first += [copy(1 + j, me, (*chip, c), src=x_ref) for j, chip in enumerate(chips)]
        for cp in first:
            cp.start()
        passed = [copy(4 + j, (*chip, c), sibling) for j, chip in enumerate(chips)]
        for j, chip in enumerate(chips):
            copy(1 + j, (*chip, c), me).wait_recv()
            passed[j].start()
        copy(0, sibling, me).wait_recv()
        for j, chip in enumerate(chips):
            copy(4 + j, (*chip, 1 - c), me).wait_recv()
        for cp in first + passed:
            cp.wait_send()
        mine.wait()
        if reduce:
            acc = all_ref[pl.ds(0, M), :]
            for d in range(1, 8):
                acc = acc + all_ref[pl.ds(d * M, M), :]
            out_ref[...] = acc

    vm = pl.BlockSpec(memory_space=pltpu.VMEM)
    scratch = [pltpu.SemaphoreType.DMA((7,)), pltpu.SemaphoreType.DMA((7,)), pltpu.SemaphoreType.DMA]
    if reduce:
        scratch = [pltpu.VMEM((8 * M, N), xs.dtype)] + scratch
    return pl.pallas_call(
        body, name=name, in_specs=[vm], out_specs=vm,
        out_shape=jax.ShapeDtypeStruct((M if reduce else 8 * M, N), xs.dtype), scratch_shapes=scratch,
    )(xs)


def add_halves(G, B, c, *, name):
    S, nl, R, C = G.shape
    hl = nl // 2
    tr = _pick(R, (512, 384, 256, 128))

    def body(c_ref, g_ref, b_ref, o_ref):
        o_ref[...] = (g_ref[...] + b_ref[...]).astype(BF16)

    blk = (1, 1, tr, C)
    return pl.pallas_call(
        body, name=name,
        grid_spec=pltpu.PrefetchScalarGridSpec(
            num_scalar_prefetch=1, grid=(S, hl, R // tr),
            in_specs=[pl.BlockSpec(blk, lambda s, l, r, cr: (s, cr[0] * hl + l, r, 0)),
                      pl.BlockSpec(blk, lambda s, l, r, cr: (s, l, r, 0))],
            out_specs=pl.BlockSpec(blk, lambda s, l, r, cr: (s, l, r, 0))),
        out_shape=jax.ShapeDtypeStruct((S, hl, R, C), BF16),
    )(jnp.reshape(c, (1,)).astype(jnp.int32), G, B)


def add_chips(G, B, Bc, c, s0, *, name):
    S, nl, R, C = G.shape
    hl = nl // 2
    tr = _pick(R, (512, 384, 256, 128))

    def body(pc_ref, ps_ref, g_ref, b_ref, c0_ref, c1_ref, c2_ref, o_ref):
        o_ref[...] = ((((g_ref[0] + b_ref[0]) + c0_ref[0].astype(F32)) + c1_ref[0].astype(F32)) + c2_ref[0].astype(F32))

    blk = (1, 1, tr, C)
    cspec = lambda k: pl.BlockSpec(blk, lambda l, r, pc, ps: (k, l, r, 0))
    return pl.pallas_call(
        body, name=name,
        grid_spec=pltpu.PrefetchScalarGridSpec(
            num_scalar_prefetch=2, grid=(hl, R // tr),
            in_specs=[pl.BlockSpec(blk, lambda l, r, pc, ps: (ps[0], pc[0] * hl + l, r, 0)),
                      pl.BlockSpec(blk, lambda l, r, pc, ps: (ps[0], l, r, 0)), cspec(0), cspec(1), cspec(2)],
            out_specs=pl.BlockSpec((1, tr, C), lambda l, r, pc, ps: (pc[0] * hl + l, r, 0))),
        out_shape=jax.ShapeDtypeStruct((nl, R, C), F32),
    )(jnp.reshape(c, (1,)).astype(jnp.int32), jnp.reshape(s0, (1,)).astype(jnp.int32), G, B, Bc, Bc, Bc)


def reduce_scatter_weight(G, c, s0, *, tag):
    B = send_half_to_sibling(G, name=f"rs_sib_{tag}")
    A = add_halves(G, B, c, name=f"rs_add1_{tag}")
    Bc = scatter_to_chips(A, name=f"rs_chips_{tag}")
    half = add_chips(G, B, Bc, c, s0, name=f"rs_add2_{tag}")
    return join_halves(half, name=f"rs_join_{tag}")


def adamw(w, g, m, v, *, name):
    shp = w.shape
    if len(shp) == 2:
        w, g, m, v = (a[None] for a in (w, g, m, v))
    nl, R, C = w.shape
    tr = R
    for t in (512, 384, 256, 128):
        if R % t == 0:
            tr = t
            break

    def body(w_ref, g_ref, m_ref, v_ref, d_ref, nm_ref, nv_ref):
        gv = g_ref[...]
        mn = ADAM_B1 * m_ref[...] + (1.0 - ADAM_B1) * gv
        vn = ADAM_B2 * v_ref[...] + (1.0 - ADAM_B2) * jnp.square(gv)
        m_hat = mn / (1.0 - ADAM_B1 ** ADAM_STEP)
        v_hat = vn / (1.0 - ADAM_B2 ** ADAM_STEP)
        d_ref[...] = -ADAM_LR * (m_hat / (jnp.sqrt(v_hat) + ADAM_EPS) + ADAM_WD * w_ref[...])
        nm_ref[...] = mn
        nv_ref[...] = vn

    spec = pl.BlockSpec((1, tr, C), lambda l, i: (l, i, 0))
    sds = jax.ShapeDtypeStruct((nl, R, C), F32)
    d, nm, nv = pl.pallas_call(body, name=name, grid=(nl, R // tr), in_specs=[spec] * 4, out_specs=(spec,) * 3,
                               out_shape=(sds,) * 3)(w, g, m, v)
    return d.reshape(shp), nm.reshape(shp), nv.reshape(shp)


_SMALL = ("ln_emb_g", "ln_emb_b", "q_norm", "kv_norm", "ln1_g", "ln1_b", "ln2_g", "ln2_b", "meta")


def _pack_small(d):
    flat = jnp.concatenate([d[k].reshape(-1) for k in _SMALL])
    rows = -(-flat.shape[0] // 128)
    rows = -(-rows // 8) * 8
    flat = jnp.concatenate([flat, jnp.zeros((rows * 128 - flat.shape[0],), F32)])
    return flat.reshape(rows, 128)


def _unpack_small(p, shapes):
    flat = p.reshape(-1)
    out, o = {}, 0
    for k in _SMALL:
        n = int(np.prod(shapes[k]))
        out[k] = flat[o:o + n].reshape(shapes[k])
        o += n
    return out


def kernel(x, meta_tokens, ln_emb_g, ln_emb_b, w_in, mla_q_norm, mla_kv_norm, w_uq, w_ukv, w_out, ln1_g, ln1_b, w_ff1, w_ff2, ln2_g, ln2_b, loss_target, m_meta_tokens, m_ln_emb_g, m_ln_emb_b, m_w_in, m_mla_q_norm, m_mla_kv_norm, m_w_uq, m_w_ukv, m_w_out, m_ln1_g, m_ln1_b, m_w_ff1, m_w_ff2, m_ln2_g, m_ln2_b, v_meta_tokens, v_ln_emb_g, v_ln_emb_b, v_w_in, v_mla_q_norm, v_mla_kv_norm, v_w_uq, v_w_ukv, v_w_out, v_ln1_g, v_ln1_b, v_w_ff1, v_w_ff2, v_ln2_g, v_ln2_b):
    xi, yi, ci = lax.axis_index("x"), lax.axis_index("y"), lax.axis_index("c")
    s0 = 2 * xi + yi
    nl = w_in.shape[0]

    big = {"w_in": w_in, "w_uq": w_uq, "w_ukv": w_ukv, "w_out": w_out, "w_ff1": w_ff1, "w_ff2": w_ff2}
    full = {}
    for k, v in big.items():
        vb = v.astype(BF16)
        full[k] = lax.dynamic_update_slice(gather_weight(vb, name=f"ag_{k}"), vb[None], (s0, 0, 0, 0))
    cols = lambda a: jnp.moveaxis(a, 0, 2).reshape(a.shape[1], a.shape[2], 4 * a.shape[3])
    k_w_in = _perm_w_in(cols(full["w_in"]))[:, None]
    k_w_uq = _perm_w_uq(cols(full["w_uq"]))[:, None]
    k_w_ukv = _perm_w_ukv(cols(full["w_ukv"]))[:, None]
    k_w_out = jnp.moveaxis(full["w_out"], 0, 1)
    k_w_ff1 = jnp.moveaxis(full["w_ff1"], 0, 1)
    k_w_ff2 = jnp.moveaxis(full["w_ff2"], 0, 1)
    meta_all = allgather8(meta_tokens, name="ag_meta", reduce=False)
    meta_full = jnp.concatenate([meta_all[32 * s:32 * s + N_META] for s in range(4)], axis=1)

    loss_part, grad_x, g = _device_step(x[0], loss_target[0], meta_full, ln_emb_g, ln_emb_b, k_w_in, mla_q_norm, mla_kv_norm,
                                        k_w_uq, k_w_ukv, k_w_out, ln1_g, ln1_b, k_w_ff1, k_w_ff2, ln2_g, ln2_b)
    loss = lax.psum(loss_part, ("x", "y", "c"))

    def col_shards(a):
        return jnp.moveaxis(a.reshape(a.shape[0], a.shape[1], 4, a.shape[2] // 4), 2, 0)

    G = {"w_in": col_shards(_unperm_w_in(g["w_in"])), "w_uq": col_shards(_unperm_w_uq(g["w_uq"])),
         "w_ukv": col_shards(_unperm_w_ukv(g["w_ukv"])), "w_out": g["w_out"], "w_ff1": g["w_ff1"], "w_ff2": g["w_ff2"]}
    gw = {k: reduce_scatter_weight(v, ci, s0, tag=k) for k, v in G.items()}

    small_shapes = {"ln_emb_g": (D_MODEL,), "ln_emb_b": (D_MODEL,), "q_norm": (nl, MLA_Q_LORA), "kv_norm": (nl, MLA_KV_LORA),
                    "ln1_g": (nl, D_MODEL), "ln1_b": (nl, D_MODEL), "ln2_g": (nl, D_MODEL), "ln2_b": (nl, D_MODEL),
                    "meta": (N_META, D_MODEL)}
    gs = _unpack_small(allgather8(_pack_small(g), name="ar_small", reduce=True), small_shapes)
    gw.update({"ln_emb_g": gs["ln_emb_g"], "ln_emb_b": gs["ln_emb_b"], "mla_q_norm": gs["q_norm"], "mla_kv_norm": gs["kv_norm"],
               "ln1_g": gs["ln1_g"], "ln1_b": gs["ln1_b"], "ln2_g": gs["ln2_g"], "ln2_b": gs["ln2_b"],
               "meta_tokens": lax.dynamic_slice_in_dim(gs["meta"], s0 * 256, 256, axis=1)})

    names = ["meta_tokens", "ln_emb_g", "ln_emb_b", "w_in", "mla_q_norm", "mla_kv_norm", "w_uq", "w_ukv", "w_out",
             "ln1_g", "ln1_b", "w_ff1", "w_ff2", "ln2_g", "ln2_b"]
    ws = [meta_tokens, ln_emb_g, ln_emb_b, w_in, mla_q_norm, mla_kv_norm, w_uq, w_ukv, w_out, ln1_g, ln1_b, w_ff1, w_ff2, ln2_g, ln2_b]
    ms = [m_meta_tokens, m_ln_emb_g, m_ln_emb_b, m_w_in, m_mla_q_norm, m_mla_kv_norm, m_w_uq, m_w_ukv, m_w_out, m_ln1_g, m_ln1_b, m_w_ff1, m_w_ff2, m_ln2_g, m_ln2_b]
    vs = [v_meta_tokens, v_ln_emb_g, v_ln_emb_b, v_w_in, v_mla_q_norm, v_mla_kv_norm, v_w_uq, v_w_ukv, v_w_out, v_ln1_g, v_ln1_b, v_w_ff1, v_w_ff2, v_ln2_g, v_ln2_b]
    deltas, new_m, new_v = [], [], []
    for n, w, m, v in zip(names, ws, ms, vs):
        w2 = w.reshape(1, -1) if w.ndim == 1 else w
        d, nm, nv = adamw(w2, gw[n].reshape(w2.shape), m.reshape(w2.shape), v.reshape(w2.shape), name=f"adamw_{n}")
        deltas.append(d.reshape(w.shape))
        new_m.append(nm.reshape(w.shape))
        new_v.append(nv.reshape(w.shape))
    grads_out = [gw[n].reshape(w.shape) for n, w in zip(names, ws)]
    return (loss, grad_x[None], *grads_out, *deltas, *new_m, *new_v)
```

```python
import functools
import math

import numpy as np
import jax
import jax.numpy as jnp
from jax import lax
from jax.experimental import pallas as pl
from jax.experimental.pallas import tpu as pltpu

F32 = jnp.float32
BF16 = jnp.bfloat16

D_MODEL = 1024
DEPTH = 4
N_META = 16
BLK = 128
N_PAD = 112
SB_HEADS = 8
MLA_HEADS = 8
MLA_NOPE = 64
MLA_ROPE = 32
MLA_V = 64
MLA_Q_LORA = 384
MLA_KV_LORA = 256
RET_HEADS = 4
RET_QK = 64
RET_V = 128
D_FF = 4 * D_MODEL
ROPE_THETA = 10000.0
LN_EPS = 1e-5
DN_ALPHA = (2 * DEPTH) ** 0.25
RET_GAMMA = tuple(1.0 - 2.0 ** (-5 - h) for h in range(RET_HEADS))
RET_LOG_G = tuple(float(np.log(np.float32(g))) for g in RET_GAMMA)
MLA_SCALE = (MLA_NOPE + MLA_ROPE) ** -0.5

ADAM_LR = 0.001
ADAM_B1 = 0.9
ADAM_B2 = 0.999
ADAM_EPS = 1e-08
ADAM_WD = 0.01
ADAM_STEP = 10

N_SB = 1536
C_SBQ, C_SBK, C_SBV = 0, 512, 1024
C_RQ, C_RK, C_RV, C_RG = 0, 256, 512, 1024
C_CQ, C_CKV, C_KR = 1536, 1920, 2176
N_IN = 3744
N_INP = 3840

NEG = -1e30


def _pick(n, cands):
    for t in cands:
        if n % t == 0:
            return t
    raise ValueError(f"no tile for {n} in {cands}")


def _row_tile(n):
    return _pick(n, (1056, 1024, 528, 512, 384, 256, 128))


def _dot(a, b):
    return jnp.dot(a, b, preferred_element_type=F32)


def _dot_nt(a, b):
    return lax.dot_general(a, b, (((1,), (1,)), ((), ())), preferred_element_type=F32)


def _dot_tn(a, b):
    return lax.dot_general(a, b, (((0,), (0,)), ((), ())), preferred_element_type=F32)


def mm_nn(a, b, *, tn, name, tk=None, prologue=None, axpy=None, out_dtype=F32):
    M, K = a.shape
    S, _, Ns = b.shape
    tm = _row_tile(M)
    tk = K if tk is None else tk
    npt = Ns // tn
    nk = K // tk
    alpha = None if axpy is None else axpy[1]

    def body(*refs):
        if axpy is None:
            a_ref, b_ref, o_ref, acc = refs
        else:
            a_ref, b_ref, e_ref, o_ref, acc = refs
        k = pl.program_id(2)

        @pl.when(k == 0)
        def _():
            acc[...] = jnp.zeros_like(acc)

        x = a_ref[...]
        if prologue == "relu2":
            x = jnp.square(jnp.maximum(x, 0.0))
        acc[...] += _dot(x.astype(BF16), b_ref[0])

        @pl.when(k == nk - 1)
        def _():
            r = acc[...]
            if axpy is not None:
                r = r + alpha * e_ref[...]
            o_ref[...] = r.astype(out_dtype)

    in_specs = [pl.BlockSpec((tm, tk), lambda i, j, k: (i, k)),
                pl.BlockSpec((1, tk, tn), lambda i, j, k: (j // npt, k, j % npt))]
    args = [a, b]
    if axpy is not None:
        in_specs.append(pl.BlockSpec((tm, tn), lambda i, j, k: (i, j)))
        args.append(axpy[0])
    return pl.pallas_call(
        body, name=name, grid=(M // tm, (S * Ns) // tn, nk), in_specs=in_specs,
        out_specs=pl.BlockSpec((tm, tn), lambda i, j, k: (i, j)),
        out_shape=jax.ShapeDtypeStruct((M, S * Ns), out_dtype),
        scratch_shapes=[pltpu.VMEM((tm, tn), F32)],
        compiler_params=pltpu.CompilerParams(dimension_semantics=("parallel", "parallel", "arbitrary")),
    )(*args)


def mm_nt(a, b, *, tn, tko, name, axpy=None, relu2grad=None, out_dtype=F32):
    M, N = a.shape
    S, K, Ns = b.shape
    tm = _row_tile(M)
    npt = Ns // tn
    nn = N // tn
    alpha = None if axpy is None else axpy[1]

    def body(*refs):
        if axpy is None and relu2grad is None:
            a_ref, b_ref, o_ref, acc = refs
        else:
            a_ref, b_ref, e_ref, o_ref, acc = refs
        n = pl.program_id(2)

        @pl.when(n == 0)
        def _():
            acc[...] = jnp.zeros_like(acc)

        acc[...] += _dot_nt(a_ref[...].astype(BF16), b_ref[0])

        @pl.when(n == nn - 1)
        def _():
            r = acc[...]
            if axpy is not None:
                r = r + alpha * e_ref[...]
            if relu2grad is not None:
                r = r * (2.0 * jnp.maximum(e_ref[...], 0.0))
            o_ref[...] = r.astype(out_dtype)

    in_specs = [pl.BlockSpec((tm, tn), lambda i, j, n: (i, n)),
                pl.BlockSpec((1, tko, tn), lambda i, j, n: (n // npt, j, n % npt))]
    args = [a, b]
    extra = axpy[0] if axpy is not None else relu2grad
    if extra is not None:
        in_specs.append(pl.BlockSpec((tm, tko), lambda i, j, n: (i, j)))
        args.append(extra)
    return pl.pallas_call(
        body, name=name, grid=(M // tm, K // tko, nn), in_specs=in_specs,
        out_specs=pl.BlockSpec((tm, tko), lambda i, j, n: (i, j)),
        out_shape=jax.ShapeDtypeStruct((M, K), out_dtype),
        scratch_shapes=[pltpu.VMEM((tm, tko), F32)],
        compiler_params=pltpu.CompilerParams(dimension_semantics=("parallel", "parallel", "arbitrary")),
    )(*args)


def mm_tn(a, g, *, shards, tko, tn, name, prologue=None, into=None):
    M, K = a.shape
    _, N = g.shape
    Ns = N // shards
    tm = _row_tile(M)
    npt = Ns // tn
    nm = M // tm

    def body(*refs):
        if into is None:
            a_ref, g_ref, o_ref, acc = refs
        else:
            a_ref, g_ref, _, o_ref, acc = refs
        m = pl.program_id(2)

        @pl.when(m == 0)
        def _():
            acc[...] = jnp.zeros_like(acc)

        x = a_ref[...]
        if prologue == "relu2":
            x = jnp.square(jnp.maximum(x, 0.0))
        acc[...] += _dot_tn(x.astype(BF16), g_ref[...].astype(BF16))

        @pl.when(m == nm - 1)
        def _():
            if into is None or into[2] == "layer":
                o_ref[0] = acc[...]
            else:
                o_ref[0, 0] = acc[...]

    in_specs = [pl.BlockSpec((tm, tko), lambda i, j, m: (m, i)),
                pl.BlockSpec((tm, tn), lambda i, j, m: (m, j))]
    scratch = [pltpu.VMEM((tko, tn), F32)]
    params = pltpu.CompilerParams(dimension_semantics=("parallel", "parallel", "arbitrary"))
    if into is None:
        return pl.pallas_call(
            body, name=name, grid=(K // tko, N // tn, nm), in_specs=in_specs,
            out_specs=pl.BlockSpec((1, tko, tn), lambda i, j, m: (j // npt, i, j % npt)),
            out_shape=jax.ShapeDtypeStruct((shards, K, Ns), F32), scratch_shapes=scratch, compiler_params=params,
        )(a, g)
    buf, layer, how = into
    if how == "layer":
        out_spec = pl.BlockSpec((1, tko, tn), lambda i, j, m: (layer, i, j))
    elif how == "cols":
        npt4 = (N // 4) // tn
        out_spec = pl.BlockSpec((1, 1, tko, tn), lambda i, j, m: (j // npt4, layer, i, j % npt4))
    else:
        kpt4 = (K // 4) // tko
        out_spec = pl.BlockSpec((1, 1, tko, tn), lambda i, j, m: (i // kpt4, layer, i % kpt4, j))
    return pl.pallas_call(
        body, name=name, grid=(K // tko, N // tn, nm), in_specs=in_specs + [pl.BlockSpec(memory_space=pl.ANY)],
        out_specs=out_spec, out_shape=jax.ShapeDtypeStruct(buf.shape, F32), input_output_aliases={2: 0},
        scratch_shapes=scratch, compiler_params=params,
    )(a, g, buf)


def _ln_stats(z):
    mu = jnp.mean(z, axis=-1, keepdims=True)
    zc = z - mu
    var = jnp.mean(jnp.square(zc), axis=-1, keepdims=True)
    r = lax.rsqrt(var + LN_EPS)
    return zc * r, r


def ln_fwd(x, g, b, *, name, res=None):
    L, Dm = x.shape
    tr = _row_tile(L)
    g2, b2 = g.reshape(1, Dm), b.reshape(1, Dm)

    def body(*refs):
        if res is None:
            x_ref, g_ref, b_ref, y_ref, yb_ref = refs
            z = x_ref[...]
        else:
            x_ref, r_ref, g_ref, b_ref, y_ref, yb_ref, z_ref = refs
            z = DN_ALPHA * r_ref[...] + x_ref[...]
            z_ref[...] = z
        xh, _ = _ln_stats(z)
        y = xh * g_ref[...] + b_ref[...]
        y_ref[...] = y
        yb_ref[...] = y.astype(BF16)

    row = pl.BlockSpec((tr, Dm), lambda i: (i, 0))
    vec = pl.BlockSpec((1, Dm), lambda i: (0, 0))
    sds = jax.ShapeDtypeStruct((L, Dm), F32)
    sdb = jax.ShapeDtypeStruct((L, Dm), BF16)
    if res is None:
        y, yb = pl.pallas_call(body, name=name, grid=(L // tr,), in_specs=[row, vec, vec], out_specs=(row, row),
                               out_shape=(sds, sdb))(x, g2, b2)
        return y, yb, x
    return pl.pallas_call(body, name=name, grid=(L // tr,), in_specs=[row, row, vec, vec], out_specs=(row, row, row),
                          out_shape=(sds, sdb, sds))(x, res, g2, b2)


def ln_bwd(dy, z, g, *, name):
    L, Dm = z.shape
    tr = _row_tile(L)

    def body(dy_ref, z_ref, g_ref, dz_ref, dg_ref, db_ref):
        @pl.when(pl.program_id(0) == 0)
        def _():
            dg_ref[...] = jnp.zeros_like(dg_ref)
            db_ref[...] = jnp.zeros_like(db_ref)

        dyv = dy_ref[...]
        xh, r = _ln_stats(z_ref[...])
        dxh = dyv * g_ref[...]
        m1 = jnp.mean(dxh, axis=-1, keepdims=True)
        m2 = jnp.mean(dxh * xh, axis=-1, keepdims=True)
        dz_ref[...] = r * (dxh - m1 - xh * m2)
        dg_ref[...] += jnp.sum(dyv * xh, axis=0, keepdims=True)
        db_ref[...] += jnp.sum(dyv, axis=0, keepdims=True)

    row = pl.BlockSpec((tr, Dm), lambda i: (i, 0))
    vec = pl.BlockSpec((1, Dm), lambda i: (0, 0))
    return pl.pallas_call(
        body, name=name, grid=(L // tr,), in_specs=[row, row, vec], out_specs=(row, vec, vec),
        out_shape=(jax.ShapeDtypeStruct((L, Dm), F32), jax.ShapeDtypeStruct((1, Dm), F32), jax.ShapeDtypeStruct((1, Dm), F32)),
        compiler_params=pltpu.CompilerParams(dimension_semantics=("arbitrary",)),
    )(dy, z, g.reshape(1, Dm))


def loss_fwd_bwd(h, target, *, name):
    L, Dm = h.shape
    nb = L // BLK

    def body(h_ref, t_ref, l_ref, dh_ref):
        i = pl.program_id(0)

        @pl.when(i == 0)
        def _():
            l_ref[...] = jnp.zeros_like(l_ref)
            dh_ref[...] = jnp.zeros_like(dh_ref)

        @pl.when(i > 0)
        def _():
            e = h_ref[...] - t_ref[...]
            dh_ref[...] = e * (1.0 / Dm)
            part = jnp.sum(jnp.sum(jnp.square(e), axis=-1, keepdims=True) * (1.0 / Dm), axis=0, keepdims=True)
            l_ref[...] += 0.5 * part

    return pl.pallas_call(
        body, name=name, grid=(nb,),
        in_specs=[pl.BlockSpec((BLK, Dm), lambda i: (i, 0)),
                  pl.BlockSpec((BLK, Dm), lambda i: (jnp.maximum(i - 1, 0), 0))],
        out_specs=(pl.BlockSpec((8, 128), lambda i: (0, 0)), pl.BlockSpec((BLK, Dm), lambda i: (i, 0))),
        out_shape=(jax.ShapeDtypeStruct((8, 128), F32), jax.ShapeDtypeStruct((L, Dm), F32)),
        compiler_params=pltpu.CompilerParams(dimension_semantics=("arbitrary",)),
    )(h, target)


def _swap_half(x, half):
    ax = x.ndim - 1
    n = x.shape[ax]
    lane = lax.broadcasted_iota(jnp.int32, x.shape, ax)
    up = pltpu.roll(x, n - half, ax)
    dn = pltpu.roll(x, half, ax)
    return jnp.where((lane % (2 * half)) < half, up, dn)


def _rope(x, cs, sn, half):
    return x * cs + _swap_half(x, half) * sn


def _rope_t(dy, cs, sn, half):
    return dy * cs + _swap_half(dy * sn, half)


def _rms(x):
    r = lax.rsqrt(jnp.mean(jnp.square(x), axis=-1, keepdims=True) + LN_EPS)
    return x * r, r


def mla_pre_fwd(P, gq, gkv, cs, sn, *, name):
    L = P.shape[0]
    tr = _row_tile(L)

    def body(p_ref, gq_ref, gkv_ref, cs_ref, sn_ref, nq_ref, nkv_ref, kr_ref):
        cq = p_ref[:, 0:MLA_Q_LORA]
        ckv = p_ref[:, MLA_Q_LORA:MLA_Q_LORA + MLA_KV_LORA]
        kr = p_ref[:, 640:768]
        nq_ref[...] = (_rms(cq)[0] * gq_ref[...]).astype(BF16)
        nkv_ref[...] = (_rms(ckv)[0] * gkv_ref[...]).astype(BF16)
        krr = _rope(kr, cs_ref[...], sn_ref[...], MLA_ROPE // 2)
        kr_ref[...] = (krr + pltpu.roll(krr, MLA_ROPE, 1)).astype(BF16)

    return pl.pallas_call(
        body, name=name, grid=(L // tr,),
        in_specs=[pl.BlockSpec((tr, 768), lambda i: (i, C_CQ // 768)),
                  pl.BlockSpec((1, MLA_Q_LORA), lambda i: (0, 0)), pl.BlockSpec((1, MLA_KV_LORA), lambda i: (0, 0)),
                  pl.BlockSpec((tr, 128), lambda i: (i, 0)), pl.BlockSpec((tr, 128), lambda i: (i, 0))],
        out_specs=(pl.BlockSpec((tr, MLA_Q_LORA), lambda i: (i, 0)), pl.BlockSpec((tr, MLA_KV_LORA), lambda i: (i, 0)),
                   pl.BlockSpec((tr, 128), lambda i: (i, 0))),
        out_shape=(jax.ShapeDtypeStruct((L, MLA_Q_LORA), BF16), jax.ShapeDtypeStruct((L, MLA_KV_LORA), BF16),
                   jax.ShapeDtypeStruct((L, 128), BF16)),
    )(P, gq.reshape(1, -1), gkv.reshape(1, -1), cs, sn)


def mla_pre_bwd(P, dnq, dnkv, dkr, gq, gkv, cs, sn, *, name):
    L = P.shape[0]
    tr = _row_tile(L)

    def body(p_ref, dnq_ref, dnkv_ref, dkr_ref, gq_ref, gkv_ref, cs_ref, sn_ref, dp_ref, dgq_ref, dgkv_ref):
        @pl.when(pl.program_id(0) == 0)
        def _():
            dgq_ref[...] = jnp.zeros_like(dgq_ref)
            dgkv_ref[...] = jnp.zeros_like(dgkv_ref)

        def rms_bwd(x, dy, g_ref, dg_ref):
            xn, r = _rms(x)
            dxn = dy * g_ref[...]
            dg_ref[...] += jnp.sum(dy * xn, axis=0, keepdims=True)
            return r * (dxn - xn * jnp.mean(dxn * xn, axis=-1, keepdims=True))

        dp_ref[:, 0:MLA_Q_LORA] = rms_bwd(p_ref[:, 0:MLA_Q_LORA], dnq_ref[...], gq_ref, dgq_ref)
        dp_ref[:, MLA_Q_LORA:640] = rms_bwd(p_ref[:, MLA_Q_LORA:640], dnkv_ref[...], gkv_ref, dgkv_ref)
        d2 = dkr_ref[...]
        lane = lax.broadcasted_iota(jnp.int32, d2.shape, 1)
        dkr = jnp.where(lane < MLA_ROPE, d2 + pltpu.roll(d2, 128 - MLA_ROPE, 1), 0.0)
        dp_ref[:, 640:768] = _rope_t(dkr, cs_ref[...], sn_ref[...], MLA_ROPE // 2)

    return pl.pallas_call(
        body, name=name, grid=(L // tr,),
        in_specs=[pl.BlockSpec((tr, 768), lambda i: (i, C_CQ // 768)),
                  pl.BlockSpec((tr, MLA_Q_LORA), lambda i: (i, 0)), pl.BlockSpec((tr, MLA_KV_LORA), lambda i: (i, 0)),
                  pl.BlockSpec((tr, 128), lambda i: (i, 0)),
                  pl.BlockSpec((1, MLA_Q_LORA), lambda i: (0, 0)), pl.BlockSpec((1, MLA_KV_LORA), lambda i: (0, 0)),
                  pl.BlockSpec((tr, 128), lambda i: (i, 0)), pl.BlockSpec((tr, 128), lambda i: (i, 0))],
        out_specs=(pl.BlockSpec((tr, 768), lambda i: (i, 0)), pl.BlockSpec((1, MLA_Q_LORA), lambda i: (0, 0)),
                   pl.BlockSpec((1, MLA_KV_LORA), lambda i: (0, 0))),
        out_shape=(jax.ShapeDtypeStruct((L, 768), F32), jax.ShapeDtypeStruct((1, MLA_Q_LORA), F32),
                   jax.ShapeDtypeStruct((1, MLA_KV_LORA), F32)),
        compiler_params=pltpu.CompilerParams(dimension_semantics=("arbitrary",)),
    )(P, dnq, dnkv, dkr, gq.reshape(1, -1), gkv.reshape(1, -1), cs, sn)


def _tri(kind):
    r = lax.broadcasted_iota(jnp.int32, (BLK, BLK), 0)
    c = lax.broadcasted_iota(jnp.int32, (BLK, BLK), 1)
    t = ((r > c) if kind == "right" else (r < c)).astype(BF16)
    return jnp.concatenate([t, t], axis=0)


def _tri_sum(x, tt):
    hi = x.astype(BF16)
    lo = (x - hi.astype(F32)).astype(BF16)
    return _dot(jnp.concatenate([hi, lo], axis=1), tt)


def _sb_tile(q, k, i, j, tt_right, R):
    row = lax.broadcasted_iota(jnp.int32, (BLK, BLK), 0)
    col = lax.broadcasted_iota(jnp.int32, (BLK, BLK), 1)
    s_idx = j * BLK + col
    mask = (s_idx < i * BLK + row) & (s_idx >= N_PAD)
    z = _dot_nt(q, k)
    sp = jnp.maximum(z, 0.0) + jnp.log1p(jnp.exp(-jnp.abs(z)))
    lk = jnp.where(mask, -sp, 0.0)
    E = _tri_sum(lk, tt_right) + R
    return mask, z, sp, lk, E


def _old_sb_fwd(P, *, name):
    L = P.shape[0]
    nb = L // BLK

    def body(q_ref, k_ref, v_ref, o_ref, c_ref):
        i = pl.program_id(1)
        tt = _tri("right")
        lane = lax.broadcasted_iota(jnp.int32, (BLK, 128), 1)
        qs = [(q_ref[:, 64 * h:64 * h + 64] * 0.125).astype(BF16) for h in range(2)]

        def step(jj, carry):
            j = i - jj
            off = pl.multiple_of(j * BLK, BLK)
            kb = k_ref[pl.ds(off, BLK), :].astype(BF16)
            vb = v_ref[pl.ds(off, BLK), :].astype(BF16)
            out = []
            for h in range(2):
                o, R = carry[2 * h], carry[2 * h + 1]
                sl = slice(64 * h, 64 * h + 64)
                mask, z, sp, lk, E = _sb_tile(qs[h], kb[:, sl], i, j, tt, R)
                w = jnp.where(mask, jnp.exp(z - sp + E), 0.0)
                c_ref[h] = jnp.where(lane == j, R, c_ref[h])
                out += [o + _dot(w.astype(BF16), vb[:, sl]), R + jnp.sum(lk, axis=1, keepdims=True)]
            return tuple(out)

        c_ref[...] = jnp.zeros_like(c_ref)
        z0 = (jnp.zeros((BLK, 64), F32), jnp.zeros((BLK, 1), F32))
        res = lax.fori_loop(0, i + 1, step, z0 + z0)
        o_ref[...] = jnp.concatenate([res[0], res[2]], axis=1)

    return pl.pallas_call(
        body, name=name, grid=(SB_HEADS // 2, nb),
        in_specs=[pl.BlockSpec((BLK, 128), lambda hp, i: (i, C_SBQ // 128 + hp)),
                  pl.BlockSpec((L, 128), lambda hp, i: (0, C_SBK // 128 + hp)),
                  pl.BlockSpec((L, 128), lambda hp, i: (0, C_SBV // 128 + hp))],
        out_specs=(pl.BlockSpec((BLK, 128), lambda hp, i: (i, hp)), pl.BlockSpec((2, BLK, 128), lambda hp, i: (hp, i, 0))),
        out_shape=(jax.ShapeDtypeStruct((L, 512), F32), jax.ShapeDtypeStruct((SB_HEADS, L, 128), F32)),
        compiler_params=pltpu.CompilerParams(dimension_semantics=("parallel", "arbitrary")),
    )(P, P, P)


def _old_sb_bwd(P, carries, dmixed, *, name):
    L = P.shape[0]
    nb = L // BLK

    def body(q_ref, k_ref, v_ref, c_ref, do_ref, dq_ref, dk_ref, dv_ref):
        i = pl.program_id(1)

        @pl.when(i == 0)
        def _():
            dk_ref[...] = jnp.zeros_like(dk_ref)
            dv_ref[...] = jnp.zeros_like(dv_ref)

        tr = _tri("right")
        tl = _tri("left")
        lane = lax.broadcasted_iota(jnp.int32, (BLK, 128), 1)
        qs = [(q_ref[:, 64 * h:64 * h + 64] * 0.125).astype(BF16) for h in range(2)]
        dos = [do_ref[:, 64 * h:64 * h + 64].astype(BF16) for h in range(2)]

        def step(j, carry):
            off = pl.multiple_of(j * BLK, BLK)
            kb = k_ref[pl.ds(off, BLK), :].astype(BF16)
            vb = v_ref[pl.ds(off, BLK), :].astype(BF16)
            out, dks, dvs = [], [], []
            for h in range(2):
                dq, PL = carry[2 * h], carry[2 * h + 1]
                sl = slice(64 * h, 64 * h + 64)
                R = jnp.sum(jnp.where(lane == j, c_ref[h], 0.0), axis=1, keepdims=True)
                mask, z, sp, lk, E = _sb_tile(qs[h], kb[:, sl], i, j, tr, R)
                sig = jnp.exp(z - sp)
                w = jnp.where(mask, sig * jnp.exp(E), 0.0)
                dA = _dot_nt(dos[h], vb[:, sl]) * w
                Pp = _tri_sum(dA, tl) + PL
                dz = jnp.where(mask, dA - sig * (dA + Pp), 0.0).astype(BF16)
                dks.append(_dot_tn(dz, qs[h]))
                dvs.append(_dot_tn(w.astype(BF16), dos[h]))
                out += [dq + _dot(dz, kb[:, sl]), PL + jnp.sum(dA, axis=1, keepdims=True)]
            dk_ref[pl.ds(off, BLK), :] += jnp.concatenate(dks, axis=1)
            dv_ref[pl.ds(off, BLK), :] += jnp.concatenate(dvs, axis=1)
            return tuple(out)

        z0 = (jnp.zeros((BLK, 64), F32), jnp.zeros((BLK, 1), F32))
        res = lax.fori_loop(0, i + 1, step, z0 + z0)
        dq_ref[...] = jnp.concatenate([res[0], res[2]], axis=1) * 0.125

    blk = lambda c0: pl.BlockSpec((BLK, 128), lambda hp, i: (i, c0 + hp))
    full = lambda c0: pl.BlockSpec((L, 128), lambda hp, i: (0, c0 + hp))
    sds = jax.ShapeDtypeStruct((L, 512), F32)
    return pl.pallas_call(
        body, name=name, grid=(SB_HEADS // 2, nb),
        in_specs=[blk(C_SBQ // 128), full(C_SBK // 128), full(C_SBV // 128),
                  pl.BlockSpec((2, BLK, 128), lambda hp, i: (hp, i, 0)), blk(0)],
        out_specs=(blk(0), full(0), full(0)), out_shape=(sds, sds, sds),
        compiler_params=pltpu.CompilerParams(dimension_semantics=("parallel", "arbitrary")),
    )(P, P, P, carries, dmixed)


def _mla_mask(i, j):
    row = lax.broadcasted_iota(jnp.int32, (BLK, BLK), 0)
    col = lax.broadcasted_iota(jnp.int32, (BLK, BLK), 1)
    t_idx = i * BLK + row
    s_idx = j * BLK + col
    return (s_idx <= t_idx) & ((s_idx >= N_PAD) | (s_idx == t_idx))


def _mla_q(q_ref, cs_ref, sn_ref):
    qr = _rope(q_ref[:, 128:256], cs_ref[...], sn_ref[...], MLA_ROPE // 2)
    qn = [q_ref[:, 64 * h:64 * h + 64].astype(BF16) for h in range(2)]
    qrs = [qr[:, 32 * h:32 * h + 32].astype(BF16) for h in range(2)]
    return qn, qrs


def _old_mla_fwd(Q, KV, KR, cs, sn, *, name):
    L = Q.shape[0]
    nb = L // BLK

    def body(q_ref, kn_ref, v_ref, kr_ref, cs_ref, sn_ref, o_ref, lse_ref):
        i = pl.program_id(1)
        qn, qrs = _mla_q(q_ref, cs_ref, sn_ref)

        def step(j, carry):
            off = pl.multiple_of(j * BLK, BLK)
            knb = kn_ref[pl.ds(off, BLK), :]
            vb = v_ref[pl.ds(off, BLK), :]
            krb = kr_ref[pl.ds(off, BLK), 0:MLA_ROPE]
            mask = _mla_mask(i, j)
            out = []
            for h in range(2):
                m, l, acc = carry[3 * h], carry[3 * h + 1], carry[3 * h + 2]
                sl = slice(64 * h, 64 * h + 64)
                s = (_dot_nt(qn[h], knb[:, sl]) + _dot_nt(qrs[h], krb)) * MLA_SCALE
                s = jnp.where(mask, s, NEG)
                m_new = jnp.maximum(m, jnp.max(s, axis=1, keepdims=True))
                a = jnp.exp(m - m_new)
                p = jnp.exp(s - m_new)
                out += [m_new, a * l + jnp.sum(p, axis=1, keepdims=True), a * acc + _dot(p.astype(BF16), vb[:, sl])]
            return tuple(out)

        z0 = (jnp.full((BLK, 1), NEG, F32), jnp.zeros((BLK, 1), F32), jnp.zeros((BLK, 64), F32))
        res = lax.fori_loop(0, i + 1, step, z0 + z0)
        o_ref[...] = jnp.concatenate([res[2] / res[1], res[5] / res[4]], axis=1)
        lane = lax.broadcasted_iota(jnp.int32, (BLK, 128), 1)
        lse0 = res[0] + jnp.log(res[1])
        lse1 = res[3] + jnp.log(res[4])
        lse_ref[0] = jnp.where(lane == 0, lse0, jnp.where(lane == 1, lse1, 0.0))

    return pl.pallas_call(
        body, name=name, grid=(MLA_HEADS // 2, nb),
        in_specs=[pl.BlockSpec((BLK, 256), lambda hp, i: (i, hp)),
                  pl.BlockSpec((L, 128), lambda hp, i: (0, hp)),
                  pl.BlockSpec((L, 128), lambda hp, i: (0, 4 + hp)),
                  pl.BlockSpec((L, 128), lambda hp, i: (0, 0)),
                  pl.BlockSpec((BLK, 128), lambda hp, i: (i, 0)), pl.BlockSpec((BLK, 128), lambda hp, i: (i, 0))],
        out_specs=(pl.BlockSpec((BLK, 128), lambda hp, i: (i, hp)), pl.BlockSpec((1, BLK, 128), lambda hp, i: (hp, i, 0))),
        out_shape=(jax.ShapeDtypeStruct((L, 512), F32), jax.ShapeDtypeStruct((4, L, 128), F32)),
        compiler_params=pltpu.CompilerParams(dimension_semantics=("parallel", "arbitrary")),
    )(Q, KV, KV, KR, cs, sn)


def _old_mla_bwd(Q, KV, KR, cs, sn, mixed, dmixed, lse, *, name):
    L = Q.shape[0]
    nb = L // BLK

    def body(q_ref, kn_ref, v_ref, kr_ref, cs_ref, sn_ref, o_ref, do_ref, lse_ref, dq_ref, dkn_ref, dv_ref, dkr_ref):
        hp = pl.program_id(0)
        i = pl.program_id(1)

        @pl.when(i == 0)
        def _():
            dkn_ref[...] = jnp.zeros_like(dkn_ref)
            dv_ref[...] = jnp.zeros_like(dv_ref)

        @pl.when((i == 0) & (hp == 0))
        def _():
            dkr_ref[...] = jnp.zeros_like(dkr_ref)

        qn, qrs = _mla_q(q_ref, cs_ref, sn_ref)
        dos, dd, lses = [], [], []
        for h in range(2):
            sl = slice(64 * h, 64 * h + 64)
            d = do_ref[:, sl]
            dos.append(d.astype(BF16))
            dd.append(jnp.sum(d * o_ref[:, sl], axis=1, keepdims=True))
            lses.append(lse_ref[0, :, h:h + 1])

        def step(j, carry):
            off = pl.multiple_of(j * BLK, BLK)
            knb = kn_ref[pl.ds(off, BLK), :]
            vb = v_ref[pl.ds(off, BLK), :]
            krb = kr_ref[pl.ds(off, BLK), 0:MLA_ROPE]
            mask = _mla_mask(i, j)
            out, dkns, dvs = [], [], []
            dkr = jnp.zeros((BLK, MLA_ROPE), F32)
            for h in range(2):
                dqn, dqr = carry[2 * h], carry[2 * h + 1]
                sl = slice(64 * h, 64 * h + 64)
                s = (_dot_nt(qn[h], knb[:, sl]) + _dot_nt(qrs[h], krb)) * MLA_SCALE
                p = jnp.where(mask, jnp.exp(s - lses[h]), 0.0)
                dp = _dot_nt(dos[h], vb[:, sl])
                ds = (p * (dp - dd[h]) * MLA_SCALE).astype(BF16)
                dkns.append(_dot_tn(ds, qn[h]))
                dvs.append(_dot_tn(p.astype(BF16), dos[h]))
                dkr = dkr + _dot_tn(ds, qrs[h])
                out += [dqn + _dot(ds, knb[:, sl]), dqr + _dot(ds, krb)]
            dkn_ref[pl.ds(off, BLK), :] += jnp.concatenate(dkns, axis=1)
            dv_ref[pl.ds(off, BLK), :] += jnp.concatenate(dvs, axis=1)
            dkr_ref[pl.ds(off, BLK), :] += jnp.concatenate([dkr, jnp.zeros((BLK, 128 - MLA_ROPE), F32)], axis=1)
            return tuple(out)

        z0 = (jnp.zeros((BLK, 64), F32), jnp.zeros((BLK, MLA_ROPE), F32))
        res = lax.fori_loop(0, i + 1, step, z0 + z0)
        dqr = jnp.concatenate([res[1], res[3], jnp.zeros((BLK, 64), F32)], axis=1)
        dq_ref[...] = jnp.concatenate([res[0], res[2], _rope_t(dqr, cs_ref[...], sn_ref[...], MLA_ROPE // 2)], axis=1)

    blk = lambda c0: pl.BlockSpec((BLK, 128), lambda hp, i: (i, c0 + hp))
    full = lambda c0: pl.BlockSpec((L, 128), lambda hp, i: (0, c0 + hp))
    tab = pl.BlockSpec((BLK, 128), lambda hp, i: (i, 0))
    return pl.pallas_call(
        body, name=name, grid=(MLA_HEADS // 2, nb),
        in_specs=[pl.BlockSpec((BLK, 256), lambda hp, i: (i, hp)), full(0), full(4),
                  pl.BlockSpec((L, 128), lambda hp, i: (0, 0)), tab, tab, blk(4), blk(4),
                  pl.BlockSpec((1, BLK, 128), lambda hp, i: (hp, i, 0))],
        out_specs=(pl.BlockSpec((BLK, 256), lambda hp, i: (i, hp)), full(0), full(0),
                   pl.BlockSpec((L, 128), lambda hp, i: (0, 0))),
        out_shape=(jax.ShapeDtypeStruct((L, 1024), F32), jax.ShapeDtypeStruct((L, 512), F32),
                   jax.ShapeDtypeStruct((L, 512), F32), jax.ShapeDtypeStruct((L, 128), F32)),
        compiler_params=pltpu.CompilerParams(dimension_semantics=("arbitrary", "arbitrary")),
    )(Q, KV, KV, KR, cs, sn, mixed, dmixed, lse)


SB_UNROLL = 2


def _tq(L):
    return 384 if L % 384 == 0 else BLK


def _softplus(z):
    na = lax.bitcast_convert_type(lax.bitcast_convert_type(z, jnp.uint32) | jnp.uint32(0x80000000), F32)
    return jnp.maximum(z, 0.0) + jnp.log(1.0 + jnp.exp(na))


def _head_split(x, first):
    zero = jnp.zeros_like(x)
    return jnp.where(first, x, zero), jnp.where(first, zero, x)


def _sb_mask(I, j, tq):
    row = lax.broadcasted_iota(jnp.int32, (tq, BLK), 0)
    col = lax.broadcasted_iota(jnp.int32, (tq, BLK), 1)
    s_idx = j * BLK + col
    return (s_idx < I * tq + row) & (s_idx >= N_PAD)


def _tri2(kind, splits):
    r = lax.broadcasted_iota(jnp.int32, (256, 256), 0)
    c = lax.broadcasted_iota(jnp.int32, (256, 256), 1)
    same = (r < BLK) == (c < BLK)
    t = (same & ((r > c) if kind == "right" else (r < c))).astype(BF16)
    return jnp.concatenate([t] * splits, axis=0)


def _split2(x):
    hi = x.astype(BF16)
    lo = (x - hi.astype(F32)).astype(BF16)
    return jnp.concatenate([hi, lo], axis=1)


def _sb_mask2(I, j, tq):
    row = lax.broadcasted_iota(jnp.int32, (tq, 256), 0)
    col = lax.broadcasted_iota(jnp.int32, (tq, 256), 1)
    s_idx = j * BLK + (col & (BLK - 1))
    return (s_idx < I * tq + row) & (s_idx >= N_PAD)


def _per_head(x, r0, r1):
    return jnp.concatenate([x[:, 0:BLK] + r0, x[:, BLK:2 * BLK] + r1], axis=1)


def sb_fwd(P, *, name, gathers=()):
    L = P.shape[0]
    tq = _tq(L)
    nd = tq // BLK
    ng = len(gathers)
    nhp, nI = SB_HEADS // 2, L // tq

    def body(q_ref, k_ref, v_ref, *rest):
        w_refs, (o_ref, c_ref), g_refs, sems = rest[:ng], rest[ng:ng + 2], rest[ng + 2:2 * ng + 2], rest[2 * ng + 2:]
        hp = pl.program_id(0)
        I = pl.program_id(1)
        _gather_ride(w_refs, g_refs, sems, (hp == 0) & (I == 0), (hp == nhp - 1) & (I == 0), (hp == nhp - 1) & (I == nI - 1))
        tt = _tri2("right", 2)
        lane_q = lax.broadcasted_iota(jnp.int32, (tq, 128), 1)
        first_k = lax.broadcasted_iota(jnp.int32, (BLK, 128), 1) < 64
        q = (q_ref[...] * 0.125).astype(BF16)
        c_ref[...] = jnp.zeros_like(c_ref)

        def tiles(T, carry, kind):
            o, R0, R1 = carry
            js = [T * nd + nd - 1 - u for u in range(nd)]
            st = []
            for j in js:
                off = pl.multiple_of(j * BLK, BLK)
                kcat = jnp.concatenate(_head_split(k_ref[pl.ds(off, BLK), :].astype(BF16), first_k), axis=0)
                st.append([_dot_nt(q, kcat), off])
            for u, (s, j) in enumerate(zip(st, js)):
                sp = _softplus(s[0])
                mask = _sb_mask2(I, j, tq) if kind == "diag" else (pad_ok if kind == "first" and u == nd - 1 else None)
                spm = sp if mask is None else jnp.where(mask, sp, 0.0)
                s += [sp, spm, mask, _dot(_split2(spm), tt)]
            for (z, off, sp, spm, mask, S), j in zip(st, js):
                vcat = jnp.concatenate(_head_split(v_ref[pl.ds(off, BLK), :].astype(BF16), first_k), axis=0)
                w = jnp.exp(_per_head(z - sp - S, R0, R1))
                if mask is not None:
                    w = jnp.where(mask, w, 0.0)
                c_ref[0] = jnp.where(lane_q == j, R0, c_ref[0])
                c_ref[1] = jnp.where(lane_q == j, R1, c_ref[1])
                o = o + _dot(w.astype(BF16), vcat)
                R0 = R0 - (S[:, 0:1] + spm[:, 0:1])
                R1 = R1 - (S[:, BLK:BLK + 1] + spm[:, BLK:BLK + 1])
            return (o, R0, R1)

        pad_ok = (lax.broadcasted_iota(jnp.int32, (tq, 256), 1) & (BLK - 1)) >= N_PAD
        carry = (jnp.zeros((tq, 128), F32), jnp.zeros((tq, 1), F32), jnp.zeros((tq, 1), F32))
        carry = tiles(I, carry, "diag")
        carry = lax.fori_loop(0, jnp.maximum(I - 1, 0), lambda t, c: tiles(I - 1 - t, c, None), carry)
        carry = lax.fori_loop(0, jnp.minimum(I, 1), lambda t, c: tiles(0, c, "first"), carry)
        o_ref[...] = carry[0]

    g_in, g_out, g_shapes, g_sems = _gather_io(gathers)
    return pl.pallas_call(
        body, name=name, grid=(nhp, nI),
        in_specs=[pl.BlockSpec((tq, 128), lambda hp, i: (i, C_SBQ // 128 + hp)),
                  pl.BlockSpec((L, 128), lambda hp, i: (0, C_SBK // 128 + hp)),
                  pl.BlockSpec((L, 128), lambda hp, i: (0, C_SBV // 128 + hp))] + g_in,
        out_specs=[pl.BlockSpec((tq, 128), lambda hp, i: (i, hp)), pl.BlockSpec((2, tq, 128), lambda hp, i: (hp, i, 0))] + g_out,
        out_shape=[jax.ShapeDtypeStruct((L, 1536), F32), jax.ShapeDtypeStruct((SB_HEADS, L, 128), F32)] + g_shapes,
        scratch_shapes=g_sems,
        compiler_params=pltpu.CompilerParams(dimension_semantics=("arbitrary", "arbitrary")),
    )(P, P, P, *gathers)


def sb_bwd(P, carries, dmixed, *, name):
    L = P.shape[0]
    tq = _tq(L)
    nd = tq // BLK

    def body(q_ref, k_ref, v_ref, c_ref, do_ref, dq_ref, dk_ref, dv_ref):
        I = pl.program_id(1)

        @pl.when(I == 0)
        def _():
            dk_ref[...] = jnp.zeros_like(dk_ref)
            dv_ref[...] = jnp.zeros_like(dv_ref)

        tr = _tri2("right", 2)
        tl = _tri2("left", 1)
        lane_q = lax.broadcasted_iota(jnp.int32, (tq, 128), 1)
        first_k = lax.broadcasted_iota(jnp.int32, (BLK, 128), 1) < 64
        q = (q_ref[...] * 0.125).astype(BF16)
        do = do_ref[...].astype(BF16)

        def tiles(T, carry, kind):
            dq, PL0, PL1 = carry
            js = [T * nd + u for u in range(nd)]
            st = []
            for j in js:
                off = pl.multiple_of(j * BLK, BLK)
                kcat = jnp.concatenate(_head_split(k_ref[pl.ds(off, BLK), :].astype(BF16), first_k), axis=0)
                vcat = jnp.concatenate(_head_split(v_ref[pl.ds(off, BLK), :].astype(BF16), first_k), axis=0)
                st.append([off, kcat, _dot_nt(q, kcat), _dot_nt(do, vcat)])
            for u, (s, j) in enumerate(zip(st, js)):
                z = s[2]
                sp = _softplus(z)
                mask = _sb_mask2(I, j, tq) if kind == "diag" else (pad_ok if kind == "first" and u == 0 else None)
                spm = sp if mask is None else jnp.where(mask, sp, 0.0)
                s += [mask, jnp.exp(z - sp), _dot(_split2(spm), tr)]
            for s, j in zip(st, js):
                off, kcat, z, dw, mask, sig, S = s
                R0 = jnp.sum(jnp.where(lane_q == j, c_ref[0], 0.0), axis=1, keepdims=True)
                R1 = jnp.sum(jnp.where(lane_q == j, c_ref[1], 0.0), axis=1, keepdims=True)
                w = sig * jnp.exp(_per_head(-S, R0, R1))
                if mask is not None:
                    w = jnp.where(mask, w, 0.0)
                dA = dw * w
                dvf = _dot_tn(w.astype(BF16), do)
                dv_ref[pl.ds(off, BLK), :] += jnp.where(first_k, dvf[0:BLK], dvf[BLK:2 * BLK])
                s += [dA, _dot(dA.astype(BF16), tl)]
            for off, kcat, z, dw, mask, sig, S, dA, pre in st:
                dz = dA - sig * (dA + _per_head(pre, PL0, PL1))
                if mask is not None:
                    dz = jnp.where(mask, dz, 0.0)
                dzb = dz.astype(BF16)
                dkf = _dot_tn(dzb, q)
                dk_ref[pl.ds(off, BLK), :] += jnp.where(first_k, dkf[0:BLK], dkf[BLK:2 * BLK])
                dq = dq + _dot(dzb, kcat)
                PL0 = PL0 + (pre[:, BLK - 1:BLK] + dA[:, BLK - 1:BLK])
                PL1 = PL1 + (pre[:, 2 * BLK - 1:2 * BLK] + dA[:, 2 * BLK - 1:2 * BLK])
            return (dq, PL0, PL1)

        pad_ok = (lax.broadcasted_iota(jnp.int32, (tq, 256), 1) & (BLK - 1)) >= N_PAD
        carry = (jnp.zeros((tq, 128), F32), jnp.zeros((tq, 1), F32), jnp.zeros((tq, 1), F32))
        carry = lax.fori_loop(0, jnp.minimum(I, 1), lambda t, c: tiles(0, c, "first"), carry)
        carry = lax.fori_loop(1, jnp.maximum(I, 1), lambda T, c: tiles(T, c, None), carry)
        carry = tiles(I, carry, "diag")
        dq_ref[...] = carry[0] * 0.125

    blk = lambda c0: pl.BlockSpec((tq, 128), lambda hp, i: (i, c0 + hp))
    full = lambda c0: pl.BlockSpec((L, 128), lambda hp, i: (0, c0 + hp))
    sds = jax.ShapeDtypeStruct((L, 512), F32)
    return pl.pallas_call(
        body, name=name, grid=(SB_HEADS // 2, L // tq),
        in_specs=[blk(C_SBQ // 128), full(C_SBK // 128), full(C_SBV // 128),
                  pl.BlockSpec((2, tq, 128), lambda hp, i: (hp, i, 0)), blk(0)],
        out_specs=(blk(0), full(0), full(0)), out_shape=(sds, sds, sds),
        compiler_params=pltpu.CompilerParams(dimension_semantics=("parallel", "arbitrary")),
    )(P, P, P, carries, dmixed)


def _v2_sb_fwd(P, *, name):
    L = P.shape[0]
    tq = _tq(L)
    nd = tq // BLK

    def body(q_ref, k_ref, v_ref, o_ref, c_ref):
        I = pl.program_id(1)
        tt = _tri("right")
        lane_q = lax.broadcasted_iota(jnp.int32, (tq, 128), 1)
        first_k = lax.broadcasted_iota(jnp.int32, (BLK, 128), 1) < 64
        qm = [x.astype(BF16) for x in _head_split(q_ref[...] * 0.125, lane_q < 64)]
        c_ref[...] = jnp.zeros_like(c_ref)

        def tile(j, carry, masked):
            o, R = carry[0], carry[1:]
            off = pl.multiple_of(j * BLK, BLK)
            kb = k_ref[pl.ds(off, BLK), :].astype(BF16)
            vcat = jnp.concatenate(_head_split(v_ref[pl.ds(off, BLK), :].astype(BF16), first_k), axis=0)
            mask = _sb_mask(I, j, tq) if masked else None
            ws, Rn = [], []
            for h in range(2):
                z = _dot_nt(qm[h], kb)
                sp = _softplus(z)
                spm = jnp.where(mask, sp, 0.0) if masked else sp
                w = jnp.exp(z - sp - _tri_sum(spm, tt) + R[h])
                if masked:
                    w = jnp.where(mask, w, 0.0)
                c_ref[h] = jnp.where(lane_q == j, R[h], c_ref[h])
                ws.append(w.astype(BF16))
                Rn.append(R[h] - jnp.sum(spm, axis=1, keepdims=True))
            return (o + _dot(jnp.concatenate(ws, axis=1), vcat), Rn[0], Rn[1])

        carry = (jnp.zeros((tq, 128), F32), jnp.zeros((tq, 1), F32), jnp.zeros((tq, 1), F32))
        carry = lax.fori_loop(0, nd, lambda t, c: tile(I * nd + nd - 1 - t, c, True), carry)
        carry = lax.fori_loop(0, jnp.maximum(I * nd - 1, 0), lambda t, c: tile(I * nd - 1 - t, c, False), carry)
        carry = lax.fori_loop(0, jnp.minimum(I, 1), lambda t, c: tile(0, c, True), carry)
        o_ref[...] = carry[0]

    return pl.pallas_call(
        body, name=name, grid=(SB_HEADS // 2, L // tq),
        in_specs=[pl.BlockSpec((tq, 128), lambda hp, i: (i, C_SBQ // 128 + hp)),
                  pl.BlockSpec((L, 128), lambda hp, i: (0, C_SBK // 128 + hp)),
                  pl.BlockSpec((L, 128), lambda hp, i: (0, C_SBV // 128 + hp))],
        out_specs=(pl.BlockSpec((tq, 128), lambda hp, i: (i, hp)), pl.BlockSpec((2, tq, 128), lambda hp, i: (hp, i, 0))),
        out_shape=(jax.ShapeDtypeStruct((L, 512), F32), jax.ShapeDtypeStruct((SB_HEADS, L, 128), F32)),
        compiler_params=pltpu.CompilerParams(dimension_semantics=("parallel", "arbitrary")),
    )(P, P, P)


def _v2_sb_bwd(P, carries, dmixed, *, name):
    L = P.shape[0]
    tq = _tq(L)
    nd = tq // BLK

    def body(q_ref, k_ref, v_ref, c_ref, do_ref, dq_ref, dk_ref, dv_ref):
        I = pl.program_id(1)

        @pl.when(I == 0)
        def _():
            dk_ref[...] = jnp.zeros_like(dk_ref)
            dv_ref[...] = jnp.zeros_like(dv_ref)

        tr = _tri("right")
        tl = _tri("left")
        lane_q = lax.broadcasted_iota(jnp.int32, (tq, 128), 1)
        first_k = lax.broadcasted_iota(jnp.int32, (BLK, 128), 1) < 64
        qm = [x.astype(BF16) for x in _head_split(q_ref[...] * 0.125, lane_q < 64)]
        dom = [x.astype(BF16) for x in _head_split(do_ref[...], lane_q < 64)]
        qcat = jnp.concatenate(qm, axis=0)
        docat = jnp.concatenate(dom, axis=0)

        def tile(j, carry, masked):
            dq, PL = carry[0], carry[1:]
            off = pl.multiple_of(j * BLK, BLK)
            kb = k_ref[pl.ds(off, BLK), :].astype(BF16)
            vb = v_ref[pl.ds(off, BLK), :].astype(BF16)
            kcat = jnp.concatenate(_head_split(kb, first_k), axis=0)
            mask = _sb_mask(I, j, tq) if masked else None
            dzs, wsb, PLn = [], [], []
            for h in range(2):
                R = jnp.sum(jnp.where(lane_q == j, c_ref[h], 0.0), axis=1, keepdims=True)
                z = _dot_nt(qm[h], kb)
                sp = _softplus(z)
                spm = jnp.where(mask, sp, 0.0) if masked else sp
                sig = jnp.exp(z - sp)
                w = sig * jnp.exp(R - _tri_sum(spm, tr))
                if masked:
                    w = jnp.where(mask, w, 0.0)
                dA = _dot_nt(dom[h], vb) * w
                dz = dA - sig * (dA + _tri_sum(dA, tl) + PL[h])
                if masked:
                    dz = jnp.where(mask, dz, 0.0)
                dzs.append(dz.astype(BF16))
                wsb.append(w.astype(BF16))
                PLn.append(PL[h] + jnp.sum(dA, axis=1, keepdims=True))
            dk_ref[pl.ds(off, BLK), :] += _dot_tn(jnp.concatenate(dzs, axis=0), qcat)
            dv_ref[pl.ds(off, BLK), :] += _dot_tn(jnp.concatenate(wsb, axis=0), docat)
            return (dq + _dot(jnp.concatenate(dzs, axis=1), kcat), PLn[0], PLn[1])

        carry = (jnp.zeros((tq, 128), F32), jnp.zeros((tq, 1), F32), jnp.zeros((tq, 1), F32))
        carry = lax.fori_loop(0, jnp.minimum(I, 1), lambda t, c: tile(0, c, True), carry)
        carry = lax.fori_loop(1, jnp.maximum(I * nd, 1), lambda j, c: tile(j, c, False), carry)
        carry = lax.fori_loop(0, nd, lambda t, c: tile(I * nd + t, c, True), carry)
        dq_ref[...] = carry[0] * 0.125

    blk = lambda c0: pl.BlockSpec((tq, 128), lambda hp, i: (i, c0 + hp))
    full = lambda c0: pl.BlockSpec((L, 128), lambda hp, i: (0, c0 + hp))
    sds = jax.ShapeDtypeStruct((L, 512), F32)
    return pl.pallas_call(
        body, name=name, grid=(SB_HEADS // 2, L // tq),
        in_specs=[blk(C_SBQ // 128), full(C_SBK // 128), full(C_SBV // 128),
                  pl.BlockSpec((2, tq, 128), lambda hp, i: (hp, i, 0)), blk(0)],
        out_specs=(blk(0), full(0), full(0)), out_shape=(sds, sds, sds),
        compiler_params=pltpu.CompilerParams(dimension_semantics=("parallel", "arbitrary")),
    )(P, P, P, carries, dmixed)


def _mla_mask2(I, j, tq):
    row = lax.broadcasted_iota(jnp.int32, (tq, tq), 0)
    col = lax.broadcasted_iota(jnp.int32, (tq, tq), 1)
    t_idx = I * tq + row
    s_idx = j * tq + col
    return (s_idx <= t_idx) & ((s_idx >= N_PAD) | (s_idx == t_idx))


def _mla_qcat(q_ref, cs_ref, sn_ref, lane_q):
    qn = q_ref[:, 0:128]
    qr = _rope(q_ref[:, 128:256], cs_ref[...], sn_ref[...], MLA_ROPE // 2)
    zero = jnp.zeros_like(qn)
    r0 = lane_q < MLA_ROPE
    r1 = (lane_q >= MLA_ROPE) & (lane_q < 2 * MLA_ROPE)
    n0, n1 = _head_split(qn, lane_q < 64)
    return [jnp.concatenate([n0, jnp.where(r0, qr, zero)], axis=1).astype(BF16),
            jnp.concatenate([n1, jnp.where(r1, qr, zero)], axis=1).astype(BF16)]


def mla_fwd(Q, KV, KR, cs, sn, mixed, *, name, gathers=()):
    L = Q.shape[0]
    tq = _tq(L)

    def body(q_ref, kn_ref, v_ref, kr_ref, cs_ref, sn_ref, o_ref, lse_ref):
        I = pl.program_id(1)
        lane_q = lax.broadcasted_iota(jnp.int32, (tq, 128), 1)
        first_q = lane_q < 64
        qcat = _mla_qcat(q_ref, cs_ref, sn_ref, lane_q)

        def tile(j, carry, masked, wide=1):
            acc, ml = carry[0], carry[1:]
            off = pl.multiple_of(j * tq, tq)
            tk = wide * tq
            first_k = lax.broadcasted_iota(jnp.int32, (tk, 128), 1) < 64
            kcat = jnp.concatenate([kn_ref[pl.ds(off, tk), :], kr_ref[pl.ds(off, tk), :]], axis=1)
            vcat = jnp.concatenate(_head_split(v_ref[pl.ds(off, tk), :], first_k), axis=0)
            mask = _mla_mask2(I, j, tq) if masked else None
            ps, al, out = [], [], []
            for h in range(2):
                m, l = ml[2 * h], ml[2 * h + 1]
                s = _dot_nt(qcat[h], kcat) * MLA_SCALE
                if masked:
                    s = jnp.where(mask, s, NEG)
                m_new = jnp.maximum(m, jnp.max(s, axis=1, keepdims=True))
                a = jnp.exp(m - m_new)
                p = jnp.exp(s - m_new)
                ps.append(p.astype(BF16))
                al.append(a)
                out += [m_new, a * l + jnp.sum(p, axis=1, keepdims=True)]
            acc = acc * jnp.where(first_q, al[0], al[1]) + _dot(jnp.concatenate(ps, axis=1), vcat)
            return (acc,) + tuple(out)

        ml0 = (jnp.full((tq, 1), NEG, F32), jnp.zeros((tq, 1), F32))
        carry = (jnp.zeros((tq, 128), F32),) + ml0 + ml0
        carry = lax.fori_loop(0, jnp.minimum(I, 1), lambda t, c: tile(0, c, True), carry)
        n_in = jnp.maximum(I - 1, 0)
        carry = lax.fori_loop(0, n_in // 2, lambda t, c: tile(1 + 2 * t, c, False, 2), carry)
        carry = lax.fori_loop(0, n_in % 2, lambda t, c: tile(I - 1, c, False), carry)
        carry = tile(I, carry, True)
        acc, m0, l0, m1, l1 = carry
        o_ref[...] = acc / jnp.where(first_q, l0, l1)
        lse_ref[0] = jnp.where(lane_q == 0, m0 + jnp.log(l0), jnp.where(lane_q == 1, m1 + jnp.log(l1), 0.0))

    ng = len(gathers)
    nhp, nI = MLA_HEADS // 2, L // tq

    def kern(q, kn, v, kr, c, s, mixed_any, *rest):
        w_refs, (o, lse), g_refs, sems = rest[:ng], rest[ng:ng + 2], rest[ng + 2:2 * ng + 2], rest[2 * ng + 2:]
        hp, I = pl.program_id(0), pl.program_id(1)
        _gather_ride(w_refs, g_refs, sems, (hp == 0) & (I == 0), (hp == nhp - 1) & (I == 0), (hp == nhp - 1) & (I == nI - 1))
        body(q, kn, v, kr, c, s, o, lse)

    g_in, g_out, g_shapes, g_sems = _gather_io(gathers)
    return pl.pallas_call(
        kern, name=name, grid=(nhp, nI),
        in_specs=[pl.BlockSpec((tq, 256), lambda hp, i: (i, hp)),
                  pl.BlockSpec((L, 128), lambda hp, i: (0, hp)),
                  pl.BlockSpec((L, 128), lambda hp, i: (0, 4 + hp)),
                  pl.BlockSpec((L, 128), lambda hp, i: (0, 0)),
                  pl.BlockSpec((tq, 128), lambda hp, i: (i, 0)), pl.BlockSpec((tq, 128), lambda hp, i: (i, 0)),
                  pl.BlockSpec(memory_space=pl.ANY)] + g_in,
        out_specs=[pl.BlockSpec((tq, 128), lambda hp, i: (i, 4 + hp)), pl.BlockSpec((1, tq, 128), lambda hp, i: (hp, i, 0))] + g_out,
        out_shape=[jax.ShapeDtypeStruct(mixed.shape, F32), jax.ShapeDtypeStruct((4, L, 128), F32)] + g_shapes,
        input_output_aliases={6: 0}, scratch_shapes=g_sems,
        compiler_params=pltpu.CompilerParams(dimension_semantics=("arbitrary", "arbitrary")),
    )(Q, KV, KV, KR, cs, sn, mixed, *gathers)


def mla_bwd(Q, KV, KR, cs, sn, mixed, dmixed, lse, *, name):
    L = Q.shape[0]
    tq = _tq(L)

    def body(q_ref, kn_ref, v_ref, kr_ref, cs_ref, sn_ref, o_ref, do_ref, lse_ref, dq_ref, dkn_ref, dv_ref, dkr_ref):
        hp = pl.program_id(0)
        I = pl.program_id(1)

        @pl.when(I == 0)
        def _():
            dkn_ref[...] = jnp.zeros_like(dkn_ref)
            dv_ref[...] = jnp.zeros_like(dv_ref)

        @pl.when((I == 0) & (hp == 0))
        def _():
            dkr_ref[...] = jnp.zeros_like(dkr_ref)

        lane_q = lax.broadcasted_iota(jnp.int32, (tq, 128), 1)
        first_q = lane_q < 64
        qcat = _mla_qcat(q_ref, cs_ref, sn_ref, lane_q)
        qq = jnp.concatenate(qcat, axis=0)
        do = do_ref[...]
        prod = do * o_ref[...]
        dd = [jnp.sum(jnp.where(first_q, prod, 0.0), axis=1, keepdims=True),
              jnp.sum(jnp.where(first_q, 0.0, prod), axis=1, keepdims=True)]
        dom = [x.astype(BF16) for x in _head_split(do, first_q)]
        docat = jnp.concatenate(dom, axis=0)
        lses = [lse_ref[0, :, 0:1], lse_ref[0, :, 1:2]]

        def tile(j, dq, masked, wide=1):
            off = pl.multiple_of(j * tq, tq)
            tk = wide * tq
            lane_k = lax.broadcasted_iota(jnp.int32, (tk, 256), 1)
            sel0 = (lane_k < 64) | ((lane_k >= 128) & (lane_k < 128 + MLA_ROPE))
            sel1 = ((lane_k >= 64) & (lane_k < 128)) | ((lane_k >= 128 + MLA_ROPE) & (lane_k < 128 + 2 * MLA_ROPE))
            kcat = jnp.concatenate([kn_ref[pl.ds(off, tk), :], kr_ref[pl.ds(off, tk), :]], axis=1)
            vb = v_ref[pl.ds(off, tk), :]
            zero = jnp.zeros_like(kcat)
            kk = jnp.concatenate([jnp.where(sel0, kcat, zero), jnp.where(sel1, kcat, zero)], axis=0)
            mask = _mla_mask2(I, j, tq) if masked else None
            dss, pbs = [], []
            for h in range(2):
                s = _dot_nt(qcat[h], kcat) * MLA_SCALE
                p = jnp.exp(s - lses[h])
                if masked:
                    p = jnp.where(mask, p, 0.0)
                dp = _dot_nt(dom[h], vb)
                dss.append((p * (dp - dd[h]) * MLA_SCALE).astype(BF16))
                pbs.append(p.astype(BF16))
            dkc = _dot_tn(jnp.concatenate(dss, axis=0), qq)
            dkn_ref[pl.ds(off, tk), :] += dkc[:, 0:128]
            dkr_ref[pl.ds(off, tk), :] += dkc[:, 128:256]
            dv_ref[pl.ds(off, tk), :] += _dot_tn(jnp.concatenate(pbs, axis=0), docat)
            return dq + _dot(jnp.concatenate(dss, axis=1), kk)

        dq = jnp.zeros((tq, 256), F32)
        dq = lax.fori_loop(0, jnp.minimum(I, 1), lambda t, c: tile(0, c, True), dq)
        n_in = jnp.maximum(I - 1, 0)
        dq = lax.fori_loop(0, n_in // 2, lambda t, c: tile(1 + 2 * t, c, False, 2), dq)
        dq = lax.fori_loop(0, n_in % 2, lambda t, c: tile(I - 1, c, False), dq)
        dq = tile(I, dq, True)
        dq_ref[:, 0:128] = dq[:, 0:128]
        dq_ref[:, 128:256] = _rope_t(dq[:, 128:256], cs_ref[...], sn_ref[...], MLA_ROPE // 2)

    blk = lambda c0: pl.BlockSpec((tq, 128), lambda hp, i: (i, c0 + hp))
    full = lambda c0: pl.BlockSpec((L, 128), lambda hp, i: (0, c0 + hp))
    tab = pl.BlockSpec((tq, 128), lambda hp, i: (i, 0))
    return pl.pallas_call(
        body, name=name, grid=(MLA_HEADS // 2, L // tq),
        in_specs=[pl.BlockSpec((tq, 256), lambda hp, i: (i, hp)), full(0), full(4),
                  pl.BlockSpec((L, 128), lambda hp, i: (0, 0)), tab, tab, blk(4), blk(4),
                  pl.BlockSpec((1, tq, 128), lambda hp, i: (hp, i, 0))],
        out_specs=(pl.BlockSpec((tq, 256), lambda hp, i: (i, hp)), full(0), full(0),
                   pl.BlockSpec((L, 128), lambda hp, i: (0, 0))),
        out_shape=(jax.ShapeDtypeStruct((L, 1024), F32), jax.ShapeDtypeStruct((L, 512), F32),
                   jax.ShapeDtypeStruct((L, 512), F32), jax.ShapeDtypeStruct((L, 128), F32)),
        compiler_params=pltpu.CompilerParams(dimension_semantics=("arbitrary", "arbitrary")),
    )(Q, KV, KV, KR, cs, sn, mixed, dmixed, lse)


def _ret_decay(h):
    lg = RET_LOG_G[h]
    r = lax.broadcasted_iota(jnp.int32, (BLK, BLK), 0)
    c = lax.broadcasted_iota(jnp.int32, (BLK, BLK), 1)
    diff = (r - c).astype(F32)
    d_in = jnp.where(diff >= 0, jnp.exp(jnp.maximum(diff, 0.0) * lg), 0.0)
    idx = lax.broadcasted_iota(jnp.int32, (BLK, 1), 0).astype(F32)
    q_decay = jnp.exp((idx + 1.0) * lg)
    k_decay = jnp.exp((BLK - 1.0 - idx) * lg)
    c_decay = math.exp(BLK * lg)
    return d_in, q_decay, k_decay, c_decay


def _ret_qk(qk_ref, cs_ref, sn_ref, n):
    cs = jnp.concatenate([cs_ref[...]] * 2, axis=1)
    sn = jnp.concatenate([sn_ref[...]] * 2, axis=1)
    rq = _rope(qk_ref[:, 0:256], cs, sn, RET_QK // 2)
    row = n * BLK + lax.broadcasted_iota(jnp.int32, (BLK, 256), 0)
    kmul = jnp.where(row >= N_PAD, 0.125, 0.0)
    rk = _rope(qk_ref[:, 256:512], cs, sn, RET_QK // 2) * kmul
    return rq, rk, cs, sn, kmul


def _head_norm(y):
    mu = jnp.mean(y, axis=-1, keepdims=True)
    yc = y - mu
    r = lax.rsqrt(jnp.mean(jnp.square(yc), axis=-1, keepdims=True) + LN_EPS)
    return yc * r, r


def ret_fwd(P, cs, sn, mixed, *, name):
    L = P.shape[0]
    nb = L // BLK

    def body(qk_ref, v_ref, g_ref, cs_ref, sn_ref, o_ref, y_ref, st_ref, state):
        n = pl.program_id(0)

        @pl.when(n == 0)
        def _():
            state[...] = jnp.zeros_like(state)

        st_ref[0] = state[...]
        rq, rk, _, _, _ = _ret_qk(qk_ref, cs_ref, sn_ref, n)
        outs, ys = [], []
        for h in range(RET_HEADS):
            d_in, q_decay, k_decay, c_decay = _ret_decay(h)
            q = rq[:, 64 * h:64 * h + 64].astype(BF16)
            kf = rk[:, 64 * h:64 * h + 64]
            v = v_ref[:, 128 * h:128 * h + 128].astype(BF16)
            S = state[h]
            inner = _dot_nt(q, kf.astype(BF16)) * d_in
            y = _dot(inner.astype(BF16), v) + _dot(q, S.astype(BF16)) * q_decay
            state[h] = S * c_decay + _dot_tn((kf * k_decay).astype(BF16), v)
            g = g_ref[:, 128 * h:128 * h + 128]
            ys.append(y)
            outs.append(g * jax.nn.sigmoid(g) * _head_norm(y)[0])
        o_ref[...] = jnp.concatenate(outs, axis=1)
        y_ref[...] = jnp.concatenate(ys, axis=1)

    blk512 = lambda c: pl.BlockSpec((BLK, 512), lambda n: (n, c))
    tab = pl.BlockSpec((BLK, 128), lambda n: (n, 0))
    return pl.pallas_call(
        lambda qk, v, g, c, s, mixed_any, o, y, st, state: body(qk, v, g, c, s, o, y, st, state),
        name=name, grid=(nb,),
        in_specs=[blk512(C_RQ // 512), blk512(C_RV // 512), blk512(C_RG // 512), tab, tab, pl.BlockSpec(memory_space=pl.ANY)],
        out_specs=(blk512(2), blk512(0), pl.BlockSpec((1, RET_HEADS, RET_QK, RET_V), lambda n: (n, 0, 0, 0))),
        out_shape=(jax.ShapeDtypeStruct(mixed.shape, F32), jax.ShapeDtypeStruct((L, 512), F32),
                   jax.ShapeDtypeStruct((nb, RET_HEADS, RET_QK, RET_V), F32)),
        input_output_aliases={5: 0},
        scratch_shapes=[pltpu.VMEM((RET_HEADS, RET_QK, RET_V), F32)],
        compiler_params=pltpu.CompilerParams(dimension_semantics=("arbitrary",)),
    )(P, P, P, cs, sn, mixed)


def ret_bwd(P, y, states, dmixed, cs, sn, *, name):
    L = P.shape[0]
    nb = L // BLK

    def body(qk_ref, v_ref, g_ref, y_ref, st_ref, do_ref, cs_ref, sn_ref, dqk_ref, dv_ref, dg_ref, dstate):
        n = nb - 1 - pl.program_id(0)

        @pl.when(pl.program_id(0) == 0)
        def _():
            dstate[...] = jnp.zeros_like(dstate)

        rq, rk, cs, sn, kmul = _ret_qk(qk_ref, cs_ref, sn_ref, n)
        dqs, dks, dvs, dgs = [], [], [], []
        for h in range(RET_HEADS):
            d_in, q_decay, k_decay, c_decay = _ret_decay(h)
            sv = slice(128 * h, 128 * h + 128)
            q = rq[:, 64 * h:64 * h + 64].astype(BF16)
            kf = rk[:, 64 * h:64 * h + 64]
            k = kf.astype(BF16)
            kd = (kf * k_decay).astype(BF16)
            v = v_ref[:, sv].astype(BF16)
            g = g_ref[:, sv]
            do = do_ref[:, sv]
            yh = y_ref[:, sv]
            S = st_ref[0, h].astype(BF16)
            dS = dstate[h]
            sg = jax.nn.sigmoid(g)
            yn, r = _head_norm(yh)
            dgs.append(do * yn * (sg * (1.0 + g * (1.0 - sg))))
            dyn = do * (g * sg)
            dy = r * (dyn - jnp.mean(dyn, axis=-1, keepdims=True) - yn * jnp.mean(dyn * yn, axis=-1, keepdims=True))
            dyb = dy.astype(BF16)
            dyq = (dy * q_decay).astype(BF16)
            inner = (_dot_nt(q, k) * d_in).astype(BF16)
            A = (_dot_nt(dyb, v) * d_in).astype(BF16)
            dSb = dS.astype(BF16)
            dqs.append(_dot(A, k) + _dot_nt(dyq, S))
            dks.append(_dot_tn(A, q) + _dot_nt(v, dSb) * k_decay)
            dvs.append(_dot_tn(inner, dyb) + _dot(kd, dSb))
            dstate[h] = dS * c_decay + _dot_tn(q, dyq)
        drq = _rope_t(jnp.concatenate(dqs, axis=1), cs, sn, RET_QK // 2)
        drk = _rope_t(jnp.concatenate(dks, axis=1) * kmul, cs, sn, RET_QK // 2)
        dqk_ref[...] = jnp.concatenate([drq, drk], axis=1)
        dv_ref[...] = jnp.concatenate(dvs, axis=1)
        dg_ref[...] = jnp.concatenate(dgs, axis=1)

    blk512 = lambda c: pl.BlockSpec((BLK, 512), lambda t: (nb - 1 - t, c))
    tab = pl.BlockSpec((BLK, 128), lambda t: (nb - 1 - t, 0))
    sds = jax.ShapeDtypeStruct((L, 512), F32)
    return pl.pallas_call(
        body, name=name, grid=(nb,),
        in_specs=[blk512(C_RQ // 512), blk512(C_RV // 512), blk512(C_RG // 512), blk512(0),
                  pl.BlockSpec((1, RET_HEADS, RET_QK, RET_V), lambda t: (nb - 1 - t, 0, 0, 0)), blk512(2), tab, tab],
        out_specs=(blk512(0), blk512(0), blk512(0)), out_shape=(sds, sds, sds),
        scratch_shapes=[pltpu.VMEM((RET_HEADS, RET_QK, RET_V), F32)],
        compiler_params=pltpu.CompilerParams(dimension_semantics=("arbitrary",)),
    )(P, P, P, y, states, dmixed, cs, sn)


def _perm_w_in(w):
    pad = jnp.zeros(w.shape[:-1] + (N_INP - N_IN,), w.dtype)
    return jnp.concatenate([w[..., 0:1536], w[..., 2208:3744], w[..., 1536:2208], pad], axis=-1)


def _unperm_w_in(g):
    return jnp.concatenate([g[..., 0:1536], g[..., 3072:3744], g[..., 1536:3072]], axis=-1)


def _perm_w_uq(w):
    lead = w.shape[:-1]
    w5 = w.reshape(lead + (4, 2, 96))
    nope = w5[..., :64].reshape(lead + (4, 128))
    rope = w5[..., 64:].reshape(lead + (4, 64))
    return jnp.concatenate([nope, rope, jnp.zeros(lead + (4, 64), w.dtype)], axis=-1).reshape(lead + (1024,))


def _unperm_w_uq(g):
    lead = g.shape[:-1]
    g4 = g.reshape(lead + (4, 256))
    nope = g4[..., :128].reshape(lead + (4, 2, 64))
    rope = g4[..., 128:192].reshape(lead + (4, 2, 32))
    return jnp.concatenate([nope, rope], axis=-1).reshape(lead + (768,))


def _perm_w_ukv(w):
    lead = w.shape[:-1]
    w4 = w.reshape(lead + (8, 128))
    return jnp.concatenate([w4[..., :64].reshape(lead + (512,)), w4[..., 64:].reshape(lead + (512,))], axis=-1)


def _unperm_w_ukv(g):
    lead = g.shape[:-1]
    return jnp.concatenate([g[..., :512].reshape(lead + (8, 64)), g[..., 512:].reshape(lead + (8, 64))],
                           axis=-1).reshape(lead + (1024,))


def _rope_tables(L, half):
    pos = (jnp.arange(L) - N_PAD).astype(F32)
    inv = ROPE_THETA ** (-jnp.arange(half, dtype=F32) / half)
    ang = pos[:, None] * inv[None, :]
    cos, sin = jnp.cos(ang), jnp.sin(ang)
    reps = 128 // (2 * half)
    cs = jnp.tile(jnp.concatenate([cos, cos], axis=1), (1, reps))
    sn = jnp.tile(jnp.concatenate([-sin, sin], axis=1), (1, reps))
    return cs, sn


def _device_step(x, target, meta, ln_emb_g, ln_emb_b, w_in, q_norm, kv_norm, w_uq, w_ukv, w_out,
                 ln1_g, ln1_b, w_ff1, w_ff2, ln2_g, ln2_b, late=None):
    S = x.shape[0]
    L = S + BLK
    depth = w_in.shape[0]
    cs_m, sn_m = _rope_tables(L, MLA_ROPE // 2)
    cs_r, sn_r = _rope_tables(L, RET_QK // 2)
    hcat = jnp.concatenate([jnp.zeros((N_PAD, D_MODEL), F32), meta, x], axis=0)
    h, hb, _ = ln_fwd(hcat, ln_emb_g, ln_emb_b, name="ln_emb_fwd")

    w_in_sb, w_in_rest = w_in[..., :N_SB], w_in[..., N_SB:]

    def own_slot(gathered, shard):
        return jnp.moveaxis(lax.dynamic_update_slice(gathered, shard[None], (late["s0"], 0, 0, 0)), 0, 1)

    saved = []
    for l in range(depth):
        Psb = mm_nn(hb, w_in_sb[l], tn=N_SB, name=f"in_proj_sb_{l}", out_dtype=BF16)
        P = mm_nn(hb, w_in_rest[l], tn=768, name=f"in_proj_{l}")
        if late is not None and l == 0:
            mixed, sbc, ga, gb = sb_fwd(Psb, name=f"sb_fwd_{l}", gathers=(late["w_out"], late["w_ff1"]))
            w_out, w_ff1 = own_slot(ga, late["w_out"]), own_slot(gb, late["w_ff1"])
        else:
            mixed, sbc = sb_fwd(Psb, name=f"sb_fwd_{l}")
        nq, nkv, KR = mla_pre_fwd(P, q_norm[l], kv_norm[l], cs_m, sn_m, name=f"mla_pre_fwd_{l}")
        Q = mm_nn(nq, w_uq[l], tn=512, name=f"uq_{l}")
        KV = mm_nn(nkv, w_ukv[l], tn=512, name=f"ukv_{l}", out_dtype=BF16)
        if late is not None and l == 0:
            mixed, lse, ga = mla_fwd(Q, KV, KR, cs_m, sn_m, mixed, name=f"mla_fwd_{l}", gathers=(late["w_ff2"],))
            w_ff2 = own_slot(ga, late["w_ff2"])
        else:
            mixed, lse = mla_fwd(Q, KV, KR, cs_m, sn_m, mixed, name=f"mla_fwd_{l}")
        mixed, y, states = ret_fwd(P, cs_r, sn_r, mixed, name=f"ret_fwd_{l}")
        w_out_l = w_out[l].reshape(1, 1536, D_MODEL)
        mix = mm_nn(mixed, w_out_l, tn=1024, name=f"out_proj_{l}")
        h1, h1b, z1 = ln_fwd(mix, ln1_g[l], ln1_b[l], res=h, name=f"ln1_fwd_{l}")
        U = mm_nn(h1b, w_ff1[l], tn=1024, name=f"ff1_{l}")
        w_ff2_l = w_ff2[l].reshape(1, D_FF, D_MODEL)
        mlp = mm_nn(U, w_ff2_l, tn=1024, tk=2048, prologue="relu2", name=f"ff2_{l}")
        h2, h2b, z2 = ln_fwd(mlp, ln2_g[l], ln2_b[l], res=h1, name=f"ln2_fwd_{l}")
        saved.append((hb, Psb, P, sbc, nq, nkv, KR, Q, KV, lse, y, states, mixed, z1, h1b, U, z2))
        h, hb = h2, h2b

    loss_t, dh = loss_fwd_bwd(h, target, name="loss")

    grads = {k: [None] * depth for k in ("q_norm", "kv_norm", "ln1_g", "ln1_b", "ln2_g", "ln2_b")}
    g_ff1 = lax.empty((4, depth, D_MODEL, D_FF // 4), F32)
    g_ff2 = lax.empty((4, depth, D_FF // 4, D_MODEL), F32)
    g_out = lax.empty((4, depth, 384, D_MODEL), F32)
    g_in = lax.empty((depth, D_MODEL, N_INP), F32)
    g_uq = lax.empty((depth, MLA_Q_LORA, 1024), F32)
    g_ukv = lax.empty((depth, MLA_KV_LORA, 1024), F32)
    for l in reversed(range(depth)):
        hb_in, Psb, P, sbc, nq, nkv, KR, Q, KV, lse, y, states, mixed, z1, h1b, U, z2 = saved[l]
        dz2, grads["ln2_g"][l], grads["ln2_b"][l] = ln_bwd(dh, z2, ln2_g[l], name=f"ln2_bwd_{l}")
        w_ff2_l = w_ff2[l].reshape(1, D_FF, D_MODEL)
        g_ff2 = mm_tn(U, dz2, shards=1, tko=1024, tn=1024, prologue="relu2", name=f"ff2_dw_{l}", into=(g_ff2, l, "rows"))
        dU = mm_nt(dz2, w_ff2_l, tn=1024, tko=1024, relu2grad=U, name=f"ff2_dx_{l}", out_dtype=BF16)
        g_ff1 = mm_tn(h1b, dU, shards=4, tko=1024, tn=1024, name=f"ff1_dw_{l}", into=(g_ff1, l, "cols"))
        dh1 = mm_nt(dU, w_ff1[l], tn=1024, tko=1024, axpy=(dz2, DN_ALPHA), name=f"ff1_dx_{l}")
        dz1, grads["ln1_g"][l], grads["ln1_b"][l] = ln_bwd(dh1, z1, ln1_g[l], name=f"ln1_bwd_{l}")
        w_out_l = w_out[l].reshape(1, 1536, D_MODEL)
        g_out = mm_tn(mixed, dz1, shards=1, tko=384, tn=1024, name=f"out_dw_{l}", into=(g_out, l, "rows"))
        dmixed = mm_nt(dz1, w_out_l, tn=1024, tko=1536, name=f"out_dx_{l}")
        d_rqk, d_rv, d_rg = ret_bwd(P, y, states, dmixed, cs_r, sn_r, name=f"ret_bwd_{l}")
        dQ, dKN, dV, dKR = mla_bwd(Q, KV, KR, cs_m, sn_m, mixed, dmixed, lse, name=f"mla_bwd_{l}")
        dKV = jnp.concatenate([dKN, dV], axis=1)
        g_uq = mm_tn(nq, dQ, shards=1, tko=MLA_Q_LORA, tn=512, name=f"uq_dw_{l}", into=(g_uq, l, "layer"))
        g_ukv = mm_tn(nkv, dKV, shards=1, tko=MLA_KV_LORA, tn=512, name=f"ukv_dw_{l}", into=(g_ukv, l, "layer"))
        dnq = mm_nt(dQ, w_uq[l], tn=1024, tko=MLA_Q_LORA, name=f"uq_dx_{l}")
        dnkv = mm_nt(dKV, w_ukv[l], tn=1024, tko=MLA_KV_LORA, name=f"ukv_dx_{l}")
        d_lat, grads["q_norm"][l], grads["kv_norm"][l] = mla_pre_bwd(P, dnq, dnkv, dKR, q_norm[l], kv_norm[l], cs_m, sn_m,
                                                                     name=f"mla_pre_bwd_{l}")
        dq_sb, dk_sb, dv_sb = sb_bwd(Psb, sbc, dmixed, name=f"sb_bwd_{l}")
        dP = jnp.concatenate([dq_sb, dk_sb, dv_sb, d_rqk, d_rv, d_rg, d_lat], axis=1).astype(BF16)
        g_in = mm_tn(hb_in, dP, shards=1, tko=1024, tn=1280, name=f"in_dw_{l}", into=(g_in, l, "layer"))
        dh = mm_nt(dP, w_in[l], tn=1920, tko=1024, axpy=(dz1, DN_ALPHA), name=f"in_dx_{l}")

    dhcat, dg_emb, db_emb = ln_bwd(dh, hcat, ln_emb_g, name="ln_emb_bwd")
    out = {k: jnp.stack(v) for k, v in grads.items()}
    out["w_ff1"], out["w_ff2"], out["w_out"] = g_ff1, g_ff2, g_out
    out["w_in"], out["w_uq"], out["w_ukv"] = g_in, g_uq, g_ukv
    out["ln_emb_g"], out["ln_emb_b"] = dg_emb, db_emb
    out["meta"] = dhcat[N_PAD:BLK]
    return loss_t[0, 0], dhcat[BLK:], out


MESH = pl.DeviceIdType.MESH
PEER_XOR = (2, 1, 3)
_HBM = pl.BlockSpec(memory_space=pltpu.HBM)


def _place():
    x, y, c = lax.axis_index("x"), lax.axis_index("y"), lax.axis_index("c")
    peers = [(1 - x, y, c), (x, 1 - y, c), (1 - x, 1 - y, c)]
    return x, y, c, 2 * x + y, peers, (x, y, 1 - c)


def _gather_plan(w_ref, out_ref, send_sems, recv_sems, base):
    x, y, c, s0, peers, sibling = _place()
    hl = w_ref.shape[0] // 2

    def piece(s, half):
        return out_ref.at[s, pl.ds(half * hl, hl)]

    def copy(k, s, half, to, src=None):
        return pltpu.make_async_remote_copy(src_ref=piece(s, half) if src is None else src, dst_ref=piece(s, half),
                                            send_sem=send_sems.at[base + k], recv_sem=recv_sems.at[base + k],
                                            device_id=to, device_id_type=MESH)

    def first():
        return [copy(k, s0, c, peers[k], src=w_ref.at[pl.ds(c * hl, hl)]) for k in range(3)]

    def passed():
        return [copy(3 + k, s0 ^ PEER_XOR[k], c, sibling) for k in range(3)]

    def start():
        for cp in first():
            cp.start()

    def forward():
        for k, cp in enumerate(passed()):
            copy(k, s0 ^ PEER_XOR[k], c, peers[k]).wait_recv()
            cp.start()

    def finish():
        for k in range(3):
            copy(3 + k, s0 ^ PEER_XOR[k], 1 - c, sibling).wait_recv()
        for cp in first() + passed():
            cp.wait_send()

    return start, forward, finish


def _gather_io(gathers):
    n = len(gathers)
    return ([_HBM] * n, [_HBM] * n, [jax.ShapeDtypeStruct((4,) + w.shape, w.dtype) for w in gathers],
            [pltpu.SemaphoreType.DMA((6 * n,)), pltpu.SemaphoreType.DMA((6 * n,))] if n else [])


def _gather_ride(w_refs, g_refs, sems, at_start, at_forward, at_finish):
    if not w_refs:
        return
    plans = [_gather_plan(w, g, sems[0], sems[1], 6 * n) for n, (w, g) in enumerate(zip(w_refs, g_refs))]
    for step, cond in enumerate((at_start, at_forward, at_finish)):
        @pl.when(cond)
        def _():
            for p in plans:
                p[step]()


def gather_weight(w_shard, *, name):
    nl = w_shard.shape[0]
    hl = nl // 2

    def body(w_ref, out_ref, send_sems, recv_sems):
        x, y, c, s0, peers, sibling = _place()

        def piece(s, half):
            return out_ref.at[s, pl.ds(half * hl, hl)]

        def copy(k, s, half, to, src=None):
            return pltpu.make_async_remote_copy(src_ref=piece(s, half) if src is None else src, dst_ref=piece(s, half),
                                                send_sem=send_sems.at[k], recv_sem=recv_sems.at[k],
                                                device_id=to, device_id_type=MESH)

        first = [copy(k, s0, c, peers[k], src=w_ref.at[pl.ds(c * hl, hl)]) for k in range(3)]
        for cp in first:
            cp.start()
        passed = [copy(3 + k, s0 ^ PEER_XOR[k], c, sibling) for k in range(3)]
        for k in range(3):
            copy(k, s0 ^ PEER_XOR[k], c, peers[k]).wait_recv()
            passed[k].start()
        for k in range(3):
            copy(3 + k, s0 ^ PEER_XOR[k], 1 - c, sibling).wait_recv()
        for cp in first + passed:
            cp.wait_send()

    return pl.pallas_call(
        body, name=name, in_specs=[_HBM], out_specs=_HBM,
        out_shape=jax.ShapeDtypeStruct((4,) + w_shard.shape, w_shard.dtype),
        scratch_shapes=[pltpu.SemaphoreType.DMA((6,)), pltpu.SemaphoreType.DMA((6,))],
    )(w_shard)


def send_half_to_sibling(G, *, name):
    hl = G.shape[1] // 2

    def body(g_ref, out_ref, send_sem, recv_sem):
        x, y, c, s0, peers, sibling = _place()
        cp = pltpu.make_async_remote_copy(src_ref=g_ref.at[:, pl.ds((1 - c) * hl, hl)], dst_ref=out_ref,
                                          send_sem=send_sem, recv_sem=recv_sem, device_id=sibling, device_id_type=MESH)
        cp.start()
        cp.wait()

    return pl.pallas_call(
        body, name=name, in_specs=[_HBM], out_specs=_HBM,
        out_shape=jax.ShapeDtypeStruct((4, hl) + G.shape[2:], G.dtype),
        scratch_shapes=[pltpu.SemaphoreType.DMA, pltpu.SemaphoreType.DMA],
    )(G)


def scatter_to_chips(A, *, name):
    def body(a_ref, out_ref, send_sems, recv_sems):
        x, y, c, s0, peers, sibling = _place()
        copies = [pltpu.make_async_remote_copy(src_ref=a_ref.at[s0 ^ PEER_XOR[k]], dst_ref=out_ref.at[k],
                                               send_sem=send_sems.at[k], recv_sem=recv_sems.at[k],
                                               device_id=peers[k], device_id_type=MESH) for k in range(3)]
        for cp in copies:
            cp.start()
        for cp in copies:
            cp.wait()

    return pl.pallas_call(
        body, name=name, in_specs=[_HBM], out_specs=_HBM,
        out_shape=jax.ShapeDtypeStruct((3,) + A.shape[1:], A.dtype),
        scratch_shapes=[pltpu.SemaphoreType.DMA((3,)), pltpu.SemaphoreType.DMA((3,))],
    )(A)


def join_halves(buf, *, name):
    hl = buf.shape[0] // 2

    def body(b_ref, out_ref, send_sem, recv_sem):
        x, y, c, s0, peers, sibling = _place()
        cp = pltpu.make_async_remote_copy(src_ref=b_ref.at[pl.ds(c * hl, hl)], dst_ref=out_ref.at[pl.ds(c * hl, hl)],
                                          send_sem=send_sem, recv_sem=recv_sem, device_id=sibling, device_id_type=MESH)
        cp.start()
        pltpu.make_async_remote_copy(src_ref=b_ref.at[pl.ds((1 - c) * hl, hl)], dst_ref=out_ref.at[pl.ds((1 - c) * hl, hl)],
                                     send_sem=send_sem, recv_sem=recv_sem, device_id=sibling, device_id_type=MESH).wait_recv()
        cp.wait_send()

    return pl.pallas_call(
        body, name=name, in_specs=[_HBM], out_specs=_HBM, input_output_aliases={0: 0},
        out_shape=jax.ShapeDtypeStruct(buf.shape, buf.dtype),
        scratch_shapes=[pltpu.SemaphoreType.DMA, pltpu.SemaphoreType.DMA],
    )(buf)


def allgather8(xs, *, name, reduce):
    M, N = xs.shape

    def body(x_ref, out_ref, *rest):
        if reduce:
            all_ref, send_sems, recv_sems, local_sem = rest
        else:
            all_ref = out_ref
            send_sems, recv_sems, local_sem = rest
        x, y, c, s0, peers, sibling = _place()
        me = (x, y, c)
        chips = [(1 - x, y), (x, 1 - y), (1 - x, 1 - y)]

        def rows(px, py, pc):
            return all_ref.at[pl.ds((4 * px + 2 * py + pc) * M, M), :]

        def copy(k, block, to, src=None):
            return pltpu.make_async_remote_copy(src_ref=rows(*block) if src is None else src, dst_ref=rows(*block),
                                                send_sem=send_sems.at[k], recv_sem=recv_sems.at[k],
                                                device_id=to, device_id_type=MESH)

        mine = pltpu.make_async_copy(x_ref, rows(*me), local_sem)
        mine.start()
        first = [copy(0, me, sibling, src=x_ref)]
        first += [copy(1 + j, me, (*chip, c), src=x_ref) for j, chip in enumerate(chips)]
        for cp in first:
            cp.start()
        passed = [copy(4 + j, (*chip, c), sibling) for j, chip in enumerate(chips)]
        for j, chip in enumerate(chips):
            copy(1 + j, (*chip, c), me).wait_recv()
            passed[j].start()
        copy(0, sibling, me).wait_recv()
        for j, chip in enumerate(chips):
            copy(4 + j, (*chip, 1 - c), me).wait_recv()
        for cp in first + passed:
            cp.wait_send()
        mine.wait()
        if reduce:
            acc = all_ref[pl.ds(0, M), :]
            for d in range(1, 8):
                acc = acc + all_ref[pl.ds(d * M, M), :]
            out_ref[...] = acc

    vm = pl.BlockSpec(memory_space=pltpu.VMEM)
    scratch = [pltpu.SemaphoreType.DMA((7,)), pltpu.SemaphoreType.DMA((7,)), pltpu.SemaphoreType.DMA]
    if reduce:
        scratch = [pltpu.VMEM((8 * M, N), xs.dtype)] + scratch
    return pl.pallas_call(
        body, name=name, in_specs=[vm], out_specs=vm,
        out_shape=jax.ShapeDtypeStruct((M if reduce else 8 * M, N), xs.dtype), scratch_shapes=scratch,
    )(xs)


def add_halves(G, B, c, *, name):
    S, nl, R, C = G.shape
    hl = nl // 2
    tr = _pick(R, (512, 384, 256, 128))

    def body(c_ref, g_ref, b_ref, o_ref):
        o_ref[...] = (g_ref[...] + b_ref[...]).astype(BF16)

    blk = (1, 1, tr, C)
    return pl.pallas_call(
        body, name=name,
        grid_spec=pltpu.PrefetchScalarGridSpec(
            num_scalar_prefetch=1, grid=(S, hl, R // tr),
            in_specs=[pl.BlockSpec(blk, lambda s, l, r, cr: (s, cr[0] * hl + l, r, 0)),
                      pl.BlockSpec(blk, lambda s, l, r, cr: (s, l, r, 0))],
            out_specs=pl.BlockSpec(blk, lambda s, l, r, cr: (s, l, r, 0))),
        out_shape=jax.ShapeDtypeStruct((S, hl, R, C), BF16),
    )(jnp.reshape(c, (1,)).astype(jnp.int32), G, B)


def add_chips(G, B, Bc, c, s0, *, name):
    S, nl, R, C = G.shape
    hl = nl // 2
    tr = _pick(R, (512, 384, 256, 128))

    def body(pc_ref, ps_ref, g_ref, b_ref, c0_ref, c1_ref, c2_ref, o_ref):
        o_ref[...] = ((((g_ref[0] + b_ref[0]) + c0_ref[0].astype(F32)) + c1_ref[0].astype(F32)) + c2_ref[0].astype(F32))

    blk = (1, 1, tr, C)
    cspec = lambda k: pl.BlockSpec(blk, lambda l, r, pc, ps: (k, l, r, 0))
    return pl.pallas_call(
        body, name=name,
        grid_spec=pltpu.PrefetchScalarGridSpec(
            num_scalar_prefetch=2, grid=(hl, R // tr),
            in_specs=[pl.BlockSpec(blk, lambda l, r, pc, ps: (ps[0], pc[0] * hl + l, r, 0)),
                      pl.BlockSpec(blk, lambda l, r, pc, ps: (ps[0], l, r, 0)), cspec(0), cspec(1), cspec(2)],
            out_specs=pl.BlockSpec((1, tr, C), lambda l, r, pc, ps: (pc[0] * hl + l, r, 0))),
        out_shape=jax.ShapeDtypeStruct((nl, R, C), F32),
    )(jnp.reshape(c, (1,)).astype(jnp.int32), jnp.reshape(s0, (1,)).astype(jnp.int32), G, B, Bc, Bc, Bc)


def reduce_scatter_weight(G, c, s0, *, tag):
    B = send_half_to_sibling(G, name=f"rs_sib_{tag}")
    A = add_halves(G, B, c, name=f"rs_add1_{tag}")
    Bc = scatter_to_chips(A, name=f"rs_chips_{tag}")
    half = add_chips(G, B, Bc, c, s0, name=f"rs_add2_{tag}")
    return join_halves(half, name=f"rs_join_{tag}")


def adamw(w, g, m, v, *, name):
    shp = w.shape
    if len(shp) == 2:
        w, g, m, v = (a[None] for a in (w, g, m, v))
    nl, R, C = w.shape
    tr = R
    for t in (512, 384, 256, 128):
        if R % t == 0:
            tr = t
            break

    def body(w_ref, g_ref, m_ref, v_ref, d_ref, nm_ref, nv_ref):
        gv = g_ref[...]
        mn = ADAM_B1 * m_ref[...] + (1.0 - ADAM_B1) * gv
        vn = ADAM_B2 * v_ref[...] + (1.0 - ADAM_B2) * jnp.square(gv)
        m_hat = mn / (1.0 - ADAM_B1 ** ADAM_STEP)
        v_hat = vn / (1.0 - ADAM_B2 ** ADAM_STEP)
        d_ref[...] = -ADAM_LR * (m_hat / (jnp.sqrt(v_hat) + ADAM_EPS) + ADAM_WD * w_ref[...])
        nm_ref[...] = mn
        nv_ref[...] = vn

    spec = pl.BlockSpec((1, tr, C), lambda l, i: (l, i, 0))
    sds = jax.ShapeDtypeStruct((nl, R, C), F32)
    d, nm, nv = pl.pallas_call(body, name=name, grid=(nl, R // tr), in_specs=[spec] * 4, out_specs=(spec,) * 3,
                               out_shape=(sds,) * 3)(w, g, m, v)
    return d.reshape(shp), nm.reshape(shp), nv.reshape(shp)


_SMALL = ("ln_emb_g", "ln_emb_b", "q_norm", "kv_norm", "ln1_g", "ln1_b", "ln2_g", "ln2_b", "meta")


def _pack_small(d):
    flat = jnp.concatenate([d[k].reshape(-1) for k in _SMALL])
    rows = -(-flat.shape[0] // 128)
    rows = -(-rows // 8) * 8
    flat = jnp.concatenate([flat, jnp.zeros((rows * 128 - flat.shape[0],), F32)])
    return flat.reshape(rows, 128)


def _unpack_small(p, shapes):
    flat = p.reshape(-1)
    out, o = {}, 0
    for k in _SMALL:
        n = int(np.prod(shapes[k]))
        out[k] = flat[o:o + n].reshape(shapes[k])
        o += n
    return out


def kernel(x, meta_tokens, ln_emb_g, ln_emb_b, w_in, mla_q_norm, mla_kv_norm, w_uq, w_ukv, w_out, ln1_g, ln1_b, w_ff1, w_ff2, ln2_g, ln2_b, loss_target, m_meta_tokens, m_ln_emb_g, m_ln_emb_b, m_w_in, m_mla_q_norm, m_mla_kv_norm, m_w_uq, m_w_ukv, m_w_out, m_ln1_g, m_ln1_b, m_w_ff1, m_w_ff2, m_ln2_g, m_ln2_b, v_meta_tokens, v_ln_emb_g, v_ln_emb_b, v_w_in, v_mla_q_norm, v_mla_kv_norm, v_w_uq, v_w_ukv, v_w_out, v_ln1_g, v_ln1_b, v_w_ff1, v_w_ff2, v_ln2_g, v_ln2_b):
    xi, yi, ci = lax.axis_index("x"), lax.axis_index("y"), lax.axis_index("c")
    s0 = 2 * xi + yi
    nl = w_in.shape[0]

    big = {"w_in": w_in, "w_uq": w_uq, "w_ukv": w_ukv}
    late = {"w_out": w_out.astype(BF16), "w_ff1": w_ff1.astype(BF16), "w_ff2": w_ff2.astype(BF16), "s0": s0}
    full = {}
    for k, v in big.items():
        vb = v.astype(BF16)
        full[k] = lax.dynamic_update_slice(gather_weight(vb, name=f"ag_{k}"), vb[None], (s0, 0, 0, 0))
    cols = lambda a: jnp.moveaxis(a, 0, 2).reshape(a.shape[1], a.shape[2], 4 * a.shape[3])
    k_w_in = _perm_w_in(cols(full["w_in"]))[:, None]
    k_w_uq = _perm_w_uq(cols(full["w_uq"]))[:, None]
    k_w_ukv = _perm_w_ukv(cols(full["w_ukv"]))[:, None]
    meta_all = allgather8(meta_tokens, name="ag_meta", reduce=False)
    meta_full = jnp.concatenate([meta_all[32 * s:32 * s + N_META] for s in range(4)], axis=1)

    loss_part, grad_x, g = _device_step(x[0], loss_target[0], meta_full, ln_emb_g, ln_emb_b, k_w_in, mla_q_norm, mla_kv_norm,
                                        k_w_uq, k_w_ukv, None, ln1_g, ln1_b, None, None, ln2_g, ln2_b, late=late)
    loss = lax.psum(loss_part, ("x", "y", "c"))

    def col_shards(a):
        return jnp.moveaxis(a.reshape(a.shape[0], a.shape[1], 4, a.shape[2] // 4), 2, 0)

    G = {"w_in": col_shards(_unperm_w_in(g["w_in"])), "w_uq": col_shards(_unperm_w_uq(g["w_uq"])),
         "w_ukv": col_shards(_unperm_w_ukv(g["w_ukv"])), "w_out": g["w_out"], "w_ff1": g["w_ff1"], "w_ff2": g["w_ff2"]}
    gw = {k: reduce_scatter_weight(v, ci, s0, tag=k) for k, v in G.items()}

    small_shapes = {"ln_emb_g": (D_MODEL,), "ln_emb_b": (D_MODEL,), "q_norm": (nl, MLA_Q_LORA), "kv_norm": (nl, MLA_KV_LORA),
                    "ln1_g": (nl, D_MODEL), "ln1_b": (nl, D_MODEL), "ln2_g": (nl, D_MODEL), "ln2_b": (nl, D_MODEL),
                    "meta": (N_META, D_MODEL)}
    gs = _unpack_small(allgather8(_pack_small(g), name="ar_small", reduce=True), small_shapes)
    gw.update({"ln_emb_g": gs["ln_emb_g"], "ln_emb_b": gs["ln_emb_b"], "mla_q_norm": gs["q_norm"], "mla_kv_norm": gs["kv_norm"],
               "ln1_g": gs["ln1_g"], "ln1_b": gs["ln1_b"], "ln2_g": gs["ln2_g"], "ln2_b": gs["ln2_b"],
               "meta_tokens": lax.dynamic_slice_in_dim(gs["meta"], s0 * 256, 256, axis=1)})

    names = ["meta_tokens", "ln_emb_g", "ln_emb_b", "w_in", "mla_q_norm", "mla_kv_norm", "w_uq", "w_ukv", "w_out",
             "ln1_g", "ln1_b", "w_ff1", "w_ff2", "ln2_g", "ln2_b"]
    ws = [meta_tokens, ln_emb_g, ln_emb_b, w_in, mla_q_norm, mla_kv_norm, w_uq, w_ukv, w_out, ln1_g, ln1_b, w_ff1, w_ff2, ln2_g, ln2_b]
    ms = [m_meta_tokens, m_ln_emb_g, m_ln_emb_b, m_w_in, m_mla_q_norm, m_mla_kv_norm, m_w_uq, m_w_ukv, m_w_out, m_ln1_g, m_ln1_b, m_w_ff1, m_w_ff2, m_ln2_g, m_ln2_b]
    vs = [v_meta_tokens, v_ln_emb_g, v_ln_emb_b, v_w_in, v_mla_q_norm, v_mla_kv_norm, v_w_uq, v_w_ukv, v_w_out, v_ln1_g, v_ln1_b, v_w_ff1, v_w_ff2, v_ln2_g, v_ln2_b]
    deltas, new_m, new_v = [], [], []
    for n, w, m, v in zip(names, ws, ms, vs):
        w2 = w.reshape(1, -1) if w.ndim == 1 else w
        d, nm, nv = adamw(w2, gw[n].reshape(w2.shape), m.reshape(w2.shape), v.reshape(w2.shape), name=f"adamw_{n}")
        deltas.append(d.reshape(w.shape))
        new_m.append(nm.reshape(w.shape))
        new_v.append(nv.reshape(w.shape))
    grads_out = [gw[n].reshape(w.shape) for n, w in zip(names, ws)]
    return (loss, grad_x[None], *grads_out, *deltas, *new_m, *new_v)
```

```python
import functools
import math

import numpy as np
import jax
import jax.numpy as jnp
from jax import lax
from jax.experimental import pallas as pl
from jax.experimental.pallas import tpu as pltpu

F32 = jnp.float32
BF16 = jnp.bfloat16

D_MODEL = 1024
DEPTH = 4
N_META = 16
BLK = 128
N_PAD = 112
SB_HEADS = 8
MLA_HEADS = 8
MLA_NOPE = 64
MLA_ROPE = 32
MLA_V = 64
MLA_Q_LORA = 384
MLA_KV_LORA = 256
RET_HEADS = 4
RET_QK = 64
RET_V = 128
D_FF = 4 * D_MODEL
ROPE_THETA = 10000.0
LN_EPS = 1e-5
DN_ALPHA = (2 * DEPTH) ** 0.25
RET_GAMMA = tuple(1.0 - 2.0 ** (-5 - h) for h in range(RET_HEADS))
RET_LOG_G = tuple(float(np.log(np.float32(g))) for g in RET_GAMMA)
MLA_SCALE = (MLA_NOPE + MLA_ROPE) ** -0.5

ADAM_LR = 0.001
ADAM_B1 = 0.9
ADAM_B2 = 0.999
ADAM_EPS = 1e-08
ADAM_WD = 0.01
ADAM_STEP = 10

N_SB = 1536
C_SBQ, C_SBK, C_SBV = 0, 512, 1024
C_RQ, C_RK, C_RV, C_RG = 0, 256, 512, 1024
C_CQ, C_CKV, C_KR = 1536, 1920, 2176
N_IN = 3744
N_INP = 3840

NEG = -1e30


def _pick(n, cands):
    for t in cands:
        if n % t == 0:
            return t
    raise ValueError(f"no tile for {n} in {cands}")


def _row_tile(n):
    return _pick(n, (1056, 1024, 528, 512, 384, 256, 128))


def _dot(a, b):
    return jnp.dot(a, b, preferred_element_type=F32)


def _dot_nt(a, b):
    return lax.dot_general(a, b, (((1,), (1,)), ((), ())), preferred_element_type=F32)


def _dot_tn(a, b):
    return lax.dot_general(a, b, (((0,), (0,)), ((), ())), preferred_element_type=F32)


def mm_nn(a, b, *, tn, name, tk=None, prologue=None, axpy=None, out_dtype=F32):
    M, K = a.shape
    S, _, Ns = b.shape
    tm = _row_tile(M)
    tk = K if tk is None else tk
    npt = Ns // tn
    nk = K // tk
    alpha = None if axpy is None else axpy[1]

    def body(*refs):
        if axpy is None:
            a_ref, b_ref, o_ref, acc = refs
        else:
            a_ref, b_ref, e_ref, o_ref, acc = refs
        k = pl.program_id(2)

        @pl.when(k == 0)
        def _():
            acc[...] = jnp.zeros_like(acc)

        x = a_ref[...]
        if prologue == "relu2":
            x = jnp.square(jnp.maximum(x, 0.0))
        acc[...] += _dot(x.astype(BF16), b_ref[0])

        @pl.when(k == nk - 1)
        def _():
            r = acc[...]
            if axpy is not None:
                r = r + alpha * e_ref[...]
            o_ref[...] = r.astype(out_dtype)

    in_specs = [pl.BlockSpec((tm, tk), lambda i, j, k: (i, k)),
                pl.BlockSpec((1, tk, tn), lambda i, j, k: (j // npt, k, j % npt))]
    args = [a, b]
    if axpy is not None:
        in_specs.append(pl.BlockSpec((tm, tn), lambda i, j, k: (i, j)))
        args.append(axpy[0])
    return pl.pallas_call(
        body, name=name, grid=(M // tm, (S * Ns) // tn, nk), in_specs=in_specs,
        out_specs=pl.BlockSpec((tm, tn), lambda i, j, k: (i, j)),
        out_shape=jax.ShapeDtypeStruct((M, S * Ns), out_dtype),
        scratch_shapes=[pltpu.VMEM((tm, tn), F32)],
        compiler_params=pltpu.CompilerParams(dimension_semantics=("parallel", "parallel", "arbitrary")),
    )(*args)


def mm_nt(a, b, *, tn, tko, name, axpy=None, relu2grad=None, out_dtype=F32):
    M, N = a.shape
    S, K, Ns = b.shape
    tm = _row_tile(M)
    npt = Ns // tn
    nn = N // tn
    alpha = None if axpy is None else axpy[1]

    def body(*refs):
        if axpy is None and relu2grad is None:
            a_ref, b_ref, o_ref, acc = refs
        else:
            a_ref, b_ref, e_ref, o_ref, acc = refs
        n = pl.program_id(2)

        @pl.when(n == 0)
        def _():
            acc[...] = jnp.zeros_like(acc)

        acc[...] += _dot_nt(a_ref[...].astype(BF16), b_ref[0])

        @pl.when(n == nn - 1)
        def _():
            r = acc[...]
            if axpy is not None:
                r = r + alpha * e_ref[...]
            if relu2grad is not None:
                r = r * (2.0 * jnp.maximum(e_ref[...], 0.0))
            o_ref[...] = r.astype(out_dtype)

    in_specs = [pl.BlockSpec((tm, tn), lambda i, j, n: (i, n)),
                pl.BlockSpec((1, tko, tn), lambda i, j, n: (n // npt, j, n % npt))]
    args = [a, b]
    extra = axpy[0] if axpy is not None else relu2grad
    if extra is not None:
        in_specs.append(pl.BlockSpec((tm, tko), lambda i, j, n: (i, j)))
        args.append(extra)
    return pl.pallas_call(
        body, name=name, grid=(M // tm, K // tko, nn), in_specs=in_specs,
        out_specs=pl.BlockSpec((tm, tko), lambda i, j, n: (i, j)),
        out_shape=jax.ShapeDtypeStruct((M, K), out_dtype),
        scratch_shapes=[pltpu.VMEM((tm, tko), F32)],
        compiler_params=pltpu.CompilerParams(dimension_semantics=("parallel", "parallel", "arbitrary")),
    )(*args)


def mm_tn(a, g, *, shards, tko, tn, name, prologue=None, into=None):
    M, K = a.shape
    _, N = g.shape
    Ns = N // shards
    tm = _row_tile(M)
    npt = Ns // tn
    nm = M // tm

    def body(*refs):
        if into is None:
            a_ref, g_ref, o_ref, acc = refs
        else:
            a_ref, g_ref, _, o_ref, acc = refs
        m = pl.program_id(2)

        @pl.when(m == 0)
        def _():
            acc[...] = jnp.zeros_like(acc)

        x = a_ref[...]
        if prologue == "relu2":
            x = jnp.square(jnp.maximum(x, 0.0))
        acc[...] += _dot_tn(x.astype(BF16), g_ref[...].astype(BF16))

        @pl.when(m == nm - 1)
        def _():
            if into is None or into[2] == "layer":
                o_ref[0] = acc[...]
            else:
                o_ref[0, 0] = acc[...]

    in_specs = [pl.BlockSpec((tm, tko), lambda i, j, m: (m, i)),
                pl.BlockSpec((tm, tn), lambda i, j, m: (m, j))]
    scratch = [pltpu.VMEM((tko, tn), F32)]
    params = pltpu.CompilerParams(dimension_semantics=("parallel", "parallel", "arbitrary"))
    if into is None:
        return pl.pallas_call(
            body, name=name, grid=(K // tko, N // tn, nm), in_specs=in_specs,
            out_specs=pl.BlockSpec((1, tko, tn), lambda i, j, m: (j // npt, i, j % npt)),
            out_shape=jax.ShapeDtypeStruct((shards, K, Ns), F32), scratch_shapes=scratch, compiler_params=params,
        )(a, g)
    buf, layer, how = into
    if how == "layer":
        out_spec = pl.BlockSpec((1, tko, tn), lambda i, j, m: (layer, i, j))
    elif how == "cols":
        npt4 = (N // 4) // tn
        out_spec = pl.BlockSpec((1, 1, tko, tn), lambda i, j, m: (j // npt4, layer, i, j % npt4))
    else:
        kpt4 = (K // 4) // tko
        out_spec = pl.BlockSpec((1, 1, tko, tn), lambda i, j, m: (i // kpt4, layer, i % kpt4, j))
    return pl.pallas_call(
        body, name=name, grid=(K // tko, N // tn, nm), in_specs=in_specs + [pl.BlockSpec(memory_space=pl.ANY)],
        out_specs=out_spec, out_shape=jax.ShapeDtypeStruct(buf.shape, F32), input_output_aliases={2: 0},
        scratch_shapes=scratch, compiler_params=params,
    )(a, g, buf)


def _ln_stats(z):
    mu = jnp.mean(z, axis=-1, keepdims=True)
    zc = z - mu
    var = jnp.mean(jnp.square(zc), axis=-1, keepdims=True)
    r = lax.rsqrt(var + LN_EPS)
    return zc * r, r


def ln_fwd(x, g, b, *, name, res=None):
    L, Dm = x.shape
    tr = _row_tile(L)
    g2, b2 = g.reshape(1, Dm), b.reshape(1, Dm)

    def body(*refs):
        if res is None:
            x_ref, g_ref, b_ref, y_ref, yb_ref = refs
            z = x_ref[...]
        else:
            x_ref, r_ref, g_ref, b_ref, y_ref, yb_ref, z_ref = refs
            z = DN_ALPHA * r_ref[...] + x_ref[...]
            z_ref[...] = z
        xh, _ = _ln_stats(z)
        y = xh * g_ref[...] + b_ref[...]
        y_ref[...] = y
        yb_ref[...] = y.astype(BF16)

    row = pl.BlockSpec((tr, Dm), lambda i: (i, 0))
    vec = pl.BlockSpec((1, Dm), lambda i: (0, 0))
    sds = jax.ShapeDtypeStruct((L, Dm), F32)
    sdb = jax.ShapeDtypeStruct((L, Dm), BF16)
    if res is None:
        y, yb = pl.pallas_call(body, name=name, grid=(L // tr,), in_specs=[row, vec, vec], out_specs=(row, row),
                               out_shape=(sds, sdb))(x, g2, b2)
        return y, yb, x
    return pl.pallas_call(body, name=name, grid=(L // tr,), in_specs=[row, row, vec, vec], out_specs=(row, row, row),
                          out_shape=(sds, sdb, sds))(x, res, g2, b2)


def ln_bwd(dy, z, g, *, name):
    L, Dm = z.shape
    tr = _row_tile(L)

    def body(dy_ref, z_ref, g_ref, dz_ref, dg_ref, db_ref):
        @pl.when(pl.program_id(0) == 0)
        def _():
            dg_ref[...] = jnp.zeros_like(dg_ref)
            db_ref[...] = jnp.zeros_like(db_ref)

        dyv = dy_ref[...]
        xh, r = _ln_stats(z_ref[...])
        dxh = dyv * g_ref[...]
        m1 = jnp.mean(dxh, axis=-1, keepdims=True)
        m2 = jnp.mean(dxh * xh, axis=-1, keepdims=True)
        dz_ref[...] = r * (dxh - m1 - xh * m2)
        dg_ref[...] += jnp.sum(dyv * xh, axis=0, keepdims=True)
        db_ref[...] += jnp.sum(dyv, axis=0, keepdims=True)

    row = pl.BlockSpec((tr, Dm), lambda i: (i, 0))
    vec = pl.BlockSpec((1, Dm), lambda i: (0, 0))
    return pl.pallas_call(
        body, name=name, grid=(L // tr,), in_specs=[row, row, vec], out_specs=(row, vec, vec),
        out_shape=(jax.ShapeDtypeStruct((L, Dm), F32), jax.ShapeDtypeStruct((1, Dm), F32), jax.ShapeDtypeStruct((1, Dm), F32)),
        compiler_params=pltpu.CompilerParams(dimension_semantics=("arbitrary",)),
    )(dy, z, g.reshape(1, Dm))


def loss_fwd_bwd(h, target, *, name):
    L, Dm = h.shape
    nb = L // BLK

    def body(h_ref, t_ref, l_ref, dh_ref):
        i = pl.program_id(0)

        @pl.when(i == 0)
        def _():
            l_ref[...] = jnp.zeros_like(l_ref)
            dh_ref[...] = jnp.zeros_like(dh_ref)

        @pl.when(i > 0)
        def _():
            e = h_ref[...] - t_ref[...]
            dh_ref[...] = e * (1.0 / Dm)
            part = jnp.sum(jnp.sum(jnp.square(e), axis=-1, keepdims=True) * (1.0 / Dm), axis=0, keepdims=True)
            l_ref[...] += 0.5 * part

    return pl.pallas_call(
        body, name=name, grid=(nb,),
        in_specs=[pl.BlockSpec((BLK, Dm), lambda i: (i, 0)),
                  pl.BlockSpec((BLK, Dm), lambda i: (jnp.maximum(i - 1, 0), 0))],
        out_specs=(pl.BlockSpec((8, 128), lambda i: (0, 0)), pl.BlockSpec((BLK, Dm), lambda i: (i, 0))),
        out_shape=(jax.ShapeDtypeStruct((8, 128), F32), jax.ShapeDtypeStruct((L, Dm), F32)),
        compiler_params=pltpu.CompilerParams(dimension_semantics=("arbitrary",)),
    )(h, target)


def _swap_half(x, half):
    ax = x.ndim - 1
    n = x.shape[ax]
    lane = lax.broadcasted_iota(jnp.int32, x.shape, ax)
    up = pltpu.roll(x, n - half, ax)
    dn = pltpu.roll(x, half, ax)
    return jnp.where((lane % (2 * half)) < half, up, dn)


def _rope(x, cs, sn, half):
    return x * cs + _swap_half(x, half) * sn


def _rope_t(dy, cs, sn, half):
    return dy * cs + _swap_half(dy * sn, half)


def _rms(x):
    r = lax.rsqrt(jnp.mean(jnp.square(x), axis=-1, keepdims=True) + LN_EPS)
    return x * r, r


def mla_pre_fwd(P, gq, gkv, cs, sn, *, name):
    L = P.shape[0]
    tr = _row_tile(L)

    def body(p_ref, gq_ref, gkv_ref, cs_ref, sn_ref, nq_ref, nkv_ref, kr_ref):
        cq = p_ref[:, 0:MLA_Q_LORA]
        ckv = p_ref[:, MLA_Q_LORA:MLA_Q_LORA + MLA_KV_LORA]
        kr = p_ref[:, 640:768]
        nq_ref[...] = (_rms(cq)[0] * gq_ref[...]).astype(BF16)
        nkv_ref[...] = (_rms(ckv)[0] * gkv_ref[...]).astype(BF16)
        krr = _rope(kr, cs_ref[...], sn_ref[...], MLA_ROPE // 2)
        kr_ref[...] = (krr + pltpu.roll(krr, MLA_ROPE, 1)).astype(BF16)

    return pl.pallas_call(
        body, name=name, grid=(L // tr,),
        in_specs=[pl.BlockSpec((tr, 768), lambda i: (i, C_CQ // 768)),
                  pl.BlockSpec((1, MLA_Q_LORA), lambda i: (0, 0)), pl.BlockSpec((1, MLA_KV_LORA), lambda i: (0, 0)),
                  pl.BlockSpec((tr, 128), lambda i: (i, 0)), pl.BlockSpec((tr, 128), lambda i: (i, 0))],
        out_specs=(pl.BlockSpec((tr, MLA_Q_LORA), lambda i: (i, 0)), pl.BlockSpec((tr, MLA_KV_LORA), lambda i: (i, 0)),
                   pl.BlockSpec((tr, 128), lambda i: (i, 0))),
        out_shape=(jax.ShapeDtypeStruct((L, MLA_Q_LORA), BF16), jax.ShapeDtypeStruct((L, MLA_KV_LORA), BF16),
                   jax.ShapeDtypeStruct((L, 128), BF16)),
    )(P, gq.reshape(1, -1), gkv.reshape(1, -1), cs, sn)


def mla_pre_bwd(P, dnq, dnkv, dkr, gq, gkv, cs, sn, *, name):
    L = P.shape[0]
    tr = _row_tile(L)

    def body(p_ref, dnq_ref, dnkv_ref, dkr_ref, gq_ref, gkv_ref, cs_ref, sn_ref, dp_ref, dgq_ref, dgkv_ref):
        @pl.when(pl.program_id(0) == 0)
        def _():
            dgq_ref[...] = jnp.zeros_like(dgq_ref)
            dgkv_ref[...] = jnp.zeros_like(dgkv_ref)

        def rms_bwd(x, dy, g_ref, dg_ref):
            xn, r = _rms(x)
            dxn = dy * g_ref[...]
            dg_ref[...] += jnp.sum(dy * xn, axis=0, keepdims=True)
            return r * (dxn - xn * jnp.mean(dxn * xn, axis=-1, keepdims=True))

        dp_ref[:, 0:MLA_Q_LORA] = rms_bwd(p_ref[:, 0:MLA_Q_LORA], dnq_ref[...], gq_ref, dgq_ref)
        dp_ref[:, MLA_Q_LORA:640] = rms_bwd(p_ref[:, MLA_Q_LORA:640], dnkv_ref[...], gkv_ref, dgkv_ref)
        d2 = dkr_ref[...]
        lane = lax.broadcasted_iota(jnp.int32, d2.shape, 1)
        dkr = jnp.where(lane < MLA_ROPE, d2 + pltpu.roll(d2, 128 - MLA_ROPE, 1), 0.0)
        dp_ref[:, 640:768] = _rope_t(dkr, cs_ref[...], sn_ref[...], MLA_ROPE // 2)

    return pl.pallas_call(
        body, name=name, grid=(L // tr,),
        in_specs=[pl.BlockSpec((tr, 768), lambda i: (i, C_CQ // 768)),
                  pl.BlockSpec((tr, MLA_Q_LORA), lambda i: (i, 0)), pl.BlockSpec((tr, MLA_KV_LORA), lambda i: (i, 0)),
                  pl.BlockSpec((tr, 128), lambda i: (i, 0)),
                  pl.BlockSpec((1, MLA_Q_LORA), lambda i: (0, 0)), pl.BlockSpec((1, MLA_KV_LORA), lambda i: (0, 0)),
                  pl.BlockSpec((tr, 128), lambda i: (i, 0)), pl.BlockSpec((tr, 128), lambda i: (i, 0))],
        out_specs=(pl.BlockSpec((tr, 768), lambda i: (i, 0)), pl.BlockSpec((1, MLA_Q_LORA), lambda i: (0, 0)),
                   pl.BlockSpec((1, MLA_KV_LORA), lambda i: (0, 0))),
        out_shape=(jax.ShapeDtypeStruct((L, 768), F32), jax.ShapeDtypeStruct((1, MLA_Q_LORA), F32),
                   jax.ShapeDtypeStruct((1, MLA_KV_LORA), F32)),
        compiler_params=pltpu.CompilerParams(dimension_semantics=("arbitrary",)),
    )(P, dnq, dnkv, dkr, gq.reshape(1, -1), gkv.reshape(1, -1), cs, sn)


def _tri(kind):
    r = lax.broadcasted_iota(jnp.int32, (BLK, BLK), 0)
    c = lax.broadcasted_iota(jnp.int32, (BLK, BLK), 1)
    t = ((r > c) if kind == "right" else (r < c)).astype(BF16)
    return jnp.concatenate([t, t], axis=0)


def _tri_sum(x, tt):
    hi = x.astype(BF16)
    lo = (x - hi.astype(F32)).astype(BF16)
    return _dot(jnp.concatenate([hi, lo], axis=1), tt)


def _sb_tile(q, k, i, j, tt_right, R):
    row = lax.broadcasted_iota(jnp.int32, (BLK, BLK), 0)
    col = lax.broadcasted_iota(jnp.int32, (BLK, BLK), 1)
    s_idx = j * BLK + col
    mask = (s_idx < i * BLK + row) & (s_idx >= N_PAD)
    z = _dot_nt(q, k)
    sp = jnp.maximum(z, 0.0) + jnp.log1p(jnp.exp(-jnp.abs(z)))
    lk = jnp.where(mask, -sp, 0.0)
    E = _tri_sum(lk, tt_right) + R
    return mask, z, sp, lk, E


def _old_sb_fwd(P, *, name):
    L = P.shape[0]
    nb = L // BLK

    def body(q_ref, k_ref, v_ref, o_ref, c_ref):
        i = pl.program_id(1)
        tt = _tri("right")
        lane = lax.broadcasted_iota(jnp.int32, (BLK, 128), 1)
        qs = [(q_ref[:, 64 * h:64 * h + 64] * 0.125).astype(BF16) for h in range(2)]

        def step(jj, carry):
            j = i - jj
            off = pl.multiple_of(j * BLK, BLK)
            kb = k_ref[pl.ds(off, BLK), :].astype(BF16)
            vb = v_ref[pl.ds(off, BLK), :].astype(BF16)
            out = []
            for h in range(2):
                o, R = carry[2 * h], carry[2 * h + 1]
                sl = slice(64 * h, 64 * h + 64)
                mask, z, sp, lk, E = _sb_tile(qs[h], kb[:, sl], i, j, tt, R)
                w = jnp.where(mask, jnp.exp(z - sp + E), 0.0)
                c_ref[h] = jnp.where(lane == j, R, c_ref[h])
                out += [o + _dot(w.astype(BF16), vb[:, sl]), R + jnp.sum(lk, axis=1, keepdims=True)]
            return tuple(out)

        c_ref[...] = jnp.zeros_like(c_ref)
        z0 = (jnp.zeros((BLK, 64), F32), jnp.zeros((BLK, 1), F32))
        res = lax.fori_loop(0, i + 1, step, z0 + z0)
        o_ref[...] = jnp.concatenate([res[0], res[2]], axis=1)

    return pl.pallas_call(
        body, name=name, grid=(SB_HEADS // 2, nb),
        in_specs=[pl.BlockSpec((BLK, 128), lambda hp, i: (i, C_SBQ // 128 + hp)),
                  pl.BlockSpec((L, 128), lambda hp, i: (0, C_SBK // 128 + hp)),
                  pl.BlockSpec((L, 128), lambda hp, i: (0, C_SBV // 128 + hp))],
        out_specs=(pl.BlockSpec((BLK, 128), lambda hp, i: (i, hp)), pl.BlockSpec((2, BLK, 128), lambda hp, i: (hp, i, 0))),
        out_shape=(jax.ShapeDtypeStruct((L, 512), F32), jax.ShapeDtypeStruct((SB_HEADS, L, 128), F32)),
        compiler_params=pltpu.CompilerParams(dimension_semantics=("parallel", "arbitrary")),
    )(P, P, P)


def _old_sb_bwd(P, carries, dmixed, *, name):
    L = P.shape[0]
    nb = L // BLK

    def body(q_ref, k_ref, v_ref, c_ref, do_ref, dq_ref, dk_ref, dv_ref):
        i = pl.program_id(1)

        @pl.when(i == 0)
        def _():
            dk_ref[...] = jnp.zeros_like(dk_ref)
            dv_ref[...] = jnp.zeros_like(dv_ref)

        tr = _tri("right")
        tl = _tri("left")
        lane = lax.broadcasted_iota(jnp.int32, (BLK, 128), 1)
        qs = [(q_ref[:, 64 * h:64 * h + 64] * 0.125).astype(BF16) for h in range(2)]
        dos = [do_ref[:, 64 * h:64 * h + 64].astype(BF16) for h in range(2)]

        def step(j, carry):
            off = pl.multiple_of(j * BLK, BLK)
            kb = k_ref[pl.ds(off, BLK), :].astype(BF16)
            vb = v_ref[pl.ds(off, BLK), :].astype(BF16)
            out, dks, dvs = [], [], []
            for h in range(2):
                dq, PL = carry[2 * h], carry[2 * h + 1]
                sl = slice(64 * h, 64 * h + 64)
                R = jnp.sum(jnp.where(lane == j, c_ref[h], 0.0), axis=1, keepdims=True)
                mask, z, sp, lk, E = _sb_tile(qs[h], kb[:, sl], i, j, tr, R)
                sig = jnp.exp(z - sp)
                w = jnp.where(mask, sig * jnp.exp(E), 0.0)
                dA = _dot_nt(dos[h], vb[:, sl]) * w
                Pp = _tri_sum(dA, tl) + PL
                dz = jnp.where(mask, dA - sig * (dA + Pp), 0.0).astype(BF16)
                dks.append(_dot_tn(dz, qs[h]))
                dvs.append(_dot_tn(w.astype(BF16), dos[h]))
                out += [dq + _dot(dz, kb[:, sl]), PL + jnp.sum(dA, axis=1, keepdims=True)]
            dk_ref[pl.ds(off, BLK), :] += jnp.concatenate(dks, axis=1)
            dv_ref[pl.ds(off, BLK), :] += jnp.concatenate(dvs, axis=1)
            return tuple(out)

        z0 = (jnp.zeros((BLK, 64), F32), jnp.zeros((BLK, 1), F32))
        res = lax.fori_loop(0, i + 1, step, z0 + z0)
        dq_ref[...] = jnp.concatenate([res[0], res[2]], axis=1) * 0.125

    blk = lambda c0: pl.BlockSpec((BLK, 128), lambda hp, i: (i, c0 + hp))
    full = lambda c0: pl.BlockSpec((L, 128), lambda hp, i: (0, c0 + hp))
    sds = jax.ShapeDtypeStruct((L, 512), F32)
    return pl.pallas_call(
        body, name=name, grid=(SB_HEADS // 2, nb),
        in_specs=[blk(C_SBQ // 128), full(C_SBK // 128), full(C_SBV // 128),
                  pl.BlockSpec((2, BLK, 128), lambda hp, i: (hp, i, 0)), blk(0)],
        out_specs=(blk(0), full(0), full(0)), out_shape=(sds, sds, sds),
        compiler_params=pltpu.CompilerParams(dimension_semantics=("parallel", "arbitrary")),
    )(P, P, P, carries, dmixed)


def _mla_mask(i, j):
    row = lax.broadcasted_iota(jnp.int32, (BLK, BLK), 0)
    col = lax.broadcasted_iota(jnp.int32, (BLK, BLK), 1)
    t_idx = i * BLK + row
    s_idx = j * BLK + col
    return (s_idx <= t_idx) & ((s_idx >= N_PAD) | (s_idx == t_idx))


def _mla_q(q_ref, cs_ref, sn_ref):
    qr = _rope(q_ref[:, 128:256], cs_ref[...], sn_ref[...], MLA_ROPE // 2)
    qn = [q_ref[:, 64 * h:64 * h + 64].astype(BF16) for h in range(2)]
    qrs = [qr[:, 32 * h:32 * h + 32].astype(BF16) for h in range(2)]
    return qn, qrs


def _old_mla_fwd(Q, KV, KR, cs, sn, *, name):
    L = Q.shape[0]
    nb = L // BLK

    def body(q_ref, kn_ref, v_ref, kr_ref, cs_ref, sn_ref, o_ref, lse_ref):
        i = pl.program_id(1)
        qn, qrs = _mla_q(q_ref, cs_ref, sn_ref)

        def step(j, carry):
            off = pl.multiple_of(j * BLK, BLK)
            knb = kn_ref[pl.ds(off, BLK), :]
            vb = v_ref[pl.ds(off, BLK), :]
            krb = kr_ref[pl.ds(off, BLK), 0:MLA_ROPE]
            mask = _mla_mask(i, j)
            out = []
            for h in range(2):
                m, l, acc = carry[3 * h], carry[3 * h + 1], carry[3 * h + 2]
                sl = slice(64 * h, 64 * h + 64)
                s = (_dot_nt(qn[h], knb[:, sl]) + _dot_nt(qrs[h], krb)) * MLA_SCALE
                s = jnp.where(mask, s, NEG)
                m_new = jnp.maximum(m, jnp.max(s, axis=1, keepdims=True))
                a = jnp.exp(m - m_new)
                p = jnp.exp(s - m_new)
                out += [m_new, a * l + jnp.sum(p, axis=1, keepdims=True), a * acc + _dot(p.astype(BF16), vb[:, sl])]
            return tuple(out)

        z0 = (jnp.full((BLK, 1), NEG, F32), jnp.zeros((BLK, 1), F32), jnp.zeros((BLK, 64), F32))
        res = lax.fori_loop(0, i + 1, step, z0 + z0)
        o_ref[...] = jnp.concatenate([res[2] / res[1], res[5] / res[4]], axis=1)
        lane = lax.broadcasted_iota(jnp.int32, (BLK, 128), 1)
        lse0 = res[0] + jnp.log(res[1])
        lse1 = res[3] + jnp.log(res[4])
        lse_ref[0] = jnp.where(lane == 0, lse0, jnp.where(lane == 1, lse1, 0.0))

    return pl.pallas_call(
        body, name=name, grid=(MLA_HEADS // 2, nb),
        in_specs=[pl.BlockSpec((BLK, 256), lambda hp, i: (i, hp)),
                  pl.BlockSpec((L, 128), lambda hp, i: (0, hp)),
                  pl.BlockSpec((L, 128), lambda hp, i: (0, 4 + hp)),
                  pl.BlockSpec((L, 128), lambda hp, i: (0, 0)),
                  pl.BlockSpec((BLK, 128), lambda hp, i: (i, 0)), pl.BlockSpec((BLK, 128), lambda hp, i: (i, 0))],
        out_specs=(pl.BlockSpec((BLK, 128), lambda hp, i: (i, hp)), pl.BlockSpec((1, BLK, 128), lambda hp, i: (hp, i, 0))),
        out_shape=(jax.ShapeDtypeStruct((L, 512), F32), jax.ShapeDtypeStruct((4, L, 128), F32)),
        compiler_params=pltpu.CompilerParams(dimension_semantics=("parallel", "arbitrary")),
    )(Q, KV, KV, KR, cs, sn)


def _old_mla_bwd(Q, KV, KR, cs, sn, mixed, dmixed, lse, *, name):
    L = Q.shape[0]
    nb = L // BLK

    def body(q_ref, kn_ref, v_ref, kr_ref, cs_ref, sn_ref, o_ref, do_ref, lse_ref, dq_ref, dkn_ref, dv_ref, dkr_ref):
        hp = pl.program_id(0)
        i = pl.program_id(1)

        @pl.when(i == 0)
        def _():
            dkn_ref[...] = jnp.zeros_like(dkn_ref)
            dv_ref[...] = jnp.zeros_like(dv_ref)

        @pl.when((i == 0) & (hp == 0))
        def _():
            dkr_ref[...] = jnp.zeros_like(dkr_ref)

        qn, qrs = _mla_q(q_ref, cs_ref, sn_ref)
        dos, dd, lses = [], [], []
        for h in range(2):
            sl = slice(64 * h, 64 * h + 64)
            d = do_ref[:, sl]
            dos.append(d.astype(BF16))
            dd.append(jnp.sum(d * o_ref[:, sl], axis=1, keepdims=True))
            lses.append(lse_ref[0, :, h:h + 1])

        def step(j, carry):
            off = pl.multiple_of(j * BLK, BLK)
            knb = kn_ref[pl.ds(off, BLK), :]
            vb = v_ref[pl.ds(off, BLK), :]
            krb = kr_ref[pl.ds(off, BLK), 0:MLA_ROPE]
            mask = _mla_mask(i, j)
            out, dkns, dvs = [], [], []
            dkr = jnp.zeros((BLK, MLA_ROPE), F32)
            for h in range(2):
                dqn, dqr = carry[2 * h], carry[2 * h + 1]
                sl = slice(64 * h, 64 * h + 64)
                s = (_dot_nt(qn[h], knb[:, sl]) + _dot_nt(qrs[h], krb)) * MLA_SCALE
                p = jnp.where(mask, jnp.exp(s - lses[h]), 0.0)
                dp = _dot_nt(dos[h], vb[:, sl])
                ds = (p * (dp - dd[h]) * MLA_SCALE).astype(BF16)
                dkns.append(_dot_tn(ds, qn[h]))
                dvs.append(_dot_tn(p.astype(BF16), dos[h]))
                dkr = dkr + _dot_tn(ds, qrs[h])
                out += [dqn + _dot(ds, knb[:, sl]), dqr + _dot(ds, krb)]
            dkn_ref[pl.ds(off, BLK), :] += jnp.concatenate(dkns, axis=1)
            dv_ref[pl.ds(off, BLK), :] += jnp.concatenate(dvs, axis=1)
            dkr_ref[pl.ds(off, BLK), :] += jnp.concatenate([dkr, jnp.zeros((BLK, 128 - MLA_ROPE), F32)], axis=1)
            return tuple(out)

        z0 = (jnp.zeros((BLK, 64), F32), jnp.zeros((BLK, MLA_ROPE), F32))
        res = lax.fori_loop(0, i + 1, step, z0 + z0)
        dqr = jnp.concatenate([res[1], res[3], jnp.zeros((BLK, 64), F32)], axis=1)
        dq_ref[...] = jnp.concatenate([res[0], res[2], _rope_t(dqr, cs_ref[...], sn_ref[...], MLA_ROPE // 2)], axis=1)

    blk = lambda c0: pl.BlockSpec((BLK, 128), lambda hp, i: (i, c0 + hp))
    full = lambda c0: pl.BlockSpec((L, 128), lambda hp, i: (0, c0 + hp))
    tab = pl.BlockSpec((BLK, 128), lambda hp, i: (i, 0))
    return pl.pallas_call(
        body, name=name, grid=(MLA_HEADS // 2, nb),
        in_specs=[pl.BlockSpec((BLK, 256), lambda hp, i: (i, hp)), full(0), full(4),
                  pl.BlockSpec((L, 128), lambda hp, i: (0, 0)), tab, tab, blk(4), blk(4),
                  pl.BlockSpec((1, BLK, 128), lambda hp, i: (hp, i, 0))],
        out_specs=(pl.BlockSpec((BLK, 256), lambda hp, i: (i, hp)), full(0), full(0),
                   pl.BlockSpec((L, 128), lambda hp, i: (0, 0))),
        out_shape=(jax.ShapeDtypeStruct((L, 1024), F32), jax.ShapeDtypeStruct((L, 512), F32),
                   jax.ShapeDtypeStruct((L, 512), F32), jax.ShapeDtypeStruct((L, 128), F32)),
        compiler_params=pltpu.CompilerParams(dimension_semantics=("arbitrary", "arbitrary")),
    )(Q, KV, KV, KR, cs, sn, mixed, dmixed, lse)


SB_UNROLL = 2


def _tq(L):
    return 384 if L % 384 == 0 else BLK


def _softplus(z):
    na = lax.bitcast_convert_type(lax.bitcast_convert_type(z, jnp.uint32) | jnp.uint32(0x80000000), F32)
    return jnp.maximum(z, 0.0) + jnp.log(1.0 + jnp.exp(na))


def _head_split(x, first):
    zero = jnp.zeros_like(x)
    return jnp.where(first, x, zero), jnp.where(first, zero, x)


def _sb_mask(I, j, tq):
    row = lax.broadcasted_iota(jnp.int32, (tq, BLK), 0)
    col = lax.broadcasted_iota(jnp.int32, (tq, BLK), 1)
    s_idx = j * BLK + col
    return (s_idx < I * tq + row) & (s_idx >= N_PAD)


def _tri2(kind, splits):
    r = lax.broadcasted_iota(jnp.int32, (256, 256), 0)
    c = lax.broadcasted_iota(jnp.int32, (256, 256), 1)
    same = (r < BLK) == (c < BLK)
    t = (same & ((r > c) if kind == "right" else (r < c))).astype(BF16)
    return jnp.concatenate([t] * splits, axis=0)


def _split2(x):
    hi = x.astype(BF16)
    lo = (x - hi.astype(F32)).astype(BF16)
    return jnp.concatenate([hi, lo], axis=1)


def _sb_mask2(I, j, tq):
    row = lax.broadcasted_iota(jnp.int32, (tq, 256), 0)
    col = lax.broadcasted_iota(jnp.int32, (tq, 256), 1)
    s_idx = j * BLK + (col & (BLK - 1))
    return (s_idx < I * tq + row) & (s_idx >= N_PAD)


def _per_head(x, r0, r1):
    return jnp.concatenate([x[:, 0:BLK] + r0, x[:, BLK:2 * BLK] + r1], axis=1)


def sb_fwd(P, *, name, gathers=()):
    L = P.shape[0]
    tq = _tq(L)
    nd = tq // BLK
    ng = len(gathers)
    nhp, nI = SB_HEADS // 2, L // tq

    def body(q_ref, k_ref, v_ref, *rest):
        w_refs, (o_ref, c_ref), g_refs, sems = rest[:ng], rest[ng:ng + 2], rest[ng + 2:2 * ng + 2], rest[2 * ng + 2:]
        hp = pl.program_id(0)
        I = pl.program_id(1)
        _gather_ride(w_refs, g_refs, sems, (hp == 0) & (I == 0), (hp == nhp - 1) & (I == 0), (hp == nhp - 1) & (I == nI - 1))
        tt = _tri2("right", 2)
        lane_q = lax.broadcasted_iota(jnp.int32, (tq, 128), 1)
        first_k = lax.broadcasted_iota(jnp.int32, (BLK, 128), 1) < 64
        q = (q_ref[...] * 0.125).astype(BF16)
        c_ref[...] = jnp.zeros_like(c_ref)

        def tiles(T, carry, kind):
            o, R0, R1 = carry
            js = [T * nd + nd - 1 - u for u in range(nd)]
            st = []
            for j in js:
                off = pl.multiple_of(j * BLK, BLK)
                kcat = jnp.concatenate(_head_split(k_ref[pl.ds(off, BLK), :].astype(BF16), first_k), axis=0)
                st.append([_dot_nt(q, kcat), off])
            for u, (s, j) in enumerate(zip(st, js)):
                sp = _softplus(s[0])
                mask = _sb_mask2(I, j, tq) if kind == "diag" else (pad_ok if kind == "first" and u == nd - 1 else None)
                spm = sp if mask is None else jnp.where(mask, sp, 0.0)
                s += [sp, spm, mask, _dot(_split2(spm), tt)]
            for (z, off, sp, spm, mask, S), j in zip(st, js):
                vcat = jnp.concatenate(_head_split(v_ref[pl.ds(off, BLK), :].astype(BF16), first_k), axis=0)
                w = jnp.exp(_per_head(z - sp - S, R0, R1))
                if mask is not None:
                    w = jnp.where(mask, w, 0.0)
                c_ref[0] = jnp.where(lane_q == j, R0, c_ref[0])
                c_ref[1] = jnp.where(lane_q == j, R1, c_ref[1])
                o = o + _dot(w.astype(BF16), vcat)
                R0 = R0 - (S[:, 0:1] + spm[:, 0:1])
                R1 = R1 - (S[:, BLK:BLK + 1] + spm[:, BLK:BLK + 1])
            return (o, R0, R1)

        pad_ok = (lax.broadcasted_iota(jnp.int32, (tq, 256), 1) & (BLK - 1)) >= N_PAD
        carry = (jnp.zeros((tq, 128), F32), jnp.zeros((tq, 1), F32), jnp.zeros((tq, 1), F32))
        carry = tiles(I, carry, "diag")
        carry = lax.fori_loop(0, jnp.maximum(I - 1, 0), lambda t, c: tiles(I - 1 - t, c, None), carry)
        carry = lax.fori_loop(0, jnp.minimum(I, 1), lambda t, c: tiles(0, c, "first"), carry)
        o_ref[...] = carry[0]

    g_in, g_out, g_shapes, g_sems = _gather_io(gathers)
    return pl.pallas_call(
        body, name=name, grid=(nhp, nI),
        in_specs=[pl.BlockSpec((tq, 128), lambda hp, i: (i, C_SBQ // 128 + hp)),
                  pl.BlockSpec((L, 128), lambda hp, i: (0, C_SBK // 128 + hp)),
                  pl.BlockSpec((L, 128), lambda hp, i: (0, C_SBV // 128 + hp))] + g_in,
        out_specs=[pl.BlockSpec((tq, 128), lambda hp, i: (i, hp)), pl.BlockSpec((2, tq, 128), lambda hp, i: (hp, i, 0))] + g_out,
        out_shape=[jax.ShapeDtypeStruct((L, 1536), F32), jax.ShapeDtypeStruct((SB_HEADS, L, 128), F32)] + g_shapes,
        scratch_shapes=g_sems,
        compiler_params=pltpu.CompilerParams(dimension_semantics=("arbitrary", "arbitrary")),
    )(P, P, P, *gathers)


def sb_bwd(P, carries, dmixed, *, name, rides=()):
    L = P.shape[0]
    tq = _tq(L)
    nd = tq // BLK

    def body(q_ref, k_ref, v_ref, c_ref, do_ref, dq_ref, dk_ref, dv_ref):
        I = pl.program_id(1)

        @pl.when(I == 0)
        def _():
            dk_ref[...] = jnp.zeros_like(dk_ref)
            dv_ref[...] = jnp.zeros_like(dv_ref)

        tr = _tri2("right", 2)
        tl = _tri2("left", 1)
        lane_q = lax.broadcasted_iota(jnp.int32, (tq, 128), 1)
        first_k = lax.broadcasted_iota(jnp.int32, (BLK, 128), 1) < 64
        q = (q_ref[...] * 0.125).astype(BF16)
        do = do_ref[...].astype(BF16)

        def tiles(T, carry, kind):
            dq, PL0, PL1 = carry
            js = [T * nd + u for u in range(nd)]
            st = []
            for j in js:
                off = pl.multiple_of(j * BLK, BLK)
                kcat = jnp.concatenate(_head_split(k_ref[pl.ds(off, BLK), :].astype(BF16), first_k), axis=0)
                vcat = jnp.concatenate(_head_split(v_ref[pl.ds(off, BLK), :].astype(BF16), first_k), axis=0)
                st.append([off, kcat, _dot_nt(q, kcat), _dot_nt(do, vcat)])
            for u, (s, j) in enumerate(zip(st, js)):
                z = s[2]
                sp = _softplus(z)
                mask = _sb_mask2(I, j, tq) if kind == "diag" else (pad_ok if kind == "first" and u == 0 else None)
                spm = sp if mask is None else jnp.where(mask, sp, 0.0)
                s += [mask, jnp.exp(z - sp), _dot(_split2(spm), tr)]
            for s, j in zip(st, js):
                off, kcat, z, dw, mask, sig, S = s
                R0 = jnp.sum(jnp.where(lane_q == j, c_ref[0], 0.0), axis=1, keepdims=True)
                R1 = jnp.sum(jnp.where(lane_q == j, c_ref[1], 0.0), axis=1, keepdims=True)
                w = sig * jnp.exp(_per_head(-S, R0, R1))
                if mask is not None:
                    w = jnp.where(mask, w, 0.0)
                dA = dw * w
                dvf = _dot_tn(w.astype(BF16), do)
                dv_ref[pl.ds(off, BLK), :] += jnp.where(first_k, dvf[0:BLK], dvf[BLK:2 * BLK])
                s += [dA, _dot(dA.astype(BF16), tl)]
            for off, kcat, z, dw, mask, sig, S, dA, pre in st:
                dz = dA - sig * (dA + _per_head(pre, PL0, PL1))
                if mask is not None:
                    dz = jnp.where(mask, dz, 0.0)
                dzb = dz.astype(BF16)
                dkf = _dot_tn(dzb, q)
                dk_ref[pl.ds(off, BLK), :] += jnp.where(first_k, dkf[0:BLK], dkf[BLK:2 * BLK])
                dq = dq + _dot(dzb, kcat)
                PL0 = PL0 + (pre[:, BLK - 1:BLK] + dA[:, BLK - 1:BLK])
                PL1 = PL1 + (pre[:, 2 * BLK - 1:2 * BLK] + dA[:, 2 * BLK - 1:2 * BLK])
            return (dq, PL0, PL1)

        pad_ok = (lax.broadcasted_iota(jnp.int32, (tq, 256), 1) & (BLK - 1)) >= N_PAD
        carry = (jnp.zeros((tq, 128), F32), jnp.zeros((tq, 1), F32), jnp.zeros((tq, 1), F32))
        carry = lax.fori_loop(0, jnp.minimum(I, 1), lambda t, c: tiles(0, c, "first"), carry)
        carry = lax.fori_loop(1, jnp.maximum(I, 1), lambda T, c: tiles(T, c, None), carry)
        carry = tiles(I, carry, "diag")
        dq_ref[...] = carry[0] * 0.125

    blk = lambda c0: pl.BlockSpec((tq, 128), lambda hp, i: (i, c0 + hp))
    full = lambda c0: pl.BlockSpec((L, 128), lambda hp, i: (0, c0 + hp))
    sds = jax.ShapeDtypeStruct((L, 512), F32)
    nhp, nI = SB_HEADS // 2, L // tq
    r_in, r_out, r_shapes, r_sems = _ride_io(rides)
    return pl.pallas_call(
        _ride_kernel(body, 5, 3, rides, nhp, nI), name=name, grid=(nhp, nI),
        in_specs=[blk(C_SBQ // 128), full(C_SBK // 128), full(C_SBV // 128),
                  pl.BlockSpec((2, tq, 128), lambda hp, i: (hp, i, 0)), blk(0)] + r_in,
        out_specs=[blk(0), full(0), full(0)] + r_out, out_shape=[sds, sds, sds] + r_shapes, scratch_shapes=r_sems,
        compiler_params=pltpu.CompilerParams(dimension_semantics=("arbitrary", "arbitrary")),
    )(P, P, P, carries, dmixed, *[a for _, a in rides])


def _v2_sb_fwd(P, *, name):
    L = P.shape[0]
    tq = _tq(L)
    nd = tq // BLK

    def body(q_ref, k_ref, v_ref, o_ref, c_ref):
        I = pl.program_id(1)
        tt = _tri("right")
        lane_q = lax.broadcasted_iota(jnp.int32, (tq, 128), 1)
        first_k = lax.broadcasted_iota(jnp.int32, (BLK, 128), 1) < 64
        qm = [x.astype(BF16) for x in _head_split(q_ref[...] * 0.125, lane_q < 64)]
        c_ref[...] = jnp.zeros_like(c_ref)

        def tile(j, carry, masked):
            o, R = carry[0], carry[1:]
            off = pl.multiple_of(j * BLK, BLK)
            kb = k_ref[pl.ds(off, BLK), :].astype(BF16)
            vcat = jnp.concatenate(_head_split(v_ref[pl.ds(off, BLK), :].astype(BF16), first_k), axis=0)
            mask = _sb_mask(I, j, tq) if masked else None
            ws, Rn = [], []
            for h in range(2):
                z = _dot_nt(qm[h], kb)
                sp = _softplus(z)
                spm = jnp.where(mask, sp, 0.0) if masked else sp
                w = jnp.exp(z - sp - _tri_sum(spm, tt) + R[h])
                if masked:
                    w = jnp.where(mask, w, 0.0)
                c_ref[h] = jnp.where(lane_q == j, R[h], c_ref[h])
                ws.append(w.astype(BF16))
                Rn.append(R[h] - jnp.sum(spm, axis=1, keepdims=True))
            return (o + _dot(jnp.concatenate(ws, axis=1), vcat), Rn[0], Rn[1])

        carry = (jnp.zeros((tq, 128), F32), jnp.zeros((tq, 1), F32), jnp.zeros((tq, 1), F32))
        carry = lax.fori_loop(0, nd, lambda t, c: tile(I * nd + nd - 1 - t, c, True), carry)
        carry = lax.fori_loop(0, jnp.maximum(I * nd - 1, 0), lambda t, c: tile(I * nd - 1 - t, c, False), carry)
        carry = lax.fori_loop(0, jnp.minimum(I, 1), lambda t, c: tile(0, c, True), carry)
        o_ref[...] = carry[0]

    return pl.pallas_call(
        body, name=name, grid=(SB_HEADS // 2, L // tq),
        in_specs=[pl.BlockSpec((tq, 128), lambda hp, i: (i, C_SBQ // 128 + hp)),
                  pl.BlockSpec((L, 128), lambda hp, i: (0, C_SBK // 128 + hp)),
                  pl.BlockSpec((L, 128), lambda hp, i: (0, C_SBV // 128 + hp))],
        out_specs=(pl.BlockSpec((tq, 128), lambda hp, i: (i, hp)), pl.BlockSpec((2, tq, 128), lambda hp, i: (hp, i, 0))),
        out_shape=(jax.ShapeDtypeStruct((L, 512), F32), jax.ShapeDtypeStruct((SB_HEADS, L, 128), F32)),
        compiler_params=pltpu.CompilerParams(dimension_semantics=("parallel", "arbitrary")),
    )(P, P, P)


def _v2_sb_bwd(P, carries, dmixed, *, name):
    L = P.shape[0]
    tq = _tq(L)
    nd = tq // BLK

    def body(q_ref, k_ref, v_ref, c_ref, do_ref, dq_ref, dk_ref, dv_ref):
        I = pl.program_id(1)

        @pl.when(I == 0)
        def _():
            dk_ref[...] = jnp.zeros_like(dk_ref)
            dv_ref[...] = jnp.zeros_like(dv_ref)

        tr = _tri("right")
        tl = _tri("left")
        lane_q = lax.broadcasted_iota(jnp.int32, (tq, 128), 1)
        first_k = lax.broadcasted_iota(jnp.int32, (BLK, 128), 1) < 64
        qm = [x.astype(BF16) for x in _head_split(q_ref[...] * 0.125, lane_q < 64)]
        dom = [x.astype(BF16) for x in _head_split(do_ref[...], lane_q < 64)]
        qcat = jnp.concatenate(qm, axis=0)
        docat = jnp.concatenate(dom, axis=0)

        def tile(j, carry, masked):
            dq, PL = carry[0], carry[1:]
            off = pl.multiple_of(j * BLK, BLK)
            kb = k_ref[pl.ds(off, BLK), :].astype(BF16)
            vb = v_ref[pl.ds(off, BLK), :].astype(BF16)
            kcat = jnp.concatenate(_head_split(kb, first_k), axis=0)
            mask = _sb_mask(I, j, tq) if masked else None
            dzs, wsb, PLn = [], [], []
            for h in range(2):
                R = jnp.sum(jnp.where(lane_q == j, c_ref[h], 0.0), axis=1, keepdims=True)
                z = _dot_nt(qm[h], kb)
                sp = _softplus(z)
                spm = jnp.where(mask, sp, 0.0) if masked else sp
                sig = jnp.exp(z - sp)
                w = sig * jnp.exp(R - _tri_sum(spm, tr))
                if masked:
                    w = jnp.where(mask, w, 0.0)
                dA = _dot_nt(dom[h], vb) * w
                dz = dA - sig * (dA + _tri_sum(dA, tl) + PL[h])
                if masked:
                    dz = jnp.where(mask, dz, 0.0)
                dzs.append(dz.astype(BF16))
                wsb.append(w.astype(BF16))
                PLn.append(PL[h] + jnp.sum(dA, axis=1, keepdims=True))
            dk_ref[pl.ds(off, BLK), :] += _dot_tn(jnp.concatenate(dzs, axis=0), qcat)
            dv_ref[pl.ds(off, BLK), :] += _dot_tn(jnp.concatenate(wsb, axis=0), docat)
            return (dq + _dot(jnp.concatenate(dzs, axis=1), kcat), PLn[0], PLn[1])

        carry = (jnp.zeros((tq, 128), F32), jnp.zeros((tq, 1), F32), jnp.zeros((tq, 1), F32))
        carry = lax.fori_loop(0, jnp.minimum(I, 1), lambda t, c: tile(0, c, True), carry)
        carry = lax.fori_loop(1, jnp.maximum(I * nd, 1), lambda j, c: tile(j, c, False), carry)
        carry = lax.fori_loop(0, nd, lambda t, c: tile(I * nd + t, c, True), carry)
        dq_ref[...] = carry[0] * 0.125

    blk = lambda c0: pl.BlockSpec((tq, 128), lambda hp, i: (i, c0 + hp))
    full = lambda c0: pl.BlockSpec((L, 128), lambda hp, i: (0, c0 + hp))
    sds = jax.ShapeDtypeStruct((L, 512), F32)
    return pl.pallas_call(
        body, name=name, grid=(SB_HEADS // 2, L // tq),
        in_specs=[blk(C_SBQ // 128), full(C_SBK // 128), full(C_SBV // 128),
                  pl.BlockSpec((2, tq, 128), lambda hp, i: (hp, i, 0)), blk(0)],
        out_specs=(blk(0), full(0), full(0)), out_shape=(sds, sds, sds),
        compiler_params=pltpu.CompilerParams(dimension_semantics=("parallel", "arbitrary")),
    )(P, P, P, carries, dmixed)


def _mla_mask2(I, j, tq):
    row = lax.broadcasted_iota(jnp.int32, (tq, tq), 0)
    col = lax.broadcasted_iota(jnp.int32, (tq, tq), 1)
    t_idx = I * tq + row
    s_idx = j * tq + col
    return (s_idx <= t_idx) & ((s_idx >= N_PAD) | (s_idx == t_idx))


def _mla_qcat(q_ref, cs_ref, sn_ref, lane_q):
    qn = q_ref[:, 0:128]
    qr = _rope(q_ref[:, 128:256], cs_ref[...], sn_ref[...], MLA_ROPE // 2)
    zero = jnp.zeros_like(qn)
    r0 = lane_q < MLA_ROPE
    r1 = (lane_q >= MLA_ROPE) & (lane_q < 2 * MLA_ROPE)
    n0, n1 = _head_split(qn, lane_q < 64)
    return [jnp.concatenate([n0, jnp.where(r0, qr, zero)], axis=1).astype(BF16),
            jnp.concatenate([n1, jnp.where(r1, qr, zero)], axis=1).astype(BF16)]


def mla_fwd(Q, KV, KR, cs, sn, mixed, *, name, gathers=()):
    L = Q.shape[0]
    tq = _tq(L)

    def body(q_ref, kn_ref, v_ref, kr_ref, cs_ref, sn_ref, o_ref, lse_ref):
        I = pl.program_id(1)
        lane_q = lax.broadcasted_iota(jnp.int32, (tq, 128), 1)
        first_q = lane_q < 64
        qcat = _mla_qcat(q_ref, cs_ref, sn_ref, lane_q)

        def tile(j, carry, masked, wide=1):
            acc, ml = carry[0], carry[1:]
            off = pl.multiple_of(j * tq, tq)
            tk = wide * tq
            first_k = lax.broadcasted_iota(jnp.int32, (tk, 128), 1) < 64
            kcat = jnp.concatenate([kn_ref[pl.ds(off, tk), :], kr_ref[pl.ds(off, tk), :]], axis=1)
            vcat = jnp.concatenate(_head_split(v_ref[pl.ds(off, tk), :], first_k), axis=0)
            mask = _mla_mask2(I, j, tq) if masked else None
            ps, al, out = [], [], []
            for h in range(2):
                m, l = ml[2 * h], ml[2 * h + 1]
                s = _dot_nt(qcat[h], kcat) * MLA_SCALE
                if masked:
                    s = jnp.where(mask, s, NEG)
                m_new = jnp.maximum(m, jnp.max(s, axis=1, keepdims=True))
                a = jnp.exp(m - m_new)
                p = jnp.exp(s - m_new)
                ps.append(p.astype(BF16))
                al.append(a)
                out += [m_new, a * l + jnp.sum(p, axis=1, keepdims=True)]
            acc = acc * jnp.where(first_q, al[0], al[1]) + _dot(jnp.concatenate(ps, axis=1), vcat)
            return (acc,) + tuple(out)

        ml0 = (jnp.full((tq, 1), NEG, F32), jnp.zeros((tq, 1), F32))
        carry = (jnp.zeros((tq, 128), F32),) + ml0 + ml0
        carry = lax.fori_loop(0, jnp.minimum(I, 1), lambda t, c: tile(0, c, True), carry)
        n_in = jnp.maximum(I - 1, 0)
        carry = lax.fori_loop(0, n_in // 2, lambda t, c: tile(1 + 2 * t, c, False, 2), carry)
        carry = lax.fori_loop(0, n_in % 2, lambda t, c: tile(I - 1, c, False), carry)
        carry = tile(I, carry, True)
        acc, m0, l0, m1, l1 = carry
        o_ref[...] = acc / jnp.where(first_q, l0, l1)
        lse_ref[0] = jnp.where(lane_q == 0, m0 + jnp.log(l0), jnp.where(lane_q == 1, m1 + jnp.log(l1), 0.0))

    ng = len(gathers)
    nhp, nI = MLA_HEADS // 2, L // tq

    def kern(q, kn, v, kr, c, s, mixed_any, *rest):
        w_refs, (o, lse), g_refs, sems = rest[:ng], rest[ng:ng + 2], rest[ng + 2:2 * ng + 2], rest[2 * ng + 2:]
        hp, I = pl.program_id(0), pl.program_id(1)
        _gather_ride(w_refs, g_refs, sems, (hp == 0) & (I == 0), (hp == nhp - 1) & (I == 0), (hp == nhp - 1) & (I == nI - 1))
        body(q, kn, v, kr, c, s, o, lse)

    g_in, g_out, g_shapes, g_sems = _gather_io(gathers)
    return pl.pallas_call(
        kern, name=name, grid=(nhp, nI),
        in_specs=[pl.BlockSpec((tq, 256), lambda hp, i: (i, hp)),
                  pl.BlockSpec((L, 128), lambda hp, i: (0, hp)),
                  pl.BlockSpec((L, 128), lambda hp, i: (0, 4 + hp)),
                  pl.BlockSpec((L, 128), lambda hp, i: (0, 0)),
                  pl.BlockSpec((tq, 128), lambda hp, i: (i, 0)), pl.BlockSpec((tq, 128), lambda hp, i: (i, 0)),
                  pl.BlockSpec(memory_space=pl.ANY)] + g_in,
        out_specs=[pl.BlockSpec((tq, 128), lambda hp, i: (i, 4 + hp)), pl.BlockSpec((1, tq, 128), lambda hp, i: (hp, i, 0))] + g_out,
        out_shape=[jax.ShapeDtypeStruct(mixed.shape, F32), jax.ShapeDtypeStruct((4, L, 128), F32)] + g_shapes,
        input_output_aliases={6: 0}, scratch_shapes=g_sems,
        compiler_params=pltpu.CompilerParams(dimension_semantics=("arbitrary", "arbitrary")),
    )(Q, KV, KV, KR, cs, sn, mixed, *gathers)


def mla_bwd(Q, KV, KR, cs, sn, mixed, dmixed, lse, *, name, rides=()):
    L = Q.shape[0]
    tq = _tq(L)

    def body(q_ref, kn_ref, v_ref, kr_ref, cs_ref, sn_ref, o_ref, do_ref, lse_ref, dq_ref, dkn_ref, dv_ref, dkr_ref):
        hp = pl.program_id(0)
        I = pl.program_id(1)

        @pl.when(I == 0)
        def _():
            dkn_ref[...] = jnp.zeros_like(dkn_ref)
            dv_ref[...] = jnp.zeros_like(dv_ref)

        @pl.when((I == 0) & (hp == 0))
        def _():
            dkr_ref[...] = jnp.zeros_like(dkr_ref)

        lane_q = lax.broadcasted_iota(jnp.int32, (tq, 128), 1)
        first_q = lane_q < 64
        qcat = _mla_qcat(q_ref, cs_ref, sn_ref, lane_q)
        qq = jnp.concatenate(qcat, axis=0)
        do = do_ref[...]
        prod = do * o_ref[...]
        dd = [jnp.sum(jnp.where(first_q, prod, 0.0), axis=1, keepdims=True),
              jnp.sum(jnp.where(first_q, 0.0, prod), axis=1, keepdims=True)]
        dom = [x.astype(BF16) for x in _head_split(do, first_q)]
        docat = jnp.concatenate(dom, axis=0)
        lses = [lse_ref[0, :, 0:1], lse_ref[0, :, 1:2]]

        def tile(j, dq, masked, wide=1):
            off = pl.multiple_of(j * tq, tq)
            tk = wide * tq
            lane_k = lax.broadcasted_iota(jnp.int32, (tk, 256), 1)
            sel0 = (lane_k < 64) | ((lane_k >= 128) & (lane_k < 128 + MLA_ROPE))
            sel1 = ((lane_k >= 64) & (lane_k < 128)) | ((lane_k >= 128 + MLA_ROPE) & (lane_k < 128 + 2 * MLA_ROPE))
            kcat = jnp.concatenate([kn_ref[pl.ds(off, tk), :], kr_ref[pl.ds(off, tk), :]], axis=1)
            vb = v_ref[pl.ds(off, tk), :]
            zero = jnp.zeros_like(kcat)
            kk = jnp.concatenate([jnp.where(sel0, kcat, zero), jnp.where(sel1, kcat, zero)], axis=0)
            mask = _mla_mask2(I, j, tq) if masked else None
            dss, pbs = [], []
            for h in range(2):
                s = _dot_nt(qcat[h], kcat) * MLA_SCALE
                p = jnp.exp(s - lses[h])
                if masked:
                    p = jnp.where(mask, p, 0.0)
                dp = _dot_nt(dom[h], vb)
                dss.append((p * (dp - dd[h]) * MLA_SCALE).astype(BF16))
                pbs.append(p.astype(BF16))
            dkc = _dot_tn(jnp.concatenate(dss, axis=0), qq)
            dkn_ref[pl.ds(off, tk), :] += dkc[:, 0:128]
            dkr_ref[pl.ds(off, tk), :] += dkc[:, 128:256]
            dv_ref[pl.ds(off, tk), :] += _dot_tn(jnp.concatenate(pbs, axis=0), docat)
            return dq + _dot(jnp.concatenate(dss, axis=1), kk)

        dq = jnp.zeros((tq, 256), F32)
        dq = lax.fori_loop(0, jnp.minimum(I, 1), lambda t, c: tile(0, c, True), dq)
        n_in = jnp.maximum(I - 1, 0)
        dq = lax.fori_loop(0, n_in // 2, lambda t, c: tile(1 + 2 * t, c, False, 2), dq)
        dq = lax.fori_loop(0, n_in % 2, lambda t, c: tile(I - 1, c, False), dq)
        dq = tile(I, dq, True)
        dq_ref[:, 0:128] = dq[:, 0:128]
        dq_ref[:, 128:256] = _rope_t(dq[:, 128:256], cs_ref[...], sn_ref[...], MLA_ROPE // 2)

    blk = lambda c0: pl.BlockSpec((tq, 128), lambda hp, i: (i, c0 + hp))
    full = lambda c0: pl.BlockSpec((L, 128), lambda hp, i: (0, c0 + hp))
    tab = pl.BlockSpec((tq, 128), lambda hp, i: (i, 0))
    nhp, nI = MLA_HEADS // 2, L // tq
    r_in, r_out, r_shapes, r_sems = _ride_io(rides)
    return pl.pallas_call(
        _ride_kernel(body, 9, 4, rides, nhp, nI), name=name, grid=(nhp, nI),
        in_specs=[pl.BlockSpec((tq, 256), lambda hp, i: (i, hp)), full(0), full(4),
                  pl.BlockSpec((L, 128), lambda hp, i: (0, 0)), tab, tab, blk(4), blk(4),
                  pl.BlockSpec((1, tq, 128), lambda hp, i: (hp, i, 0))] + r_in,
        out_specs=[pl.BlockSpec((tq, 256), lambda hp, i: (i, hp)), full(0), full(0),
                   pl.BlockSpec((L, 128), lambda hp, i: (0, 0))] + r_out,
        out_shape=[jax.ShapeDtypeStruct((L, 1024), F32), jax.ShapeDtypeStruct((L, 512), F32),
                   jax.ShapeDtypeStruct((L, 512), F32), jax.ShapeDtypeStruct((L, 128), F32)] + r_shapes,
        scratch_shapes=r_sems,
        compiler_params=pltpu.CompilerParams(dimension_semantics=("arbitrary", "arbitrary")),
    )(Q, KV, KV, KR, cs, sn, mixed, dmixed, lse, *[a for _, a in rides])


def _ret_decay(h):
    lg = RET_LOG_G[h]
    r = lax.broadcasted_iota(jnp.int32, (BLK, BLK), 0)
    c = lax.broadcasted_iota(jnp.int32, (BLK, BLK), 1)
    diff = (r - c).astype(F32)
    d_in = jnp.where(diff >= 0, jnp.exp(jnp.maximum(diff, 0.0) * lg), 0.0)
    idx = lax.broadcasted_iota(jnp.int32, (BLK, 1), 0).astype(F32)
    q_decay = jnp.exp((idx + 1.0) * lg)
    k_decay = jnp.exp((BLK - 1.0 - idx) * lg)
    c_decay = math.exp(BLK * lg)
    return d_in, q_decay, k_decay, c_decay


def _ret_qk(qk_ref, cs_ref, sn_ref, n):
    cs = jnp.concatenate([cs_ref[...]] * 2, axis=1)
    sn = jnp.concatenate([sn_ref[...]] * 2, axis=1)
    rq = _rope(qk_ref[:, 0:256], cs, sn, RET_QK // 2)
    row = n * BLK + lax.broadcasted_iota(jnp.int32, (BLK, 256), 0)
    kmul = jnp.where(row >= N_PAD, 0.125, 0.0)
    rk = _rope(qk_ref[:, 256:512], cs, sn, RET_QK // 2) * kmul
    return rq, rk, cs, sn, kmul


def _head_norm(y):
    mu = jnp.mean(y, axis=-1, keepdims=True)
    yc = y - mu
    r = lax.rsqrt(jnp.mean(jnp.square(yc), axis=-1, keepdims=True) + LN_EPS)
    return yc * r, r


def ret_fwd(P, cs, sn, mixed, *, name):
    L = P.shape[0]
    nb = L // BLK

    def body(qk_ref, v_ref, g_ref, cs_ref, sn_ref, o_ref, y_ref, st_ref, state):
        n = pl.program_id(0)

        @pl.when(n == 0)
        def _():
            state[...] = jnp.zeros_like(state)

        st_ref[0] = state[...]
        rq, rk, _, _, _ = _ret_qk(qk_ref, cs_ref, sn_ref, n)
        outs, ys = [], []
        for h in range(RET_HEADS):
            d_in, q_decay, k_decay, c_decay = _ret_decay(h)
            q = rq[:, 64 * h:64 * h + 64].astype(BF16)
            kf = rk[:, 64 * h:64 * h + 64]
            v = v_ref[:, 128 * h:128 * h + 128].astype(BF16)
            S = state[h]
            inner = _dot_nt(q, kf.astype(BF16)) * d_in
            y = _dot(inner.astype(BF16), v) + _dot(q, S.astype(BF16)) * q_decay
            state[h] = S * c_decay + _dot_tn((kf * k_decay).astype(BF16), v)
            g = g_ref[:, 128 * h:128 * h + 128]
            ys.append(y)
            outs.append(g * jax.nn.sigmoid(g) * _head_norm(y)[0])
        o_ref[...] = jnp.concatenate(outs, axis=1)
        y_ref[...] = jnp.concatenate(ys, axis=1)

    blk512 = lambda c: pl.BlockSpec((BLK, 512), lambda n: (n, c))
    tab = pl.BlockSpec((BLK, 128), lambda n: (n, 0))
    return pl.pallas_call(
        lambda qk, v, g, c, s, mixed_any, o, y, st, state: body(qk, v, g, c, s, o, y, st, state),
        name=name, grid=(nb,),
        in_specs=[blk512(C_RQ // 512), blk512(C_RV // 512), blk512(C_RG // 512), tab, tab, pl.BlockSpec(memory_space=pl.ANY)],
        out_specs=(blk512(2), blk512(0), pl.BlockSpec((1, RET_HEADS, RET_QK, RET_V), lambda n: (n, 0, 0, 0))),
        out_shape=(jax.ShapeDtypeStruct(mixed.shape, F32), jax.ShapeDtypeStruct((L, 512), F32),
                   jax.ShapeDtypeStruct((nb, RET_HEADS, RET_QK, RET_V), F32)),
        input_output_aliases={5: 0},
        scratch_shapes=[pltpu.VMEM((RET_HEADS, RET_QK, RET_V), F32)],
        compiler_params=pltpu.CompilerParams(dimension_semantics=("arbitrary",)),
    )(P, P, P, cs, sn, mixed)


def ret_bwd(P, y, states, dmixed, cs, sn, *, name):
    L = P.shape[0]
    nb = L // BLK

    def body(qk_ref, v_ref, g_ref, y_ref, st_ref, do_ref, cs_ref, sn_ref, dqk_ref, dv_ref, dg_ref, dstate):
        n = nb - 1 - pl.program_id(0)

        @pl.when(pl.program_id(0) == 0)
        def _():
            dstate[...] = jnp.zeros_like(dstate)

        rq, rk, cs, sn, kmul = _ret_qk(qk_ref, cs_ref, sn_ref, n)
        dqs, dks, dvs, dgs = [], [], [], []
        for h in range(RET_HEADS):
            d_in, q_decay, k_decay, c_decay = _ret_decay(h)
            sv = slice(128 * h, 128 * h + 128)
            q = rq[:, 64 * h:64 * h + 64].astype(BF16)
            kf = rk[:, 64 * h:64 * h + 64]
            k = kf.astype(BF16)
            kd = (kf * k_decay).astype(BF16)
            v = v_ref[:, sv].astype(BF16)
            g = g_ref[:, sv]
            do = do_ref[:, sv]
            yh = y_ref[:, sv]
            S = st_ref[0, h].astype(BF16)
            dS = dstate[h]
            sg = jax.nn.sigmoid(g)
            yn, r = _head_norm(yh)
            dgs.append(do * yn * (sg * (1.0 + g * (1.0 - sg))))
            dyn = do * (g * sg)
            dy = r * (dyn - jnp.mean(dyn, axis=-1, keepdims=True) - yn * jnp.mean(dyn * yn, axis=-1, keepdims=True))
            dyb = dy.astype(BF16)
            dyq = (dy * q_decay).astype(BF16)
            inner = (_dot_nt(q, k) * d_in).astype(BF16)
            A = (_dot_nt(dyb, v) * d_in).astype(BF16)
            dSb = dS.astype(BF16)
            dqs.append(_dot(A, k) + _dot_nt(dyq, S))
            dks.append(_dot_tn(A, q) + _dot_nt(v, dSb) * k_decay)
            dvs.append(_dot_tn(inner, dyb) + _dot(kd, dSb))
            dstate[h] = dS * c_decay + _dot_tn(q, dyq)
        drq = _rope_t(jnp.concatenate(dqs, axis=1), cs, sn, RET_QK // 2)
        drk = _rope_t(jnp.concatenate(dks, axis=1) * kmul, cs, sn, RET_QK // 2)
        dqk_ref[...] = jnp.concatenate([drq, drk], axis=1)
        dv_ref[...] = jnp.concatenate(dvs, axis=1)
        dg_ref[...] = jnp.concatenate(dgs, axis=1)

    blk512 = lambda c: pl.BlockSpec((BLK, 512), lambda t: (nb - 1 - t, c))
    tab = pl.BlockSpec((BLK, 128), lambda t: (nb - 1 - t, 0))
    sds = jax.ShapeDtypeStruct((L, 512), F32)
    return pl.pallas_call(
        body, name=name, grid=(nb,),
        in_specs=[blk512(C_RQ // 512), blk512(C_RV // 512), blk512(C_RG // 512), blk512(0),
                  pl.BlockSpec((1, RET_HEADS, RET_QK, RET_V), lambda t: (nb - 1 - t, 0, 0, 0)), blk512(2), tab, tab],
        out_specs=(blk512(0), blk512(0), blk512(0)), out_shape=(sds, sds, sds),
        scratch_shapes=[pltpu.VMEM((RET_HEADS, RET_QK, RET_V), F32)],
        compiler_params=pltpu.CompilerParams(dimension_semantics=("arbitrary",)),
    )(P, P, P, y, states, dmixed, cs, sn)


def _perm_w_in(w):
    pad = jnp.zeros(w.shape[:-1] + (N_INP - N_IN,), w.dtype)
    return jnp.concatenate([w[..., 0:1536], w[..., 2208:3744], w[..., 1536:2208], pad], axis=-1)


def _unperm_w_in(g):
    return jnp.concatenate([g[..., 0:1536], g[..., 3072:3744], g[..., 1536:3072]], axis=-1)


def _perm_w_uq(w):
    lead = w.shape[:-1]
    w5 = w.reshape(lead + (4, 2, 96))
    nope = w5[..., :64].reshape(lead + (4, 128))
    rope = w5[..., 64:].reshape(lead + (4, 64))
    return jnp.concatenate([nope, rope, jnp.zeros(lead + (4, 64), w.dtype)], axis=-1).reshape(lead + (1024,))


def _unperm_w_uq(g):
    lead = g.shape[:-1]
    g4 = g.reshape(lead + (4, 256))
    nope = g4[..., :128].reshape(lead + (4, 2, 64))
    rope = g4[..., 128:192].reshape(lead + (4, 2, 32))
    return jnp.concatenate([nope, rope], axis=-1).reshape(lead + (768,))


def _perm_w_ukv(w):
    lead = w.shape[:-1]
    w4 = w.reshape(lead + (8, 128))
    return jnp.concatenate([w4[..., :64].reshape(lead + (512,)), w4[..., 64:].reshape(lead + (512,))], axis=-1)


def _unperm_w_ukv(g):
    lead = g.shape[:-1]
    return jnp.concatenate([g[..., :512].reshape(lead + (8, 64)), g[..., 512:].reshape(lead + (8, 64))],
                           axis=-1).reshape(lead + (1024,))


def _col_shards(a):
    return jnp.moveaxis(a.reshape(a.shape[0], a.shape[1], 4, a.shape[2] // 4), 2, 0)


_RS_SHAPES = {"w_in": (D_MODEL, N_IN // 4), "w_uq": (MLA_Q_LORA, 192), "w_ukv": (MLA_KV_LORA, 256),
              "w_out": (384, D_MODEL), "w_ff1": (D_MODEL, D_FF // 4), "w_ff2": (D_FF // 4, D_MODEL)}


def _rope_tables(L, half):
    pos = (jnp.arange(L) - N_PAD).astype(F32)
    inv = ROPE_THETA ** (-jnp.arange(half, dtype=F32) / half)
    ang = pos[:, None] * inv[None, :]
    cos, sin = jnp.cos(ang), jnp.sin(ang)
    reps = 128 // (2 * half)
    cs = jnp.tile(jnp.concatenate([cos, cos], axis=1), (1, reps))
    sn = jnp.tile(jnp.concatenate([-sin, sin], axis=1), (1, reps))
    return cs, sn


def _device_step(x, target, meta, ln_emb_g, ln_emb_b, w_in, q_norm, kv_norm, w_uq, w_ukv, w_out,
                 ln1_g, ln1_b, w_ff1, w_ff2, ln2_g, ln2_b, late=None):
    S = x.shape[0]
    L = S + BLK
    depth = w_in.shape[0]
    cs_m, sn_m = _rope_tables(L, MLA_ROPE // 2)
    cs_r, sn_r = _rope_tables(L, RET_QK // 2)
    hcat = jnp.concatenate([jnp.zeros((N_PAD, D_MODEL), F32), meta, x], axis=0)
    h, hb, _ = ln_fwd(hcat, ln_emb_g, ln_emb_b, name="ln_emb_fwd")

    w_in_sb, w_in_rest = w_in[..., :N_SB], w_in[..., N_SB:]

    def own_slot(gathered, shard):
        return jnp.moveaxis(lax.dynamic_update_slice(gathered, shard[None], (late["s0"], 0, 0, 0)), 0, 1)

    saved = []
    for l in range(depth):
        Psb = mm_nn(hb, w_in_sb[l], tn=N_SB, name=f"in_proj_sb_{l}", out_dtype=BF16)
        P = mm_nn(hb, w_in_rest[l], tn=768, name=f"in_proj_{l}")
        if late is not None and l == 0:
            mixed, sbc, ga, gb = sb_fwd(Psb, name=f"sb_fwd_{l}", gathers=(late["w_out"], late["w_ff1"]))
            w_out, w_ff1 = own_slot(ga, late["w_out"]), own_slot(gb, late["w_ff1"])
        else:
            mixed, sbc = sb_fwd(Psb, name=f"sb_fwd_{l}")
        nq, nkv, KR = mla_pre_fwd(P, q_norm[l], kv_norm[l], cs_m, sn_m, name=f"mla_pre_fwd_{l}")
        Q = mm_nn(nq, w_uq[l], tn=512, name=f"uq_{l}")
        KV = mm_nn(nkv, w_ukv[l], tn=512, name=f"ukv_{l}", out_dtype=BF16)
        if late is not None and l == 0:
            mixed, lse, ga = mla_fwd(Q, KV, KR, cs_m, sn_m, mixed, name=f"mla_fwd_{l}", gathers=(late["w_ff2"],))
            w_ff2 = own_slot(ga, late["w_ff2"])
        else:
            mixed, lse = mla_fwd(Q, KV, KR, cs_m, sn_m, mixed, name=f"mla_fwd_{l}")
        mixed, y, states = ret_fwd(P, cs_r, sn_r, mixed, name=f"ret_fwd_{l}")
        w_out_l = w_out[l].reshape(1, 1536, D_MODEL)
        mix = mm_nn(mixed, w_out_l, tn=1024, name=f"out_proj_{l}")
        h1, h1b, z1 = ln_fwd(mix, ln1_g[l], ln1_b[l], res=h, name=f"ln1_fwd_{l}")
        U = mm_nn(h1b, w_ff1[l], tn=1024, name=f"ff1_{l}")
        w_ff2_l = w_ff2[l].reshape(1, D_FF, D_MODEL)
        mlp = mm_nn(U, w_ff2_l, tn=1024, tk=2048, prologue="relu2", name=f"ff2_{l}")
        h2, h2b, z2 = ln_fwd(mlp, ln2_g[l], ln2_b[l], res=h1, name=f"ln2_fwd_{l}")
        saved.append((hb, Psb, P, sbc, nq, nkv, KR, Q, KV, lse, y, states, mixed, z1, h1b, U, z2))
        h, hb = h2, h2b

    loss_t, dh = loss_fwd_bwd(h, target, name="loss")

    grads = {k: [None] * depth for k in ("q_norm", "kv_norm", "ln1_g", "ln1_b", "ln2_g", "ln2_b")}
    pairs = depth // 2
    g_ff1 = [lax.empty((4, 2, D_MODEL, D_FF // 4), F32) for _ in range(pairs)]
    g_ff2 = [lax.empty((4, 2, D_FF // 4, D_MODEL), F32) for _ in range(pairs)]
    g_out = [lax.empty((4, 2, 384, D_MODEL), F32) for _ in range(pairs)]
    g_in = [lax.empty((2, D_MODEL, N_INP), F32) for _ in range(pairs)]
    g_uq = [lax.empty((2, MLA_Q_LORA, 1024), F32) for _ in range(pairs)]
    g_ukv = [lax.empty((2, MLA_KV_LORA, 1024), F32) for _ in range(pairs)]

    def pair_grads(p):
        return {"w_in": _col_shards(_unperm_w_in(g_in[p])), "w_uq": _col_shards(_unperm_w_uq(g_uq[p])),
                "w_ukv": _col_shards(_unperm_w_ukv(g_ukv[p])), "w_out": g_out[p], "w_ff1": g_ff1[p], "w_ff2": g_ff2[p]}

    early = late is not None and depth == 4
    if late is not None:
        acc = {k: lax.empty((depth,) + s, F32) for k, s in _RS_SHAPES.items()}
    for l in reversed(range(depth)):
        p, lp = l // 2, l % 2
        hb_in, Psb, P, sbc, nq, nkv, KR, Q, KV, lse, y, states, mixed, z1, h1b, U, z2 = saved[l]
        dz2, grads["ln2_g"][l], grads["ln2_b"][l] = ln_bwd(dh, z2, ln2_g[l], name=f"ln2_bwd_{l}")
        w_ff2_l = w_ff2[l].reshape(1, D_FF, D_MODEL)
        g_ff2[p] = mm_tn(U, dz2, shards=1, tko=1024, tn=1024, prologue="relu2", name=f"ff2_dw_{l}", into=(g_ff2[p], lp, "rows"))
        dU = mm_nt(dz2, w_ff2_l, tn=1024, tko=1024, relu2grad=U, name=f"ff2_dx_{l}", out_dtype=BF16)
        g_ff1[p] = mm_tn(h1b, dU, shards=4, tko=1024, tn=1024, name=f"ff1_dw_{l}", into=(g_ff1[p], lp, "cols"))
        dh1 = mm_nt(dU, w_ff1[l], tn=1024, tko=1024, axpy=(dz2, DN_ALPHA), name=f"ff1_dx_{l}")
        dz1, grads["ln1_g"][l], grads["ln1_b"][l] = ln_bwd(dh1, z1, ln1_g[l], name=f"ln1_bwd_{l}")
        w_out_l = w_out[l].reshape(1, 1536, D_MODEL)
        g_out[p] = mm_tn(mixed, dz1, shards=1, tko=384, tn=1024, name=f"out_dw_{l}", into=(g_out[p], lp, "rows"))
        dmixed = mm_nt(dz1, w_out_l, tn=1024, tko=1536, name=f"out_dx_{l}")
        d_rqk, d_rv, d_rg = ret_bwd(P, y, states, dmixed, cs_r, sn_r, name=f"ret_bwd_{l}")
        if early and l == 1:
            GA = pair_grads(1)
            dQ, dKN, dV, dKR, *Bs = mla_bwd(Q, KV, KR, cs_m, sn_m, mixed, dmixed, lse, name=f"mla_bwd_{l}",
                                            rides=[("sib", GA[k]) for k in _RS_SHAPES])
        else:
            dQ, dKN, dV, dKR = mla_bwd(Q, KV, KR, cs_m, sn_m, mixed, dmixed, lse, name=f"mla_bwd_{l}")
        dKV = jnp.concatenate([dKN, dV], axis=1)
        g_uq[p] = mm_tn(nq, dQ, shards=1, tko=MLA_Q_LORA, tn=512, name=f"uq_dw_{l}", into=(g_uq[p], lp, "layer"))
        g_ukv[p] = mm_tn(nkv, dKV, shards=1, tko=MLA_KV_LORA, tn=512, name=f"ukv_dw_{l}", into=(g_ukv[p], lp, "layer"))
        dnq = mm_nt(dQ, w_uq[l], tn=1024, tko=MLA_Q_LORA, name=f"uq_dx_{l}")
        dnkv = mm_nt(dKV, w_ukv[l], tn=1024, tko=MLA_KV_LORA, name=f"ukv_dx_{l}")
        d_lat, grads["q_norm"][l], grads["kv_norm"][l] = mla_pre_bwd(P, dnq, dnkv, dKR, q_norm[l], kv_norm[l], cs_m, sn_m,
                                                                     name=f"mla_pre_bwd_{l}")
        if early and l == 1:
            As = [add_halves(GA[k], B, late["c"], name=f"rs_add1_hi_{k}") for k, B in zip(_RS_SHAPES, Bs)]
            dq_sb, dk_sb, dv_sb, *Bcs = sb_bwd(Psb, sbc, dmixed, name=f"sb_bwd_{l}", rides=[("chips", A) for A in As])
            for k, B, Bc in zip(_RS_SHAPES, Bs, Bcs):
                acc[k] = reduce_scatter_finish(GA[k], B, Bc, late["c"], late["s0"], acc[k], 2, tag=f"hi_{k}")
        else:
            dq_sb, dk_sb, dv_sb = sb_bwd(Psb, sbc, dmixed, name=f"sb_bwd_{l}")
        dP = jnp.concatenate([dq_sb, dk_sb, dv_sb, d_rqk, d_rv, d_rg, d_lat], axis=1).astype(BF16)
        g_in[p] = mm_tn(hb_in, dP, shards=1, tko=1024, tn=1280, name=f"in_dw_{l}", into=(g_in[p], lp, "layer"))
        dh = mm_nt(dP, w_in[l], tn=1920, tko=1024, axpy=(dz1, DN_ALPHA), name=f"in_dx_{l}")

    dhcat, dg_emb, db_emb = ln_bwd(dh, hcat, ln_emb_g, name="ln_emb_bwd")
    out = {k: jnp.stack(v) for k, v in grads.items()}
    if late is None:
        out.update({k: jnp.concatenate([pair_grads(p)[k] for p in range(pairs)], axis=1) for k in _RS_SHAPES})
    else:
        for p in range(pairs):
            if not (early and p == 1):
                for k, G in pair_grads(p).items():
                    acc[k] = reduce_scatter_weight(G, late["c"], late["s0"], acc[k], 2 * p, tag=f"{p}_{k}")
        out.update(acc)
    out["ln_emb_g"], out["ln_emb_b"] = dg_emb, db_emb
    out["meta"] = dhcat[N_PAD:BLK]
    return loss_t[0, 0], dhcat[BLK:], out


MESH = pl.DeviceIdType.MESH
PEER_XOR = (2, 1, 3)
_HBM = pl.BlockSpec(memory_space=pltpu.HBM)


def _place():
    x, y, c = lax.axis_index("x"), lax.axis_index("y"), lax.axis_index("c")
    peers = [(1 - x, y, c), (x, 1 - y, c), (1 - x, 1 - y, c)]
    return x, y, c, 2 * x + y, peers, (x, y, 1 - c)


def _gather_plan(w_ref, out_ref, send_sems, recv_sems, base):
    x, y, c, s0, peers, sibling = _place()
    hl = w_ref.shape[0] // 2

    def piece(s, half):
        return out_ref.at[s, pl.ds(half * hl, hl)]

    def copy(k, s, half, to, src=None):
        return pltpu.make_async_remote_copy(src_ref=piece(s, half) if src is None else src, dst_ref=piece(s, half),
                                            send_sem=send_sems.at[base + k], recv_sem=recv_sems.at[base + k],
                                            device_id=to, device_id_type=MESH)

    def first():
        return [copy(k, s0, c, peers[k], src=w_ref.at[pl.ds(c * hl, hl)]) for k in range(3)]

    def passed():
        return [copy(3 + k, s0 ^ PEER_XOR[k], c, sibling) for k in range(3)]

    def start():
        for cp in first():
            cp.start()

    def forward():
        for k, cp in enumerate(passed()):
            copy(k, s0 ^ PEER_XOR[k], c, peers[k]).wait_recv()
            cp.start()

    def finish():
        for k in range(3):
            copy(3 + k, s0 ^ PEER_XOR[k], 1 - c, sibling).wait_recv()
        for cp in first() + passed():
            cp.wait_send()

    return start, forward, finish


def _gather_io(gathers):
    n = len(gathers)
    return ([_HBM] * n, [_HBM] * n, [jax.ShapeDtypeStruct((4,) + w.shape, w.dtype) for w in gathers],
            [pltpu.SemaphoreType.DMA((6 * n,)), pltpu.SemaphoreType.DMA((6 * n,))] if n else [])


def _gather_ride(w_refs, g_refs, sems, at_start, at_forward, at_finish):
    if not w_refs:
        return
    plans = [_gather_plan(w, g, sems[0], sems[1], 6 * n) for n, (w, g) in enumerate(zip(w_refs, g_refs))]
    for step, cond in enumerate((at_start, at_forward, at_finish)):
        @pl.when(cond)
        def _():
            for p in plans:
                p[step]()


def _ride_copies(kind, src_ref, dst_ref, send_sems, recv_sems, base):
    x, y, c, s0, peers, sibling = _place()
    if kind == "sib":
        hl = src_ref.shape[1] // 2
        return [pltpu.make_async_remote_copy(src_ref=src_ref.at[:, pl.ds((1 - c) * hl, hl)], dst_ref=dst_ref,
                                             send_sem=send_sems.at[base], recv_sem=recv_sems.at[base],
                                             device_id=sibling, device_id_type=MESH)]
    return [pltpu.make_async_remote_copy(src_ref=src_ref.at[s0 ^ PEER_XOR[k]], dst_ref=dst_ref.at[k],
                                         send_sem=send_sems.at[base + k], recv_sem=recv_sems.at[base + k],
                                         device_id=peers[k], device_id_type=MESH) for k in range(3)]


def _ride_io(rides):
    shapes, nsem = [], 0
    for kind, a in rides:
        shapes.append(jax.ShapeDtypeStruct(((4, a.shape[1] // 2) if kind == "sib" else (3, a.shape[1])) + a.shape[2:], a.dtype))
        nsem += 1 if kind == "sib" else 3
    sems = [pltpu.SemaphoreType.DMA((nsem,)), pltpu.SemaphoreType.DMA((nsem,))] if rides else []
    return [_HBM] * len(rides), [_HBM] * len(rides), shapes, sems


def _ride_kernel(body, n_in, n_out, rides, nhp, nI):
    nr = len(rides)

    def kern(*refs):
        ins, r_in = refs[:n_in], refs[n_in:n_in + nr]
        outs, r_out = refs[n_in + nr:n_in + nr + n_out], refs[n_in + nr + n_out:n_in + 2 * nr + n_out]
        sems = refs[n_in + 2 * nr + n_out:]
        if nr:
            hp, I = pl.program_id(0), pl.program_id(1)

            def copies():
                cps = []
                for (kind, _), s, d in zip(rides, r_in, r_out):
                    cps += _ride_copies(kind, s, d, sems[0], sems[1], len(cps))
                return cps

            @pl.when((hp == 0) & (I == 0))
            def _():
                for cp in copies():
                    cp.start()

            @pl.when((hp == nhp - 1) & (I == nI - 1))
            def _():
                for cp in copies():
                    cp.wait()

        body(*ins, *outs)

    return kern


def gather_weight(w_shard, *, name):
    nl = w_shard.shape[0]
    hl = nl // 2

    def body(w_ref, out_ref, send_sems, recv_sems):
        x, y, c, s0, peers, sibling = _place()

        def piece(s, half):
            return out_ref.at[s, pl.ds(half * hl, hl)]

        def copy(k, s, half, to, src=None):
            return pltpu.make_async_remote_copy(src_ref=piece(s, half) if src is None else src, dst_ref=piece(s, half),
                                                send_sem=send_sems.at[k], recv_sem=recv_sems.at[k],
                                                device_id=to, device_id_type=MESH)

        first = [copy(k, s0, c, peers[k], src=w_ref.at[pl.ds(c * hl, hl)]) for k in range(3)]
        for cp in first:
            cp.start()
        passed = [copy(3 + k, s0 ^ PEER_XOR[k], c, sibling) for k in range(3)]
        for k in range(3):
            copy(k, s0 ^ PEER_XOR[k], c, peers[k]).wait_recv()
            passed[k].start()
        for k in range(3):
            copy(3 + k, s0 ^ PEER_XOR[k], 1 - c, sibling).wait_recv()
        for cp in first + passed:
            cp.wait_send()

    return pl.pallas_call(
        body, name=name, in_specs=[_HBM], out_specs=_HBM,
        out_shape=jax.ShapeDtypeStruct((4,) + w_shard.shape, w_shard.dtype),
        scratch_shapes=[pltpu.SemaphoreType.DMA((6,)), pltpu.SemaphoreType.DMA((6,))],
    )(w_shard)


def send_half_to_sibling(G, *, name):
    hl = G.shape[1] // 2

    def body(g_ref, out_ref, send_sem, recv_sem):
        x, y, c, s0, peers, sibling = _place()
        cp = pltpu.make_async_remote_copy(src_ref=g_ref.at[:, pl.ds((1 - c) * hl, hl)], dst_ref=out_ref,
                                          send_sem=send_sem, recv_sem=recv_sem, device_id=sibling, device_id_type=MESH)
        cp.start()
        cp.wait()

    return pl.pallas_call(
        body, name=name, in_specs=[_HBM], out_specs=_HBM,
        out_shape=jax.ShapeDtypeStruct((4, hl) + G.shape[2:], G.dtype),
        scratch_shapes=[pltpu.SemaphoreType.DMA, pltpu.SemaphoreType.DMA],
    )(G)


def scatter_to_chips(A, *, name):
    def body(a_ref, out_ref, send_sems, recv_sems):
        x, y, c, s0, peers, sibling = _place()
        copies = [pltpu.make_async_remote_copy(src_ref=a_ref.at[s0 ^ PEER_XOR[k]], dst_ref=out_ref.at[k],
                                               send_sem=send_sems.at[k], recv_sem=recv_sems.at[k],
                                               device_id=peers[k], device_id_type=MESH) for k in range(3)]
        for cp in copies:
            cp.start()
        for cp in copies:
            cp.wait()

    return pl.pallas_call(
        body, name=name, in_specs=[_HBM], out_specs=_HBM,
        out_shape=jax.ShapeDtypeStruct((3,) + A.shape[1:], A.dtype),
        scratch_shapes=[pltpu.SemaphoreType.DMA((3,)), pltpu.SemaphoreType.DMA((3,))],
    )(A)


def join_halves(buf, a, hl, *, name):
    def body(b_ref, out_ref, send_sem, recv_sem):
        x, y, c, s0, peers, sibling = _place()
        mine, other = pl.ds(a + c * hl, hl), pl.ds(a + (1 - c) * hl, hl)
        cp = pltpu.make_async_remote_copy(src_ref=b_ref.at[mine], dst_ref=out_ref.at[mine],
                                          send_sem=send_sem, recv_sem=recv_sem, device_id=sibling, device_id_type=MESH)
        cp.start()
        pltpu.make_async_remote_copy(src_ref=b_ref.at[other], dst_ref=out_ref.at[other],
                                     send_sem=send_sem, recv_sem=recv_sem, device_id=sibling, device_id_type=MESH).wait_recv()
        cp.wait_send()

    return pl.pallas_call(
        body, name=name, in_specs=[_HBM], out_specs=_HBM, input_output_aliases={0: 0},
        out_shape=jax.ShapeDtypeStruct(buf.shape, buf.dtype),
        scratch_shapes=[pltpu.SemaphoreType.DMA, pltpu.SemaphoreType.DMA],
    )(buf)


def allgather8(xs, *, name, reduce):
    M, N = xs.shape

    def body(x_ref, out_ref, *rest):
        if reduce:
            all_ref, send_sems, recv_sems, local_sem = rest
        else:
            all_ref = out_ref
            send_sems, recv_sems, local_sem = rest
        x, y, c, s0, peers, sibling = _place()
        me = (x, y, c)
        chips = [(1 - x, y), (x, 1 - y), (1 - x, 1 - y)]

        def rows(px, py, pc):
            return all_ref.at[pl.ds((4 * px + 2 * py + pc) * M, M), :]

        def copy(k, block, to, src=None):
            return pltpu.make_async_remote_copy(src_ref=rows(*block) if src is None else src, dst_ref=rows(*block),
                                                send_sem=send_sems.at[k], recv_sem=recv_sems.at[k],
                                                device_id=to, device_id_type=MESH)

        mine = pltpu.make_async_copy(x_ref, rows(*me), local_sem)
        mine.start()
        first = [copy(0, me, sibling, src=x_ref)]
        first += [copy(1 + j, me, (*chip, c), src=x_ref) for j, chip in enumerate(chips)]
        for cp in first:
            cp.start()
        passed = [copy(4 + j, (*chip, c), sibling) for j, chip in enumerate(chips)]
        for j, chip in enumerate(chips):
            copy(1 + j, (*chip, c), me).wait_recv()
            passed[j].start()
        copy(0, sibling, me).wait_recv()
        for j, chip in enumerate(chips):
            copy(4 + j, (*chip, 1 - c), me).wait_recv()
        for cp in first + passed:
            cp.wait_send()
        mine.wait()
        if reduce:
            acc = all_ref[pl.ds(0, M), :]
            for d in range(1, 8):
                acc = acc + all_ref[pl.ds(d * M, M), :]
            out_ref[...] = acc

    vm = pl.BlockSpec(memory_space=pltpu.VMEM)
    scratch = [pltpu.SemaphoreType.DMA((7,)), pltpu.SemaphoreType.DMA((7,)), pltpu.SemaphoreType.DMA]
    if reduce:
        scratch = [pltpu.VMEM((8 * M, N), xs.dtype)] + scratch
    return pl.pallas_call(
        body, name=name, in_specs=[vm], out_specs=vm,
        out_shape=jax.ShapeDtypeStruct((M if reduce else 8 * M, N), xs.dtype), scratch_shapes=scratch,
    )(xs)


def add_halves(G, B, c, *, name):
    S, nl, R, C = G.shape
    hl = nl // 2
    tr = _pick(R, (512, 384, 256, 128))

    def body(c_ref, g_ref, b_ref, o_ref):
        o_ref[...] = (g_ref[...] + b_ref[...]).astype(BF16)

    blk = (1, 1, tr, C)
    return pl.pallas_call(
        body, name=name,
        grid_spec=pltpu.PrefetchScalarGridSpec(
            num_scalar_prefetch=1, grid=(S, hl, R // tr),
            in_specs=[pl.BlockSpec(blk, lambda s, l, r, cr: (s, cr[0] * hl + l, r, 0)),
                      pl.BlockSpec(blk, lambda s, l, r, cr: (s, l, r, 0))],
            out_specs=pl.BlockSpec(blk, lambda s, l, r, cr: (s, l, r, 0))),
        out_shape=jax.ShapeDtypeStruct((S, hl, R, C), BF16),
    )(jnp.reshape(c, (1,)).astype(jnp.int32), G, B)


def add_chips(G, B, Bc, c, s0, acc, a, *, name):
    S, nl, R, C = G.shape
    hl = nl // 2
    tr = _pick(R, (512, 384, 256, 128))

    def body(pc_ref, ps_ref, g_ref, b_ref, c0_ref, c1_ref, c2_ref, acc_ref, o_ref):
        o_ref[...] = ((((g_ref[0] + b_ref[0]) + c0_ref[0].astype(F32)) + c1_ref[0].astype(F32)) + c2_ref[0].astype(F32))

    blk = (1, 1, tr, C)
    cspec = lambda k: pl.BlockSpec(blk, lambda l, r, pc, ps: (k, l, r, 0))
    return pl.pallas_call(
        body, name=name,
        grid_spec=pltpu.PrefetchScalarGridSpec(
            num_scalar_prefetch=2, grid=(hl, R // tr),
            in_specs=[pl.BlockSpec(blk, lambda l, r, pc, ps: (ps[0], pc[0] * hl + l, r, 0)),
                      pl.BlockSpec(blk, lambda l, r, pc, ps: (ps[0], l, r, 0)), cspec(0), cspec(1), cspec(2),
                      pl.BlockSpec(memory_space=pl.ANY)],
            out_specs=pl.BlockSpec((1, tr, C), lambda l, r, pc, ps: (a + pc[0] * hl + l, r, 0))),
        out_shape=jax.ShapeDtypeStruct(acc.shape, F32), input_output_aliases={7: 0},
    )(jnp.reshape(c, (1,)).astype(jnp.int32), jnp.reshape(s0, (1,)).astype(jnp.int32), G, B, Bc, Bc, Bc, acc)


def reduce_scatter_finish(G, B, Bc, c, s0, acc, a, *, tag):
    acc = add_chips(G, B, Bc, c, s0, acc, a, name=f"rs_add2_{tag}")
    return join_halves(acc, a, G.shape[1] // 2, name=f"rs_join_{tag}")


def reduce_scatter_weight(G, c, s0, acc, a, *, tag):
    B = send_half_to_sibling(G, name=f"rs_sib_{tag}")
    A = add_halves(G, B, c, name=f"rs_add1_{tag}")
    Bc = scatter_to_chips(A, name=f"rs_chips_{tag}")
    return reduce_scatter_finish(G, B, Bc, c, s0, acc, a, tag=tag)


def adamw(w, g, m, v, *, name):
    shp = w.shape
    if len(shp) == 2:
        w, g, m, v = (a[None] for a in (w, g, m, v))
    nl, R, C = w.shape
    tr = R
    for t in (512, 384, 256, 128):
        if R % t == 0:
            tr = t
            break

    def body(w_ref, g_ref, m_ref, v_ref, d_ref, nm_ref, nv_ref):
        gv = g_ref[...]
        mn = ADAM_B1 * m_ref[...] + (1.0 - ADAM_B1) * gv
        vn = ADAM_B2 * v_ref[...] + (1.0 - ADAM_B2) * jnp.square(gv)
        m_hat = mn / (1.0 - ADAM_B1 ** ADAM_STEP)
        v_hat = vn / (1.0 - ADAM_B2 ** ADAM_STEP)
        d_ref[...] = -ADAM_LR * (m_hat / (jnp.sqrt(v_hat) + ADAM_EPS) + ADAM_WD * w_ref[...])
        nm_ref[...] = mn
        nv_ref[...] = vn

    spec = pl.BlockSpec((1, tr, C), lambda l, i: (l, i, 0))
    sds = jax.ShapeDtypeStruct((nl, R, C), F32)
    d, nm, nv = pl.pallas_call(body, name=name, grid=(nl, R // tr), in_specs=[spec] * 4, out_specs=(spec,) * 3,
                               out_shape=(sds,) * 3)(w, g, m, v)
    return d.reshape(shp), nm.reshape(shp), nv.reshape(shp)


_SMALL = ("ln_emb_g", "ln_emb_b", "q_norm", "kv_norm", "ln1_g", "ln1_b", "ln2_g", "ln2_b", "meta")


def _pack_small(d):
    flat = jnp.concatenate([d[k].reshape(-1) for k in _SMALL])
    rows = -(-flat.shape[0] // 128)
    rows = -(-rows // 8) * 8
    flat = jnp.concatenate([flat, jnp.zeros((rows * 128 - flat.shape[0],), F32)])
    return flat.reshape(rows, 128)


def _unpack_small(p, shapes):
    flat = p.reshape(-1)
    out, o = {}, 0
    for k in _SMALL:
        n = int(np.prod(shapes[k]))
        out[k] = flat[o:o + n].reshape(shapes[k])
        o += n
    return out


def kernel(x, meta_tokens, ln_emb_g, ln_emb_b, w_in, mla_q_norm, mla_kv_norm, w_uq, w_ukv, w_out, ln1_g, ln1_b, w_ff1, w_ff2, ln2_g, ln2_b, loss_target, m_meta_tokens, m_ln_emb_g, m_ln_emb_b, m_w_in, m_mla_q_norm, m_mla_kv_norm, m_w_uq, m_w_ukv, m_w_out, m_ln1_g, m_ln1_b, m_w_ff1, m_w_ff2, m_ln2_g, m_ln2_b, v_meta_tokens, v_ln_emb_g, v_ln_emb_b, v_w_in, v_mla_q_norm, v_mla_kv_norm, v_w_uq, v_w_ukv, v_w_out, v_ln1_g, v_ln1_b, v_w_ff1, v_w_ff2, v_ln2_g, v_ln2_b):
    xi, yi, ci = lax.axis_index("x"), lax.axis_index("y"), lax.axis_index("c")
    s0 = 2 * xi + yi
    nl = w_in.shape[0]

    big = {"w_in": w_in, "w_uq": w_uq, "w_ukv": w_ukv}
    late = {"w_out": w_out.astype(BF16), "w_ff1": w_ff1.astype(BF16), "w_ff2": w_ff2.astype(BF16), "s0": s0, "c": ci}
    full = {}
    for k, v in big.items():
        vb = v.astype(BF16)
        full[k] = lax.dynamic_update_slice(gather_weight(vb, name=f"ag_{k}"), vb[None], (s0, 0, 0, 0))
    cols = lambda a: jnp.moveaxis(a, 0, 2).reshape(a.shape[1], a.shape[2], 4 * a.shape[3])
    k_w_in = _perm_w_in(cols(full["w_in"]))[:, None]
    k_w_uq = _perm_w_uq(cols(full["w_uq"]))[:, None]
    k_w_ukv = _perm_w_ukv(cols(full["w_ukv"]))[:, None]
    meta_all = allgather8(meta_tokens, name="ag_meta", reduce=False)
    meta_full = jnp.concatenate([meta_all[32 * s:32 * s + N_META] for s in range(4)], axis=1)

    loss_part, grad_x, g = _device_step(x[0], loss_target[0], meta_full, ln_emb_g, ln_emb_b, k_w_in, mla_q_norm, mla_kv_norm,
                                        k_w_uq, k_w_ukv, None, ln1_g, ln1_b, None, None, ln2_g, ln2_b, late=late)
    loss = lax.psum(loss_part, ("x", "y", "c"))

    gw = {k: g[k] for k in _RS_SHAPES}

    small_shapes = {"ln_emb_g": (D_MODEL,), "ln_emb_b": (D_MODEL,), "q_norm": (nl, MLA_Q_LORA), "kv_norm": (nl, MLA_KV_LORA),
                    "ln1_g": (nl, D_MODEL), "ln1_b": (nl, D_MODEL), "ln2_g": (nl, D_MODEL), "ln2_b": (nl, D_MODEL),
                    "meta": (N_META, D_MODEL)}
    gs = _unpack_small(allgather8(_pack_small(g), name="ar_small", reduce=True), small_shapes)
    gw.update({"ln_emb_g": gs["ln_emb_g"], "ln_emb_b": gs["ln_emb_b"], "mla_q_norm": gs["q_norm"], "mla_kv_norm": gs["kv_norm"],
               "ln1_g": gs["ln1_g"], "ln1_b": gs["ln1_b"], "ln2_g": gs["ln2_g"], "ln2_b": gs["ln2_b"],
               "meta_tokens": lax.dynamic_slice_in_dim(gs["meta"], s0 * 256, 256, axis=1)})

    names = ["meta_tokens", "ln_emb_g", "ln_emb_b", "w_in", "mla_q_norm", "mla_kv_norm", "w_uq", "w_ukv", "w_out",
             "ln1_g", "ln1_b", "w_ff1", "w_ff2", "ln2_g", "ln2_b"]
    ws = [meta_tokens, ln_emb_g, ln_emb_b, w_in, mla_q_norm, mla_kv_norm, w_uq, w_ukv, w_out, ln1_g, ln1_b, w_ff1, w_ff2, ln2_g, ln2_b]
    ms = [m_meta_tokens, m_ln_emb_g, m_ln_emb_b, m_w_in, m_mla_q_norm, m_mla_kv_norm, m_w_uq, m_w_ukv, m_w_out, m_ln1_g, m_ln1_b, m_w_ff1, m_w_ff2, m_ln2_g, m_ln2_b]
    vs = [v_meta_tokens, v_ln_emb_g, v_ln_emb_b, v_w_in, v_mla_q_norm, v_mla_kv_norm, v_w_uq, v_w_ukv, v_w_out, v_ln1_g, v_ln1_b, v_w_ff1, v_w_ff2, v_ln2_g, v_ln2_b]
    deltas, new_m, new_v = [], [], []
    for n, w, m, v in zip(names, ws, ms, vs):
        w2 = w.reshape(1, -1) if w.ndim == 1 else w
        d, nm, nv = adamw(w2, gw[n].reshape(w2.shape), m.reshape(w2.shape), v.reshape(w2.shape), name=f"adamw_{n}")
        deltas.append(d.reshape(w.shape))
        new_m.append(nm.reshape(w.shape))
        new_v.append(nv.reshape(w.shape))
    grads_out = [gw[n].reshape(w.shape) for n, w in zip(names, ws)]
    return (loss, grad_x[None], *grads_out, *deltas, *new_m, *new_v)
```

```python
import functools
import math

import numpy as np
import jax
import jax.numpy as jnp
from jax import lax
from jax.experimental import pallas as pl
from jax.experimental.pallas import tpu as pltpu

F32 = jnp.float32
BF16 = jnp.bfloat16

D_MODEL = 1024
DEPTH = 4
N_META = 16
BLK = 128
N_PAD = 112
SB_HEADS = 8
MLA_HEADS = 8
MLA_NOPE = 64
MLA_ROPE = 32
MLA_V = 64
MLA_Q_LORA = 384
MLA_KV_LORA = 256
RET_HEADS = 4
RET_QK = 64
RET_V = 128
D_FF = 4 * D_MODEL
ROPE_THETA = 10000.0
LN_EPS = 1e-5
DN_ALPHA = (2 * DEPTH) ** 0.25
RET_GAMMA = tuple(1.0 - 2.0 ** (-5 - h) for h in range(RET_HEADS))
RET_LOG_G = tuple(float(np.log(np.float32(g))) for g in RET_GAMMA)
MLA_SCALE = (MLA_NOPE + MLA_ROPE) ** -0.5

ADAM_LR = 0.001
ADAM_B1 = 0.9
ADAM_B2 = 0.999
ADAM_EPS = 1e-08
ADAM_WD = 0.01
ADAM_STEP = 10

N_SB = 1536
C_SBQ, C_SBK, C_SBV = 0, 512, 1024
C_RQ, C_RK, C_RV, C_RG = 0, 256, 512, 1024
C_CQ, C_CKV, C_KR = 1536, 1920, 2176
N_IN = 3744
N_INP = 3840

NEG = -1e30


def _pick(n, cands):
    for t in cands:
        if n % t == 0:
            return t
    raise ValueError(f"no tile for {n} in {cands}")


def _row_tile(n):
    return _pick(n, (1056, 1024, 528, 512, 384, 256, 128))


def _dot(a, b):
    return jnp.dot(a, b, preferred_element_type=F32)


def _dot_nt(a, b):
    return lax.dot_general(a, b, (((1,), (1,)), ((), ())), preferred_element_type=F32)


def _dot_tn(a, b):
    return lax.dot_general(a, b, (((0,), (0,)), ((), ())), preferred_element_type=F32)


def mm_nn(a, b, *, tn, name, tk=None, prologue=None, axpy=None, out_dtype=F32):
    M, K = a.shape
    S, _, Ns = b.shape
    tm = _row_tile(M)
    tk = K if tk is None else tk
    npt = Ns // tn
    nk = K // tk
    alpha = None if axpy is None else axpy[1]

    def body(*refs):
        if axpy is None:
            a_ref, b_ref, o_ref, acc = refs
        else:
            a_ref, b_ref, e_ref, o_ref, acc = refs
        k = pl.program_id(2)

        @pl.when(k == 0)
        def _():
            acc[...] = jnp.zeros_like(acc)

        x = a_ref[...]
        if prologue == "relu2":
            x = jnp.square(jnp.maximum(x, 0.0))
        acc[...] += _dot(x.astype(BF16), b_ref[0])

        @pl.when(k == nk - 1)
        def _():
            r = acc[...]
            if axpy is not None:
                r = r + alpha * e_ref[...]
            o_ref[...] = r.astype(out_dtype)

    in_specs = [pl.BlockSpec((tm, tk), lambda i, j, k: (i, k)),
                pl.BlockSpec((1, tk, tn), lambda i, j, k: (j // npt, k, j % npt))]
    args = [a, b]
    if axpy is not None:
        in_specs.append(pl.BlockSpec((tm, tn), lambda i, j, k: (i, j)))
        args.append(axpy[0])
    return pl.pallas_call(
        body, name=name, grid=(M // tm, (S * Ns) // tn, nk), in_specs=in_specs,
        out_specs=pl.BlockSpec((tm, tn), lambda i, j, k: (i, j)),
        out_shape=jax.ShapeDtypeStruct((M, S * Ns), out_dtype),
        scratch_shapes=[pltpu.VMEM((tm, tn), F32)],
        compiler_params=pltpu.CompilerParams(dimension_semantics=("parallel", "parallel", "arbitrary")),
    )(*args)


def mm_nt(a, b, *, tn, tko, name, axpy=None, relu2grad=None, out_dtype=F32):
    M, N = a.shape
    S, K, Ns = b.shape
    tm = _row_tile(M)
    npt = Ns // tn
    nn = N // tn
    alpha = None if axpy is None else axpy[1]

    def body(*refs):
        if axpy is None and relu2grad is None:
            a_ref, b_ref, o_ref, acc = refs
        else:
            a_ref, b_ref, e_ref, o_ref, acc = refs
        n = pl.program_id(2)

        @pl.when(n == 0)
        def _():
            acc[...] = jnp.zeros_like(acc)

        acc[...] += _dot_nt(a_ref[...].astype(BF16), b_ref[0])

        @pl.when(n == nn - 1)
        def _():
            r = acc[...]
            if axpy is not None:
                r = r + alpha * e_ref[...]
            if relu2grad is not None:
                r = r * (2.0 * jnp.maximum(e_ref[...], 0.0))
            o_ref[...] = r.astype(out_dtype)

    in_specs = [pl.BlockSpec((tm, tn), lambda i, j, n: (i, n)),
                pl.BlockSpec((1, tko, tn), lambda i, j, n: (n // npt, j, n % npt))]
    args = [a, b]
    extra = axpy[0] if axpy is not None else relu2grad
    if extra is not None:
        in_specs.append(pl.BlockSpec((tm, tko), lambda i, j, n: (i, j)))
        args.append(extra)
    return pl.pallas_call(
        body, name=name, grid=(M // tm, K // tko, nn), in_specs=in_specs,
        out_specs=pl.BlockSpec((tm, tko), lambda i, j, n: (i, j)),
        out_shape=jax.ShapeDtypeStruct((M, K), out_dtype),
        scratch_shapes=[pltpu.VMEM((tm, tko), F32)],
        compiler_params=pltpu.CompilerParams(dimension_semantics=("parallel", "parallel", "arbitrary")),
    )(*args)


def mm_tn(a, g, *, shards, tko, tn, name, prologue=None, into=None):
    M, K = a.shape
    _, N = g.shape
    Ns = N // shards
    tm = _row_tile(M)
    npt = Ns // tn
    nm = M // tm

    def body(*refs):
        if into is None:
            a_ref, g_ref, o_ref, acc = refs
        else:
            a_ref, g_ref, _, o_ref, acc = refs
        m = pl.program_id(2)

        @pl.when(m == 0)
        def _():
            acc[...] = jnp.zeros_like(acc)

        x = a_ref[...]
        if prologue == "relu2":
            x = jnp.square(jnp.maximum(x, 0.0))
        acc[...] += _dot_tn(x.astype(BF16), g_ref[...].astype(BF16))

        @pl.when(m == nm - 1)
        def _():
            if into is None or into[2] == "layer":
                o_ref[0] = acc[...]
            else:
                o_ref[0, 0] = acc[...]

    in_specs = [pl.BlockSpec((tm, tko), lambda i, j, m: (m, i)),
                pl.BlockSpec((tm, tn), lambda i, j, m: (m, j))]
    scratch = [pltpu.VMEM((tko, tn), F32)]
    params = pltpu.CompilerParams(dimension_semantics=("parallel", "parallel", "arbitrary"))
    if into is None:
        return pl.pallas_call(
            body, name=name, grid=(K // tko, N // tn, nm), in_specs=in_specs,
            out_specs=pl.BlockSpec((1, tko, tn), lambda i, j, m: (j // npt, i, j % npt)),
            out_shape=jax.ShapeDtypeStruct((shards, K, Ns), F32), scratch_shapes=scratch, compiler_params=params,
        )(a, g)
    buf, layer, how = into
    if how == "layer":
        out_spec = pl.BlockSpec((1, tko, tn), lambda i, j, m: (layer, i, j))
    elif how == "cols":
        npt4 = (N // 4) // tn
        out_spec = pl.BlockSpec((1, 1, tko, tn), lambda i, j, m: (j // npt4, layer, i, j % npt4))
    else:
        kpt4 = (K // 4) // tko
        out_spec = pl.BlockSpec((1, 1, tko, tn), lambda i, j, m: (i // kpt4, layer, i % kpt4, j))
    return pl.pallas_call(
        body, name=name, grid=(K // tko, N // tn, nm), in_specs=in_specs + [pl.BlockSpec(memory_space=pl.ANY)],
        out_specs=out_spec, out_shape=jax.ShapeDtypeStruct(buf.shape, F32), input_output_aliases={2: 0},
        scratch_shapes=scratch, compiler_params=params,
    )(a, g, buf)


def _ln_stats(z):
    mu = jnp.mean(z, axis=-1, keepdims=True)
    zc = z - mu
    var = jnp.mean(jnp.square(zc), axis=-1, keepdims=True)
    r = lax.rsqrt(var + LN_EPS)
    return zc * r, r


def ln_fwd(x, g, b, *, name, res=None):
    L, Dm = x.shape
    tr = _row_tile(L)
    g2, b2 = g.reshape(1, Dm), b.reshape(1, Dm)

    def body(*refs):
        if res is None:
            x_ref, g_ref, b_ref, y_ref, yb_ref = refs
            z = x_ref[...]
        else:
            x_ref, r_ref, g_ref, b_ref, y_ref, yb_ref, z_ref = refs
            z = DN_ALPHA * r_ref[...] + x_ref[...]
            z_ref[...] = z
        xh, _ = _ln_stats(z)
        y = xh * g_ref[...] + b_ref[...]
        y_ref[...] = y
        yb_ref[...] = y.astype(BF16)

    row = pl.BlockSpec((tr, Dm), lambda i: (i, 0))
    vec = pl.BlockSpec((1, Dm), lambda i: (0, 0))
    sds = jax.ShapeDtypeStruct((L, Dm), F32)
    sdb = jax.ShapeDtypeStruct((L, Dm), BF16)
    if res is None:
        y, yb = pl.pallas_call(body, name=name, grid=(L // tr,), in_specs=[row, vec, vec], out_specs=(row, row),
                               out_shape=(sds, sdb))(x, g2, b2)
        return y, yb, x
    return pl.pallas_call(body, name=name, grid=(L // tr,), in_specs=[row, row, vec, vec], out_specs=(row, row, row),
                          out_shape=(sds, sdb, sds))(x, res, g2, b2)


def ln_bwd(dy, z, g, *, name):
    L, Dm = z.shape
    tr = _row_tile(L)

    def body(dy_ref, z_ref, g_ref, dz_ref, dg_ref, db_ref):
        @pl.when(pl.program_id(0) == 0)
        def _():
            dg_ref[...] = jnp.zeros_like(dg_ref)
            db_ref[...] = jnp.zeros_like(db_ref)

        dyv = dy_ref[...]
        xh, r = _ln_stats(z_ref[...])
        dxh = dyv * g_ref[...]
        m1 = jnp.mean(dxh, axis=-1, keepdims=True)
        m2 = jnp.mean(dxh * xh, axis=-1, keepdims=True)
        dz_ref[...] = r * (dxh - m1 - xh * m2)
        dg_ref[...] += jnp.sum(dyv * xh, axis=0, keepdims=True)
        db_ref[...] += jnp.sum(dyv, axis=0, keepdims=True)

    row = pl.BlockSpec((tr, Dm), lambda i: (i, 0))
    vec = pl.BlockSpec((1, Dm), lambda i: (0, 0))
    return pl.pallas_call(
        body, name=name, grid=(L // tr,), in_specs=[row, row, vec], out_specs=(row, vec, vec),
        out_shape=(jax.ShapeDtypeStruct((L, Dm), F32), jax.ShapeDtypeStruct((1, Dm), F32), jax.ShapeDtypeStruct((1, Dm), F32)),
        compiler_params=pltpu.CompilerParams(dimension_semantics=("arbitrary",)),
    )(dy, z, g.reshape(1, Dm))


def loss_fwd_bwd(h, target, *, name):
    L, Dm = h.shape
    nb = L // BLK

    def body(h_ref, t_ref, l_ref, dh_ref):
        i = pl.program_id(0)

        @pl.when(i == 0)
        def _():
            l_ref[...] = jnp.zeros_like(l_ref)
            dh_ref[...] = jnp.zeros_like(dh_ref)

        @pl.when(i > 0)
        def _():
            e = h_ref[...] - t_ref[...]
            dh_ref[...] = e * (1.0 / Dm)
            part = jnp.sum(jnp.sum(jnp.square(e), axis=-1, keepdims=True) * (1.0 / Dm), axis=0, keepdims=True)
            l_ref[...] += 0.5 * part

    return pl.pallas_call(
        body, name=name, grid=(nb,),
        in_specs=[pl.BlockSpec((BLK, Dm), lambda i: (i, 0)),
                  pl.BlockSpec((BLK, Dm), lambda i: (jnp.maximum(i - 1, 0), 0))],
        out_specs=(pl.BlockSpec((8, 128), lambda i: (0, 0)), pl.BlockSpec((BLK, Dm), lambda i: (i, 0))),
        out_shape=(jax.ShapeDtypeStruct((8, 128), F32), jax.ShapeDtypeStruct((L, Dm), F32)),
        compiler_params=pltpu.CompilerParams(dimension_semantics=("arbitrary",)),
    )(h, target)


def _swap_half(x, half):
    ax = x.ndim - 1
    n = x.shape[ax]
    lane = lax.broadcasted_iota(jnp.int32, x.shape, ax)
    up = pltpu.roll(x, n - half, ax)
    dn = pltpu.roll(x, half, ax)
    return jnp.where((lane % (2 * half)) < half, up, dn)


def _rope(x, cs, sn, half):
    return x * cs + _swap_half(x, half) * sn


def _rope_t(dy, cs, sn, half):
    return dy * cs + _swap_half(dy * sn, half)


def _rms(x):
    r = lax.rsqrt(jnp.mean(jnp.square(x), axis=-1, keepdims=True) + LN_EPS)
    return x * r, r


def mla_pre_fwd(P, gq, gkv, cs, sn, *, name):
    L = P.shape[0]
    tr = _row_tile(L)

    def body(p_ref, gq_ref, gkv_ref, cs_ref, sn_ref, nq_ref, nkv_ref, kr_ref):
        cq = p_ref[:, 0:MLA_Q_LORA]
        ckv = p_ref[:, MLA_Q_LORA:MLA_Q_LORA + MLA_KV_LORA]
        kr = p_ref[:, 640:768]
        nq_ref[...] = (_rms(cq)[0] * gq_ref[...]).astype(BF16)
        nkv_ref[...] = (_rms(ckv)[0] * gkv_ref[...]).astype(BF16)
        krr = _rope(kr, cs_ref[...], sn_ref[...], MLA_ROPE // 2)
        kr_ref[...] = (krr + pltpu.roll(krr, MLA_ROPE, 1)).astype(BF16)

    return pl.pallas_call(
        body, name=name, grid=(L // tr,),
        in_specs=[pl.BlockSpec((tr, 768), lambda i: (i, C_CQ // 768)),
                  pl.BlockSpec((1, MLA_Q_LORA), lambda i: (0, 0)), pl.BlockSpec((1, MLA_KV_LORA), lambda i: (0, 0)),
                  pl.BlockSpec((tr, 128), lambda i: (i, 0)), pl.BlockSpec((tr, 128), lambda i: (i, 0))],
        out_specs=(pl.BlockSpec((tr, MLA_Q_LORA), lambda i: (i, 0)), pl.BlockSpec((tr, MLA_KV_LORA), lambda i: (i, 0)),
                   pl.BlockSpec((tr, 128), lambda i: (i, 0))),
        out_shape=(jax.ShapeDtypeStruct((L, MLA_Q_LORA), BF16), jax.ShapeDtypeStruct((L, MLA_KV_LORA), BF16),
                   jax.ShapeDtypeStruct((L, 128), BF16)),
    )(P, gq.reshape(1, -1), gkv.reshape(1, -1), cs, sn)


def mla_pre_bwd(P, dnq, dnkv, dkr, gq, gkv, cs, sn, *, name):
    L = P.shape[0]
    tr = _row_tile(L)

    def body(p_ref, dnq_ref, dnkv_ref, dkr_ref, gq_ref, gkv_ref, cs_ref, sn_ref, dp_ref, dgq_ref, dgkv_ref):
        @pl.when(pl.program_id(0) == 0)
        def _():
            dgq_ref[...] = jnp.zeros_like(dgq_ref)
            dgkv_ref[...] = jnp.zeros_like(dgkv_ref)

        def rms_bwd(x, dy, g_ref, dg_ref):
            xn, r = _rms(x)
            dxn = dy * g_ref[...]
            dg_ref[...] += jnp.sum(dy * xn, axis=0, keepdims=True)
            return r * (dxn - xn * jnp.mean(dxn * xn, axis=-1, keepdims=True))

        dp_ref[:, 0:MLA_Q_LORA] = rms_bwd(p_ref[:, 0:MLA_Q_LORA], dnq_ref[...], gq_ref, dgq_ref)
        dp_ref[:, MLA_Q_LORA:640] = rms_bwd(p_ref[:, MLA_Q_LORA:640], dnkv_ref[...], gkv_ref, dgkv_ref)
        d2 = dkr_ref[...]
        lane = lax.broadcasted_iota(jnp.int32, d2.shape, 1)
        dkr = jnp.where(lane < MLA_ROPE, d2 + pltpu.roll(d2, 128 - MLA_ROPE, 1), 0.0)
        dp_ref[:, 640:768] = _rope_t(dkr, cs_ref[...], sn_ref[...], MLA_ROPE // 2)

    return pl.pallas_call(
        body, name=name, grid=(L // tr,),
        in_specs=[pl.BlockSpec((tr, 768), lambda i: (i, C_CQ // 768)),
                  pl.BlockSpec((tr, MLA_Q_LORA), lambda i: (i, 0)), pl.BlockSpec((tr, MLA_KV_LORA), lambda i: (i, 0)),
                  pl.BlockSpec((tr, 128), lambda i: (i, 0)),
                  pl.BlockSpec((1, MLA_Q_LORA), lambda i: (0, 0)), pl.BlockSpec((1, MLA_KV_LORA), lambda i: (0, 0)),
                  pl.BlockSpec((tr, 128), lambda i: (i, 0)), pl.BlockSpec((tr, 128), lambda i: (i, 0))],
        out_specs=(pl.BlockSpec((tr, 768), lambda i: (i, 0)), pl.BlockSpec((1, MLA_Q_LORA), lambda i: (0, 0)),
                   pl.BlockSpec((1, MLA_KV_LORA), lambda i: (0, 0))),
        out_shape=(jax.ShapeDtypeStruct((L, 768), F32), jax.ShapeDtypeStruct((1, MLA_Q_LORA), F32),
                   jax.ShapeDtypeStruct((1, MLA_KV_LORA), F32)),
        compiler_params=pltpu.CompilerParams(dimension_semantics=("arbitrary",)),
    )(P, dnq, dnkv, dkr, gq.reshape(1, -1), gkv.reshape(1, -1), cs, sn)


def _tri(kind):
    r = lax.broadcasted_iota(jnp.int32, (BLK, BLK), 0)
    c = lax.broadcasted_iota(jnp.int32, (BLK, BLK), 1)
    t = ((r > c) if kind == "right" else (r < c)).astype(BF16)
    return jnp.concatenate([t, t], axis=0)


def _tri_sum(x, tt):
    hi = x.astype(BF16)
    lo = (x - hi.astype(F32)).astype(BF16)
    return _dot(jnp.concatenate([hi, lo], axis=1), tt)


def _sb_tile(q, k, i, j, tt_right, R):
    row = lax.broadcasted_iota(jnp.int32, (BLK, BLK), 0)
    col = lax.broadcasted_iota(jnp.int32, (BLK, BLK), 1)
    s_idx = j * BLK + col
    mask = (s_idx < i * BLK + row) & (s_idx >= N_PAD)
    z = _dot_nt(q, k)
    sp = jnp.maximum(z, 0.0) + jnp.log1p(jnp.exp(-jnp.abs(z)))
    lk = jnp.where(mask, -sp, 0.0)
    E = _tri_sum(lk, tt_right) + R
    return mask, z, sp, lk, E


def _old_sb_fwd(P, *, name):
    L = P.shape[0]
    nb = L // BLK

    def body(q_ref, k_ref, v_ref, o_ref, c_ref):
        i = pl.program_id(1)
        tt = _tri("right")
        lane = lax.broadcasted_iota(jnp.int32, (BLK, 128), 1)
        qs = [(q_ref[:, 64 * h:64 * h + 64] * 0.125).astype(BF16) for h in range(2)]

        def step(jj, carry):
            j = i - jj
            off = pl.multiple_of(j * BLK, BLK)
            kb = k_ref[pl.ds(off, BLK), :].astype(BF16)
            vb = v_ref[pl.ds(off, BLK), :].astype(BF16)
            out = []
            for h in range(2):
                o, R = carry[2 * h], carry[2 * h + 1]
                sl = slice(64 * h, 64 * h + 64)
                mask, z, sp, lk, E = _sb_tile(qs[h], kb[:, sl], i, j, tt, R)
                w = jnp.where(mask, jnp.exp(z - sp + E), 0.0)
                c_ref[h] = jnp.where(lane == j, R, c_ref[h])
                out += [o + _dot(w.astype(BF16), vb[:, sl]), R + jnp.sum(lk, axis=1, keepdims=True)]
            return tuple(out)

        c_ref[...] = jnp.zeros_like(c_ref)
        z0 = (jnp.zeros((BLK, 64), F32), jnp.zeros((BLK, 1), F32))
        res = lax.fori_loop(0, i + 1, step, z0 + z0)
        o_ref[...] = jnp.concatenate([res[0], res[2]], axis=1)

    return pl.pallas_call(
        body, name=name, grid=(SB_HEADS // 2, nb),
        in_specs=[pl.BlockSpec((BLK, 128), lambda hp, i: (i, C_SBQ // 128 + hp)),
                  pl.BlockSpec((L, 128), lambda hp, i: (0, C_SBK // 128 + hp)),
                  pl.BlockSpec((L, 128), lambda hp, i: (0, C_SBV // 128 + hp))],
        out_specs=(pl.BlockSpec((BLK, 128), lambda hp, i: (i, hp)), pl.BlockSpec((2, BLK, 128), lambda hp, i: (hp, i, 0))),
        out_shape=(jax.ShapeDtypeStruct((L, 512), F32), jax.ShapeDtypeStruct((SB_HEADS, L, 128), F32)),
        compiler_params=pltpu.CompilerParams(dimension_semantics=("parallel", "arbitrary")),
    )(P, P, P)


def _old_sb_bwd(P, carries, dmixed, *, name):
    L = P.shape[0]
    nb = L // BLK

    def body(q_ref, k_ref, v_ref, c_ref, do_ref, dq_ref, dk_ref, dv_ref):
        i = pl.program_id(1)

        @pl.when(i == 0)
        def _():
            dk_ref[...] = jnp.zeros_like(dk_ref)
            dv_ref[...] = jnp.zeros_like(dv_ref)

        tr = _tri("right")
        tl = _tri("left")
        lane = lax.broadcasted_iota(jnp.int32, (BLK, 128), 1)
        qs = [(q_ref[:, 64 * h:64 * h + 64] * 0.125).astype(BF16) for h in range(2)]
        dos = [do_ref[:, 64 * h:64 * h + 64].astype(BF16) for h in range(2)]

        def step(j, carry):
            off = pl.multiple_of(j * BLK, BLK)
            kb = k_ref[pl.ds(off, BLK), :].astype(BF16)
            vb = v_ref[pl.ds(off, BLK), :].astype(BF16)
            out, dks, dvs = [], [], []
            for h in range(2):
                dq, PL = carry[2 * h], carry[2 * h + 1]
                sl = slice(64 * h, 64 * h + 64)
                R = jnp.sum(jnp.where(lane == j, c_ref[h], 0.0), axis=1, keepdims=True)
                mask, z, sp, lk, E = _sb_tile(qs[h], kb[:, sl], i, j, tr, R)
                sig = jnp.exp(z - sp)
                w = jnp.where(mask, sig * jnp.exp(E), 0.0)
                dA = _dot_nt(dos[h], vb[:, sl]) * w
                Pp = _tri_sum(dA, tl) + PL
                dz = jnp.where(mask, dA - sig * (dA + Pp), 0.0).astype(BF16)
                dks.append(_dot_tn(dz, qs[h]))
                dvs.append(_dot_tn(w.astype(BF16), dos[h]))
                out += [dq + _dot(dz, kb[:, sl]), PL + jnp.sum(dA, axis=1, keepdims=True)]
            dk_ref[pl.ds(off, BLK), :] += jnp.concatenate(dks, axis=1)
            dv_ref[pl.ds(off, BLK), :] += jnp.concatenate(dvs, axis=1)
            return tuple(out)

        z0 = (jnp.zeros((BLK, 64), F32), jnp.zeros((BLK, 1), F32))
        res = lax.fori_loop(0, i + 1, step, z0 + z0)
        dq_ref[...] = jnp.concatenate([res[0], res[2]], axis=1) * 0.125

    blk = lambda c0: pl.BlockSpec((BLK, 128), lambda hp, i: (i, c0 + hp))
    full = lambda c0: pl.BlockSpec((L, 128), lambda hp, i: (0, c0 + hp))
    sds = jax.ShapeDtypeStruct((L, 512), F32)
    return pl.pallas_call(
        body, name=name, grid=(SB_HEADS // 2, nb),
        in_specs=[blk(C_SBQ // 128), full(C_SBK // 128), full(C_SBV // 128),
                  pl.BlockSpec((2, BLK, 128), lambda hp, i: (hp, i, 0)), blk(0)],
        out_specs=(blk(0), full(0), full(0)), out_shape=(sds, sds, sds),
        compiler_params=pltpu.CompilerParams(dimension_semantics=("parallel", "arbitrary")),
    )(P, P, P, carries, dmixed)


def _mla_mask(i, j):
    row = lax.broadcasted_iota(jnp.int32, (BLK, BLK), 0)
    col = lax.broadcasted_iota(jnp.int32, (BLK, BLK), 1)
    t_idx = i * BLK + row
    s_idx = j * BLK + col
    return (s_idx <= t_idx) & ((s_idx >= N_PAD) | (s_idx == t_idx))


def _mla_q(q_ref, cs_ref, sn_ref):
    qr = _rope(q_ref[:, 128:256], cs_ref[...], sn_ref[...], MLA_ROPE // 2)
    qn = [q_ref[:, 64 * h:64 * h + 64].astype(BF16) for h in range(2)]
    qrs = [qr[:, 32 * h:32 * h + 32].astype(BF16) for h in range(2)]
    return qn, qrs


def _old_mla_fwd(Q, KV, KR, cs, sn, *, name):
    L = Q.shape[0]
    nb = L // BLK

    def body(q_ref, kn_ref, v_ref, kr_ref, cs_ref, sn_ref, o_ref, lse_ref):
        i = pl.program_id(1)
        qn, qrs = _mla_q(q_ref, cs_ref, sn_ref)

        def step(j, carry):
            off = pl.multiple_of(j * BLK, BLK)
            knb = kn_ref[pl.ds(off, BLK), :]
            vb = v_ref[pl.ds(off, BLK), :]
            krb = kr_ref[pl.ds(off, BLK), 0:MLA_ROPE]
            mask = _mla_mask(i, j)
            out = []
            for h in range(2):
                m, l, acc = carry[3 * h], carry[3 * h + 1], carry[3 * h + 2]
                sl = slice(64 * h, 64 * h + 64)
                s = (_dot_nt(qn[h], knb[:, sl]) + _dot_nt(qrs[h], krb)) * MLA_SCALE
                s = jnp.where(mask, s, NEG)
                m_new = jnp.maximum(m, jnp.max(s, axis=1, keepdims=True))
                a = jnp.exp(m - m_new)
                p = jnp.exp(s - m_new)
                out += [m_new, a * l + jnp.sum(p, axis=1, keepdims=True), a * acc + _dot(p.astype(BF16), vb[:, sl])]
            return tuple(out)

        z0 = (jnp.full((BLK, 1), NEG, F32), jnp.zeros((BLK, 1), F32), jnp.zeros((BLK, 64), F32))
        res = lax.fori_loop(0, i + 1, step, z0 + z0)
        o_ref[...] = jnp.concatenate([res[2] / res[1], res[5] / res[4]], axis=1)
        lane = lax.broadcasted_iota(jnp.int32, (BLK, 128), 1)
        lse0 = res[0] + jnp.log(res[1])
        lse1 = res[3] + jnp.log(res[4])
        lse_ref[0] = jnp.where(lane == 0, lse0, jnp.where(lane == 1, lse1, 0.0))

    return pl.pallas_call(
        body, name=name, grid=(MLA_HEADS // 2, nb),
        in_specs=[pl.BlockSpec((BLK, 256), lambda hp, i: (i, hp)),
                  pl.BlockSpec((L, 128), lambda hp, i: (0, hp)),
                  pl.BlockSpec((L, 128), lambda hp, i: (0, 4 + hp)),
                  pl.BlockSpec((L, 128), lambda hp, i: (0, 0)),
                  pl.BlockSpec((BLK, 128), lambda hp, i: (i, 0)), pl.BlockSpec((BLK, 128), lambda hp, i: (i, 0))],
        out_specs=(pl.BlockSpec((BLK, 128), lambda hp, i: (i, hp)), pl.BlockSpec((1, BLK, 128), lambda hp, i: (hp, i, 0))),
        out_shape=(jax.ShapeDtypeStruct((L, 512), F32), jax.ShapeDtypeStruct((4, L, 128), F32)),
        compiler_params=pltpu.CompilerParams(dimension_semantics=("parallel", "arbitrary")),
    )(Q, KV, KV, KR, cs, sn)


def _old_mla_bwd(Q, KV, KR, cs, sn, mixed, dmixed, lse, *, name):
    L = Q.shape[0]
    nb = L // BLK

    def body(q_ref, kn_ref, v_ref, kr_ref, cs_ref, sn_ref, o_ref, do_ref, lse_ref, dq_ref, dkn_ref, dv_ref, dkr_ref):
        hp = pl.program_id(0)
        i = pl.program_id(1)

        @pl.when(i == 0)
        def _():
            dkn_ref[...] = jnp.zeros_like(dkn_ref)
            dv_ref[...] = jnp.zeros_like(dv_ref)

        @pl.when((i == 0) & (hp == 0))
        def _():
            dkr_ref[...] = jnp.zeros_like(dkr_ref)

        qn, qrs = _mla_q(q_ref, cs_ref, sn_ref)
        dos, dd, lses = [], [], []
        for h in range(2):
            sl = slice(64 * h, 64 * h + 64)
            d = do_ref[:, sl]
            dos.append(d.astype(BF16))
            dd.append(jnp.sum(d * o_ref[:, sl], axis=1, keepdims=True))
            lses.append(lse_ref[0, :, h:h + 1])

        def step(j, carry):
            off = pl.multiple_of(j * BLK, BLK)
            knb = kn_ref[pl.ds(off, BLK), :]
            vb = v_ref[pl.ds(off, BLK), :]
            krb = kr_ref[pl.ds(off, BLK), 0:MLA_ROPE]
            mask = _mla_mask(i, j)
            out, dkns, dvs = [], [], []
            dkr = jnp.zeros((BLK, MLA_ROPE), F32)
            for h in range(2):
                dqn, dqr = carry[2 * h], carry[2 * h + 1]
                sl = slice(64 * h, 64 * h + 64)
                s = (_dot_nt(qn[h], knb[:, sl]) + _dot_nt(qrs[h], krb)) * MLA_SCALE
                p = jnp.where(mask, jnp.exp(s - lses[h]), 0.0)
                dp = _dot_nt(dos[h], vb[:, sl])
                ds = (p * (dp - dd[h]) * MLA_SCALE).astype(BF16)
                dkns.append(_dot_tn(ds, qn[h]))
                dvs.append(_dot_tn(p.astype(BF16), dos[h]))
                dkr = dkr + _dot_tn(ds, qrs[h])
                out += [dqn + _dot(ds, knb[:, sl]), dqr + _dot(ds, krb)]
            dkn_ref[pl.ds(off, BLK), :] += jnp.concatenate(dkns, axis=1)
            dv_ref[pl.ds(off, BLK), :] += jnp.concatenate(dvs, axis=1)
            dkr_ref[pl.ds(off, BLK), :] += jnp.concatenate([dkr, jnp.zeros((BLK, 128 - MLA_ROPE), F32)], axis=1)
            return tuple(out)

        z0 = (jnp.zeros((BLK, 64), F32), jnp.zeros((BLK, MLA_ROPE), F32))
        res = lax.fori_loop(0, i + 1, step, z0 + z0)
        dqr = jnp.concatenate([res[1], res[3], jnp.zeros((BLK, 64), F32)], axis=1)
        dq_ref[...] = jnp.concatenate([res[0], res[2], _rope_t(dqr, cs_ref[...], sn_ref[...], MLA_ROPE // 2)], axis=1)

    blk = lambda c0: pl.BlockSpec((BLK, 128), lambda hp, i: (i, c0 + hp))
    full = lambda c0: pl.BlockSpec((L, 128), lambda hp, i: (0, c0 + hp))
    tab = pl.BlockSpec((BLK, 128), lambda hp, i: (i, 0))
    return pl.pallas_call(
        body, name=name, grid=(MLA_HEADS // 2, nb),
        in_specs=[pl.BlockSpec((BLK, 256), lambda hp, i: (i, hp)), full(0), full(4),
                  pl.BlockSpec((L, 128), lambda hp, i: (0, 0)), tab, tab, blk(4), blk(4),
                  pl.BlockSpec((1, BLK, 128), lambda hp, i: (hp, i, 0))],
        out_specs=(pl.BlockSpec((BLK, 256), lambda hp, i: (i, hp)), full(0), full(0),
                   pl.BlockSpec((L, 128), lambda hp, i: (0, 0))),
        out_shape=(jax.ShapeDtypeStruct((L, 1024), F32), jax.ShapeDtypeStruct((L, 512), F32),
                   jax.ShapeDtypeStruct((L, 512), F32), jax.ShapeDtypeStruct((L, 128), F32)),
        compiler_params=pltpu.CompilerParams(dimension_semantics=("arbitrary", "arbitrary")),
    )(Q, KV, KV, KR, cs, sn, mixed, dmixed, lse)


SB_UNROLL = 2


def _tq(L):
    return 384 if L % 384 == 0 else BLK


def _softplus(z):
    na = lax.bitcast_convert_type(lax.bitcast_convert_type(z, jnp.uint32) | jnp.uint32(0x80000000), F32)
    return jnp.maximum(z, 0.0) + jnp.log(1.0 + jnp.exp(na))


def _head_split(x, first):
    zero = jnp.zeros_like(x)
    return jnp.where(first, x, zero), jnp.where(first, zero, x)


def _sb_mask(I, j, tq):
    row = lax.broadcasted_iota(jnp.int32, (tq, BLK), 0)
    col = lax.broadcasted_iota(jnp.int32, (tq, BLK), 1)
    s_idx = j * BLK + col
    return (s_idx < I * tq + row) & (s_idx >= N_PAD)


def _tri2(kind, splits):
    r = lax.broadcasted_iota(jnp.int32, (256, 256), 0)
    c = lax.broadcasted_iota(jnp.int32, (256, 256), 1)
    same = (r < BLK) == (c < BLK)
    t = (same & ((r > c) if kind == "right" else (r < c))).astype(BF16)
    return jnp.concatenate([t] * splits, axis=0)


def _split2(x):
    hi = x.astype(BF16)
    lo = (x - hi.astype(F32)).astype(BF16)
    return jnp.concatenate([hi, lo], axis=1)


def _sb_mask2(I, j, tq):
    row = lax.broadcasted_iota(jnp.int32, (tq, 256), 0)
    col = lax.broadcasted_iota(jnp.int32, (tq, 256), 1)
    s_idx = j * BLK + (col & (BLK - 1))
    return (s_idx < I * tq + row) & (s_idx >= N_PAD)


def _per_head(x, r0, r1):
    return jnp.concatenate([x[:, 0:BLK] + r0, x[:, BLK:2 * BLK] + r1], axis=1)


def sb_fwd(P, *, name, gathers=()):
    L = P.shape[0]
    tq = _tq(L)
    nd = tq // BLK
    ng = len(gathers)
    nhp, nI = SB_HEADS // 2, L // tq

    def body(q_ref, k_ref, v_ref, *rest):
        w_refs, (o_ref, c_ref), g_refs, sems = rest[:ng], rest[ng:ng + 2], rest[ng + 2:2 * ng + 2], rest[2 * ng + 2:]
        hp = pl.program_id(0)
        I = pl.program_id(1)
        _gather_ride(w_refs, g_refs, sems, (hp == 0) & (I == 0), (hp == nhp - 1) & (I == 0), (hp == nhp - 1) & (I == nI - 1))
        tt = _tri2("right", 2)
        lane_q = lax.broadcasted_iota(jnp.int32, (tq, 128), 1)
        first_k = lax.broadcasted_iota(jnp.int32, (BLK, 128), 1) < 64
        q = (q_ref[...] * 0.125).astype(BF16)
        c_ref[...] = jnp.zeros_like(c_ref)

        def tiles(T, carry, kind):
            o, R0, R1 = carry
            js = [T * nd + nd - 1 - u for u in range(nd)]
            st = []
            for j in js:
                off = pl.multiple_of(j * BLK, BLK)
                kcat = jnp.concatenate(_head_split(k_ref[pl.ds(off, BLK), :].astype(BF16), first_k), axis=0)
                st.append([_dot_nt(q, kcat), off])
            for u, (s, j) in enumerate(zip(st, js)):
                sp = _softplus(s[0])
                mask = _sb_mask2(I, j, tq) if kind == "diag" else (pad_ok if kind == "first" and u == nd - 1 else None)
                spm = sp if mask is None else jnp.where(mask, sp, 0.0)
                s += [sp, spm, mask, _dot(_split2(spm), tt)]
            for (z, off, sp, spm, mask, S), j in zip(st, js):
                vcat = jnp.concatenate(_head_split(v_ref[pl.ds(off, BLK), :].astype(BF16), first_k), axis=0)
                w = jnp.exp(_per_head(z - sp - S, R0, R1))
                if mask is not None:
                    w = jnp.where(mask, w, 0.0)
                c_ref[0] = jnp.where(lane_q == j, R0, c_ref[0])
                c_ref[1] = jnp.where(lane_q == j, R1, c_ref[1])
                o = o + _dot(w.astype(BF16), vcat)
                R0 = R0 - (S[:, 0:1] + spm[:, 0:1])
                R1 = R1 - (S[:, BLK:BLK + 1] + spm[:, BLK:BLK + 1])
            return (o, R0, R1)

        pad_ok = (lax.broadcasted_iota(jnp.int32, (tq, 256), 1) & (BLK - 1)) >= N_PAD
        carry = (jnp.zeros((tq, 128), F32), jnp.zeros((tq, 1), F32), jnp.zeros((tq, 1), F32))
        carry = tiles(I, carry, "diag")
        carry = lax.fori_loop(0, jnp.maximum(I - 1, 0), lambda t, c: tiles(I - 1 - t, c, None), carry)
        carry = lax.fori_loop(0, jnp.minimum(I, 1), lambda t, c: tiles(0, c, "first"), carry)
        o_ref[...] = carry[0]

    g_in, g_out, g_shapes, g_sems = _gather_io(gathers)
    return pl.pallas_call(
        body, name=name, grid=(nhp, nI),
        in_specs=[pl.BlockSpec((tq, 128), lambda hp, i: (i, C_SBQ // 128 + hp)),
                  pl.BlockSpec((L, 128), lambda hp, i: (0, C_SBK // 128 + hp)),
                  pl.BlockSpec((L, 128), lambda hp, i: (0, C_SBV // 128 + hp))] + g_in,
        out_specs=[pl.BlockSpec((tq, 128), lambda hp, i: (i, hp)), pl.BlockSpec((2, tq, 128), lambda hp, i: (hp, i, 0))] + g_out,
        out_shape=[jax.ShapeDtypeStruct((L, 1536), F32), jax.ShapeDtypeStruct((SB_HEADS, L, 128), F32)] + g_shapes,
        scratch_shapes=g_sems,
        compiler_params=pltpu.CompilerParams(dimension_semantics=("arbitrary", "arbitrary")),
    )(P, P, P, *gathers)


def sb_bwd(P, carries, dmixed, *, name, rides=()):
    L = P.shape[0]
    tq = _tq(L)
    nd = tq // BLK

    def body(q_ref, k_ref, v_ref, c_ref, do_ref, dq_ref, dk_ref, dv_ref):
        I = pl.program_id(1)

        @pl.when(I == 0)
        def _():
            dk_ref[...] = jnp.zeros_like(dk_ref)
            dv_ref[...] = jnp.zeros_like(dv_ref)

        tr = _tri2("right", 2)
        tl = _tri2("left", 1)
        lane_q = lax.broadcasted_iota(jnp.int32, (tq, 128), 1)
        first_k = lax.broadcasted_iota(jnp.int32, (BLK, 128), 1) < 64
        q = (q_ref[...] * 0.125).astype(BF16)
        do = do_ref[...].astype(BF16)

        def tiles(T, carry, kind):
            dq, PL0, PL1 = carry
            js = [T * nd + u for u in range(nd)]
            st = []
            for j in js:
                off = pl.multiple_of(j * BLK, BLK)
                kcat = jnp.concatenate(_head_split(k_ref[pl.ds(off, BLK), :].astype(BF16), first_k), axis=0)
                vcat = jnp.concatenate(_head_split(v_ref[pl.ds(off, BLK), :].astype(BF16), first_k), axis=0)
                st.append([off, kcat, _dot_nt(q, kcat), _dot_nt(do, vcat)])
            for u, (s, j) in enumerate(zip(st, js)):
                z = s[2]
                sp = _softplus(z)
                mask = _sb_mask2(I, j, tq) if kind == "diag" else (pad_ok if kind == "first" and u == 0 else None)
                spm = sp if mask is None else jnp.where(mask, sp, 0.0)
                s += [mask, jnp.exp(z - sp), _dot(_split2(spm), tr)]
            for s, j in zip(st, js):
                off, kcat, z, dw, mask, sig, S = s
                R0 = jnp.sum(jnp.where(lane_q == j, c_ref[0], 0.0), axis=1, keepdims=True)
                R1 = jnp.sum(jnp.where(lane_q == j, c_ref[1], 0.0), axis=1, keepdims=True)
                w = sig * jnp.exp(_per_head(-S, R0, R1))
                if mask is not None:
                    w = jnp.where(mask, w, 0.0)
                dA = dw * w
                dvf = _dot_tn(w.astype(BF16), do)
                dv_ref[pl.ds(off, BLK), :] += jnp.where(first_k, dvf[0:BLK], dvf[BLK:2 * BLK])
                s += [dA, _dot(dA.astype(BF16), tl)]
            for off, kcat, z, dw, mask, sig, S, dA, pre in st:
                dz = dA - sig * (dA + _per_head(pre, PL0, PL1))
                if mask is not None:
                    dz = jnp.where(mask, dz, 0.0)
                dzb = dz.astype(BF16)
                dkf = _dot_tn(dzb, q)
                dk_ref[pl.ds(off, BLK), :] += jnp.where(first_k, dkf[0:BLK], dkf[BLK:2 * BLK])
                dq = dq + _dot(dzb, kcat)
                PL0 = PL0 + (pre[:, BLK - 1:BLK] + dA[:, BLK - 1:BLK])
                PL1 = PL1 + (pre[:, 2 * BLK - 1:2 * BLK] + dA[:, 2 * BLK - 1:2 * BLK])
            return (dq, PL0, PL1)

        pad_ok = (lax.broadcasted_iota(jnp.int32, (tq, 256), 1) & (BLK - 1)) >= N_PAD
        carry = (jnp.zeros((tq, 128), F32), jnp.zeros((tq, 1), F32), jnp.zeros((tq, 1), F32))
        carry = lax.fori_loop(0, jnp.minimum(I, 1), lambda t, c: tiles(0, c, "first"), carry)
        carry = lax.fori_loop(1, jnp.maximum(I, 1), lambda T, c: tiles(T, c, None), carry)
        carry = tiles(I, carry, "diag")
        dq_ref[...] = carry[0] * 0.125

    blk = lambda c0: pl.BlockSpec((tq, 128), lambda hp, i: (i, c0 + hp))
    full = lambda c0: pl.BlockSpec((L, 128), lambda hp, i: (0, c0 + hp))
    sds = jax.ShapeDtypeStruct((L, 512), F32)
    nhp, nI = SB_HEADS // 2, L // tq
    r_in, r_out, r_shapes, r_sems = _ride_io(rides)
    return pl.pallas_call(
        _ride_kernel(body, 5, 3, rides, nhp, nI), name=name, grid=(nhp, nI),
        in_specs=[blk(C_SBQ // 128), full(C_SBK // 128), full(C_SBV // 128),
                  pl.BlockSpec((2, tq, 128), lambda hp, i: (hp, i, 0)), blk(0)] + r_in,
        out_specs=[blk(0), full(0), full(0)] + r_out, out_shape=[sds, sds, sds] + r_shapes, scratch_shapes=r_sems,
        compiler_params=pltpu.CompilerParams(dimension_semantics=("arbitrary", "arbitrary")),
    )(P, P, P, carries, dmixed, *[a for _, a in rides])


def _v2_sb_fwd(P, *, name):
    L = P.shape[0]
    tq = _tq(L)
    nd = tq // BLK

    def body(q_ref, k_ref, v_ref, o_ref, c_ref):
        I = pl.program_id(1)
        tt = _tri("right")
        lane_q = lax.broadcasted_iota(jnp.int32, (tq, 128), 1)
        first_k = lax.broadcasted_iota(jnp.int32, (BLK, 128), 1) < 64
        qm = [x.astype(BF16) for x in _head_split(q_ref[...] * 0.125, lane_q < 64)]
        c_ref[...] = jnp.zeros_like(c_ref)

        def tile(j, carry, masked):
            o, R = carry[0], carry[1:]
            off = pl.multiple_of(j * BLK, BLK)
            kb = k_ref[pl.ds(off, BLK), :].astype(BF16)
            vcat = jnp.concatenate(_head_split(v_ref[pl.ds(off, BLK), :].astype(BF16), first_k), axis=0)
            mask = _sb_mask(I, j, tq) if masked else None
            ws, Rn = [], []
            for h in range(2):
                z = _dot_nt(qm[h], kb)
                sp = _softplus(z)
                spm = jnp.where(mask, sp, 0.0) if masked else sp
                w = jnp.exp(z - sp - _tri_sum(spm, tt) + R[h])
                if masked:
                    w = jnp.where(mask, w, 0.0)
                c_ref[h] = jnp.where(lane_q == j, R[h], c_ref[h])
                ws.append(w.astype(BF16))
                Rn.append(R[h] - jnp.sum(spm, axis=1, keepdims=True))
            return (o + _dot(jnp.concatenate(ws, axis=1), vcat), Rn[0], Rn[1])

        carry = (jnp.zeros((tq, 128), F32), jnp.zeros((tq, 1), F32), jnp.zeros((tq, 1), F32))
        carry = lax.fori_loop(0, nd, lambda t, c: tile(I * nd + nd - 1 - t, c, True), carry)
        carry = lax.fori_loop(0, jnp.maximum(I * nd - 1, 0), lambda t, c: tile(I * nd - 1 - t, c, False), carry)
        carry = lax.fori_loop(0, jnp.minimum(I, 1), lambda t, c: tile(0, c, True), carry)
        o_ref[...] = carry[0]

    return pl.pallas_call(
        body, name=name, grid=(SB_HEADS // 2, L // tq),
        in_specs=[pl.BlockSpec((tq, 128), lambda hp, i: (i, C_SBQ // 128 + hp)),
                  pl.BlockSpec((L, 128), lambda hp, i: (0, C_SBK // 128 + hp)),
                  pl.BlockSpec((L, 128), lambda hp, i: (0, C_SBV // 128 + hp))],
        out_specs=(pl.BlockSpec((tq, 128), lambda hp, i: (i, hp)), pl.BlockSpec((2, tq, 128), lambda hp, i: (hp, i, 0))),
        out_shape=(jax.ShapeDtypeStruct((L, 512), F32), jax.ShapeDtypeStruct((SB_HEADS, L, 128), F32)),
        compiler_params=pltpu.CompilerParams(dimension_semantics=("parallel", "arbitrary")),
    )(P, P, P)


def _v2_sb_bwd(P, carries, dmixed, *, name):
    L = P.shape[0]
    tq = _tq(L)
    nd = tq // BLK

    def body(q_ref, k_ref, v_ref, c_ref, do_ref, dq_ref, dk_ref, dv_ref):
        I = pl.program_id(1)

        @pl.when(I == 0)
        def _():
            dk_ref[...] = jnp.zeros_like(dk_ref)
            dv_ref[...] = jnp.zeros_like(dv_ref)

        tr = _tri("right")
        tl = _tri("left")
        lane_q = lax.broadcasted_iota(jnp.int32, (tq, 128), 1)
        first_k = lax.broadcasted_iota(jnp.int32, (BLK, 128), 1) < 64
        qm = [x.astype(BF16) for x in _head_split(q_ref[...] * 0.125, lane_q < 64)]
        dom = [x.astype(BF16) for x in _head_split(do_ref[...], lane_q < 64)]
        qcat = jnp.concatenate(qm, axis=0)
        docat = jnp.concatenate(dom, axis=0)

        def tile(j, carry, masked):
            dq, PL = carry[0], carry[1:]
            off = pl.multiple_of(j * BLK, BLK)
            kb = k_ref[pl.ds(off, BLK), :].astype(BF16)
            vb = v_ref[pl.ds(off, BLK), :].astype(BF16)
            kcat = jnp.concatenate(_head_split(kb, first_k), axis=0)
            mask = _sb_mask(I, j, tq) if masked else None
            dzs, wsb, PLn = [], [], []
            for h in range(2):
                R = jnp.sum(jnp.where(lane_q == j, c_ref[h], 0.0), axis=1, keepdims=True)
                z = _dot_nt(qm[h], kb)
                sp = _softplus(z)
                spm = jnp.where(mask, sp, 0.0) if masked else sp
                sig = jnp.exp(z - sp)
                w = sig * jnp.exp(R - _tri_sum(spm, tr))
                if masked:
                    w = jnp.where(mask, w, 0.0)
                dA = _dot_nt(dom[h], vb) * w
                dz = dA - sig * (dA + _tri_sum(dA, tl) + PL[h])
                if masked:
                    dz = jnp.where(mask, dz, 0.0)
                dzs.append(dz.astype(BF16))
                wsb.append(w.astype(BF16))
                PLn.append(PL[h] + jnp.sum(dA, axis=1, keepdims=True))
            dk_ref[pl.ds(off, BLK), :] += _dot_tn(jnp.concatenate(dzs, axis=0), qcat)
            dv_ref[pl.ds(off, BLK), :] += _dot_tn(jnp.concatenate(wsb, axis=0), docat)
            return (dq + _dot(jnp.concatenate(dzs, axis=1), kcat), PLn[0], PLn[1])

        carry = (jnp.zeros((tq, 128), F32), jnp.zeros((tq, 1), F32), jnp.zeros((tq, 1), F32))
        carry = lax.fori_loop(0, jnp.minimum(I, 1), lambda t, c: tile(0, c, True), carry)
        carry = lax.fori_loop(1, jnp.maximum(I * nd, 1), lambda j, c: tile(j, c, False), carry)
        carry = lax.fori_loop(0, nd, lambda t, c: tile(I * nd + t, c, True), carry)
        dq_ref[...] = carry[0] * 0.125

    blk = lambda c0: pl.BlockSpec((tq, 128), lambda hp, i: (i, c0 + hp))
    full = lambda c0: pl.BlockSpec((L, 128), lambda hp, i: (0, c0 + hp))
    sds = jax.ShapeDtypeStruct((L, 512), F32)
    return pl.pallas_call(
        body, name=name, grid=(SB_HEADS // 2, L // tq),
        in_specs=[blk(C_SBQ // 128), full(C_SBK // 128), full(C_SBV // 128),
                  pl.BlockSpec((2, tq, 128), lambda hp, i: (hp, i, 0)), blk(0)],
        out_specs=(blk(0), full(0), full(0)), out_shape=(sds, sds, sds),
        compiler_params=pltpu.CompilerParams(dimension_semantics=("parallel", "arbitrary")),
    )(P, P, P, carries, dmixed)


def _mla_mask2(I, j, tq):
    row = lax.broadcasted_iota(jnp.int32, (tq, tq), 0)
    col = lax.broadcasted_iota(jnp.int32, (tq, tq), 1)
    t_idx = I * tq + row
    s_idx = j * tq + col
    return (s_idx <= t_idx) & ((s_idx >= N_PAD) | (s_idx == t_idx))


def _mla_qcat(q_ref, cs_ref, sn_ref, lane_q):
    qn = q_ref[:, 0:128]
    qr = _rope(q_ref[:, 128:256], cs_ref[...], sn_ref[...], MLA_ROPE // 2)
    zero = jnp.zeros_like(qn)
    r0 = lane_q < MLA_ROPE
    r1 = (lane_q >= MLA_ROPE) & (lane_q < 2 * MLA_ROPE)
    n0, n1 = _head_split(qn, lane_q < 64)
    return [jnp.concatenate([n0, jnp.where(r0, qr, zero)], axis=1).astype(BF16),
            jnp.concatenate([n1, jnp.where(r1, qr, zero)], axis=1).astype(BF16)]


def mla_fwd(Q, KV, KR, cs, sn, mixed, *, name, gathers=()):
    L = Q.shape[0]
    tq = _tq(L)

    def body(q_ref, kn_ref, v_ref, kr_ref, cs_ref, sn_ref, o_ref, lse_ref):
        I = pl.program_id(1)
        lane_q = lax.broadcasted_iota(jnp.int32, (tq, 128), 1)
        first_q = lane_q < 64
        qcat = _mla_qcat(q_ref, cs_ref, sn_ref, lane_q)

        def tile(j, carry, masked, wide=1):
            acc, ml = carry[0], carry[1:]
            off = pl.multiple_of(j * tq, tq)
            tk = wide * tq
            first_k = lax.broadcasted_iota(jnp.int32, (tk, 128), 1) < 64
            kcat = jnp.concatenate([kn_ref[pl.ds(off, tk), :], kr_ref[pl.ds(off, tk), :]], axis=1)
            vcat = jnp.concatenate(_head_split(v_ref[pl.ds(off, tk), :], first_k), axis=0)
            mask = _mla_mask2(I, j, tq) if masked else None
            ps, al, out = [], [], []
            for h in range(2):
                m, l = ml[2 * h], ml[2 * h + 1]
                s = _dot_nt(qcat[h], kcat) * MLA_SCALE
                if masked:
                    s = jnp.where(mask, s, NEG)
                m_new = jnp.maximum(m, jnp.max(s, axis=1, keepdims=True))
                a = jnp.exp(m - m_new)
                p = jnp.exp(s - m_new)
                ps.append(p.astype(BF16))
                al.append(a)
                out += [m_new, a * l + jnp.sum(p, axis=1, keepdims=True)]
            acc = acc * jnp.where(first_q, al[0], al[1]) + _dot(jnp.concatenate(ps, axis=1), vcat)
            return (acc,) + tuple(out)

        ml0 = (jnp.full((tq, 1), NEG, F32), jnp.zeros((tq, 1), F32))
        carry = (jnp.zeros((tq, 128), F32),) + ml0 + ml0
        carry = lax.fori_loop(0, jnp.minimum(I, 1), lambda t, c: tile(0, c, True), carry)
        n_in = jnp.maximum(I - 1, 0)
        carry = lax.fori_loop(0, n_in // 2, lambda t, c: tile(1 + 2 * t, c, False, 2), carry)
        carry = lax.fori_loop(0, n_in % 2, lambda t, c: tile(I - 1, c, False), carry)
        carry = tile(I, carry, True)
        acc, m0, l0, m1, l1 = carry
        o_ref[...] = acc / jnp.where(first_q, l0, l1)
        lse_ref[0] = jnp.where(lane_q == 0, m0 + jnp.log(l0), jnp.where(lane_q == 1, m1 + jnp.log(l1), 0.0))

    ng = len(gathers)
    nhp, nI = MLA_HEADS // 2, L // tq

    def kern(q, kn, v, kr, c, s, mixed_any, *rest):
        w_refs, (o, lse), g_refs, sems = rest[:ng], rest[ng:ng + 2], rest[ng + 2:2 * ng + 2], rest[2 * ng + 2:]
        hp, I = pl.program_id(0), pl.program_id(1)
        _gather_ride(w_refs, g_refs, sems, (hp == 0) & (I == 0), (hp == nhp - 1) & (I == 0), (hp == nhp - 1) & (I == nI - 1))
        body(q, kn, v, kr, c, s, o, lse)

    g_in, g_out, g_shapes, g_sems = _gather_io(gathers)
    return pl.pallas_call(
        kern, name=name, grid=(nhp, nI),
        in_specs=[pl.BlockSpec((tq, 256), lambda hp, i: (i, hp)),
                  pl.BlockSpec((L, 128), lambda hp, i: (0, hp)),
                  pl.BlockSpec((L, 128), lambda hp, i: (0, 4 + hp)),
                  pl.BlockSpec((L, 128), lambda hp, i: (0, 0)),
                  pl.BlockSpec((tq, 128), lambda hp, i: (i, 0)), pl.BlockSpec((tq, 128), lambda hp, i: (i, 0)),
                  pl.BlockSpec(memory_space=pl.ANY)] + g_in,
        out_specs=[pl.BlockSpec((tq, 128), lambda hp, i: (i, 4 + hp)), pl.BlockSpec((1, tq, 128), lambda hp, i: (hp, i, 0))] + g_out,
        out_shape=[jax.ShapeDtypeStruct(mixed.shape, F32), jax.ShapeDtypeStruct((4, L, 128), F32)] + g_shapes,
        input_output_aliases={6: 0}, scratch_shapes=g_sems,
        compiler_params=pltpu.CompilerParams(dimension_semantics=("arbitrary", "arbitrary")),
    )(Q, KV, KV, KR, cs, sn, mixed, *gathers)


def mla_bwd(Q, KV, KR, cs, sn, mixed, dmixed, lse, *, name, rides=()):
    L = Q.shape[0]
    tq = _tq(L)

    def body(q_ref, kn_ref, v_ref, kr_ref, cs_ref, sn_ref, o_ref, do_ref, lse_ref, dq_ref, dkn_ref, dv_ref, dkr_ref):
        hp = pl.program_id(0)
        I = pl.program_id(1)

        @pl.when(I == 0)
        def _():
            dkn_ref[...] = jnp.zeros_like(dkn_ref)
            dv_ref[...] = jnp.zeros_like(dv_ref)

        @pl.when((I == 0) & (hp == 0))
        def _():
            dkr_ref[...] = jnp.zeros_like(dkr_ref)

        lane_q = lax.broadcasted_iota(jnp.int32, (tq, 128), 1)
        first_q = lane_q < 64
        qcat = _mla_qcat(q_ref, cs_ref, sn_ref, lane_q)
        qq = jnp.concatenate(qcat, axis=0)
        do = do_ref[...]
        prod = do * o_ref[...]
        dd = [jnp.sum(jnp.where(first_q, prod, 0.0), axis=1, keepdims=True),
              jnp.sum(jnp.where(first_q, 0.0, prod), axis=1, keepdims=True)]
        dom = [x.astype(BF16) for x in _head_split(do, first_q)]
        docat = jnp.concatenate(dom, axis=0)
        lses = [lse_ref[0, :, 0:1], lse_ref[0, :, 1:2]]

        def tile(j, dq, masked, wide=1):
            off = pl.multiple_of(j * tq, tq)
            tk = wide * tq
            lane_k = lax.broadcasted_iota(jnp.int32, (tk, 256), 1)
            sel0 = (lane_k < 64) | ((lane_k >= 128) & (lane_k < 128 + MLA_ROPE))
            sel1 = ((lane_k >= 64) & (lane_k < 128)) | ((lane_k >= 128 + MLA_ROPE) & (lane_k < 128 + 2 * MLA_ROPE))
            kcat = jnp.concatenate([kn_ref[pl.ds(off, tk), :], kr_ref[pl.ds(off, tk), :]], axis=1)
            vb = v_ref[pl.ds(off, tk), :]
            zero = jnp.zeros_like(kcat)
            kk = jnp.concatenate([jnp.where(sel0, kcat, zero), jnp.where(sel1, kcat, zero)], axis=0)
            mask = _mla_mask2(I, j, tq) if masked else None
            dss, pbs = [], []
            for h in range(2):
                s = _dot_nt(qcat[h], kcat) * MLA_SCALE
                p = jnp.exp(s - lses[h])
                if masked:
                    p = jnp.where(mask, p, 0.0)
                dp = _dot_nt(dom[h], vb)
                dss.append((p * (dp - dd[h]) * MLA_SCALE).astype(BF16))
                pbs.append(p.astype(BF16))
            dkc = _dot_tn(jnp.concatenate(dss, axis=0), qq)
            dkn_ref[pl.ds(off, tk), :] += dkc[:, 0:128]
            dkr_ref[pl.ds(off, tk), :] += dkc[:, 128:256]
            dv_ref[pl.ds(off, tk), :] += _dot_tn(jnp.concatenate(pbs, axis=0), docat)
            return dq + _dot(jnp.concatenate(dss, axis=1), kk)

        dq = jnp.zeros((tq, 256), F32)
        dq = lax.fori_loop(0, jnp.minimum(I, 1), lambda t, c: tile(0, c, True), dq)
        n_in = jnp.maximum(I - 1, 0)
        dq = lax.fori_loop(0, n_in // 2, lambda t, c: tile(1 + 2 * t, c, False, 2), dq)
        dq = lax.fori_loop(0, n_in % 2, lambda t, c: tile(I - 1, c, False), dq)
        dq = tile(I, dq, True)
        dq_ref[:, 0:128] = dq[:, 0:128]
        dq_ref[:, 128:256] = _rope_t(dq[:, 128:256], cs_ref[...], sn_ref[...], MLA_ROPE // 2)

    blk = lambda c0: pl.BlockSpec((tq, 128), lambda hp, i: (i, c0 + hp))
    full = lambda c0: pl.BlockSpec((L, 128), lambda hp, i: (0, c0 + hp))
    tab = pl.BlockSpec((tq, 128), lambda hp, i: (i, 0))
    nhp, nI = MLA_HEADS // 2, L // tq
    r_in, r_out, r_shapes, r_sems = _ride_io(rides)
    return pl.pallas_call(
        _ride_kernel(body, 9, 4, rides, nhp, nI), name=name, grid=(nhp, nI),
        in_specs=[pl.BlockSpec((tq, 256), lambda hp, i: (i, hp)), full(0), full(4),
                  pl.BlockSpec((L, 128), lambda hp, i: (0, 0)), tab, tab, blk(4), blk(4),
                  pl.BlockSpec((1, tq, 128), lambda hp, i: (hp, i, 0))] + r_in,
        out_specs=[pl.BlockSpec((tq, 256), lambda hp, i: (i, hp)), full(0), full(0),
                   pl.BlockSpec((L, 128), lambda hp, i: (0, 0))] + r_out,
        out_shape=[jax.ShapeDtypeStruct((L, 1024), F32), jax.ShapeDtypeStruct((L, 512), F32),
                   jax.ShapeDtypeStruct((L, 512), F32), jax.ShapeDtypeStruct((L, 128), F32)] + r_shapes,
        scratch_shapes=r_sems,
        compiler_params=pltpu.CompilerParams(dimension_semantics=("arbitrary", "arbitrary")),
    )(Q, KV, KV, KR, cs, sn, mixed, dmixed, lse, *[a for _, a in rides])


def _ret_decay(h):
    lg = RET_LOG_G[h]
    r = lax.broadcasted_iota(jnp.int32, (BLK, BLK), 0)
    c = lax.broadcasted_iota(jnp.int32, (BLK, BLK), 1)
    diff = (r - c).astype(F32)
    d_in = jnp.where(diff >= 0, jnp.exp(jnp.maximum(diff, 0.0) * lg), 0.0)
    idx = lax.broadcasted_iota(jnp.int32, (BLK, 1), 0).astype(F32)
    q_decay = jnp.exp((idx + 1.0) * lg)
    k_decay = jnp.exp((BLK - 1.0 - idx) * lg)
    c_decay = math.exp(BLK * lg)
    return d_in, q_decay, k_decay, c_decay


def _ret_qk(qk_ref, cs_ref, sn_ref, n):
    cs = jnp.concatenate([cs_ref[...]] * 2, axis=1)
    sn = jnp.concatenate([sn_ref[...]] * 2, axis=1)
    rq = _rope(qk_ref[:, 0:256], cs, sn, RET_QK // 2)
    row = n * BLK + lax.broadcasted_iota(jnp.int32, (BLK, 256), 0)
    kmul = jnp.where(row >= N_PAD, 0.125, 0.0)
    rk = _rope(qk_ref[:, 256:512], cs, sn, RET_QK // 2) * kmul
    return rq, rk, cs, sn, kmul


def _head_norm(y):
    mu = jnp.mean(y, axis=-1, keepdims=True)
    yc = y - mu
    r = lax.rsqrt(jnp.mean(jnp.square(yc), axis=-1, keepdims=True) + LN_EPS)
    return yc * r, r


def ret_fwd(P, cs, sn, mixed, *, name):
    L = P.shape[0]
    nb = L // BLK

    def body(qk_ref, v_ref, g_ref, cs_ref, sn_ref, o_ref, y_ref, st_ref, state):
        n = pl.program_id(0)

        @pl.when(n == 0)
        def _():
            state[...] = jnp.zeros_like(state)

        st_ref[0] = state[...]
        rq, rk, _, _, _ = _ret_qk(qk_ref, cs_ref, sn_ref, n)
        outs, ys = [], []
        for h in range(RET_HEADS):
            d_in, q_decay, k_decay, c_decay = _ret_decay(h)
            q = rq[:, 64 * h:64 * h + 64].astype(BF16)
            kf = rk[:, 64 * h:64 * h + 64]
            v = v_ref[:, 128 * h:128 * h + 128].astype(BF16)
            S = state[h]
            inner = _dot_nt(q, kf.astype(BF16)) * d_in
            y = _dot(inner.astype(BF16), v) + _dot(q, S.astype(BF16)) * q_decay
            state[h] = S * c_decay + _dot_tn((kf * k_decay).astype(BF16), v)
            g = g_ref[:, 128 * h:128 * h + 128]
            ys.append(y)
            outs.append(g * jax.nn.sigmoid(g) * _head_norm(y)[0])
        o_ref[...] = jnp.concatenate(outs, axis=1)
        y_ref[...] = jnp.concatenate(ys, axis=1)

    blk512 = lambda c: pl.BlockSpec((BLK, 512), lambda n: (n, c))
    tab = pl.BlockSpec((BLK, 128), lambda n: (n, 0))
    return pl.pallas_call(
        lambda qk, v, g, c, s, mixed_any, o, y, st, state: body(qk, v, g, c, s, o, y, st, state),
        name=name, grid=(nb,),
        in_specs=[blk512(C_RQ // 512), blk512(C_RV // 512), blk512(C_RG // 512), tab, tab, pl.BlockSpec(memory_space=pl.ANY)],
        out_specs=(blk512(2), blk512(0), pl.BlockSpec((1, RET_HEADS, RET_QK, RET_V), lambda n: (n, 0, 0, 0))),
        out_shape=(jax.ShapeDtypeStruct(mixed.shape, F32), jax.ShapeDtypeStruct((L, 512), F32),
                   jax.ShapeDtypeStruct((nb, RET_HEADS, RET_QK, RET_V), F32)),
        input_output_aliases={5: 0},
        scratch_shapes=[pltpu.VMEM((RET_HEADS, RET_QK, RET_V), F32)],
        compiler_params=pltpu.CompilerParams(dimension_semantics=("arbitrary",)),
    )(P, P, P, cs, sn, mixed)


def ret_bwd(P, y, states, dmixed, cs, sn, *, name):
    L = P.shape[0]
    nb = L // BLK

    def body(qk_ref, v_ref, g_ref, y_ref, st_ref, do_ref, cs_ref, sn_ref, dqk_ref, dv_ref, dg_ref, dstate):
        n = nb - 1 - pl.program_id(0)

        @pl.when(pl.program_id(0) == 0)
        def _():
            dstate[...] = jnp.zeros_like(dstate)

        rq, rk, cs, sn, kmul = _ret_qk(qk_ref, cs_ref, sn_ref, n)
        dqs, dks, dvs, dgs = [], [], [], []
        for h in range(RET_HEADS):
            d_in, q_decay, k_decay, c_decay = _ret_decay(h)
            sv = slice(128 * h, 128 * h + 128)
            q = rq[:, 64 * h:64 * h + 64].astype(BF16)
            kf = rk[:, 64 * h:64 * h + 64]
            k = kf.astype(BF16)
            kd = (kf * k_decay).astype(BF16)
            v = v_ref[:, sv].astype(BF16)
            g = g_ref[:, sv]
            do = do_ref[:, sv]
            yh = y_ref[:, sv]
            S = st_ref[0, h].astype(BF16)
            dS = dstate[h]
            sg = jax.nn.sigmoid(g)
            yn, r = _head_norm(yh)
            dgs.append(do * yn * (sg * (1.0 + g * (1.0 - sg))))
            dyn = do * (g * sg)
            dy = r * (dyn - jnp.mean(dyn, axis=-1, keepdims=True) - yn * jnp.mean(dyn * yn, axis=-1, keepdims=True))
            dyb = dy.astype(BF16)
            dyq = (dy * q_decay).astype(BF16)
            inner = (_dot_nt(q, k) * d_in).astype(BF16)
            A = (_dot_nt(dyb, v) * d_in).astype(BF16)
            dSb = dS.astype(BF16)
            dqs.append(_dot(A, k) + _dot_nt(dyq, S))
            dks.append(_dot_tn(A, q) + _dot_nt(v, dSb) * k_decay)
            dvs.append(_dot_tn(inner, dyb) + _dot(kd, dSb))
            dstate[h] = dS * c_decay + _dot_tn(q, dyq)
        drq = _rope_t(jnp.concatenate(dqs, axis=1), cs, sn, RET_QK // 2)
        drk = _rope_t(jnp.concatenate(dks, axis=1) * kmul, cs, sn, RET_QK // 2)
        dqk_ref[...] = jnp.concatenate([drq, drk], axis=1)
        dv_ref[...] = jnp.concatenate(dvs, axis=1)
        dg_ref[...] = jnp.concatenate(dgs, axis=1)

    blk512 = lambda c: pl.BlockSpec((BLK, 512), lambda t: (nb - 1 - t, c))
    tab = pl.BlockSpec((BLK, 128), lambda t: (nb - 1 - t, 0))
    sds = jax.ShapeDtypeStruct((L, 512), F32)
    return pl.pallas_call(
        body, name=name, grid=(nb,),
        in_specs=[blk512(C_RQ // 512), blk512(C_RV // 512), blk512(C_RG // 512), blk512(0),
                  pl.BlockSpec((1, RET_HEADS, RET_QK, RET_V), lambda t: (nb - 1 - t, 0, 0, 0)), blk512(2), tab, tab],
        out_specs=(blk512(0), blk512(0), blk512(0)), out_shape=(sds, sds, sds),
        scratch_shapes=[pltpu.VMEM((RET_HEADS, RET_QK, RET_V), F32)],
        compiler_params=pltpu.CompilerParams(dimension_semantics=("arbitrary",)),
    )(P, P, P, y, states, dmixed, cs, sn)


def _perm_w_in(w):
    pad = jnp.zeros(w.shape[:-1] + (N_INP - N_IN,), w.dtype)
    return jnp.concatenate([w[..., 0:1536], w[..., 2208:3744], w[..., 1536:2208], pad], axis=-1)


def _unperm_w_in(g):
    return jnp.concatenate([g[..., 0:1536], g[..., 3072:3744], g[..., 1536:3072]], axis=-1)


def _perm_w_uq(w):
    lead = w.shape[:-1]
    w5 = w.reshape(lead + (4, 2, 96))
    nope = w5[..., :64].reshape(lead + (4, 128))
    rope = w5[..., 64:].reshape(lead + (4, 64))
    return jnp.concatenate([nope, rope, jnp.zeros(lead + (4, 64), w.dtype)], axis=-1).reshape(lead + (1024,))


def _unperm_w_uq(g):
    lead = g.shape[:-1]
    g4 = g.reshape(lead + (4, 256))
    nope = g4[..., :128].reshape(lead + (4, 2, 64))
    rope = g4[..., 128:192].reshape(lead + (4, 2, 32))
    return jnp.concatenate([nope, rope], axis=-1).reshape(lead + (768,))


def _perm_w_ukv(w):
    lead = w.shape[:-1]
    w4 = w.reshape(lead + (8, 128))
    return jnp.concatenate([w4[..., :64].reshape(lead + (512,)), w4[..., 64:].reshape(lead + (512,))], axis=-1)


def _unperm_w_ukv(g):
    lead = g.shape[:-1]
    return jnp.concatenate([g[..., :512].reshape(lead + (8, 64)), g[..., 512:].reshape(lead + (8, 64))],
                           axis=-1).reshape(lead + (1024,))


def _col_shards(a):
    return jnp.moveaxis(a.reshape(a.shape[0], a.shape[1], 4, a.shape[2] // 4), 2, 0)


_RS_SHAPES = {"w_in": (D_MODEL, N_IN // 4), "w_uq": (MLA_Q_LORA, 192), "w_ukv": (MLA_KV_LORA, 256),
              "w_out": (384, D_MODEL), "w_ff1": (D_MODEL, D_FF // 4), "w_ff2": (D_FF // 4, D_MODEL)}


def _rope_tables(L, half):
    pos = (jnp.arange(L) - N_PAD).astype(F32)
    inv = ROPE_THETA ** (-jnp.arange(half, dtype=F32) / half)
    ang = pos[:, None] * inv[None, :]
    cos, sin = jnp.cos(ang), jnp.sin(ang)
    reps = 128 // (2 * half)
    cs = jnp.tile(jnp.concatenate([cos, cos], axis=1), (1, reps))
    sn = jnp.tile(jnp.concatenate([-sin, sin], axis=1), (1, reps))
    return cs, sn


def _device_step(x, target, meta, ln_emb_g, ln_emb_b, w_in, q_norm, kv_norm, w_uq, w_ukv, w_out,
                 ln1_g, ln1_b, w_ff1, w_ff2, ln2_g, ln2_b, late=None):
    S = x.shape[0]
    L = S + BLK
    depth = w_in.shape[0]
    cs_m, sn_m = _rope_tables(L, MLA_ROPE // 2)
    cs_r, sn_r = _rope_tables(L, RET_QK // 2)
    hcat = jnp.concatenate([jnp.zeros((N_PAD, D_MODEL), F32), meta, x], axis=0)
    h, hb, _ = ln_fwd(hcat, ln_emb_g, ln_emb_b, name="ln_emb_fwd")

    w_in_sb, w_in_rest = w_in[..., :N_SB], w_in[..., N_SB:]

    def own_slot(gathered, shard):
        return jnp.moveaxis(lax.dynamic_update_slice(gathered, shard[None], (late["s0"], 0, 0, 0)), 0, 1)

    saved = []
    for l in range(depth):
        Psb = mm_nn(hb, w_in_sb[l], tn=N_SB, name=f"in_proj_sb_{l}", out_dtype=BF16)
        P = mm_nn(hb, w_in_rest[l], tn=768, name=f"in_proj_{l}")
        if late is not None and l == 0:
            mixed, sbc, ga, gb = sb_fwd(Psb, name=f"sb_fwd_{l}", gathers=(late["w_out"], late["w_ff1"]))
            w_out, w_ff1 = own_slot(ga, late["w_out"]), own_slot(gb, late["w_ff1"])
        else:
            mixed, sbc = sb_fwd(Psb, name=f"sb_fwd_{l}")
        nq, nkv, KR = mla_pre_fwd(P, q_norm[l], kv_norm[l], cs_m, sn_m, name=f"mla_pre_fwd_{l}")
        Q = mm_nn(nq, w_uq[l], tn=512, name=f"uq_{l}")
        KV = mm_nn(nkv, w_ukv[l], tn=512, name=f"ukv_{l}", out_dtype=BF16)
        if late is not None and l == 0:
            mixed, lse, ga = mla_fwd(Q, KV, KR, cs_m, sn_m, mixed, name=f"mla_fwd_{l}", gathers=(late["w_ff2"],))
            w_ff2 = own_slot(ga, late["w_ff2"])
        else:
            mixed, lse = mla_fwd(Q, KV, KR, cs_m, sn_m, mixed, name=f"mla_fwd_{l}")
        mixed, y, states = ret_fwd(P, cs_r, sn_r, mixed, name=f"ret_fwd_{l}")
        w_out_l = w_out[l].reshape(1, 1536, D_MODEL)
        mix = mm_nn(mixed, w_out_l, tn=1024, name=f"out_proj_{l}")
        h1, h1b, z1 = ln_fwd(mix, ln1_g[l], ln1_b[l], res=h, name=f"ln1_fwd_{l}")
        U = mm_nn(h1b, w_ff1[l], tn=1024, name=f"ff1_{l}")
        w_ff2_l = w_ff2[l].reshape(1, D_FF, D_MODEL)
        mlp = mm_nn(U, w_ff2_l, tn=1024, tk=2048, prologue="relu2", name=f"ff2_{l}")
        h2, h2b, z2 = ln_fwd(mlp, ln2_g[l], ln2_b[l], res=h1, name=f"ln2_fwd_{l}")
        saved.append((hb, Psb, P, sbc, nq, nkv, KR, Q, KV, lse, y, states, mixed, z1, h1b, U, z2))
        h, hb = h2, h2b

    loss_t, dh = loss_fwd_bwd(h, target, name="loss")

    grads = {k: [None] * depth for k in ("q_norm", "kv_norm", "ln1_g", "ln1_b", "ln2_g", "ln2_b")}
    pairs = depth // 2
    g_ff1 = [lax.empty((4, 2, D_MODEL, D_FF // 4), F32) for _ in range(pairs)]
    g_ff2 = [lax.empty((4, 2, D_FF // 4, D_MODEL), F32) for _ in range(pairs)]
    g_out = [lax.empty((4, 2, 384, D_MODEL), F32) for _ in range(pairs)]
    g_in = [lax.empty((2, D_MODEL, N_INP), F32) for _ in range(pairs)]
    g_uq = [lax.empty((2, MLA_Q_LORA, 1024), F32) for _ in range(pairs)]
    g_ukv = [lax.empty((2, MLA_KV_LORA, 1024), F32) for _ in range(pairs)]

    def pair_grads(p):
        return {"w_in": _col_shards(_unperm_w_in(g_in[p])), "w_uq": _col_shards(_unperm_w_uq(g_uq[p])),
                "w_ukv": _col_shards(_unperm_w_ukv(g_ukv[p])), "w_out": g_out[p], "w_ff1": g_ff1[p], "w_ff2": g_ff2[p]}

    riding = {1: (1, tuple(_RS_SHAPES)), 0: (0, ("w_out", "w_ff1", "w_ff2"))} if late is not None and depth == 4 else {}
    done = set()
    if late is not None:
        acc = {k: lax.empty((depth,) + s, F32) for k, s in _RS_SHAPES.items()}
    for l in reversed(range(depth)):
        p, lp = l // 2, l % 2
        hb_in, Psb, P, sbc, nq, nkv, KR, Q, KV, lse, y, states, mixed, z1, h1b, U, z2 = saved[l]
        dz2, grads["ln2_g"][l], grads["ln2_b"][l] = ln_bwd(dh, z2, ln2_g[l], name=f"ln2_bwd_{l}")
        w_ff2_l = w_ff2[l].reshape(1, D_FF, D_MODEL)
        g_ff2[p] = mm_tn(U, dz2, shards=1, tko=1024, tn=1024, prologue="relu2", name=f"ff2_dw_{l}", into=(g_ff2[p], lp, "rows"))
        dU = mm_nt(dz2, w_ff2_l, tn=1024, tko=1024, relu2grad=U, name=f"ff2_dx_{l}", out_dtype=BF16)
        g_ff1[p] = mm_tn(h1b, dU, shards=4, tko=1024, tn=1024, name=f"ff1_dw_{l}", into=(g_ff1[p], lp, "cols"))
        dh1 = mm_nt(dU, w_ff1[l], tn=1024, tko=1024, axpy=(dz2, DN_ALPHA), name=f"ff1_dx_{l}")
        dz1, grads["ln1_g"][l], grads["ln1_b"][l] = ln_bwd(dh1, z1, ln1_g[l], name=f"ln1_bwd_{l}")
        w_out_l = w_out[l].reshape(1, 1536, D_MODEL)
        g_out[p] = mm_tn(mixed, dz1, shards=1, tko=384, tn=1024, name=f"out_dw_{l}", into=(g_out[p], lp, "rows"))
        dmixed = mm_nt(dz1, w_out_l, tn=1024, tko=1536, name=f"out_dx_{l}")
        d_rqk, d_rv, d_rg = ret_bwd(P, y, states, dmixed, cs_r, sn_r, name=f"ret_bwd_{l}")
        if l in riding:
            rp, rkeys = riding[l]
            GA = pair_grads(rp)
            dQ, dKN, dV, dKR, *Bs = mla_bwd(Q, KV, KR, cs_m, sn_m, mixed, dmixed, lse, name=f"mla_bwd_{l}",
                                            rides=[("sib", GA[k]) for k in rkeys])
        else:
            dQ, dKN, dV, dKR = mla_bwd(Q, KV, KR, cs_m, sn_m, mixed, dmixed, lse, name=f"mla_bwd_{l}")
        dKV = jnp.concatenate([dKN, dV], axis=1)
        g_uq[p] = mm_tn(nq, dQ, shards=1, tko=MLA_Q_LORA, tn=512, name=f"uq_dw_{l}", into=(g_uq[p], lp, "layer"))
        g_ukv[p] = mm_tn(nkv, dKV, shards=1, tko=MLA_KV_LORA, tn=512, name=f"ukv_dw_{l}", into=(g_ukv[p], lp, "layer"))
        dnq = mm_nt(dQ, w_uq[l], tn=1024, tko=MLA_Q_LORA, name=f"uq_dx_{l}")
        dnkv = mm_nt(dKV, w_ukv[l], tn=1024, tko=MLA_KV_LORA, name=f"ukv_dx_{l}")
        d_lat, grads["q_norm"][l], grads["kv_norm"][l] = mla_pre_bwd(P, dnq, dnkv, dKR, q_norm[l], kv_norm[l], cs_m, sn_m,
                                                                     name=f"mla_pre_bwd_{l}")
        if l in riding:
            As = [add_halves(GA[k], B, late["c"], name=f"rs_add1_r{rp}_{k}") for k, B in zip(rkeys, Bs)]
            dq_sb, dk_sb, dv_sb, *Bcs = sb_bwd(Psb, sbc, dmixed, name=f"sb_bwd_{l}", rides=[("chips", A) for A in As])
            for k, B, Bc in zip(rkeys, Bs, Bcs):
                acc[k] = reduce_scatter_finish(GA[k], B, Bc, late["c"], late["s0"], acc[k], 2 * rp, tag=f"r{rp}_{k}")
                done.add((rp, k))
        else:
            dq_sb, dk_sb, dv_sb = sb_bwd(Psb, sbc, dmixed, name=f"sb_bwd_{l}")
        dP = jnp.concatenate([dq_sb, dk_sb, dv_sb, d_rqk, d_rv, d_rg, d_lat], axis=1).astype(BF16)
        g_in[p] = mm_tn(hb_in, dP, shards=1, tko=1024, tn=1280, name=f"in_dw_{l}", into=(g_in[p], lp, "layer"))
        dh = mm_nt(dP, w_in[l], tn=1920, tko=1024, axpy=(dz1, DN_ALPHA), name=f"in_dx_{l}")

    dhcat, dg_emb, db_emb = ln_bwd(dh, hcat, ln_emb_g, name="ln_emb_bwd")
    out = {k: jnp.stack(v) for k, v in grads.items()}
    if late is None:
        out.update({k: jnp.concatenate([pair_grads(p)[k] for p in range(pairs)], axis=1) for k in _RS_SHAPES})
    else:
        for p in range(pairs):
            for k, G in pair_grads(p).items():
                if (p, k) not in done:
                    acc[k] = reduce_scatter_weight(G, late["c"], late["s0"], acc[k], 2 * p, tag=f"{p}_{k}")
        out.update(acc)
    out["ln_emb_g"], out["ln_emb_b"] = dg_emb, db_emb
    out["meta"] = dhcat[N_PAD:BLK]
    return loss_t[0, 0], dhcat[BLK:], out


MESH = pl.DeviceIdType.MESH
PEER_XOR = (2, 1, 3)
_HBM = pl.BlockSpec(memory_space=pltpu.HBM)


def _place():
    x, y, c = lax.axis_index("x"), lax.axis_index("y"), lax.axis_index("c")
    peers = [(1 - x, y, c), (x, 1 - y, c), (1 - x, 1 - y, c)]
    return x, y, c, 2 * x + y, peers, (x, y, 1 - c)


def _gather_plan(w_ref, out_ref, send_sems, recv_sems, base):
    x, y, c, s0, peers, sibling = _place()
    hl = w_ref.shape[0] // 2

    def piece(s, half):
        return out_ref.at[s, pl.ds(half * hl, hl)]

    def copy(k, s, half, to, src=None):
        return pltpu.make_async_remote_copy(src_ref=piece(s, half) if src is None else src, dst_ref=piece(s, half),
                                            send_sem=send_sems.at[base + k], recv_sem=recv_sems.at[base + k],
                                            device_id=to, device_id_type=MESH)

    def first():
        return [copy(k, s0, c, peers[k], src=w_ref.at[pl.ds(c * hl, hl)]) for k in range(3)]

    def passed():
        return [copy(3 + k, s0 ^ PEER_XOR[k], c, sibling) for k in range(3)]

    def start():
        for cp in first():
            cp.start()

    def forward():
        for k, cp in enumerate(passed()):
            copy(k, s0 ^ PEER_XOR[k], c, peers[k]).wait_recv()
            cp.start()

    def finish():
        for k in range(3):
            copy(3 + k, s0 ^ PEER_XOR[k], 1 - c, sibling).wait_recv()
        for cp in first() + passed():
            cp.wait_send()

    return start, forward, finish


def _gather_io(gathers):
    n = len(gathers)
    return ([_HBM] * n, [_HBM] * n, [jax.ShapeDtypeStruct((4,) + w.shape, w.dtype) for w in gathers],
            [pltpu.SemaphoreType.DMA((6 * n,)), pltpu.SemaphoreType.DMA((6 * n,))] if n else [])


def _gather_ride(w_refs, g_refs, sems, at_start, at_forward, at_finish):
    if not w_refs:
        return
    plans = [_gather_plan(w, g, sems[0], sems[1], 6 * n) for n, (w, g) in enumerate(zip(w_refs, g_refs))]
    for step, cond in enumerate((at_start, at_forward, at_finish)):
        @pl.when(cond)
        def _():
            for p in plans:
                p[step]()


def _ride_copies(kind, src_ref, dst_ref, send_sems, recv_sems, base):
    x, y, c, s0, peers, sibling = _place()
    if kind == "sib":
        hl = src_ref.shape[1] // 2
        return [pltpu.make_async_remote_copy(src_ref=src_ref.at[:, pl.ds((1 - c) * hl, hl)], dst_ref=dst_ref,
                                             send_sem=send_sems.at[base], recv_sem=recv_sems.at[base],
                                             device_id=sibling, device_id_type=MESH)]
    return [pltpu.make_async_remote_copy(src_ref=src_ref.at[s0 ^ PEER_XOR[k]], dst_ref=dst_ref.at[k],
                                         send_sem=send_sems.at[base + k], recv_sem=recv_sems.at[base + k],
                                         device_id=peers[k], device_id_type=MESH) for k in range(3)]


def _ride_io(rides):
    shapes, nsem = [], 0
    for kind, a in rides:
        shapes.append(jax.ShapeDtypeStruct(((4, a.shape[1] // 2) if kind == "sib" else (3, a.shape[1])) + a.shape[2:], a.dtype))
        nsem += 1 if kind == "sib" else 3
    sems = [pltpu.SemaphoreType.DMA((nsem,)), pltpu.SemaphoreType.DMA((nsem,))] if rides else []
    return [_HBM] * len(rides), [_HBM] * len(rides), shapes, sems


def _ride_kernel(body, n_in, n_out, rides, nhp, nI):
    nr = len(rides)

    def kern(*refs):
        ins, r_in = refs[:n_in], refs[n_in:n_in + nr]
        outs, r_out = refs[n_in + nr:n_in + nr + n_out], refs[n_in + nr + n_out:n_in + 2 * nr + n_out]
        sems = refs[n_in + 2 * nr + n_out:]
        if nr:
            hp, I = pl.program_id(0), pl.program_id(1)

            def copies():
                cps = []
                for (kind, _), s, d in zip(rides, r_in, r_out):
                    cps += _ride_copies(kind, s, d, sems[0], sems[1], len(cps))
                return cps

            @pl.when((hp == 0) & (I == 0))
            def _():
                for cp in copies():
                    cp.start()

            @pl.when((hp == nhp - 1) & (I == nI - 1))
            def _():
                for cp in copies():
                    cp.wait()

        body(*ins, *outs)

    return kern


def gather_weight(w_shard, *, name):
    nl = w_shard.shape[0]
    hl = nl // 2

    def body(w_ref, out_ref, send_sems, recv_sems):
        x, y, c, s0, peers, sibling = _place()

        def piece(s, half):
            return out_ref.at[s, pl.ds(half * hl, hl)]

        def copy(k, s, half, to, src=None):
            return pltpu.make_async_remote_copy(src_ref=piece(s, half) if src is None else src, dst_ref=piece(s, half),
                                                send_sem=send_sems.at[k], recv_sem=recv_sems.at[k],
                                                device_id=to, device_id_type=MESH)

        first = [copy(k, s0, c, peers[k], src=w_ref.at[pl.ds(c * hl, hl)]) for k in range(3)]
        for cp in first:
            cp.start()
        passed = [copy(3 + k, s0 ^ PEER_XOR[k], c, sibling) for k in range(3)]
        for k in range(3):
            copy(k, s0 ^ PEER_XOR[k], c, peers[k]).wait_recv()
            passed[k].start()
        for k in range(3):
            copy(3 + k, s0 ^ PEER_XOR[k], 1 - c, sibling).wait_recv()
        for cp in first + passed:
            cp.wait_send()

    return pl.pallas_call(
        body, name=name, in_specs=[_HBM], out_specs=_HBM,
        out_shape=jax.ShapeDtypeStruct((4,) + w_shard.shape, w_shard.dtype),
        scratch_shapes=[pltpu.SemaphoreType.DMA((6,)), pltpu.SemaphoreType.DMA((6,))],
    )(w_shard)


def send_half_to_sibling(G, *, name):
    hl = G.shape[1] // 2

    def body(g_ref, out_ref, send_sem, recv_sem):
        x, y, c, s0, peers, sibling = _place()
        cp = pltpu.make_async_remote_copy(src_ref=g_ref.at[:, pl.ds((1 - c) * hl, hl)], dst_ref=out_ref,
                                          send_sem=send_sem, recv_sem=recv_sem, device_id=sibling, device_id_type=MESH)
        cp.start()
        cp.wait()

    return pl.pallas_call(
        body, name=name, in_specs=[_HBM], out_specs=_HBM,
        out_shape=jax.ShapeDtypeStruct((4, hl) + G.shape[2:], G.dtype),
        scratch_shapes=[pltpu.SemaphoreType.DMA, pltpu.SemaphoreType.DMA],
    )(G)


def scatter_to_chips(A, *, name):
    def body(a_ref, out_ref, send_sems, recv_sems):
        x, y, c, s0, peers, sibling = _place()
        copies = [pltpu.make_async_remote_copy(src_ref=a_ref.at[s0 ^ PEER_XOR[k]], dst_ref=out_ref.at[k],
                                               send_sem=send_sems.at[k], recv_sem=recv_sems.at[k],
                                               device_id=peers[k], device_id_type=MESH) for k in range(3)]
        for cp in copies:
            cp.start()
        for cp in copies:
            cp.wait()

    return pl.pallas_call(
        body, name=name, in_specs=[_HBM], out_specs=_HBM,
        out_shape=jax.ShapeDtypeStruct((3,) + A.shape[1:], A.dtype),
        scratch_shapes=[pltpu.SemaphoreType.DMA((3,)), pltpu.SemaphoreType.DMA((3,))],
    )(A)


def join_halves(buf, a, hl, *, name):
    def body(b_ref, out_ref, send_sem, recv_sem):
        x, y, c, s0, peers, sibling = _place()
        mine, other = pl.ds(a + c * hl, hl), pl.ds(a + (1 - c) * hl, hl)
        cp = pltpu.make_async_remote_copy(src_ref=b_ref.at[mine], dst_ref=out_ref.at[mine],
                                          send_sem=send_sem, recv_sem=recv_sem, device_id=sibling, device_id_type=MESH)
        cp.start()
        pltpu.make_async_remote_copy(src_ref=b_ref.at[other], dst_ref=out_ref.at[other],
                                     send_sem=send_sem, recv_sem=recv_sem, device_id=sibling, device_id_type=MESH).wait_recv()
        cp.wait_send()

    return pl.pallas_call(
        body, name=name, in_specs=[_HBM], out_specs=_HBM, input_output_aliases={0: 0},
        out_shape=jax.ShapeDtypeStruct(buf.shape, buf.dtype),
        scratch_shapes=[pltpu.SemaphoreType.DMA, pltpu.SemaphoreType.DMA],
    )(buf)


def allgather8(xs, *, name, reduce):
    M, N = xs.shape

    def body(x_ref, out_ref, *rest):
        if reduce:
            all_ref, send_sems, recv_sems, local_sem = rest
        else:
            all_ref = out_ref
            send_sems, recv_sems, local_sem = rest
        x, y, c, s0, peers, sibling = _place()
        me = (x, y, c)
        chips = [(1 - x, y), (x, 1 - y), (1 - x, 1 - y)]

        def rows(px, py, pc):
            return all_ref.at[pl.ds((4 * px + 2 * py + pc) * M, M), :]

        def copy(k, block, to, src=None):
            return pltpu.make_async_remote_copy(src_ref=rows(*block) if src is None else src, dst_ref=rows(*block),
                                                send_sem=send_sems.at[k], recv_sem=recv_sems.at[k],
                                                device_id=to, device_id_type=MESH)

        mine = pltpu.make_async_copy(x_ref, rows(*me), local_sem)
        mine.start()
        first = [copy(0, me, sibling, src=x_ref)]
        first += [copy(1 + j, me, (*chip, c), src=x_ref) for j, chip in enumerate(chips)]
        for cp in first:
            cp.start()
        passed = [copy(4 + j, (*chip, c), sibling) for j, chip in enumerate(chips)]
        for j, chip in enumerate(chips):
            copy(1 + j, (*chip, c), me).wait_recv()
            passed[j].start()
        copy(0, sibling, me).wait_recv()
        for j, chip in enumerate(chips):
            copy(4 + j, (*chip, 1 - c), me).wait_recv()
        for cp in first + passed:
            cp.wait_send()
        mine.wait()
        if reduce:
            acc = all_ref[pl.ds(0, M), :]
            for d in range(1, 8):
                acc = acc + all_ref[pl.ds(d * M, M), :]
            out_ref[...] = acc

    vm = pl.BlockSpec(memory_space=pltpu.VMEM)
    scratch = [pltpu.SemaphoreType.DMA((7,)), pltpu.SemaphoreType.DMA((7,)), pltpu.SemaphoreType.DMA]
    if reduce:
        scratch = [pltpu.VMEM((8 * M, N), xs.dtype)] + scratch
    return pl.pallas_call(
        body, name=name, in_specs=[vm], out_specs=vm,
        out_shape=jax.ShapeDtypeStruct((M if reduce else 8 * M, N), xs.dtype), scratch_shapes=scratch,
    )(xs)


def add_halves(G, B, c, *, name):
    S, nl, R, C = G.shape
    hl = nl // 2
    tr = _pick(R, (512, 384, 256, 128))

    def body(c_ref, g_ref, b_ref, o_ref):
        o_ref[...] = (g_ref[...] + b_ref[...]).astype(BF16)

    blk = (1, 1, tr, C)
    return pl.pallas_call(
        body, name=name,
        grid_spec=pltpu.PrefetchScalarGridSpec(
            num_scalar_prefetch=1, grid=(S, hl, R // tr),
            in_specs=[pl.BlockSpec(blk, lambda s, l, r, cr: (s, cr[0] * hl + l, r, 0)),
                      pl.BlockSpec(blk, lambda s, l, r, cr: (s, l, r, 0))],
            out_specs=pl.BlockSpec(blk, lambda s, l, r, cr: (s, l, r, 0))),
        out_shape=jax.ShapeDtypeStruct((S, hl, R, C), BF16),
    )(jnp.reshape(c, (1,)).astype(jnp.int32), G, B)


def add_chips(G, B, Bc, c, s0, acc, a, *, name):
    S, nl, R, C = G.shape
    hl = nl // 2
    tr = _pick(R, (512, 384, 256, 128))

    def body(pc_ref, ps_ref, g_ref, b_ref, c0_ref, c1_ref, c2_ref, acc_ref, o_ref):
        o_ref[...] = ((((g_ref[0] + b_ref[0]) + c0_ref[0].astype(F32)) + c1_ref[0].astype(F32)) + c2_ref[0].astype(F32))

    blk = (1, 1, tr, C)
    cspec = lambda k: pl.BlockSpec(blk, lambda l, r, pc, ps: (k, l, r, 0))
    return pl.pallas_call(
        body, name=name,
        grid_spec=pltpu.PrefetchScalarGridSpec(
            num_scalar_prefetch=2, grid=(hl, R // tr),
            in_specs=[pl.BlockSpec(blk, lambda l, r, pc, ps: (ps[0], pc[0] * hl + l, r, 0)),
                      pl.BlockSpec(blk, lambda l, r, pc, ps: (ps[0], l, r, 0)), cspec(0), cspec(1), cspec(2),
                      pl.BlockSpec(memory_space=pl.ANY)],
            out_specs=pl.BlockSpec((1, tr, C), lambda l, r, pc, ps: (a + pc[0] * hl + l, r, 0))),
        out_shape=jax.ShapeDtypeStruct(acc.shape, F32), input_output_aliases={7: 0},
    )(jnp.reshape(c, (1,)).astype(jnp.int32), jnp.reshape(s0, (1,)).astype(jnp.int32), G, B, Bc, Bc, Bc, acc)


def reduce_scatter_finish(G, B, Bc, c, s0, acc, a, *, tag):
    acc = add_chips(G, B, Bc, c, s0, acc, a, name=f"rs_add2_{tag}")
    return join_halves(acc, a, G.shape[1] // 2, name=f"rs_join_{tag}")


def reduce_scatter_weight(G, c, s0, acc, a, *, tag):
    B = send_half_to_sibling(G, name=f"rs_sib_{tag}")
    A = add_halves(G, B, c, name=f"rs_add1_{tag}")
    Bc = scatter_to_chips(A, name=f"rs_chips_{tag}")
    return reduce_scatter_finish(G, B, Bc, c, s0, acc, a, tag=tag)


def adamw(w, g, m, v, *, name):
    shp = w.shape
    if len(shp) == 2:
        w, g, m, v = (a[None] for a in (w, g, m, v))
    nl, R, C = w.shape
    tr = R
    for t in (512, 384, 256, 128):
        if R % t == 0:
            tr = t
            break

    def body(w_ref, g_ref, m_ref, v_ref, d_ref, nm_ref, nv_ref):
        gv = g_ref[...]
        mn = ADAM_B1 * m_ref[...] + (1.0 - ADAM_B1) * gv
        vn = ADAM_B2 * v_ref[...] + (1.0 - ADAM_B2) * jnp.square(gv)
        m_hat = mn / (1.0 - ADAM_B1 ** ADAM_STEP)
        v_hat = vn / (1.0 - ADAM_B2 ** ADAM_STEP)
        d_ref[...] = -ADAM_LR * (m_hat / (jnp.sqrt(v_hat) + ADAM_EPS) + ADAM_WD * w_ref[...])
        nm_ref[...] = mn
        nv_ref[...] = vn

    spec = pl.BlockSpec((1, tr, C), lambda l, i: (l, i, 0))
    sds = jax.ShapeDtypeStruct((nl, R, C), F32)
    d, nm, nv = pl.pallas_call(body, name=name, grid=(nl, R // tr), in_specs=[spec] * 4, out_specs=(spec,) * 3,
                               out_shape=(sds,) * 3)(w, g, m, v)
    return d.reshape(shp), nm.reshape(shp), nv.reshape(shp)


_SMALL = ("ln_emb_g", "ln_emb_b", "q_norm", "kv_norm", "ln1_g", "ln1_b", "ln2_g", "ln2_b", "meta")


def _pack_small(d):
    flat = jnp.concatenate([d[k].reshape(-1) for k in _SMALL])
    rows = -(-flat.shape[0] // 128)
    rows = -(-rows // 8) * 8
    flat = jnp.concatenate([flat, jnp.zeros((rows * 128 - flat.shape[0],), F32)])
    return flat.reshape(rows, 128)


def _unpack_small(p, shapes):
    flat = p.reshape(-1)
    out, o = {}, 0
    for k in _SMALL:
        n = int(np.prod(shapes[k]))
        out[k] = flat[o:o + n].reshape(shapes[k])
        o += n
    return out


def kernel(x, meta_tokens, ln_emb_g, ln_emb_b, w_in, mla_q_norm, mla_kv_norm, w_uq, w_ukv, w_out, ln1_g, ln1_b, w_ff1, w_ff2, ln2_g, ln2_b, loss_target, m_meta_tokens, m_ln_emb_g, m_ln_emb_b, m_w_in, m_mla_q_norm, m_mla_kv_norm, m_w_uq, m_w_ukv, m_w_out, m_ln1_g, m_ln1_b, m_w_ff1, m_w_ff2, m_ln2_g, m_ln2_b, v_meta_tokens, v_ln_emb_g, v_ln_emb_b, v_w_in, v_mla_q_norm, v_mla_kv_norm, v_w_uq, v_w_ukv, v_w_out, v_ln1_g, v_ln1_b, v_w_ff1, v_w_ff2, v_ln2_g, v_ln2_b):
    xi, yi, ci = lax.axis_index("x"), lax.axis_index("y"), lax.axis_index("c")
    s0 = 2 * xi + yi
    nl = w_in.shape[0]

    big = {"w_in": w_in, "w_uq": w_uq, "w_ukv": w_ukv}
    late = {"w_out": w_out.astype(BF16), "w_ff1": w_ff1.astype(BF16), "w_ff2": w_ff2.astype(BF16), "s0": s0, "c": ci}
    full = {}
    for k, v in big.items():
        vb = v.astype(BF16)
        full[k] = lax.dynamic_update_slice(gather_weight(vb, name=f"ag_{k}"), vb[None], (s0, 0, 0, 0))
    cols = lambda a: jnp.moveaxis(a, 0, 2).reshape(a.shape[1], a.shape[2], 4 * a.shape[3])
    k_w_in = _perm_w_in(cols(full["w_in"]))[:, None]
    k_w_uq = _perm_w_uq(cols(full["w_uq"]))[:, None]
    k_w_ukv = _perm_w_ukv(cols(full["w_ukv"]))[:, None]
    meta_all = allgather8(meta_tokens, name="ag_meta", reduce=False)
    meta_full = jnp.concatenate([meta_all[32 * s:32 * s + N_META] for s in range(4)], axis=1)

    loss_part, grad_x, g = _device_step(x[0], loss_target[0], meta_full, ln_emb_g, ln_emb_b, k_w_in, mla_q_norm, mla_kv_norm,
                                        k_w_uq, k_w_ukv, None, ln1_g, ln1_b, None, None, ln2_g, ln2_b, late=late)
    loss = lax.psum(loss_part, ("x", "y", "c"))

    gw = {k: g[k] for k in _RS_SHAPES}

    small_shapes = {"ln_emb_g": (D_MODEL,), "ln_emb_b": (D_MODEL,), "q_norm": (nl, MLA_Q_LORA), "kv_norm": (nl, MLA_KV_LORA),
                    "ln1_g": (nl, D_MODEL), "ln1_b": (nl, D_MODEL), "ln2_g": (nl, D_MODEL), "ln2_b": (nl, D_MODEL),
                    "meta": (N_META, D_MODEL)}
    gs = _unpack_small(allgather8(_pack_small(g), name="ar_small", reduce=True), small_shapes)
    gw.update({"ln_emb_g": gs["ln_emb_g"], "ln_emb_b": gs["ln_emb_b"], "mla_q_norm": gs["q_norm"], "mla_kv_norm": gs["kv_norm"],
               "ln1_g": gs["ln1_g"], "ln1_b": gs["ln1_b"], "ln2_g": gs["ln2_g"], "ln2_b": gs["ln2_b"],
               "meta_tokens": lax.dynamic_slice_in_dim(gs["meta"], s0 * 256, 256, axis=1)})

    names = ["meta_tokens", "ln_emb_g", "ln_emb_b", "w_in", "mla_q_norm", "mla_kv_norm", "w_uq", "w_ukv", "w_out",
             "ln1_g", "ln1_b", "w_ff1", "w_ff2", "ln2_g", "ln2_b"]
    ws = [meta_tokens, ln_emb_g, ln_emb_b, w_in, mla_q_norm, mla_kv_norm, w_uq, w_ukv, w_out, ln1_g, ln1_b, w_ff1, w_ff2, ln2_g, ln2_b]
    ms = [m_meta_tokens, m_ln_emb_g, m_ln_emb_b, m_w_in, m_mla_q_norm, m_mla_kv_norm, m_w_uq, m_w_ukv, m_w_out, m_ln1_g, m_ln1_b, m_w_ff1, m_w_ff2, m_ln2_g, m_ln2_b]
    vs = [v_meta_tokens, v_ln_emb_g, v_ln_emb_b, v_w_in, v_mla_q_norm, v_mla_kv_norm, v_w_uq, v_w_ukv, v_w_out, v_ln1_g, v_ln1_b, v_w_ff1, v_w_ff2, v_ln2_g, v_ln2_b]
    deltas, new_m, new_v = [], [], []
    for n, w, m, v in zip(names, ws, ms, vs):
        w2 = w.reshape(1, -1) if w.ndim == 1 else w
        d, nm, nv = adamw(w2, gw[n].reshape(w2.shape), m.reshape(w2.shape), v.reshape(w2.shape), name=f"adamw_{n}")
        deltas.append(d.reshape(w.shape))
        new_m.append(nm.reshape(w.shape))
        new_v.append(nv.reshape(w.shape))
    grads_out = [gw[n].reshape(w.shape) for n, w in zip(names, ws)]
    return (loss, grad_x[None], *grads_out, *deltas, *new_m, *new_v)
```

```python
import functools
import math

import numpy as np
import jax
import jax.numpy as jnp
from jax import lax
from jax.experimental import pallas as pl
from jax.experimental.pallas import tpu as pltpu

F32 = jnp.float32
BF16 = jnp.bfloat16

D_MODEL = 1024
DEPTH = 4
N_META = 16
BLK = 128
N_PAD = 112
SB_HEADS = 8
MLA_HEADS = 8
MLA_NOPE = 64
MLA_ROPE = 32
MLA_V = 64
MLA_Q_LORA = 384
MLA_KV_LORA = 256
RET_HEADS = 4
RET_QK = 64
RET_V = 128
D_FF = 4 * D_MODEL
ROPE_THETA = 10000.0
LN_EPS = 1e-5
DN_ALPHA = (2 * DEPTH) ** 0.25
RET_GAMMA = tuple(1.0 - 2.0 ** (-5 - h) for h in range(RET_HEADS))
RET_LOG_G = tuple(float(np.log(np.float32(g))) for g in RET_GAMMA)
MLA_SCALE = (MLA_NOPE + MLA_ROPE) ** -0.5

ADAM_LR = 0.001
ADAM_B1 = 0.9
ADAM_B2 = 0.999
ADAM_EPS = 1e-08
ADAM_WD = 0.01
ADAM_STEP = 10

N_SB = 1536
C_SBQ, C_SBK, C_SBV = 0, 512, 1024
C_RQ, C_RK, C_RV, C_RG = 0, 256, 512, 1024
C_CQ, C_CKV, C_KR = 1536, 1920, 2176
N_IN = 3744
N_INP = 3840

NEG = -1e30


def _pick(n, cands):
    for t in cands:
        if n % t == 0:
            return t
    raise ValueError(f"no tile for {n} in {cands}")


def _row_tile(n):
    return _pick(n, (1056, 1024, 528, 512, 384, 256, 128))


def _dot(a, b):
    return jnp.dot(a, b, preferred_element_type=F32)


def _dot_nt(a, b):
    return lax.dot_general(a, b, (((1,), (1,)), ((), ())), preferred_element_type=F32)


def _dot_tn(a, b):
    return lax.dot_general(a, b, (((0,), (0,)), ((), ())), preferred_element_type=F32)


def mm_nn(a, b, *, tn, name, tk=None, prologue=None, axpy=None, out_dtype=F32):
    M, K = a.shape
    S, _, Ns = b.shape
    tm = _row_tile(M)
    tk = K if tk is None else tk
    npt = Ns // tn
    nk = K // tk
    alpha = None if axpy is None else axpy[1]

    def body(*refs):
        if axpy is None:
            a_ref, b_ref, o_ref, acc = refs
        else:
            a_ref, b_ref, e_ref, o_ref, acc = refs
        k = pl.program_id(2)

        @pl.when(k == 0)
        def _():
            acc[...] = jnp.zeros_like(acc)

        x = a_ref[...]
        if prologue == "relu2":
            x = jnp.square(jnp.maximum(x, 0.0))
        acc[...] += _dot(x.astype(BF16), b_ref[0])

        @pl.when(k == nk - 1)
        def _():
            r = acc[...]
            if axpy is not None:
                r = r + alpha * e_ref[...]
            o_ref[...] = r.astype(out_dtype)

    in_specs = [pl.BlockSpec((tm, tk), lambda i, j, k: (i, k)),
                pl.BlockSpec((1, tk, tn), lambda i, j, k: (j // npt, k, j % npt))]
    args = [a, b]
    if axpy is not None:
        in_specs.append(pl.BlockSpec((tm, tn), lambda i, j, k: (i, j)))
        args.append(axpy[0])
    return pl.pallas_call(
        body, name=name, grid=(M // tm, (S * Ns) // tn, nk), in_specs=in_specs,
        out_specs=pl.BlockSpec((tm, tn), lambda i, j, k: (i, j)),
        out_shape=jax.ShapeDtypeStruct((M, S * Ns), out_dtype),
        scratch_shapes=[pltpu.VMEM((tm, tn), F32)],
        compiler_params=pltpu.CompilerParams(dimension_semantics=("parallel", "parallel", "arbitrary")),
    )(*args)


def mm_nt(a, b, *, tn, tko, name, axpy=None, relu2grad=None, out_dtype=F32):
    M, N = a.shape
    S, K, Ns = b.shape
    tm = _row_tile(M)
    npt = Ns // tn
    nn = N // tn
    alpha = None if axpy is None else axpy[1]

    def body(*refs):
        if axpy is None and relu2grad is None:
            a_ref, b_ref, o_ref, acc = refs
        else:
            a_ref, b_ref, e_ref, o_ref, acc = refs
        n = pl.program_id(2)

        @pl.when(n == 0)
        def _():
            acc[...] = jnp.zeros_like(acc)

        acc[...] += _dot_nt(a_ref[...].astype(BF16), b_ref[0])

        @pl.when(n == nn - 1)
        def _():
            r = acc[...]
            if axpy is not None:
                r = r + alpha * e_ref[...]
            if relu2grad is not None:
                r = r * (2.0 * jnp.maximum(e_ref[...], 0.0))
            o_ref[...] = r.astype(out_dtype)

    in_specs = [pl.BlockSpec((tm, tn), lambda i, j, n: (i, n)),
                pl.BlockSpec((1, tko, tn), lambda i, j, n: (n // npt, j, n % npt))]
    args = [a, b]
    extra = axpy[0] if axpy is not None else relu2grad
    if extra is not None:
        in_specs.append(pl.BlockSpec((tm, tko), lambda i, j, n: (i, j)))
        args.append(extra)
    return pl.pallas_call(
        body, name=name, grid=(M // tm, K // tko, nn), in_specs=in_specs,
        out_specs=pl.BlockSpec((tm, tko), lambda i, j, n: (i, j)),
        out_shape=jax.ShapeDtypeStruct((M, K), out_dtype),
        scratch_shapes=[pltpu.VMEM((tm, tko), F32)],
        compiler_params=pltpu.CompilerParams(dimension_semantics=("parallel", "parallel", "arbitrary")),
    )(*args)


def mm_tn(a, g, *, shards, tko, tn, name, prologue=None, into=None):
    M, K = a.shape
    _, N = g.shape
    Ns = N // shards
    tm = _row_tile(M)
    npt = Ns // tn
    nm = M // tm

    def body(*refs):
        if into is None:
            a_ref, g_ref, o_ref, acc = refs
        else:
            a_ref, g_ref, _, o_ref, acc = refs
        m = pl.program_id(2)

        @pl.when(m == 0)
        def _():
            acc[...] = jnp.zeros_like(acc)

        x = a_ref[...]
        if prologue == "relu2":
            x = jnp.square(jnp.maximum(x, 0.0))
        acc[...] += _dot_tn(x.astype(BF16), g_ref[...].astype(BF16))

        @pl.when(m == nm - 1)
        def _():
            if into is None or into[2] == "layer":
                o_ref[0] = acc[...]
            else:
                o_ref[0, 0] = acc[...]

    in_specs = [pl.BlockSpec((tm, tko), lambda i, j, m: (m, i)),
                pl.BlockSpec((tm, tn), lambda i, j, m: (m, j))]
    scratch = [pltpu.VMEM((tko, tn), F32)]
    params = pltpu.CompilerParams(dimension_semantics=("parallel", "parallel", "arbitrary"))
    if into is None:
        return pl.pallas_call(
            body, name=name, grid=(K // tko, N // tn, nm), in_specs=in_specs,
            out_specs=pl.BlockSpec((1, tko, tn), lambda i, j, m: (j // npt, i, j % npt)),
            out_shape=jax.ShapeDtypeStruct((shards, K, Ns), F32), scratch_shapes=scratch, compiler_params=params,
        )(a, g)
    buf, layer, how = into
    if how == "layer":
        out_spec = pl.BlockSpec((1, tko, tn), lambda i, j, m: (layer, i, j))
    elif how == "cols":
        npt4 = (N // 4) // tn
        out_spec = pl.BlockSpec((1, 1, tko, tn), lambda i, j, m: (j // npt4, layer, i, j % npt4))
    else:
        kpt4 = (K // 4) // tko
        out_spec = pl.BlockSpec((1, 1, tko, tn), lambda i, j, m: (i // kpt4, layer, i % kpt4, j))
    return pl.pallas_call(
        body, name=name, grid=(K // tko, N // tn, nm), in_specs=in_specs + [pl.BlockSpec(memory_space=pl.ANY)],
        out_specs=out_spec, out_shape=jax.ShapeDtypeStruct(buf.shape, F32), input_output_aliases={2: 0},
        scratch_shapes=scratch, compiler_params=params,
    )(a, g, buf)


def _ln_stats(z):
    mu = jnp.mean(z, axis=-1, keepdims=True)
    zc = z - mu
    var = jnp.mean(jnp.square(zc), axis=-1, keepdims=True)
    r = lax.rsqrt(var + LN_EPS)
    return zc * r, r


def ln_fwd(x, g, b, *, name, res=None):
    L, Dm = x.shape
    tr = _row_tile(L)
    g2, b2 = g.reshape(1, Dm), b.reshape(1, Dm)

    def body(*refs):
        if res is None:
            x_ref, g_ref, b_ref, y_ref, yb_ref = refs
            z = x_ref[...]
        else:
            x_ref, r_ref, g_ref, b_ref, y_ref, yb_ref, z_ref = refs
            z = DN_ALPHA * r_ref[...] + x_ref[...]
            z_ref[...] = z
        xh, _ = _ln_stats(z)
        y = xh * g_ref[...] + b_ref[...]
        y_ref[...] = y
        yb_ref[...] = y.astype(BF16)

    row = pl.BlockSpec((tr, Dm), lambda i: (i, 0))
    vec = pl.BlockSpec((1, Dm), lambda i: (0, 0))
    sds = jax.ShapeDtypeStruct((L, Dm), F32)
    sdb = jax.ShapeDtypeStruct((L, Dm), BF16)
    if res is None:
        y, yb = pl.pallas_call(body, name=name, grid=(L // tr,), in_specs=[row, vec, vec], out_specs=(row, row),
                               out_shape=(sds, sdb))(x, g2, b2)
        return y, yb, x
    return pl.pallas_call(body, name=name, grid=(L // tr,), in_specs=[row, row, vec, vec], out_specs=(row, row, row),
                          out_shape=(sds, sdb, sds))(x, res, g2, b2)


def ln_bwd(dy, z, g, *, name):
    L, Dm = z.shape
    tr = _row_tile(L)

    def body(dy_ref, z_ref, g_ref, dz_ref, dg_ref, db_ref):
        @pl.when(pl.program_id(0) == 0)
        def _():
            dg_ref[...] = jnp.zeros_like(dg_ref)
            db_ref[...] = jnp.zeros_like(db_ref)

        dyv = dy_ref[...]
        xh, r = _ln_stats(z_ref[...])
        dxh = dyv * g_ref[...]
        m1 = jnp.mean(dxh, axis=-1, keepdims=True)
        m2 = jnp.mean(dxh * xh, axis=-1, keepdims=True)
        dz_ref[...] = r * (dxh - m1 - xh * m2)
        dg_ref[...] += jnp.sum(dyv * xh, axis=0, keepdims=True)
        db_ref[...] += jnp.sum(dyv, axis=0, keepdims=True)

    row = pl.BlockSpec((tr, Dm), lambda i: (i, 0))
    vec = pl.BlockSpec((1, Dm), lambda i: (0, 0))
    return pl.pallas_call(
        body, name=name, grid=(L // tr,), in_specs=[row, row, vec], out_specs=(row, vec, vec),
        out_shape=(jax.ShapeDtypeStruct((L, Dm), F32), jax.ShapeDtypeStruct((1, Dm), F32), jax.ShapeDtypeStruct((1, Dm), F32)),
        compiler_params=pltpu.CompilerParams(dimension_semantics=("arbitrary",)),
    )(dy, z, g.reshape(1, Dm))


def loss_fwd_bwd(h, target, *, name):
    L, Dm = h.shape
    nb = L // BLK

    def body(h_ref, t_ref, l_ref, dh_ref):
        i = pl.program_id(0)

        @pl.when(i == 0)
        def _():
            l_ref[...] = jnp.zeros_like(l_ref)
            dh_ref[...] = jnp.zeros_like(dh_ref)

        @pl.when(i > 0)
        def _():
            e = h_ref[...] - t_ref[...]
            dh_ref[...] = e * (1.0 / Dm)
            part = jnp.sum(jnp.sum(jnp.square(e), axis=-1, keepdims=True) * (1.0 / Dm), axis=0, keepdims=True)
            l_ref[...] += 0.5 * part

    return pl.pallas_call(
        body, name=name, grid=(nb,),
        in_specs=[pl.BlockSpec((BLK, Dm), lambda i: (i, 0)),
                  pl.BlockSpec((BLK, Dm), lambda i: (jnp.maximum(i - 1, 0), 0))],
        out_specs=(pl.BlockSpec((8, 128), lambda i: (0, 0)), pl.BlockSpec((BLK, Dm), lambda i: (i, 0))),
        out_shape=(jax.ShapeDtypeStruct((8, 128), F32), jax.ShapeDtypeStruct((L, Dm), F32)),
        compiler_params=pltpu.CompilerParams(dimension_semantics=("arbitrary",)),
    )(h, target)


def _swap_half(x, half):
    ax = x.ndim - 1
    n = x.shape[ax]
    lane = lax.broadcasted_iota(jnp.int32, x.shape, ax)
    up = pltpu.roll(x, n - half, ax)
    dn = pltpu.roll(x, half, ax)
    return jnp.where((lane % (2 * half)) < half, up, dn)


def _rope(x, cs, sn, half):
    return x * cs + _swap_half(x, half) * sn


def _rope_t(dy, cs, sn, half):
    return dy * cs + _swap_half(dy * sn, half)


def _rms(x):
    r = lax.rsqrt(jnp.mean(jnp.square(x), axis=-1, keepdims=True) + LN_EPS)
    return x * r, r


def mla_pre_fwd(P, gq, gkv, cs, sn, *, name):
    L = P.shape[0]
    tr = _row_tile(L)

    def body(p_ref, gq_ref, gkv_ref, cs_ref, sn_ref, nq_ref, nkv_ref, kr_ref):
        cq = p_ref[:, 0:MLA_Q_LORA]
        ckv = p_ref[:, MLA_Q_LORA:MLA_Q_LORA + MLA_KV_LORA]
        kr = p_ref[:, 640:768]
        nq_ref[...] = (_rms(cq)[0] * gq_ref[...]).astype(BF16)
        nkv_ref[...] = (_rms(ckv)[0] * gkv_ref[...]).astype(BF16)
        krr = _rope(kr, cs_ref[...], sn_ref[...], MLA_ROPE // 2)
        kr_ref[...] = (krr + pltpu.roll(krr, MLA_ROPE, 1)).astype(BF16)

    return pl.pallas_call(
        body, name=name, grid=(L // tr,),
        in_specs=[pl.BlockSpec((tr, 768), lambda i: (i, C_CQ // 768)),
                  pl.BlockSpec((1, MLA_Q_LORA), lambda i: (0, 0)), pl.BlockSpec((1, MLA_KV_LORA), lambda i: (0, 0)),
                  pl.BlockSpec((tr, 128), lambda i: (i, 0)), pl.BlockSpec((tr, 128), lambda i: (i, 0))],
        out_specs=(pl.BlockSpec((tr, MLA_Q_LORA), lambda i: (i, 0)), pl.BlockSpec((tr, MLA_KV_LORA), lambda i: (i, 0)),
                   pl.BlockSpec((tr, 128), lambda i: (i, 0))),
        out_shape=(jax.ShapeDtypeStruct((L, MLA_Q_LORA), BF16), jax.ShapeDtypeStruct((L, MLA_KV_LORA), BF16),
                   jax.ShapeDtypeStruct((L, 128), BF16)),
    )(P, gq.reshape(1, -1), gkv.reshape(1, -1), cs, sn)


def mla_pre_bwd(P, dnq, dnkv, dkr, gq, gkv, cs, sn, *, name):
    L = P.shape[0]
    tr = _row_tile(L)

    def body(p_ref, dnq_ref, dnkv_ref, dkr_ref, gq_ref, gkv_ref, cs_ref, sn_ref, dp_ref, dgq_ref, dgkv_ref):
        @pl.when(pl.program_id(0) == 0)
        def _():
            dgq_ref[...] = jnp.zeros_like(dgq_ref)
            dgkv_ref[...] = jnp.zeros_like(dgkv_ref)

        def rms_bwd(x, dy, g_ref, dg_ref):
            xn, r = _rms(x)
            dxn = dy * g_ref[...]
            dg_ref[...] += jnp.sum(dy * xn, axis=0, keepdims=True)
            return r * (dxn - xn * jnp.mean(dxn * xn, axis=-1, keepdims=True))

        dp_ref[:, 0:MLA_Q_LORA] = rms_bwd(p_ref[:, 0:MLA_Q_LORA], dnq_ref[...], gq_ref, dgq_ref)
        dp_ref[:, MLA_Q_LORA:640] = rms_bwd(p_ref[:, MLA_Q_LORA:640], dnkv_ref[...], gkv_ref, dgkv_ref)
        d2 = dkr_ref[...]
        lane = lax.broadcasted_iota(jnp.int32, d2.shape, 1)
        dkr = jnp.where(lane < MLA_ROPE, d2 + pltpu.roll(d2, 128 - MLA_ROPE, 1), 0.0)
        dp_ref[:, 640:768] = _rope_t(dkr, cs_ref[...], sn_ref[...], MLA_ROPE // 2)

    return pl.pallas_call(
        body, name=name, grid=(L // tr,),
        in_specs=[pl.BlockSpec((tr, 768), lambda i: (i, C_CQ // 768)),
                  pl.BlockSpec((tr, MLA_Q_LORA), lambda i: (i, 0)), pl.BlockSpec((tr, MLA_KV_LORA), lambda i: (i, 0)),
                  pl.BlockSpec((tr, 128), lambda i: (i, 0)),
                  pl.BlockSpec((1, MLA_Q_LORA), lambda i: (0, 0)), pl.BlockSpec((1, MLA_KV_LORA), lambda i: (0, 0)),
                  pl.BlockSpec((tr, 128), lambda i: (i, 0)), pl.BlockSpec((tr, 128), lambda i: (i, 0))],
        out_specs=(pl.BlockSpec((tr, 768), lambda i: (i, 0)), pl.BlockSpec((1, MLA_Q_LORA), lambda i: (0, 0)),
                   pl.BlockSpec((1, MLA_KV_LORA), lambda i: (0, 0))),
        out_shape=(jax.ShapeDtypeStruct((L, 768), F32), jax.ShapeDtypeStruct((1, MLA_Q_LORA), F32),
                   jax.ShapeDtypeStruct((1, MLA_KV_LORA), F32)),
        compiler_params=pltpu.CompilerParams(dimension_semantics=("arbitrary",)),
    )(P, dnq, dnkv, dkr, gq.reshape(1, -1), gkv.reshape(1, -1), cs, sn)


def _tri(kind):
    r = lax.broadcasted_iota(jnp.int32, (BLK, BLK), 0)
    c = lax.broadcasted_iota(jnp.int32, (BLK, BLK), 1)
    t = ((r > c) if kind == "right" else (r < c)).astype(BF16)
    return jnp.concatenate([t, t], axis=0)


def _tri_sum(x, tt):
    hi = x.astype(BF16)
    lo = (x - hi.astype(F32)).astype(BF16)
    return _dot(jnp.concatenate([hi, lo], axis=1), tt)


def _sb_tile(q, k, i, j, tt_right, R):
    row = lax.broadcasted_iota(jnp.int32, (BLK, BLK), 0)
    col = lax.broadcasted_iota(jnp.int32, (BLK, BLK), 1)
    s_idx = j * BLK + col
    mask = (s_idx < i * BLK + row) & (s_idx >= N_PAD)
    z = _dot_nt(q, k)
    sp = jnp.maximum(z, 0.0) + jnp.log1p(jnp.exp(-jnp.abs(z)))
    lk = jnp.where(mask, -sp, 0.0)
    E = _tri_sum(lk, tt_right) + R
    return mask, z, sp, lk, E


def _old_sb_fwd(P, *, name):
    L = P.shape[0]
    nb = L // BLK

    def body(q_ref, k_ref, v_ref, o_ref, c_ref):
        i = pl.program_id(1)
        tt = _tri("right")
        lane = lax.broadcasted_iota(jnp.int32, (BLK, 128), 1)
        qs = [(q_ref[:, 64 * h:64 * h + 64] * 0.125).astype(BF16) for h in range(2)]

        def step(jj, carry):
            j = i - jj
            off = pl.multiple_of(j * BLK, BLK)
            kb = k_ref[pl.ds(off, BLK), :].astype(BF16)
            vb = v_ref[pl.ds(off, BLK), :].astype(BF16)
            out = []
            for h in range(2):
                o, R = carry[2 * h], carry[2 * h + 1]
                sl = slice(64 * h, 64 * h + 64)
                mask, z, sp, lk, E = _sb_tile(qs[h], kb[:, sl], i, j, tt, R)
                w = jnp.where(mask, jnp.exp(z - sp + E), 0.0)
                c_ref[h] = jnp.where(lane == j, R, c_ref[h])
                out += [o + _dot(w.astype(BF16), vb[:, sl]), R + jnp.sum(lk, axis=1, keepdims=True)]
            return tuple(out)

        c_ref[...] = jnp.zeros_like(c_ref)
        z0 = (jnp.zeros((BLK, 64), F32), jnp.zeros((BLK, 1), F32))
        res = lax.fori_loop(0, i + 1, step, z0 + z0)
        o_ref[...] = jnp.concatenate([res[0], res[2]], axis=1)

    return pl.pallas_call(
        body, name=name, grid=(SB_HEADS // 2, nb),
        in_specs=[pl.BlockSpec((BLK, 128), lambda hp, i: (i, C_SBQ // 128 + hp)),
                  pl.BlockSpec((L, 128), lambda hp, i: (0, C_SBK // 128 + hp)),
                  pl.BlockSpec((L, 128), lambda hp, i: (0, C_SBV // 128 + hp))],
        out_specs=(pl.BlockSpec((BLK, 128), lambda hp, i: (i, hp)), pl.BlockSpec((2, BLK, 128), lambda hp, i: (hp, i, 0))),
        out_shape=(jax.ShapeDtypeStruct((L, 512), F32), jax.ShapeDtypeStruct((SB_HEADS, L, 128), F32)),
        compiler_params=pltpu.CompilerParams(dimension_semantics=("parallel", "arbitrary")),
    )(P, P, P)


def _old_sb_bwd(P, carries, dmixed, *, name):
    L = P.shape[0]
    nb = L // BLK

    def body(q_ref, k_ref, v_ref, c_ref, do_ref, dq_ref, dk_ref, dv_ref):
        i = pl.program_id(1)

        @pl.when(i == 0)
        def _():
            dk_ref[...] = jnp.zeros_like(dk_ref)
            dv_ref[...] = jnp.zeros_like(dv_ref)

        tr = _tri("right")
        tl = _tri("left")
        lane = lax.broadcasted_iota(jnp.int32, (BLK, 128), 1)
        qs = [(q_ref[:, 64 * h:64 * h + 64] * 0.125).astype(BF16) for h in range(2)]
        dos = [do_ref[:, 64 * h:64 * h + 64].astype(BF16) for h in range(2)]

        def step(j, carry):
            off = pl.multiple_of(j * BLK, BLK)
            kb = k_ref[pl.ds(off, BLK), :].astype(BF16)
            vb = v_ref[pl.ds(off, BLK), :].astype(BF16)
            out, dks, dvs = [], [], []
            for h in range(2):
                dq, PL = carry[2 * h], carry[2 * h + 1]
                sl = slice(64 * h, 64 * h + 64)
                R = jnp.sum(jnp.where(lane == j, c_ref[h], 0.0), axis=1, keepdims=True)
                mask, z, sp, lk, E = _sb_tile(qs[h], kb[:, sl], i, j, tr, R)
                sig = jnp.exp(z - sp)
                w = jnp.where(mask, sig * jnp.exp(E), 0.0)
                dA = _dot_nt(dos[h], vb[:, sl]) * w
                Pp = _tri_sum(dA, tl) + PL
                dz = jnp.where(mask, dA - sig * (dA + Pp), 0.0).astype(BF16)
                dks.append(_dot_tn(dz, qs[h]))
                dvs.append(_dot_tn(w.astype(BF16), dos[h]))
                out += [dq + _dot(dz, kb[:, sl]), PL + jnp.sum(dA, axis=1, keepdims=True)]
            dk_ref[pl.ds(off, BLK), :] += jnp.concatenate(dks, axis=1)
            dv_ref[pl.ds(off, BLK), :] += jnp.concatenate(dvs, axis=1)
            return tuple(out)

        z0 = (jnp.zeros((BLK, 64), F32), jnp.zeros((BLK, 1), F32))
        res = lax.fori_loop(0, i + 1, step, z0 + z0)
        dq_ref[...] = jnp.concatenate([res[0], res[2]], axis=1) * 0.125

    blk = lambda c0: pl.BlockSpec((BLK, 128), lambda hp, i: (i, c0 + hp))
    full = lambda c0: pl.BlockSpec((L, 128), lambda hp, i: (0, c0 + hp))
    sds = jax.ShapeDtypeStruct((L, 512), F32)
    return pl.pallas_call(
        body, name=name, grid=(SB_HEADS // 2, nb),
        in_specs=[blk(C_SBQ // 128), full(C_SBK // 128), full(C_SBV // 128),
                  pl.BlockSpec((2, BLK, 128), lambda hp, i: (hp, i, 0)), blk(0)],
        out_specs=(blk(0), full(0), full(0)), out_shape=(sds, sds, sds),
        compiler_params=pltpu.CompilerParams(dimension_semantics=("parallel", "arbitrary")),
    )(P, P, P, carries, dmixed)


def _mla_mask(i, j):
    row = lax.broadcasted_iota(jnp.int32, (BLK, BLK), 0)
    col = lax.broadcasted_iota(jnp.int32, (BLK, BLK), 1)
    t_idx = i * BLK + row
    s_idx = j * BLK + col
    return (s_idx <= t_idx) & ((s_idx >= N_PAD) | (s_idx == t_idx))


def _mla_q(q_ref, cs_ref, sn_ref):
    qr = _rope(q_ref[:, 128:256], cs_ref[...], sn_ref[...], MLA_ROPE // 2)
    qn = [q_ref[:, 64 * h:64 * h + 64].astype(BF16) for h in range(2)]
    qrs = [qr[:, 32 * h:32 * h + 32].astype(BF16) for h in range(2)]
    return qn, qrs


def _old_mla_fwd(Q, KV, KR, cs, sn, *, name):
    L = Q.shape[0]
    nb = L // BLK

    def body(q_ref, kn_ref, v_ref, kr_ref, cs_ref, sn_ref, o_ref, lse_ref):
        i = pl.program_id(1)
        qn, qrs = _mla_q(q_ref, cs_ref, sn_ref)

        def step(j, carry):
            off = pl.multiple_of(j * BLK, BLK)
            knb = kn_ref[pl.ds(off, BLK), :]
            vb = v_ref[pl.ds(off, BLK), :]
            krb = kr_ref[pl.ds(off, BLK), 0:MLA_ROPE]
            mask = _mla_mask(i, j)
            out = []
            for h in range(2):
                m, l, acc = carry[3 * h], carry[3 * h + 1], carry[3 * h + 2]
                sl = slice(64 * h, 64 * h + 64)
                s = (_dot_nt(qn[h], knb[:, sl]) + _dot_nt(qrs[h], krb)) * MLA_SCALE
                s = jnp.where(mask, s, NEG)
                m_new = jnp.maximum(m, jnp.max(s, axis=1, keepdims=True))
                a = jnp.exp(m - m_new)
                p = jnp.exp(s - m_new)
                out += [m_new, a * l + jnp.sum(p, axis=1, keepdims=True), a * acc + _dot(p.astype(BF16), vb[:, sl])]
            return tuple(out)

        z0 = (jnp.full((BLK, 1), NEG, F32), jnp.zeros((BLK, 1), F32), jnp.zeros((BLK, 64), F32))
        res = lax.fori_loop(0, i + 1, step, z0 + z0)
        o_ref[...] = jnp.concatenate([res[2] / res[1], res[5] / res[4]], axis=1)
        lane = lax.broadcasted_iota(jnp.int32, (BLK, 128), 1)
        lse0 = res[0] + jnp.log(res[1])
        lse1 = res[3] + jnp.log(res[4])
        lse_ref[0] = jnp.where(lane == 0, lse0, jnp.where(lane == 1, lse1, 0.0))

    return pl.pallas_call(
        body, name=name, grid=(MLA_HEADS // 2, nb),
        in_specs=[pl.BlockSpec((BLK, 256), lambda hp, i: (i, hp)),
                  pl.BlockSpec((L, 128), lambda hp, i: (0, hp)),
                  pl.BlockSpec((L, 128), lambda hp, i: (0, 4 + hp)),
                  pl.BlockSpec((L, 128), lambda hp, i: (0, 0)),
                  pl.BlockSpec((BLK, 128), lambda hp, i: (i, 0)), pl.BlockSpec((BLK, 128), lambda hp, i: (i, 0))],
        out_specs=(pl.BlockSpec((BLK, 128), lambda hp, i: (i, hp)), pl.BlockSpec((1, BLK, 128), lambda hp, i: (hp, i, 0))),
        out_shape=(jax.ShapeDtypeStruct((L, 512), F32), jax.ShapeDtypeStruct((4, L, 128), F32)),
        compiler_params=pltpu.CompilerParams(dimension_semantics=("parallel", "arbitrary")),
    )(Q, KV, KV, KR, cs, sn)


def _old_mla_bwd(Q, KV, KR, cs, sn, mixed, dmixed, lse, *, name):
    L = Q.shape[0]
    nb = L // BLK

    def body(q_ref, kn_ref, v_ref, kr_ref, cs_ref, sn_ref, o_ref, do_ref, lse_ref, dq_ref, dkn_ref, dv_ref, dkr_ref):
        hp = pl.program_id(0)
        i = pl.program_id(1)

        @pl.when(i == 0)
        def _():
            dkn_ref[...] = jnp.zeros_like(dkn_ref)
            dv_ref[...] = jnp.zeros_like(dv_ref)

        @pl.when((i == 0) & (hp == 0))
        def _():
            dkr_ref[...] = jnp.zeros_like(dkr_ref)

        qn, qrs = _mla_q(q_ref, cs_ref, sn_ref)
        dos, dd, lses = [], [], []
        for h in range(2):
            sl = slice(64 * h, 64 * h + 64)
            d = do_ref[:, sl]
            dos.append(d.astype(BF16))
            dd.append(jnp.sum(d * o_ref[:, sl], axis=1, keepdims=True))
            lses.append(lse_ref[0, :, h:h + 1])

        def step(j, carry):
            off = pl.multiple_of(j * BLK, BLK)
            knb = kn_ref[pl.ds(off, BLK), :]
            vb = v_ref[pl.ds(off, BLK), :]
            krb = kr_ref[pl.ds(off, BLK), 0:MLA_ROPE]
            mask = _mla_mask(i, j)
            out, dkns, dvs = [], [], []
            dkr = jnp.zeros((BLK, MLA_ROPE), F32)
            for h in range(2):
                dqn, dqr = carry[2 * h], carry[2 * h + 1]
                sl = slice(64 * h, 64 * h + 64)
                s = (_dot_nt(qn[h], knb[:, sl]) + _dot_nt(qrs[h], krb)) * MLA_SCALE
                p = jnp.where(mask, jnp.exp(s - lses[h]), 0.0)
                dp = _dot_nt(dos[h], vb[:, sl])
                ds = (p * (dp - dd[h]) * MLA_SCALE).astype(BF16)
                dkns.append(_dot_tn(ds, qn[h]))
                dvs.append(_dot_tn(p.astype(BF16), dos[h]))
                dkr = dkr + _dot_tn(ds, qrs[h])
                out += [dqn + _dot(ds, knb[:, sl]), dqr + _dot(ds, krb)]
            dkn_ref[pl.ds(off, BLK), :] += jnp.concatenate(dkns, axis=1)
            dv_ref[pl.ds(off, BLK), :] += jnp.concatenate(dvs, axis=1)
            dkr_ref[pl.ds(off, BLK), :] += jnp.concatenate([dkr, jnp.zeros((BLK, 128 - MLA_ROPE), F32)], axis=1)
            return tuple(out)

        z0 = (jnp.zeros((BLK, 64), F32), jnp.zeros((BLK, MLA_ROPE), F32))
        res = lax.fori_loop(0, i + 1, step, z0 + z0)
        dqr = jnp.concatenate([res[1], res[3], jnp.zeros((BLK, 64), F32)], axis=1)
        dq_ref[...] = jnp.concatenate([res[0], res[2], _rope_t(dqr, cs_ref[...], sn_ref[...], MLA_ROPE // 2)], axis=1)

    blk = lambda c0: pl.BlockSpec((BLK, 128), lambda hp, i: (i, c0 + hp))
    full = lambda c0: pl.BlockSpec((L, 128), lambda hp, i: (0, c0 + hp))
    tab = pl.BlockSpec((BLK, 128), lambda hp, i: (i, 0))
    return pl.pallas_call(
        body, name=name, grid=(MLA_HEADS // 2, nb),
        in_specs=[pl.BlockSpec((BLK, 256), lambda hp, i: (i, hp)), full(0), full(4),
                  pl.BlockSpec((L, 128), lambda hp, i: (0, 0)), tab, tab, blk(4), blk(4),
                  pl.BlockSpec((1, BLK, 128), lambda hp, i: (hp, i, 0))],
        out_specs=(pl.BlockSpec((BLK, 256), lambda hp, i: (i, hp)), full(0), full(0),
                   pl.BlockSpec((L, 128), lambda hp, i: (0, 0))),
        out_shape=(jax.ShapeDtypeStruct((L, 1024), F32), jax.ShapeDtypeStruct((L, 512), F32),
                   jax.ShapeDtypeStruct((L, 512), F32), jax.ShapeDtypeStruct((L, 128), F32)),
        compiler_params=pltpu.CompilerParams(dimension_semantics=("arbitrary", "arbitrary")),
    )(Q, KV, KV, KR, cs, sn, mixed, dmixed, lse)


SB_UNROLL = 2


def _tq(L):
    return 384 if L % 384 == 0 else BLK


def _softplus(z):
    na = lax.bitcast_convert_type(lax.bitcast_convert_type(z, jnp.uint32) | jnp.uint32(0x80000000), F32)
    return jnp.maximum(z, 0.0) + jnp.log(1.0 + jnp.exp(na))


def _head_split(x, first):
    zero = jnp.zeros_like(x)
    return jnp.where(first, x, zero), jnp.where(first, zero, x)


def _sb_mask(I, j, tq):
    row = lax.broadcasted_iota(jnp.int32, (tq, BLK), 0)
    col = lax.broadcasted_iota(jnp.int32, (tq, BLK), 1)
    s_idx = j * BLK + col
    return (s_idx < I * tq + row) & (s_idx >= N_PAD)


def _tri2(kind, splits):
    r = lax.broadcasted_iota(jnp.int32, (256, 256), 0)
    c = lax.broadcasted_iota(jnp.int32, (256, 256), 1)
    same = (r < BLK) == (c < BLK)
    t = (same & ((r > c) if kind == "right" else (r < c))).astype(BF16)
    return jnp.concatenate([t] * splits, axis=0)


def _split2(x):
    hi = x.astype(BF16)
    lo = (x - hi.astype(F32)).astype(BF16)
    return jnp.concatenate([hi, lo], axis=1)


def _sb_mask2(I, j, tq):
    row = lax.broadcasted_iota(jnp.int32, (tq, 256), 0)
    col = lax.broadcasted_iota(jnp.int32, (tq, 256), 1)
    s_idx = j * BLK + (col & (BLK - 1))
    return (s_idx < I * tq + row) & (s_idx >= N_PAD)


def _per_head(x, r0, r1):
    return jnp.concatenate([x[:, 0:BLK] + r0, x[:, BLK:2 * BLK] + r1], axis=1)


def sb_fwd(P, *, name, gathers=()):
    L = P.shape[0]
    tq = _tq(L)
    nd = tq // BLK
    ng = len(gathers)
    nhp, nI = SB_HEADS // 2, L // tq

    def body(q_ref, k_ref, v_ref, *rest):
        w_refs, (o_ref, c_ref), g_refs, sems = rest[:ng], rest[ng:ng + 2], rest[ng + 2:2 * ng + 2], rest[2 * ng + 2:]
        hp = pl.program_id(0)
        I = pl.program_id(1)
        _gather_ride(w_refs, g_refs, sems, (hp == 0) & (I == 0), (hp == nhp - 1) & (I == 0), (hp == nhp - 1) & (I == nI - 1))
        tt = _tri2("right", 2)
        lane_q = lax.broadcasted_iota(jnp.int32, (tq, 128), 1)
        first_k = lax.broadcasted_iota(jnp.int32, (BLK, 128), 1) < 64
        q = (q_ref[...] * 0.125).astype(BF16)
        c_ref[...] = jnp.zeros_like(c_ref)

        def tiles(T, carry, kind):
            o, R0, R1 = carry
            js = [T * nd + nd - 1 - u for u in range(nd)]
            st = []
            for j in js:
                off = pl.multiple_of(j * BLK, BLK)
                kcat = jnp.concatenate(_head_split(k_ref[pl.ds(off, BLK), :].astype(BF16), first_k), axis=0)
                st.append([_dot_nt(q, kcat), off])
            for u, (s, j) in enumerate(zip(st, js)):
                sp = _softplus(s[0])
                mask = _sb_mask2(I, j, tq) if kind == "diag" else (pad_ok if kind == "first" and u == nd - 1 else None)
                spm = sp if mask is None else jnp.where(mask, sp, 0.0)
                s += [sp, spm, mask, _dot(_split2(spm), tt)]
            for (z, off, sp, spm, mask, S), j in zip(st, js):
                vcat = jnp.concatenate(_head_split(v_ref[pl.ds(off, BLK), :].astype(BF16), first_k), axis=0)
                w = jnp.exp(_per_head(z - sp - S, R0, R1))
                if mask is not None:
                    w = jnp.where(mask, w, 0.0)
                c_ref[0] = jnp.where(lane_q == j, R0, c_ref[0])
                c_ref[1] = jnp.where(lane_q == j, R1, c_ref[1])
                o = o + _dot(w.astype(BF16), vcat)
                R0 = R0 - (S[:, 0:1] + spm[:, 0:1])
                R1 = R1 - (S[:, BLK:BLK + 1] + spm[:, BLK:BLK + 1])
            return (o, R0, R1)

        pad_ok = (lax.broadcasted_iota(jnp.int32, (tq, 256), 1) & (BLK - 1)) >= N_PAD
        carry = (jnp.zeros((tq, 128), F32), jnp.zeros((tq, 1), F32), jnp.zeros((tq, 1), F32))
        carry = tiles(I, carry, "diag")
        carry = lax.fori_loop(0, jnp.maximum(I - 1, 0), lambda t, c: tiles(I - 1 - t, c, None), carry)
        carry = lax.fori_loop(0, jnp.minimum(I, 1), lambda t, c: tiles(0, c, "first"), carry)
        o_ref[...] = carry[0]

    g_in, g_out, g_shapes, g_sems = _gather_io(gathers)
    return pl.pallas_call(
        body, name=name, grid=(nhp, nI),
        in_specs=[pl.BlockSpec((tq, 128), lambda hp, i: (i, C_SBQ // 128 + hp)),
                  pl.BlockSpec((L, 128), lambda hp, i: (0, C_SBK // 128 + hp)),
                  pl.BlockSpec((L, 128), lambda hp, i: (0, C_SBV // 128 + hp))] + g_in,
        out_specs=[pl.BlockSpec((tq, 128), lambda hp, i: (i, hp)), pl.BlockSpec((2, tq, 128), lambda hp, i: (hp, i, 0))] + g_out,
        out_shape=[jax.ShapeDtypeStruct((L, 1536), F32), jax.ShapeDtypeStruct((SB_HEADS, L, 128), F32)] + g_shapes,
        scratch_shapes=g_sems,
        compiler_params=pltpu.CompilerParams(dimension_semantics=("arbitrary", "arbitrary")),
    )(P, P, P, *gathers)


def sb_bwd(P, carries, dmixed, *, name, rides=()):
    L = P.shape[0]
    tq = _tq(L)
    nd = tq // BLK

    def body(q_ref, k_ref, v_ref, c_ref, do_ref, dq_ref, dk_ref, dv_ref):
        I = pl.program_id(1)

        @pl.when(I == 0)
        def _():
            dk_ref[...] = jnp.zeros_like(dk_ref)
            dv_ref[...] = jnp.zeros_like(dv_ref)

        tr = _tri2("right", 2)
        tl = _tri2("left", 1)
        lane_q = lax.broadcasted_iota(jnp.int32, (tq, 128), 1)
        first_k = lax.broadcasted_iota(jnp.int32, (BLK, 128), 1) < 64
        q = (q_ref[...] * 0.125).astype(BF16)
        do = do_ref[...].astype(BF16)

        def tiles(T, carry, kind):
            dq, PL0, PL1 = carry
            js = [T * nd + u for u in range(nd)]
            st = []
            for j in js:
                off = pl.multiple_of(j * BLK, BLK)
                kcat = jnp.concatenate(_head_split(k_ref[pl.ds(off, BLK), :].astype(BF16), first_k), axis=0)
                vcat = jnp.concatenate(_head_split(v_ref[pl.ds(off, BLK), :].astype(BF16), first_k), axis=0)
                st.append([off, kcat, _dot_nt(q, kcat), _dot_nt(do, vcat)])
            for u, (s, j) in enumerate(zip(st, js)):
                z = s[2]
                sp = _softplus(z)
                mask = _sb_mask2(I, j, tq) if kind == "diag" else (pad_ok if kind == "first" and u == 0 else None)
                spm = sp if mask is None else jnp.where(mask, sp, 0.0)
                s += [mask, jnp.exp(z - sp), _dot(_split2(spm), tr)]
            for s, j in zip(st, js):
                off, kcat, z, dw, mask, sig, S = s
                R0 = jnp.sum(jnp.where(lane_q == j, c_ref[0], 0.0), axis=1, keepdims=True)
                R1 = jnp.sum(jnp.where(lane_q == j, c_ref[1], 0.0), axis=1, keepdims=True)
                w = sig * jnp.exp(_per_head(-S, R0, R1))
                if mask is not None:
                    w = jnp.where(mask, w, 0.0)
                dA = dw * w
                dvf = _dot_tn(w.astype(BF16), do)
                dv_ref[pl.ds(off, BLK), :] += jnp.where(first_k, dvf[0:BLK], dvf[BLK:2 * BLK])
                s += [dA, _dot(dA.astype(BF16), tl)]
            for off, kcat, z, dw, mask, sig, S, dA, pre in st:
                dz = dA - sig * (dA + _per_head(pre, PL0, PL1))
                if mask is not None:
                    dz = jnp.where(mask, dz, 0.0)
                dzb = dz.astype(BF16)
                dkf = _dot_tn(dzb, q)
                dk_ref[pl.ds(off, BLK), :] += jnp.where(first_k, dkf[0:BLK], dkf[BLK:2 * BLK])
                dq = dq + _dot(dzb, kcat)
                PL0 = PL0 + (pre[:, BLK - 1:BLK] + dA[:, BLK - 1:BLK])
                PL1 = PL1 + (pre[:, 2 * BLK - 1:2 * BLK] + dA[:, 2 * BLK - 1:2 * BLK])
            return (dq, PL0, PL1)

        pad_ok = (lax.broadcasted_iota(jnp.int32, (tq, 256), 1) & (BLK - 1)) >= N_PAD
        carry = (jnp.zeros((tq, 128), F32), jnp.zeros((tq, 1), F32), jnp.zeros((tq, 1), F32))
        carry = lax.fori_loop(0, jnp.minimum(I, 1), lambda t, c: tiles(0, c, "first"), carry)
        carry = lax.fori_loop(1, jnp.maximum(I, 1), lambda T, c: tiles(T, c, None), carry)
        carry = tiles(I, carry, "diag")
        dq_ref[...] = carry[0] * 0.125

    blk = lambda c0: pl.BlockSpec((tq, 128), lambda hp, i: (i, c0 + hp))
    full = lambda c0: pl.BlockSpec((L, 128), lambda hp, i: (0, c0 + hp))
    sds = jax.ShapeDtypeStruct((L, 512), F32)
    nhp, nI = SB_HEADS // 2, L // tq
    r_in, r_out, r_shapes, r_sems = _ride_io(rides)
    return pl.pallas_call(
        _ride_kernel(body, 5, 3, rides, nhp, nI), name=name, grid=(nhp, nI),
        in_specs=[blk(C_SBQ // 128), full(C_SBK // 128), full(C_SBV // 128),
                  pl.BlockSpec((2, tq, 128), lambda hp, i: (hp, i, 0)), blk(0)] + r_in,
        out_specs=[blk(0), full(0), full(0)] + r_out, out_shape=[sds, sds, sds] + r_shapes, scratch_shapes=r_sems,
        compiler_params=pltpu.CompilerParams(dimension_semantics=("arbitrary", "arbitrary")),
    )(P, P, P, carries, dmixed, *[a for _, a in rides])


def _v2_sb_fwd(P, *, name):
    L = P.shape[0]
    tq = _tq(L)
    nd = tq // BLK

    def body(q_ref, k_ref, v_ref, o_ref, c_ref):
        I = pl.program_id(1)
        tt = _tri("right")
        lane_q = lax.broadcasted_iota(jnp.int32, (tq, 128), 1)
        first_k = lax.broadcasted_iota(jnp.int32, (BLK, 128), 1) < 64
        qm = [x.astype(BF16) for x in _head_split(q_ref[...] * 0.125, lane_q < 64)]
        c_ref[...] = jnp.zeros_like(c_ref)

        def tile(j, carry, masked):
            o, R = carry[0], carry[1:]
            off = pl.multiple_of(j * BLK, BLK)
            kb = k_ref[pl.ds(off, BLK), :].astype(BF16)
            vcat = jnp.concatenate(_head_split(v_ref[pl.ds(off, BLK), :].astype(BF16), first_k), axis=0)
            mask = _sb_mask(I, j, tq) if masked else None
            ws, Rn = [], []
            for h in range(2):
                z = _dot_nt(qm[h], kb)
                sp = _softplus(z)
                spm = jnp.where(mask, sp, 0.0) if masked else sp
                w = jnp.exp(z - sp - _tri_sum(spm, tt) + R[h])
                if masked:
                    w = jnp.where(mask, w, 0.0)
                c_ref[h] = jnp.where(lane_q == j, R[h], c_ref[h])
                ws.append(w.astype(BF16))
                Rn.append(R[h] - jnp.sum(spm, axis=1, keepdims=True))
            return (o + _dot(jnp.concatenate(ws, axis=1), vcat), Rn[0], Rn[1])

        carry = (jnp.zeros((tq, 128), F32), jnp.zeros((tq, 1), F32), jnp.zeros((tq, 1), F32))
        carry = lax.fori_loop(0, nd, lambda t, c: tile(I * nd + nd - 1 - t, c, True), carry)
        carry = lax.fori_loop(0, jnp.maximum(I * nd - 1, 0), lambda t, c: tile(I * nd - 1 - t, c, False), carry)
        carry = lax.fori_loop(0, jnp.minimum(I, 1), lambda t, c: tile(0, c, True), carry)
        o_ref[...] = carry[0]

    return pl.pallas_call(
        body, name=name, grid=(SB_HEADS // 2, L // tq),
        in_specs=[pl.BlockSpec((tq, 128), lambda hp, i: (i, C_SBQ // 128 + hp)),
                  pl.BlockSpec((L, 128), lambda hp, i: (0, C_SBK // 128 + hp)),
                  pl.BlockSpec((L, 128), lambda hp, i: (0, C_SBV // 128 + hp))],
        out_specs=(pl.BlockSpec((tq, 128), lambda hp, i: (i, hp)), pl.BlockSpec((2, tq, 128), lambda hp, i: (hp, i, 0))),
        out_shape=(jax.ShapeDtypeStruct((L, 512), F32), jax.ShapeDtypeStruct((SB_HEADS, L, 128), F32)),
        compiler_params=pltpu.CompilerParams(dimension_semantics=("parallel", "arbitrary")),
    )(P, P, P)


def _v2_sb_bwd(P, carries, dmixed, *, name):
    L = P.shape[0]
    tq = _tq(L)
    nd = tq // BLK

    def body(q_ref, k_ref, v_ref, c_ref, do_ref, dq_ref, dk_ref, dv_ref):
        I = pl.program_id(1)

        @pl.when(I == 0)
        def _():
            dk_ref[...] = jnp.zeros_like(dk_ref)
            dv_ref[...] = jnp.zeros_like(dv_ref)

        tr = _tri("right")
        tl = _tri("left")
        lane_q = lax.broadcasted_iota(jnp.int32, (tq, 128), 1)
        first_k = lax.broadcasted_iota(jnp.int32, (BLK, 128), 1) < 64
        qm = [x.astype(BF16) for x in _head_split(q_ref[...] * 0.125, lane_q < 64)]
        dom = [x.astype(BF16) for x in _head_split(do_ref[...], lane_q < 64)]
        qcat = jnp.concatenate(qm, axis=0)
        docat = jnp.concatenate(dom, axis=0)

        def tile(j, carry, masked):
            dq, PL = carry[0], carry[1:]
            off = pl.multiple_of(j * BLK, BLK)
            kb = k_ref[pl.ds(off, BLK), :].astype(BF16)
            vb = v_ref[pl.ds(off, BLK), :].astype(BF16)
            kcat = jnp.concatenate(_head_split(kb, first_k), axis=0)
            mask = _sb_mask(I, j, tq) if masked else None
            dzs, wsb, PLn = [], [], []
            for h in range(2):
                R = jnp.sum(jnp.where(lane_q == j, c_ref[h], 0.0), axis=1, keepdims=True)
                z = _dot_nt(qm[h], kb)
                sp = _softplus(z)
                spm = jnp.where(mask, sp, 0.0) if masked else sp
                sig = jnp.exp(z - sp)
                w = sig * jnp.exp(R - _tri_sum(spm, tr))
                if masked:
                    w = jnp.where(mask, w, 0.0)
                dA = _dot_nt(dom[h], vb) * w
                dz = dA - sig * (dA + _tri_sum(dA, tl) + PL[h])
                if masked:
                    dz = jnp.where(mask, dz, 0.0)
                dzs.append(dz.astype(BF16))
                wsb.append(w.astype(BF16))
                PLn.append(PL[h] + jnp.sum(dA, axis=1, keepdims=True))
            dk_ref[pl.ds(off, BLK), :] += _dot_tn(jnp.concatenate(dzs, axis=0), qcat)
            dv_ref[pl.ds(off, BLK), :] += _dot_tn(jnp.concatenate(wsb, axis=0), docat)
            return (dq + _dot(jnp.concatenate(dzs, axis=1), kcat), PLn[0], PLn[1])

        carry = (jnp.zeros((tq, 128), F32), jnp.zeros((tq, 1), F32), jnp.zeros((tq, 1), F32))
        carry = lax.fori_loop(0, jnp.minimum(I, 1), lambda t, c: tile(0, c, True), carry)
        carry = lax.fori_loop(1, jnp.maximum(I * nd, 1), lambda j, c: tile(j, c, False), carry)
        carry = lax.fori_loop(0, nd, lambda t, c: tile(I * nd + t, c, True), carry)
        dq_ref[...] = carry[0] * 0.125

    blk = lambda c0: pl.BlockSpec((tq, 128), lambda hp, i: (i, c0 + hp))
    full = lambda c0: pl.BlockSpec((L, 128), lambda hp, i: (0, c0 + hp))
    sds = jax.ShapeDtypeStruct((L, 512), F32)
    return pl.pallas_call(
        body, name=name, grid=(SB_HEADS // 2, L // tq),
        in_specs=[blk(C_SBQ // 128), full(C_SBK // 128), full(C_SBV // 128),
                  pl.BlockSpec((2, tq, 128), lambda hp, i: (hp, i, 0)), blk(0)],
        out_specs=(blk(0), full(0), full(0)), out_shape=(sds, sds, sds),
        compiler_params=pltpu.CompilerParams(dimension_semantics=("parallel", "arbitrary")),
    )(P, P, P, carries, dmixed)


def _mla_mask2(I, j, tq):
    row = lax.broadcasted_iota(jnp.int32, (tq, tq), 0)
    col = lax.broadcasted_iota(jnp.int32, (tq, tq), 1)
    t_idx = I * tq + row
    s_idx = j * tq + col
    return (s_idx <= t_idx) & ((s_idx >= N_PAD) | (s_idx == t_idx))


def _mla_qcat(q_ref, cs_ref, sn_ref, lane_q):
    qn = q_ref[:, 0:128]
    qr = _rope(q_ref[:, 128:256], cs_ref[...], sn_ref[...], MLA_ROPE // 2)
    zero = jnp.zeros_like(qn)
    r0 = lane_q < MLA_ROPE
    r1 = (lane_q >= MLA_ROPE) & (lane_q < 2 * MLA_ROPE)
    n0, n1 = _head_split(qn, lane_q < 64)
    return [jnp.concatenate([n0, jnp.where(r0, qr, zero)], axis=1).astype(BF16),
            jnp.concatenate([n1, jnp.where(r1, qr, zero)], axis=1).astype(BF16)]


def mla_fwd(Q, KV, KR, cs, sn, mixed, *, name, gathers=()):
    L = Q.shape[0]
    tq = _tq(L)

    def body(q_ref, kn_ref, v_ref, kr_ref, cs_ref, sn_ref, o_ref, lse_ref):
        I = pl.program_id(1)
        lane_q = lax.broadcasted_iota(jnp.int32, (tq, 128), 1)
        first_q = lane_q < 64
        qcat = _mla_qcat(q_ref, cs_ref, sn_ref, lane_q)

        def tile(j, carry, masked, wide=1):
            acc, ml = carry[0], carry[1:]
            off = pl.multiple_of(j * tq, tq)
            tk = wide * tq
            first_k = lax.broadcasted_iota(jnp.int32, (tk, 128), 1) < 64
            kcat = jnp.concatenate([kn_ref[pl.ds(off, tk), :], kr_ref[pl.ds(off, tk), :]], axis=1)
            vcat = jnp.concatenate(_head_split(v_ref[pl.ds(off, tk), :], first_k), axis=0)
            mask = _mla_mask2(I, j, tq) if masked else None
            ps, al, out = [], [], []
            for h in range(2):
                m, l = ml[2 * h], ml[2 * h + 1]
                s = _dot_nt(qcat[h], kcat) * MLA_SCALE
                if masked:
                    s = jnp.where(mask, s, NEG)
                m_new = jnp.maximum(m, jnp.max(s, axis=1, keepdims=True))
                a = jnp.exp(m - m_new)
                p = jnp.exp(s - m_new)
                ps.append(p.astype(BF16))
                al.append(a)
                out += [m_new, a * l + jnp.sum(p, axis=1, keepdims=True)]
            acc = acc * jnp.where(first_q, al[0], al[1]) + _dot(jnp.concatenate(ps, axis=1), vcat)
            return (acc,) + tuple(out)

        ml0 = (jnp.full((tq, 1), NEG, F32), jnp.zeros((tq, 1), F32))
        carry = (jnp.zeros((tq, 128), F32),) + ml0 + ml0
        carry = lax.fori_loop(0, jnp.minimum(I, 1), lambda t, c: tile(0, c, True), carry)
        n_in = jnp.maximum(I - 1, 0)
        carry = lax.fori_loop(0, n_in // 2, lambda t, c: tile(1 + 2 * t, c, False, 2), carry)
        carry = lax.fori_loop(0, n_in % 2, lambda t, c: tile(I - 1, c, False), carry)
        carry = tile(I, carry, True)
        acc, m0, l0, m1, l1 = carry
        o_ref[...] = acc / jnp.where(first_q, l0, l1)
        lse_ref[0] = jnp.where(lane_q == 0, m0 + jnp.log(l0), jnp.where(lane_q == 1, m1 + jnp.log(l1), 0.0))

    ng = len(gathers)
    nhp, nI = MLA_HEADS // 2, L // tq

    def kern(q, kn, v, kr, c, s, mixed_any, *rest):
        w_refs, (o, lse), g_refs, sems = rest[:ng], rest[ng:ng + 2], rest[ng + 2:2 * ng + 2], rest[2 * ng + 2:]
        hp, I = pl.program_id(0), pl.program_id(1)
        _gather_ride(w_refs, g_refs, sems, (hp == 0) & (I == 0), (hp == nhp - 1) & (I == 0), (hp == nhp - 1) & (I == nI - 1))
        body(q, kn, v, kr, c, s, o, lse)

    g_in, g_out, g_shapes, g_sems = _gather_io(gathers)
    return pl.pallas_call(
        kern, name=name, grid=(nhp, nI),
        in_specs=[pl.BlockSpec((tq, 256), lambda hp, i: (i, hp)),
                  pl.BlockSpec((L, 128), lambda hp, i: (0, hp)),
                  pl.BlockSpec((L, 128), lambda hp, i: (0, 4 + hp)),
                  pl.BlockSpec((L, 128), lambda hp, i: (0, 0)),
                  pl.BlockSpec((tq, 128), lambda hp, i: (i, 0)), pl.BlockSpec((tq, 128), lambda hp, i: (i, 0)),
                  pl.BlockSpec(memory_space=pl.ANY)] + g_in,
        out_specs=[pl.BlockSpec((tq, 128), lambda hp, i: (i, 4 + hp)), pl.BlockSpec((1, tq, 128), lambda hp, i: (hp, i, 0))] + g_out,
        out_shape=[jax.ShapeDtypeStruct(mixed.shape, F32), jax.ShapeDtypeStruct((4, L, 128), F32)] + g_shapes,
        input_output_aliases={6: 0}, scratch_shapes=g_sems,
        compiler_params=pltpu.CompilerParams(dimension_semantics=("arbitrary", "arbitrary")),
    )(Q, KV, KV, KR, cs, sn, mixed, *gathers)


def mla_bwd(Q, KV, KR, cs, sn, mixed, dmixed, lse, *, name, rides=()):
    L = Q.shape[0]
    tq = _tq(L)

    def body(q_ref, kn_ref, v_ref, kr_ref, cs_ref, sn_ref, o_ref, do_ref, lse_ref, dq_ref, dkn_ref, dv_ref, dkr_ref):
        hp = pl.program_id(0)
        I = pl.program_id(1)

        @pl.when(I == 0)
        def _():
            dkn_ref[...] = jnp.zeros_like(dkn_ref)
            dv_ref[...] = jnp.zeros_like(dv_ref)

        @pl.when((I == 0) & (hp == 0))
        def _():
            dkr_ref[...] = jnp.zeros_like(dkr_ref)

        lane_q = lax.broadcasted_iota(jnp.int32, (tq, 128), 1)
        first_q = lane_q < 64
        qcat = _mla_qcat(q_ref, cs_ref, sn_ref, lane_q)
        qq = jnp.concatenate(qcat, axis=0)
        do = do_ref[...]
        prod = do * o_ref[...]
        dd = [jnp.sum(jnp.where(first_q, prod, 0.0), axis=1, keepdims=True),
              jnp.sum(jnp.where(first_q, 0.0, prod), axis=1, keepdims=True)]
        dom = [x.astype(BF16) for x in _head_split(do, first_q)]
        docat = jnp.concatenate(dom, axis=0)
        lses = [lse_ref[0, :, 0:1], lse_ref[0, :, 1:2]]

        def tile(j, dq, masked, wide=1):
            off = pl.multiple_of(j * tq, tq)
            tk = wide * tq
            lane_k = lax.broadcasted_iota(jnp.int32, (tk, 256), 1)
            sel0 = (lane_k < 64) | ((lane_k >= 128) & (lane_k < 128 + MLA_ROPE))
            sel1 = ((lane_k >= 64) & (lane_k < 128)) | ((lane_k >= 128 + MLA_ROPE) & (lane_k < 128 + 2 * MLA_ROPE))
            kcat = jnp.concatenate([kn_ref[pl.ds(off, tk), :], kr_ref[pl.ds(off, tk), :]], axis=1)
            vb = v_ref[pl.ds(off, tk), :]
            zero = jnp.zeros_like(kcat)
            kk = jnp.concatenate([jnp.where(sel0, kcat, zero), jnp.where(sel1, kcat, zero)], axis=0)
            mask = _mla_mask2(I, j, tq) if masked else None
            dss, pbs = [], []
            for h in range(2):
                s = _dot_nt(qcat[h], kcat) * MLA_SCALE
                p = jnp.exp(s - lses[h])
                if masked:
                    p = jnp.where(mask, p, 0.0)
                dp = _dot_nt(dom[h], vb)
                dss.append((p * (dp - dd[h]) * MLA_SCALE).astype(BF16))
                pbs.append(p.astype(BF16))
            dkc = _dot_tn(jnp.concatenate(dss, axis=0), qq)
            dkn_ref[pl.ds(off, tk), :] += dkc[:, 0:128]
            dkr_ref[pl.ds(off, tk), :] += dkc[:, 128:256]
            dv_ref[pl.ds(off, tk), :] += _dot_tn(jnp.concatenate(pbs, axis=0), docat)
            return dq + _dot(jnp.concatenate(dss, axis=1), kk)

        dq = jnp.zeros((tq, 256), F32)
        dq = lax.fori_loop(0, jnp.minimum(I, 1), lambda t, c: tile(0, c, True), dq)
        n_in = jnp.maximum(I - 1, 0)
        dq = lax.fori_loop(0, n_in // 2, lambda t, c: tile(1 + 2 * t, c, False, 2), dq)
        dq = lax.fori_loop(0, n_in % 2, lambda t, c: tile(I - 1, c, False), dq)
        dq = tile(I, dq, True)
        dq_ref[:, 0:128] = dq[:, 0:128]
        dq_ref[:, 128:256] = _rope_t(dq[:, 128:256], cs_ref[...], sn_ref[...], MLA_ROPE // 2)

    blk = lambda c0: pl.BlockSpec((tq, 128), lambda hp, i: (i, c0 + hp))
    full = lambda c0: pl.BlockSpec((L, 128), lambda hp, i: (0, c0 + hp))
    tab = pl.BlockSpec((tq, 128), lambda hp, i: (i, 0))
    nhp, nI = MLA_HEADS // 2, L // tq
    r_in, r_out, r_shapes, r_sems = _ride_io(rides)
    return pl.pallas_call(
        _ride_kernel(body, 9, 4, rides, nhp, nI), name=name, grid=(nhp, nI),
        in_specs=[pl.BlockSpec((tq, 256), lambda hp, i: (i, hp)), full(0), full(4),
                  pl.BlockSpec((L, 128), lambda hp, i: (0, 0)), tab, tab, blk(4), blk(4),
                  pl.BlockSpec((1, tq, 128), lambda hp, i: (hp, i, 0))] + r_in,
        out_specs=[pl.BlockSpec((tq, 256), lambda hp, i: (i, hp)), full(0), full(0),
                   pl.BlockSpec((L, 128), lambda hp, i: (0, 0))] + r_out,
        out_shape=[jax.ShapeDtypeStruct((L, 1024), F32), jax.ShapeDtypeStruct((L, 512), F32),
                   jax.ShapeDtypeStruct((L, 512), F32), jax.ShapeDtypeStruct((L, 128), F32)] + r_shapes,
        scratch_shapes=r_sems,
        compiler_params=pltpu.CompilerParams(dimension_semantics=("arbitrary", "arbitrary")),
    )(Q, KV, KV, KR, cs, sn, mixed, dmixed, lse, *[a for _, a in rides])


def _ret_decay(h):
    lg = RET_LOG_G[h]
    r = lax.broadcasted_iota(jnp.int32, (BLK, BLK), 0)
    c = lax.broadcasted_iota(jnp.int32, (BLK, BLK), 1)
    diff = (r - c).astype(F32)
    d_in = jnp.where(diff >= 0, jnp.exp(jnp.maximum(diff, 0.0) * lg), 0.0)
    idx = lax.broadcasted_iota(jnp.int32, (BLK, 1), 0).astype(F32)
    q_decay = jnp.exp((idx + 1.0) * lg)
    k_decay = jnp.exp((BLK - 1.0 - idx) * lg)
    c_decay = math.exp(BLK * lg)
    return d_in, q_decay, k_decay, c_decay


def _ret_qk(qk_ref, cs_ref, sn_ref, n):
    cs = jnp.concatenate([cs_ref[...]] * 2, axis=1)
    sn = jnp.concatenate([sn_ref[...]] * 2, axis=1)
    rq = _rope(qk_ref[:, 0:256], cs, sn, RET_QK // 2)
    row = n * BLK + lax.broadcasted_iota(jnp.int32, (BLK, 256), 0)
    kmul = jnp.where(row >= N_PAD, 0.125, 0.0)
    rk = _rope(qk_ref[:, 256:512], cs, sn, RET_QK // 2) * kmul
    return rq, rk, cs, sn, kmul


def _head_norm(y):
    mu = jnp.mean(y, axis=-1, keepdims=True)
    yc = y - mu
    r = lax.rsqrt(jnp.mean(jnp.square(yc), axis=-1, keepdims=True) + LN_EPS)
    return yc * r, r


def ret_fwd(P, cs, sn, mixed, *, name):
    L = P.shape[0]
    nb = L // BLK

    def body(qk_ref, v_ref, g_ref, cs_ref, sn_ref, o_ref, y_ref, st_ref, state):
        n = pl.program_id(0)

        @pl.when(n == 0)
        def _():
            state[...] = jnp.zeros_like(state)

        st_ref[0] = state[...]
        rq, rk, _, _, _ = _ret_qk(qk_ref, cs_ref, sn_ref, n)
        outs, ys = [], []
        for h in range(RET_HEADS):
            d_in, q_decay, k_decay, c_decay = _ret_decay(h)
            q = rq[:, 64 * h:64 * h + 64].astype(BF16)
            kf = rk[:, 64 * h:64 * h + 64]
            v = v_ref[:, 128 * h:128 * h + 128].astype(BF16)
            S = state[h]
            inner = _dot_nt(q, kf.astype(BF16)) * d_in
            y = _dot(inner.astype(BF16), v) + _dot(q, S.astype(BF16)) * q_decay
            state[h] = S * c_decay + _dot_tn((kf * k_decay).astype(BF16), v)
            g = g_ref[:, 128 * h:128 * h + 128]
            ys.append(y)
            outs.append(g * jax.nn.sigmoid(g) * _head_norm(y)[0])
        o_ref[...] = jnp.concatenate(outs, axis=1)
        y_ref[...] = jnp.concatenate(ys, axis=1)

    blk512 = lambda c: pl.BlockSpec((BLK, 512), lambda n: (n, c))
    tab = pl.BlockSpec((BLK, 128), lambda n: (n, 0))
    return pl.pallas_call(
        lambda qk, v, g, c, s, mixed_any, o, y, st, state: body(qk, v, g, c, s, o, y, st, state),
        name=name, grid=(nb,),
        in_specs=[blk512(C_RQ // 512), blk512(C_RV // 512), blk512(C_RG // 512), tab, tab, pl.BlockSpec(memory_space=pl.ANY)],
        out_specs=(blk512(2), blk512(0), pl.BlockSpec((1, RET_HEADS, RET_QK, RET_V), lambda n: (n, 0, 0, 0))),
        out_shape=(jax.ShapeDtypeStruct(mixed.shape, F32), jax.ShapeDtypeStruct((L, 512), F32),
                   jax.ShapeDtypeStruct((nb, RET_HEADS, RET_QK, RET_V), F32)),
        input_output_aliases={5: 0},
        scratch_shapes=[pltpu.VMEM((RET_HEADS, RET_QK, RET_V), F32)],
        compiler_params=pltpu.CompilerParams(dimension_semantics=("arbitrary",)),
    )(P, P, P, cs, sn, mixed)


def ret_bwd(P, y, states, dmixed, cs, sn, *, name):
    L = P.shape[0]
    nb = L // BLK

    def body(qk_ref, v_ref, g_ref, y_ref, st_ref, do_ref, cs_ref, sn_ref, dqk_ref, dv_ref, dg_ref, dstate):
        n = nb - 1 - pl.program_id(0)

        @pl.when(pl.program_id(0) == 0)
        def _():
            dstate[...] = jnp.zeros_like(dstate)

        rq, rk, cs, sn, kmul = _ret_qk(qk_ref, cs_ref, sn_ref, n)
        dqs, dks, dvs, dgs = [], [], [], []
        for h in range(RET_HEADS):
            d_in, q_decay, k_decay, c_decay = _ret_decay(h)
            sv = slice(128 * h, 128 * h + 128)
            q = rq[:, 64 * h:64 * h + 64].astype(BF16)
            kf = rk[:, 64 * h:64 * h + 64]
            k = kf.astype(BF16)
            kd = (kf * k_decay).astype(BF16)
            v = v_ref[:, sv].astype(BF16)
            g = g_ref[:, sv]
            do = do_ref[:, sv]
            yh = y_ref[:, sv]
            S = st_ref[0, h].astype(BF16)
            dS = dstate[h]
            sg = jax.nn.sigmoid(g)
            yn, r = _head_norm(yh)
            dgs.append(do * yn * (sg * (1.0 + g * (1.0 - sg))))
            dyn = do * (g * sg)
            dy = r * (dyn - jnp.mean(dyn, axis=-1, keepdims=True) - yn * jnp.mean(dyn * yn, axis=-1, keepdims=True))
            dyb = dy.astype(BF16)
            dyq = (dy * q_decay).astype(BF16)
            inner = (_dot_nt(q, k) * d_in).astype(BF16)
            A = (_dot_nt(dyb, v) * d_in).astype(BF16)
            dSb = dS.astype(BF16)
            dqs.append(_dot(A, k) + _dot_nt(dyq, S))
            dks.append(_dot_tn(A, q) + _dot_nt(v, dSb) * k_decay)
            dvs.append(_dot_tn(inner, dyb) + _dot(kd, dSb))
            dstate[h] = dS * c_decay + _dot_tn(q, dyq)
        drq = _rope_t(jnp.concatenate(dqs, axis=1), cs, sn, RET_QK // 2)
        drk = _rope_t(jnp.concatenate(dks, axis=1) * kmul, cs, sn, RET_QK // 2)
        dqk_ref[...] = jnp.concatenate([drq, drk], axis=1)
        dv_ref[...] = jnp.concatenate(dvs, axis=1)
        dg_ref[...] = jnp.concatenate(dgs, axis=1)

    blk512 = lambda c: pl.BlockSpec((BLK, 512), lambda t: (nb - 1 - t, c))
    tab = pl.BlockSpec((BLK, 128), lambda t: (nb - 1 - t, 0))
    sds = jax.ShapeDtypeStruct((L, 512), F32)
    return pl.pallas_call(
        body, name=name, grid=(nb,),
        in_specs=[blk512(C_RQ // 512), blk512(C_RV // 512), blk512(C_RG // 512), blk512(0),
                  pl.BlockSpec((1, RET_HEADS, RET_QK, RET_V), lambda t: (nb - 1 - t, 0, 0, 0)), blk512(2), tab, tab],
        out_specs=(blk512(0), blk512(0), blk512(0)), out_shape=(sds, sds, sds),
        scratch_shapes=[pltpu.VMEM((RET_HEADS, RET_QK, RET_V), F32)],
        compiler_params=pltpu.CompilerParams(dimension_semantics=("arbitrary",)),
    )(P, P, P, y, states, dmixed, cs, sn)


def _perm_w_in(w):
    pad = jnp.zeros(w.shape[:-1] + (N_INP - N_IN,), w.dtype)
    return jnp.concatenate([w[..., 0:1536], w[..., 2208:3744], w[..., 1536:2208], pad], axis=-1)


def _unperm_w_in(g):
    return jnp.concatenate([g[..., 0:1536], g[..., 3072:3744], g[..., 1536:3072]], axis=-1)


def _perm_w_uq(w):
    lead = w.shape[:-1]
    w5 = w.reshape(lead + (4, 2, 96))
    nope = w5[..., :64].reshape(lead + (4, 128))
    rope = w5[..., 64:].reshape(lead + (4, 64))
    return jnp.concatenate([nope, rope, jnp.zeros(lead + (4, 64), w.dtype)], axis=-1).reshape(lead + (1024,))


def _unperm_w_uq(g):
    lead = g.shape[:-1]
    g4 = g.reshape(lead + (4, 256))
    nope = g4[..., :128].reshape(lead + (4, 2, 64))
    rope = g4[..., 128:192].reshape(lead + (4, 2, 32))
    return jnp.concatenate([nope, rope], axis=-1).reshape(lead + (768,))


def _perm_w_ukv(w):
    lead = w.shape[:-1]
    w4 = w.reshape(lead + (8, 128))
    return jnp.concatenate([w4[..., :64].reshape(lead + (512,)), w4[..., 64:].reshape(lead + (512,))], axis=-1)


def _unperm_w_ukv(g):
    lead = g.shape[:-1]
    return jnp.concatenate([g[..., :512].reshape(lead + (8, 64)), g[..., 512:].reshape(lead + (8, 64))],
                           axis=-1).reshape(lead + (1024,))


def _cols(a):
    return jnp.moveaxis(a, 0, 2).reshape(a.shape[1], a.shape[2], 4 * a.shape[3])


def _col_shards(a):
    return jnp.moveaxis(a.reshape(a.shape[0], a.shape[1], 4, a.shape[2] // 4), 2, 0)


_RS_SHAPES = {"w_in": (D_MODEL, N_IN // 4), "w_uq": (MLA_Q_LORA, 192), "w_ukv": (MLA_KV_LORA, 256),
              "w_out": (384, D_MODEL), "w_ff1": (D_MODEL, D_FF // 4), "w_ff2": (D_FF // 4, D_MODEL)}


def _rope_tables(L, half):
    pos = (jnp.arange(L) - N_PAD).astype(F32)
    inv = ROPE_THETA ** (-jnp.arange(half, dtype=F32) / half)
    ang = pos[:, None] * inv[None, :]
    cos, sin = jnp.cos(ang), jnp.sin(ang)
    reps = 128 // (2 * half)
    cs = jnp.tile(jnp.concatenate([cos, cos], axis=1), (1, reps))
    sn = jnp.tile(jnp.concatenate([-sin, sin], axis=1), (1, reps))
    return cs, sn


def _device_step(x, target, meta, ln_emb_g, ln_emb_b, w_in, q_norm, kv_norm, w_uq, w_ukv, w_out,
                 ln1_g, ln1_b, w_ff1, w_ff2, ln2_g, ln2_b, late=None):
    S = x.shape[0]
    L = S + BLK
    depth = ln1_g.shape[0]
    w_in, w_uq, w_ukv = list(w_in), list(w_uq), list(w_ukv)
    cs_m, sn_m = _rope_tables(L, MLA_ROPE // 2)
    cs_r, sn_r = _rope_tables(L, RET_QK // 2)
    hcat = jnp.concatenate([jnp.zeros((N_PAD, D_MODEL), F32), meta, x], axis=0)
    h, hb, _ = ln_fwd(hcat, ln_emb_g, ln_emb_b, name="ln_emb_fwd")

    def own(gathered, shard):
        return lax.dynamic_update_slice(gathered, shard[None], (late["s0"], 0, 0, 0))

    def own_slot(gathered, shard):
        return jnp.moveaxis(own(gathered, shard), 0, 1)

    saved = []
    for l in range(depth):
        Psb = mm_nn(hb, w_in[l][..., :N_SB], tn=N_SB, name=f"in_proj_sb_{l}", out_dtype=BF16)
        P = mm_nn(hb, w_in[l][..., N_SB:], tn=768, name=f"in_proj_{l}")
        if late is not None and l == 0:
            mixed, sbc, ga, gb = sb_fwd(Psb, name=f"sb_fwd_{l}", gathers=(late["w_out"], late["w_ff1"]))
            w_out, w_ff1 = own_slot(ga, late["w_out"]), own_slot(gb, late["w_ff1"])
        elif late is not None and l == 1:
            his = [late["w_in_hi"], late["w_uq_hi"], late["w_ukv_hi"]]
            mixed, sbc, *gs = sb_fwd(Psb, name=f"sb_fwd_{l}", gathers=tuple(his))
            gs = [_cols(own(g, s)) for g, s in zip(gs, his)]
            w_in += list(_perm_w_in(gs[0])[:, None])
            w_uq += list(_perm_w_uq(gs[1])[:, None])
            w_ukv += list(_perm_w_ukv(gs[2])[:, None])
        else:
            mixed, sbc = sb_fwd(Psb, name=f"sb_fwd_{l}")
        nq, nkv, KR = mla_pre_fwd(P, q_norm[l], kv_norm[l], cs_m, sn_m, name=f"mla_pre_fwd_{l}")
        Q = mm_nn(nq, w_uq[l], tn=512, name=f"uq_{l}")
        KV = mm_nn(nkv, w_ukv[l], tn=512, name=f"ukv_{l}", out_dtype=BF16)
        if late is not None and l == 0:
            mixed, lse, ga = mla_fwd(Q, KV, KR, cs_m, sn_m, mixed, name=f"mla_fwd_{l}", gathers=(late["w_ff2"],))
            w_ff2 = own_slot(ga, late["w_ff2"])
        else:
            mixed, lse = mla_fwd(Q, KV, KR, cs_m, sn_m, mixed, name=f"mla_fwd_{l}")
        mixed, y, states = ret_fwd(P, cs_r, sn_r, mixed, name=f"ret_fwd_{l}")
        w_out_l = w_out[l].reshape(1, 1536, D_MODEL)
        mix = mm_nn(mixed, w_out_l, tn=1024, name=f"out_proj_{l}")
        h1, h1b, z1 = ln_fwd(mix, ln1_g[l], ln1_b[l], res=h, name=f"ln1_fwd_{l}")
        U = mm_nn(h1b, w_ff1[l], tn=1024, name=f"ff1_{l}")
        w_ff2_l = w_ff2[l].reshape(1, D_FF, D_MODEL)
        mlp = mm_nn(U, w_ff2_l, tn=1024, tk=2048, prologue="relu2", name=f"ff2_{l}")
        h2, h2b, z2 = ln_fwd(mlp, ln2_g[l], ln2_b[l], res=h1, name=f"ln2_fwd_{l}")
        saved.append((hb, Psb, P, sbc, nq, nkv, KR, Q, KV, lse, y, states, mixed, z1, h1b, U, z2))
        h, hb = h2, h2b

    loss_t, dh = loss_fwd_bwd(h, target, name="loss")

    grads = {k: [None] * depth for k in ("q_norm", "kv_norm", "ln1_g", "ln1_b", "ln2_g", "ln2_b")}
    pairs = depth // 2
    g_ff1 = [lax.empty((4, 2, D_MODEL, D_FF // 4), F32) for _ in range(pairs)]
    g_ff2 = [lax.empty((4, 2, D_FF // 4, D_MODEL), F32) for _ in range(pairs)]
    g_out = [lax.empty((4, 2, 384, D_MODEL), F32) for _ in range(pairs)]
    g_in = [lax.empty((2, D_MODEL, N_INP), F32) for _ in range(pairs)]
    g_uq = [lax.empty((2, MLA_Q_LORA, 1024), F32) for _ in range(pairs)]
    g_ukv = [lax.empty((2, MLA_KV_LORA, 1024), F32) for _ in range(pairs)]

    def pair_grads(p):
        return {"w_in": _col_shards(_unperm_w_in(g_in[p])), "w_uq": _col_shards(_unperm_w_uq(g_uq[p])),
                "w_ukv": _col_shards(_unperm_w_ukv(g_ukv[p])), "w_out": g_out[p], "w_ff1": g_ff1[p], "w_ff2": g_ff2[p]}

    riding = {1: (1, tuple(_RS_SHAPES)), 0: (0, ("w_out", "w_ff1", "w_ff2"))} if late is not None and depth == 4 else {}
    done = set()
    if late is not None:
        acc = {k: lax.empty((depth,) + s, F32) for k, s in _RS_SHAPES.items()}
    for l in reversed(range(depth)):
        p, lp = l // 2, l % 2
        hb_in, Psb, P, sbc, nq, nkv, KR, Q, KV, lse, y, states, mixed, z1, h1b, U, z2 = saved[l]
        dz2, grads["ln2_g"][l], grads["ln2_b"][l] = ln_bwd(dh, z2, ln2_g[l], name=f"ln2_bwd_{l}")
        w_ff2_l = w_ff2[l].reshape(1, D_FF, D_MODEL)
        g_ff2[p] = mm_tn(U, dz2, shards=1, tko=1024, tn=1024, prologue="relu2", name=f"ff2_dw_{l}", into=(g_ff2[p], lp, "rows"))
        dU = mm_nt(dz2, w_ff2_l, tn=1024, tko=1024, relu2grad=U, name=f"ff2_dx_{l}", out_dtype=BF16)
        g_ff1[p] = mm_tn(h1b, dU, shards=4, tko=1024, tn=1024, name=f"ff1_dw_{l}", into=(g_ff1[p], lp, "cols"))
        dh1 = mm_nt(dU, w_ff1[l], tn=1024, tko=1024, axpy=(dz2, DN_ALPHA), name=f"ff1_dx_{l}")
        dz1, grads["ln1_g"][l], grads["ln1_b"][l] = ln_bwd(dh1, z1, ln1_g[l], name=f"ln1_bwd_{l}")
        w_out_l = w_out[l].reshape(1, 1536, D_MODEL)
        g_out[p] = mm_tn(mixed, dz1, shards=1, tko=384, tn=1024, name=f"out_dw_{l}", into=(g_out[p], lp, "rows"))
        dmixed = mm_nt(dz1, w_out_l, tn=1024, tko=1536, name=f"out_dx_{l}")
        d_rqk, d_rv, d_rg = ret_bwd(P, y, states, dmixed, cs_r, sn_r, name=f"ret_bwd_{l}")
        if l in riding:
            rp, rkeys = riding[l]
            GA = pair_grads(rp)
            dQ, dKN, dV, dKR, *Bs = mla_bwd(Q, KV, KR, cs_m, sn_m, mixed, dmixed, lse, name=f"mla_bwd_{l}",
                                            rides=[("sib", GA[k]) for k in rkeys])
        else:
            dQ, dKN, dV, dKR = mla_bwd(Q, KV, KR, cs_m, sn_m, mixed, dmixed, lse, name=f"mla_bwd_{l}")
        dKV = jnp.concatenate([dKN, dV], axis=1)
        g_uq[p] = mm_tn(nq, dQ, shards=1, tko=MLA_Q_LORA, tn=512, name=f"uq_dw_{l}", into=(g_uq[p], lp, "layer"))
        g_ukv[p] = mm_tn(nkv, dKV, shards=1, tko=MLA_KV_LORA, tn=512, name=f"ukv_dw_{l}", into=(g_ukv[p], lp, "layer"))
        dnq = mm_nt(dQ, w_uq[l], tn=1024, tko=MLA_Q_LORA, name=f"uq_dx_{l}")
        dnkv = mm_nt(dKV, w_ukv[l], tn=1024, tko=MLA_KV_LORA, name=f"ukv_dx_{l}")
        d_lat, grads["q_norm"][l], grads["kv_norm"][l] = mla_pre_bwd(P, dnq, dnkv, dKR, q_norm[l], kv_norm[l], cs_m, sn_m,
                                                                     name=f"mla_pre_bwd_{l}")
        if l in riding:
            As = [add_halves(GA[k], B, late["c"], name=f"rs_add1_r{rp}_{k}") for k, B in zip(rkeys, Bs)]
            dq_sb, dk_sb, dv_sb, *Bcs = sb_bwd(Psb, sbc, dmixed, name=f"sb_bwd_{l}", rides=[("chips", A) for A in As])
            for k, B, Bc in zip(rkeys, Bs, Bcs):
                acc[k] = reduce_scatter_finish(GA[k], B, Bc, late["c"], late["s0"], acc[k], 2 * rp, tag=f"r{rp}_{k}")
                done.add((rp, k))
        else:
            dq_sb, dk_sb, dv_sb = sb_bwd(Psb, sbc, dmixed, name=f"sb_bwd_{l}")
        dP = jnp.concatenate([dq_sb, dk_sb, dv_sb, d_rqk, d_rv, d_rg, d_lat], axis=1).astype(BF16)
        g_in[p] = mm_tn(hb_in, dP, shards=1, tko=1024, tn=1280, name=f"in_dw_{l}", into=(g_in[p], lp, "layer"))
        dh = mm_nt(dP, w_in[l], tn=1920, tko=1024, axpy=(dz1, DN_ALPHA), name=f"in_dx_{l}")

    dhcat, dg_emb, db_emb = ln_bwd(dh, hcat, ln_emb_g, name="ln_emb_bwd")
    out = {k: jnp.stack(v) for k, v in grads.items()}
    if late is None:
        out.update({k: jnp.concatenate([pair_grads(p)[k] for p in range(pairs)], axis=1) for k in _RS_SHAPES})
    else:
        for p in range(pairs):
            for k, G in pair_grads(p).items():
                if (p, k) not in done:
                    acc[k] = reduce_scatter_weight(G, late["c"], late["s0"], acc[k], 2 * p, tag=f"{p}_{k}")
        out.update(acc)
    out["ln_emb_g"], out["ln_emb_b"] = dg_emb, db_emb
    out["meta"] = dhcat[N_PAD:BLK]
    return loss_t[0, 0], dhcat[BLK:], out


MESH = pl.DeviceIdType.MESH
PEER_XOR = (2, 1, 3)
_HBM = pl.BlockSpec(memory_space=pltpu.HBM)


def _place():
    x, y, c = lax.axis_index("x"), lax.axis_index("y"), lax.axis_index("c")
    peers = [(1 - x, y, c), (x, 1 - y, c), (1 - x, 1 - y, c)]
    return x, y, c, 2 * x + y, peers, (x, y, 1 - c)


def _gather_plan(w_ref, out_ref, send_sems, recv_sems, base):
    x, y, c, s0, peers, sibling = _place()
    hl = w_ref.shape[0] // 2

    def piece(s, half):
        return out_ref.at[s, pl.ds(half * hl, hl)]

    def copy(k, s, half, to, src=None):
        return pltpu.make_async_remote_copy(src_ref=piece(s, half) if src is None else src, dst_ref=piece(s, half),
                                            send_sem=send_sems.at[base + k], recv_sem=recv_sems.at[base + k],
                                            device_id=to, device_id_type=MESH)

    def first():
        return [copy(k, s0, c, peers[k], src=w_ref.at[pl.ds(c * hl, hl)]) for k in range(3)]

    def passed():
        return [copy(3 + k, s0 ^ PEER_XOR[k], c, sibling) for k in range(3)]

    def start():
        for cp in first():
            cp.start()

    def forward():
        for k, cp in enumerate(passed()):
            copy(k, s0 ^ PEER_XOR[k], c, peers[k]).wait_recv()
            cp.start()

    def finish():
        for k in range(3):
            copy(3 + k, s0 ^ PEER_XOR[k], 1 - c, sibling).wait_recv()
        for cp in first() + passed():
            cp.wait_send()

    return start, forward, finish


def _gather_io(gathers):
    n = len(gathers)
    return ([_HBM] * n, [_HBM] * n, [jax.ShapeDtypeStruct((4,) + w.shape, w.dtype) for w in gathers],
            [pltpu.SemaphoreType.DMA((6 * n,)), pltpu.SemaphoreType.DMA((6 * n,))] if n else [])


def _gather_ride(w_refs, g_refs, sems, at_start, at_forward, at_finish):
    if not w_refs:
        return
    plans = [_gather_plan(w, g, sems[0], sems[1], 6 * n) for n, (w, g) in enumerate(zip(w_refs, g_refs))]
    for step, cond in enumerate((at_start, at_forward, at_finish)):
        @pl.when(cond)
        def _():
            for p in plans:
                p[step]()


def _ride_copies(kind, src_ref, dst_ref, send_sems, recv_sems, base):
    x, y, c, s0, peers, sibling = _place()
    if kind == "sib":
        hl = src_ref.shape[1] // 2
        return [pltpu.make_async_remote_copy(src_ref=src_ref.at[:, pl.ds((1 - c) * hl, hl)], dst_ref=dst_ref,
                                             send_sem=send_sems.at[base], recv_sem=recv_sems.at[base],
                                             device_id=sibling, device_id_type=MESH)]
    return [pltpu.make_async_remote_copy(src_ref=src_ref.at[s0 ^ PEER_XOR[k]], dst_ref=dst_ref.at[k],
                                         send_sem=send_sems.at[base + k], recv_sem=recv_sems.at[base + k],
                                         device_id=peers[k], device_id_type=MESH) for k in range(3)]


def _ride_io(rides):
    shapes, nsem = [], 0
    for kind, a in rides:
        shapes.append(jax.ShapeDtypeStruct(((4, a.shape[1] // 2) if kind == "sib" else (3, a.shape[1])) + a.shape[2:], a.dtype))
        nsem += 1 if kind == "sib" else 3
    sems = [pltpu.SemaphoreType.DMA((nsem,)), pltpu.SemaphoreType.DMA((nsem,))] if rides else []
    return [_HBM] * len(rides), [_HBM] * len(rides), shapes, sems


def _ride_kernel(body, n_in, n_out, rides, nhp, nI):
    nr = len(rides)

    def kern(*refs):
        ins, r_in = refs[:n_in], refs[n_in:n_in + nr]
        outs, r_out = refs[n_in + nr:n_in + nr + n_out], refs[n_in + nr + n_out:n_in + 2 * nr + n_out]
        sems = refs[n_in + 2 * nr + n_out:]
        if nr:
            hp, I = pl.program_id(0), pl.program_id(1)

            def copies():
                cps = []
                for (kind, _), s, d in zip(rides, r_in, r_out):
                    cps += _ride_copies(kind, s, d, sems[0], sems[1], len(cps))
                return cps

            @pl.when((hp == 0) & (I == 0))
            def _():
                for cp in copies():
                    cp.start()

            @pl.when((hp == nhp - 1) & (I == nI - 1))
            def _():
                for cp in copies():
                    cp.wait()

        body(*ins, *outs)

    return kern


def gather_weight(w_shard, *, name):
    nl = w_shard.shape[0]
    hl = nl // 2

    def body(w_ref, out_ref, send_sems, recv_sems):
        x, y, c, s0, peers, sibling = _place()

        def piece(s, half):
            return out_ref.at[s, pl.ds(half * hl, hl)]

        def copy(k, s, half, to, src=None):
            return pltpu.make_async_remote_copy(src_ref=piece(s, half) if src is None else src, dst_ref=piece(s, half),
                                                send_sem=send_sems.at[k], recv_sem=recv_sems.at[k],
                                                device_id=to, device_id_type=MESH)

        first = [copy(k, s0, c, peers[k], src=w_ref.at[pl.ds(c * hl, hl)]) for k in range(3)]
        for cp in first:
            cp.start()
        passed = [copy(3 + k, s0 ^ PEER_XOR[k], c, sibling) for k in range(3)]
        for k in range(3):
            copy(k, s0 ^ PEER_XOR[k], c, peers[k]).wait_recv()
            passed[k].start()
        for k in range(3):
            copy(3 + k, s0 ^ PEER_XOR[k], 1 - c, sibling).wait_recv()
        for cp in first + passed:
            cp.wait_send()

    return pl.pallas_call(
        body, name=name, in_specs=[_HBM], out_specs=_HBM,
        out_shape=jax.ShapeDtypeStruct((4,) + w_shard.shape, w_shard.dtype),
        scratch_shapes=[pltpu.SemaphoreType.DMA((6,)), pltpu.SemaphoreType.DMA((6,))],
    )(w_shard)


def send_half_to_sibling(G, *, name):
    hl = G.shape[1] // 2

    def body(g_ref, out_ref, send_sem, recv_sem):
        x, y, c, s0, peers, sibling = _place()
        cp = pltpu.make_async_remote_copy(src_ref=g_ref.at[:, pl.ds((1 - c) * hl, hl)], dst_ref=out_ref,
                                          send_sem=send_sem, recv_sem=recv_sem, device_id=sibling, device_id_type=MESH)
        cp.start()
        cp.wait()

    return pl.pallas_call(
        body, name=name, in_specs=[_HBM], out_specs=_HBM,
        out_shape=jax.ShapeDtypeStruct((4, hl) + G.shape[2:], G.dtype),
        scratch_shapes=[pltpu.SemaphoreType.DMA, pltpu.SemaphoreType.DMA],
    )(G)


def scatter_to_chips(A, *, name):
    def body(a_ref, out_ref, send_sems, recv_sems):
        x, y, c, s0, peers, sibling = _place()
        copies = [pltpu.make_async_remote_copy(src_ref=a_ref.at[s0 ^ PEER_XOR[k]], dst_ref=out_ref.at[k],
                                               send_sem=send_sems.at[k], recv_sem=recv_sems.at[k],
                                               device_id=peers[k], device_id_type=MESH) for k in range(3)]
        for cp in copies:
            cp.start()
        for cp in copies:
            cp.wait()

    return pl.pallas_call(
        body, name=name, in_specs=[_HBM], out_specs=_HBM,
        out_shape=jax.ShapeDtypeStruct((3,) + A.shape[1:], A.dtype),
        scratch_shapes=[pltpu.SemaphoreType.DMA((3,)), pltpu.SemaphoreType.DMA((3,))],
    )(A)


def join_halves(buf, a, hl, *, name):
    def body(b_ref, out_ref, send_sem, recv_sem):
        x, y, c, s0, peers, sibling = _place()
        mine, other = pl.ds(a + c * hl, hl), pl.ds(a + (1 - c) * hl, hl)
        cp = pltpu.make_async_remote_copy(src_ref=b_ref.at[mine], dst_ref=out_ref.at[mine],
                                          send_sem=send_sem, recv_sem=recv_sem, device_id=sibling, device_id_type=MESH)
        cp.start()
        pltpu.make_async_remote_copy(src_ref=b_ref.at[other], dst_ref=out_ref.at[other],
                                     send_sem=send_sem, recv_sem=recv_sem, device_id=sibling, device_id_type=MESH).wait_recv()
        cp.wait_send()

    return pl.pallas_call(
        body, name=name, in_specs=[_HBM], out_specs=_HBM, input_output_aliases={0: 0},
        out_shape=jax.ShapeDtypeStruct(buf.shape, buf.dtype),
        scratch_shapes=[pltpu.SemaphoreType.DMA, pltpu.SemaphoreType.DMA],
    )(buf)


def allgather8(xs, *, name, reduce):
    M, N = xs.shape

    def body(x_ref, out_ref, *rest):
        if reduce:
            all_ref, send_sems, recv_sems, local_sem = rest
        else:
            all_ref = out_ref
            send_sems, recv_sems, local_sem = rest
        x, y, c, s0, peers, sibling = _place()
        me = (x, y, c)
        chips = [(1 - x, y), (x, 1 - y), (1 - x, 1 - y)]

        def rows(px, py, pc):
            return all_ref.at[pl.ds((4 * px + 2 * py + pc) * M, M), :]

        def copy(k, block, to, src=None):
            return pltpu.make_async_remote_copy(src_ref=rows(*block) if src is None else src, dst_ref=rows(*block),
                                                send_sem=send_sems.at[k], recv_sem=recv_sems.at[k],
                                                device_id=to, device_id_type=MESH)

        mine = pltpu.make_async_copy(x_ref, rows(*me), local_sem)
        mine.start()
        first = [copy(0, me, sibling, src=x_ref)]
        first += [copy(1 + j, me, (*chip, c), src=x_ref) for j, chip in enumerate(chips)]
        for cp in first:
            cp.start()
        passed = [copy(4 + j, (*chip, c), sibling) for j, chip in enumerate(chips)]
        for j, chip in enumerate(chips):
            copy(1 + j, (*chip, c), me).wait_recv()
            passed[j].start()
        copy(0, sibling, me).wait_recv()
        for j, chip in enumerate(chips):
            copy(4 + j, (*chip, 1 - c), me).wait_recv()
        for cp in first + passed:
            cp.wait_send()
        mine.wait()
        if reduce:
            acc = all_ref[pl.ds(0, M), :]
            for d in range(1, 8):
                acc = acc + all_ref[pl.ds(d * M, M), :]
            out_ref[...] = acc

    vm = pl.BlockSpec(memory_space=pltpu.VMEM)
    scratch = [pltpu.SemaphoreType.DMA((7,)), pltpu.SemaphoreType.DMA((7,)), pltpu.SemaphoreType.DMA]
    if reduce:
        scratch = [pltpu.VMEM((8 * M, N), xs.dtype)] + scratch
    return pl.pallas_call(
        body, name=name, in_specs=[vm], out_specs=vm,
        out_shape=jax.ShapeDtypeStruct((M if reduce else 8 * M, N), xs.dtype), scratch_shapes=scratch,
    )(xs)


def add_halves(G, B, c, *, name):
    S, nl, R, C = G.shape
    hl = nl // 2
    tr = _pick(R, (512, 384, 256, 128))

    def body(c_ref, g_ref, b_ref, o_ref):
        o_ref[...] = (g_ref[...] + b_ref[...]).astype(BF16)

    blk = (1, 1, tr, C)
    return pl.pallas_call(
        body, name=name,
        grid_spec=pltpu.PrefetchScalarGridSpec(
            num_scalar_prefetch=1, grid=(S, hl, R // tr),
            in_specs=[pl.BlockSpec(blk, lambda s, l, r, cr: (s, cr[0] * hl + l, r, 0)),
                      pl.BlockSpec(blk, lambda s, l, r, cr: (s, l, r, 0))],
            out_specs=pl.BlockSpec(blk, lambda s, l, r, cr: (s, l, r, 0))),
        out_shape=jax.ShapeDtypeStruct((S, hl, R, C), BF16),
    )(jnp.reshape(c, (1,)).astype(jnp.int32), G, B)


def add_chips(G, B, Bc, c, s0, acc, a, *, name):
    S, nl, R, C = G.shape
    hl = nl // 2
    tr = _pick(R, (512, 384, 256, 128))

    def body(pc_ref, ps_ref, g_ref, b_ref, c0_ref, c1_ref, c2_ref, acc_ref, o_ref):
        o_ref[...] = ((((g_ref[0] + b_ref[0]) + c0_ref[0].astype(F32)) + c1_ref[0].astype(F32)) + c2_ref[0].astype(F32))

    blk = (1, 1, tr, C)
    cspec = lambda k: pl.BlockSpec(blk, lambda l, r, pc, ps: (k, l, r, 0))
    return pl.pallas_call(
        body, name=name,
        grid_spec=pltpu.PrefetchScalarGridSpec(
            num_scalar_prefetch=2, grid=(hl, R // tr),
            in_specs=[pl.BlockSpec(blk, lambda l, r, pc, ps: (ps[0], pc[0] * hl + l, r, 0)),
                      pl.BlockSpec(blk, lambda l, r, pc, ps: (ps[0], l, r, 0)), cspec(0), cspec(1), cspec(2),
                      pl.BlockSpec(memory_space=pl.ANY)],
            out_specs=pl.BlockSpec((1, tr, C), lambda l, r, pc, ps: (a + pc[0] * hl + l, r, 0))),
        out_shape=jax.ShapeDtypeStruct(acc.shape, F32), input_output_aliases={7: 0},
    )(jnp.reshape(c, (1,)).astype(jnp.int32), jnp.reshape(s0, (1,)).astype(jnp.int32), G, B, Bc, Bc, Bc, acc)


def reduce_scatter_finish(G, B, Bc, c, s0, acc, a, *, tag):
    acc = add_chips(G, B, Bc, c, s0, acc, a, name=f"rs_add2_{tag}")
    return join_halves(acc, a, G.shape[1] // 2, name=f"rs_join_{tag}")


def reduce_scatter_weight(G, c, s0, acc, a, *, tag):
    B = send_half_to_sibling(G, name=f"rs_sib_{tag}")
    A = add_halves(G, B, c, name=f"rs_add1_{tag}")
    Bc = scatter_to_chips(A, name=f"rs_chips_{tag}")
    return reduce_scatter_finish(G, B, Bc, c, s0, acc, a, tag=tag)


def adamw(w, g, m, v, *, name):
    shp = w.shape
    if len(shp) == 2:
        w, g, m, v = (a[None] for a in (w, g, m, v))
    nl, R, C = w.shape
    tr = R
    for t in (512, 384, 256, 128):
        if R % t == 0:
            tr = t
            break

    def body(w_ref, g_ref, m_ref, v_ref, d_ref, nm_ref, nv_ref):
        gv = g_ref[...]
        mn = ADAM_B1 * m_ref[...] + (1.0 - ADAM_B1) * gv
        vn = ADAM_B2 * v_ref[...] + (1.0 - ADAM_B2) * jnp.square(gv)
        m_hat = mn / (1.0 - ADAM_B1 ** ADAM_STEP)
        v_hat = vn / (1.0 - ADAM_B2 ** ADAM_STEP)
        d_ref[...] = -ADAM_LR * (m_hat / (jnp.sqrt(v_hat) + ADAM_EPS) + ADAM_WD * w_ref[...])
        nm_ref[...] = mn
        nv_ref[...] = vn

    spec = pl.BlockSpec((1, tr, C), lambda l, i: (l, i, 0))
    sds = jax.ShapeDtypeStruct((nl, R, C), F32)
    d, nm, nv = pl.pallas_call(body, name=name, grid=(nl, R // tr), in_specs=[spec] * 4, out_specs=(spec,) * 3,
                               out_shape=(sds,) * 3)(w, g, m, v)
    return d.reshape(shp), nm.reshape(shp), nv.reshape(shp)


_SMALL = ("ln_emb_g", "ln_emb_b", "q_norm", "kv_norm", "ln1_g", "ln1_b", "ln2_g", "ln2_b", "meta")


def _pack_small(d):
    flat = jnp.concatenate([d[k].reshape(-1) for k in _SMALL])
    rows = -(-flat.shape[0] // 128)
    rows = -(-rows // 8) * 8
    flat = jnp.concatenate([flat, jnp.zeros((rows * 128 - flat.shape[0],), F32)])
    return flat.reshape(rows, 128)


def _unpack_small(p, shapes):
    flat = p.reshape(-1)
    out, o = {}, 0
    for k in _SMALL:
        n = int(np.prod(shapes[k]))
        out[k] = flat[o:o + n].reshape(shapes[k])
        o += n
    return out


def kernel(x, meta_tokens, ln_emb_g, ln_emb_b, w_in, mla_q_norm, mla_kv_norm, w_uq, w_ukv, w_out, ln1_g, ln1_b, w_ff1, w_ff2, ln2_g, ln2_b, loss_target, m_meta_tokens, m_ln_emb_g, m_ln_emb_b, m_w_in, m_mla_q_norm, m_mla_kv_norm, m_w_uq, m_w_ukv, m_w_out, m_ln1_g, m_ln1_b, m_w_ff1, m_w_ff2, m_ln2_g, m_ln2_b, v_meta_tokens, v_ln_emb_g, v_ln_emb_b, v_w_in, v_mla_q_norm, v_mla_kv_norm, v_w_uq, v_w_ukv, v_w_out, v_ln1_g, v_ln1_b, v_w_ff1, v_w_ff2, v_ln2_g, v_ln2_b):
    xi, yi, ci = lax.axis_index("x"), lax.axis_index("y"), lax.axis_index("c")
    s0 = 2 * xi + yi
    nl = w_in.shape[0]

    big = {"w_in": w_in, "w_uq": w_uq, "w_ukv": w_ukv}
    late = {"w_out": w_out.astype(BF16), "w_ff1": w_ff1.astype(BF16), "w_ff2": w_ff2.astype(BF16), "s0": s0, "c": ci}
    full = {}
    for k, v in big.items():
        vb = v.astype(BF16)
        lo, late[k + "_hi"] = vb[:nl // 2], vb[nl // 2:]
        full[k] = lax.dynamic_update_slice(gather_weight(lo, name=f"ag_{k}"), lo[None], (s0, 0, 0, 0))
    k_w_in = _perm_w_in(_cols(full["w_in"]))[:, None]
    k_w_uq = _perm_w_uq(_cols(full["w_uq"]))[:, None]
    k_w_ukv = _perm_w_ukv(_cols(full["w_ukv"]))[:, None]
    meta_all = allgather8(meta_tokens, name="ag_meta", reduce=False)
    meta_full = jnp.concatenate([meta_all[32 * s:32 * s + N_META] for s in range(4)], axis=1)

    loss_part, grad_x, g = _device_step(x[0], loss_target[0], meta_full, ln_emb_g, ln_emb_b, k_w_in, mla_q_norm, mla_kv_norm,
                                        k_w_uq, k_w_ukv, None, ln1_g, ln1_b, None, None, ln2_g, ln2_b, late=late)
    loss = lax.psum(loss_part, ("x", "y", "c"))

    gw = {k: g[k] for k in _RS_SHAPES}

    small_shapes = {"ln_emb_g": (D_MODEL,), "ln_emb_b": (D_MODEL,), "q_norm": (nl, MLA_Q_LORA), "kv_norm": (nl, MLA_KV_LORA),
                    "ln1_g": (nl, D_MODEL), "ln1_b": (nl, D_MODEL), "ln2_g": (nl, D_MODEL), "ln2_b": (nl, D_MODEL),
                    "meta": (N_META, D_MODEL)}
    gs = _unpack_small(allgather8(_pack_small(g), name="ar_small", reduce=True), small_shapes)
    gw.update({"ln_emb_g": gs["ln_emb_g"], "ln_emb_b": gs["ln_emb_b"], "mla_q_norm": gs["q_norm"], "mla_kv_norm": gs["kv_norm"],
               "ln1_g": gs["ln1_g"], "ln1_b": gs["ln1_b"], "ln2_g": gs["ln2_g"], "ln2_b": gs["ln2_b"],
               "meta_tokens": lax.dynamic_slice_in_dim(gs["meta"], s0 * 256, 256, axis=1)})

    names = ["meta_tokens", "ln_emb_g", "ln_emb_b", "w_in", "mla_q_norm", "mla_kv_norm", "w_uq", "w_ukv", "w_out",
             "ln1_g", "ln1_b", "w_ff1", "w_ff2", "ln2_g", "ln2_b"]
    ws = [meta_tokens, ln_emb_g, ln_emb_b, w_in, mla_q_norm, mla_kv_norm, w_uq, w_ukv, w_out, ln1_g, ln1_b, w_ff1, w_ff2, ln2_g, ln2_b]
    ms = [m_meta_tokens, m_ln_emb_g, m_ln_emb_b, m_w_in, m_mla_q_norm, m_mla_kv_norm, m_w_uq, m_w_ukv, m_w_out, m_ln1_g, m_ln1_b, m_w_ff1, m_w_ff2, m_ln2_g, m_ln2_b]
    vs = [v_meta_tokens, v_ln_emb_g, v_ln_emb_b, v_w_in, v_mla_q_norm, v_mla_kv_norm, v_w_uq, v_w_ukv, v_w_out, v_ln1_g, v_ln1_b, v_w_ff1, v_w_ff2, v_ln2_g, v_ln2_b]
    deltas, new_m, new_v = [], [], []
    for n, w, m, v in zip(names, ws, ms, vs):
        w2 = w.reshape(1, -1) if w.ndim == 1 else w
        d, nm, nv = adamw(w2, gw[n].reshape(w2.shape), m.reshape(w2.shape), v.reshape(w2.shape), name=f"adamw_{n}")
        deltas.append(d.reshape(w.shape))
        new_m.append(nm.reshape(w.shape))
        new_v.append(nv.reshape(w.shape))
    grads_out = [gw[n].reshape(w.shape) for n, w in zip(names, ws)]
    return (loss, grad_x[None], *grads_out, *deltas, *new_m, *new_v)
```

```python
import functools
import math

import numpy as np
import jax
import jax.numpy as jnp
from jax import lax
from jax.experimental import pallas as pl
from jax.experimental.pallas import tpu as pltpu

F32 = jnp.float32
BF16 = jnp.bfloat16

D_MODEL = 1024
DEPTH = 4
N_META = 16
BLK = 128
N_PAD = 112
SB_HEADS = 8
MLA_HEADS = 8
MLA_NOPE = 64
MLA_ROPE = 32
MLA_V = 64
MLA_Q_LORA = 384
MLA_KV_LORA = 256
RET_HEADS = 4
RET_QK = 64
RET_V = 128
D_FF = 4 * D_MODEL
ROPE_THETA = 10000.0
LN_EPS = 1e-5
DN_ALPHA = (2 * DEPTH) ** 0.25
RET_GAMMA = tuple(1.0 - 2.0 ** (-5 - h) for h in range(RET_HEADS))
RET_LOG_G = tuple(float(np.log(np.float32(g))) for g in RET_GAMMA)
MLA_SCALE = (MLA_NOPE + MLA_ROPE) ** -0.5

ADAM_LR = 0.001
ADAM_B1 = 0.9
ADAM_B2 = 0.999
ADAM_EPS = 1e-08
ADAM_WD = 0.01
ADAM_STEP = 10

N_SB = 1536
C_SBQ, C_SBK, C_SBV = 0, 512, 1024
C_RQ, C_RK, C_RV, C_RG = 0, 256, 512, 1024
C_CQ, C_CKV, C_KR = 1536, 1920, 2176
N_IN = 3744
N_INP = 3840

NEG = -1e30


def _pick(n, cands):
    for t in cands:
        if n % t == 0:
            return t
    raise ValueError(f"no tile for {n} in {cands}")


def _row_tile(n):
    return _pick(n, (1056, 1024, 528, 512, 384, 256, 128))


def _dot(a, b):
    return jnp.dot(a, b, preferred_element_type=F32)


def _dot_nt(a, b):
    return lax.dot_general(a, b, (((1,), (1,)), ((), ())), preferred_element_type=F32)


def _dot_tn(a, b):
    return lax.dot_general(a, b, (((0,), (0,)), ((), ())), preferred_element_type=F32)


def mm_nn(a, b, *, tn, name, tk=None, prologue=None, axpy=None, out_dtype=F32):
    M, K = a.shape
    S, _, Ns = b.shape
    tm = _row_tile(M)
    tk = K if tk is None else tk
    npt = Ns // tn
    nk = K // tk
    alpha = None if axpy is None else axpy[1]

    def body(*refs):
        if axpy is None:
            a_ref, b_ref, o_ref, acc = refs
        else:
            a_ref, b_ref, e_ref, o_ref, acc = refs
        k = pl.program_id(2)

        @pl.when(k == 0)
        def _():
            acc[...] = jnp.zeros_like(acc)

        x = a_ref[...]
        if prologue == "relu2":
            x = jnp.square(jnp.maximum(x, 0.0))
        acc[...] += _dot(x.astype(BF16), b_ref[0])

        @pl.when(k == nk - 1)
        def _():
            r = acc[...]
            if axpy is not None:
                r = r + alpha * e_ref[...]
            o_ref[...] = r.astype(out_dtype)

    in_specs = [pl.BlockSpec((tm, tk), lambda i, j, k: (i, k)),
                pl.BlockSpec((1, tk, tn), lambda i, j, k: (j // npt, k, j % npt))]
    args = [a, b]
    if axpy is not None:
        in_specs.append(pl.BlockSpec((tm, tn), lambda i, j, k: (i, j)))
        args.append(axpy[0])
    return pl.pallas_call(
        body, name=name, grid=(M // tm, (S * Ns) // tn, nk), in_specs=in_specs,
        out_specs=pl.BlockSpec((tm, tn), lambda i, j, k: (i, j)),
        out_shape=jax.ShapeDtypeStruct((M, S * Ns), out_dtype),
        scratch_shapes=[pltpu.VMEM((tm, tn), F32)],
        compiler_params=pltpu.CompilerParams(dimension_semantics=("parallel", "parallel", "arbitrary")),
    )(*args)


def mm_nt(a, b, *, tn, tko, name, axpy=None, relu2grad=None, out_dtype=F32):
    M, N = a.shape
    S, K, Ns = b.shape
    tm = _row_tile(M)
    npt = Ns // tn
    nn = N // tn
    alpha = None if axpy is None else axpy[1]

    def body(*refs):
        if axpy is None and relu2grad is None:
            a_ref, b_ref, o_ref, acc = refs
        else:
            a_ref, b_ref, e_ref, o_ref, acc = refs
        n = pl.program_id(2)

        @pl.when(n == 0)
        def _():
            acc[...] = jnp.zeros_like(acc)

        acc[...] += _dot_nt(a_ref[...].astype(BF16), b_ref[0])

        @pl.when(n == nn - 1)
        def _():
            r = acc[...]
            if axpy is not None:
                r = r + alpha * e_ref[...]
            if relu2grad is not None:
                r = r * (2.0 * jnp.maximum(e_ref[...], 0.0))
            o_ref[...] = r.astype(out_dtype)

    in_specs = [pl.BlockSpec((tm, tn), lambda i, j, n: (i, n)),
                pl.BlockSpec((1, tko, tn), lambda i, j, n: (n // npt, j, n % npt))]
    args = [a, b]
    extra = axpy[0] if axpy is not None else relu2grad
    if extra is not None:
        in_specs.append(pl.BlockSpec((tm, tko), lambda i, j, n: (i, j)))
        args.append(extra)
    return pl.pallas_call(
        body, name=name, grid=(M // tm, K // tko, nn), in_specs=in_specs,
        out_specs=pl.BlockSpec((tm, tko), lambda i, j, n: (i, j)),
        out_shape=jax.ShapeDtypeStruct((M, K), out_dtype),
        scratch_shapes=[pltpu.VMEM((tm, tko), F32)],
        compiler_params=pltpu.CompilerParams(dimension_semantics=("parallel", "parallel", "arbitrary")),
    )(*args)


def mm_tn(a, g, *, shards, tko, tn, name, prologue=None, into=None):
    M, K = a.shape
    _, N = g.shape
    Ns = N // shards
    tm = _row_tile(M)
    npt = Ns // tn
    nm = M // tm

    def body(*refs):
        if into is None:
            a_ref, g_ref, o_ref, acc = refs
        else:
            a_ref, g_ref, _, o_ref, acc = refs
        m = pl.program_id(2)

        @pl.when(m == 0)
        def _():
            acc[...] = jnp.zeros_like(acc)

        x = a_ref[...]
        if prologue == "relu2":
            x = jnp.square(jnp.maximum(x, 0.0))
        acc[...] += _dot_tn(x.astype(BF16), g_ref[...].astype(BF16))

        @pl.when(m == nm - 1)
        def _():
            if into is None or into[2] == "layer":
                o_ref[0] = acc[...]
            else:
                o_ref[0, 0] = acc[...]

    in_specs = [pl.BlockSpec((tm, tko), lambda i, j, m: (m, i)),
                pl.BlockSpec((tm, tn), lambda i, j, m: (m, j))]
    scratch = [pltpu.VMEM((tko, tn), F32)]
    params = pltpu.CompilerParams(dimension_semantics=("parallel", "parallel", "arbitrary"))
    if into is None:
        return pl.pallas_call(
            body, name=name, grid=(K // tko, N // tn, nm), in_specs=in_specs,
            out_specs=pl.BlockSpec((1, tko, tn), lambda i, j, m: (j // npt, i, j % npt)),
            out_shape=jax.ShapeDtypeStruct((shards, K, Ns), F32), scratch_shapes=scratch, compiler_params=params,
        )(a, g)
    buf, layer, how = into
    if how == "layer":
        out_spec = pl.BlockSpec((1, tko, tn), lambda i, j, m: (layer, i, j))
    elif how == "cols":
        npt4 = (N // 4) // tn
        out_spec = pl.BlockSpec((1, 1, tko, tn), lambda i, j, m: (j // npt4, layer, i, j % npt4))
    else:
        kpt4 = (K // 4) // tko
        out_spec = pl.BlockSpec((1, 1, tko, tn), lambda i, j, m: (i // kpt4, layer, i % kpt4, j))
    return pl.pallas_call(
        body, name=name, grid=(K // tko, N // tn, nm), in_specs=in_specs + [pl.BlockSpec(memory_space=pl.ANY)],
        out_specs=out_spec, out_shape=jax.ShapeDtypeStruct(buf.shape, F32), input_output_aliases={2: 0},
        scratch_shapes=scratch, compiler_params=params,
    )(a, g, buf)


def _ln_stats(z):
    mu = jnp.mean(z, axis=-1, keepdims=True)
    zc = z - mu
    var = jnp.mean(jnp.square(zc), axis=-1, keepdims=True)
    r = lax.rsqrt(var + LN_EPS)
    return zc * r, r


def ln_fwd(x, g, b, *, name, res=None):
    L, Dm = x.shape
    tr = _row_tile(L)
    g2, b2 = g.reshape(1, Dm), b.reshape(1, Dm)

    def body(*refs):
        if res is None:
            x_ref, g_ref, b_ref, y_ref, yb_ref = refs
            z = x_ref[...]
        else:
            x_ref, r_ref, g_ref, b_ref, y_ref, yb_ref, z_ref = refs
            z = DN_ALPHA * r_ref[...] + x_ref[...]
            z_ref[...] = z
        xh, _ = _ln_stats(z)
        y = xh * g_ref[...] + b_ref[...]
        y_ref[...] = y
        yb_ref[...] = y.astype(BF16)

    row = pl.BlockSpec((tr, Dm), lambda i: (i, 0))
    vec = pl.BlockSpec((1, Dm), lambda i: (0, 0))
    sds = jax.ShapeDtypeStruct((L, Dm), F32)
    sdb = jax.ShapeDtypeStruct((L, Dm), BF16)
    if res is None:
        y, yb = pl.pallas_call(body, name=name, grid=(L // tr,), in_specs=[row, vec, vec], out_specs=(row, row),
                               out_shape=(sds, sdb))(x, g2, b2)
        return y, yb, x
    return pl.pallas_call(body, name=name, grid=(L // tr,), in_specs=[row, row, vec, vec], out_specs=(row, row, row),
                          out_shape=(sds, sdb, sds))(x, res, g2, b2)


def ln_bwd(dy, z, g, *, name):
    L, Dm = z.shape
    tr = _row_tile(L)

    def body(dy_ref, z_ref, g_ref, dz_ref, dg_ref, db_ref):
        @pl.when(pl.program_id(0) == 0)
        def _():
            dg_ref[...] = jnp.zeros_like(dg_ref)
            db_ref[...] = jnp.zeros_like(db_ref)

        dyv = dy_ref[...]
        xh, r = _ln_stats(z_ref[...])
        dxh = dyv * g_ref[...]
        m1 = jnp.mean(dxh, axis=-1, keepdims=True)
        m2 = jnp.mean(dxh * xh, axis=-1, keepdims=True)
        dz_ref[...] = r * (dxh - m1 - xh * m2)
        dg_ref[...] += jnp.sum(dyv * xh, axis=0, keepdims=True)
        db_ref[...] += jnp.sum(dyv, axis=0, keepdims=True)

    row = pl.BlockSpec((tr, Dm), lambda i: (i, 0))
    vec = pl.BlockSpec((1, Dm), lambda i: (0, 0))
    return pl.pallas_call(
        body, name=name, grid=(L // tr,), in_specs=[row, row, vec], out_specs=(row, vec, vec),
        out_shape=(jax.ShapeDtypeStruct((L, Dm), F32), jax.ShapeDtypeStruct((1, Dm), F32), jax.ShapeDtypeStruct((1, Dm), F32)),
        compiler_params=pltpu.CompilerParams(dimension_semantics=("arbitrary",)),
    )(dy, z, g.reshape(1, Dm))


def loss_fwd_bwd(h, target, *, name):
    L, Dm = h.shape
    nb = L // BLK

    def body(h_ref, t_ref, l_ref, dh_ref):
        i = pl.program_id(0)

        @pl.when(i == 0)
        def _():
            l_ref[...] = jnp.zeros_like(l_ref)
            dh_ref[...] = jnp.zeros_like(dh_ref)

        @pl.when(i > 0)
        def _():
            e = h_ref[...] - t_ref[...]
            dh_ref[...] = e * (1.0 / Dm)
            part = jnp.sum(jnp.sum(jnp.square(e), axis=-1, keepdims=True) * (1.0 / Dm), axis=0, keepdims=True)
            l_ref[...] += 0.5 * part

    return pl.pallas_call(
        body, name=name, grid=(nb,),
        in_specs=[pl.BlockSpec((BLK, Dm), lambda i: (i, 0)),
                  pl.BlockSpec((BLK, Dm), lambda i: (jnp.maximum(i - 1, 0), 0))],
        out_specs=(pl.BlockSpec((8, 128), lambda i: (0, 0)), pl.BlockSpec((BLK, Dm), lambda i: (i, 0))),
        out_shape=(jax.ShapeDtypeStruct((8, 128), F32), jax.ShapeDtypeStruct((L, Dm), F32)),
        compiler_params=pltpu.CompilerParams(dimension_semantics=("arbitrary",)),
    )(h, target)


def _swap_half(x, half):
    ax = x.ndim - 1
    n = x.shape[ax]
    lane = lax.broadcasted_iota(jnp.int32, x.shape, ax)
    up = pltpu.roll(x, n - half, ax)
    dn = pltpu.roll(x, half, ax)
    return jnp.where((lane % (2 * half)) < half, up, dn)


def _rope(x, cs, sn, half):
    return x * cs + _swap_half(x, half) * sn


def _rope_t(dy, cs, sn, half):
    return dy * cs + _swap_half(dy * sn, half)


def _rms(x):
    r = lax.rsqrt(jnp.mean(jnp.square(x), axis=-1, keepdims=True) + LN_EPS)
    return x * r, r


def mla_pre_fwd(P, gq, gkv, cs, sn, *, name):
    L = P.shape[0]
    tr = _row_tile(L)

    def body(p_ref, gq_ref, gkv_ref, cs_ref, sn_ref, nq_ref, nkv_ref, kr_ref):
        cq = p_ref[:, 0:MLA_Q_LORA]
        ckv = p_ref[:, MLA_Q_LORA:MLA_Q_LORA + MLA_KV_LORA]
        kr = p_ref[:, 640:768]
        nq_ref[...] = (_rms(cq)[0] * gq_ref[...]).astype(BF16)
        nkv_ref[...] = (_rms(ckv)[0] * gkv_ref[...]).astype(BF16)
        krr = _rope(kr, cs_ref[...], sn_ref[...], MLA_ROPE // 2)
        kr_ref[...] = (krr + pltpu.roll(krr, MLA_ROPE, 1)).astype(BF16)

    return pl.pallas_call(
        body, name=name, grid=(L // tr,),
        in_specs=[pl.BlockSpec((tr, 768), lambda i: (i, C_CQ // 768)),
                  pl.BlockSpec((1, MLA_Q_LORA), lambda i: (0, 0)), pl.BlockSpec((1, MLA_KV_LORA), lambda i: (0, 0)),
                  pl.BlockSpec((tr, 128), lambda i: (i, 0)), pl.BlockSpec((tr, 128), lambda i: (i, 0))],
        out_specs=(pl.BlockSpec((tr, MLA_Q_LORA), lambda i: (i, 0)), pl.BlockSpec((tr, MLA_KV_LORA), lambda i: (i, 0)),
                   pl.BlockSpec((tr, 128), lambda i: (i, 0))),
        out_shape=(jax.ShapeDtypeStruct((L, MLA_Q_LORA), BF16), jax.ShapeDtypeStruct((L, MLA_KV_LORA), BF16),
                   jax.ShapeDtypeStruct((L, 128), BF16)),
    )(P, gq.reshape(1, -1), gkv.reshape(1, -1), cs, sn)


def mla_pre_bwd(P, dnq, dnkv, dkr, gq, gkv, cs, sn, *, name):
    L = P.shape[0]
    tr = _row_tile(L)

    def body(p_ref, dnq_ref, dnkv_ref, dkr_ref, gq_ref, gkv_ref, cs_ref, sn_ref, dp_ref, dgq_ref, dgkv_ref):
        @pl.when(pl.program_id(0) == 0)
        def _():
            dgq_ref[...] = jnp.zeros_like(dgq_ref)
            dgkv_ref[...] = jnp.zeros_like(dgkv_ref)

        def rms_bwd(x, dy, g_ref, dg_ref):
            xn, r = _rms(x)
            dxn = dy * g_ref[...]
            dg_ref[...] += jnp.sum(dy * xn, axis=0, keepdims=True)
            return r * (dxn - xn * jnp.mean(dxn * xn, axis=-1, keepdims=True))

        dp_ref[:, 0:MLA_Q_LORA] = rms_bwd(p_ref[:, 0:MLA_Q_LORA], dnq_ref[...], gq_ref, dgq_ref)
        dp_ref[:, MLA_Q_LORA:640] = rms_bwd(p_ref[:, MLA_Q_LORA:640], dnkv_ref[...], gkv_ref, dgkv_ref)
        d2 = dkr_ref[...]
        lane = lax.broadcasted_iota(jnp.int32, d2.shape, 1)
        dkr = jnp.where(lane < MLA_ROPE, d2 + pltpu.roll(d2, 128 - MLA_ROPE, 1), 0.0)
        dp_ref[:, 640:768] = _rope_t(dkr, cs_ref[...], sn_ref[...], MLA_ROPE // 2)

    return pl.pallas_call(
        body, name=name, grid=(L // tr,),
        in_specs=[pl.BlockSpec((tr, 768), lambda i: (i, C_CQ // 768)),
                  pl.BlockSpec((tr, MLA_Q_LORA), lambda i: (i, 0)), pl.BlockSpec((tr, MLA_KV_LORA), lambda i: (i, 0)),
                  pl.BlockSpec((tr, 128), lambda i: (i, 0)),
                  pl.BlockSpec((1, MLA_Q_LORA), lambda i: (0, 0)), pl.BlockSpec((1, MLA_KV_LORA), lambda i: (0, 0)),
                  pl.BlockSpec((tr, 128), lambda i: (i, 0)), pl.BlockSpec((tr, 128), lambda i: (i, 0))],
        out_specs=(pl.BlockSpec((tr, 768), lambda i: (i, 0)), pl.BlockSpec((1, MLA_Q_LORA), lambda i: (0, 0)),
                   pl.BlockSpec((1, MLA_KV_LORA), lambda i: (0, 0))),
        out_shape=(jax.ShapeDtypeStruct((L, 768), F32), jax.ShapeDtypeStruct((1, MLA_Q_LORA), F32),
                   jax.ShapeDtypeStruct((1, MLA_KV_LORA), F32)),
        compiler_params=pltpu.CompilerParams(dimension_semantics=("arbitrary",)),
    )(P, dnq, dnkv, dkr, gq.reshape(1, -1), gkv.reshape(1, -1), cs, sn)


def _tq(L):
    return 384 if L % 384 == 0 else BLK


def _softplus(z):
    na = lax.bitcast_convert_type(lax.bitcast_convert_type(z, jnp.uint32) | jnp.uint32(0x80000000), F32)
    return jnp.maximum(z, 0.0) + jnp.log(1.0 + jnp.exp(na))


def _head_split(x, first):
    zero = jnp.zeros_like(x)
    return jnp.where(first, x, zero), jnp.where(first, zero, x)


def _tri2(kind, splits):
    r = lax.broadcasted_iota(jnp.int32, (256, 256), 0)
    c = lax.broadcasted_iota(jnp.int32, (256, 256), 1)
    same = (r < BLK) == (c < BLK)
    t = (same & ((r > c) if kind == "right" else (r < c))).astype(BF16)
    return jnp.concatenate([t] * splits, axis=0)


def _split2(x):
    hi = x.astype(BF16)
    lo = (x - hi.astype(F32)).astype(BF16)
    return jnp.concatenate([hi, lo], axis=1)


def _sb_mask2(I, j, tq):
    row = lax.broadcasted_iota(jnp.int32, (tq, 256), 0)
    col = lax.broadcasted_iota(jnp.int32, (tq, 256), 1)
    s_idx = j * BLK + (col & (BLK - 1))
    return (s_idx < I * tq + row) & (s_idx >= N_PAD)


def _per_head(x, r0, r1):
    return jnp.concatenate([x[:, 0:BLK] + r0, x[:, BLK:2 * BLK] + r1], axis=1)


def sb_fwd(P, *, name, gathers=()):
    L = P.shape[0]
    tq = _tq(L)
    nd = tq // BLK
    ng = len(gathers)
    nhp, nI = SB_HEADS // 2, L // tq

    def body(q_ref, k_ref, v_ref, *rest):
        w_refs, (o_ref, c_ref), g_refs, sems = rest[:ng], rest[ng:ng + 2], rest[ng + 2:2 * ng + 2], rest[2 * ng + 2:]
        hp = pl.program_id(0)
        I = pl.program_id(1)
        _gather_ride(w_refs, g_refs, sems, (hp == 0) & (I == 0), (hp == nhp - 1) & (I == 0), (hp == nhp - 1) & (I == nI - 1))
        tt = _tri2("right", 2)
        lane_q = lax.broadcasted_iota(jnp.int32, (tq, 128), 1)
        first_k = lax.broadcasted_iota(jnp.int32, (BLK, 128), 1) < 64
        q = (q_ref[...] * 0.125).astype(BF16)
        c_ref[...] = jnp.zeros_like(c_ref)

        def tiles(T, carry, kind):
            o, R0, R1 = carry
            js = [T * nd + nd - 1 - u for u in range(nd)]
            st = []
            for j in js:
                off = pl.multiple_of(j * BLK, BLK)
                kcat = jnp.concatenate(_head_split(k_ref[pl.ds(off, BLK), :].astype(BF16), first_k), axis=0)
                st.append([_dot_nt(q, kcat), off])
            for u, (s, j) in enumerate(zip(st, js)):
                sp = _softplus(s[0])
                mask = _sb_mask2(I, j, tq) if kind == "diag" else (pad_ok if kind == "first" and u == nd - 1 else None)
                spm = sp if mask is None else jnp.where(mask, sp, 0.0)
                s += [sp, spm, mask, _dot(_split2(spm), tt)]
            for (z, off, sp, spm, mask, S), j in zip(st, js):
                vcat = jnp.concatenate(_head_split(v_ref[pl.ds(off, BLK), :].astype(BF16), first_k), axis=0)
                w = jnp.exp(_per_head(z - sp - S, R0, R1))
                if mask is not None:
                    w = jnp.where(mask, w, 0.0)
                c_ref[0] = jnp.where(lane_q == j, R0, c_ref[0])
                c_ref[1] = jnp.where(lane_q == j, R1, c_ref[1])
                o = o + _dot(w.astype(BF16), vcat)
                R0 = R0 - (S[:, 0:1] + spm[:, 0:1])
                R1 = R1 - (S[:, BLK:BLK + 1] + spm[:, BLK:BLK + 1])
            return (o, R0, R1)

        pad_ok = (lax.broadcasted_iota(jnp.int32, (tq, 256), 1) & (BLK - 1)) >= N_PAD
        carry = (jnp.zeros((tq, 128), F32), jnp.zeros((tq, 1), F32), jnp.zeros((tq, 1), F32))
        carry = tiles(I, carry, "diag")
        carry = lax.fori_loop(0, jnp.maximum(I - 1, 0), lambda t, c: tiles(I - 1 - t, c, None), carry)
        carry = lax.fori_loop(0, jnp.minimum(I, 1), lambda t, c: tiles(0, c, "first"), carry)
        o_ref[...] = carry[0]

    g_in, g_out, g_shapes, g_sems = _gather_io(gathers)
    return pl.pallas_call(
        body, name=name, grid=(nhp, nI),
        in_specs=[pl.BlockSpec((tq, 128), lambda hp, i: (i, C_SBQ // 128 + hp)),
                  pl.BlockSpec((L, 128), lambda hp, i: (0, C_SBK // 128 + hp)),
                  pl.BlockSpec((L, 128), lambda hp, i: (0, C_SBV // 128 + hp))] + g_in,
        out_specs=[pl.BlockSpec((tq, 128), lambda hp, i: (i, hp)), pl.BlockSpec((2, tq, 128), lambda hp, i: (hp, i, 0))] + g_out,
        out_shape=[jax.ShapeDtypeStruct((L, 1536), F32), jax.ShapeDtypeStruct((SB_HEADS, L, 128), F32)] + g_shapes,
        scratch_shapes=g_sems,
        compiler_params=pltpu.CompilerParams(dimension_semantics=("arbitrary", "arbitrary")),
    )(P, P, P, *gathers)


def sb_bwd(P, carries, dmixed, *, name, rides=()):
    L = P.shape[0]
    tq = _tq(L)
    nd = tq // BLK

    def body(q_ref, k_ref, v_ref, c_ref, do_ref, dq_ref, dk_ref, dv_ref):
        I = pl.program_id(1)

        @pl.when(I == 0)
        def _():
            dk_ref[...] = jnp.zeros_like(dk_ref)
            dv_ref[...] = jnp.zeros_like(dv_ref)

        tr = _tri2("right", 2)
        tl = _tri2("left", 1)
        lane_q = lax.broadcasted_iota(jnp.int32, (tq, 128), 1)
        first_k = lax.broadcasted_iota(jnp.int32, (BLK, 128), 1) < 64
        q = (q_ref[...] * 0.125).astype(BF16)
        do = do_ref[...].astype(BF16)

        def tiles(T, carry, kind):
            dq, PL0, PL1 = carry
            js = [T * nd + u for u in range(nd)]
            st = []
            for j in js:
                off = pl.multiple_of(j * BLK, BLK)
                kcat = jnp.concatenate(_head_split(k_ref[pl.ds(off, BLK), :].astype(BF16), first_k), axis=0)
                vcat = jnp.concatenate(_head_split(v_ref[pl.ds(off, BLK), :].astype(BF16), first_k), axis=0)
                st.append([off, kcat, _dot_nt(q, kcat), _dot_nt(do, vcat)])
            for u, (s, j) in enumerate(zip(st, js)):
                z = s[2]
                sp = _softplus(z)
                mask = _sb_mask2(I, j, tq) if kind == "diag" else (pad_ok if kind == "first" and u == 0 else None)
                spm = sp if mask is None else jnp.where(mask, sp, 0.0)
                s += [mask, jnp.exp(z - sp), _dot(_split2(spm), tr)]
            for s, j in zip(st, js):
                off, kcat, z, dw, mask, sig, S = s
                R0 = jnp.sum(jnp.where(lane_q == j, c_ref[0], 0.0), axis=1, keepdims=True)
                R1 = jnp.sum(jnp.where(lane_q == j, c_ref[1], 0.0), axis=1, keepdims=True)
                w = sig * jnp.exp(_per_head(-S, R0, R1))
                if mask is not None:
                    w = jnp.where(mask, w, 0.0)
                dA = dw * w
                dvf = _dot_tn(w.astype(BF16), do)
                dv_ref[pl.ds(off, BLK), :] += jnp.where(first_k, dvf[0:BLK], dvf[BLK:2 * BLK])
                s += [dA, _dot(dA.astype(BF16), tl)]
            for off, kcat, z, dw, mask, sig, S, dA, pre in st:
                dz = dA - sig * (dA + _per_head(pre, PL0, PL1))
                if mask is not None:
                    dz = jnp.where(mask, dz, 0.0)
                dzb = dz.astype(BF16)
                dkf = _dot_tn(dzb, q)
                dk_ref[pl.ds(off, BLK), :] += jnp.where(first_k, dkf[0:BLK], dkf[BLK:2 * BLK])
                dq = dq + _dot(dzb, kcat)
                PL0 = PL0 + (pre[:, BLK - 1:BLK] + dA[:, BLK - 1:BLK])
                PL1 = PL1 + (pre[:, 2 * BLK - 1:2 * BLK] + dA[:, 2 * BLK - 1:2 * BLK])
            return (dq, PL0, PL1)

        pad_ok = (lax.broadcasted_iota(jnp.int32, (tq, 256), 1) & (BLK - 1)) >= N_PAD
        carry = (jnp.zeros((tq, 128), F32), jnp.zeros((tq, 1), F32), jnp.zeros((tq, 1), F32))
        carry = lax.fori_loop(0, jnp.minimum(I, 1), lambda t, c: tiles(0, c, "first"), carry)
        carry = lax.fori_loop(1, jnp.maximum(I, 1), lambda T, c: tiles(T, c, None), carry)
        carry = tiles(I, carry, "diag")
        dq_ref[...] = carry[0] * 0.125

    blk = lambda c0: pl.BlockSpec((tq, 128), lambda hp, i: (i, c0 + hp))
    full = lambda c0: pl.BlockSpec((L, 128), lambda hp, i: (0, c0 + hp))
    sds = jax.ShapeDtypeStruct((L, 512), F32)
    nhp, nI = SB_HEADS // 2, L // tq
    r_in, r_out, r_shapes, r_sems = _ride_io(rides)
    return pl.pallas_call(
        _ride_kernel(body, 5, 3, rides, nhp, nI), name=name, grid=(nhp, nI),
        in_specs=[blk(C_SBQ // 128), full(C_SBK // 128), full(C_SBV // 128),
                  pl.BlockSpec((2, tq, 128), lambda hp, i: (hp, i, 0)), blk(0)] + r_in,
        out_specs=[blk(0), full(0), full(0)] + r_out, out_shape=[sds, sds, sds] + r_shapes, scratch_shapes=r_sems,
        compiler_params=pltpu.CompilerParams(dimension_semantics=("arbitrary", "arbitrary")),
    )(P, P, P, carries, dmixed, *[a for _, a in rides])


def _mla_mask2(I, j, tq):
    row = lax.broadcasted_iota(jnp.int32, (tq, tq), 0)
    col = lax.broadcasted_iota(jnp.int32, (tq, tq), 1)
    t_idx = I * tq + row
    s_idx = j * tq + col
    return (s_idx <= t_idx) & ((s_idx >= N_PAD) | (s_idx == t_idx))


def _mla_qcat(q_ref, cs_ref, sn_ref, lane_q):
    qn = q_ref[:, 0:128]
    qr = _rope(q_ref[:, 128:256], cs_ref[...], sn_ref[...], MLA_ROPE // 2)
    zero = jnp.zeros_like(qn)
    r0 = lane_q < MLA_ROPE
    r1 = (lane_q >= MLA_ROPE) & (lane_q < 2 * MLA_ROPE)
    n0, n1 = _head_split(qn, lane_q < 64)
    return [jnp.concatenate([n0, jnp.where(r0, qr, zero)], axis=1).astype(BF16),
            jnp.concatenate([n1, jnp.where(r1, qr, zero)], axis=1).astype(BF16)]


def mla_fwd(Q, KV, KR, cs, sn, mixed, *, name, gathers=()):
    L = Q.shape[0]
    tq = _tq(L)

    def body(q_ref, kn_ref, v_ref, kr_ref, cs_ref, sn_ref, o_ref, lse_ref):
        I = pl.program_id(1)
        lane_q = lax.broadcasted_iota(jnp.int32, (tq, 128), 1)
        first_q = lane_q < 64
        qcat = _mla_qcat(q_ref, cs_ref, sn_ref, lane_q)

        def tile(j, carry, masked, wide=1):
            acc, ml = carry[0], carry[1:]
            off = pl.multiple_of(j * tq, tq)
            tk = wide * tq
            first_k = lax.broadcasted_iota(jnp.int32, (tk, 128), 1) < 64
            kcat = jnp.concatenate([kn_ref[pl.ds(off, tk), :], kr_ref[pl.ds(off, tk), :]], axis=1)
            vcat = jnp.concatenate(_head_split(v_ref[pl.ds(off, tk), :], first_k), axis=0)
            mask = _mla_mask2(I, j, tq) if masked else None
            ps, al, out = [], [], []
            for h in range(2):
                m, l = ml[2 * h], ml[2 * h + 1]
                s = _dot_nt(qcat[h], kcat) * MLA_SCALE
                if masked:
                    s = jnp.where(mask, s, NEG)
                m_new = jnp.maximum(m, jnp.max(s, axis=1, keepdims=True))
                a = jnp.exp(m - m_new)
                p = jnp.exp(s - m_new)
                ps.append(p.astype(BF16))
                al.append(a)
                out += [m_new, a * l + jnp.sum(p, axis=1, keepdims=True)]
            acc = acc * jnp.where(first_q, al[0], al[1]) + _dot(jnp.concatenate(ps, axis=1), vcat)
            return (acc,) + tuple(out)

        ml0 = (jnp.full((tq, 1), NEG, F32), jnp.zeros((tq, 1), F32))
        carry = (jnp.zeros((tq, 128), F32),) + ml0 + ml0
        carry = lax.fori_loop(0, jnp.minimum(I, 1), lambda t, c: tile(0, c, True), carry)
        n_in = jnp.maximum(I - 1, 0)
        carry = lax.fori_loop(0, n_in // 2, lambda t, c: tile(1 + 2 * t, c, False, 2), carry)
        carry = lax.fori_loop(0, n_in % 2, lambda t, c: tile(I - 1, c, False), carry)
        carry = tile(I, carry, True)
        acc, m0, l0, m1, l1 = carry
        o_ref[...] = acc / jnp.where(first_q, l0, l1)
        lse_ref[0] = jnp.where(lane_q == 0, m0 + jnp.log(l0), jnp.where(lane_q == 1, m1 + jnp.log(l1), 0.0))

    ng = len(gathers)
    nhp, nI = MLA_HEADS // 2, L // tq

    def kern(q, kn, v, kr, c, s, mixed_any, *rest):
        w_refs, (o, lse), g_refs, sems = rest[:ng], rest[ng:ng + 2], rest[ng + 2:2 * ng + 2], rest[2 * ng + 2:]
        hp, I = pl.program_id(0), pl.program_id(1)
        _gather_ride(w_refs, g_refs, sems, (hp == 0) & (I == 0), (hp == nhp - 1) & (I == 0), (hp == nhp - 1) & (I == nI - 1))
        body(q, kn, v, kr, c, s, o, lse)

    g_in, g_out, g_shapes, g_sems = _gather_io(gathers)
    return pl.pallas_call(
        kern, name=name, grid=(nhp, nI),
        in_specs=[pl.BlockSpec((tq, 256), lambda hp, i: (i, hp)),
                  pl.BlockSpec((L, 128), lambda hp, i: (0, hp)),
                  pl.BlockSpec((L, 128), lambda hp, i: (0, 4 + hp)),
                  pl.BlockSpec((L, 128), lambda hp, i: (0, 0)),
                  pl.BlockSpec((tq, 128), lambda hp, i: (i, 0)), pl.BlockSpec((tq, 128), lambda hp, i: (i, 0)),
                  pl.BlockSpec(memory_space=pl.ANY)] + g_in,
        out_specs=[pl.BlockSpec((tq, 128), lambda hp, i: (i, 4 + hp)), pl.BlockSpec((1, tq, 128), lambda hp, i: (hp, i, 0))] + g_out,
        out_shape=[jax.ShapeDtypeStruct(mixed.shape, F32), jax.ShapeDtypeStruct((4, L, 128), F32)] + g_shapes,
        input_output_aliases={6: 0}, scratch_shapes=g_sems,
        compiler_params=pltpu.CompilerParams(dimension_semantics=("arbitrary", "arbitrary")),
    )(Q, KV, KV, KR, cs, sn, mixed, *gathers)


def mla_bwd(Q, KV, KR, cs, sn, mixed, dmixed, lse, *, name, rides=()):
    L = Q.shape[0]
    tq = _tq(L)

    def body(q_ref, kn_ref, v_ref, kr_ref, cs_ref, sn_ref, o_ref, do_ref, lse_ref, dq_ref, dkn_ref, dv_ref, dkr_ref):
        hp = pl.program_id(0)
        I = pl.program_id(1)

        @pl.when(I == 0)
        def _():
            dkn_ref[...] = jnp.zeros_like(dkn_ref)
            dv_ref[...] = jnp.zeros_like(dv_ref)

        @pl.when((I == 0) & (hp == 0))
        def _():
            dkr_ref[...] = jnp.zeros_like(dkr_ref)

        lane_q = lax.broadcasted_iota(jnp.int32, (tq, 128), 1)
        first_q = lane_q < 64
        qcat = _mla_qcat(q_ref, cs_ref, sn_ref, lane_q)
        qq = jnp.concatenate(qcat, axis=0)
        do = do_ref[...]
        prod = do * o_ref[...]
        dd = [jnp.sum(jnp.where(first_q, prod, 0.0), axis=1, keepdims=True),
              jnp.sum(jnp.where(first_q, 0.0, prod), axis=1, keepdims=True)]
        dom = [x.astype(BF16) for x in _head_split(do, first_q)]
        docat = jnp.concatenate(dom, axis=0)
        lses = [lse_ref[0, :, 0:1], lse_ref[0, :, 1:2]]

        def tile(j, dq, masked, wide=1):
            off = pl.multiple_of(j * tq, tq)
            tk = wide * tq
            lane_k = lax.broadcasted_iota(jnp.int32, (tk, 256), 1)
            sel0 = (lane_k < 64) | ((lane_k >= 128) & (lane_k < 128 + MLA_ROPE))
            sel1 = ((lane_k >= 64) & (lane_k < 128)) | ((lane_k >= 128 + MLA_ROPE) & (lane_k < 128 + 2 * MLA_ROPE))
            kcat = jnp.concatenate([kn_ref[pl.ds(off, tk), :], kr_ref[pl.ds(off, tk), :]], axis=1)
            vb = v_ref[pl.ds(off, tk), :]
            zero = jnp.zeros_like(kcat)
            kk = jnp.concatenate([jnp.where(sel0, kcat, zero), jnp.where(sel1, kcat, zero)], axis=0)
            mask = _mla_mask2(I, j, tq) if masked else None
            dss, pbs = [], []
            for h in range(2):
                s = _dot_nt(qcat[h], kcat) * MLA_SCALE
                p = jnp.exp(s - lses[h])
                if masked:
                    p = jnp.where(mask, p, 0.0)
                dp = _dot_nt(dom[h], vb)
                dss.append((p * (dp - dd[h]) * MLA_SCALE).astype(BF16))
                pbs.append(p.astype(BF16))
            dkc = _dot_tn(jnp.concatenate(dss, axis=0), qq)
            dkn_ref[pl.ds(off, tk), :] += dkc[:, 0:128]
            dkr_ref[pl.ds(off, tk), :] += dkc[:, 128:256]
            dv_ref[pl.ds(off, tk), :] += _dot_tn(jnp.concatenate(pbs, axis=0), docat)
            return dq + _dot(jnp.concatenate(dss, axis=1), kk)

        dq = jnp.zeros((tq, 256), F32)
        dq = lax.fori_loop(0, jnp.minimum(I, 1), lambda t, c: tile(0, c, True), dq)
        n_in = jnp.maximum(I - 1, 0)
        dq = lax.fori_loop(0, n_in // 2, lambda t, c: tile(1 + 2 * t, c, False, 2), dq)
        dq = lax.fori_loop(0, n_in % 2, lambda t, c: tile(I - 1, c, False), dq)
        dq = tile(I, dq, True)
        dq_ref[:, 0:128] = dq[:, 0:128]
        dq_ref[:, 128:256] = _rope_t(dq[:, 128:256], cs_ref[...], sn_ref[...], MLA_ROPE // 2)

    blk = lambda c0: pl.BlockSpec((tq, 128), lambda hp, i: (i, c0 + hp))
    full = lambda c0: pl.BlockSpec((L, 128), lambda hp, i: (0, c0 + hp))
    tab = pl.BlockSpec((tq, 128), lambda hp, i: (i, 0))
    nhp, nI = MLA_HEADS // 2, L // tq
    r_in, r_out, r_shapes, r_sems = _ride_io(rides)
    return pl.pallas_call(
        _ride_kernel(body, 9, 4, rides, nhp, nI), name=name, grid=(nhp, nI),
        in_specs=[pl.BlockSpec((tq, 256), lambda hp, i: (i, hp)), full(0), full(4),
                  pl.BlockSpec((L, 128), lambda hp, i: (0, 0)), tab, tab, blk(4), blk(4),
                  pl.BlockSpec((1, tq, 128), lambda hp, i: (hp, i, 0))] + r_in,
        out_specs=[pl.BlockSpec((tq, 256), lambda hp, i: (i, hp)), full(0), full(0),
                   pl.BlockSpec((L, 128), lambda hp, i: (0, 0))] + r_out,
        out_shape=[jax.ShapeDtypeStruct((L, 1024), F32), jax.ShapeDtypeStruct((L, 512), F32),
                   jax.ShapeDtypeStruct((L, 512), F32), jax.ShapeDtypeStruct((L, 128), F32)] + r_shapes,
        scratch_shapes=r_sems,
        compiler_params=pltpu.CompilerParams(dimension_semantics=("arbitrary", "arbitrary")),
    )(Q, KV, KV, KR, cs, sn, mixed, dmixed, lse, *[a for _, a in rides])


def _ret_decay(h):
    lg = RET_LOG_G[h]
    r = lax.broadcasted_iota(jnp.int32, (BLK, BLK), 0)
    c = lax.broadcasted_iota(jnp.int32, (BLK, BLK), 1)
    diff = (r - c).astype(F32)
    d_in = jnp.where(diff >= 0, jnp.exp(jnp.maximum(diff, 0.0) * lg), 0.0)
    idx = lax.broadcasted_iota(jnp.int32, (BLK, 1), 0).astype(F32)
    q_decay = jnp.exp((idx + 1.0) * lg)
    k_decay = jnp.exp((BLK - 1.0 - idx) * lg)
    c_decay = math.exp(BLK * lg)
    return d_in, q_decay, k_decay, c_decay


def _ret_qk(qk_ref, cs_ref, sn_ref, n):
    cs = jnp.concatenate([cs_ref[...]] * 2, axis=1)
    sn = jnp.concatenate([sn_ref[...]] * 2, axis=1)
    rq = _rope(qk_ref[:, 0:256], cs, sn, RET_QK // 2)
    row = n * BLK + lax.broadcasted_iota(jnp.int32, (BLK, 256), 0)
    kmul = jnp.where(row >= N_PAD, 0.125, 0.0)
    rk = _rope(qk_ref[:, 256:512], cs, sn, RET_QK // 2) * kmul
    return rq, rk, cs, sn, kmul


def _head_norm(y):
    mu = jnp.mean(y, axis=-1, keepdims=True)
    yc = y - mu
    r = lax.rsqrt(jnp.mean(jnp.square(yc), axis=-1, keepdims=True) + LN_EPS)
    return yc * r, r


def ret_fwd(P, cs, sn, mixed, *, name):
    L = P.shape[0]
    nb = L // BLK

    def body(qk_ref, v_ref, g_ref, cs_ref, sn_ref, o_ref, y_ref, st_ref, state):
        n = pl.program_id(0)

        @pl.when(n == 0)
        def _():
            state[...] = jnp.zeros_like(state)

        st_ref[0] = state[...]
        rq, rk, _, _, _ = _ret_qk(qk_ref, cs_ref, sn_ref, n)
        outs, ys = [], []
        for h in range(RET_HEADS):
            d_in, q_decay, k_decay, c_decay = _ret_decay(h)
            q = rq[:, 64 * h:64 * h + 64].astype(BF16)
            kf = rk[:, 64 * h:64 * h + 64]
            v = v_ref[:, 128 * h:128 * h + 128].astype(BF16)
            S = state[h]
            inner = _dot_nt(q, kf.astype(BF16)) * d_in
            y = _dot(inner.astype(BF16), v) + _dot(q, S.astype(BF16)) * q_decay
            state[h] = S * c_decay + _dot_tn((kf * k_decay).astype(BF16), v)
            g = g_ref[:, 128 * h:128 * h + 128]
            ys.append(y)
            outs.append(g * jax.nn.sigmoid(g) * _head_norm(y)[0])
        o_ref[...] = jnp.concatenate(outs, axis=1)
        y_ref[...] = jnp.concatenate(ys, axis=1)

    blk512 = lambda c: pl.BlockSpec((BLK, 512), lambda n: (n, c))
    tab = pl.BlockSpec((BLK, 128), lambda n: (n, 0))
    return pl.pallas_call(
        lambda qk, v, g, c, s, mixed_any, o, y, st, state: body(qk, v, g, c, s, o, y, st, state),
        name=name, grid=(nb,),
        in_specs=[blk512(C_RQ // 512), blk512(C_RV // 512), blk512(C_RG // 512), tab, tab, pl.BlockSpec(memory_space=pl.ANY)],
        out_specs=(blk512(2), blk512(0), pl.BlockSpec((1, RET_HEADS, RET_QK, RET_V), lambda n: (n, 0, 0, 0))),
        out_shape=(jax.ShapeDtypeStruct(mixed.shape, F32), jax.ShapeDtypeStruct((L, 512), F32),
                   jax.ShapeDtypeStruct((nb, RET_HEADS, RET_QK, RET_V), F32)),
        input_output_aliases={5: 0},
        scratch_shapes=[pltpu.VMEM((RET_HEADS, RET_QK, RET_V), F32)],
        compiler_params=pltpu.CompilerParams(dimension_semantics=("arbitrary",)),
    )(P, P, P, cs, sn, mixed)


def ret_bwd(P, y, states, dmixed, cs, sn, *, name):
    L = P.shape[0]
    nb = L // BLK

    def body(qk_ref, v_ref, g_ref, y_ref, st_ref, do_ref, cs_ref, sn_ref, dqk_ref, dv_ref, dg_ref, dstate):
        n = nb - 1 - pl.program_id(0)

        @pl.when(pl.program_id(0) == 0)
        def _():
            dstate[...] = jnp.zeros_like(dstate)

        rq, rk, cs, sn, kmul = _ret_qk(qk_ref, cs_ref, sn_ref, n)
        dqs, dks, dvs, dgs = [], [], [], []
        for h in range(RET_HEADS):
            d_in, q_decay, k_decay, c_decay = _ret_decay(h)
            sv = slice(128 * h, 128 * h + 128)
            q = rq[:, 64 * h:64 * h + 64].astype(BF16)
            kf = rk[:, 64 * h:64 * h + 64]
            k = kf.astype(BF16)
            kd = (kf * k_decay).astype(BF16)
            v = v_ref[:, sv].astype(BF16)
            g = g_ref[:, sv]
            do = do_ref[:, sv]
            yh = y_ref[:, sv]
            S = st_ref[0, h].astype(BF16)
            dS = dstate[h]
            sg = jax.nn.sigmoid(g)
            yn, r = _head_norm(yh)
            dgs.append(do * yn * (sg * (1.0 + g * (1.0 - sg))))
            dyn = do * (g * sg)
            dy = r * (dyn - jnp.mean(dyn, axis=-1, keepdims=True) - yn * jnp.mean(dyn * yn, axis=-1, keepdims=True))
            dyb = dy.astype(BF16)
            dyq = (dy * q_decay).astype(BF16)
            inner = (_dot_nt(q, k) * d_in).astype(BF16)
            A = (_dot_nt(dyb, v) * d_in).astype(BF16)
            dSb = dS.astype(BF16)
            dqs.append(_dot(A, k) + _dot_nt(dyq, S))
            dks.append(_dot_tn(A, q) + _dot_nt(v, dSb) * k_decay)
            dvs.append(_dot_tn(inner, dyb) + _dot(kd, dSb))
            dstate[h] = dS * c_decay + _dot_tn(q, dyq)
        drq = _rope_t(jnp.concatenate(dqs, axis=1), cs, sn, RET_QK // 2)
        drk = _rope_t(jnp.concatenate(dks, axis=1) * kmul, cs, sn, RET_QK // 2)
        dqk_ref[...] = jnp.concatenate([drq, drk], axis=1)
        dv_ref[...] = jnp.concatenate(dvs, axis=1)
        dg_ref[...] = jnp.concatenate(dgs, axis=1)

    blk512 = lambda c: pl.BlockSpec((BLK, 512), lambda t: (nb - 1 - t, c))
    tab = pl.BlockSpec((BLK, 128), lambda t: (nb - 1 - t, 0))
    sds = jax.ShapeDtypeStruct((L, 512), F32)
    return pl.pallas_call(
        body, name=name, grid=(nb,),
        in_specs=[blk512(C_RQ // 512), blk512(C_RV // 512), blk512(C_RG // 512), blk512(0),
                  pl.BlockSpec((1, RET_HEADS, RET_QK, RET_V), lambda t: (nb - 1 - t, 0, 0, 0)), blk512(2), tab, tab],
        out_specs=(blk512(0), blk512(0), blk512(0)), out_shape=(sds, sds, sds),
        scratch_shapes=[pltpu.VMEM((RET_HEADS, RET_QK, RET_V), F32)],
        compiler_params=pltpu.CompilerParams(dimension_semantics=("arbitrary",)),
    )(P, P, P, y, states, dmixed, cs, sn)


def _perm_w_in(w):
    pad = jnp.zeros(w.shape[:-1] + (N_INP - N_IN,), w.dtype)
    return jnp.concatenate([w[..., 0:1536], w[..., 2208:3744], w[..., 1536:2208], pad], axis=-1)


def _unperm_w_in(g):
    return jnp.concatenate([g[..., 0:1536], g[..., 3072:3744], g[..., 1536:3072]], axis=-1)


def _perm_w_uq(w):
    lead = w.shape[:-1]
    w5 = w.reshape(lead + (4, 2, 96))
    nope = w5[..., :64].reshape(lead + (4, 128))
    rope = w5[..., 64:].reshape(lead + (4, 64))
    return jnp.concatenate([nope, rope, jnp.zeros(lead + (4, 64), w.dtype)], axis=-1).reshape(lead + (1024,))


def _unperm_w_uq(g):
    lead = g.shape[:-1]
    g4 = g.reshape(lead + (4, 256))
    nope = g4[..., :128].reshape(lead + (4, 2, 64))
    rope = g4[..., 128:192].reshape(lead + (4, 2, 32))
    return jnp.concatenate([nope, rope], axis=-1).reshape(lead + (768,))


def _perm_w_ukv(w):
    lead = w.shape[:-1]
    w4 = w.reshape(lead + (8, 128))
    return jnp.concatenate([w4[..., :64].reshape(lead + (512,)), w4[..., 64:].reshape(lead + (512,))], axis=-1)


def _unperm_w_ukv(g):
    lead = g.shape[:-1]
    return jnp.concatenate([g[..., :512].reshape(lead + (8, 64)), g[..., 512:].reshape(lead + (8, 64))],
                           axis=-1).reshape(lead + (1024,))


def _cols(a):
    return jnp.moveaxis(a, 0, 2).reshape(a.shape[1], a.shape[2], 4 * a.shape[3])


def _col_shards(a):
    return jnp.moveaxis(a.reshape(a.shape[0], a.shape[1], 4, a.shape[2] // 4), 2, 0)


_RS_SHAPES = {"w_in": (D_MODEL, N_IN // 4), "w_uq": (MLA_Q_LORA, 192), "w_ukv": (MLA_KV_LORA, 256),
              "w_out": (384, D_MODEL), "w_ff1": (D_MODEL, D_FF // 4), "w_ff2": (D_FF // 4, D_MODEL)}


def _rope_tables(L, half):
    pos = (jnp.arange(L) - N_PAD).astype(F32)
    inv = ROPE_THETA ** (-jnp.arange(half, dtype=F32) / half)
    ang = pos[:, None] * inv[None, :]
    cos, sin = jnp.cos(ang), jnp.sin(ang)
    reps = 128 // (2 * half)
    cs = jnp.tile(jnp.concatenate([cos, cos], axis=1), (1, reps))
    sn = jnp.tile(jnp.concatenate([-sin, sin], axis=1), (1, reps))
    return cs, sn


def _device_step(x, target, meta, ln_emb_g, ln_emb_b, w_in, q_norm, kv_norm, w_uq, w_ukv, w_out,
                 ln1_g, ln1_b, w_ff1, w_ff2, ln2_g, ln2_b, late=None):
    S = x.shape[0]
    L = S + BLK
    depth = ln1_g.shape[0]
    w_in, w_uq, w_ukv = list(w_in), list(w_uq), list(w_ukv)
    cs_m, sn_m = _rope_tables(L, MLA_ROPE // 2)
    cs_r, sn_r = _rope_tables(L, RET_QK // 2)
    hcat = jnp.concatenate([jnp.zeros((N_PAD, D_MODEL), F32), meta, x], axis=0)
    h, hb, _ = ln_fwd(hcat, ln_emb_g, ln_emb_b, name="ln_emb_fwd")

    def own(gathered, shard):
        return lax.dynamic_update_slice(gathered, shard[None], (late["s0"], 0, 0, 0))

    def own_slot(gathered, shard):
        return jnp.moveaxis(own(gathered, shard), 0, 1)

    saved = []
    for l in range(depth):
        Psb = mm_nn(hb, w_in[l][..., :N_SB], tn=N_SB, name=f"in_proj_sb_{l}", out_dtype=BF16)
        P = mm_nn(hb, w_in[l][..., N_SB:], tn=768, name=f"in_proj_{l}")
        if late is not None and l == 0:
            mixed, sbc, ga, gb = sb_fwd(Psb, name=f"sb_fwd_{l}", gathers=(late["w_out"], late["w_ff1"]))
            w_out, w_ff1 = own_slot(ga, late["w_out"]), own_slot(gb, late["w_ff1"])
        elif late is not None and l == 1:
            his = [late["w_in_hi"], late["w_uq_hi"], late["w_ukv_hi"]]
            mixed, sbc, *gs = sb_fwd(Psb, name=f"sb_fwd_{l}", gathers=tuple(his))
            gs = [_cols(own(g, s)) for g, s in zip(gs, his)]
            w_in += list(_perm_w_in(gs[0])[:, None])
            w_uq += list(_perm_w_uq(gs[1])[:, None])
            w_ukv += list(_perm_w_ukv(gs[2])[:, None])
        else:
            mixed, sbc = sb_fwd(Psb, name=f"sb_fwd_{l}")
        nq, nkv, KR = mla_pre_fwd(P, q_norm[l], kv_norm[l], cs_m, sn_m, name=f"mla_pre_fwd_{l}")
        Q = mm_nn(nq, w_uq[l], tn=512, name=f"uq_{l}")
        KV = mm_nn(nkv, w_ukv[l], tn=512, name=f"ukv_{l}", out_dtype=BF16)
        if late is not None and l == 0:
            mixed, lse, ga = mla_fwd(Q, KV, KR, cs_m, sn_m, mixed, name=f"mla_fwd_{l}", gathers=(late["w_ff2"],))
            w_ff2 = own_slot(ga, late["w_ff2"])
        else:
            mixed, lse = mla_fwd(Q, KV, KR, cs_m, sn_m, mixed, name=f"mla_fwd_{l}")
        mixed, y, states = ret_fwd(P, cs_r, sn_r, mixed, name=f"ret_fwd_{l}")
        w_out_l = w_out[l].reshape(1, 1536, D_MODEL)
        mix = mm_nn(mixed, w_out_l, tn=1024, name=f"out_proj_{l}")
        h1, h1b, z1 = ln_fwd(mix, ln1_g[l], ln1_b[l], res=h, name=f"ln1_fwd_{l}")
        U = mm_nn(h1b, w_ff1[l], tn=1024, name=f"ff1_{l}")
        w_ff2_l = w_ff2[l].reshape(1, D_FF, D_MODEL)
        mlp = mm_nn(U, w_ff2_l, tn=1024, tk=2048, prologue="relu2", name=f"ff2_{l}")
        h2, h2b, z2 = ln_fwd(mlp, ln2_g[l], ln2_b[l], res=h1, name=f"ln2_fwd_{l}")
        saved.append((hb, Psb, P, sbc, nq, nkv, KR, Q, KV, lse, y, states, mixed, z1, h1b, U, z2))
        h, hb = h2, h2b

    loss_t, dh = loss_fwd_bwd(h, target, name="loss")

    grads = {k: [None] * depth for k in ("q_norm", "kv_norm", "ln1_g", "ln1_b", "ln2_g", "ln2_b")}
    pairs = depth // 2
    g_ff1 = [lax.empty((4, 2, D_MODEL, D_FF // 4), F32) for _ in range(pairs)]
    g_ff2 = [lax.empty((4, 2, D_FF // 4, D_MODEL), F32) for _ in range(pairs)]
    g_out = [lax.empty((4, 2, 384, D_MODEL), F32) for _ in range(pairs)]
    g_in = [lax.empty((2, D_MODEL, N_INP), F32) for _ in range(pairs)]
    g_uq = [lax.empty((2, MLA_Q_LORA, 1024), F32) for _ in range(pairs)]
    g_ukv = [lax.empty((2, MLA_KV_LORA, 1024), F32) for _ in range(pairs)]

    def pair_grads(p):
        return {"w_in": _col_shards(_unperm_w_in(g_in[p])), "w_uq": _col_shards(_unperm_w_uq(g_uq[p])),
                "w_ukv": _col_shards(_unperm_w_ukv(g_ukv[p])), "w_out": g_out[p], "w_ff1": g_ff1[p], "w_ff2": g_ff2[p]}

    riding = {1: (1, tuple(_RS_SHAPES)), 0: (0, ("w_out", "w_ff1", "w_ff2"))} if late is not None and depth == 4 else {}
    done = set()
    if late is not None:
        acc = {k: lax.empty((depth,) + s, F32) for k, s in _RS_SHAPES.items()}
    for l in reversed(range(depth)):
        p, lp = l // 2, l % 2
        hb_in, Psb, P, sbc, nq, nkv, KR, Q, KV, lse, y, states, mixed, z1, h1b, U, z2 = saved[l]
        dz2, grads["ln2_g"][l], grads["ln2_b"][l] = ln_bwd(dh, z2, ln2_g[l], name=f"ln2_bwd_{l}")
        w_ff2_l = w_ff2[l].reshape(1, D_FF, D_MODEL)
        g_ff2[p] = mm_tn(U, dz2, shards=1, tko=1024, tn=1024, prologue="relu2", name=f"ff2_dw_{l}", into=(g_ff2[p], lp, "rows"))
        dU = mm_nt(dz2, w_ff2_l, tn=1024, tko=1024, relu2grad=U, name=f"ff2_dx_{l}", out_dtype=BF16)
        g_ff1[p] = mm_tn(h1b, dU, shards=4, tko=1024, tn=1024, name=f"ff1_dw_{l}", into=(g_ff1[p], lp, "cols"))
        dh1 = mm_nt(dU, w_ff1[l], tn=1024, tko=1024, axpy=(dz2, DN_ALPHA), name=f"ff1_dx_{l}")
        dz1, grads["ln1_g"][l], grads["ln1_b"][l] = ln_bwd(dh1, z1, ln1_g[l], name=f"ln1_bwd_{l}")
        w_out_l = w_out[l].reshape(1, 1536, D_MODEL)
        g_out[p] = mm_tn(mixed, dz1, shards=1, tko=384, tn=1024, name=f"out_dw_{l}", into=(g_out[p], lp, "rows"))
        dmixed = mm_nt(dz1, w_out_l, tn=1024, tko=1536, name=f"out_dx_{l}")
        d_rqk, d_rv, d_rg = ret_bwd(P, y, states, dmixed, cs_r, sn_r, name=f"ret_bwd_{l}")
        if l in riding:
            rp, rkeys = riding[l]
            GA = pair_grads(rp)
            dQ, dKN, dV, dKR, *Bs = mla_bwd(Q, KV, KR, cs_m, sn_m, mixed, dmixed, lse, name=f"mla_bwd_{l}",
                                            rides=[("sib", GA[k]) for k in rkeys])
        else:
            dQ, dKN, dV, dKR = mla_bwd(Q, KV, KR, cs_m, sn_m, mixed, dmixed, lse, name=f"mla_bwd_{l}")
        dKV = jnp.concatenate([dKN, dV], axis=1)
        g_uq[p] = mm_tn(nq, dQ, shards=1, tko=MLA_Q_LORA, tn=512, name=f"uq_dw_{l}", into=(g_uq[p], lp, "layer"))
        g_ukv[p] = mm_tn(nkv, dKV, shards=1, tko=MLA_KV_LORA, tn=512, name=f"ukv_dw_{l}", into=(g_ukv[p], lp, "layer"))
        dnq = mm_nt(dQ, w_uq[l], tn=1024, tko=MLA_Q_LORA, name=f"uq_dx_{l}")
        dnkv = mm_nt(dKV, w_ukv[l], tn=1024, tko=MLA_KV_LORA, name=f"ukv_dx_{l}")
        d_lat, grads["q_norm"][l], grads["kv_norm"][l] = mla_pre_bwd(P, dnq, dnkv, dKR, q_norm[l], kv_norm[l], cs_m, sn_m,
                                                                     name=f"mla_pre_bwd_{l}")
        if l in riding:
            As = [add_halves(GA[k], B, late["c"], name=f"rs_add1_r{rp}_{k}") for k, B in zip(rkeys, Bs)]
            dq_sb, dk_sb, dv_sb, *Bcs = sb_bwd(Psb, sbc, dmixed, name=f"sb_bwd_{l}", rides=[("chips", A) for A in As])
            for k, B, Bc in zip(rkeys, Bs, Bcs):
                acc[k] = reduce_scatter_finish(GA[k], B, Bc, late["c"], late["s0"], acc[k], 2 * rp, tag=f"r{rp}_{k}")
                done.add((rp, k))
        else:
            dq_sb, dk_sb, dv_sb = sb_bwd(Psb, sbc, dmixed, name=f"sb_bwd_{l}")
        dP = jnp.concatenate([dq_sb, dk_sb, dv_sb, d_rqk, d_rv, d_rg, d_lat], axis=1).astype(BF16)
        g_in[p] = mm_tn(hb_in, dP, shards=1, tko=1024, tn=1280, name=f"in_dw_{l}", into=(g_in[p], lp, "layer"))
        dh = mm_nt(dP, w_in[l], tn=1920, tko=1024, axpy=(dz1, DN_ALPHA), name=f"in_dx_{l}")

    dhcat, dg_emb, db_emb = ln_bwd(dh, hcat, ln_emb_g, name="ln_emb_bwd")
    out = {k: jnp.stack(v) for k, v in grads.items()}
    if late is None:
        out.update({k: jnp.concatenate([pair_grads(p)[k] for p in range(pairs)], axis=1) for k in _RS_SHAPES})
    else:
        for p in range(pairs):
            for k, G in pair_grads(p).items():
                if (p, k) not in done:
                    acc[k] = reduce_scatter_weight(G, late["c"], late["s0"], acc[k], 2 * p, tag=f"{p}_{k}")
        out.update(acc)
    out["ln_emb_g"], out["ln_emb_b"] = dg_emb, db_emb
    out["meta"] = dhcat[N_PAD:BLK]
    return loss_t[0, 0], dhcat[BLK:], out


MESH = pl.DeviceIdType.MESH
PEER_XOR = (2, 1, 3)
_HBM = pl.BlockSpec(memory_space=pltpu.HBM)


def _place():
    x, y, c = lax.axis_index("x"), lax.axis_index("y"), lax.axis_index("c")
    peers = [(1 - x, y, c), (x, 1 - y, c), (1 - x, 1 - y, c)]
    return x, y, c, 2 * x + y, peers, (x, y, 1 - c)


def _gather_plan(w_ref, out_ref, send_sems, recv_sems, base):
    x, y, c, s0, peers, sibling = _place()
    hl = w_ref.shape[0] // 2

    def piece(s, half):
        return out_ref.at[s, pl.ds(half * hl, hl)]

    def copy(k, s, half, to, src=None):
        return pltpu.make_async_remote_copy(src_ref=piece(s, half) if src is None else src, dst_ref=piece(s, half),
                                            send_sem=send_sems.at[base + k], recv_sem=recv_sems.at[base + k],
                                            device_id=to, device_id_type=MESH)

    def first():
        return [copy(k, s0, c, peers[k], src=w_ref.at[pl.ds(c * hl, hl)]) for k in range(3)]

    def passed():
        return [copy(3 + k, s0 ^ PEER_XOR[k], c, sibling) for k in range(3)]

    def start():
        for cp in first():
            cp.start()

    def forward():
        for k, cp in enumerate(passed()):
            copy(k, s0 ^ PEER_XOR[k], c, peers[k]).wait_recv()
            cp.start()

    def finish():
        for k in range(3):
            copy(3 + k, s0 ^ PEER_XOR[k], 1 - c, sibling).wait_recv()
        for cp in first() + passed():
            cp.wait_send()

    return start, forward, finish


def _gather_io(gathers):
    n = len(gathers)
    return ([_HBM] * n, [_HBM] * n, [jax.ShapeDtypeStruct((4,) + w.shape, w.dtype) for w in gathers],
            [pltpu.SemaphoreType.DMA((6 * n,)), pltpu.SemaphoreType.DMA((6 * n,))] if n else [])


def _gather_ride(w_refs, g_refs, sems, at_start, at_forward, at_finish):
    if not w_refs:
        return
    plans = [_gather_plan(w, g, sems[0], sems[1], 6 * n) for n, (w, g) in enumerate(zip(w_refs, g_refs))]
    for step, cond in enumerate((at_start, at_forward, at_finish)):
        @pl.when(cond)
        def _():
            for p in plans:
                p[step]()


def _ride_copies(kind, src_ref, dst_ref, send_sems, recv_sems, base):
    x, y, c, s0, peers, sibling = _place()
    if kind == "sib":
        hl = src_ref.shape[1] // 2
        return [pltpu.make_async_remote_copy(src_ref=src_ref.at[:, pl.ds((1 - c) * hl, hl)], dst_ref=dst_ref,
                                             send_sem=send_sems.at[base], recv_sem=recv_sems.at[base],
                                             device_id=sibling, device_id_type=MESH)]
    return [pltpu.make_async_remote_copy(src_ref=src_ref.at[s0 ^ PEER_XOR[k]], dst_ref=dst_ref.at[k],
                                         send_sem=send_sems.at[base + k], recv_sem=recv_sems.at[base + k],
                                         device_id=peers[k], device_id_type=MESH) for k in range(3)]


def _ride_io(rides):
    shapes, nsem = [], 0
    for kind, a in rides:
        shapes.append(jax.ShapeDtypeStruct(((4, a.shape[1] // 2) if kind == "sib" else (3, a.shape[1])) + a.shape[2:], a.dtype))
        nsem += 1 if kind == "sib" else 3
    sems = [pltpu.SemaphoreType.DMA((nsem,)), pltpu.SemaphoreType.DMA((nsem,))] if rides else []
    return [_HBM] * len(rides), [_HBM] * len(rides), shapes, sems


def _ride_kernel(body, n_in, n_out, rides, nhp, nI):
    nr = len(rides)

    def kern(*refs):
        ins, r_in = refs[:n_in], refs[n_in:n_in + nr]
        outs, r_out = refs[n_in + nr:n_in + nr + n_out], refs[n_in + nr + n_out:n_in + 2 * nr + n_out]
        sems = refs[n_in + 2 * nr + n_out:]
        if nr:
            hp, I = pl.program_id(0), pl.program_id(1)

            def copies():
                cps = []
                for (kind, _), s, d in zip(rides, r_in, r_out):
                    cps += _ride_copies(kind, s, d, sems[0], sems[1], len(cps))
                return cps

            @pl.when((hp == 0) & (I == 0))
            def _():
                for cp in copies():
                    cp.start()

            @pl.when((hp == nhp - 1) & (I == nI - 1))
            def _():
                for cp in copies():
                    cp.wait()

        body(*ins, *outs)

    return kern


def gather_weights(shards, *, name):
    n = len(shards)
    g_in, g_out, g_shapes, g_sems = _gather_io(shards)

    def body(*refs):
        plans = [_gather_plan(w, g, refs[2 * n], refs[2 * n + 1], 6 * i) for i, (w, g) in enumerate(zip(refs[:n], refs[n:2 * n]))]
        for step in range(3):
            for p in plans:
                p[step]()

    return pl.pallas_call(body, name=name, in_specs=g_in, out_specs=g_out, out_shape=g_shapes, scratch_shapes=g_sems)(*shards)


def gather_weight(w_shard, *, name):
    nl = w_shard.shape[0]
    hl = nl // 2

    def body(w_ref, out_ref, send_sems, recv_sems):
        x, y, c, s0, peers, sibling = _place()

        def piece(s, half):
            return out_ref.at[s, pl.ds(half * hl, hl)]

        def copy(k, s, half, to, src=None):
            return pltpu.make_async_remote_copy(src_ref=piece(s, half) if src is None else src, dst_ref=piece(s, half),
                                                send_sem=send_sems.at[k], recv_sem=recv_sems.at[k],
                                                device_id=to, device_id_type=MESH)

        first = [copy(k, s0, c, peers[k], src=w_ref.at[pl.ds(c * hl, hl)]) for k in range(3)]
        for cp in first:
            cp.start()
        passed = [copy(3 + k, s0 ^ PEER_XOR[k], c, sibling) for k in range(3)]
        for k in range(3):
            copy(k, s0 ^ PEER_XOR[k], c, peers[k]).wait_recv()
            passed[k].start()
        for k in range(3):
            copy(3 + k, s0 ^ PEER_XOR[k], 1 - c, sibling).wait_recv()
        for cp in first + passed:
            cp.wait_send()

    return pl.pallas_call(
        body, name=name, in_specs=[_HBM], out_specs=_HBM,
        out_shape=jax.ShapeDtypeStruct((4,) + w_shard.shape, w_shard.dtype),
        scratch_shapes=[pltpu.SemaphoreType.DMA((6,)), pltpu.SemaphoreType.DMA((6,))],
    )(w_shard)


def send_half_to_sibling(G, *, name):
    hl = G.shape[1] // 2

    def body(g_ref, out_ref, send_sem, recv_sem):
        x, y, c, s0, peers, sibling = _place()
        cp = pltpu.make_async_remote_copy(src_ref=g_ref.at[:, pl.ds((1 - c) * hl, hl)], dst_ref=out_ref,
                                          send_sem=send_sem, recv_sem=recv_sem, device_id=sibling, device_id_type=MESH)
        cp.start()
        cp.wait()

    return pl.pallas_call(
        body, name=name, in_specs=[_HBM], out_specs=_HBM,
        out_shape=jax.ShapeDtypeStruct((4, hl) + G.shape[2:], G.dtype),
        scratch_shapes=[pltpu.SemaphoreType.DMA, pltpu.SemaphoreType.DMA],
    )(G)


def scatter_to_chips(A, *, name):
    def body(a_ref, out_ref, send_sems, recv_sems):
        x, y, c, s0, peers, sibling = _place()
        copies = [pltpu.make_async_remote_copy(src_ref=a_ref.at[s0 ^ PEER_XOR[k]], dst_ref=out_ref.at[k],
                                               send_sem=send_sems.at[k], recv_sem=recv_sems.at[k],
                                               device_id=peers[k], device_id_type=MESH) for k in range(3)]
        for cp in copies:
            cp.start()
        for cp in copies:
            cp.wait()

    return pl.pallas_call(
        body, name=name, in_specs=[_HBM], out_specs=_HBM,
        out_shape=jax.ShapeDtypeStruct((3,) + A.shape[1:], A.dtype),
        scratch_shapes=[pltpu.SemaphoreType.DMA((3,)), pltpu.SemaphoreType.DMA((3,))],
    )(A)


def join_halves(buf, a, hl, *, name):
    def body(b_ref, out_ref, send_sem, recv_sem):
        x, y, c, s0, peers, sibling = _place()
        mine, other = pl.ds(a + c * hl, hl), pl.ds(a + (1 - c) * hl, hl)
        cp = pltpu.make_async_remote_copy(src_ref=b_ref.at[mine], dst_ref=out_ref.at[mine],
                                          send_sem=send_sem, recv_sem=recv_sem, device_id=sibling, device_id_type=MESH)
        cp.start()
        pltpu.make_async_remote_copy(src_ref=b_ref.at[other], dst_ref=out_ref.at[other],
                                     send_sem=send_sem, recv_sem=recv_sem, device_id=sibling, device_id_type=MESH).wait_recv()
        cp.wait_send()

    return pl.pallas_call(
        body, name=name, in_specs=[_HBM], out_specs=_HBM, input_output_aliases={0: 0},
        out_shape=jax.ShapeDtypeStruct(buf.shape, buf.dtype),
        scratch_shapes=[pltpu.SemaphoreType.DMA, pltpu.SemaphoreType.DMA],
    )(buf)


def allgather8(xs, *, name, reduce):
    M, N = xs.shape

    def body(x_ref, out_ref, *rest):
        if reduce:
            all_ref, send_sems, recv_sems, local_sem = rest
        else:
            all_ref = out_ref
            send_sems, recv_sems, local_sem = rest
        x, y, c, s0, peers, sibling = _place()
        me = (x, y, c)
        chips = [(1 - x, y), (x, 1 - y), (1 - x, 1 - y)]

        def rows(px, py, pc):
            return all_ref.at[pl.ds((4 * px + 2 * py + pc) * M, M), :]

        def copy(k, block, to, src=None):
            return pltpu.make_async_remote_copy(src_ref=rows(*block) if src is None else src, dst_ref=rows(*block),
                                                send_sem=send_sems.at[k], recv_sem=recv_sems.at[k],
                                                device_id=to, device_id_type=MESH)

        mine = pltpu.make_async_copy(x_ref, rows(*me), local_sem)
        mine.start()
        first = [copy(0, me, sibling, src=x_ref)]
        first += [copy(1 + j, me, (*chip, c), src=x_ref) for j, chip in enumerate(chips)]
        for cp in first:
            cp.start()
        passed = [copy(4 + j, (*chip, c), sibling) for j, chip in enumerate(chips)]
        for j, chip in enumerate(chips):
            copy(1 + j, (*chip, c), me).wait_recv()
            passed[j].start()
        copy(0, sibling, me).wait_recv()
        for j, chip in enumerate(chips):
            copy(4 + j, (*chip, 1 - c), me).wait_recv()
        for cp in first + passed:
            cp.wait_send()
        mine.wait()
        if reduce:
            acc = all_ref[pl.ds(0, M), :]
            for d in range(1, 8):
                acc = acc + all_ref[pl.ds(d * M, M), :]
            out_ref[...] = acc

    vm = pl.BlockSpec(memory_space=pltpu.VMEM)
    scratch = [pltpu.SemaphoreType.DMA((7,)), pltpu.SemaphoreType.DMA((7,)), pltpu.SemaphoreType.DMA]
    if reduce:
        scratch = [pltpu.VMEM((8 * M, N), xs.dtype)] + scratch
    return pl.pallas_call(
        body, name=name, in_specs=[vm], out_specs=vm,
        out_shape=jax.ShapeDtypeStruct((M if reduce else 8 * M, N), xs.dtype), scratch_shapes=scratch,
    )(xs)


def add_halves(G, B, c, *, name):
    S, nl, R, C = G.shape
    hl = nl // 2
    tr = _pick(R, (512, 384, 256, 128))

    def body(c_ref, g_ref, b_ref, o_ref):
        o_ref[...] = (g_ref[...] + b_ref[...]).astype(BF16)

    blk = (1, 1, tr, C)
    return pl.pallas_call(
        body, name=name,
        grid_spec=pltpu.PrefetchScalarGridSpec(
            num_scalar_prefetch=1, grid=(S, hl, R // tr),
            in_specs=[pl.BlockSpec(blk, lambda s, l, r, cr: (s, cr[0] * hl + l, r, 0)),
                      pl.BlockSpec(blk, lambda s, l, r, cr: (s, l, r, 0))],
            out_specs=pl.BlockSpec(blk, lambda s, l, r, cr: (s, l, r, 0))),
        out_shape=jax.ShapeDtypeStruct((S, hl, R, C), BF16),
    )(jnp.reshape(c, (1,)).astype(jnp.int32), G, B)


def add_chips(G, B, Bc, c, s0, acc, a, *, name):
    S, nl, R, C = G.shape
    hl = nl // 2
    tr = _pick(R, (512, 384, 256, 128))

    def body(pc_ref, ps_ref, g_ref, b_ref, c0_ref, c1_ref, c2_ref, acc_ref, o_ref):
        o_ref[...] = ((((g_ref[0] + b_ref[0]) + c0_ref[0].astype(F32)) + c1_ref[0].astype(F32)) + c2_ref[0].astype(F32))

    blk = (1, 1, tr, C)
    cspec = lambda k: pl.BlockSpec(blk, lambda l, r, pc, ps: (k, l, r, 0))
    return pl.pallas_call(
        body, name=name,
        grid_spec=pltpu.PrefetchScalarGridSpec(
            num_scalar_prefetch=2, grid=(hl, R // tr),
            in_specs=[pl.BlockSpec(blk, lambda l, r, pc, ps: (ps[0], pc[0] * hl + l, r, 0)),
                      pl.BlockSpec(blk, lambda l, r, pc, ps: (ps[0], l, r, 0)), cspec(0), cspec(1), cspec(2),
                      pl.BlockSpec(memory_space=pl.ANY)],
            out_specs=pl.BlockSpec((1, tr, C), lambda l, r, pc, ps: (a + pc[0] * hl + l, r, 0))),
        out_shape=jax.ShapeDtypeStruct(acc.shape, F32), input_output_aliases={7: 0},
    )(jnp.reshape(c, (1,)).astype(jnp.int32), jnp.reshape(s0, (1,)).astype(jnp.int32), G, B, Bc, Bc, Bc, acc)


def reduce_scatter_finish(G, B, Bc, c, s0, acc, a, *, tag):
    acc = add_chips(G, B, Bc, c, s0, acc, a, name=f"rs_add2_{tag}")
    return join_halves(acc, a, G.shape[1] // 2, name=f"rs_join_{tag}")


def reduce_scatter_weight(G, c, s0, acc, a, *, tag):
    B = send_half_to_sibling(G, name=f"rs_sib_{tag}")
    A = add_halves(G, B, c, name=f"rs_add1_{tag}")
    Bc = scatter_to_chips(A, name=f"rs_chips_{tag}")
    return reduce_scatter_finish(G, B, Bc, c, s0, acc, a, tag=tag)


def adamw(w, g, m, v, *, name):
    shp = w.shape
    if len(shp) == 2:
        w, g, m, v = (a[None] for a in (w, g, m, v))
    nl, R, C = w.shape
    tr = R
    for t in (512, 384, 256, 128):
        if R % t == 0:
            tr = t
            break

    def body(w_ref, g_ref, m_ref, v_ref, d_ref, nm_ref, nv_ref):
        gv = g_ref[...]
        mn = ADAM_B1 * m_ref[...] + (1.0 - ADAM_B1) * gv
        vn = ADAM_B2 * v_ref[...] + (1.0 - ADAM_B2) * jnp.square(gv)
        m_hat = mn / (1.0 - ADAM_B1 ** ADAM_STEP)
        v_hat = vn / (1.0 - ADAM_B2 ** ADAM_STEP)
        d_ref[...] = -ADAM_LR * (m_hat / (jnp.sqrt(v_hat) + ADAM_EPS) + ADAM_WD * w_ref[...])
        nm_ref[...] = mn
        nv_ref[...] = vn

    spec = pl.BlockSpec((1, tr, C), lambda l, i: (l, i, 0))
    sds = jax.ShapeDtypeStruct((nl, R, C), F32)
    d, nm, nv = pl.pallas_call(body, name=name, grid=(nl, R // tr), in_specs=[spec] * 4, out_specs=(spec,) * 3,
                               out_shape=(sds,) * 3)(w, g, m, v)
    return d.reshape(shp), nm.reshape(shp), nv.reshape(shp)


_SMALL = ("ln_emb_g", "ln_emb_b", "q_norm", "kv_norm", "ln1_g", "ln1_b", "ln2_g", "ln2_b", "meta")


def _pack_small(d):
    flat = jnp.concatenate([d[k].reshape(-1) for k in _SMALL])
    rows = -(-flat.shape[0] // 128)
    rows = -(-rows // 8) * 8
    flat = jnp.concatenate([flat, jnp.zeros((rows * 128 - flat.shape[0],), F32)])
    return flat.reshape(rows, 128)


def _unpack_small(p, shapes):
    flat = p.reshape(-1)
    out, o = {}, 0
    for k in _SMALL:
        n = int(np.prod(shapes[k]))
        out[k] = flat[o:o + n].reshape(shapes[k])
        o += n
    return out


def kernel(x, meta_tokens, ln_emb_g, ln_emb_b, w_in, mla_q_norm, mla_kv_norm, w_uq, w_ukv, w_out, ln1_g, ln1_b, w_ff1, w_ff2, ln2_g, ln2_b, loss_target, m_meta_tokens, m_ln_emb_g, m_ln_emb_b, m_w_in, m_mla_q_norm, m_mla_kv_norm, m_w_uq, m_w_ukv, m_w_out, m_ln1_g, m_ln1_b, m_w_ff1, m_w_ff2, m_ln2_g, m_ln2_b, v_meta_tokens, v_ln_emb_g, v_ln_emb_b, v_w_in, v_mla_q_norm, v_mla_kv_norm, v_w_uq, v_w_ukv, v_w_out, v_ln1_g, v_ln1_b, v_w_ff1, v_w_ff2, v_ln2_g, v_ln2_b):
    xi, yi, ci = lax.axis_index("x"), lax.axis_index("y"), lax.axis_index("c")
    s0 = 2 * xi + yi
    nl = w_in.shape[0]

    big = {"w_in": w_in, "w_uq": w_uq, "w_ukv": w_ukv}
    late = {"w_out": w_out.astype(BF16), "w_ff1": w_ff1.astype(BF16), "w_ff2": w_ff2.astype(BF16), "s0": s0, "c": ci}
    full = {}
    los = {}
    for k, v in big.items():
        vb = v.astype(BF16)
        los[k], late[k + "_hi"] = vb[:nl // 2], vb[nl // 2:]
    for k, g in zip(los, gather_weights(tuple(los.values()), name="ag_lower")):
        full[k] = lax.dynamic_update_slice(g, los[k][None], (s0, 0, 0, 0))
    k_w_in = _perm_w_in(_cols(full["w_in"]))[:, None]
    k_w_uq = _perm_w_uq(_cols(full["w_uq"]))[:, None]
    k_w_ukv = _perm_w_ukv(_cols(full["w_ukv"]))[:, None]
    meta_all = allgather8(meta_tokens, name="ag_meta", reduce=False)
    meta_full = jnp.concatenate([meta_all[32 * s:32 * s + N_META] for s in range(4)], axis=1)

    loss_part, grad_x, g = _device_step(x[0], loss_target[0], meta_full, ln_emb_g, ln_emb_b, k_w_in, mla_q_norm, mla_kv_norm,
                                        k_w_uq, k_w_ukv, None, ln1_g, ln1_b, None, None, ln2_g, ln2_b, late=late)
    loss = lax.psum(loss_part, ("x", "y", "c"))

    gw = {k: g[k] for k in _RS_SHAPES}

    small_shapes = {"ln_emb_g": (D_MODEL,), "ln_emb_b": (D_MODEL,), "q_norm": (nl, MLA_Q_LORA), "kv_norm": (nl, MLA_KV_LORA),
                    "ln1_g": (nl, D_MODEL), "ln1_b": (nl, D_MODEL), "ln2_g": (nl, D_MODEL), "ln2_b": (nl, D_MODEL),
                    "meta": (N_META, D_MODEL)}
    gs = _unpack_small(allgather8(_pack_small(g), name="ar_small", reduce=True), small_shapes)
    gw.update({"ln_emb_g": gs["ln_emb_g"], "ln_emb_b": gs["ln_emb_b"], "mla_q_norm": gs["q_norm"], "mla_kv_norm": gs["kv_norm"],
               "ln1_g": gs["ln1_g"], "ln1_b": gs["ln1_b"], "ln2_g": gs["ln2_g"], "ln2_b": gs["ln2_b"],
               "meta_tokens": lax.dynamic_slice_in_dim(gs["meta"], s0 * 256, 256, axis=1)})

    names = ["meta_tokens", "ln_emb_g", "ln_emb_b", "w_in", "mla_q_norm", "mla_kv_norm", "w_uq", "w_ukv", "w_out",
             "ln1_g", "ln1_b", "w_ff1", "w_ff2", "ln2_g", "ln2_b"]
    ws = [meta_tokens, ln_emb_g, ln_emb_b, w_in, mla_q_norm, mla_kv_norm, w_uq, w_ukv, w_out, ln1_g, ln1_b, w_ff1, w_ff2, ln2_g, ln2_b]
    ms = [m_meta_tokens, m_ln_emb_g, m_ln_emb_b, m_w_in, m_mla_q_norm, m_mla_kv_norm, m_w_uq, m_w_ukv, m_w_out, m_ln1_g, m_ln1_b, m_w_ff1, m_w_ff2, m_ln2_g, m_ln2_b]
    vs = [v_meta_tokens, v_ln_emb_g, v_ln_emb_b, v_w_in, v_mla_q_norm, v_mla_kv_norm, v_w_uq, v_w_ukv, v_w_out, v_ln1_g, v_ln1_b, v_w_ff1, v_w_ff2, v_ln2_g, v_ln2_b]
    deltas, new_m, new_v = [], [], []
    for n, w, m, v in zip(names, ws, ms, vs):
        w2 = w.reshape(1, -1) if w.ndim == 1 else w
        d, nm, nv = adamw(w2, gw[n].reshape(w2.shape), m.reshape(w2.shape), v.reshape(w2.shape), name=f"adamw_{n}")
        deltas.append(d.reshape(w.shape))
        new_m.append(nm.reshape(w.shape))
        new_v.append(nv.reshape(w.shape))
    grads_out = [gw[n].reshape(w.shape) for n, w in zip(names, ws)]
    return (loss, grad_x[None], *grads_out, *deltas, *new_m, *new_v)
```
